```python
import math
import jax, jax.numpy as jnp
from jax import lax
import numpy as np

D_MODEL = 1024
BATCH = 8
SEQ = 4096
DEPTH = 2

CHUNK = 64
Q_BLOCK = 128
D_MIX = D_MODEL
SSD_WIDTH = D_MIX // 2
SSD_HEAD_DIM = 64
SSD_HEADS = SSD_WIDTH // SSD_HEAD_DIM
SSD_GROUPS = 2
SSD_HPG = SSD_HEADS // SSD_GROUPS
SSD_STATE = 128
SSD_CONV = 4
SSD_CONV_DIM = SSD_WIDTH + 2 * SSD_GROUPS * SSD_STATE
SSD_IN = SSD_WIDTH + SSD_CONV_DIM + SSD_HEADS
FOX_WIDTH = D_MIX // 4
FOX_HEAD_DIM = 64
FOX_HEADS = FOX_WIDTH // FOX_HEAD_DIM
FOX_IN = 3 * FOX_WIDTH + FOX_HEADS
SCONV_WIDTH = D_MIX - SSD_WIDTH - FOX_WIDTH
SCONV_K = 3
SCONV_IN = 3 * SCONV_WIDTH
D_IN_PROJ = SSD_IN + FOX_IN + SCONV_IN
D_FF = 2816
ALPHA = (2 * DEPTH) ** 0.25
BETA = (8 * DEPTH) ** -0.25
LN_EPS = 1e-5
RMS_EPS = 1e-5
N_SUB = 3

kernel_name = "hybrid_ssd_fox_shortconv_macaron_deepnorm_adaln"


def layer_norm(x, g, b):
    xf = x.astype(jnp.float32)
    mu = jnp.mean(xf, axis=-1, keepdims=True)
    var = jnp.mean(jnp.square(xf - mu), axis=-1, keepdims=True)
    return ((xf - mu) * lax.rsqrt(var + LN_EPS) * g + b).astype(x.dtype)


def causal_depthwise_conv(x, w, b=None):
    k_w, ch = w.shape
    y = lax.conv_general_dilated(
        x, w[:, None, :], window_strides=(1,), padding=[(k_w - 1, 0)],
        dimension_numbers=("NWC", "WIO", "NWC"), feature_group_count=ch)
    return y if b is None else y + b


def swiglu_ffn(h, w_in, w_out):
    gate, up = jnp.split(h @ w_in, 2, axis=-1)
    return (jax.nn.silu(gate) * up) @ w_out


def segsum(a):
    t = a.shape[-1]
    cs = jnp.cumsum(a, axis=-1)
    diff = cs[..., :, None] - cs[..., None, :]
    mask = jnp.tril(jnp.ones((t, t), dtype=bool))
    return jnp.where(mask, diff, -jnp.inf)


def ssd_chunked_scan(xdt, a, bm, cm):
    b, seq, g, e, p = xdt.shape
    n = bm.shape[-1]
    nc = seq // CHUNK
    xdt = xdt.reshape(b, nc, CHUNK, g, e, p)
    bm = bm.reshape(b, nc, CHUNK, g, n)
    cm = cm.reshape(b, nc, CHUNK, g, n)
    a = jnp.transpose(a.reshape(b, nc, CHUNK, g, e), (0, 3, 4, 1, 2))
    a_cs = jnp.cumsum(a, axis=-1)
    l_mat = jnp.exp(segsum(a))
    cb = jnp.einsum("bclgn,bcsgn->bgcls", cm, bm)
    y_diag = jnp.einsum("bgcls,bgecls,bcsgep->bclgep", cb, l_mat, xdt)
    decay_states = jnp.exp(a_cs[..., -1:] - a_cs)
    states = jnp.einsum("bclgn,bgecl,bclgep->bcgepn", bm, decay_states, xdt)
    chunk_a = jnp.pad(a_cs[..., -1], ((0, 0), (0, 0), (0, 0), (1, 0)))
    decay_chunk = jnp.exp(segsum(chunk_a))
    states = jnp.concatenate([jnp.zeros_like(states[:, :1]), states], axis=1)
    states_in = jnp.einsum("bgezc,bcgepn->bzgepn", decay_chunk, states)[:, :-1]
    y_off = jnp.einsum("bclgn,bcgepn,bgecl->bclgep", cm, states_in, jnp.exp(a_cs))
    return (y_diag + y_off).reshape(b, seq, g, e, p)


def mamba2_group(proj, conv_w, conv_b, dt_bias, a_log, d_skip, norm_g):
    b, seq, _ = proj.shape
    z, xbc, dt_raw = jnp.split(proj, [SSD_WIDTH, SSD_WIDTH + SSD_CONV_DIM], axis=-1)
    xbc = jax.nn.silu(causal_depthwise_conv(xbc, conv_w, conv_b))
    xs, bm, cm = jnp.split(xbc, [SSD_WIDTH, SSD_WIDTH + SSD_GROUPS * SSD_STATE], axis=-1)
    xs = xs.astype(jnp.float32).reshape(b, seq, SSD_GROUPS, SSD_HPG, SSD_HEAD_DIM)
    bm = bm.astype(jnp.float32).reshape(b, seq, SSD_GROUPS, SSD_STATE)
    cm = cm.astype(jnp.float32).reshape(b, seq, SSD_GROUPS, SSD_STATE)
    dt = jax.nn.softplus(dt_raw.astype(jnp.float32) + dt_bias.astype(jnp.float32))
    dt = dt.reshape(b, seq, SSD_GROUPS, SSD_HPG)
    a_head = -jnp.exp(a_log.astype(jnp.float32)).reshape(SSD_GROUPS, SSD_HPG)
    y = ssd_chunked_scan(xs * dt[..., None], dt * a_head, bm, cm)
    y = y + d_skip.astype(jnp.float32).reshape(SSD_GROUPS, SSD_HPG)[:, :, None] * xs
    y = y.reshape(b, seq, SSD_WIDTH) * jax.nn.silu(z.astype(jnp.float32))
    yg = y.reshape(b, seq, SSD_GROUPS, SSD_WIDTH // SSD_GROUPS)
    yg = yg * lax.rsqrt(jnp.mean(jnp.square(yg), axis=-1, keepdims=True) + RMS_EPS)
    return (yg.reshape(b, seq, SSD_WIDTH) * norm_g).astype(proj.dtype)


def fox_group(proj, f_bias):
    b, seq, _ = proj.shape
    q, k, v, f = jnp.split(proj, [FOX_WIDTH, 2 * FOX_WIDTH, 3 * FOX_WIDTH], axis=-1)
    q = q.reshape(b, seq, FOX_HEADS, FOX_HEAD_DIM)
    k = k.reshape(b, seq, FOX_HEADS, FOX_HEAD_DIM)
    v = v.reshape(b, seq, FOX_HEADS, FOX_HEAD_DIM)
    log_f = jax.nn.log_sigmoid(f.astype(jnp.float32) + f_bias.astype(jnp.float32))
    cum_f = jnp.transpose(jnp.cumsum(log_f, axis=1), (0, 2, 1))
    scale = FOX_HEAD_DIM ** -0.5
    outs = []
    for i in range(seq // Q_BLOCK):
        q0, end = i * Q_BLOCK, (i + 1) * Q_BLOCK
        s = jnp.einsum("bqhd,bkhd->bhqk", q[:, q0:end], k[:, :end]).astype(jnp.float32) * scale
        s = s + cum_f[:, :, q0:end, None] - cum_f[:, :, None, :end]
        mask = jnp.arange(q0, end)[:, None] >= jnp.arange(end)[None, :]
        prob = jax.nn.softmax(jnp.where(mask, s, -jnp.inf), axis=-1)
        outs.append(jnp.einsum("bhqk,bkhd->bqhd", prob.astype(v.dtype), v[:, :end]))
    return jnp.concatenate(outs, axis=1).reshape(b, seq, FOX_WIDTH)


def shortconv_group(proj, conv_w):
    bg, cg, xin = jnp.split(proj, 3, axis=-1)
    return bg * causal_depthwise_conv(cg * xin, conv_w)


def modulate(x, shift, scale):
    return x * (1.0 + scale) + shift


def _fwd_setup_inputs(seed: int = 0) -> dict:
    key = jax.random.key(seed)
    ks = jax.random.split(key, 24)

    def nrm(k, shape, s):
        return jax.random.normal(k, shape, jnp.float32) * s

    dt0 = jnp.exp(jax.random.uniform(ks[12], (DEPTH, SSD_HEADS), jnp.float32,
                                     minval=math.log(1e-3), maxval=math.log(1e-1)))
    return {
        "x": nrm(ks[0], (BATCH, SEQ, D_MODEL), 1.0),
        "c": nrm(ks[1], (BATCH, D_MODEL), 1.0),
        "ln_in_g": 1.0 + nrm(ks[2], (D_MODEL,), 0.01),
        "ln_in_b": nrm(ks[3], (D_MODEL,), 0.01),
        "ada_w": nrm(ks[4], (DEPTH, D_MODEL, N_SUB * 3 * D_MODEL), 0.5 * D_MODEL ** -0.5),
        "ada_b": nrm(ks[5], (DEPTH, N_SUB * 3 * D_MODEL), 0.01),
        "ffn1_w_in": nrm(ks[6], (DEPTH, D_MODEL, 2 * D_FF), D_MODEL ** -0.5),
        "ffn1_w_out": nrm(ks[7], (DEPTH, D_FF, D_MODEL), BETA * D_FF ** -0.5),
        "mix_w_in": nrm(ks[8], (DEPTH, D_MODEL, D_IN_PROJ), D_MODEL ** -0.5),
        "mix_w_out": nrm(ks[9], (DEPTH, D_MIX, D_MODEL), BETA * D_MIX ** -0.5),
        "ssd_conv_w": nrm(ks[10], (DEPTH, SSD_CONV, SSD_CONV_DIM), SSD_CONV ** -0.5),
        "ssd_conv_b": nrm(ks[11], (DEPTH, SSD_CONV_DIM), 0.01),
        "ssd_dt_bias": dt0 + jnp.log(-jnp.expm1(-dt0)),
        "ssd_a_log": jnp.log(jax.random.uniform(ks[13], (DEPTH, SSD_HEADS), jnp.float32,
                                                 minval=1.0, maxval=16.0)),
        "ssd_d": 1.0 + nrm(ks[14], (DEPTH, SSD_HEADS), 0.01),
        "ssd_norm_g": 1.0 + nrm(ks[15], (DEPTH, SSD_WIDTH), 0.01),
        "fox_f_bias": jax.random.uniform(ks[16], (DEPTH, FOX_HEADS), jnp.float32,
                                         minval=1.0, maxval=5.0),
        "sconv_w": nrm(ks[17], (DEPTH, SCONV_K, SCONV_WIDTH), SCONV_K ** -0.5),
        "ffn2_w_in": nrm(ks[18], (DEPTH, D_MODEL, 2 * D_FF), D_MODEL ** -0.5),
        "ffn2_w_out": nrm(ks[19], (DEPTH, D_FF, D_MODEL), BETA * D_FF ** -0.5),
        "ln_g": 1.0 + nrm(ks[20], (DEPTH, N_SUB, D_MODEL), 0.01),
        "ln_b": nrm(ks[21], (DEPTH, N_SUB, D_MODEL), 0.01),
    }


def _fwd_reference(x, c, ln_in_g, ln_in_b, ada_w, ada_b, ffn1_w_in, ffn1_w_out,
              mix_w_in, mix_w_out, ssd_conv_w, ssd_conv_b, ssd_dt_bias, ssd_a_log,
              ssd_d, ssd_norm_g, fox_f_bias, sconv_w, ffn2_w_in, ffn2_w_out,
              ln_g, ln_b):
    b = x.shape[0]
    x = layer_norm(x, ln_in_g, ln_in_b)
    c_act = jax.nn.silu(c)
    for l in range(DEPTH):
        mod = (c_act @ ada_w[l] + ada_b[l]).reshape(b, N_SUB, 3, 1, D_MODEL)

        h = modulate(x, mod[:, 0, 0], mod[:, 0, 1])
        y = swiglu_ffn(h, ffn1_w_in[l], ffn1_w_out[l])
        x = layer_norm(ALPHA * x + 0.5 * mod[:, 0, 2] * y, ln_g[l, 0], ln_b[l, 0])

        h = modulate(x, mod[:, 1, 0], mod[:, 1, 1])
        proj = h @ mix_w_in[l]
        p_ssd, p_fox, p_sc = jnp.split(proj, [SSD_IN, SSD_IN + FOX_IN], axis=-1)
        y_ssd = mamba2_group(p_ssd, ssd_conv_w[l], ssd_conv_b[l], ssd_dt_bias[l],
                             ssd_a_log[l], ssd_d[l], ssd_norm_g[l])
        y_fox = fox_group(p_fox, fox_f_bias[l])
        y_sc = shortconv_group(p_sc, sconv_w[l])
        y = jnp.concatenate([y_ssd, y_fox, y_sc], axis=-1) @ mix_w_out[l]
        x = layer_norm(ALPHA * x + mod[:, 1, 2] * y, ln_g[l, 1], ln_b[l, 1])

        h = modulate(x, mod[:, 2, 0], mod[:, 2, 1])
        y = swiglu_ffn(h, ffn2_w_in[l], ffn2_w_out[l])
        x = layer_norm(ALPHA * x + 0.5 * mod[:, 2, 2] * y, ln_g[l, 2], ln_b[l, 2])
    return x


import jax as _jax
import jax.numpy as _jnp

TWIN_FORMAT = 'train_step'
FWD_PARAMS = ['x', 'c', 'ln_in_g', 'ln_in_b', 'ada_w', 'ada_b', 'ffn1_w_in', 'ffn1_w_out', 'mix_w_in', 'mix_w_out', 'ssd_conv_w', 'ssd_conv_b', 'ssd_dt_bias', 'ssd_a_log', 'ssd_d', 'ssd_norm_g', 'fox_f_bias', 'sconv_w', 'ffn2_w_in', 'ffn2_w_out', 'ln_g', 'ln_b']
TWIN_WEIGHTS = ['ln_in_g', 'ln_in_b', 'ada_w', 'ada_b', 'ffn1_w_in', 'ffn1_w_out', 'mix_w_in', 'mix_w_out', 'ssd_conv_w', 'ssd_conv_b', 'ssd_dt_bias', 'ssd_a_log', 'ssd_d', 'ssd_norm_g', 'fox_f_bias', 'sconv_w', 'ffn2_w_in', 'ffn2_w_out', 'ln_g', 'ln_b']
TWIN_DIFF_INPUT = 'x'
TWIN_INPUTS = ['x', 'c', 'ln_in_g', 'ln_in_b', 'ada_w', 'ada_b', 'ffn1_w_in', 'ffn1_w_out', 'mix_w_in', 'mix_w_out', 'ssd_conv_w', 'ssd_conv_b', 'ssd_dt_bias', 'ssd_a_log', 'ssd_d', 'ssd_norm_g', 'fox_f_bias', 'sconv_w', 'ffn2_w_in', 'ffn2_w_out', 'ln_g', 'ln_b', 'loss_target', 'm_ln_in_g', 'm_ln_in_b', 'm_ada_w', 'm_ada_b', 'm_ffn1_w_in', 'm_ffn1_w_out', 'm_mix_w_in', 'm_mix_w_out', 'm_ssd_conv_w', 'm_ssd_conv_b', 'm_ssd_dt_bias', 'm_ssd_a_log', 'm_ssd_d', 'm_ssd_norm_g', 'm_fox_f_bias', 'm_sconv_w', 'm_ffn2_w_in', 'm_ffn2_w_out', 'm_ln_g', 'm_ln_b', 'v_ln_in_g', 'v_ln_in_b', 'v_ada_w', 'v_ada_b', 'v_ffn1_w_in', 'v_ffn1_w_out', 'v_mix_w_in', 'v_mix_w_out', 'v_ssd_conv_w', 'v_ssd_conv_b', 'v_ssd_dt_bias', 'v_ssd_a_log', 'v_ssd_d', 'v_ssd_norm_g', 'v_fox_f_bias', 'v_sconv_w', 'v_ffn2_w_in', 'v_ffn2_w_out', 'v_ln_g', 'v_ln_b']
TWIN_OUTPUTS = ['loss', 'grad_x', 'grad_ln_in_g', 'grad_ln_in_b', 'grad_ada_w', 'grad_ada_b', 'grad_ffn1_w_in', 'grad_ffn1_w_out', 'grad_mix_w_in', 'grad_mix_w_out', 'grad_ssd_conv_w', 'grad_ssd_conv_b', 'grad_ssd_dt_bias', 'grad_ssd_a_log', 'grad_ssd_d', 'grad_ssd_norm_g', 'grad_fox_f_bias', 'grad_sconv_w', 'grad_ffn2_w_in', 'grad_ffn2_w_out', 'grad_ln_g', 'grad_ln_b', 'delta_ln_in_g', 'delta_ln_in_b', 'delta_ada_w', 'delta_ada_b', 'delta_ffn1_w_in', 'delta_ffn1_w_out', 'delta_mix_w_in', 'delta_mix_w_out', 'delta_ssd_conv_w', 'delta_ssd_conv_b', 'delta_ssd_dt_bias', 'delta_ssd_a_log', 'delta_ssd_d', 'delta_ssd_norm_g', 'delta_fox_f_bias', 'delta_sconv_w', 'delta_ffn2_w_in', 'delta_ffn2_w_out', 'delta_ln_g', 'delta_ln_b', 'new_m_ln_in_g', 'new_m_ln_in_b', 'new_m_ada_w', 'new_m_ada_b', 'new_m_ffn1_w_in', 'new_m_ffn1_w_out', 'new_m_mix_w_in', 'new_m_mix_w_out', 'new_m_ssd_conv_w', 'new_m_ssd_conv_b', 'new_m_ssd_dt_bias', 'new_m_ssd_a_log', 'new_m_ssd_d', 'new_m_ssd_norm_g', 'new_m_fox_f_bias', 'new_m_sconv_w', 'new_m_ffn2_w_in', 'new_m_ffn2_w_out', 'new_m_ln_g', 'new_m_ln_b', 'new_v_ln_in_g', 'new_v_ln_in_b', 'new_v_ada_w', 'new_v_ada_b', 'new_v_ffn1_w_in', 'new_v_ffn1_w_out', 'new_v_mix_w_in', 'new_v_mix_w_out', 'new_v_ssd_conv_w', 'new_v_ssd_conv_b', 'new_v_ssd_dt_bias', 'new_v_ssd_a_log', 'new_v_ssd_d', 'new_v_ssd_norm_g', 'new_v_fox_f_bias', 'new_v_sconv_w', 'new_v_ffn2_w_in', 'new_v_ffn2_w_out', 'new_v_ln_g', 'new_v_ln_b']
TWIN_LEAF_KINDS = {'loss': 'loss', 'grad_x': 'grad_x', 'grad_ln_in_g': 'grad_w', 'grad_ln_in_b': 'grad_w', 'grad_ada_w': 'grad_w', 'grad_ada_b': 'grad_w', 'grad_ffn1_w_in': 'grad_w', 'grad_ffn1_w_out': 'grad_w', 'grad_mix_w_in': 'grad_w', 'grad_mix_w_out': 'grad_w', 'grad_ssd_conv_w': 'grad_w', 'grad_ssd_conv_b': 'grad_w', 'grad_ssd_dt_bias': 'grad_w', 'grad_ssd_a_log': 'grad_w', 'grad_ssd_d': 'grad_w', 'grad_ssd_norm_g': 'grad_w', 'grad_fox_f_bias': 'grad_w', 'grad_sconv_w': 'grad_w', 'grad_ffn2_w_in': 'grad_w', 'grad_ffn2_w_out': 'grad_w', 'grad_ln_g': 'grad_w', 'grad_ln_b': 'grad_w', 'delta_ln_in_g': 'delta_w', 'delta_ln_in_b': 'delta_w', 'delta_ada_w': 'delta_w', 'delta_ada_b': 'delta_w', 'delta_ffn1_w_in': 'delta_w', 'delta_ffn1_w_out': 'delta_w', 'delta_mix_w_in': 'delta_w', 'delta_mix_w_out': 'delta_w', 'delta_ssd_conv_w': 'delta_w', 'delta_ssd_conv_b': 'delta_w', 'delta_ssd_dt_bias': 'delta_w', 'delta_ssd_a_log': 'delta_w', 'delta_ssd_d': 'delta_w', 'delta_ssd_norm_g': 'delta_w', 'delta_fox_f_bias': 'delta_w', 'delta_sconv_w': 'delta_w', 'delta_ffn2_w_in': 'delta_w', 'delta_ffn2_w_out': 'delta_w', 'delta_ln_g': 'delta_w', 'delta_ln_b': 'delta_w', 'new_m_ln_in_g': 'new_m', 'new_m_ln_in_b': 'new_m', 'new_m_ada_w': 'new_m', 'new_m_ada_b': 'new_m', 'new_m_ffn1_w_in': 'new_m', 'new_m_ffn1_w_out': 'new_m', 'new_m_mix_w_in': 'new_m', 'new_m_mix_w_out': 'new_m', 'new_m_ssd_conv_w': 'new_m', 'new_m_ssd_conv_b': 'new_m', 'new_m_ssd_dt_bias': 'new_m', 'new_m_ssd_a_log': 'new_m', 'new_m_ssd_d': 'new_m', 'new_m_ssd_norm_g': 'new_m', 'new_m_fox_f_bias': 'new_m', 'new_m_sconv_w': 'new_m', 'new_m_ffn2_w_in': 'new_m', 'new_m_ffn2_w_out': 'new_m', 'new_m_ln_g': 'new_m', 'new_m_ln_b': 'new_m', 'new_v_ln_in_g': 'new_v', 'new_v_ln_in_b': 'new_v', 'new_v_ada_w': 'new_v', 'new_v_ada_b': 'new_v', 'new_v_ffn1_w_in': 'new_v', 'new_v_ffn1_w_out': 'new_v', 'new_v_mix_w_in': 'new_v', 'new_v_mix_w_out': 'new_v', 'new_v_ssd_conv_w': 'new_v', 'new_v_ssd_conv_b': 'new_v', 'new_v_ssd_dt_bias': 'new_v', 'new_v_ssd_a_log': 'new_v', 'new_v_ssd_d': 'new_v', 'new_v_ssd_norm_g': 'new_v', 'new_v_fox_f_bias': 'new_v', 'new_v_sconv_w': 'new_v', 'new_v_ffn2_w_in': 'new_v', 'new_v_ffn2_w_out': 'new_v', 'new_v_ln_g': 'new_v', 'new_v_ln_b': 'new_v'}


def _forward(args):
    return _fwd_reference(*[args[k] for k in FWD_PARAMS])


def _output_shape():
    def fwd():
        inp = _fwd_setup_inputs(0)
        return _fwd_reference(*[inp[k] for k in FWD_PARAMS])
    out = _jax.eval_shape(fwd)
    return out.shape, out.dtype

N_MICROBATCH = 1
ADAM_LR = 0.001
ADAM_B1 = 0.9
ADAM_B2 = 0.999
ADAM_EPS = 1e-08
ADAM_WD = 0.01
ADAM_STEP = 10
PER_EXAMPLE_BATCH_AXIS = {'x': 0, 'c': 0, 'loss_target': 0}
SHARED_INPUTS = []
_WEIGHT_DTYPES = {'ln_in_g': _jnp.float32, 'ln_in_b': _jnp.float32, 'ada_w': _jnp.float32, 'ada_b': _jnp.float32, 'ffn1_w_in': _jnp.float32, 'ffn1_w_out': _jnp.float32, 'mix_w_in': _jnp.float32, 'mix_w_out': _jnp.float32, 'ssd_conv_w': _jnp.float32, 'ssd_conv_b': _jnp.float32, 'ssd_dt_bias': _jnp.float32, 'ssd_a_log': _jnp.float32, 'ssd_d': _jnp.float32, 'ssd_norm_g': _jnp.float32, 'fox_f_bias': _jnp.float32, 'sconv_w': _jnp.float32, 'ffn2_w_in': _jnp.float32, 'ffn2_w_out': _jnp.float32, 'ln_g': _jnp.float32, 'ln_b': _jnp.float32}
MOMENT_SCALE = {'ln_in_g': 5.315630e-01, 'ln_in_b': 3.065104e-01, 'ada_w': 1.649643e-02, 'ada_b': 2.779431e-02, 'ffn1_w_in': 4.227035e-03, 'ffn1_w_out': 1.377850e-02, 'mix_w_in': 1.698943e-02, 'mix_w_out': 3.698984e-02, 'ssd_conv_w': 1.412467e-02, 'ssd_conv_b': 1.753622e-02, 'ssd_dt_bias': 3.668995e-02, 'ssd_a_log': 4.782889e-02, 'ssd_d': 1.485606e-01, 'ssd_norm_g': 1.860852e-02, 'fox_f_bias': 3.005786e-02, 'sconv_w': 2.433024e-02, 'ffn2_w_in': 4.132918e-03, 'ffn2_w_out': 1.350984e-02, 'ln_g': 1.308711e+01, 'ln_b': 4.490232e-01}


def _to_microbatches(a, axis):
    t = _jnp.moveaxis(a, axis, 0)
    t = t.reshape((N_MICROBATCH, t.shape[0] // N_MICROBATCH) + t.shape[1:])
    return _jnp.moveaxis(t, 1, axis + 1)


def setup_inputs(seed: int = 0) -> dict:
    inp = _fwd_setup_inputs(seed)
    key = _jax.random.fold_in(_jax.random.key(seed), 7919)
    shape, _ = _output_shape()
    out = dict(inp)
    out["loss_target"] = _jax.random.normal(_jax.random.fold_in(key, 0), shape, _jnp.float32)
    for i, name in enumerate(TWIN_WEIGHTS):
        w = inp[name].astype(_jnp.float32)
        if MOMENT_SCALE is None:
            s = _jnp.sqrt(_jnp.mean(_jnp.square(w)) + 1e-30)
        else:
            s = MOMENT_SCALE[name]
        km, kv = _jax.random.split(_jax.random.fold_in(key, i + 1))
        out[name] = w
        out["m_" + name] = s * _jax.random.normal(km, w.shape, _jnp.float32)
        out["v_" + name] = (s * s) * _jax.random.uniform(kv, w.shape, _jnp.float32, 0.5, 1.5)
    if N_MICROBATCH > 1:
        for name, axis in PER_EXAMPLE_BATCH_AXIS.items():
            out[name] = _to_microbatches(out[name], axis)
    return {'x': out['x'], 'c': out['c'], 'ln_in_g': out['ln_in_g'], 'ln_in_b': out['ln_in_b'], 'ada_w': out['ada_w'], 'ada_b': out['ada_b'], 'ffn1_w_in': out['ffn1_w_in'], 'ffn1_w_out': out['ffn1_w_out'], 'mix_w_in': out['mix_w_in'], 'mix_w_out': out['mix_w_out'], 'ssd_conv_w': out['ssd_conv_w'], 'ssd_conv_b': out['ssd_conv_b'], 'ssd_dt_bias': out['ssd_dt_bias'], 'ssd_a_log': out['ssd_a_log'], 'ssd_d': out['ssd_d'], 'ssd_norm_g': out['ssd_norm_g'], 'fox_f_bias': out['fox_f_bias'], 'sconv_w': out['sconv_w'], 'ffn2_w_in': out['ffn2_w_in'], 'ffn2_w_out': out['ffn2_w_out'], 'ln_g': out['ln_g'], 'ln_b': out['ln_b'], 'loss_target': out['loss_target'], 'm_ln_in_g': out['m_ln_in_g'], 'm_ln_in_b': out['m_ln_in_b'], 'm_ada_w': out['m_ada_w'], 'm_ada_b': out['m_ada_b'], 'm_ffn1_w_in': out['m_ffn1_w_in'], 'm_ffn1_w_out': out['m_ffn1_w_out'], 'm_mix_w_in': out['m_mix_w_in'], 'm_mix_w_out': out['m_mix_w_out'], 'm_ssd_conv_w': out['m_ssd_conv_w'], 'm_ssd_conv_b': out['m_ssd_conv_b'], 'm_ssd_dt_bias': out['m_ssd_dt_bias'], 'm_ssd_a_log': out['m_ssd_a_log'], 'm_ssd_d': out['m_ssd_d'], 'm_ssd_norm_g': out['m_ssd_norm_g'], 'm_fox_f_bias': out['m_fox_f_bias'], 'm_sconv_w': out['m_sconv_w'], 'm_ffn2_w_in': out['m_ffn2_w_in'], 'm_ffn2_w_out': out['m_ffn2_w_out'], 'm_ln_g': out['m_ln_g'], 'm_ln_b': out['m_ln_b'], 'v_ln_in_g': out['v_ln_in_g'], 'v_ln_in_b': out['v_ln_in_b'], 'v_ada_w': out['v_ada_w'], 'v_ada_b': out['v_ada_b'], 'v_ffn1_w_in': out['v_ffn1_w_in'], 'v_ffn1_w_out': out['v_ffn1_w_out'], 'v_mix_w_in': out['v_mix_w_in'], 'v_mix_w_out': out['v_mix_w_out'], 'v_ssd_conv_w': out['v_ssd_conv_w'], 'v_ssd_conv_b': out['v_ssd_conv_b'], 'v_ssd_dt_bias': out['v_ssd_dt_bias'], 'v_ssd_a_log': out['v_ssd_a_log'], 'v_ssd_d': out['v_ssd_d'], 'v_ssd_norm_g': out['v_ssd_norm_g'], 'v_fox_f_bias': out['v_fox_f_bias'], 'v_sconv_w': out['v_sconv_w'], 'v_ffn2_w_in': out['v_ffn2_w_in'], 'v_ffn2_w_out': out['v_ffn2_w_out'], 'v_ln_g': out['v_ln_g'], 'v_ln_b': out['v_ln_b']}


def _loss(weights, diff, rest, loss_target):
    with _jax.named_scope("forward"):
        args = {**rest, TWIN_DIFF_INPUT: diff, **{k: w.astype(_WEIGHT_DTYPES[k]) for k, w in weights.items()}}
        y = _forward(args)
    with _jax.named_scope("loss_head"):
        err = _jnp.square(y.astype(_jnp.float32) - loss_target)
        return 0.5 * _jnp.sum(_jnp.mean(err, axis=-1)) if err.ndim else 0.5 * err


def _adamw(w, g, m, v):
    m = ADAM_B1 * m + (1.0 - ADAM_B1) * g
    v = ADAM_B2 * v + (1.0 - ADAM_B2) * _jnp.square(g)
    m_hat = m / (1.0 - ADAM_B1 ** ADAM_STEP)
    v_hat = v / (1.0 - ADAM_B2 ** ADAM_STEP)
    delta = -ADAM_LR * (m_hat / (_jnp.sqrt(v_hat) + ADAM_EPS) + ADAM_WD * w)
    return delta, m, v


def reference(x, c, ln_in_g, ln_in_b, ada_w, ada_b, ffn1_w_in, ffn1_w_out, mix_w_in, mix_w_out, ssd_conv_w, ssd_conv_b, ssd_dt_bias, ssd_a_log, ssd_d, ssd_norm_g, fox_f_bias, sconv_w, ffn2_w_in, ffn2_w_out, ln_g, ln_b, loss_target, m_ln_in_g, m_ln_in_b, m_ada_w, m_ada_b, m_ffn1_w_in, m_ffn1_w_out, m_mix_w_in, m_mix_w_out, m_ssd_conv_w, m_ssd_conv_b, m_ssd_dt_bias, m_ssd_a_log, m_ssd_d, m_ssd_norm_g, m_fox_f_bias, m_sconv_w, m_ffn2_w_in, m_ffn2_w_out, m_ln_g, m_ln_b, v_ln_in_g, v_ln_in_b, v_ada_w, v_ada_b, v_ffn1_w_in, v_ffn1_w_out, v_mix_w_in, v_mix_w_out, v_ssd_conv_w, v_ssd_conv_b, v_ssd_dt_bias, v_ssd_a_log, v_ssd_d, v_ssd_norm_g, v_fox_f_bias, v_sconv_w, v_ffn2_w_in, v_ffn2_w_out, v_ln_g, v_ln_b):
    given = dict(x=x, c=c, ln_in_g=ln_in_g, ln_in_b=ln_in_b, ada_w=ada_w, ada_b=ada_b, ffn1_w_in=ffn1_w_in, ffn1_w_out=ffn1_w_out, mix_w_in=mix_w_in, mix_w_out=mix_w_out, ssd_conv_w=ssd_conv_w, ssd_conv_b=ssd_conv_b, ssd_dt_bias=ssd_dt_bias, ssd_a_log=ssd_a_log, ssd_d=ssd_d, ssd_norm_g=ssd_norm_g, fox_f_bias=fox_f_bias, sconv_w=sconv_w, ffn2_w_in=ffn2_w_in, ffn2_w_out=ffn2_w_out, ln_g=ln_g, ln_b=ln_b, loss_target=loss_target, m_ln_in_g=m_ln_in_g, m_ln_in_b=m_ln_in_b, m_ada_w=m_ada_w, m_ada_b=m_ada_b, m_ffn1_w_in=m_ffn1_w_in, m_ffn1_w_out=m_ffn1_w_out, m_mix_w_in=m_mix_w_in, m_mix_w_out=m_mix_w_out, m_ssd_conv_w=m_ssd_conv_w, m_ssd_conv_b=m_ssd_conv_b, m_ssd_dt_bias=m_ssd_dt_bias, m_ssd_a_log=m_ssd_a_log, m_ssd_d=m_ssd_d, m_ssd_norm_g=m_ssd_norm_g, m_fox_f_bias=m_fox_f_bias, m_sconv_w=m_sconv_w, m_ffn2_w_in=m_ffn2_w_in, m_ffn2_w_out=m_ffn2_w_out, m_ln_g=m_ln_g, m_ln_b=m_ln_b, v_ln_in_g=v_ln_in_g, v_ln_in_b=v_ln_in_b, v_ada_w=v_ada_w, v_ada_b=v_ada_b, v_ffn1_w_in=v_ffn1_w_in, v_ffn1_w_out=v_ffn1_w_out, v_mix_w_in=v_mix_w_in, v_mix_w_out=v_mix_w_out, v_ssd_conv_w=v_ssd_conv_w, v_ssd_conv_b=v_ssd_conv_b, v_ssd_dt_bias=v_ssd_dt_bias, v_ssd_a_log=v_ssd_a_log, v_ssd_d=v_ssd_d, v_ssd_norm_g=v_ssd_norm_g, v_fox_f_bias=v_fox_f_bias, v_sconv_w=v_sconv_w, v_ffn2_w_in=v_ffn2_w_in, v_ffn2_w_out=v_ffn2_w_out, v_ln_g=v_ln_g, v_ln_b=v_ln_b)
    weights = {n: given[n] for n in TWIN_WEIGHTS}
    shared = {n: given[n] for n in SHARED_INPUTS}
    per_example = {n: given[n] for n in ['x', 'c']}
    grad_fn = _jax.value_and_grad(_loss, argnums=(0, 1))

    def one_microbatch(ex, loss_target):
        ex = dict(ex)
        diff = ex.pop(TWIN_DIFF_INPUT)
        return grad_fn(weights, diff, {**shared, **ex}, loss_target)

    if N_MICROBATCH == 1:
        loss, (grad_w, grad_x) = one_microbatch(per_example, given["loss_target"])
    else:
        def body(carry, xs):
            loss_sum, grad_sum = carry
            l_k, (gw_k, gx_k) = one_microbatch(xs[0], xs[1])
            with _jax.named_scope("update"):
                return (loss_sum + l_k, _jax.tree.map(_jnp.add, grad_sum, gw_k)), gx_k

        init = (_jnp.zeros((), _jnp.float32), _jax.tree.map(_jnp.zeros_like, weights))
        (loss, grad_w), grad_x = _jax.lax.scan(body, init, (per_example, given["loss_target"]))
    with _jax.named_scope("update"):
        delta_w, new_m, new_v = {}, {}, {}
        for n in TWIN_WEIGHTS:
            delta_w[n], new_m[n], new_v[n] = _adamw(weights[n], grad_w[n], given["m_" + n], given["v_" + n])
    return (loss, grad_x, *[grad_w[n] for n in TWIN_WEIGHTS], *[delta_w[n] for n in TWIN_WEIGHTS],
            *[new_m[n] for n in TWIN_WEIGHTS], *[new_v[n] for n in TWIN_WEIGHTS])
```

```python
import functools

import jax
import jax.numpy as jnp
from jax import lax
from jax.experimental import pallas as pl
from jax.experimental.pallas import tpu as pltpu

F32 = jnp.float32
BF16 = jnp.bfloat16
MESH = pl.DeviceIdType.MESH

D = 1024
DEPTH = 2
N_SUB = 3
D_FF = 2816
ALPHA = (2 * DEPTH) ** 0.25
LN_EPS = 1e-5
RMS_EPS = 1e-5
SSD_W = 512
SSD_HEADS = 8
SSD_CONV_K = 4
SSD_CONV_DIM = 1024
FOX_W = 256
FOX_HEADS = 4
FOX_HD = 64
SC_W = 256
SC_K = 3
D_IN_PROJ = 3084
P_Z, P_XBC, P_QKV, P_SC, P_PAD = 0, 512, 1536, 2304, 3072
D_PROJ_PAD = 3200
PAD_DT0, PAD_F0 = 0, 8
SSD_CHUNK = 256
ATT_BLOCK = 256
LANES = 128
VMEM_LIMIT = 56 * 1024 * 1024

ADAM_LR, ADAM_B1, ADAM_B2, ADAM_EPS, ADAM_WD, ADAM_STEP = 0.001, 0.9, 0.999, 1e-08, 0.01, 10
NEG = -1e30


def _cp(*sem):
    return pltpu.CompilerParams(dimension_semantics=sem, vmem_limit_bytes=VMEM_LIMIT)


def _pick(n, cands):
    for c in cands:
        if n % c == 0:
            return c
    return n


def _dot(a, b):
    return lax.dot_general(a, b, (((1,), (0,)), ((), ())), preferred_element_type=F32)


def _dot_nt(a, b):
    return lax.dot_general(a, b, (((1,), (1,)), ((), ())), preferred_element_type=F32)


def _dot_tn(a, b):
    return lax.dot_general(a, b, (((0,), (0,)), ((), ())), preferred_element_type=F32)


def _sigmoid(x):
    return 1.0 / (1.0 + jnp.exp(-x))


def _softplus(x):
    return jnp.maximum(x, 0.0) + jnp.log(1.0 + jnp.exp(-jnp.abs(x)))


def _split3(v):
    h = v.astype(BF16)
    r = v - h.astype(F32)
    m = r.astype(BF16)
    l = (r - m.astype(F32)).astype(BF16)
    return h, m, l


def _tri_left(tri, v):
    h, m, l = _split3(v)
    return _dot(tri, h) + _dot(tri, m) + _dot(tri, l)


def _tri_right(v, tri):
    h, m, l = _split3(v)
    return _dot(h, tri) + _dot(m, tri) + _dot(l, tri)


def _matmul(a, b, mode, out_dtype, name):
    assert a.dtype == BF16 and b.dtype == BF16, (name, a.dtype, b.dtype)
    if mode == "nn":
        (m, k), n = a.shape, b.shape[1]
    elif mode == "nt":
        (m, k), n = a.shape, b.shape[0]
    else:
        (k, m), n = a.shape, b.shape[1]
    tm = _pick(m, (512, 640, 256, 128))
    tn = _pick(n, (512, 640, 256, 128))
    tk = _pick(k, (1024, 512, 640, 256, 128))
    nk = k // tk
    if mode == "nn":
        dn = (((1,), (0,)), ((), ()))
        a_spec = pl.BlockSpec((tm, tk), lambda i, j, kk: (i, kk))
        b_spec = pl.BlockSpec((tk, tn), lambda i, j, kk: (kk, j))
    elif mode == "nt":
        dn = (((1,), (1,)), ((), ()))
        a_spec = pl.BlockSpec((tm, tk), lambda i, j, kk: (i, kk))
        b_spec = pl.BlockSpec((tn, tk), lambda i, j, kk: (j, kk))
    else:
        dn = (((0,), (0,)), ((), ()))
        a_spec = pl.BlockSpec((tk, tm), lambda i, j, kk: (kk, i))
        b_spec = pl.BlockSpec((tk, tn), lambda i, j, kk: (kk, j))

    def body(a_ref, b_ref, o_ref, acc_ref):
        kk = pl.program_id(2)

        @pl.when(kk == 0)
        def _():
            acc_ref[...] = jnp.zeros_like(acc_ref)

        acc_ref[...] += lax.dot_general(a_ref[...], b_ref[...], dn, preferred_element_type=F32)

        @pl.when(kk == nk - 1)
        def _():
            o_ref[...] = acc_ref[...].astype(o_ref.dtype)

    return pl.pallas_call(
        body, name=name, grid=(m // tm, n // tn, nk),
        in_specs=[a_spec, b_spec],
        out_specs=pl.BlockSpec((tm, tn), lambda i, j, kk: (i, j)),
        out_shape=jax.ShapeDtypeStruct((m, n), out_dtype),
        scratch_shapes=[pltpu.VMEM((tm, tn), F32)],
        compiler_params=_cp("parallel", "parallel", "arbitrary"),
    )(a, b)


def _rows(body, name, t, tb, row_in, full_in, row_out, acc_out):
    in_specs, args = [], []
    for r in row_in:
        if isinstance(r, tuple):
            arr, w, j = r
            in_specs.append(pl.BlockSpec((tb, w), functools.partial(lambda i, jj: (i, jj), jj=j)))
            args.append(arr)
        else:
            in_specs.append(pl.BlockSpec((tb, r.shape[1]), lambda i: (i, 0)))
            args.append(r)
    for f in full_in:
        in_specs.append(pl.BlockSpec(f.shape, functools.partial(lambda i, nd: (0,) * nd, nd=f.ndim)))
        args.append(f)
    out_specs = [pl.BlockSpec((tb, c), lambda i: (i, 0)) for c, _ in row_out]
    out_specs += [pl.BlockSpec(s, functools.partial(lambda i, nd: (0,) * nd, nd=len(s))) for s in acc_out]
    out_shape = [jax.ShapeDtypeStruct((t, c), dt) for c, dt in row_out]
    out_shape += [jax.ShapeDtypeStruct(s, F32) for s in acc_out]
    return pl.pallas_call(
        body, name=name, grid=(t // tb,), in_specs=in_specs, out_specs=out_specs, out_shape=out_shape,
        compiler_params=_cp("arbitrary"),
    )(*args)


def _ln_stats(r):
    mu = jnp.mean(r, axis=-1, keepdims=True)
    xc = r - mu
    var = jnp.mean(xc * xc, axis=-1, keepdims=True)
    rstd = lax.rsqrt(var + LN_EPS)
    return xc * rstd, rstd


def _ln_bwd(dout, xhat, rstd, g):
    dxh = dout * g
    m1 = jnp.mean(dxh, axis=-1, keepdims=True)
    m2 = jnp.mean(dxh * xhat, axis=-1, keepdims=True)
    return rstd * (dxh - m1 - xhat * m2)


def _first(i, acc_refs):
    @pl.when(i == 0)
    def _():
        for a in acc_refs:
            a[...] = jnp.zeros_like(a)


def _ln_in_fwd(x, g, b):
    t = x.shape[0]

    def body(x_ref, g_ref, b_ref, o_ref):
        xhat, _ = _ln_stats(x_ref[...])
        o_ref[...] = xhat * g_ref[...] + b_ref[...]

    return _rows(body, "ln_in_fwd", t, 256, [x], [g, b], [(D, F32)], [])[0]


def _ln_in_bwd(dx0, x, g):
    t = x.shape[0]

    def body(d_ref, x_ref, g_ref, o_ref, acc_ref):
        _first(pl.program_id(0), [acc_ref])
        xhat, rstd = _ln_stats(x_ref[...])
        d = d_ref[...]
        o_ref[...] = _ln_bwd(d, xhat, rstd, g_ref[...])
        acc_ref[0:1, :] += jnp.sum(d * xhat, axis=0, keepdims=True)
        acc_ref[1:2, :] += jnp.sum(d, axis=0, keepdims=True)

    return _rows(body, "ln_in_bwd", t, 256, [dx0, x], [g], [(D, F32)], [(2, D)])


def _modulate(xin, mod3, name):
    t = xin.shape[0]

    def body(x_ref, m_ref, o_ref):
        o_ref[...] = (x_ref[...] * (1.0 + m_ref[1:2, :]) + m_ref[0:1, :]).astype(BF16)

    return _rows(body, name, t, 256, [xin], [mod3], [(D, BF16)], [])[0]


def _modulate_bwd(dxres, dh, xin, mod3, name):
    t = xin.shape[0]

    def body(r_ref, dh_ref, x_ref, m_ref, o_ref, acc_ref):
        _first(pl.program_id(0), [acc_ref])
        dh_v = dh_ref[...]
        o_ref[...] = r_ref[...] + dh_v * (1.0 + m_ref[1:2, :])
        acc_ref[0:1, :] += jnp.sum(dh_v, axis=0, keepdims=True)
        acc_ref[1:2, :] += jnp.sum(dh_v * x_ref[...], axis=0, keepdims=True)

    return _rows(body, name, t, 256, [dxres, dh, xin], [mod3], [(D, F32)], [(2, D)])


def _swiglu(u, name):
    t = u.shape[0]

    def body(g_ref, u_ref, o_ref):
        gate = g_ref[...]
        o_ref[...] = (gate * _sigmoid(gate) * u_ref[...]).astype(BF16)

    return _rows(body, name, t, 128, [(u, D_FF, 0), (u, D_FF, 1)], [], [(D_FF, BF16)], [])[0]


def _swiglu_bwd(u, da, name):
    t = u.shape[0]

    def body(g_ref, u_ref, da_ref, o_ref):
        gate, up, d = g_ref[...], u_ref[...], da_ref[...]
        sg = _sigmoid(gate)
        o_ref[:, 0:D_FF] = (d * up * (sg * (1.0 + gate * (1.0 - sg)))).astype(BF16)
        o_ref[:, D_FF:2 * D_FF] = (d * gate * sg).astype(BF16)

    return _rows(body, name, t, 128, [(u, D_FF, 0), (u, D_FF, 1), da], [], [(2 * D_FF, BF16)], [])[0]


def _res_ln(xin, y, mod3, lg, lb, factor, name):
    t = xin.shape[0]

    def body(x_ref, y_ref, m_ref, g_ref, b_ref, o_ref):
        r = ALPHA * x_ref[...] + (factor * m_ref[2:3, :]) * y_ref[...]
        xhat, _ = _ln_stats(r)
        o_ref[...] = xhat * g_ref[...] + b_ref[...]

    return _rows(body, name, t, 256, [xin, y], [mod3, lg, lb], [(D, F32)], [])[0]


def _res_ln_bwd(dout, xin, y, mod3, lg, factor, name):
    t = xin.shape[0]

    def body(d_ref, x_ref, y_ref, m_ref, g_ref, dres_ref, dy_ref, acc_ref):
        _first(pl.program_id(0), [acc_ref])
        gate = factor * m_ref[2:3, :]
        yv = y_ref[...]
        r = ALPHA * x_ref[...] + gate * yv
        xhat, rstd = _ln_stats(r)
        d = d_ref[...]
        dr = _ln_bwd(d, xhat, rstd, g_ref[...])
        dres_ref[...] = ALPHA * dr
        dy_ref[...] = (gate * dr).astype(BF16)
        acc_ref[0:1, :] += jnp.sum(d * xhat, axis=0, keepdims=True)
        acc_ref[1:2, :] += jnp.sum(d, axis=0, keepdims=True)
        acc_ref[2:3, :] += jnp.sum(factor * yv * dr, axis=0, keepdims=True)

    return _rows(body, name, t, 256, [dout, xin, y], [mod3, lg], [(D, F32), (D, BF16)], [(3, D)])


def _loss_head(xf, tgt):
    t = xf.shape[0]

    def body(x_ref, t_ref, d_ref, acc_ref):
        _first(pl.program_id(0), [acc_ref])
        e = x_ref[...] - t_ref[...]
        d_ref[...] = e * (1.0 / D)
        part = 0.5 * jnp.sum(jnp.mean(e * e, axis=-1, keepdims=True), axis=0, keepdims=True)
        acc_ref[...] += jnp.broadcast_to(part, acc_ref.shape)

    return _rows(body, "loss_head", t, 256, [xf, tgt], [], [(D, F32)], [(1, LANES)])


def _silu_bf16(c_all):
    def body(c_ref, o_ref):
        v = c_ref[...]
        o_ref[...] = (v * _sigmoid(v)).astype(BF16)

    return _rows(body, "silu_c", c_all.shape[0], c_all.shape[0], [c_all], [], [(c_all.shape[1], BF16)], [])[0]


def _sum_rows(v, name):
    r, n = v.shape
    tn = _pick(n, (8192, 4096, 2048, 1024, 512, 256, 128))

    def body(v_ref, o_ref):
        acc = v_ref[0:1, :]
        for k in range(1, r):
            acc = acc + v_ref[k:k + 1, :]
        o_ref[...] = acc

    return pl.pallas_call(
        body, name=name, grid=(n // tn,),
        in_specs=[pl.BlockSpec((r, tn), lambda j: (0, j))],
        out_specs=pl.BlockSpec((1, tn), lambda j: (0, j)),
        out_shape=jax.ShapeDtypeStruct((1, n), F32),
        compiler_params=_cp("parallel"),
    )(v)


def _elementwise(fn, name, ins, out_dtypes):
    r, c = ins[0].shape
    tb = _pick(r, (128, 64, 32, 16, 8))
    n_in = len(ins)

    def body(*refs):
        outs = fn(*[x[...] for x in refs[:n_in]])
        for o_ref, o in zip(refs[n_in:], outs):
            o_ref[...] = o.astype(o_ref.dtype)

    spec = pl.BlockSpec((tb, c), lambda i: (i, 0))
    return pl.pallas_call(
        body, name=name, grid=(r // tb,), in_specs=[spec] * n_in, out_specs=[spec] * len(out_dtypes),
        out_shape=[jax.ShapeDtypeStruct((r, c), dt) for dt in out_dtypes],
        compiler_params=_cp("parallel"),
    )(*ins)


def _adamw_math(w, g, m, v):
    m = ADAM_B1 * m + (1.0 - ADAM_B1) * g
    v = ADAM_B2 * v + (1.0 - ADAM_B2) * (g * g)
    m_hat = m / (1.0 - ADAM_B1 ** ADAM_STEP)
    v_hat = v / (1.0 - ADAM_B2 ** ADAM_STEP)
    delta = -ADAM_LR * (m_hat / (jnp.sqrt(v_hat) + ADAM_EPS) + ADAM_WD * w)
    return delta, m, v


def _adamw(w, g, m, v, name):
    shape = w.shape
    c = shape[-1]
    w2, g2, m2, v2 = (a.reshape(-1, c) for a in (w, g, m, v))
    outs = _elementwise(_adamw_math, name, [w2, g2, m2, v2], [F32, F32, F32])
    return tuple(o.reshape(shape) for o in outs)


def _remote_exchange(ins, plan, peers_of, name):
    n_in, n_out = len(ins), len(plan)

    def body(*refs):
        in_refs, out_refs = refs[:n_in], refs[n_in:n_in + n_out]
        send_sems, recv_sems = refs[n_in + n_out], refs[n_in + n_out + 1]
        peers = peers_of(lax.axis_index("x"), lax.axis_index("y"), lax.axis_index("c"))
        copies = [
            pltpu.make_async_remote_copy(
                src_ref=in_refs[src], dst_ref=out_refs[k], send_sem=send_sems.at[k], recv_sem=recv_sems.at[k],
                device_id=peers[peer], device_id_type=MESH)
            for k, (peer, src) in enumerate(plan)
        ]
        for cp in copies:
            cp.start()
        for cp in copies:
            cp.wait()

    any_spec = pl.BlockSpec(memory_space=pl.ANY)
    return list(pl.pallas_call(
        body, name=name,
        in_specs=[any_spec] * n_in, out_specs=[any_spec] * n_out,
        out_shape=[jax.ShapeDtypeStruct(ins[src].shape, ins[src].dtype) for _, src in plan],
        scratch_shapes=[pltpu.SemaphoreType.DMA((n_out,)), pltpu.SemaphoreType.DMA((n_out,))],
    )(*ins))


def _sibling(x, y, c):
    return [(x, y, 1 - c)]


def _other_chips(x, y, c):
    return [(1 - x, y, c), (x, 1 - y, c), (1 - x, 1 - y, c)]


def _swap_sibling(arrs, name):
    return _remote_exchange(arrs, [(0, i) for i in range(len(arrs))], _sibling, name)


def _bcast_chips(arrs, name):
    n = len(arrs)
    out = _remote_exchange(arrs, [(p, i) for p in range(3) for i in range(n)], _other_chips, name)
    return [out[p * n:(p + 1) * n] for p in range(3)]


def _scatter_chips(send3, name):
    n = len(send3[0])
    flat = [a for p in range(3) for a in send3[p]]
    out = _remote_exchange(flat, [(p, p * n + i) for p in range(3) for i in range(n)], _other_chips, name)
    return [out[p * n:(p + 1) * n] for p in range(3)]


def _by_chip(me, own, got3):
    by_rel = [own, got3[0], got3[1], got3[2]]
    rel_bits = (0, 2, 1, 3)

    def branch(m):
        def f(ops):
            return [ops[rel_bits.index(i ^ m)] for i in range(4)]
        return f

    return lax.switch(me, [branch(m) for m in range(4)], by_rel)


def _shift_down(v, s, t_iota):
    return jnp.where(t_iota >= s, pltpu.roll(v, s, 0), 0.0)


def _shift_up(v, s, t_iota, t):
    return jnp.where(t_iota < t - s, pltpu.roll(v, t - s, 0), 0.0)


def _ssd_conv_fwd(proj, w, b):
    t = proj.shape[0]
    k_w = SSD_CONV_K

    def body(x_ref, w_ref, b_ref, o_ref):
        x = x_ref[...]
        ti = lax.broadcasted_iota(jnp.int32, x.shape, 0)
        pre = x * w_ref[k_w - 1:k_w, :] + b_ref[...]
        for s in range(1, k_w):
            pre = pre + _shift_down(x, s, ti) * w_ref[k_w - 1 - s:k_w - s, :]
        o_ref[...] = pre * _sigmoid(pre)

    off = P_XBC // LANES
    return pl.pallas_call(
        body, name="ssd_conv_fwd", grid=(SSD_CONV_DIM // LANES,),
        in_specs=[pl.BlockSpec((t, LANES), lambda j: (0, off + j)),
                  pl.BlockSpec((k_w, LANES), lambda j: (0, j)),
                  pl.BlockSpec((1, LANES), lambda j: (0, j))],
        out_specs=pl.BlockSpec((t, LANES), lambda j: (0, j)),
        out_shape=jax.ShapeDtypeStruct((t, SSD_CONV_DIM), F32),
        compiler_params=_cp("parallel"),
    )(proj, w, b)


def _ssd_conv_bwd(dxbc, proj, w, b):
    t = proj.shape[0]
    k_w = SSD_CONV_K

    def body(d_ref, x_ref, w_ref, b_ref, dx_ref, dw_ref, db_ref):
        x = x_ref[...]
        ti = lax.broadcasted_iota(jnp.int32, x.shape, 0)
        shifted = [x] + [_shift_down(x, s, ti) for s in range(1, k_w)]
        pre = b_ref[...] + shifted[0] * w_ref[k_w - 1:k_w, :]
        for s in range(1, k_w):
            pre = pre + shifted[s] * w_ref[k_w - 1 - s:k_w - s, :]
        sg = _sigmoid(pre)
        dpre = d_ref[...] * (sg * (1.0 + pre * (1.0 - sg)))
        db_ref[...] = jnp.sum(dpre, axis=0, keepdims=True)
        dx = dpre * w_ref[k_w - 1:k_w, :]
        for s in range(k_w):
            dw_ref[k_w - 1 - s:k_w - s, :] = jnp.sum(dpre * shifted[s], axis=0, keepdims=True)
            if s:
                dx = dx + _shift_up(dpre, s, ti, t) * w_ref[k_w - 1 - s:k_w - s, :]
        dx_ref[...] = dx

    off = P_XBC // LANES
    return pl.pallas_call(
        body, name="ssd_conv_bwd", grid=(SSD_CONV_DIM // LANES,),
        in_specs=[pl.BlockSpec((t, LANES), lambda j: (0, j)),
                  pl.BlockSpec((t, LANES), lambda j: (0, off + j)),
                  pl.BlockSpec((k_w, LANES), lambda j: (0, j)),
                  pl.BlockSpec((1, LANES), lambda j: (0, j))],
        out_specs=[pl.BlockSpec((t, LANES), lambda j: (0, j)),
                   pl.BlockSpec((k_w, LANES), lambda j: (0, j)),
                   pl.BlockSpec((1, LANES), lambda j: (0, j))],
        out_shape=[jax.ShapeDtypeStruct((t, SSD_CONV_DIM), F32),
                   jax.ShapeDtypeStruct((k_w, SSD_CONV_DIM), F32),
                   jax.ShapeDtypeStruct((1, SSD_CONV_DIM), F32)],
        compiler_params=_cp("parallel"),
    )(dxbc, proj, w, b)


def _shortconv_fwd(proj, w):
    t = proj.shape[0]
    nb = SC_W // LANES
    off = P_SC // LANES

    def body(b_ref, c_ref, x_ref, w_ref, o_ref):
        u = c_ref[...] * x_ref[...]
        ti = lax.broadcasted_iota(jnp.int32, u.shape, 0)
        cv = u * w_ref[SC_K - 1:SC_K, :]
        for s in range(1, SC_K):
            cv = cv + _shift_down(u, s, ti) * w_ref[SC_K - 1 - s:SC_K - s, :]
        o_ref[...] = (b_ref[...] * cv).astype(BF16)

    return pl.pallas_call(
        body, name="shortconv_fwd", grid=(nb,),
        in_specs=[pl.BlockSpec((t, LANES), lambda j: (0, off + j)),
                  pl.BlockSpec((t, LANES), lambda j: (0, off + nb + j)),
                  pl.BlockSpec((t, LANES), lambda j: (0, off + 2 * nb + j)),
                  pl.BlockSpec((SC_K, LANES), lambda j: (0, j))],
        out_specs=pl.BlockSpec((t, LANES), lambda j: (0, j)),
        out_shape=jax.ShapeDtypeStruct((t, SC_W), BF16),
        compiler_params=_cp("parallel"),
    )(proj, proj, proj, w)


def _shortconv_bwd(dy, dy_off, proj, w):
    t = proj.shape[0]
    nb = SC_W // LANES
    off = P_SC // LANES
    doff = dy_off // LANES

    def body(d_ref, b_ref, c_ref, x_ref, w_ref, db_ref, dc_ref, dx_ref, dw_ref):
        cg, xin = c_ref[...], x_ref[...]
        u = cg * xin
        ti = lax.broadcasted_iota(jnp.int32, u.shape, 0)
        shifted = [u] + [_shift_down(u, s, ti) for s in range(1, SC_K)]
        cv = shifted[0] * w_ref[SC_K - 1:SC_K, :]
        for s in range(1, SC_K):
            cv = cv + shifted[s] * w_ref[SC_K - 1 - s:SC_K - s, :]
        d = d_ref[...]
        db_ref[...] = (d * cv).astype(BF16)
        dcv = d * b_ref[...]
        du = dcv * w_ref[SC_K - 1:SC_K, :]
        for s in range(SC_K):
            dw_ref[SC_K - 1 - s:SC_K - s, :] = jnp.sum(dcv * shifted[s], axis=0, keepdims=True)
            if s:
                du = du + _shift_up(dcv, s, ti, t) * w_ref[SC_K - 1 - s:SC_K - s, :]
        dc_ref[...] = (du * xin).astype(BF16)
        dx_ref[...] = (du * cg).astype(BF16)

    tile = pl.BlockSpec((t, LANES), lambda j: (0, j))
    outs = pl.pallas_call(
        body, name="shortconv_bwd", grid=(nb,),
        in_specs=[pl.BlockSpec((t, LANES), lambda j: (0, doff + j)),
                  pl.BlockSpec((t, LANES), lambda j: (0, off + j)),
                  pl.BlockSpec((t, LANES), lambda j: (0, off + nb + j)),
                  pl.BlockSpec((t, LANES), lambda j: (0, off + 2 * nb + j)),
                  pl.BlockSpec((SC_K, LANES), lambda j: (0, j))],
        out_specs=[tile, tile, tile, pl.BlockSpec((SC_K, LANES), lambda j: (0, j))],
        out_shape=[jax.ShapeDtypeStruct((t, SC_W), BF16)] * 3 + [jax.ShapeDtypeStruct((SC_K, SC_W), F32)],
        compiler_params=_cp("parallel"),
    )(dy, proj, proj, proj, w)
    return outs


def _cum_logf(proj, fbias_row):
    t = proj.shape[0]
    blk = ATT_BLOCK

    def body(p_ref, b_ref, o_ref, carry_ref):
        i = pl.program_id(0)

        @pl.when(i == 0)
        def _():
            carry_ref[...] = jnp.zeros_like(carry_ref)

        lf = -_softplus(-(p_ref[...] + b_ref[...]))
        r = lax.broadcasted_iota(jnp.int32, (blk, blk), 0)
        c = lax.broadcasted_iota(jnp.int32, (blk, blk), 1)
        tri = (r >= c).astype(BF16)
        o_ref[...] = _tri_left(tri, lf) + carry_ref[...]
        carry_ref[...] = o_ref[blk - 1:blk, :]

    return pl.pallas_call(
        body, name="cum_logf", grid=(t // blk,),
        in_specs=[pl.BlockSpec((blk, LANES), lambda i: (i, P_PAD // LANES)),
                  pl.BlockSpec((1, LANES), lambda i: (0, 0))],
        out_specs=pl.BlockSpec((blk, LANES), lambda i: (i, 0)),
        out_shape=jax.ShapeDtypeStruct((t, LANES), F32),
        scratch_shapes=[pltpu.VMEM((1, LANES), F32)],
        compiler_params=_cp("arbitrary"),
    )(proj, fbias_row)


def _pad_block_bwd(dcf, ddt, proj, fbias_row):
    t = proj.shape[0]
    blk = ATT_BLOCK
    nb = t // blk

    def body(dcf_ref, ddt_ref, p_ref, b_ref, o_ref, db_ref, carry_ref):
        i = pl.program_id(0)

        @pl.when(i == 0)
        def _():
            carry_ref[...] = jnp.zeros_like(carry_ref)
            db_ref[...] = jnp.zeros_like(db_ref)

        r = lax.broadcasted_iota(jnp.int32, (blk, blk), 0)
        c = lax.broadcasted_iota(jnp.int32, (blk, blk), 1)
        tri = (r <= c).astype(BF16)
        rev = _tri_left(tri, dcf_ref[...]) + carry_ref[...]
        carry_ref[...] = jnp.sum(dcf_ref[...], axis=0, keepdims=True) + carry_ref[...]
        lane = lax.broadcasted_iota(jnp.int32, (blk, LANES), 1)
        is_f = (lane >= PAD_F0) & (lane < PAD_F0 + FOX_HEADS)
        df = jnp.where(is_f, rev * _sigmoid(-(p_ref[...] + b_ref[...])), 0.0)
        db_ref[...] += jnp.sum(df, axis=0, keepdims=True)
        o_ref[...] = jnp.where(lane < PAD_DT0 + SSD_HEADS, ddt_ref[...], df).astype(BF16)

    return pl.pallas_call(
        body, name="pad_block_bwd", grid=(nb,),
        in_specs=[pl.BlockSpec((blk, LANES), lambda i: (nb - 1 - i, 0)),
                  pl.BlockSpec((blk, LANES), lambda i: (nb - 1 - i, 0)),
                  pl.BlockSpec((blk, LANES), lambda i: (nb - 1 - i, P_PAD // LANES)),
                  pl.BlockSpec((1, LANES), lambda i: (0, 0))],
        out_specs=[pl.BlockSpec((blk, LANES), lambda i: (nb - 1 - i, 0)),
                   pl.BlockSpec((1, LANES), lambda i: (0, 0))],
        out_shape=[jax.ShapeDtypeStruct((t, LANES), BF16), jax.ShapeDtypeStruct((1, LANES), F32)],
        scratch_shapes=[pltpu.VMEM((1, LANES), F32)],
        compiler_params=_cp("arbitrary"),
    )(dcf, ddt, proj, fbias_row)


def _att_scores(q, k, cq, ck, qi, kj, blk):
    s = _dot_nt(q, k) * (FOX_HD ** -0.5) + cq - ck
    r = qi * blk + lax.broadcasted_iota(jnp.int32, (blk, blk), 0)
    c = kj * blk + lax.broadcasted_iota(jnp.int32, (blk, blk), 1)
    return jnp.where(r >= c, s, NEG)


def _fox_fwd(q, k, v, cf_col, cf_row):
    h, t, hd = q.shape
    blk = ATT_BLOCK
    nb = t // blk

    def body(q_ref, k_ref, v_ref, cq_ref, ck_ref, o_ref, lse_ref, m_s, l_s, acc_s):
        qi, kj = pl.program_id(1), pl.program_id(2)

        @pl.when(kj == 0)
        def _():
            m_s[...] = jnp.full_like(m_s, NEG)
            l_s[...] = jnp.zeros_like(l_s)
            acc_s[...] = jnp.zeros_like(acc_s)

        @pl.when(kj <= qi)
        def _():
            s = _att_scores(q_ref[...], k_ref[...], cq_ref[...], ck_ref[...], qi, kj, blk)
            m_new = jnp.maximum(m_s[...], jnp.max(s, axis=1, keepdims=True))
            alpha = jnp.exp(m_s[...] - m_new)
            p = jnp.exp(s - m_new)
            l_s[...] = alpha * l_s[...] + jnp.sum(p, axis=1, keepdims=True)
            acc_s[...] = alpha * acc_s[...] + _dot(p.astype(BF16), v_ref[...])
            m_s[...] = m_new

        @pl.when(kj == nb - 1)
        def _():
            o_ref[...] = acc_s[...] / l_s[...]
            lse_ref[...] = m_s[...] + jnp.log(l_s[...])

    qmap = lambda hh, i, j: (hh, i, 0)
    kmap = lambda hh, i, j: (hh, jnp.minimum(j, i), 0)
    return pl.pallas_call(
        body, name="fox_fwd", grid=(h, nb, nb),
        in_specs=[pl.BlockSpec((None, blk, hd), qmap), pl.BlockSpec((None, blk, hd), kmap),
                  pl.BlockSpec((None, blk, hd), kmap), pl.BlockSpec((None, blk, 1), qmap),
                  pl.BlockSpec((None, 1, blk), lambda hh, i, j: (hh, 0, jnp.minimum(j, i)))],
        out_specs=[pl.BlockSpec((None, blk, hd), qmap), pl.BlockSpec((None, blk, 1), qmap)],
        out_shape=[jax.ShapeDtypeStruct((h, t, hd), F32), jax.ShapeDtypeStruct((h, t, 1), F32)],
        scratch_shapes=[pltpu.VMEM((blk, 1), F32), pltpu.VMEM((blk, 1), F32), pltpu.VMEM((blk, hd), F32)],
        compiler_params=_cp("parallel", "parallel", "arbitrary"),
    )(q, k, v, cf_col, cf_row)


def _fox_bwd_q(q, k, v, cf_col, cf_row, o, lse, do):
    h, t, hd = q.shape
    blk = ATT_BLOCK
    nb = t // blk

    def body(q_ref, k_ref, v_ref, cq_ref, ck_ref, o_ref, lse_ref, do_ref, dq_ref, dcq_ref):
        qi, kj = pl.program_id(1), pl.program_id(2)

        @pl.when(kj == 0)
        def _():
            dq_ref[...] = jnp.zeros_like(dq_ref)
            dcq_ref[...] = jnp.zeros_like(dcq_ref)

        @pl.when(kj <= qi)
        def _():
            s = _att_scores(q_ref[...], k_ref[...], cq_ref[...], ck_ref[...], qi, kj, blk)
            p = jnp.exp(s - lse_ref[...])
            do_v = do_ref[...]
            delta = jnp.sum(do_v * o_ref[...], axis=1, keepdims=True)
            dp = _dot_nt(do_v.astype(BF16), v_ref[...])
            ds = p * (dp - delta)
            dq_ref[...] += _dot(ds.astype(BF16), k_ref[...]) * (FOX_HD ** -0.5)
            dcq_ref[...] += jnp.sum(ds, axis=1, keepdims=True)

    qmap = lambda hh, i, j: (hh, i, 0)
    kmap = lambda hh, i, j: (hh, jnp.minimum(j, i), 0)
    return pl.pallas_call(
        body, name="fox_bwd_q", grid=(h, nb, nb),
        in_specs=[pl.BlockSpec((None, blk, hd), qmap), pl.BlockSpec((None, blk, hd), kmap),
                  pl.BlockSpec((None, blk, hd), kmap), pl.BlockSpec((None, blk, 1), qmap),
                  pl.BlockSpec((None, 1, blk), lambda hh, i, j: (hh, 0, jnp.minimum(j, i))),
                  pl.BlockSpec((None, blk, hd), qmap), pl.BlockSpec((None, blk, 1), qmap),
                  pl.BlockSpec((None, blk, hd), qmap)],
        out_specs=[pl.BlockSpec((None, blk, hd), qmap), pl.BlockSpec((None, blk, 1), qmap)],
        out_shape=[jax.ShapeDtypeStruct((h, t, hd), F32), jax.ShapeDtypeStruct((h, t, 1), F32)],
        compiler_params=_cp("parallel", "parallel", "arbitrary"),
    )(q, k, v, cf_col, cf_row, o, lse, do)


def _fox_bwd_kv(q, k, v, cf_col, cf_row, o, lse, do):
    h, t, hd = q.shape
    blk = ATT_BLOCK
    nb = t // blk

    def body(q_ref, k_ref, v_ref, cq_ref, ck_ref, o_ref, lse_ref, do_ref, dk_ref, dv_ref, dck_ref):
        kj, qi = pl.program_id(1), pl.program_id(2)

        @pl.when(qi == 0)
        def _():
            dk_ref[...] = jnp.zeros_like(dk_ref)
            dv_ref[...] = jnp.zeros_like(dv_ref)
            dck_ref[...] = jnp.zeros_like(dck_ref)

        @pl.when(qi >= kj)
        def _():
            s = _att_scores(q_ref[...], k_ref[...], cq_ref[...], ck_ref[...], qi, kj, blk)
            p = jnp.exp(s - lse_ref[...])
            do_v = do_ref[...]
            do_b = do_v.astype(BF16)
            delta = jnp.sum(do_v * o_ref[...], axis=1, keepdims=True)
            dv_ref[...] += _dot_tn(p.astype(BF16), do_b)
            dp = _dot_nt(do_b, v_ref[...])
            ds = p * (dp - delta)
            dk_ref[...] += _dot_tn(ds.astype(BF16), q_ref[...]) * (FOX_HD ** -0.5)
            dck_ref[...] -= jnp.sum(ds, axis=0, keepdims=True)

    qmap = lambda hh, j, i: (hh, jnp.maximum(i, j), 0)
    kmap = lambda hh, j, i: (hh, j, 0)
    return pl.pallas_call(
        body, name="fox_bwd_kv", grid=(h, nb, nb),
        in_specs=[pl.BlockSpec((None, blk, hd), qmap), pl.BlockSpec((None, blk, hd), kmap),
                  pl.BlockSpec((None, blk, hd), kmap), pl.BlockSpec((None, blk, 1), qmap),
                  pl.BlockSpec((None, 1, blk), lambda hh, j, i: (hh, 0, j)),
                  pl.BlockSpec((None, blk, hd), qmap), pl.BlockSpec((None, blk, 1), qmap),
                  pl.BlockSpec((None, blk, hd), qmap)],
        out_specs=[pl.BlockSpec((None, blk, hd), kmap), pl.BlockSpec((None, blk, hd), kmap),
                   pl.BlockSpec((None, 1, blk), lambda hh, j, i: (hh, 0, j))],
        out_shape=[jax.ShapeDtypeStruct((h, t, hd), F32), jax.ShapeDtypeStruct((h, t, hd), F32),
                   jax.ShapeDtypeStruct((h, 1, t), F32)],
        compiler_params=_cp("parallel", "parallel", "arbitrary"),
    )(q, k, v, cf_col, cf_row, o, lse, do)


def _lane_col(v, h):
    lane = lax.broadcasted_iota(jnp.int32, v.shape, 1)
    return jnp.sum(jnp.where(lane == h, v, 0.0), axis=1, keepdims=True)


def _sub_row(v, h):
    sub = lax.broadcasted_iota(jnp.int32, v.shape, 0)
    return jnp.sum(jnp.where(sub == h, v, 0.0), axis=0, keepdims=True)


def _ssd_decays(pad, pad_t, dtb_row, alog_row, dtb_col, alog_col, blk):
    r = lax.broadcasted_iota(jnp.int32, (blk, blk), 0)
    c = lax.broadcasted_iota(jnp.int32, (blk, blk), 1)
    tril = r >= c
    dt_c = _softplus(pad + dtb_row)
    acs_c = _tri_left(tril.astype(BF16), dt_c * (-jnp.exp(alog_row)))
    dt_r = _softplus(pad_t + dtb_col)
    acs_r = _tri_right(dt_r * (-jnp.exp(alog_col)), (r <= c).astype(BF16))
    rows = lax.broadcasted_iota(jnp.int32, acs_c.shape, 0)
    acs_last = jnp.sum(jnp.where(rows == blk - 1, acs_c, 0.0), axis=0, keepdims=True)
    return dt_c, acs_c, acs_r, acs_last, tril


def _pair_terms(pair, dt_c, acs_c, acs_r, acs_last, d_row, blk):
    lane = lax.broadcasted_iota(jnp.int32, (blk, LANES), 1)
    lo = lane < 64
    lo_row = lax.broadcasted_iota(jnp.int32, (1, LANES), 1) < 64
    h0, h1 = 2 * pair, 2 * pair + 1
    col = [_lane_col(acs_c, h0), _lane_col(acs_c, h1)]
    row = [_sub_row(acs_r, h0), _sub_row(acs_r, h1)]
    last = [_lane_col(acs_last, h0), _lane_col(acs_last, h1)]
    dt_p = jnp.where(lo, _lane_col(dt_c, h0), _lane_col(dt_c, h1))
    e_p = jnp.where(lo, jnp.exp(col[0]), jnp.exp(col[1]))
    w_p = jnp.where(lo, jnp.exp(last[0] - col[0]), jnp.exp(last[1] - col[1]))
    decay_p = jnp.where(lo_row, jnp.exp(last[0]), jnp.exp(last[1]))
    d_p = jnp.where(lo_row, _lane_col(d_row, h0), _lane_col(d_row, h1))
    return lo, lo_row, col, row, last, dt_p, e_p, w_p, decay_p, d_p


def _ssd_specs(t, blk, rev):
    nc = t // blk
    ix = (lambda i: nc - 1 - i) if rev else (lambda i: i)
    xbc = pl.BlockSpec((blk, SSD_CONV_DIM), lambda i: (ix(i), 0))
    pad = pl.BlockSpec((blk, LANES), lambda i: (ix(i), P_PAD // LANES))
    pad_t = pl.BlockSpec((LANES, blk), lambda i: (0, ix(i)))
    z = pl.BlockSpec((blk, SSD_W), lambda i: (ix(i), 0))
    row = pl.BlockSpec((1, LANES), lambda i: (0, 0))
    colv = pl.BlockSpec((LANES, 1), lambda i: (0, 0))
    ng = pl.BlockSpec((1, SSD_W), lambda i: (0, 0))
    y = pl.BlockSpec((blk, SSD_W), lambda i: (ix(i), 0))
    st = pl.BlockSpec((None, 4, LANES, LANES), lambda i: (ix(i), 0, 0, 0))
    return nc, xbc, pad, pad_t, z, row, colv, ng, y, st


def _ssd_fwd(xbc, proj, pad_t, dtb_row, alog_row, d_row, dtb_col, alog_col, ng):
    t = xbc.shape[0]
    blk = min(SSD_CHUNK, t)
    nc, s_xbc, s_pad, s_padt, s_z, s_row, s_col, s_ng, s_y, s_st = _ssd_specs(t, blk, False)

    def body(xbc_ref, pad_ref, padt_ref, z_ref, dtb_ref, alog_ref, d_ref, dtbc_ref, alogc_ref, ng_ref,
             out_ref, ypre_ref, st_ref, state):
        @pl.when(pl.program_id(0) == 0)
        def _():
            state[...] = jnp.zeros_like(state)

        dt_c, acs_c, acs_r, acs_last, tril = _ssd_decays(
            pad_ref[...], padt_ref[...], dtb_ref[...], alog_ref[...], dtbc_ref[...], alogc_ref[...], blk)
        ys = []
        g_mat = {}
        for pair in range(4):
            g = pair // 2
            bg = xbc_ref[:, SSD_W + LANES * g:SSD_W + LANES * (g + 1)].astype(BF16)
            cg = xbc_ref[:, SSD_W + 2 * LANES + LANES * g:SSD_W + 2 * LANES + LANES * (g + 1)].astype(BF16)
            if g not in g_mat:
                g_mat[g] = _dot_nt(cg, bg)
            xs_p = xbc_ref[:, LANES * pair:LANES * (pair + 1)]
            lo, _, col, row, _, dt_p, e_p, w_p, decay_p, d_p = _pair_terms(
                pair, dt_c, acs_c, acs_r, acs_last, d_ref[...], blk)
            x_p = xs_p * dt_p
            y = None
            for hh in range(2):
                lm = jnp.exp(jnp.where(tril, col[hh] - row[hh], NEG))
                m_h = (g_mat[g] * lm).astype(BF16)
                x_h = jnp.where(lo if hh == 0 else ~lo, x_p, 0.0).astype(BF16)
                y_h = _dot(m_h, x_h)
                y = y_h if y is None else y + y_h
            st_in = state[pair]
            st_ref[pair] = st_in
            y = y + e_p * _dot(cg, st_in.astype(BF16))
            state[pair] = decay_p * st_in + _dot_tn(bg, (x_p * w_p).astype(BF16))
            ys.append(y + d_p * xs_p)
        y_all = jnp.concatenate(ys, axis=1)
        ypre_ref[...] = y_all
        z = z_ref[...]
        y2 = y_all * (z * _sigmoid(z))
        outs = []
        for g in range(2):
            seg = y2[:, 256 * g:256 * (g + 1)]
            rr = lax.rsqrt(jnp.mean(seg * seg, axis=-1, keepdims=True) + RMS_EPS)
            outs.append(seg * rr * ng_ref[:, 256 * g:256 * (g + 1)])
        out_ref[...] = jnp.concatenate(outs, axis=1).astype(BF16)

    return pl.pallas_call(
        body, name="ssd_scan_fwd", grid=(nc,),
        in_specs=[s_xbc, s_pad, s_padt, s_z, s_row, s_row, s_row, s_col, s_col, s_ng],
        out_specs=[s_y, s_y, s_st],
        out_shape=[jax.ShapeDtypeStruct((t, SSD_W), BF16), jax.ShapeDtypeStruct((t, SSD_W), F32),
                   jax.ShapeDtypeStruct((nc, 4, LANES, LANES), F32)],
        scratch_shapes=[pltpu.VMEM((4, LANES, LANES), F32)],
        compiler_params=_cp("arbitrary"),
    )(xbc, proj, pad_t, proj, dtb_row, alog_row, d_row, dtb_col, alog_col, ng)


def _ssd_bwd(dout, dout_off, xbc, proj, pad_t, ypre, states, dtb_row, alog_row, d_row, dtb_col, alog_col, ng):
    t = xbc.shape[0]
    blk = min(SSD_CHUNK, t)
    nc, s_xbc, s_pad, s_padt, s_z, s_row, s_col, s_ng, s_y, s_st = _ssd_specs(t, blk, True)
    s_dout = pl.BlockSpec((blk, SSD_W), lambda i: (nc - 1 - i, dout_off // SSD_W))

    def body(dout_ref, xbc_ref, pad_ref, padt_ref, z_ref, ypre_ref, st_ref, dtb_ref, alog_ref, d_ref,
             dtbc_ref, alogc_ref, ng_ref, dxbc_ref, ddt_ref, dz_ref, acc_ref, dng_ref, dstate):
        @pl.when(pl.program_id(0) == 0)
        def _():
            dstate[...] = jnp.zeros_like(dstate)
            acc_ref[...] = jnp.zeros_like(acc_ref)
            dng_ref[...] = jnp.zeros_like(dng_ref)

        pad = pad_ref[...]
        dt_c, acs_c, acs_r, acs_last, tril = _ssd_decays(
            pad, padt_ref[...], dtb_ref[...], alog_ref[...], dtbc_ref[...], alogc_ref[...], blk)
        a_row = -jnp.exp(alog_ref[...])

        z = z_ref[...]
        sz = _sigmoid(z)
        silu_z = z * sz
        y_pre = ypre_ref[...]
        y2 = y_pre * silu_z
        dy2 = []
        for g in range(2):
            sl = slice(256 * g, 256 * (g + 1))
            seg = y2[:, sl]
            rr = lax.rsqrt(jnp.mean(seg * seg, axis=-1, keepdims=True) + RMS_EPS)
            nrm = seg * rr
            d_seg = dout_ref[:, sl]
            dng_ref[:, sl] += jnp.sum(d_seg * nrm, axis=0, keepdims=True)
            dn = d_seg * ng_ref[:, sl]
            dy2.append(rr * (dn - nrm * jnp.mean(dn * nrm, axis=-1, keepdims=True)))
        dy2 = jnp.concatenate(dy2, axis=1)
        dz_ref[...] = (dy2 * y_pre * (sz * (1.0 + z * (1.0 - sz)))).astype(BF16)
        dy_all = dy2 * silu_z

        lane_row = lax.broadcasted_iota(jnp.int32, (1, LANES), 1)
        lane_blk = lax.broadcasted_iota(jnp.int32, (blk, LANES), 1)
        row_col = lax.broadcasted_iota(jnp.int32, (blk, 1), 0)
        ddt = jnp.zeros((blk, LANES), F32)
        dacs = jnp.zeros((blk, LANES), F32)
        dd_row = jnp.zeros((1, LANES), F32)
        ones_b = jnp.ones((blk, LANES), BF16)
        dxs = []
        d_b = [None, None]
        d_c = [None, None]
        d_g = [None, None]
        bgs, cgs = {}, {}
        g_mat = {}
        for pair in range(4):
            g = pair // 2
            if g not in g_mat:
                bgs[g] = xbc_ref[:, SSD_W + LANES * g:SSD_W + LANES * (g + 1)].astype(BF16)
                cgs[g] = xbc_ref[:, SSD_W + 2 * LANES + LANES * g:SSD_W + 2 * LANES + LANES * (g + 1)].astype(BF16)
                g_mat[g] = _dot_nt(cgs[g], bgs[g])
            bg, cg = bgs[g], cgs[g]
            xs_p = xbc_ref[:, LANES * pair:LANES * (pair + 1)]
            lo, lo_row, col, row, last, dt_p, e_p, w_p, decay_p, d_p = _pair_terms(
                pair, dt_c, acs_c, acs_r, acs_last, d_ref[...], blk)
            x_p = xs_p * dt_p
            dy_p = dy_all[:, LANES * pair:LANES * (pair + 1)]
            st_in = st_ref[pair]
            dst = dstate[pair]
            dx_diag = None
            for hh in range(2):
                sel = lo if hh == 0 else ~lo
                lm = jnp.exp(jnp.where(tril, col[hh] - row[hh], NEG))
                m_f = g_mat[g] * lm
                m_h = m_f.astype(BF16)
                x_h = jnp.where(sel, x_p, 0.0).astype(BF16)
                dy_h = jnp.where(sel, dy_p, 0.0).astype(BF16)
                dxd = _dot_tn(m_h, dy_h)
                dm = _dot_nt(dy_h, x_h)
                dg_h = dm * lm
                p_b = (dm * m_f).astype(BF16)
                dacs = dacs + jnp.where(lane_blk == 2 * pair + hh, _dot(p_b, ones_b) - _dot_tn(p_b, ones_b), 0.0)
                dx_diag = dxd if dx_diag is None else dx_diag + dxd
                d_g[g] = dg_h if d_g[g] is None else d_g[g] + dg_h
            st_b = st_in.astype(BF16)
            dst_b = dst.astype(BF16)
            y_off = e_p * _dot(cg, st_b)
            edy = (e_p * dy_p).astype(BF16)
            dc_off = _dot_nt(edy, st_b)
            d_c[g] = dc_off if d_c[g] is None else d_c[g] + dc_off
            dstate[pair] = decay_p * dst + _dot_tn(cg, edy)
            dx_state = _dot(bg, dst_b) * w_p
            db_st = _dot_nt((x_p * w_p).astype(BF16), dst_b)
            d_b[g] = db_st if d_b[g] is None else d_b[g] + db_st
            dx = dx_diag + dx_state
            dxs.append(dx * dt_p + d_p * dy_p)
            prod_dt = dx * xs_p
            prod_acs = dy_p * y_off - x_p * dx_state
            prod_st = x_p * dx_state
            prod_d = dy_p * xs_p
            st_prod = jnp.sum(dst * st_in, axis=0, keepdims=True)
            for hh in range(2):
                h = 2 * pair + hh
                sel = lo if hh == 0 else ~lo
                sel_row = lo_row if hh == 0 else ~lo_row
                ddt_h = jnp.sum(jnp.where(sel, prod_dt, 0.0), axis=1, keepdims=True)
                dacs_h = jnp.sum(jnp.where(sel, prod_acs, 0.0), axis=1, keepdims=True)
                tail = jnp.sum(jnp.sum(jnp.where(sel, prod_st, 0.0), axis=1, keepdims=True), axis=0, keepdims=True)
                tail = tail + jnp.exp(last[hh]) * jnp.sum(jnp.where(sel_row, st_prod, 0.0), axis=1, keepdims=True)
                dacs_h = dacs_h + jnp.where(row_col == blk - 1, tail, 0.0)
                dd_h = jnp.sum(jnp.sum(jnp.where(sel, prod_d, 0.0), axis=1, keepdims=True), axis=0, keepdims=True)
                ddt = ddt + jnp.where(lane_blk == h, ddt_h, 0.0)
                dacs = dacs + jnp.where(lane_blk == h, dacs_h, 0.0)
                dd_row = dd_row + jnp.where(lane_row == h, dd_h, 0.0)
        for g in range(2):
            dg_b = d_g[g].astype(BF16)
            d_c[g] = d_c[g] + _dot(dg_b, bgs[g])
            d_b[g] = d_b[g] + _dot_tn(dg_b, cgs[g])
        r = lax.broadcasted_iota(jnp.int32, (blk, blk), 0)
        c = lax.broadcasted_iota(jnp.int32, (blk, blk), 1)
        da = _tri_left((r <= c).astype(BF16), dacs)
        ddt = ddt + da * a_row
        d_raw = ddt * _sigmoid(pad + dtb_ref[...])
        ddt_ref[...] = d_raw
        acc_ref[0:1, :] += jnp.sum(da * dt_c, axis=0, keepdims=True) * a_row
        acc_ref[1:2, :] += dd_row
        acc_ref[2:3, :] += jnp.sum(d_raw, axis=0, keepdims=True)
        dxbc_ref[...] = jnp.concatenate(dxs + d_b + d_c, axis=1)

    return pl.pallas_call(
        body, name="ssd_scan_bwd", grid=(nc,),
        in_specs=[s_dout, s_xbc, s_pad, s_padt, s_z, s_y, s_st, s_row, s_row, s_row, s_col, s_col, s_ng],
        out_specs=[s_xbc, pl.BlockSpec((blk, LANES), lambda i: (nc - 1 - i, 0)), s_y,
                   pl.BlockSpec((8, LANES), lambda i: (0, 0)), s_ng],
        out_shape=[jax.ShapeDtypeStruct((t, SSD_CONV_DIM), F32), jax.ShapeDtypeStruct((t, LANES), F32),
                   jax.ShapeDtypeStruct((t, SSD_W), BF16), jax.ShapeDtypeStruct((8, LANES), F32),
                   jax.ShapeDtypeStruct((1, SSD_W), F32)],
        scratch_shapes=[pltpu.VMEM((4, LANES, LANES), F32)],
        compiler_params=_cp("arbitrary"),
    )(dout, xbc, proj, pad_t, proj, ypre, states, dtb_row, alog_row, d_row, dtb_col, alog_col, ng)


def _pad_lanes(v, off):
    return jnp.zeros((1, LANES), F32).at[0, off:off + v.shape[0]].set(v)


def _perm_mix_w_in(w):
    z, xbc, dt = w[:, 0:512], w[:, 512:1536], w[:, 1536:1544]
    qkv, f, sc = w[:, 1544:2312], w[:, 2312:2316], w[:, 2316:3084]
    padblk = jnp.zeros((w.shape[0], LANES), w.dtype).at[:, PAD_DT0:PAD_DT0 + 8].set(dt).at[:, PAD_F0:PAD_F0 + 4].set(f)
    return jnp.concatenate([z, xbc, qkv, sc, padblk], axis=1)


def _unperm_mix_w_in(wp):
    z, xbc, qkv, sc = wp[:, 0:512], wp[:, 512:1536], wp[:, 1536:2304], wp[:, 2304:3072]
    dt, f = wp[:, P_PAD + PAD_DT0:P_PAD + PAD_DT0 + 8], wp[:, P_PAD + PAD_F0:P_PAD + PAD_F0 + 4]
    return jnp.concatenate([z, xbc, dt, qkv, f, sc], axis=1)


def _heads(m):
    return jnp.transpose(m.reshape(m.shape[0], FOX_HEADS, FOX_HD), (1, 0, 2))


def _unheads(m):
    return jnp.transpose(m, (1, 0, 2)).reshape(m.shape[1], FOX_W)


def _ffn_fwd(xin, mod3, w_in, w_out, lg, lb, tag):
    h = _modulate(xin, mod3, f"mod_{tag}")
    u = _matmul(h, w_in, "nn", F32, f"ffn_in_{tag}")
    a = _swiglu(u, f"swiglu_{tag}")
    y = _matmul(a, w_out, "nn", F32, f"ffn_out_{tag}")
    xout = _res_ln(xin, y, mod3, lg, lb, 0.5, f"res_ln_{tag}")
    return xout, (xin, h, u, a, y)


def _ffn_bwd(dout, saved, mod3, w_in, w_out, lg, tag):
    xin, h, u, a, y = saved
    dres, dy, acc_ln = _res_ln_bwd(dout, xin, y, mod3, lg, 0.5, f"res_ln_bwd_{tag}")
    da = _matmul(dy, w_out, "nt", F32, f"ffn_out_dx_{tag}")
    dw_out = _matmul(a, dy, "tn", F32, f"ffn_out_dw_{tag}")
    du = _swiglu_bwd(u, da, f"swiglu_bwd_{tag}")
    dh = _matmul(du, w_in, "nt", F32, f"ffn_in_dx_{tag}")
    dw_in = _matmul(h, du, "tn", F32, f"ffn_in_dw_{tag}")
    dxin, acc_mod = _modulate_bwd(dres, dh, xin, mod3, f"mod_bwd_{tag}")
    dmod3 = jnp.concatenate([acc_mod, acc_ln[2:3]], axis=0)
    return dxin, dw_in, dw_out, acc_ln[0], acc_ln[1], dmod3


def _mix_fwd(xin, mod3, wp, w_out, lg, lb, sp, tag):
    h = _modulate(xin, mod3, f"mod_{tag}")
    proj = _matmul(h, wp, "nn", F32, f"mix_in_{tag}")
    pad_t = jnp.transpose(proj[:, P_PAD:P_PAD + LANES])
    xbc = _ssd_conv_fwd(proj, sp["conv_w"], sp["conv_b"])
    y_ssd, ypre, states = _ssd_fwd(xbc, proj, pad_t, sp["dtb_row"], sp["alog_row"], sp["d_row"],
                                   sp["dtb_col"], sp["alog_col"], sp["ng"])
    cf = _cum_logf(proj, sp["fbias_row"])
    cf4 = jnp.transpose(cf[:, PAD_F0:PAD_F0 + FOX_HEADS])
    cf_col, cf_row = cf4[:, :, None], cf4[:, None, :]
    q = _heads(proj[:, P_QKV:P_QKV + 256].astype(BF16))
    k = _heads(proj[:, P_QKV + 256:P_QKV + 512].astype(BF16))
    v = _heads(proj[:, P_QKV + 512:P_QKV + 768].astype(BF16))
    o, lse = _fox_fwd(q, k, v, cf_col, cf_row)
    y_sc = _shortconv_fwd(proj, sp["sconv_w"])
    ymix = jnp.concatenate([y_ssd, _unheads(o).astype(BF16), y_sc], axis=1)
    y = _matmul(ymix, w_out, "nn", F32, f"mix_out_{tag}")
    xout = _res_ln(xin, y, mod3, lg, lb, 1.0, f"res_ln_{tag}")
    return xout, (xin, h, proj, pad_t, xbc, ypre, states, q, k, v, cf_col, cf_row, o, lse, ymix, y)


def _mix_bwd(dout, saved, mod3, wp, w_out, lg, sp, tag):
    xin, h, proj, pad_t, xbc, ypre, states, q, k, v, cf_col, cf_row, o, lse, ymix, y = saved
    dres, dy, acc_ln = _res_ln_bwd(dout, xin, y, mod3, lg, 1.0, f"res_ln_bwd_{tag}")
    dymix = _matmul(dy, w_out, "nt", F32, f"mix_out_dx_{tag}")
    dw_out = _matmul(ymix, dy, "tn", F32, f"mix_out_dw_{tag}")
    dxbc, ddt, dz, ssd_acc, dng = _ssd_bwd(dymix, 0, xbc, proj, pad_t, ypre, states, sp["dtb_row"],
                                           sp["alog_row"], sp["d_row"], sp["dtb_col"], sp["alog_col"], sp["ng"])
    dxbc_raw, dconv_w, dconv_b = _ssd_conv_bwd(dxbc, proj, sp["conv_w"], sp["conv_b"])
    do = _heads(dymix[:, SSD_W:SSD_W + FOX_W])
    dq, dcq = _fox_bwd_q(q, k, v, cf_col, cf_row, o, lse, do)
    dk, dv, dck = _fox_bwd_kv(q, k, v, cf_col, cf_row, o, lse, do)
    dcf4 = dcq[:, :, 0] + dck[:, 0, :]
    dcf = jnp.zeros((xin.shape[0], LANES), F32).at[:, PAD_F0:PAD_F0 + FOX_HEADS].set(jnp.transpose(dcf4))
    dpad, dfb = _pad_block_bwd(dcf, ddt, proj, sp["fbias_row"])
    dscb, dscc, dscx, dsconv_w = _shortconv_bwd(dymix, SSD_W + FOX_W, proj, sp["sconv_w"])
    dproj = jnp.concatenate([dz, dxbc_raw.astype(BF16), _unheads(dq).astype(BF16), _unheads(dk).astype(BF16),
                             _unheads(dv).astype(BF16), dscb, dscc, dscx, dpad], axis=1)
    dh = _matmul(dproj, wp, "nt", F32, f"mix_in_dx_{tag}")
    dwp = _matmul(h, dproj, "tn", F32, f"mix_in_dw_{tag}")
    dxin, acc_mod = _modulate_bwd(dres, dh, xin, mod3, f"mod_bwd_{tag}")
    dmod3 = jnp.concatenate([acc_mod, acc_ln[2:3]], axis=0)
    small = dict(conv_w=dconv_w, conv_b=dconv_b[0], dt_bias=ssd_acc[2, 0:8], a_log=ssd_acc[0, 0:8],
                 d=ssd_acc[1, 0:8], norm_g=dng[0], f_bias=dfb[0, PAD_F0:PAD_F0 + FOX_HEADS], sconv_w=dsconv_w)
    return dxin, _unperm_mix_w_in(dwp), dw_out, acc_ln[0], acc_ln[1], dmod3, small


BIG = ("ffn1_w_in", "ffn1_w_out", "mix_w_in", "mix_w_out", "ffn2_w_in", "ffn2_w_out")
SHARD_AXIS = {"ffn1_w_in": 1, "ffn1_w_out": 0, "mix_w_in": 1, "mix_w_out": 0, "ffn2_w_in": 1, "ffn2_w_out": 0}


def _local_step(x, tgt, mod, wfull, small_p):
    t = x.shape[0]
    row = lambda v: v.reshape(1, -1)
    x0 = _ln_in_fwd(x, row(small_p["ln_in_g"]), row(small_p["ln_in_b"]))
    cur = x0
    saved, sps, wps = [], [], []
    for l in range(DEPTH):
        w = wfull[l]
        sp = dict(
            conv_w=small_p["ssd_conv_w"][l], conv_b=row(small_p["ssd_conv_b"][l]),
            dtb_row=_pad_lanes(small_p["ssd_dt_bias"][l], PAD_DT0), alog_row=_pad_lanes(small_p["ssd_a_log"][l], 0),
            d_row=_pad_lanes(small_p["ssd_d"][l], 0), ng=row(small_p["ssd_norm_g"][l]),
            fbias_row=_pad_lanes(small_p["fox_f_bias"][l], PAD_F0), sconv_w=small_p["sconv_w"][l])
        sp["dtb_col"] = jnp.transpose(sp["dtb_row"])
        sp["alog_col"] = jnp.transpose(sp["alog_row"])
        wp = _perm_mix_w_in(w["mix_w_in"])
        lg = [row(small_p["ln_g"][l, j]) for j in range(N_SUB)]
        lb = [row(small_p["ln_b"][l, j]) for j in range(N_SUB)]
        cur, s0 = _ffn_fwd(cur, mod[l, 0], w["ffn1_w_in"], w["ffn1_w_out"], lg[0], lb[0], f"l{l}f1")
        cur, s1 = _mix_fwd(cur, mod[l, 1], wp, w["mix_w_out"], lg[1], lb[1], sp, f"l{l}mx")
        cur, s2 = _ffn_fwd(cur, mod[l, 2], w["ffn2_w_in"], w["ffn2_w_out"], lg[2], lb[2], f"l{l}f2")
        saved.append((s0, s1, s2))
        sps.append(sp)
        wps.append(wp)

    dcur, loss_acc = _loss_head(cur, tgt)
    big_grads = [None] * DEPTH
    small_g = [None] * DEPTH
    dmod = [None] * DEPTH
    for l in reversed(range(DEPTH)):
        w = wfull[l]
        s0, s1, s2 = saved[l]
        lg = [row(small_p["ln_g"][l, j]) for j in range(N_SUB)]
        dcur, g2_in, g2_out, dlg2, dlb2, dm2 = _ffn_bwd(dcur, s2, mod[l, 2], w["ffn2_w_in"], w["ffn2_w_out"],
                                                      lg[2], f"l{l}f2")
        dcur, gm_in, gm_out, dlg1, dlb1, dm1, sm = _mix_bwd(dcur, s1, mod[l, 1], wps[l], w["mix_w_out"], lg[1],
                                                           sps[l], f"l{l}mx")
        dcur, g1_in, g1_out, dlg0, dlb0, dm0 = _ffn_bwd(dcur, s0, mod[l, 0], w["ffn1_w_in"], w["ffn1_w_out"],
                                                      lg[0], f"l{l}f1")
        big_grads[l] = dict(ffn1_w_in=g1_in, ffn1_w_out=g1_out, mix_w_in=gm_in, mix_w_out=gm_out,
                            ffn2_w_in=g2_in, ffn2_w_out=g2_out)
        sm["ln_g"] = jnp.stack([dlg0, dlg1, dlg2])
        sm["ln_b"] = jnp.stack([dlb0, dlb1, dlb2])
        small_g[l] = sm
        dmod[l] = jnp.stack([dm0, dm1, dm2])
    dx, acc_in = _ln_in_bwd(dcur, x, row(small_p["ln_in_g"]))
    return loss_acc[0, 0], dx, big_grads, small_g, jnp.stack(dmod), acc_in


SMALL_ORDER = ("ssd_conv_w", "ssd_conv_b", "ssd_dt_bias", "ssd_a_log", "ssd_d", "ssd_norm_g", "fox_f_bias",
               "sconv_w", "ln_g", "ln_b")
SMALL_KEY = dict(ssd_conv_w="conv_w", ssd_conv_b="conv_b", ssd_dt_bias="dt_bias", ssd_a_log="a_log", ssd_d="d",
                 ssd_norm_g="norm_g", fox_f_bias="f_bias", sconv_w="sconv_w", ln_g="ln_g", ln_b="ln_b")
COL_SHARDED_SMALL = ("ssd_conv_w", "sconv_w", "ln_g", "ln_b")


def _pad_to(v, n):
    return jnp.concatenate([v, jnp.zeros((n - v.shape[0],), v.dtype)])


def kernel(x, c, ln_in_g, ln_in_b, ada_w, ada_b, ffn1_w_in, ffn1_w_out, mix_w_in, mix_w_out, ssd_conv_w, ssd_conv_b, ssd_dt_bias, ssd_a_log, ssd_d, ssd_norm_g, fox_f_bias, sconv_w, ffn2_w_in, ffn2_w_out, ln_g, ln_b, loss_target, m_ln_in_g, m_ln_in_b, m_ada_w, m_ada_b, m_ffn1_w_in, m_ffn1_w_out, m_mix_w_in, m_mix_w_out, m_ssd_conv_w, m_ssd_conv_b, m_ssd_dt_bias, m_ssd_a_log, m_ssd_d, m_ssd_norm_g, m_fox_f_bias, m_sconv_w, m_ffn2_w_in, m_ffn2_w_out, m_ln_g, m_ln_b, v_ln_in_g, v_ln_in_b, v_ada_w, v_ada_b, v_ffn1_w_in, v_ffn1_w_out, v_mix_w_in, v_mix_w_out, v_ssd_conv_w, v_ssd_conv_b, v_ssd_dt_bias, v_ssd_a_log, v_ssd_d, v_ssd_norm_g, v_fox_f_bias, v_sconv_w, v_ffn2_w_in, v_ffn2_w_out, v_ln_g, v_ln_b):
    names = ("ln_in_g", "ln_in_b", "ada_w", "ada_b", "ffn1_w_in", "ffn1_w_out", "mix_w_in", "mix_w_out",
             "ssd_conv_w", "ssd_conv_b", "ssd_dt_bias", "ssd_a_log", "ssd_d", "ssd_norm_g", "fox_f_bias", "sconv_w",
             "ffn2_w_in", "ffn2_w_out", "ln_g", "ln_b")
    w_loc = dict(zip(names, (ln_in_g, ln_in_b, ada_w, ada_b, ffn1_w_in, ffn1_w_out, mix_w_in, mix_w_out, ssd_conv_w,
                             ssd_conv_b, ssd_dt_bias, ssd_a_log, ssd_d, ssd_norm_g, fox_f_bias, sconv_w, ffn2_w_in,
                             ffn2_w_out, ln_g, ln_b)))
    m_loc = dict(zip(names, (m_ln_in_g, m_ln_in_b, m_ada_w, m_ada_b, m_ffn1_w_in, m_ffn1_w_out, m_mix_w_in,
                             m_mix_w_out, m_ssd_conv_w, m_ssd_conv_b, m_ssd_dt_bias, m_ssd_a_log, m_ssd_d,
                             m_ssd_norm_g, m_fox_f_bias, m_sconv_w, m_ffn2_w_in, m_ffn2_w_out, m_ln_g, m_ln_b)))
    v_loc = dict(zip(names, (v_ln_in_g, v_ln_in_b, v_ada_w, v_ada_b, v_ffn1_w_in, v_ffn1_w_out, v_mix_w_in,
                             v_mix_w_out, v_ssd_conv_w, v_ssd_conv_b, v_ssd_dt_bias, v_ssd_a_log, v_ssd_d,
                             v_ssd_norm_g, v_fox_f_bias, v_sconv_w, v_ffn2_w_in, v_ffn2_w_out, v_ln_g, v_ln_b)))

    xi, yi, ci = lax.axis_index("x"), lax.axis_index("y"), lax.axis_index("c")
    me = 2 * xi + yi
    dev = 2 * me + ci

    def gather8(v, tag):
        v2 = v.reshape(1, -1)
        got = _bcast_chips([v2], f"gather_chips_{tag}")
        same_c = jnp.concatenate(_by_chip(me, v2, [g[0] for g in got]), axis=0)
        other_c = _swap_sibling([same_c], f"gather_sibling_{tag}")[0]
        pair = lax.switch(ci, [lambda a, b: jnp.stack([a, b], axis=1), lambda a, b: jnp.stack([b, a], axis=1)],
                          same_c, other_c)
        return pair.reshape(8, -1)

    def chip_concat(own, got3, axis):
        return jnp.concatenate(_by_chip(me, own, got3), axis=axis)

    small_cols = [w_loc[n].reshape(-1, w_loc[n].shape[-1]) for n in COL_SHARDED_SMALL]
    got = _bcast_chips(small_cols, "gather_small_params")
    small_p = {n: w_loc[n] for n in ("ln_in_g", "ln_in_b", "ssd_conv_b", "ssd_dt_bias", "ssd_a_log", "ssd_d",
                                     "ssd_norm_g", "fox_f_bias")}
    for i, n in enumerate(COL_SHARDED_SMALL):
        full = chip_concat(small_cols[i], [g[i] for g in got], 1)
        small_p[n] = full.reshape(w_loc[n].shape[:-1] + (full.shape[-1],))

    mine = [lax.dynamic_index_in_dim(w_loc[n], ci, 0, keepdims=False).astype(BF16) for n in BIG]
    got = _bcast_chips(mine, "gather_weights_chips")
    my_layer = [chip_concat(mine[i], [g[i] for g in got], SHARD_AXIS[n]) for i, n in enumerate(BIG)]
    other_layer = _swap_sibling(my_layer, "gather_weights_sibling")
    layers = lax.switch(ci, [lambda a, b: (a, b), lambda a, b: (b, a)], my_layer, other_layer)
    wfull = [dict(zip(BIG, layers[l])) for l in range(DEPTH)]

    c_all = gather8(c[0], "c")
    c_act = _silu_bf16(c_all)
    ada_w_b = ada_w.astype(BF16)
    mod_loc = [_matmul(c_act, ada_w_b[l], "nn", F32, f"ada_fwd_l{l}") for l in range(DEPTH)]
    mod_loc = jnp.stack(mod_loc)
    got = _bcast_chips([mod_loc], "gather_mod")
    mod_all = chip_concat(mod_loc, [g[0] for g in got], 2)
    mod = lax.dynamic_index_in_dim(mod_all, dev, 1, keepdims=False) + ada_b
    mod = mod.reshape(DEPTH, N_SUB, 3, D)

    loss_part, dx, big_g, small_g, dmod, acc_in = _local_step(x[0], loss_target[0], mod, wfull, small_p)
    loss = lax.psum(loss_part, ("x", "y", "c"))

    pieces = [dmod.reshape(-1), acc_in[0], acc_in[1]]
    for n in SMALL_ORDER:
        pieces.append(jnp.stack([small_g[l][SMALL_KEY[n]] for l in range(DEPTH)]).reshape(-1))
    sizes = [p.shape[0] for p in pieces]
    total = sum(sizes)
    padded = -(-total // 1024) * 1024
    vec = _pad_to(jnp.concatenate(pieces), padded)
    all_rows = gather8(vec, "small_grads")
    summed = _sum_rows(all_rows, "sum_small_grads")[0]
    offs = [0]
    for s in sizes:
        offs.append(offs[-1] + s)
    n_mod = sizes[0]
    grads = {"ada_b": summed[0:n_mod].reshape(DEPTH, 3 * N_SUB * D),
             "ln_in_g": summed[offs[1]:offs[2]], "ln_in_b": summed[offs[2]:offs[3]]}
    for i, n in enumerate(SMALL_ORDER):
        full = summed[offs[3 + i]:offs[4 + i]].reshape(small_p[n].shape)
        if n in COL_SHARDED_SMALL:
            wcols = w_loc[n].shape[-1]
            full = lax.dynamic_slice_in_dim(full, me * wcols, wcols, axis=full.ndim - 1)
        grads[n] = full

    dmod_all = all_rows[:, 0:n_mod].reshape(8, DEPTH, 3 * N_SUB * D)
    ncol = ada_w.shape[-1]
    dmod_cols = lax.dynamic_slice_in_dim(dmod_all, me * ncol, ncol, axis=2).astype(BF16)
    grads["ada_w"] = jnp.stack([_matmul(c_act, dmod_cols[:, l], "tn", F32, f"ada_bwd_l{l}") for l in range(DEPTH)])

    give = lax.switch(ci, [lambda g0, g1: [g1[n] for n in BIG], lambda g0, g1: [g0[n] for n in BIG]],
                      big_g[0], big_g[1])
    keep = lax.switch(ci, [lambda g0, g1: [g0[n] for n in BIG], lambda g0, g1: [g1[n] for n in BIG]],
                      big_g[0], big_g[1])
    from_sib = _swap_sibling(give, "grad_swap_sibling")
    chip_part = []
    for i, n in enumerate(BIG):
        chip_part.append(_elementwise(lambda a, b: (a + b,), f"grad_add_cores_{n}", [keep[i], from_sib[i]], [BF16])[0])

    def shards_for(mm):
        def f(parts):
            out = []
            for rel in (0, 2, 1, 3):
                tgt_chip = mm ^ rel
                row = []
                for i, n in enumerate(BIG):
                    ax = SHARD_AXIS[n]
                    w = parts[i].shape[ax] // 4
                    row.append(lax.slice_in_dim(parts[i], tgt_chip * w, (tgt_chip + 1) * w, axis=ax))
                out.append(row)
            return out
        return f

    split = lax.switch(me, [shards_for(mm) for mm in range(4)], chip_part)
    got = _scatter_chips(split[1:], "grad_scatter_chips")
    reduced = []
    for i, n in enumerate(BIG):
        four = _by_chip(me, split[0][i], [g[i] for g in got])
        reduced.append(_elementwise(
            lambda a, b, cc, d: (((a.astype(F32) + b.astype(F32)) + cc.astype(F32)) + d.astype(F32),),
            f"grad_add_chips_{n}", four, [F32])[0])
    other = _swap_sibling(reduced, "grad_swap_back")
    per_layer = lax.switch(ci, [lambda a, b: (a, b), lambda a, b: (b, a)], reduced, other)
    for i, n in enumerate(BIG):
        grads[n] = jnp.stack([per_layer[0][i], per_layer[1][i]])

    delta, new_m, new_v = {}, {}, {}
    for n in ("ada_w",) + BIG:
        delta[n], new_m[n], new_v[n] = _adamw(w_loc[n], grads[n], m_loc[n], v_loc[n], f"adamw_{n}")
    small_names = [n for n in names if n not in ("ada_w",) + BIG]
    flat = lambda d: jnp.concatenate([d[n].reshape(-1) for n in small_names])
    n_small = sum(w_loc[n].size for n in small_names)
    n_pad = -(-n_small // 1024) * 1024
    packed = [_pad_to(flat(d), n_pad).reshape(-1, LANES) for d in (w_loc, grads, m_loc, v_loc)]
    d_s, m_s, v_s = _adamw(*packed, "adamw_small")
    off = 0
    for n in small_names:
        sz = w_loc[n].size
        delta[n] = d_s.reshape(-1)[off:off + sz].reshape(w_loc[n].shape)
        new_m[n] = m_s.reshape(-1)[off:off + sz].reshape(w_loc[n].shape)
        new_v[n] = v_s.reshape(-1)[off:off + sz].reshape(w_loc[n].shape)
        off += sz

    return (loss, dx[None], *[grads[n] for n in names], *[delta[n] for n in names],
            *[new_m[n] for n in names], *[new_v[n] for n in names])
```

```python
import functools

import jax
import jax.numpy as jnp
from jax import lax
from jax.experimental import pallas as pl
from jax.experimental.pallas import tpu as pltpu

F32 = jnp.float32
BF16 = jnp.bfloat16
MESH = pl.DeviceIdType.MESH

D = 1024
DEPTH = 2
N_SUB = 3
D_FF = 2816
ALPHA = (2 * DEPTH) ** 0.25
LN_EPS = 1e-5
RMS_EPS = 1e-5
SSD_W = 512
SSD_HEADS = 8
SSD_CONV_K = 4
SSD_CONV_DIM = 1024
FOX_W = 256
FOX_HEADS = 4
FOX_HD = 64
SC_W = 256
SC_K = 3
D_IN_PROJ = 3084
P_Z, P_XBC, P_QKV, P_SC, P_PAD = 0, 512, 1536, 2304, 3072
D_PROJ_PAD = 3200
PAD_DT0, PAD_F0 = 0, 8
SSD_CHUNK = 256
ATT_BLOCK = 512
CUM_BLOCK = 256
LANES = 128
VMEM_LIMIT = 56 * 1024 * 1024

ADAM_LR, ADAM_B1, ADAM_B2, ADAM_EPS, ADAM_WD, ADAM_STEP = 0.001, 0.9, 0.999, 1e-08, 0.01, 10
NEG = -1e30


def _cp(*sem):
    return pltpu.CompilerParams(dimension_semantics=sem, vmem_limit_bytes=VMEM_LIMIT)


def _pick(n, cands):
    for c in cands:
        if n % c == 0:
            return c
    return n


def _dot(a, b):
    return lax.dot_general(a, b, (((1,), (0,)), ((), ())), preferred_element_type=F32)


def _dot_nt(a, b):
    return lax.dot_general(a, b, (((1,), (1,)), ((), ())), preferred_element_type=F32)


def _dot_tn(a, b):
    return lax.dot_general(a, b, (((0,), (0,)), ((), ())), preferred_element_type=F32)


def _sigmoid(x):
    return 1.0 / (1.0 + jnp.exp(-x))


def _softplus(x):
    return jnp.maximum(x, 0.0) + jnp.log(1.0 + jnp.exp(-jnp.abs(x)))


def _split3(v):
    h = v.astype(BF16)
    r = v - h.astype(F32)
    m = r.astype(BF16)
    l = (r - m.astype(F32)).astype(BF16)
    return h, m, l


def _tri_left(tri, v):
    h, m, l = _split3(v)
    return _dot(tri, h) + _dot(tri, m) + _dot(tri, l)


def _tri_right(v, tri):
    h, m, l = _split3(v)
    return _dot(h, tri) + _dot(m, tri) + _dot(l, tri)


def _matmul(a, b, mode, out_dtype, name):
    assert a.dtype == BF16 and b.dtype == BF16, (name, a.dtype, b.dtype)
    if mode == "nn":
        (m, k), n = a.shape, b.shape[1]
    elif mode == "nt":
        (m, k), n = a.shape, b.shape[0]
    else:
        (k, m), n = a.shape, b.shape[1]
    tm = m if m <= 1024 else _pick(m, (1024, 1408, 512, 256, 128))
    tn = n if n <= 1024 else _pick(n, (1408, 640, 512, 256, 128))
    tk = k if k <= 1024 else _pick(k, (1408, 1024, 640, 512, 256, 128))
    nk = k // tk
    if mode == "nn":
        dn = (((1,), (0,)), ((), ()))
        a_spec = pl.BlockSpec((tm, tk), lambda i, j, kk: (i, kk))
        b_spec = pl.BlockSpec((tk, tn), lambda i, j, kk: (kk, j))
    elif mode == "nt":
        dn = (((1,), (1,)), ((), ()))
        a_spec = pl.BlockSpec((tm, tk), lambda i, j, kk: (i, kk))
        b_spec = pl.BlockSpec((tn, tk), lambda i, j, kk: (j, kk))
    else:
        dn = (((0,), (0,)), ((), ()))
        a_spec = pl.BlockSpec((tk, tm), lambda i, j, kk: (kk, i))
        b_spec = pl.BlockSpec((tk, tn), lambda i, j, kk: (kk, j))

    def body(a_ref, b_ref, o_ref, *acc):
        d = lax.dot_general(a_ref[...], b_ref[...], dn, preferred_element_type=F32)
        if nk == 1:
            o_ref[...] = d.astype(o_ref.dtype)
            return
        acc_ref, = acc
        kk = pl.program_id(2)

        @pl.when(kk == 0)
        def _():
            acc_ref[...] = d

        @pl.when((kk > 0) & (kk < nk - 1))
        def _():
            acc_ref[...] += d

        @pl.when(kk == nk - 1)
        def _():
            o_ref[...] = (acc_ref[...] + d).astype(o_ref.dtype)

    return pl.pallas_call(
        body, name=name, grid=(m // tm, n // tn, nk),
        in_specs=[a_spec, b_spec],
        out_specs=pl.BlockSpec((tm, tn), lambda i, j, kk: (i, j)),
        out_shape=jax.ShapeDtypeStruct((m, n), out_dtype),
        scratch_shapes=[pltpu.VMEM((tm, tn), F32)] if nk > 1 else [],
        compiler_params=_cp("parallel", "parallel", "arbitrary"),
    )(a, b)


def _rows(body, name, t, tb, row_in, full_in, row_out, acc_out):
    in_specs, args = [], []
    for r in row_in:
        if isinstance(r, tuple):
            arr, w, j = r
            in_specs.append(pl.BlockSpec((tb, w), functools.partial(lambda i, jj: (i, jj), jj=j)))
            args.append(arr)
        else:
            in_specs.append(pl.BlockSpec((tb, r.shape[1]), lambda i: (i, 0)))
            args.append(r)
    for f in full_in:
        in_specs.append(pl.BlockSpec(f.shape, functools.partial(lambda i, nd: (0,) * nd, nd=f.ndim)))
        args.append(f)
    out_specs = [pl.BlockSpec((tb, c), lambda i: (i, 0)) for c, _ in row_out]
    out_specs += [pl.BlockSpec(s, functools.partial(lambda i, nd: (0,) * nd, nd=len(s))) for s in acc_out]
    out_shape = [jax.ShapeDtypeStruct((t, c), dt) for c, dt in row_out]
    out_shape += [jax.ShapeDtypeStruct(s, F32) for s in acc_out]
    return pl.pallas_call(
        body, name=name, grid=(t // tb,), in_specs=in_specs, out_specs=out_specs, out_shape=out_shape,
        compiler_params=_cp("arbitrary"),
    )(*args)


def _ln_stats(r):
    mu = jnp.mean(r, axis=-1, keepdims=True)
    xc = r - mu
    var = jnp.mean(xc * xc, axis=-1, keepdims=True)
    rstd = lax.rsqrt(var + LN_EPS)
    return xc * rstd, rstd


def _ln_bwd(dout, xhat, rstd, g):
    dxh = dout * g
    m1 = jnp.mean(dxh, axis=-1, keepdims=True)
    m2 = jnp.mean(dxh * xhat, axis=-1, keepdims=True)
    return rstd * (dxh - m1 - xhat * m2)


def _first(i, acc_refs):
    @pl.when(i == 0)
    def _():
        for a in acc_refs:
            a[...] = jnp.zeros_like(a)


def _ln_in_fwd(x, g, b):
    t = x.shape[0]

    def body(x_ref, g_ref, b_ref, o_ref):
        xhat, _ = _ln_stats(x_ref[...])
        o_ref[...] = xhat * g_ref[...] + b_ref[...]

    return _rows(body, "ln_in_fwd", t, 256, [x], [g, b], [(D, F32)], [])[0]


def _ln_in_bwd(dx0, x, g):
    t = x.shape[0]

    def body(d_ref, x_ref, g_ref, o_ref, acc_ref):
        _first(pl.program_id(0), [acc_ref])
        xhat, rstd = _ln_stats(x_ref[...])
        d = d_ref[...]
        o_ref[...] = _ln_bwd(d, xhat, rstd, g_ref[...])
        acc_ref[0:1, :] += jnp.sum(d * xhat, axis=0, keepdims=True)
        acc_ref[1:2, :] += jnp.sum(d, axis=0, keepdims=True)

    return _rows(body, "ln_in_bwd", t, 256, [dx0, x], [g], [(D, F32)], [(2, D)])


def _modulate(xin, mod3, name):
    t = xin.shape[0]

    def body(x_ref, m_ref, o_ref):
        o_ref[...] = (x_ref[...] * (1.0 + m_ref[1:2, :]) + m_ref[0:1, :]).astype(BF16)

    return _rows(body, name, t, 256, [xin], [mod3], [(D, BF16)], [])[0]


def _modulate_bwd(dxres, dh, xin, mod3, name):
    t = xin.shape[0]

    def body(r_ref, dh_ref, x_ref, m_ref, o_ref, acc_ref):
        _first(pl.program_id(0), [acc_ref])
        dh_v = dh_ref[...]
        o_ref[...] = r_ref[...] + dh_v * (1.0 + m_ref[1:2, :])
        acc_ref[0:1, :] += jnp.sum(dh_v, axis=0, keepdims=True)
        acc_ref[1:2, :] += jnp.sum(dh_v * x_ref[...], axis=0, keepdims=True)

    return _rows(body, name, t, 256, [dxres, dh, xin], [mod3], [(D, F32)], [(2, D)])


def _swiglu(u, name):
    t = u.shape[0]

    def body(g_ref, u_ref, o_ref):
        gate = g_ref[...]
        o_ref[...] = (gate * _sigmoid(gate) * u_ref[...]).astype(BF16)

    return _rows(body, name, t, 128, [(u, D_FF, 0), (u, D_FF, 1)], [], [(D_FF, BF16)], [])[0]


def _swiglu_bwd(u, da, name):
    t = u.shape[0]

    def body(g_ref, u_ref, da_ref, o_ref):
        gate, up, d = g_ref[...], u_ref[...], da_ref[...]
        sg = _sigmoid(gate)
        o_ref[:, 0:D_FF] = (d * up * (sg * (1.0 + gate * (1.0 - sg)))).astype(BF16)
        o_ref[:, D_FF:2 * D_FF] = (d * gate * sg).astype(BF16)

    return _rows(body, name, t, 128, [(u, D_FF, 0), (u, D_FF, 1), da], [], [(2 * D_FF, BF16)], [])[0]


def _res_ln(xin, y, mod3, lg, lb, factor, name):
    t = xin.shape[0]

    def body(x_ref, y_ref, m_ref, g_ref, b_ref, o_ref):
        r = ALPHA * x_ref[...] + (factor * m_ref[2:3, :]) * y_ref[...]
        xhat, _ = _ln_stats(r)
        o_ref[...] = xhat * g_ref[...] + b_ref[...]

    return _rows(body, name, t, 256, [xin, y], [mod3, lg, lb], [(D, F32)], [])[0]


def _res_ln_bwd(dout, xin, y, mod3, lg, factor, name):
    t = xin.shape[0]

    def body(d_ref, x_ref, y_ref, m_ref, g_ref, dres_ref, dy_ref, acc_ref):
        _first(pl.program_id(0), [acc_ref])
        gate = factor * m_ref[2:3, :]
        yv = y_ref[...]
        r = ALPHA * x_ref[...] + gate * yv
        xhat, rstd = _ln_stats(r)
        d = d_ref[...]
        dr = _ln_bwd(d, xhat, rstd, g_ref[...])
        dres_ref[...] = ALPHA * dr
        dy_ref[...] = (gate * dr).astype(BF16)
        acc_ref[0:1, :] += jnp.sum(d * xhat, axis=0, keepdims=True)
        acc_ref[1:2, :] += jnp.sum(d, axis=0, keepdims=True)
        acc_ref[2:3, :] += jnp.sum(factor * yv * dr, axis=0, keepdims=True)

    return _rows(body, name, t, 256, [dout, xin, y], [mod3, lg], [(D, F32), (D, BF16)], [(3, D)])


def _loss_head(xf, tgt):
    t = xf.shape[0]

    def body(x_ref, t_ref, d_ref, acc_ref):
        _first(pl.program_id(0), [acc_ref])
        e = x_ref[...] - t_ref[...]
        d_ref[...] = e * (1.0 / D)
        part = 0.5 * jnp.sum(jnp.mean(e * e, axis=-1, keepdims=True), axis=0, keepdims=True)
        acc_ref[...] += jnp.broadcast_to(part, acc_ref.shape)

    return _rows(body, "loss_head", t, 256, [xf, tgt], [], [(D, F32)], [(1, LANES)])


def _silu_bf16(c_all):
    def body(c_ref, o_ref):
        v = c_ref[...]
        o_ref[...] = (v * _sigmoid(v)).astype(BF16)

    return _rows(body, "silu_c", c_all.shape[0], c_all.shape[0], [c_all], [], [(c_all.shape[1], BF16)], [])[0]


def _sum_rows(v, name):
    r, n = v.shape
    tn = _pick(n, (8192, 4096, 2048, 1024, 512, 256, 128))

    def body(v_ref, o_ref):
        acc = v_ref[0:1, :]
        for k in range(1, r):
            acc = acc + v_ref[k:k + 1, :]
        o_ref[...] = acc

    return pl.pallas_call(
        body, name=name, grid=(n // tn,),
        in_specs=[pl.BlockSpec((r, tn), lambda j: (0, j))],
        out_specs=pl.BlockSpec((1, tn), lambda j: (0, j)),
        out_shape=jax.ShapeDtypeStruct((1, n), F32),
        compiler_params=_cp("parallel"),
    )(v)


def _elementwise(fn, name, ins, out_dtypes):
    r, c = ins[0].shape
    tb = _pick(r, (128, 64, 32, 16, 8))
    n_in = len(ins)

    def body(*refs):
        outs = fn(*[x[...] for x in refs[:n_in]])
        for o_ref, o in zip(refs[n_in:], outs):
            o_ref[...] = o.astype(o_ref.dtype)

    spec = pl.BlockSpec((tb, c), lambda i: (i, 0))
    return pl.pallas_call(
        body, name=name, grid=(r // tb,), in_specs=[spec] * n_in, out_specs=[spec] * len(out_dtypes),
        out_shape=[jax.ShapeDtypeStruct((r, c), dt) for dt in out_dtypes],
        compiler_params=_cp("parallel"),
    )(*ins)


def _adamw_math(w, g, m, v):
    m = ADAM_B1 * m + (1.0 - ADAM_B1) * g
    v = ADAM_B2 * v + (1.0 - ADAM_B2) * (g * g)
    m_hat = m / (1.0 - ADAM_B1 ** ADAM_STEP)
    v_hat = v / (1.0 - ADAM_B2 ** ADAM_STEP)
    delta = -ADAM_LR * (m_hat / (jnp.sqrt(v_hat) + ADAM_EPS) + ADAM_WD * w)
    return delta, m, v


def _adamw(w, g, m, v, name):
    shape = w.shape
    c = shape[-1]
    w2, g2, m2, v2 = (a.reshape(-1, c) for a in (w, g, m, v))
    outs = _elementwise(_adamw_math, name, [w2, g2, m2, v2], [F32, F32, F32])
    return tuple(o.reshape(shape) for o in outs)


def _remote_exchange(ins, plan, peers_of, name):
    n_in, n_out = len(ins), len(plan)

    def body(*refs):
        in_refs, out_refs = refs[:n_in], refs[n_in:n_in + n_out]
        send_sems, recv_sems = refs[n_in + n_out], refs[n_in + n_out + 1]
        peers = peers_of(lax.axis_index("x"), lax.axis_index("y"), lax.axis_index("c"))
        copies = [
            pltpu.make_async_remote_copy(
                src_ref=in_refs[src], dst_ref=out_refs[k], send_sem=send_sems.at[k], recv_sem=recv_sems.at[k],
                device_id=peers[peer], device_id_type=MESH)
            for k, (peer, src) in enumerate(plan)
        ]
        for cp in copies:
            cp.start()
        for cp in copies:
            cp.wait()

    any_spec = pl.BlockSpec(memory_space=pl.ANY)
    return list(pl.pallas_call(
        body, name=name,
        in_specs=[any_spec] * n_in, out_specs=[any_spec] * n_out,
        out_shape=[jax.ShapeDtypeStruct(ins[src].shape, ins[src].dtype) for _, src in plan],
        scratch_shapes=[pltpu.SemaphoreType.DMA((n_out,)), pltpu.SemaphoreType.DMA((n_out,))],
    )(*ins))


def _sibling(x, y, c):
    return [(x, y, 1 - c)]


def _other_chips(x, y, c):
    return [(1 - x, y, c), (x, 1 - y, c), (1 - x, 1 - y, c)]


def _swap_sibling(arrs, name):
    return _remote_exchange(arrs, [(0, i) for i in range(len(arrs))], _sibling, name)


def _bcast_chips(arrs, name):
    n = len(arrs)
    out = _remote_exchange(arrs, [(p, i) for p in range(3) for i in range(n)], _other_chips, name)
    return [out[p * n:(p + 1) * n] for p in range(3)]


def _scatter_chips(send3, name):
    n = len(send3[0])
    flat = [a for p in range(3) for a in send3[p]]
    out = _remote_exchange(flat, [(p, p * n + i) for p in range(3) for i in range(n)], _other_chips, name)
    return [out[p * n:(p + 1) * n] for p in range(3)]


def _by_chip(me, own, got3):
    by_rel = [own, got3[0], got3[1], got3[2]]
    rel_bits = (0, 2, 1, 3)

    def branch(m):
        def f(ops):
            return [ops[rel_bits.index(i ^ m)] for i in range(4)]
        return f

    return lax.switch(me, [branch(m) for m in range(4)], by_rel)


def _shift_down(v, s, t_iota):
    return jnp.where(t_iota >= s, pltpu.roll(v, s, 0), 0.0)


def _shift_up(v, s, t_iota, t):
    return jnp.where(t_iota < t - s, pltpu.roll(v, t - s, 0), 0.0)


def _ssd_conv_fwd(proj, w, b):
    t = proj.shape[0]
    k_w = SSD_CONV_K

    def body(x_ref, w_ref, b_ref, o_ref):
        x = x_ref[...]
        ti = lax.broadcasted_iota(jnp.int32, x.shape, 0)
        pre = x * w_ref[k_w - 1:k_w, :] + b_ref[...]
        for s in range(1, k_w):
            pre = pre + _shift_down(x, s, ti) * w_ref[k_w - 1 - s:k_w - s, :]
        o_ref[...] = pre * _sigmoid(pre)

    off = P_XBC // LANES
    return pl.pallas_call(
        body, name="ssd_conv_fwd", grid=(SSD_CONV_DIM // LANES,),
        in_specs=[pl.BlockSpec((t, LANES), lambda j: (0, off + j)),
                  pl.BlockSpec((k_w, LANES), lambda j: (0, j)),
                  pl.BlockSpec((1, LANES), lambda j: (0, j))],
        out_specs=pl.BlockSpec((t, LANES), lambda j: (0, j)),
        out_shape=jax.ShapeDtypeStruct((t, SSD_CONV_DIM), F32),
        compiler_params=_cp("parallel"),
    )(proj, w, b)


def _ssd_conv_bwd(dxbc, proj, w, b):
    t = proj.shape[0]
    k_w = SSD_CONV_K

    def body(d_ref, x_ref, w_ref, b_ref, dx_ref, dw_ref, db_ref):
        x = x_ref[...]
        ti = lax.broadcasted_iota(jnp.int32, x.shape, 0)
        shifted = [x] + [_shift_down(x, s, ti) for s in range(1, k_w)]
        pre = b_ref[...] + shifted[0] * w_ref[k_w - 1:k_w, :]
        for s in range(1, k_w):
            pre = pre + shifted[s] * w_ref[k_w - 1 - s:k_w - s, :]
        sg = _sigmoid(pre)
        dpre = d_ref[...] * (sg * (1.0 + pre * (1.0 - sg)))
        db_ref[...] = jnp.sum(dpre, axis=0, keepdims=True)
        dx = dpre * w_ref[k_w - 1:k_w, :]
        for s in range(k_w):
            dw_ref[k_w - 1 - s:k_w - s, :] = jnp.sum(dpre * shifted[s], axis=0, keepdims=True)
            if s:
                dx = dx + _shift_up(dpre, s, ti, t) * w_ref[k_w - 1 - s:k_w - s, :]
        dx_ref[...] = dx

    off = P_XBC // LANES
    return pl.pallas_call(
        body, name="ssd_conv_bwd", grid=(SSD_CONV_DIM // LANES,),
        in_specs=[pl.BlockSpec((t, LANES), lambda j: (0, j)),
                  pl.BlockSpec((t, LANES), lambda j: (0, off + j)),
                  pl.BlockSpec((k_w, LANES), lambda j: (0, j)),
                  pl.BlockSpec((1, LANES), lambda j: (0, j))],
        out_specs=[pl.BlockSpec((t, LANES), lambda j: (0, j)),
                   pl.BlockSpec((k_w, LANES), lambda j: (0, j)),
                   pl.BlockSpec((1, LANES), lambda j: (0, j))],
        out_shape=[jax.ShapeDtypeStruct((t, SSD_CONV_DIM), F32),
                   jax.ShapeDtypeStruct((k_w, SSD_CONV_DIM), F32),
                   jax.ShapeDtypeStruct((1, SSD_CONV_DIM), F32)],
        compiler_params=_cp("parallel"),
    )(dxbc, proj, w, b)


def _shortconv_fwd(proj, w):
    t = proj.shape[0]
    nb = SC_W // LANES
    off = P_SC // LANES

    def body(b_ref, c_ref, x_ref, w_ref, o_ref):
        u = c_ref[...] * x_ref[...]
        ti = lax.broadcasted_iota(jnp.int32, u.shape, 0)
        cv = u * w_ref[SC_K - 1:SC_K, :]
        for s in range(1, SC_K):
            cv = cv + _shift_down(u, s, ti) * w_ref[SC_K - 1 - s:SC_K - s, :]
        o_ref[...] = (b_ref[...] * cv).astype(BF16)

    return pl.pallas_call(
        body, name="shortconv_fwd", grid=(nb,),
        in_specs=[pl.BlockSpec((t, LANES), lambda j: (0, off + j)),
                  pl.BlockSpec((t, LANES), lambda j: (0, off + nb + j)),
                  pl.BlockSpec((t, LANES), lambda j: (0, off + 2 * nb + j)),
                  pl.BlockSpec((SC_K, LANES), lambda j: (0, j))],
        out_specs=pl.BlockSpec((t, LANES), lambda j: (0, j)),
        out_shape=jax.ShapeDtypeStruct((t, SC_W), BF16),
        compiler_params=_cp("parallel"),
    )(proj, proj, proj, w)


def _shortconv_bwd(dy, dy_off, proj, w):
    t = proj.shape[0]
    nb = SC_W // LANES
    off = P_SC // LANES
    doff = dy_off // LANES

    def body(d_ref, b_ref, c_ref, x_ref, w_ref, db_ref, dc_ref, dx_ref, dw_ref):
        cg, xin = c_ref[...], x_ref[...]
        u = cg * xin
        ti = lax.broadcasted_iota(jnp.int32, u.shape, 0)
        shifted = [u] + [_shift_down(u, s, ti) for s in range(1, SC_K)]
        cv = shifted[0] * w_ref[SC_K - 1:SC_K, :]
        for s in range(1, SC_K):
            cv = cv + shifted[s] * w_ref[SC_K - 1 - s:SC_K - s, :]
        d = d_ref[...]
        db_ref[...] = (d * cv).astype(BF16)
        dcv = d * b_ref[...]
        du = dcv * w_ref[SC_K - 1:SC_K, :]
        for s in range(SC_K):
            dw_ref[SC_K - 1 - s:SC_K - s, :] = jnp.sum(dcv * shifted[s], axis=0, keepdims=True)
            if s:
                du = du + _shift_up(dcv, s, ti, t) * w_ref[SC_K - 1 - s:SC_K - s, :]
        dc_ref[...] = (du * xin).astype(BF16)
        dx_ref[...] = (du * cg).astype(BF16)

    tile = pl.BlockSpec((t, LANES), lambda j: (0, j))
    outs = pl.pallas_call(
        body, name="shortconv_bwd", grid=(nb,),
        in_specs=[pl.BlockSpec((t, LANES), lambda j: (0, doff + j)),
                  pl.BlockSpec((t, LANES), lambda j: (0, off + j)),
                  pl.BlockSpec((t, LANES), lambda j: (0, off + nb + j)),
                  pl.BlockSpec((t, LANES), lambda j: (0, off + 2 * nb + j)),
                  pl.BlockSpec((SC_K, LANES), lambda j: (0, j))],
        out_specs=[tile, tile, tile, pl.BlockSpec((SC_K, LANES), lambda j: (0, j))],
        out_shape=[jax.ShapeDtypeStruct((t, SC_W), BF16)] * 3 + [jax.ShapeDtypeStruct((SC_K, SC_W), F32)],
        compiler_params=_cp("parallel"),
    )(dy, proj, proj, proj, w)
    return outs


def _cum_logf(proj, fbias_row):
    t = proj.shape[0]
    blk = CUM_BLOCK

    def body(p_ref, b_ref, o_ref, carry_ref):
        i = pl.program_id(0)

        @pl.when(i == 0)
        def _():
            carry_ref[...] = jnp.zeros_like(carry_ref)

        lf = -_softplus(-(p_ref[...] + b_ref[...]))
        r = lax.broadcasted_iota(jnp.int32, (blk, blk), 0)
        c = lax.broadcasted_iota(jnp.int32, (blk, blk), 1)
        tri = (r >= c).astype(BF16)
        o_ref[...] = _tri_left(tri, lf) + carry_ref[...]
        carry_ref[...] = o_ref[blk - 1:blk, :]

    return pl.pallas_call(
        body, name="cum_logf", grid=(t // blk,),
        in_specs=[pl.BlockSpec((blk, LANES), lambda i: (i, P_PAD // LANES)),
                  pl.BlockSpec((1, LANES), lambda i: (0, 0))],
        out_specs=pl.BlockSpec((blk, LANES), lambda i: (i, 0)),
        out_shape=jax.ShapeDtypeStruct((t, LANES), F32),
        scratch_shapes=[pltpu.VMEM((1, LANES), F32)],
        compiler_params=_cp("arbitrary"),
    )(proj, fbias_row)


def _pad_block_bwd(dcf, ddt, proj, fbias_row):
    t = proj.shape[0]
    blk = CUM_BLOCK
    nb = t // blk

    def body(dcf_ref, ddt_ref, p_ref, b_ref, o_ref, db_ref, carry_ref):
        i = pl.program_id(0)

        @pl.when(i == 0)
        def _():
            carry_ref[...] = jnp.zeros_like(carry_ref)
            db_ref[...] = jnp.zeros_like(db_ref)

        r = lax.broadcasted_iota(jnp.int32, (blk, blk), 0)
        c = lax.broadcasted_iota(jnp.int32, (blk, blk), 1)
        tri = (r <= c).astype(BF16)
        rev = _tri_left(tri, dcf_ref[...]) + carry_ref[...]
        carry_ref[...] = jnp.sum(dcf_ref[...], axis=0, keepdims=True) + carry_ref[...]
        lane = lax.broadcasted_iota(jnp.int32, (blk, LANES), 1)
        is_f = (lane >= PAD_F0) & (lane < PAD_F0 + FOX_HEADS)
        df = jnp.where(is_f, rev * _sigmoid(-(p_ref[...] + b_ref[...])), 0.0)
        db_ref[...] += jnp.sum(df, axis=0, keepdims=True)
        o_ref[...] = jnp.where(lane < PAD_DT0 + SSD_HEADS, ddt_ref[...], df).astype(BF16)

    return pl.pallas_call(
        body, name="pad_block_bwd", grid=(nb,),
        in_specs=[pl.BlockSpec((blk, LANES), lambda i: (nb - 1 - i, 0)),
                  pl.BlockSpec((blk, LANES), lambda i: (nb - 1 - i, 0)),
                  pl.BlockSpec((blk, LANES), lambda i: (nb - 1 - i, P_PAD // LANES)),
                  pl.BlockSpec((1, LANES), lambda i: (0, 0))],
        out_specs=[pl.BlockSpec((blk, LANES), lambda i: (nb - 1 - i, 0)),
                   pl.BlockSpec((1, LANES), lambda i: (0, 0))],
        out_shape=[jax.ShapeDtypeStruct((t, LANES), BF16), jax.ShapeDtypeStruct((1, LANES), F32)],
        scratch_shapes=[pltpu.VMEM((1, LANES), F32)],
        compiler_params=_cp("arbitrary"),
    )(dcf, ddt, proj, fbias_row)


def _att_scores(q, k, cq, ck, diagonal, blk):
    s = _dot_nt(q, k) * (FOX_HD ** -0.5) + cq - ck
    if diagonal:
        r = lax.broadcasted_iota(jnp.int32, (blk, blk), 0)
        c = lax.broadcasted_iota(jnp.int32, (blk, blk), 1)
        s = jnp.where(r >= c, s, NEG)
    return s


def _fox_fwd(q, k, v, cf_col, cf_row):
    h, t, hd = q.shape
    blk = min(ATT_BLOCK, t)
    nb = t // blk

    def body(q_ref, k_ref, v_ref, cq_ref, ck_ref, o_ref, lse_ref):
        qi = pl.program_id(1)
        qv, cq = q_ref[...], cq_ref[...]

        def step(j, carry, diagonal):
            m, l, acc = carry
            off = pl.multiple_of(j * blk, blk)
            s = _att_scores(qv, k_ref[pl.ds(off, blk), :], cq, ck_ref[:, pl.ds(off, blk)], diagonal, blk)
            m_new = jnp.maximum(m, jnp.max(s, axis=1, keepdims=True))
            alpha = jnp.exp(m - m_new)
            p = jnp.exp(s - m_new)
            l = alpha * l + jnp.sum(p, axis=1, keepdims=True)
            acc = alpha * acc + _dot(p.astype(BF16), v_ref[pl.ds(off, blk), :])
            return m_new, l, acc

        init = (jnp.full((blk, 1), NEG, F32), jnp.zeros((blk, 1), F32), jnp.zeros((blk, hd), F32))
        carry = lax.fori_loop(0, qi, lambda j, cr: step(j, cr, False), init)
        m, l, acc = step(qi, carry, True)
        o_ref[...] = acc / l
        lse_ref[...] = m + jnp.log(l)

    qmap = lambda hh, i: (hh, i, 0)
    whole = lambda hh, i: (hh, 0, 0)
    return pl.pallas_call(
        body, name="fox_fwd", grid=(h, nb),
        in_specs=[pl.BlockSpec((None, blk, hd), qmap), pl.BlockSpec((None, t, hd), whole),
                  pl.BlockSpec((None, t, hd), whole), pl.BlockSpec((None, blk, 1), qmap),
                  pl.BlockSpec((None, 1, t), whole)],
        out_specs=[pl.BlockSpec((None, blk, hd), qmap), pl.BlockSpec((None, blk, 1), qmap)],
        out_shape=[jax.ShapeDtypeStruct((h, t, hd), F32), jax.ShapeDtypeStruct((h, t, 1), F32)],
        compiler_params=_cp("parallel", "arbitrary"),
    )(q, k, v, cf_col, cf_row)


def _fox_bwd(q, k, v, cf_col, cf_row, o, lse, do):
    h, t, hd = q.shape
    blk = min(ATT_BLOCK, t)
    nb = t // blk
    scale = FOX_HD ** -0.5

    def body(q_ref, k_ref, v_ref, cq_ref, ck_ref, o_ref, lse_ref, do_ref,
             dq_ref, dk_ref, dv_ref, dcq_ref, dck_ref, delta_s):
        kj = pl.program_id(1)

        @pl.when(kj == 0)
        def _():
            dq_ref[...] = jnp.zeros_like(dq_ref)
            dcq_ref[...] = jnp.zeros_like(dcq_ref)

            def fill(i, _):
                rows = pl.ds(pl.multiple_of(i * blk, blk), blk)
                delta_s[rows, :] = jnp.sum(do_ref[rows, :] * o_ref[rows, :], axis=1, keepdims=True)
                return 0

            lax.fori_loop(0, nb, fill, 0)

        kb, vb, ck = k_ref[...], v_ref[...], ck_ref[...]

        def step(i, carry, diagonal):
            dk, dv, dck = carry
            rows = pl.ds(pl.multiple_of(i * blk, blk), blk)
            qb = q_ref[rows, :]
            do_b = do_ref[rows, :].astype(BF16)
            s = _att_scores(qb, kb, cq_ref[rows, :], ck, diagonal, blk)
            p = jnp.exp(s - lse_ref[rows, :])
            dv = dv + _dot_tn(p.astype(BF16), do_b)
            ds = p * (_dot_nt(do_b, vb) - delta_s[rows, :])
            ds_b = ds.astype(BF16)
            dk = dk + _dot_tn(ds_b, qb)
            dq_ref[rows, :] += _dot(ds_b, kb) * scale
            dcq_ref[rows, :] += jnp.sum(ds, axis=1, keepdims=True)
            dck = dck - jnp.sum(ds, axis=0, keepdims=True)
            return dk, dv, dck

        init = (jnp.zeros((blk, hd), F32), jnp.zeros((blk, hd), F32), jnp.zeros((1, blk), F32))
        carry = step(kj, init, True)
        dk, dv, dck = lax.fori_loop(kj + 1, nb, lambda i, cr: step(i, cr, False), carry)
        dk_ref[...] = dk * scale
        dv_ref[...] = dv
        dck_ref[...] = dck

    kmap = lambda hh, j: (hh, j, 0)
    whole = lambda hh, j: (hh, 0, 0)
    return pl.pallas_call(
        body, name="fox_bwd", grid=(h, nb),
        in_specs=[pl.BlockSpec((None, t, hd), whole), pl.BlockSpec((None, blk, hd), kmap),
                  pl.BlockSpec((None, blk, hd), kmap), pl.BlockSpec((None, t, 1), whole),
                  pl.BlockSpec((None, 1, blk), lambda hh, j: (hh, 0, j)),
                  pl.BlockSpec((None, t, hd), whole), pl.BlockSpec((None, t, 1), whole),
                  pl.BlockSpec((None, t, hd), whole)],
        out_specs=[pl.BlockSpec((None, t, hd), whole), pl.BlockSpec((None, blk, hd), kmap),
                   pl.BlockSpec((None, blk, hd), kmap), pl.BlockSpec((None, t, 1), whole),
                   pl.BlockSpec((None, 1, blk), lambda hh, j: (hh, 0, j))],
        out_shape=[jax.ShapeDtypeStruct((h, t, hd), F32), jax.ShapeDtypeStruct((h, t, hd), F32),
                   jax.ShapeDtypeStruct((h, t, hd), F32), jax.ShapeDtypeStruct((h, t, 1), F32),
                   jax.ShapeDtypeStruct((h, 1, t), F32)],
        scratch_shapes=[pltpu.VMEM((t, 1), F32)],
        compiler_params=_cp("parallel", "arbitrary"),
    )(q, k, v, cf_col, cf_row, o, lse, do)


def _lane_col(v, h):
    lane = lax.broadcasted_iota(jnp.int32, v.shape, 1)
    return jnp.sum(jnp.where(lane == h, v, 0.0), axis=1, keepdims=True)


def _sub_row(v, h):
    sub = lax.broadcasted_iota(jnp.int32, v.shape, 0)
    return jnp.sum(jnp.where(sub == h, v, 0.0), axis=0, keepdims=True)


def _ssd_decays(pad, pad_t, dtb_row, alog_row, dtb_col, alog_col, blk):
    r = lax.broadcasted_iota(jnp.int32, (blk, blk), 0)
    c = lax.broadcasted_iota(jnp.int32, (blk, blk), 1)
    tril = r >= c
    dt_c = _softplus(pad + dtb_row)
    acs_c = _tri_left(tril.astype(BF16), dt_c * (-jnp.exp(alog_row)))
    dt_r = _softplus(pad_t + dtb_col)
    acs_r = _tri_right(dt_r * (-jnp.exp(alog_col)), (r <= c).astype(BF16))
    rows = lax.broadcasted_iota(jnp.int32, acs_c.shape, 0)
    acs_last = jnp.sum(jnp.where(rows == blk - 1, acs_c, 0.0), axis=0, keepdims=True)
    return dt_c, acs_c, acs_r, acs_last, tril


def _pair_terms(pair, dt_c, acs_c, acs_r, acs_last, d_row, blk):
    lane = lax.broadcasted_iota(jnp.int32, (blk, LANES), 1)
    lo = lane < 64
    lo_row = lax.broadcasted_iota(jnp.int32, (1, LANES), 1) < 64
    h0, h1 = 2 * pair, 2 * pair + 1
    col = [_lane_col(acs_c, h0), _lane_col(acs_c, h1)]
    row = [_sub_row(acs_r, h0), _sub_row(acs_r, h1)]
    last = [_lane_col(acs_last, h0), _lane_col(acs_last, h1)]
    dt_p = jnp.where(lo, _lane_col(dt_c, h0), _lane_col(dt_c, h1))
    e_p = jnp.where(lo, jnp.exp(col[0]), jnp.exp(col[1]))
    w_p = jnp.where(lo, jnp.exp(last[0] - col[0]), jnp.exp(last[1] - col[1]))
    decay_p = jnp.where(lo_row, jnp.exp(last[0]), jnp.exp(last[1]))
    d_p = jnp.where(lo_row, _lane_col(d_row, h0), _lane_col(d_row, h1))
    return lo, lo_row, col, row, last, dt_p, e_p, w_p, decay_p, d_p


def _ssd_specs(t, blk, rev):
    nc = t // blk
    ix = (lambda i: nc - 1 - i) if rev else (lambda i: i)
    xbc = pl.BlockSpec((blk, SSD_CONV_DIM), lambda i: (ix(i), 0))
    pad = pl.BlockSpec((blk, LANES), lambda i: (ix(i), P_PAD // LANES))
    pad_t = pl.BlockSpec((LANES, blk), lambda i: (0, ix(i)))
    z = pl.BlockSpec((blk, SSD_W), lambda i: (ix(i), 0))
    row = pl.BlockSpec((1, LANES), lambda i: (0, 0))
    colv = pl.BlockSpec((LANES, 1), lambda i: (0, 0))
    ng = pl.BlockSpec((1, SSD_W), lambda i: (0, 0))
    y = pl.BlockSpec((blk, SSD_W), lambda i: (ix(i), 0))
    st = pl.BlockSpec((None, 4, LANES, LANES), lambda i: (ix(i), 0, 0, 0))
    return nc, xbc, pad, pad_t, z, row, colv, ng, y, st


def _ssd_fwd(xbc, proj, pad_t, dtb_row, alog_row, d_row, dtb_col, alog_col, ng):
    t = xbc.shape[0]
    blk = min(SSD_CHUNK, t)
    nc, s_xbc, s_pad, s_padt, s_z, s_row, s_col, s_ng, s_y, s_st = _ssd_specs(t, blk, False)

    def body(xbc_ref, pad_ref, padt_ref, z_ref, dtb_ref, alog_ref, d_ref, dtbc_ref, alogc_ref, ng_ref,
             out_ref, ypre_ref, st_ref, state):
        @pl.when(pl.program_id(0) == 0)
        def _():
            state[...] = jnp.zeros_like(state)

        dt_c, acs_c, acs_r, acs_last, tril = _ssd_decays(
            pad_ref[...], padt_ref[...], dtb_ref[...], alog_ref[...], dtbc_ref[...], alogc_ref[...], blk)
        ys = []
        g_mat = {}
        for pair in range(4):
            g = pair // 2
            bg = xbc_ref[:, SSD_W + LANES * g:SSD_W + LANES * (g + 1)].astype(BF16)
            cg = xbc_ref[:, SSD_W + 2 * LANES + LANES * g:SSD_W + 2 * LANES + LANES * (g + 1)].astype(BF16)
            if g not in g_mat:
                g_mat[g] = _dot_nt(cg, bg)
            xs_p = xbc_ref[:, LANES * pair:LANES * (pair + 1)]
            lo, _, col, row, _, dt_p, e_p, w_p, decay_p, d_p = _pair_terms(
                pair, dt_c, acs_c, acs_r, acs_last, d_ref[...], blk)
            x_p = xs_p * dt_p
            y = None
            for hh in range(2):
                lm = jnp.exp(jnp.where(tril, col[hh] - row[hh], NEG))
                m_h = (g_mat[g] * lm).astype(BF16)
                x_h = jnp.where(lo if hh == 0 else ~lo, x_p, 0.0).astype(BF16)
                y_h = _dot(m_h, x_h)
                y = y_h if y is None else y + y_h
            st_in = state[pair]
            st_ref[pair] = st_in
            y = y + e_p * _dot(cg, st_in.astype(BF16))
            state[pair] = decay_p * st_in + _dot_tn(bg, (x_p * w_p).astype(BF16))
            ys.append(y + d_p * xs_p)
        y_all = jnp.concatenate(ys, axis=1)
        ypre_ref[...] = y_all
        z = z_ref[...]
        y2 = y_all * (z * _sigmoid(z))
        outs = []
        for g in range(2):
            seg = y2[:, 256 * g:256 * (g + 1)]
            rr = lax.rsqrt(jnp.mean(seg * seg, axis=-1, keepdims=True) + RMS_EPS)
            outs.append(seg * rr * ng_ref[:, 256 * g:256 * (g + 1)])
        out_ref[...] = jnp.concatenate(outs, axis=1).astype(BF16)

    return pl.pallas_call(
        body, name="ssd_scan_fwd", grid=(nc,),
        in_specs=[s_xbc, s_pad, s_padt, s_z, s_row, s_row, s_row, s_col, s_col, s_ng],
        out_specs=[s_y, s_y, s_st],
        out_shape=[jax.ShapeDtypeStruct((t, SSD_W), BF16), jax.ShapeDtypeStruct((t, SSD_W), F32),
                   jax.ShapeDtypeStruct((nc, 4, LANES, LANES), F32)],
        scratch_shapes=[pltpu.VMEM((4, LANES, LANES), F32)],
        compiler_params=_cp("arbitrary"),
    )(xbc, proj, pad_t, proj, dtb_row, alog_row, d_row, dtb_col, alog_col, ng)


def _ssd_bwd(dout, dout_off, xbc, proj, pad_t, ypre, states, dtb_row, alog_row, d_row, dtb_col, alog_col, ng):
    t = xbc.shape[0]
    blk = min(SSD_CHUNK, t)
    nc, s_xbc, s_pad, s_padt, s_z, s_row, s_col, s_ng, s_y, s_st = _ssd_specs(t, blk, True)
    s_dout = pl.BlockSpec((blk, SSD_W), lambda i: (nc - 1 - i, dout_off // SSD_W))

    def body(dout_ref, xbc_ref, pad_ref, padt_ref, z_ref, ypre_ref, st_ref, dtb_ref, alog_ref, d_ref,
             dtbc_ref, alogc_ref, ng_ref, dxbc_ref, ddt_ref, dz_ref, acc_ref, dng_ref, dstate):
        @pl.when(pl.program_id(0) == 0)
        def _():
            dstate[...] = jnp.zeros_like(dstate)
            acc_ref[...] = jnp.zeros_like(acc_ref)
            dng_ref[...] = jnp.zeros_like(dng_ref)

        pad = pad_ref[...]
        dt_c, acs_c, acs_r, acs_last, tril = _ssd_decays(
            pad, padt_ref[...], dtb_ref[...], alog_ref[...], dtbc_ref[...], alogc_ref[...], blk)
        a_row = -jnp.exp(alog_ref[...])

        z = z_ref[...]
        sz = _sigmoid(z)
        silu_z = z * sz
        y_pre = ypre_ref[...]
        y2 = y_pre * silu_z
        dy2 = []
        for g in range(2):
            sl = slice(256 * g, 256 * (g + 1))
            seg = y2[:, sl]
            rr = lax.rsqrt(jnp.mean(seg * seg, axis=-1, keepdims=True) + RMS_EPS)
            nrm = seg * rr
            d_seg = dout_ref[:, sl]
            dng_ref[:, sl] += jnp.sum(d_seg * nrm, axis=0, keepdims=True)
            dn = d_seg * ng_ref[:, sl]
            dy2.append(rr * (dn - nrm * jnp.mean(dn * nrm, axis=-1, keepdims=True)))
        dy2 = jnp.concatenate(dy2, axis=1)
        dz_ref[...] = (dy2 * y_pre * (sz * (1.0 + z * (1.0 - sz)))).astype(BF16)
        dy_all = dy2 * silu_z

        lane_row = lax.broadcasted_iota(jnp.int32, (1, LANES), 1)
        lane_blk = lax.broadcasted_iota(jnp.int32, (blk, LANES), 1)
        row_col = lax.broadcasted_iota(jnp.int32, (blk, 1), 0)
        ddt = jnp.zeros((blk, LANES), F32)
        dacs = jnp.zeros((blk, LANES), F32)
        dd_row = jnp.zeros((1, LANES), F32)
        ones_b = jnp.ones((blk, LANES), BF16)
        dxs = []
        d_b = [None, None]
        d_c = [None, None]
        d_g = [None, None]
        bgs, cgs = {}, {}
        g_mat = {}
        for pair in range(4):
            g = pair // 2
            if g not in g_mat:
                bgs[g] = xbc_ref[:, SSD_W + LANES * g:SSD_W + LANES * (g + 1)].astype(BF16)
                cgs[g] = xbc_ref[:, SSD_W + 2 * LANES + LANES * g:SSD_W + 2 * LANES + LANES * (g + 1)].astype(BF16)
                g_mat[g] = _dot_nt(cgs[g], bgs[g])
            bg, cg = bgs[g], cgs[g]
            xs_p = xbc_ref[:, LANES * pair:LANES * (pair + 1)]
            lo, lo_row, col, row, last, dt_p, e_p, w_p, decay_p, d_p = _pair_terms(
                pair, dt_c, acs_c, acs_r, acs_last, d_ref[...], blk)
            x_p = xs_p * dt_p
            dy_p = dy_all[:, LANES * pair:LANES * (pair + 1)]
            st_in = st_ref[pair]
            dst = dstate[pair]
            dx_diag = None
            for hh in range(2):
                sel = lo if hh == 0 else ~lo
                lm = jnp.exp(jnp.where(tril, col[hh] - row[hh], NEG))
                m_f = g_mat[g] * lm
                m_h = m_f.astype(BF16)
                x_h = jnp.where(sel, x_p, 0.0).astype(BF16)
                dy_h = jnp.where(sel, dy_p, 0.0).astype(BF16)
                dxd = _dot_tn(m_h, dy_h)
                dm = _dot_nt(dy_h, x_h)
                dg_h = dm * lm
                p_b = (dm * m_f).astype(BF16)
                dacs = dacs + jnp.where(lane_blk == 2 * pair + hh, _dot(p_b, ones_b) - _dot_tn(p_b, ones_b), 0.0)
                dx_diag = dxd if dx_diag is None else dx_diag + dxd
                d_g[g] = dg_h if d_g[g] is None else d_g[g] + dg_h
            st_b = st_in.astype(BF16)
            dst_b = dst.astype(BF16)
            y_off = e_p * _dot(cg, st_b)
            edy = (e_p * dy_p).astype(BF16)
            dc_off = _dot_nt(edy, st_b)
            d_c[g] = dc_off if d_c[g] is None else d_c[g] + dc_off
            dstate[pair] = decay_p * dst + _dot_tn(cg, edy)
            dx_state = _dot(bg, dst_b) * w_p
            db_st = _dot_nt((x_p * w_p).astype(BF16), dst_b)
            d_b[g] = db_st if d_b[g] is None else d_b[g] + db_st
            dx = dx_diag + dx_state
            dxs.append(dx * dt_p + d_p * dy_p)
            prod_dt = dx * xs_p
            prod_acs = dy_p * y_off - x_p * dx_state
            prod_st = x_p * dx_state
            prod_d = dy_p * xs_p
            st_prod = jnp.sum(dst * st_in, axis=0, keepdims=True)
            for hh in range(2):
                h = 2 * pair + hh
                sel = lo if hh == 0 else ~lo
                sel_row = lo_row if hh == 0 else ~lo_row
                ddt_h = jnp.sum(jnp.where(sel, prod_dt, 0.0), axis=1, keepdims=True)
                dacs_h = jnp.sum(jnp.where(sel, prod_acs, 0.0), axis=1, keepdims=True)
                tail = jnp.sum(jnp.sum(jnp.where(sel, prod_st, 0.0), axis=1, keepdims=True), axis=0, keepdims=True)
                tail = tail + jnp.exp(last[hh]) * jnp.sum(jnp.where(sel_row, st_prod, 0.0), axis=1, keepdims=True)
                dacs_h = dacs_h + jnp.where(row_col == blk - 1, tail, 0.0)
                dd_h = jnp.sum(jnp.sum(jnp.where(sel, prod_d, 0.0), axis=1, keepdims=True), axis=0, keepdims=True)
                ddt = ddt + jnp.where(lane_blk == h, ddt_h, 0.0)
                dacs = dacs + jnp.where(lane_blk == h, dacs_h, 0.0)
                dd_row = dd_row + jnp.where(lane_row == h, dd_h, 0.0)
        for g in range(2):
            dg_b = d_g[g].astype(BF16)
            d_c[g] = d_c[g] + _dot(dg_b, bgs[g])
            d_b[g] = d_b[g] + _dot_tn(dg_b, cgs[g])
        r = lax.broadcasted_iota(jnp.int32, (blk, blk), 0)
        c = lax.broadcasted_iota(jnp.int32, (blk, blk), 1)
        da = _tri_left((r <= c).astype(BF16), dacs)
        ddt = ddt + da * a_row
        d_raw = ddt * _sigmoid(pad + dtb_ref[...])
        ddt_ref[...] = d_raw
        acc_ref[0:1, :] += jnp.sum(da * dt_c, axis=0, keepdims=True) * a_row
        acc_ref[1:2, :] += dd_row
        acc_ref[2:3, :] += jnp.sum(d_raw, axis=0, keepdims=True)
        dxbc_ref[...] = jnp.concatenate(dxs + d_b + d_c, axis=1)

    return pl.pallas_call(
        body, name="ssd_scan_bwd", grid=(nc,),
        in_specs=[s_dout, s_xbc, s_pad, s_padt, s_z, s_y, s_st, s_row, s_row, s_row, s_col, s_col, s_ng],
        out_specs=[s_xbc, pl.BlockSpec((blk, LANES), lambda i: (nc - 1 - i, 0)), s_y,
                   pl.BlockSpec((8, LANES), lambda i: (0, 0)), s_ng],
        out_shape=[jax.ShapeDtypeStruct((t, SSD_CONV_DIM), F32), jax.ShapeDtypeStruct((t, LANES), F32),
                   jax.ShapeDtypeStruct((t, SSD_W), BF16), jax.ShapeDtypeStruct((8, LANES), F32),
                   jax.ShapeDtypeStruct((1, SSD_W), F32)],
        scratch_shapes=[pltpu.VMEM((4, LANES, LANES), F32)],
        compiler_params=_cp("arbitrary"),
    )(dout, xbc, proj, pad_t, proj, ypre, states, dtb_row, alog_row, d_row, dtb_col, alog_col, ng)


def _pad_lanes(v, off):
    return jnp.zeros((1, LANES), F32).at[0, off:off + v.shape[0]].set(v)


def _perm_mix_w_in(w):
    z, xbc, dt = w[:, 0:512], w[:, 512:1536], w[:, 1536:1544]
    qkv, f, sc = w[:, 1544:2312], w[:, 2312:2316], w[:, 2316:3084]
    padblk = jnp.zeros((w.shape[0], LANES), w.dtype).at[:, PAD_DT0:PAD_DT0 + 8].set(dt).at[:, PAD_F0:PAD_F0 + 4].set(f)
    return jnp.concatenate([z, xbc, qkv, sc, padblk], axis=1)


def _unperm_mix_w_in(wp):
    z, xbc, qkv, sc = wp[:, 0:512], wp[:, 512:1536], wp[:, 1536:2304], wp[:, 2304:3072]
    dt, f = wp[:, P_PAD + PAD_DT0:P_PAD + PAD_DT0 + 8], wp[:, P_PAD + PAD_F0:P_PAD + PAD_F0 + 4]
    return jnp.concatenate([z, xbc, dt, qkv, f, sc], axis=1)


def _heads(m):
    return jnp.transpose(m.reshape(m.shape[0], FOX_HEADS, FOX_HD), (1, 0, 2))


def _unheads(m):
    return jnp.transpose(m, (1, 0, 2)).reshape(m.shape[1], FOX_W)


def _ffn_fwd(xin, mod3, w_in, w_out, lg, lb, tag):
    h = _modulate(xin, mod3, f"mod_{tag}")
    u = _matmul(h, w_in, "nn", F32, f"ffn_in_{tag}")
    a = _swiglu(u, f"swiglu_{tag}")
    y = _matmul(a, w_out, "nn", F32, f"ffn_out_{tag}")
    xout = _res_ln(xin, y, mod3, lg, lb, 0.5, f"res_ln_{tag}")
    return xout, (xin, h, u, a, y)


def _ffn_bwd(dout, saved, mod3, w_in, w_out, lg, tag):
    xin, h, u, a, y = saved
    dres, dy, acc_ln = _res_ln_bwd(dout, xin, y, mod3, lg, 0.5, f"res_ln_bwd_{tag}")
    da = _matmul(dy, w_out, "nt", F32, f"ffn_out_dx_{tag}")
    dw_out = _matmul(a, dy, "tn", BF16, f"ffn_out_dw_{tag}")
    du = _swiglu_bwd(u, da, f"swiglu_bwd_{tag}")
    dh = _matmul(du, w_in, "nt", F32, f"ffn_in_dx_{tag}")
    dw_in = _matmul(h, du, "tn", BF16, f"ffn_in_dw_{tag}")
    dxin, acc_mod = _modulate_bwd(dres, dh, xin, mod3, f"mod_bwd_{tag}")
    dmod3 = jnp.concatenate([acc_mod, acc_ln[2:3]], axis=0)
    return dxin, dw_in, dw_out, acc_ln[0], acc_ln[1], dmod3


def _mix_fwd(xin, mod3, wp, w_out, lg, lb, sp, tag):
    h = _modulate(xin, mod3, f"mod_{tag}")
    proj = _matmul(h, wp, "nn", F32, f"mix_in_{tag}")
    pad_t = jnp.transpose(proj[:, P_PAD:P_PAD + LANES])
    xbc = _ssd_conv_fwd(proj, sp["conv_w"], sp["conv_b"])
    y_ssd, ypre, states = _ssd_fwd(xbc, proj, pad_t, sp["dtb_row"], sp["alog_row"], sp["d_row"],
                                   sp["dtb_col"], sp["alog_col"], sp["ng"])
    cf = _cum_logf(proj, sp["fbias_row"])
    cf4 = jnp.transpose(cf[:, PAD_F0:PAD_F0 + FOX_HEADS])
    cf_col, cf_row = cf4[:, :, None], cf4[:, None, :]
    q = _heads(proj[:, P_QKV:P_QKV + 256].astype(BF16))
    k = _heads(proj[:, P_QKV + 256:P_QKV + 512].astype(BF16))
    v = _heads(proj[:, P_QKV + 512:P_QKV + 768].astype(BF16))
    o, lse = _fox_fwd(q, k, v, cf_col, cf_row)
    y_sc = _shortconv_fwd(proj, sp["sconv_w"])
    ymix = jnp.concatenate([y_ssd, _unheads(o).astype(BF16), y_sc], axis=1)
    y = _matmul(ymix, w_out, "nn", F32, f"mix_out_{tag}")
    xout = _res_ln(xin, y, mod3, lg, lb, 1.0, f"res_ln_{tag}")
    return xout, (xin, h, proj, pad_t, xbc, ypre, states, q, k, v, cf_col, cf_row, o, lse, ymix, y)


def _mix_bwd(dout, saved, mod3, wp, w_out, lg, sp, tag):
    xin, h, proj, pad_t, xbc, ypre, states, q, k, v, cf_col, cf_row, o, lse, ymix, y = saved
    dres, dy, acc_ln = _res_ln_bwd(dout, xin, y, mod3, lg, 1.0, f"res_ln_bwd_{tag}")
    dymix = _matmul(dy, w_out, "nt", F32, f"mix_out_dx_{tag}")
    dw_out = _matmul(ymix, dy, "tn", BF16, f"mix_out_dw_{tag}")
    dxbc, ddt, dz, ssd_acc, dng = _ssd_bwd(dymix, 0, xbc, proj, pad_t, ypre, states, sp["dtb_row"],
                                           sp["alog_row"], sp["d_row"], sp["dtb_col"], sp["alog_col"], sp["ng"])
    dxbc_raw, dconv_w, dconv_b = _ssd_conv_bwd(dxbc, proj, sp["conv_w"], sp["conv_b"])
    do = _heads(dymix[:, SSD_W:SSD_W + FOX_W])
    dq, dk, dv, dcq, dck = _fox_bwd(q, k, v, cf_col, cf_row, o, lse, do)
    dcf4 = dcq[:, :, 0] + dck[:, 0, :]
    dcf = jnp.zeros((xin.shape[0], LANES), F32).at[:, PAD_F0:PAD_F0 + FOX_HEADS].set(jnp.transpose(dcf4))
    dpad, dfb = _pad_block_bwd(dcf, ddt, proj, sp["fbias_row"])
    dscb, dscc, dscx, dsconv_w = _shortconv_bwd(dymix, SSD_W + FOX_W, proj, sp["sconv_w"])
    dproj = jnp.concatenate([dz, dxbc_raw.astype(BF16), _unheads(dq).astype(BF16), _unheads(dk).astype(BF16),
                             _unheads(dv).astype(BF16), dscb, dscc, dscx, dpad], axis=1)
    dh = _matmul(dproj, wp, "nt", F32, f"mix_in_dx_{tag}")
    dwp = _matmul(h, dproj, "tn", BF16, f"mix_in_dw_{tag}")
    dxin, acc_mod = _modulate_bwd(dres, dh, xin, mod3, f"mod_bwd_{tag}")
    dmod3 = jnp.concatenate([acc_mod, acc_ln[2:3]], axis=0)
    small = dict(conv_w=dconv_w, conv_b=dconv_b[0], dt_bias=ssd_acc[2, 0:8], a_log=ssd_acc[0, 0:8],
                 d=ssd_acc[1, 0:8], norm_g=dng[0], f_bias=dfb[0, PAD_F0:PAD_F0 + FOX_HEADS], sconv_w=dsconv_w)
    return dxin, _unperm_mix_w_in(dwp), dw_out, acc_ln[0], acc_ln[1], dmod3, small


BIG = ("ffn1_w_in", "ffn1_w_out", "mix_w_in", "mix_w_out", "ffn2_w_in", "ffn2_w_out")
SHARD_AXIS = {"ffn1_w_in": 1, "ffn1_w_out": 0, "mix_w_in": 1, "mix_w_out": 0, "ffn2_w_in": 1, "ffn2_w_out": 0}


def _local_step(x, tgt, mod, wfull, small_p):
    t = x.shape[0]
    row = lambda v: v.reshape(1, -1)
    x0 = _ln_in_fwd(x, row(small_p["ln_in_g"]), row(small_p["ln_in_b"]))
    cur = x0
    saved, sps, wps = [], [], []
    for l in range(DEPTH):
        w = wfull[l]
        sp = dict(
            conv_w=small_p["ssd_conv_w"][l], conv_b=row(small_p["ssd_conv_b"][l]),
            dtb_row=_pad_lanes(small_p["ssd_dt_bias"][l], PAD_DT0), alog_row=_pad_lanes(small_p["ssd_a_log"][l], 0),
            d_row=_pad_lanes(small_p["ssd_d"][l], 0), ng=row(small_p["ssd_norm_g"][l]),
            fbias_row=_pad_lanes(small_p["fox_f_bias"][l], PAD_F0), sconv_w=small_p["sconv_w"][l])
        sp["dtb_col"] = jnp.transpose(sp["dtb_row"])
        sp["alog_col"] = jnp.transpose(sp["alog_row"])
        wp = _perm_mix_w_in(w["mix_w_in"])
        lg = [row(small_p["ln_g"][l, j]) for j in range(N_SUB)]
        lb = [row(small_p["ln_b"][l, j]) for j in range(N_SUB)]
        cur, s0 = _ffn_fwd(cur, mod[l, 0], w["ffn1_w_in"], w["ffn1_w_out"], lg[0], lb[0], f"l{l}f1")
        cur, s1 = _mix_fwd(cur, mod[l, 1], wp, w["mix_w_out"], lg[1], lb[1], sp, f"l{l}mx")
        cur, s2 = _ffn_fwd(cur, mod[l, 2], w["ffn2_w_in"], w["ffn2_w_out"], lg[2], lb[2], f"l{l}f2")
        saved.append((s0, s1, s2))
        sps.append(sp)
        wps.append(wp)

    dcur, loss_acc = _loss_head(cur, tgt)
    big_grads = [None] * DEPTH
    small_g = [None] * DEPTH
    dmod = [None] * DEPTH
    for l in reversed(range(DEPTH)):
        w = wfull[l]
        s0, s1, s2 = saved[l]
        lg = [row(small_p["ln_g"][l, j]) for j in range(N_SUB)]
        dcur, g2_in, g2_out, dlg2, dlb2, dm2 = _ffn_bwd(dcur, s2, mod[l, 2], w["ffn2_w_in"], w["ffn2_w_out"],
                                                      lg[2], f"l{l}f2")
        dcur, gm_in, gm_out, dlg1, dlb1, dm1, sm = _mix_bwd(dcur, s1, mod[l, 1], wps[l], w["mix_w_out"], lg[1],
                                                           sps[l], f"l{l}mx")
        dcur, g1_in, g1_out, dlg0, dlb0, dm0 = _ffn_bwd(dcur, s0, mod[l, 0], w["ffn1_w_in"], w["ffn1_w_out"],
                                                      lg[0], f"l{l}f1")
        big_grads[l] = dict(ffn1_w_in=g1_in, ffn1_w_out=g1_out, mix_w_in=gm_in, mix_w_out=gm_out,
                            ffn2_w_in=g2_in, ffn2_w_out=g2_out)
        sm["ln_g"] = jnp.stack([dlg0, dlg1, dlg2])
        sm["ln_b"] = jnp.stack([dlb0, dlb1, dlb2])
        small_g[l] = sm
        dmod[l] = jnp.stack([dm0, dm1, dm2])
    dx, acc_in = _ln_in_bwd(dcur, x, row(small_p["ln_in_g"]))
    return loss_acc[0, 0], dx, big_grads, small_g, jnp.stack(dmod), acc_in


SMALL_ORDER = ("ssd_conv_w", "ssd_conv_b", "ssd_dt_bias", "ssd_a_log", "ssd_d", "ssd_norm_g", "fox_f_bias",
               "sconv_w", "ln_g", "ln_b")
SMALL_KEY = dict(ssd_conv_w="conv_w", ssd_conv_b="conv_b", ssd_dt_bias="dt_bias", ssd_a_log="a_log", ssd_d="d",
                 ssd_norm_g="norm_g", fox_f_bias="f_bias", sconv_w="sconv_w", ln_g="ln_g", ln_b="ln_b")
COL_SHARDED_SMALL = ("ssd_conv_w", "sconv_w", "ln_g", "ln_b")


def _pad_to(v, n):
    return jnp.concatenate([v, jnp.zeros((n - v.shape[0],), v.dtype)])


def kernel(x, c, ln_in_g, ln_in_b, ada_w, ada_b, ffn1_w_in, ffn1_w_out, mix_w_in, mix_w_out, ssd_conv_w, ssd_conv_b, ssd_dt_bias, ssd_a_log, ssd_d, ssd_norm_g, fox_f_bias, sconv_w, ffn2_w_in, ffn2_w_out, ln_g, ln_b, loss_target, m_ln_in_g, m_ln_in_b, m_ada_w, m_ada_b, m_ffn1_w_in, m_ffn1_w_out, m_mix_w_in, m_mix_w_out, m_ssd_conv_w, m_ssd_conv_b, m_ssd_dt_bias, m_ssd_a_log, m_ssd_d, m_ssd_norm_g, m_fox_f_bias, m_sconv_w, m_ffn2_w_in, m_ffn2_w_out, m_ln_g, m_ln_b, v_ln_in_g, v_ln_in_b, v_ada_w, v_ada_b, v_ffn1_w_in, v_ffn1_w_out, v_mix_w_in, v_mix_w_out, v_ssd_conv_w, v_ssd_conv_b, v_ssd_dt_bias, v_ssd_a_log, v_ssd_d, v_ssd_norm_g, v_fox_f_bias, v_sconv_w, v_ffn2_w_in, v_ffn2_w_out, v_ln_g, v_ln_b):
    names = ("ln_in_g", "ln_in_b", "ada_w", "ada_b", "ffn1_w_in", "ffn1_w_out", "mix_w_in", "mix_w_out",
             "ssd_conv_w", "ssd_conv_b", "ssd_dt_bias", "ssd_a_log", "ssd_d", "ssd_norm_g", "fox_f_bias", "sconv_w",
             "ffn2_w_in", "ffn2_w_out", "ln_g", "ln_b")
    w_loc = dict(zip(names, (ln_in_g, ln_in_b, ada_w, ada_b, ffn1_w_in, ffn1_w_out, mix_w_in, mix_w_out, ssd_conv_w,
                             ssd_conv_b, ssd_dt_bias, ssd_a_log, ssd_d, ssd_norm_g, fox_f_bias, sconv_w, ffn2_w_in,
                             ffn2_w_out, ln_g, ln_b)))
    m_loc = dict(zip(names, (m_ln_in_g, m_ln_in_b, m_ada_w, m_ada_b, m_ffn1_w_in, m_ffn1_w_out, m_mix_w_in,
                             m_mix_w_out, m_ssd_conv_w, m_ssd_conv_b, m_ssd_dt_bias, m_ssd_a_log, m_ssd_d,
                             m_ssd_norm_g, m_fox_f_bias, m_sconv_w, m_ffn2_w_in, m_ffn2_w_out, m_ln_g, m_ln_b)))
    v_loc = dict(zip(names, (v_ln_in_g, v_ln_in_b, v_ada_w, v_ada_b, v_ffn1_w_in, v_ffn1_w_out, v_mix_w_in,
                             v_mix_w_out, v_ssd_conv_w, v_ssd_conv_b, v_ssd_dt_bias, v_ssd_a_log, v_ssd_d,
                             v_ssd_norm_g, v_fox_f_bias, v_sconv_w, v_ffn2_w_in, v_ffn2_w_out, v_ln_g, v_ln_b)))

    xi, yi, ci = lax.axis_index("x"), lax.axis_index("y"), lax.axis_index("c")
    me = 2 * xi + yi
    dev = 2 * me + ci

    def gather8(v, tag):
        v2 = v.reshape(1, -1)
        got = _bcast_chips([v2], f"gather_chips_{tag}")
        same_c = jnp.concatenate(_by_chip(me, v2, [g[0] for g in got]), axis=0)
        other_c = _swap_sibling([same_c], f"gather_sibling_{tag}")[0]
        pair = lax.switch(ci, [lambda a, b: jnp.stack([a, b], axis=1), lambda a, b: jnp.stack([b, a], axis=1)],
                          same_c, other_c)
        return pair.reshape(8, -1)

    def chip_concat(own, got3, axis):
        return jnp.concatenate(_by_chip(me, own, got3), axis=axis)

    small_cols = [w_loc[n].reshape(-1, w_loc[n].shape[-1]) for n in COL_SHARDED_SMALL]
    got = _bcast_chips(small_cols, "gather_small_params")
    small_p = {n: w_loc[n] for n in ("ln_in_g", "ln_in_b", "ssd_conv_b", "ssd_dt_bias", "ssd_a_log", "ssd_d",
                                     "ssd_norm_g", "fox_f_bias")}
    for i, n in enumerate(COL_SHARDED_SMALL):
        full = chip_concat(small_cols[i], [g[i] for g in got], 1)
        small_p[n] = full.reshape(w_loc[n].shape[:-1] + (full.shape[-1],))

    mine = [lax.dynamic_index_in_dim(w_loc[n], ci, 0, keepdims=False).astype(BF16) for n in BIG]
    got = _bcast_chips(mine, "gather_weights_chips")
    my_layer = [chip_concat(mine[i], [g[i] for g in got], SHARD_AXIS[n]) for i, n in enumerate(BIG)]
    other_layer = _swap_sibling(my_layer, "gather_weights_sibling")
    layers = lax.switch(ci, [lambda a, b: (a, b), lambda a, b: (b, a)], my_layer, other_layer)
    wfull = [dict(zip(BIG, layers[l])) for l in range(DEPTH)]

    c_all = gather8(c[0], "c")
    c_act = _silu_bf16(c_all)
    ada_w_b = ada_w.astype(BF16)
    mod_loc = [_matmul(c_act, ada_w_b[l], "nn", F32, f"ada_fwd_l{l}") for l in range(DEPTH)]
    mod_loc = jnp.stack(mod_loc)
    got = _bcast_chips([mod_loc], "gather_mod")
    mod_all = chip_concat(mod_loc, [g[0] for g in got], 2)
    mod = lax.dynamic_index_in_dim(mod_all, dev, 1, keepdims=False) + ada_b
    mod = mod.reshape(DEPTH, N_SUB, 3, D)

    loss_part, dx, big_g, small_g, dmod, acc_in = _local_step(x[0], loss_target[0], mod, wfull, small_p)
    loss = lax.psum(loss_part, ("x", "y", "c"))

    pieces = [dmod.reshape(-1), acc_in[0], acc_in[1]]
    for n in SMALL_ORDER:
        pieces.append(jnp.stack([small_g[l][SMALL_KEY[n]] for l in range(DEPTH)]).reshape(-1))
    sizes = [p.shape[0] for p in pieces]
    total = sum(sizes)
    padded = -(-total // 1024) * 1024
    vec = _pad_to(jnp.concatenate(pieces), padded)
    all_rows = gather8(vec, "small_grads")
    summed = _sum_rows(all_rows, "sum_small_grads")[0]
    offs = [0]
    for s in sizes:
        offs.append(offs[-1] + s)
    n_mod = sizes[0]
    grads = {"ada_b": summed[0:n_mod].reshape(DEPTH, 3 * N_SUB * D),
             "ln_in_g": summed[offs[1]:offs[2]], "ln_in_b": summed[offs[2]:offs[3]]}
    for i, n in enumerate(SMALL_ORDER):
        full = summed[offs[3 + i]:offs[4 + i]].reshape(small_p[n].shape)
        if n in COL_SHARDED_SMALL:
            wcols = w_loc[n].shape[-1]
            full = lax.dynamic_slice_in_dim(full, me * wcols, wcols, axis=full.ndim - 1)
        grads[n] = full

    dmod_all = all_rows[:, 0:n_mod].reshape(8, DEPTH, 3 * N_SUB * D)
    ncol = ada_w.shape[-1]
    dmod_cols = lax.dynamic_slice_in_dim(dmod_all, me * ncol, ncol, axis=2).astype(BF16)
    grads["ada_w"] = jnp.stack([_matmul(c_act, dmod_cols[:, l], "tn", F32, f"ada_bwd_l{l}") for l in range(DEPTH)])

    give = lax.switch(ci, [lambda g0, g1: [g1[n] for n in BIG], lambda g0, g1: [g0[n] for n in BIG]],
                      big_g[0], big_g[1])
    keep = lax.switch(ci, [lambda g0, g1: [g0[n] for n in BIG], lambda g0, g1: [g1[n] for n in BIG]],
                      big_g[0], big_g[1])
    from_sib = _swap_sibling(give, "grad_swap_sibling")
    chip_part = []
    for i, n in enumerate(BIG):
        chip_part.append(_elementwise(lambda a, b: (a.astype(F32) + b.astype(F32),), f"grad_add_cores_{n}",
                                      [keep[i], from_sib[i]], [BF16])[0])

    def shards_for(mm):
        def f(parts):
            out = []
            for rel in (0, 2, 1, 3):
                tgt_chip = mm ^ rel
                row = []
                for i, n in enumerate(BIG):
                    ax = SHARD_AXIS[n]
                    w = parts[i].shape[ax] // 4
                    row.append(lax.slice_in_dim(parts[i], tgt_chip * w, (tgt_chip + 1) * w, axis=ax))
                out.append(row)
            return out
        return f

    split = lax.switch(me, [shards_for(mm) for mm in range(4)], chip_part)
    got = _scatter_chips(split[1:], "grad_scatter_chips")
    reduced = []
    for i, n in enumerate(BIG):
        four = _by_chip(me, split[0][i], [g[i] for g in got])
        reduced.append(_elementwise(
            lambda a, b, cc, d: (((a.astype(F32) + b.astype(F32)) + cc.astype(F32)) + d.astype(F32),),
            f"grad_add_chips_{n}", four, [F32])[0])
    other = _swap_sibling(reduced, "grad_swap_back")
    per_layer = lax.switch(ci, [lambda a, b: (a, b), lambda a, b: (b, a)], reduced, other)
    for i, n in enumerate(BIG):
        grads[n] = jnp.stack([per_layer[0][i], per_layer[1][i]])

    delta, new_m, new_v = {}, {}, {}
    for n in ("ada_w",) + BIG:
        delta[n], new_m[n], new_v[n] = _adamw(w_loc[n], grads[n], m_loc[n], v_loc[n], f"adamw_{n}")
    small_names = [n for n in names if n not in ("ada_w",) + BIG]
    flat = lambda d: jnp.concatenate([d[n].reshape(-1) for n in small_names])
    n_small = sum(w_loc[n].size for n in small_names)
    n_pad = -(-n_small // 1024) * 1024
    packed = [_pad_to(flat(d), n_pad).reshape(-1, LANES) for d in (w_loc, grads, m_loc, v_loc)]
    d_s, m_s, v_s = _adamw(*packed, "adamw_small")
    off = 0
    for n in small_names:
        sz = w_loc[n].size
        delta[n] = d_s.reshape(-1)[off:off + sz].reshape(w_loc[n].shape)
        new_m[n] = m_s.reshape(-1)[off:off + sz].reshape(w_loc[n].shape)
        new_v[n] = v_s.reshape(-1)[off:off + sz].reshape(w_loc[n].shape)
        off += sz

    return (loss, dx[None], *[grads[n] for n in names], *[delta[n] for n in names],
            *[new_m[n] for n in names], *[new_v[n] for n in names])
```

```python
import functools

import jax
import jax.numpy as jnp
from jax import lax
from jax.experimental import pallas as pl
from jax.experimental.pallas import tpu as pltpu

F32 = jnp.float32
BF16 = jnp.bfloat16
MESH = pl.DeviceIdType.MESH

D = 1024
DEPTH = 2
N_SUB = 3
D_FF = 2816
FF_TILE = D_FF // 2
ALPHA = (2 * DEPTH) ** 0.25
LN_EPS = 1e-5
RMS_EPS = 1e-5
SSD_W = 512
SSD_HEADS = 8
SSD_CONV_K = 4
SSD_CONV_DIM = 1024
FOX_W = 256
FOX_HEADS = 4
FOX_HD = 64
SC_W = 256
SC_K = 3
D_IN_PROJ = 3084
P_Z, P_XBC, P_QKV, P_SC, P_PAD = 0, 512, 1536, 2304, 3072
D_PROJ_PAD = 3200
PAD_DT0, PAD_F0 = 0, 8
SSD_CHUNK = 256
ATT_BLOCK = 512
CUM_BLOCK = 256
LANES = 128
VMEM_LIMIT = 56 * 1024 * 1024

ADAM_LR, ADAM_B1, ADAM_B2, ADAM_EPS, ADAM_WD, ADAM_STEP = 0.001, 0.9, 0.999, 1e-08, 0.01, 10
NEG = -1e30


def _cp(*sem):
    return pltpu.CompilerParams(dimension_semantics=sem, vmem_limit_bytes=VMEM_LIMIT)


def _pick(n, cands):
    for c in cands:
        if n % c == 0:
            return c
    return n


def _dot(a, b):
    return lax.dot_general(a, b, (((1,), (0,)), ((), ())), preferred_element_type=F32)


def _dot_nt(a, b):
    return lax.dot_general(a, b, (((1,), (1,)), ((), ())), preferred_element_type=F32)


def _dot_tn(a, b):
    return lax.dot_general(a, b, (((0,), (0,)), ((), ())), preferred_element_type=F32)


def _sigmoid(x):
    return 1.0 / (1.0 + jnp.exp(-x))


def _softplus(x):
    return jnp.maximum(x, 0.0) + jnp.log(1.0 + jnp.exp(-jnp.abs(x)))


def _split3(v):
    h = v.astype(BF16)
    r = v - h.astype(F32)
    m = r.astype(BF16)
    l = (r - m.astype(F32)).astype(BF16)
    return h, m, l


def _tri_left(tri, v):
    h, m, l = _split3(v)
    return _dot(tri, h) + _dot(tri, m) + _dot(tri, l)


def _tri_right(v, tri):
    h, m, l = _split3(v)
    return _dot(h, tri) + _dot(m, tri) + _dot(l, tri)


def _matmul(a, b, mode, out_dtype, name, halves=None):
    assert a.dtype == BF16 and b.dtype == BF16, (name, a.dtype, b.dtype)
    if halves == "a":
        assert mode == "nt" and a.shape[0] == 2 and a.shape[2] == D_FF
        m, k, n = a.shape[1], 2 * D_FF, b.shape[0]
    elif halves == "b":
        assert mode == "tn" and b.shape[0] == 2 and b.shape[2] == D_FF
        (k, m), n = a.shape, 2 * D_FF
    elif mode == "nn":
        (m, k), n = a.shape, b.shape[1]
    elif mode == "nt":
        (m, k), n = a.shape, b.shape[0]
    else:
        (k, m), n = a.shape, b.shape[1]
    tm = m if m <= 1024 else _pick(m, (1024, 1408, 512, 256, 128))
    tn = n if n <= 1024 else _pick(n, (1408, 640, 512, 256, 128))
    tk = k if k <= 1024 else _pick(k, (1408, 1024, 640, 512, 256, 128))
    nk = k // tk
    if mode == "nn":
        dn = (((1,), (0,)), ((), ()))
        a_spec = pl.BlockSpec((tm, tk), lambda i, j, kk: (i, kk))
        b_spec = pl.BlockSpec((tk, tn), lambda i, j, kk: (kk, j))
    elif mode == "nt":
        dn = (((1,), (1,)), ((), ()))
        a_spec = pl.BlockSpec((tm, tk), lambda i, j, kk: (i, kk))
        b_spec = pl.BlockSpec((tn, tk), lambda i, j, kk: (j, kk))
    else:
        dn = (((0,), (0,)), ((), ()))
        a_spec = pl.BlockSpec((tk, tm), lambda i, j, kk: (kk, i))
        b_spec = pl.BlockSpec((tk, tn), lambda i, j, kk: (kk, j))
    per_half = D_FF // FF_TILE
    if halves == "a":
        assert tk == FF_TILE
        a_spec = pl.BlockSpec((None, tm, tk), lambda i, j, kk: (kk // per_half, i, kk % per_half))
    elif halves == "b":
        assert tn == FF_TILE
        b_spec = pl.BlockSpec((None, tk, tn), lambda i, j, kk: (j // per_half, kk, j % per_half))

    def body(a_ref, b_ref, o_ref, *acc):
        d = lax.dot_general(a_ref[...], b_ref[...], dn, preferred_element_type=F32)
        if nk == 1:
            o_ref[...] = d.astype(o_ref.dtype)
            return
        acc_ref, = acc
        kk = pl.program_id(2)

        @pl.when(kk == 0)
        def _():
            acc_ref[...] = d

        @pl.when((kk > 0) & (kk < nk - 1))
        def _():
            acc_ref[...] += d

        @pl.when(kk == nk - 1)
        def _():
            o_ref[...] = (acc_ref[...] + d).astype(o_ref.dtype)

    return pl.pallas_call(
        body, name=name, grid=(m // tm, n // tn, nk),
        in_specs=[a_spec, b_spec],
        out_specs=pl.BlockSpec((tm, tn), lambda i, j, kk: (i, j)),
        out_shape=jax.ShapeDtypeStruct((m, n), out_dtype),
        scratch_shapes=[pltpu.VMEM((tm, tn), F32)] if nk > 1 else [],
        compiler_params=_cp("parallel", "parallel", "arbitrary"),
    )(a, b)


def _rows(body, name, t, tb, row_in, full_in, row_out, acc_out):
    in_specs, args = [], []
    for r in row_in:
        if isinstance(r, tuple):
            arr, w, j = r
            in_specs.append(pl.BlockSpec((tb, w), functools.partial(lambda i, jj: (i, jj), jj=j)))
            args.append(arr)
        else:
            in_specs.append(pl.BlockSpec((tb, r.shape[1]), lambda i: (i, 0)))
            args.append(r)
    for f in full_in:
        in_specs.append(pl.BlockSpec(f.shape, functools.partial(lambda i, nd: (0,) * nd, nd=f.ndim)))
        args.append(f)
    out_specs = [pl.BlockSpec((tb, c), lambda i: (i, 0)) for c, _ in row_out]
    out_specs += [pl.BlockSpec(s, functools.partial(lambda i, nd: (0,) * nd, nd=len(s))) for s in acc_out]
    out_shape = [jax.ShapeDtypeStruct((t, c), dt) for c, dt in row_out]
    out_shape += [jax.ShapeDtypeStruct(s, F32) for s in acc_out]
    return pl.pallas_call(
        body, name=name, grid=(t // tb,), in_specs=in_specs, out_specs=out_specs, out_shape=out_shape,
        compiler_params=_cp("arbitrary"),
    )(*args)


def _ln_stats(r):
    mu = jnp.mean(r, axis=-1, keepdims=True)
    xc = r - mu
    var = jnp.mean(xc * xc, axis=-1, keepdims=True)
    rstd = lax.rsqrt(var + LN_EPS)
    return xc * rstd, rstd


def _ln_bwd(dout, xhat, rstd, g):
    dxh = dout * g
    m1 = jnp.mean(dxh, axis=-1, keepdims=True)
    m2 = jnp.mean(dxh * xhat, axis=-1, keepdims=True)
    return rstd * (dxh - m1 - xhat * m2)


def _first(i, acc_refs):
    @pl.when(i == 0)
    def _():
        for a in acc_refs:
            a[...] = jnp.zeros_like(a)


def _modulated(xv, m_ref):
    return (xv * (1.0 + m_ref[1:2, :]) + m_ref[0:1, :]).astype(BF16)


def _ln_in_fwd(x, g, b, next_mod3):
    t = x.shape[0]

    def body(x_ref, g_ref, b_ref, m_ref, o_ref, h_ref):
        xhat, _ = _ln_stats(x_ref[...])
        out = xhat * g_ref[...] + b_ref[...]
        o_ref[...] = out
        h_ref[...] = _modulated(out, m_ref)

    return _rows(body, "ln_in_fwd", t, 256, [x], [g, b, next_mod3], [(D, F32), (D, BF16)], [])


def _ln_in_bwd(dx0, x, g):
    t = x.shape[0]

    def body(d_ref, x_ref, g_ref, o_ref, acc_ref):
        _first(pl.program_id(0), [acc_ref])
        xhat, rstd = _ln_stats(x_ref[...])
        d = d_ref[...]
        o_ref[...] = _ln_bwd(d, xhat, rstd, g_ref[...])
        acc_ref[0:1, :] += jnp.sum(d * xhat, axis=0, keepdims=True)
        acc_ref[1:2, :] += jnp.sum(d, axis=0, keepdims=True)

    return _rows(body, "ln_in_bwd", t, 256, [dx0, x], [g], [(D, F32)], [(2, D)])


def _modulate_bwd(dxres, dh, xin, mod3, name):
    t = xin.shape[0]

    def body(r_ref, dh_ref, x_ref, m_ref, o_ref, acc_ref):
        _first(pl.program_id(0), [acc_ref])
        dh_v = dh_ref[...]
        o_ref[...] = r_ref[...] + dh_v * (1.0 + m_ref[1:2, :])
        acc_ref[0:1, :] += jnp.sum(dh_v, axis=0, keepdims=True)
        acc_ref[1:2, :] += jnp.sum(dh_v * x_ref[...], axis=0, keepdims=True)

    return _rows(body, name, t, 256, [dxres, dh, xin], [mod3], [(D, F32)], [(2, D)])


def _ffn_in_swiglu(h, w_in, name):
    t = h.shape[0]
    tm = _pick(t, (512, 256, 128))
    nj = D_FF // FF_TILE

    def body(h_ref, wg_ref, wu_ref, u_ref, a_ref):
        hv = h_ref[...]
        gate = _dot(hv, wg_ref[...])
        up = _dot(hv, wu_ref[...])
        u_ref[0] = gate.astype(BF16)
        u_ref[1] = up.astype(BF16)
        a_ref[...] = (gate * _sigmoid(gate) * up).astype(BF16)

    return pl.pallas_call(
        body, name=name, grid=(nj, t // tm),
        in_specs=[pl.BlockSpec((tm, D), lambda j, i: (i, 0)),
                  pl.BlockSpec((D, FF_TILE), lambda j, i: (0, j)),
                  pl.BlockSpec((D, FF_TILE), lambda j, i: (0, nj + j))],
        out_specs=[pl.BlockSpec((2, tm, FF_TILE), lambda j, i: (0, i, j)),
                   pl.BlockSpec((tm, FF_TILE), lambda j, i: (i, j))],
        out_shape=[jax.ShapeDtypeStruct((2, t, D_FF), BF16), jax.ShapeDtypeStruct((t, D_FF), BF16)],
        compiler_params=_cp("parallel", "parallel"),
    )(h, w_in, w_in)


def _ffn_out_dx_swiglu(dy, w_out, u, name):
    t = dy.shape[0]
    tm = _pick(t, (512, 256, 128))
    nj = D_FF // FF_TILE

    def body(dy_ref, w_ref, u_ref, du_ref):
        da = _dot_nt(dy_ref[...], w_ref[...])
        gate = u_ref[0].astype(F32)
        up = u_ref[1].astype(F32)
        sg = _sigmoid(gate)
        du_ref[0] = (da * up * (sg * (1.0 + gate * (1.0 - sg)))).astype(BF16)
        du_ref[1] = (da * gate * sg).astype(BF16)

    blk3 = pl.BlockSpec((2, tm, FF_TILE), lambda j, i: (0, i, j))
    return pl.pallas_call(
        body, name=name, grid=(nj, t // tm),
        in_specs=[pl.BlockSpec((tm, D), lambda j, i: (i, 0)),
                  pl.BlockSpec((FF_TILE, D), lambda j, i: (j, 0)), blk3],
        out_specs=blk3,
        out_shape=jax.ShapeDtypeStruct((2, t, D_FF), BF16),
        compiler_params=_cp("parallel", "parallel"),
    )(dy, w_out, u)


def _res_ln(xin, y, mod3, lg, lb, factor, name, next_mod3=None):
    t = xin.shape[0]

    def body(x_ref, y_ref, m_ref, g_ref, b_ref, *rest):
        r = ALPHA * x_ref[...] + (factor * m_ref[2:3, :]) * y_ref[...]
        xhat, _ = _ln_stats(r)
        out = xhat * g_ref[...] + b_ref[...]
        if next_mod3 is None:
            rest[0][...] = out
        else:
            rest[1][...] = out
            rest[2][...] = _modulated(out, rest[0])

    if next_mod3 is None:
        return _rows(body, name, t, 256, [xin, y], [mod3, lg, lb], [(D, F32)], [])[0], None
    return _rows(body, name, t, 256, [xin, y], [mod3, lg, lb, next_mod3], [(D, F32), (D, BF16)], [])


def _mod_res_bwd(dxres, dh, mod3, xin_p, y_p, mod3_p, lg_p, lb_p, factor_p, name):
    t = dxres.shape[0]

    def body(r_ref, dh_ref, xp_ref, yp_ref, m_ref, mp_ref, g_ref, b_ref, dres_ref, dy_ref, acc_ref):
        _first(pl.program_id(0), [acc_ref])
        gate = factor_p * mp_ref[2:3, :]
        yv = yp_ref[...]
        xhat, rstd = _ln_stats(ALPHA * xp_ref[...] + gate * yv)
        xin = xhat * g_ref[...] + b_ref[...]
        dh_v = dh_ref[...]
        d = r_ref[...] + dh_v * (1.0 + m_ref[1:2, :])
        dr = _ln_bwd(d, xhat, rstd, g_ref[...])
        dres_ref[...] = ALPHA * dr
        dy_ref[...] = (gate * dr).astype(BF16)
        acc_ref[0:1, :] += jnp.sum(dh_v, axis=0, keepdims=True)
        acc_ref[1:2, :] += jnp.sum(dh_v * xin, axis=0, keepdims=True)
        acc_ref[2:3, :] += jnp.sum(d * xhat, axis=0, keepdims=True)
        acc_ref[3:4, :] += jnp.sum(d, axis=0, keepdims=True)
        acc_ref[4:5, :] += jnp.sum(factor_p * yv * dr, axis=0, keepdims=True)

    return _rows(body, name, t, 256, [dxres, dh, xin_p, y_p], [mod3, mod3_p, lg_p, lb_p],
                 [(D, F32), (D, BF16)], [(5, D)])


def _res_ln_bwd(dout, xin, y, mod3, lg, factor, name):
    t = xin.shape[0]

    def body(d_ref, x_ref, y_ref, m_ref, g_ref, dres_ref, dy_ref, acc_ref):
        _first(pl.program_id(0), [acc_ref])
        gate = factor * m_ref[2:3, :]
        yv = y_ref[...]
        r = ALPHA * x_ref[...] + gate * yv
        xhat, rstd = _ln_stats(r)
        d = d_ref[...]
        dr = _ln_bwd(d, xhat, rstd, g_ref[...])
        dres_ref[...] = ALPHA * dr
        dy_ref[...] = (gate * dr).astype(BF16)
        acc_ref[0:1, :] += jnp.sum(d * xhat, axis=0, keepdims=True)
        acc_ref[1:2, :] += jnp.sum(d, axis=0, keepdims=True)
        acc_ref[2:3, :] += jnp.sum(factor * yv * dr, axis=0, keepdims=True)

    return _rows(body, name, t, 256, [dout, xin, y], [mod3, lg], [(D, F32), (D, BF16)], [(3, D)])


def _loss_head(xf, tgt):
    t = xf.shape[0]

    def body(x_ref, t_ref, d_ref, acc_ref):
        _first(pl.program_id(0), [acc_ref])
        e = x_ref[...] - t_ref[...]
        d_ref[...] = e * (1.0 / D)
        part = 0.5 * jnp.sum(jnp.mean(e * e, axis=-1, keepdims=True), axis=0, keepdims=True)
        acc_ref[...] += jnp.broadcast_to(part, acc_ref.shape)

    return _rows(body, "loss_head", t, 256, [xf, tgt], [], [(D, F32)], [(1, LANES)])


def _silu_bf16(c_all):
    def body(c_ref, o_ref):
        v = c_ref[...]
        o_ref[...] = (v * _sigmoid(v)).astype(BF16)

    return _rows(body, "silu_c", c_all.shape[0], c_all.shape[0], [c_all], [], [(c_all.shape[1], BF16)], [])[0]


def _sum_rows(v, name):
    r, n = v.shape
    tn = _pick(n, (8192, 4096, 2048, 1024, 512, 256, 128))

    def body(v_ref, o_ref):
        acc = v_ref[0:1, :]
        for k in range(1, r):
            acc = acc + v_ref[k:k + 1, :]
        o_ref[...] = acc

    return pl.pallas_call(
        body, name=name, grid=(n // tn,),
        in_specs=[pl.BlockSpec((r, tn), lambda j: (0, j))],
        out_specs=pl.BlockSpec((1, tn), lambda j: (0, j)),
        out_shape=jax.ShapeDtypeStruct((1, n), F32),
        compiler_params=_cp("parallel"),
    )(v)


def _elementwise(fn, name, ins, out_dtypes):
    r, c = ins[0].shape
    tb = _pick(r, (128, 64, 32, 16, 8))
    n_in = len(ins)

    def body(*refs):
        outs = fn(*[x[...] for x in refs[:n_in]])
        for o_ref, o in zip(refs[n_in:], outs):
            o_ref[...] = o.astype(o_ref.dtype)

    spec = pl.BlockSpec((tb, c), lambda i: (i, 0))
    return pl.pallas_call(
        body, name=name, grid=(r // tb,), in_specs=[spec] * n_in, out_specs=[spec] * len(out_dtypes),
        out_shape=[jax.ShapeDtypeStruct((r, c), dt) for dt in out_dtypes],
        compiler_params=_cp("parallel"),
    )(*ins)


def _adamw_math(w, g, m, v):
    m = ADAM_B1 * m + (1.0 - ADAM_B1) * g
    v = ADAM_B2 * v + (1.0 - ADAM_B2) * (g * g)
    m_hat = m / (1.0 - ADAM_B1 ** ADAM_STEP)
    v_hat = v / (1.0 - ADAM_B2 ** ADAM_STEP)
    delta = -ADAM_LR * (m_hat / (jnp.sqrt(v_hat) + ADAM_EPS) + ADAM_WD * w)
    return delta, m, v


def _adamw(w, g, m, v, name):
    shape = w.shape
    c = shape[-1]
    w2, g2, m2, v2 = (a.reshape(-1, c) for a in (w, g, m, v))
    outs = _elementwise(_adamw_math, name, [w2, g2, m2, v2], [F32, F32, F32])
    return tuple(o.reshape(shape) for o in outs)


def _remote_exchange(ins, plan, peers_of, name):
    n_in, n_out = len(ins), len(plan)

    def body(*refs):
        in_refs, out_refs = refs[:n_in], refs[n_in:n_in + n_out]
        send_sems, recv_sems = refs[n_in + n_out], refs[n_in + n_out + 1]
        peers = peers_of(lax.axis_index("x"), lax.axis_index("y"), lax.axis_index("c"))
        copies = [
            pltpu.make_async_remote_copy(
                src_ref=in_refs[src], dst_ref=out_refs[k], send_sem=send_sems.at[k], recv_sem=recv_sems.at[k],
                device_id=peers[peer], device_id_type=MESH)
            for k, (peer, src) in enumerate(plan)
        ]
        for cp in copies:
            cp.start()
        for cp in copies:
            cp.wait()

    any_spec = pl.BlockSpec(memory_space=pl.ANY)
    return list(pl.pallas_call(
        body, name=name,
        in_specs=[any_spec] * n_in, out_specs=[any_spec] * n_out,
        out_shape=[jax.ShapeDtypeStruct(ins[src].shape, ins[src].dtype) for _, src in plan],
        scratch_shapes=[pltpu.SemaphoreType.DMA((n_out,)), pltpu.SemaphoreType.DMA((n_out,))],
    )(*ins))


def _sibling(x, y, c):
    return [(x, y, 1 - c)]


def _other_chips(x, y, c):
    return [(1 - x, y, c), (x, 1 - y, c), (1 - x, 1 - y, c)]


def _swap_sibling(arrs, name):
    return _remote_exchange(arrs, [(0, i) for i in range(len(arrs))], _sibling, name)


def _bcast_chips(arrs, name):
    n = len(arrs)
    out = _remote_exchange(arrs, [(p, i) for p in range(3) for i in range(n)], _other_chips, name)
    return [out[p * n:(p + 1) * n] for p in range(3)]


def _scatter_chips(send3, name):
    n = len(send3[0])
    flat = [a for p in range(3) for a in send3[p]]
    out = _remote_exchange(flat, [(p, p * n + i) for p in range(3) for i in range(n)], _other_chips, name)
    return [out[p * n:(p + 1) * n] for p in range(3)]


def _by_chip(me, own, got3):
    by_rel = [own, got3[0], got3[1], got3[2]]
    rel_bits = (0, 2, 1, 3)

    def branch(m):
        def f(ops):
            return [ops[rel_bits.index(i ^ m)] for i in range(4)]
        return f

    return lax.switch(me, [branch(m) for m in range(4)], by_rel)


def _shift_down(v, s, t_iota):
    return jnp.where(t_iota >= s, pltpu.roll(v, s, 0), 0.0)


def _shift_up(v, s, t_iota, t):
    return jnp.where(t_iota < t - s, pltpu.roll(v, t - s, 0), 0.0)


def _ssd_conv_fwd(proj, w, b):
    t = proj.shape[0]
    k_w = SSD_CONV_K

    def body(x_ref, w_ref, b_ref, o_ref):
        x = x_ref[...]
        ti = lax.broadcasted_iota(jnp.int32, x.shape, 0)
        pre = x * w_ref[k_w - 1:k_w, :] + b_ref[...]
        for s in range(1, k_w):
            pre = pre + _shift_down(x, s, ti) * w_ref[k_w - 1 - s:k_w - s, :]
        o_ref[...] = pre * _sigmoid(pre)

    off = P_XBC // LANES
    return pl.pallas_call(
        body, name="ssd_conv_fwd", grid=(SSD_CONV_DIM // LANES,),
        in_specs=[pl.BlockSpec((t, LANES), lambda j: (0, off + j)),
                  pl.BlockSpec((k_w, LANES), lambda j: (0, j)),
                  pl.BlockSpec((1, LANES), lambda j: (0, j))],
        out_specs=pl.BlockSpec((t, LANES), lambda j: (0, j)),
        out_shape=jax.ShapeDtypeStruct((t, SSD_CONV_DIM), F32),
        compiler_params=_cp("parallel"),
    )(proj, w, b)


def _ssd_conv_bwd(dxbc, proj, w, b):
    t = proj.shape[0]
    k_w = SSD_CONV_K

    def body(d_ref, x_ref, w_ref, b_ref, dx_ref, dw_ref, db_ref):
        x = x_ref[...]
        ti = lax.broadcasted_iota(jnp.int32, x.shape, 0)
        shifted = [x] + [_shift_down(x, s, ti) for s in range(1, k_w)]
        pre = b_ref[...] + shifted[0] * w_ref[k_w - 1:k_w, :]
        for s in range(1, k_w):
            pre = pre + shifted[s] * w_ref[k_w - 1 - s:k_w - s, :]
        sg = _sigmoid(pre)
        dpre = d_ref[...] * (sg * (1.0 + pre * (1.0 - sg)))
        db_ref[...] = jnp.sum(dpre, axis=0, keepdims=True)
        dx = dpre * w_ref[k_w - 1:k_w, :]
        for s in range(k_w):
            dw_ref[k_w - 1 - s:k_w - s, :] = jnp.sum(dpre * shifted[s], axis=0, keepdims=True)
            if s:
                dx = dx + _shift_up(dpre, s, ti, t) * w_ref[k_w - 1 - s:k_w - s, :]
        dx_ref[...] = dx

    off = P_XBC // LANES
    return pl.pallas_call(
        body, name="ssd_conv_bwd", grid=(SSD_CONV_DIM // LANES,),
        in_specs=[pl.BlockSpec((t, LANES), lambda j: (0, j)),
                  pl.BlockSpec((t, LANES), lambda j: (0, off + j)),
                  pl.BlockSpec((k_w, LANES), lambda j: (0, j)),
                  pl.BlockSpec((1, LANES), lambda j: (0, j))],
        out_specs=[pl.BlockSpec((t, LANES), lambda j: (0, j)),
                   pl.BlockSpec((k_w, LANES), lambda j: (0, j)),
                   pl.BlockSpec((1, LANES), lambda j: (0, j))],
        out_shape=[jax.ShapeDtypeStruct((t, SSD_CONV_DIM), F32),
                   jax.ShapeDtypeStruct((k_w, SSD_CONV_DIM), F32),
                   jax.ShapeDtypeStruct((1, SSD_CONV_DIM), F32)],
        compiler_params=_cp("parallel"),
    )(dxbc, proj, w, b)


def _shortconv_fwd(proj, w):
    t = proj.shape[0]
    nb = SC_W // LANES
    off = P_SC // LANES

    def body(b_ref, c_ref, x_ref, w_ref, o_ref):
        u = c_ref[...] * x_ref[...]
        ti = lax.broadcasted_iota(jnp.int32, u.shape, 0)
        cv = u * w_ref[SC_K - 1:SC_K, :]
        for s in range(1, SC_K):
            cv = cv + _shift_down(u, s, ti) * w_ref[SC_K - 1 - s:SC_K - s, :]
        o_ref[...] = (b_ref[...] * cv).astype(BF16)

    return pl.pallas_call(
        body, name="shortconv_fwd", grid=(nb,),
        in_specs=[pl.BlockSpec((t, LANES), lambda j: (0, off + j)),
                  pl.BlockSpec((t, LANES), lambda j: (0, off + nb + j)),
                  pl.BlockSpec((t, LANES), lambda j: (0, off + 2 * nb + j)),
                  pl.BlockSpec((SC_K, LANES), lambda j: (0, j))],
        out_specs=pl.BlockSpec((t, LANES), lambda j: (0, j)),
        out_shape=jax.ShapeDtypeStruct((t, SC_W), BF16),
        compiler_params=_cp("parallel"),
    )(proj, proj, proj, w)


def _shortconv_bwd(dy, dy_off, proj, w):
    t = proj.shape[0]
    nb = SC_W // LANES
    off = P_SC // LANES
    doff = dy_off // LANES

    def body(d_ref, b_ref, c_ref, x_ref, w_ref, db_ref, dc_ref, dx_ref, dw_ref):
        cg, xin = c_ref[...], x_ref[...]
        u = cg * xin
        ti = lax.broadcasted_iota(jnp.int32, u.shape, 0)
        shifted = [u] + [_shift_down(u, s, ti) for s in range(1, SC_K)]
        cv = shifted[0] * w_ref[SC_K - 1:SC_K, :]
        for s in range(1, SC_K):
            cv = cv + shifted[s] * w_ref[SC_K - 1 - s:SC_K - s, :]
        d = d_ref[...]
        db_ref[...] = (d * cv).astype(BF16)
        dcv = d * b_ref[...]
        du = dcv * w_ref[SC_K - 1:SC_K, :]
        for s in range(SC_K):
            dw_ref[SC_K - 1 - s:SC_K - s, :] = jnp.sum(dcv * shifted[s], axis=0, keepdims=True)
            if s:
                du = du + _shift_up(dcv, s, ti, t) * w_ref[SC_K - 1 - s:SC_K - s, :]
        dc_ref[...] = (du * xin).astype(BF16)
        dx_ref[...] = (du * cg).astype(BF16)

    tile = pl.BlockSpec((t, LANES), lambda j: (0, j))
    outs = pl.pallas_call(
        body, name="shortconv_bwd", grid=(nb,),
        in_specs=[pl.BlockSpec((t, LANES), lambda j: (0, doff + j)),
                  pl.BlockSpec((t, LANES), lambda j: (0, off + j)),
                  pl.BlockSpec((t, LANES), lambda j: (0, off + nb + j)),
                  pl.BlockSpec((t, LANES), lambda j: (0, off + 2 * nb + j)),
                  pl.BlockSpec((SC_K, LANES), lambda j: (0, j))],
        out_specs=[tile, tile, tile, pl.BlockSpec((SC_K, LANES), lambda j: (0, j))],
        out_shape=[jax.ShapeDtypeStruct((t, SC_W), BF16)] * 3 + [jax.ShapeDtypeStruct((SC_K, SC_W), F32)],
        compiler_params=_cp("parallel"),
    )(dy, proj, proj, proj, w)
    return outs


def _cum_logf(proj, fbias_row):
    t = proj.shape[0]
    blk = CUM_BLOCK

    def body(p_ref, b_ref, o_ref, carry_ref):
        i = pl.program_id(0)

        @pl.when(i == 0)
        def _():
            carry_ref[...] = jnp.zeros_like(carry_ref)

        lf = -_softplus(-(p_ref[...] + b_ref[...]))
        r = lax.broadcasted_iota(jnp.int32, (blk, blk), 0)
        c = lax.broadcasted_iota(jnp.int32, (blk, blk), 1)
        tri = (r >= c).astype(BF16)
        o_ref[...] = _tri_left(tri, lf) + carry_ref[...]
        carry_ref[...] = o_ref[blk - 1:blk, :]

    return pl.pallas_call(
        body, name="cum_logf", grid=(t // blk,),
        in_specs=[pl.BlockSpec((blk, LANES), lambda i: (i, P_PAD // LANES)),
                  pl.BlockSpec((1, LANES), lambda i: (0, 0))],
        out_specs=pl.BlockSpec((blk, LANES), lambda i: (i, 0)),
        out_shape=jax.ShapeDtypeStruct((t, LANES), F32),
        scratch_shapes=[pltpu.VMEM((1, LANES), F32)],
        compiler_params=_cp("arbitrary"),
    )(proj, fbias_row)


def _pad_block_bwd(dcf, ddt, proj, fbias_row):
    t = proj.shape[0]
    blk = CUM_BLOCK
    nb = t // blk

    def body(dcf_ref, ddt_ref, p_ref, b_ref, o_ref, db_ref, carry_ref):
        i = pl.program_id(0)

        @pl.when(i == 0)
        def _():
            carry_ref[...] = jnp.zeros_like(carry_ref)
            db_ref[...] = jnp.zeros_like(db_ref)

        r = lax.broadcasted_iota(jnp.int32, (blk, blk), 0)
        c = lax.broadcasted_iota(jnp.int32, (blk, blk), 1)
        tri = (r <= c).astype(BF16)
        rev = _tri_left(tri, dcf_ref[...]) + carry_ref[...]
        carry_ref[...] = jnp.sum(dcf_ref[...], axis=0, keepdims=True) + carry_ref[...]
        lane = lax.broadcasted_iota(jnp.int32, (blk, LANES), 1)
        is_f = (lane >= PAD_F0) & (lane < PAD_F0 + FOX_HEADS)
        df = jnp.where(is_f, rev * _sigmoid(-(p_ref[...] + b_ref[...])), 0.0)
        db_ref[...] += jnp.sum(df, axis=0, keepdims=True)
        o_ref[...] = jnp.where(lane < PAD_DT0 + SSD_HEADS, ddt_ref[...], df).astype(BF16)

    return pl.pallas_call(
        body, name="pad_block_bwd", grid=(nb,),
        in_specs=[pl.BlockSpec((blk, LANES), lambda i: (nb - 1 - i, 0)),
                  pl.BlockSpec((blk, LANES), lambda i: (nb - 1 - i, 0)),
                  pl.BlockSpec((blk, LANES), lambda i: (nb - 1 - i, P_PAD // LANES)),
                  pl.BlockSpec((1, LANES), lambda i: (0, 0))],
        out_specs=[pl.BlockSpec((blk, LANES), lambda i: (nb - 1 - i, 0)),
                   pl.BlockSpec((1, LANES), lambda i: (0, 0))],
        out_shape=[jax.ShapeDtypeStruct((t, LANES), BF16), jax.ShapeDtypeStruct((1, LANES), F32)],
        scratch_shapes=[pltpu.VMEM((1, LANES), F32)],
        compiler_params=_cp("arbitrary"),
    )(dcf, ddt, proj, fbias_row)


def _att_scores(q, k, cq, ck, diagonal, blk):
    s = _dot_nt(q, k) * (FOX_HD ** -0.5) + cq - ck
    if diagonal:
        r = lax.broadcasted_iota(jnp.int32, (blk, blk), 0)
        c = lax.broadcasted_iota(jnp.int32, (blk, blk), 1)
        s = jnp.where(r >= c, s, NEG)
    return s


def _fox_fwd(q, k, v, cf_col, cf_row):
    h, t, hd = q.shape
    blk = min(ATT_BLOCK, t)
    nb = t // blk

    def body(q_ref, k_ref, v_ref, cq_ref, ck_ref, o_ref, lse_ref):
        qi = pl.program_id(1)
        qv, cq = q_ref[...], cq_ref[...]

        def step(j, carry, diagonal):
            m, l, acc = carry
            off = pl.multiple_of(j * blk, blk)
            s = _att_scores(qv, k_ref[pl.ds(off, blk), :], cq, ck_ref[:, pl.ds(off, blk)], diagonal, blk)
            m_new = jnp.maximum(m, jnp.max(s, axis=1, keepdims=True))
            alpha = jnp.exp(m - m_new)
            p = jnp.exp(s - m_new)
            l = alpha * l + jnp.sum(p, axis=1, keepdims=True)
            acc = alpha * acc + _dot(p.astype(BF16), v_ref[pl.ds(off, blk), :])
            return m_new, l, acc

        init = (jnp.full((blk, 1), NEG, F32), jnp.zeros((blk, 1), F32), jnp.zeros((blk, hd), F32))
        carry = lax.fori_loop(0, qi, lambda j, cr: step(j, cr, False), init)
        m, l, acc = step(qi, carry, True)
        o_ref[...] = acc / l
        lse_ref[...] = m + jnp.log(l)

    qmap = lambda hh, i: (hh, i, 0)
    whole = lambda hh, i: (hh, 0, 0)
    return pl.pallas_call(
        body, name="fox_fwd", grid=(h, nb),
        in_specs=[pl.BlockSpec((None, blk, hd), qmap), pl.BlockSpec((None, t, hd), whole),
                  pl.BlockSpec((None, t, hd), whole), pl.BlockSpec((None, blk, 1), qmap),
                  pl.BlockSpec((None, 1, t), whole)],
        out_specs=[pl.BlockSpec((None, blk, hd), qmap), pl.BlockSpec((None, blk, 1), qmap)],
        out_shape=[jax.ShapeDtypeStruct((h, t, hd), F32), jax.ShapeDtypeStruct((h, t, 1), F32)],
        compiler_params=_cp("parallel", "arbitrary"),
    )(q, k, v, cf_col, cf_row)


def _fox_bwd(q, k, v, cf_col, cf_row, o, lse, do):
    h, t, hd = q.shape
    blk = min(ATT_BLOCK, t)
    nb = t // blk
    scale = FOX_HD ** -0.5

    def body(q_ref, k_ref, v_ref, cq_ref, ck_ref, o_ref, lse_ref, do_ref,
             dq_ref, dk_ref, dv_ref, dcq_ref, dck_ref, delta_s):
        kj = pl.program_id(1)

        @pl.when(kj == 0)
        def _():
            dq_ref[...] = jnp.zeros_like(dq_ref)
            dcq_ref[...] = jnp.zeros_like(dcq_ref)

            def fill(i, _):
                rows = pl.ds(pl.multiple_of(i * blk, blk), blk)
                delta_s[rows, :] = jnp.sum(do_ref[rows, :] * o_ref[rows, :], axis=1, keepdims=True)
                return 0

            lax.fori_loop(0, nb, fill, 0)

        kb, vb, ck = k_ref[...], v_ref[...], ck_ref[...]

        def step(i, carry, diagonal):
            dk, dv, dck = carry
            rows = pl.ds(pl.multiple_of(i * blk, blk), blk)
            qb = q_ref[rows, :]
            do_b = do_ref[rows, :].astype(BF16)
            s = _att_scores(qb, kb, cq_ref[rows, :], ck, diagonal, blk)
            p = jnp.exp(s - lse_ref[rows, :])
            dv = dv + _dot_tn(p.astype(BF16), do_b)
            ds = p * (_dot_nt(do_b, vb) - delta_s[rows, :])
            ds_b = ds.astype(BF16)
            dk = dk + _dot_tn(ds_b, qb)
            dq_ref[rows, :] += _dot(ds_b, kb) * scale
            dcq_ref[rows, :] += jnp.sum(ds, axis=1, keepdims=True)
            dck = dck - jnp.sum(ds, axis=0, keepdims=True)
            return dk, dv, dck

        init = (jnp.zeros((blk, hd), F32), jnp.zeros((blk, hd), F32), jnp.zeros((1, blk), F32))
        carry = step(kj, init, True)
        dk, dv, dck = lax.fori_loop(kj + 1, nb, lambda i, cr: step(i, cr, False), carry)
        dk_ref[...] = dk * scale
        dv_ref[...] = dv
        dck_ref[...] = dck

    kmap = lambda hh, j: (hh, j, 0)
    whole = lambda hh, j: (hh, 0, 0)
    return pl.pallas_call(
        body, name="fox_bwd", grid=(h, nb),
        in_specs=[pl.BlockSpec((None, t, hd), whole), pl.BlockSpec((None, blk, hd), kmap),
                  pl.BlockSpec((None, blk, hd), kmap), pl.BlockSpec((None, t, 1), whole),
                  pl.BlockSpec((None, 1, blk), lambda hh, j: (hh, 0, j)),
                  pl.BlockSpec((None, t, hd), whole), pl.BlockSpec((None, t, 1), whole),
                  pl.BlockSpec((None, t, hd), whole)],
        out_specs=[pl.BlockSpec((None, t, hd), whole), pl.BlockSpec((None, blk, hd), kmap),
                   pl.BlockSpec((None, blk, hd), kmap), pl.BlockSpec((None, t, 1), whole),
                   pl.BlockSpec((None, 1, blk), lambda hh, j: (hh, 0, j))],
        out_shape=[jax.ShapeDtypeStruct((h, t, hd), F32), jax.ShapeDtypeStruct((h, t, hd), F32),
                   jax.ShapeDtypeStruct((h, t, hd), F32), jax.ShapeDtypeStruct((h, t, 1), F32),
                   jax.ShapeDtypeStruct((h, 1, t), F32)],
        scratch_shapes=[pltpu.VMEM((t, 1), F32)],
        compiler_params=_cp("parallel", "arbitrary"),
    )(q, k, v, cf_col, cf_row, o, lse, do)


def _lane_col(v, h):
    lane = lax.broadcasted_iota(jnp.int32, v.shape, 1)
    return jnp.sum(jnp.where(lane == h, v, 0.0), axis=1, keepdims=True)


def _sub_row(v, h):
    sub = lax.broadcasted_iota(jnp.int32, v.shape, 0)
    return jnp.sum(jnp.where(sub == h, v, 0.0), axis=0, keepdims=True)


def _ssd_decays(pad, pad_t, dtb_row, alog_row, dtb_col, alog_col, blk):
    r = lax.broadcasted_iota(jnp.int32, (blk, blk), 0)
    c = lax.broadcasted_iota(jnp.int32, (blk, blk), 1)
    tril = r >= c
    dt_c = _softplus(pad + dtb_row)
    acs_c = _tri_left(tril.astype(BF16), dt_c * (-jnp.exp(alog_row)))
    dt_r = _softplus(pad_t + dtb_col)
    acs_r = _tri_right(dt_r * (-jnp.exp(alog_col)), (r <= c).astype(BF16))
    rows = lax.broadcasted_iota(jnp.int32, acs_c.shape, 0)
    acs_last = jnp.sum(jnp.where(rows == blk - 1, acs_c, 0.0), axis=0, keepdims=True)
    return dt_c, acs_c, acs_r, acs_last, tril


def _pair_terms(pair, dt_c, acs_c, acs_r, acs_last, d_row, blk):
    lane = lax.broadcasted_iota(jnp.int32, (blk, LANES), 1)
    lo = lane < 64
    lo_row = lax.broadcasted_iota(jnp.int32, (1, LANES), 1) < 64
    h0, h1 = 2 * pair, 2 * pair + 1
    col = [_lane_col(acs_c, h0), _lane_col(acs_c, h1)]
    row = [_sub_row(acs_r, h0), _sub_row(acs_r, h1)]
    last = [_lane_col(acs_last, h0), _lane_col(acs_last, h1)]
    dt_p = jnp.where(lo, _lane_col(dt_c, h0), _lane_col(dt_c, h1))
    e_p = jnp.where(lo, jnp.exp(col[0]), jnp.exp(col[1]))
    w_p = jnp.where(lo, jnp.exp(last[0] - col[0]), jnp.exp(last[1] - col[1]))
    decay_p = jnp.where(lo_row, jnp.exp(last[0]), jnp.exp(last[1]))
    d_p = jnp.where(lo_row, _lane_col(d_row, h0), _lane_col(d_row, h1))
    return lo, lo_row, col, row, last, dt_p, e_p, w_p, decay_p, d_p


def _ssd_specs(t, blk, rev):
    nc = t // blk
    ix = (lambda i: nc - 1 - i) if rev else (lambda i: i)
    xbc = pl.BlockSpec((blk, SSD_CONV_DIM), lambda i: (ix(i), 0))
    pad = pl.BlockSpec((blk, LANES), lambda i: (ix(i), P_PAD // LANES))
    pad_t = pl.BlockSpec((LANES, blk), lambda i: (0, ix(i)))
    z = pl.BlockSpec((blk, SSD_W), lambda i: (ix(i), 0))
    row = pl.BlockSpec((1, LANES), lambda i: (0, 0))
    colv = pl.BlockSpec((LANES, 1), lambda i: (0, 0))
    ng = pl.BlockSpec((1, SSD_W), lambda i: (0, 0))
    y = pl.BlockSpec((blk, SSD_W), lambda i: (ix(i), 0))
    st = pl.BlockSpec((None, 4, LANES, LANES), lambda i: (ix(i), 0, 0, 0))
    return nc, xbc, pad, pad_t, z, row, colv, ng, y, st


def _ssd_fwd(xbc, proj, pad_t, dtb_row, alog_row, d_row, dtb_col, alog_col, ng):
    t = xbc.shape[0]
    blk = min(SSD_CHUNK, t)
    nc, s_xbc, s_pad, s_padt, s_z, s_row, s_col, s_ng, s_y, s_st = _ssd_specs(t, blk, False)

    def body(xbc_ref, pad_ref, padt_ref, z_ref, dtb_ref, alog_ref, d_ref, dtbc_ref, alogc_ref, ng_ref,
             out_ref, ypre_ref, st_ref, state):
        @pl.when(pl.program_id(0) == 0)
        def _():
            state[...] = jnp.zeros_like(state)

        dt_c, acs_c, acs_r, acs_last, tril = _ssd_decays(
            pad_ref[...], padt_ref[...], dtb_ref[...], alog_ref[...], dtbc_ref[...], alogc_ref[...], blk)
        ys = []
        g_mat = {}
        for pair in range(4):
            g = pair // 2
            bg = xbc_ref[:, SSD_W + LANES * g:SSD_W + LANES * (g + 1)].astype(BF16)
            cg = xbc_ref[:, SSD_W + 2 * LANES + LANES * g:SSD_W + 2 * LANES + LANES * (g + 1)].astype(BF16)
            if g not in g_mat:
                g_mat[g] = _dot_nt(cg, bg)
            xs_p = xbc_ref[:, LANES * pair:LANES * (pair + 1)]
            lo, _, col, row, _, dt_p, e_p, w_p, decay_p, d_p = _pair_terms(
                pair, dt_c, acs_c, acs_r, acs_last, d_ref[...], blk)
            x_p = xs_p * dt_p
            y = None
            for hh in range(2):
                lm = jnp.exp(jnp.where(tril, col[hh] - row[hh], NEG))
                m_h = (g_mat[g] * lm).astype(BF16)
                x_h = jnp.where(lo if hh == 0 else ~lo, x_p, 0.0).astype(BF16)
                y_h = _dot(m_h, x_h)
                y = y_h if y is None else y + y_h
            st_in = state[pair]
            st_ref[pair] = st_in
            y = y + e_p * _dot(cg, st_in.astype(BF16))
            state[pair] = decay_p * st_in + _dot_tn(bg, (x_p * w_p).astype(BF16))
            ys.append(y + d_p * xs_p)
        y_all = jnp.concatenate(ys, axis=1)
        ypre_ref[...] = y_all
        z = z_ref[...]
        y2 = y_all * (z * _sigmoid(z))
        outs = []
        for g in range(2):
            seg = y2[:, 256 * g:256 * (g + 1)]
            rr = lax.rsqrt(jnp.mean(seg * seg, axis=-1, keepdims=True) + RMS_EPS)
            outs.append(seg * rr * ng_ref[:, 256 * g:256 * (g + 1)])
        out_ref[...] = jnp.concatenate(outs, axis=1).astype(BF16)

    return pl.pallas_call(
        body, name="ssd_scan_fwd", grid=(nc,),
        in_specs=[s_xbc, s_pad, s_padt, s_z, s_row, s_row, s_row, s_col, s_col, s_ng],
        out_specs=[s_y, s_y, s_st],
        out_shape=[jax.ShapeDtypeStruct((t, SSD_W), BF16), jax.ShapeDtypeStruct((t, SSD_W), F32),
                   jax.ShapeDtypeStruct((nc, 4, LANES, LANES), F32)],
        scratch_shapes=[pltpu.VMEM((4, LANES, LANES), F32)],
        compiler_params=_cp("arbitrary"),
    )(xbc, proj, pad_t, proj, dtb_row, alog_row, d_row, dtb_col, alog_col, ng)


def _ssd_bwd(dout, dout_off, xbc, proj, pad_t, ypre, states, dtb_row, alog_row, d_row, dtb_col, alog_col, ng):
    t = xbc.shape[0]
    blk = min(SSD_CHUNK, t)
    nc, s_xbc, s_pad, s_padt, s_z, s_row, s_col, s_ng, s_y, s_st = _ssd_specs(t, blk, True)
    s_dout = pl.BlockSpec((blk, SSD_W), lambda i: (nc - 1 - i, dout_off // SSD_W))

    def body(dout_ref, xbc_ref, pad_ref, padt_ref, z_ref, ypre_ref, st_ref, dtb_ref, alog_ref, d_ref,
             dtbc_ref, alogc_ref, ng_ref, dxbc_ref, ddt_ref, dz_ref, acc_ref, dng_ref, dstate):
        @pl.when(pl.program_id(0) == 0)
        def _():
            dstate[...] = jnp.zeros_like(dstate)
            acc_ref[...] = jnp.zeros_like(acc_ref)
            dng_ref[...] = jnp.zeros_like(dng_ref)

        pad = pad_ref[...]
        dt_c, acs_c, acs_r, acs_last, tril = _ssd_decays(
            pad, padt_ref[...], dtb_ref[...], alog_ref[...], dtbc_ref[...], alogc_ref[...], blk)
        a_row = -jnp.exp(alog_ref[...])

        z = z_ref[...]
        sz = _sigmoid(z)
        silu_z = z * sz
        y_pre = ypre_ref[...]
        y2 = y_pre * silu_z
        dy2 = []
        for g in range(2):
            sl = slice(256 * g, 256 * (g + 1))
            seg = y2[:, sl]
            rr = lax.rsqrt(jnp.mean(seg * seg, axis=-1, keepdims=True) + RMS_EPS)
            nrm = seg * rr
            d_seg = dout_ref[:, sl]
            dng_ref[:, sl] += jnp.sum(d_seg * nrm, axis=0, keepdims=True)
            dn = d_seg * ng_ref[:, sl]
            dy2.append(rr * (dn - nrm * jnp.mean(dn * nrm, axis=-1, keepdims=True)))
        dy2 = jnp.concatenate(dy2, axis=1)
        dz_ref[...] = (dy2 * y_pre * (sz * (1.0 + z * (1.0 - sz)))).astype(BF16)
        dy_all = dy2 * silu_z

        lane_row = lax.broadcasted_iota(jnp.int32, (1, LANES), 1)
        lane_blk = lax.broadcasted_iota(jnp.int32, (blk, LANES), 1)
        row_col = lax.broadcasted_iota(jnp.int32, (blk, 1), 0)
        ddt = jnp.zeros((blk, LANES), F32)
        dacs = jnp.zeros((blk, LANES), F32)
        dd_row = jnp.zeros((1, LANES), F32)
        ones_b = jnp.ones((blk, LANES), BF16)
        dxs = []
        d_b = [None, None]
        d_c = [None, None]
        d_g = [None, None]
        bgs, cgs = {}, {}
        g_mat = {}
        for pair in range(4):
            g = pair // 2
            if g not in g_mat:
                bgs[g] = xbc_ref[:, SSD_W + LANES * g:SSD_W + LANES * (g + 1)].astype(BF16)
                cgs[g] = xbc_ref[:, SSD_W + 2 * LANES + LANES * g:SSD_W + 2 * LANES + LANES * (g + 1)].astype(BF16)
                g_mat[g] = _dot_nt(cgs[g], bgs[g])
            bg, cg = bgs[g], cgs[g]
            xs_p = xbc_ref[:, LANES * pair:LANES * (pair + 1)]
            lo, lo_row, col, row, last, dt_p, e_p, w_p, decay_p, d_p = _pair_terms(
                pair, dt_c, acs_c, acs_r, acs_last, d_ref[...], blk)
            x_p = xs_p * dt_p
            dy_p = dy_all[:, LANES * pair:LANES * (pair + 1)]
            st_in = st_ref[pair]
            dst = dstate[pair]
            dx_diag = None
            for hh in range(2):
                sel = lo if hh == 0 else ~lo
                lm = jnp.exp(jnp.where(tril, col[hh] - row[hh], NEG))
                m_f = g_mat[g] * lm
                m_h = m_f.astype(BF16)
                x_h = jnp.where(sel, x_p, 0.0).astype(BF16)
                dy_h = jnp.where(sel, dy_p, 0.0).astype(BF16)
                dxd = _dot_tn(m_h, dy_h)
                dm = _dot_nt(dy_h, x_h)
                dg_h = dm * lm
                p_b = (dm * m_f).astype(BF16)
                dacs = dacs + jnp.where(lane_blk == 2 * pair + hh, _dot(p_b, ones_b) - _dot_tn(p_b, ones_b), 0.0)
                dx_diag = dxd if dx_diag is None else dx_diag + dxd
                d_g[g] = dg_h if d_g[g] is None else d_g[g] + dg_h
            st_b = st_in.astype(BF16)
            dst_b = dst.astype(BF16)
            y_off = e_p * _dot(cg, st_b)
            edy = (e_p * dy_p).astype(BF16)
            dc_off = _dot_nt(edy, st_b)
            d_c[g] = dc_off if d_c[g] is None else d_c[g] + dc_off
            dstate[pair] = decay_p * dst + _dot_tn(cg, edy)
            dx_state = _dot(bg, dst_b) * w_p
            db_st = _dot_nt((x_p * w_p).astype(BF16), dst_b)
            d_b[g] = db_st if d_b[g] is None else d_b[g] + db_st
            dx = dx_diag + dx_state
            dxs.append(dx * dt_p + d_p * dy_p)
            prod_dt = dx * xs_p
            prod_acs = dy_p * y_off - x_p * dx_state
            prod_st = x_p * dx_state
            prod_d = dy_p * xs_p
            st_prod = jnp.sum(dst * st_in, axis=0, keepdims=True)
            for hh in range(2):
                h = 2 * pair + hh
                sel = lo if hh == 0 else ~lo
                sel_row = lo_row if hh == 0 else ~lo_row
                ddt_h = jnp.sum(jnp.where(sel, prod_dt, 0.0), axis=1, keepdims=True)
                dacs_h = jnp.sum(jnp.where(sel, prod_acs, 0.0), axis=1, keepdims=True)
                tail = jnp.sum(jnp.sum(jnp.where(sel, prod_st, 0.0), axis=1, keepdims=True), axis=0, keepdims=True)
                tail = tail + jnp.exp(last[hh]) * jnp.sum(jnp.where(sel_row, st_prod, 0.0), axis=1, keepdims=True)
                dacs_h = dacs_h + jnp.where(row_col == blk - 1, tail, 0.0)
                dd_h = jnp.sum(jnp.sum(jnp.where(sel, prod_d, 0.0), axis=1, keepdims=True), axis=0, keepdims=True)
                ddt = ddt + jnp.where(lane_blk == h, ddt_h, 0.0)
                dacs = dacs + jnp.where(lane_blk == h, dacs_h, 0.0)
                dd_row = dd_row + jnp.where(lane_row == h, dd_h, 0.0)
        for g in range(2):
            dg_b = d_g[g].astype(BF16)
            d_c[g] = d_c[g] + _dot(dg_b, bgs[g])
            d_b[g] = d_b[g] + _dot_tn(dg_b, cgs[g])
        r = lax.broadcasted_iota(jnp.int32, (blk, blk), 0)
        c = lax.broadcasted_iota(jnp.int32, (blk, blk), 1)
        da = _tri_left((r <= c).astype(BF16), dacs)
        ddt = ddt + da * a_row
        d_raw = ddt * _sigmoid(pad + dtb_ref[...])
        ddt_ref[...] = d_raw
        acc_ref[0:1, :] += jnp.sum(da * dt_c, axis=0, keepdims=True) * a_row
        acc_ref[1:2, :] += dd_row
        acc_ref[2:3, :] += jnp.sum(d_raw, axis=0, keepdims=True)
        dxbc_ref[...] = jnp.concatenate(dxs + d_b + d_c, axis=1)

    return pl.pallas_call(
        body, name="ssd_scan_bwd", grid=(nc,),
        in_specs=[s_dout, s_xbc, s_pad, s_padt, s_z, s_y, s_st, s_row, s_row, s_row, s_col, s_col, s_ng],
        out_specs=[s_xbc, pl.BlockSpec((blk, LANES), lambda i: (nc - 1 - i, 0)), s_y,
                   pl.BlockSpec((8, LANES), lambda i: (0, 0)), s_ng],
        out_shape=[jax.ShapeDtypeStruct((t, SSD_CONV_DIM), F32), jax.ShapeDtypeStruct((t, LANES), F32),
                   jax.ShapeDtypeStruct((t, SSD_W), BF16), jax.ShapeDtypeStruct((8, LANES), F32),
                   jax.ShapeDtypeStruct((1, SSD_W), F32)],
        scratch_shapes=[pltpu.VMEM((4, LANES, LANES), F32)],
        compiler_params=_cp("arbitrary"),
    )(dout, xbc, proj, pad_t, proj, ypre, states, dtb_row, alog_row, d_row, dtb_col, alog_col, ng)


def _pad_lanes(v, off):
    return jnp.zeros((1, LANES), F32).at[0, off:off + v.shape[0]].set(v)


def _perm_mix_w_in(w):
    z, xbc, dt = w[:, 0:512], w[:, 512:1536], w[:, 1536:1544]
    qkv, f, sc = w[:, 1544:2312], w[:, 2312:2316], w[:, 2316:3084]
    padblk = jnp.zeros((w.shape[0], LANES), w.dtype).at[:, PAD_DT0:PAD_DT0 + 8].set(dt).at[:, PAD_F0:PAD_F0 + 4].set(f)
    return jnp.concatenate([z, xbc, qkv, sc, padblk], axis=1)


def _unperm_mix_w_in(wp):
    z, xbc, qkv, sc = wp[:, 0:512], wp[:, 512:1536], wp[:, 1536:2304], wp[:, 2304:3072]
    dt, f = wp[:, P_PAD + PAD_DT0:P_PAD + PAD_DT0 + 8], wp[:, P_PAD + PAD_F0:P_PAD + PAD_F0 + 4]
    return jnp.concatenate([z, xbc, dt, qkv, f, sc], axis=1)


def _heads(m):
    return jnp.transpose(m.reshape(m.shape[0], FOX_HEADS, FOX_HD), (1, 0, 2))


def _unheads(m):
    return jnp.transpose(m, (1, 0, 2)).reshape(m.shape[1], FOX_W)


def _ffn_fwd(h, w_in, w_out, tag):
    u, a = _ffn_in_swiglu(h, w_in, f"ffn_in_{tag}")
    y = _matmul(a, w_out, "nn", F32, f"ffn_out_{tag}")
    return y, (h, u, a)


def _ffn_bwd(dy, saved, w_in, w_out, tag):
    h, u, a = saved
    du = _ffn_out_dx_swiglu(dy, w_out, u, f"ffn_out_dx_{tag}")
    dw_out = _matmul(a, dy, "tn", BF16, f"ffn_out_dw_{tag}")
    dh = _matmul(du, w_in, "nt", F32, f"ffn_in_dx_{tag}", halves="a")
    dw_in = _matmul(h, du, "tn", BF16, f"ffn_in_dw_{tag}", halves="b")
    return dh, dw_in, dw_out


def _mix_fwd(h, wp, w_out, sp, tag):
    proj = _matmul(h, wp, "nn", F32, f"mix_in_{tag}")
    pad_t = jnp.transpose(proj[:, P_PAD:P_PAD + LANES])
    xbc = _ssd_conv_fwd(proj, sp["conv_w"], sp["conv_b"])
    y_ssd, ypre, states = _ssd_fwd(xbc, proj, pad_t, sp["dtb_row"], sp["alog_row"], sp["d_row"],
                                   sp["dtb_col"], sp["alog_col"], sp["ng"])
    cf = _cum_logf(proj, sp["fbias_row"])
    cf4 = jnp.transpose(cf[:, PAD_F0:PAD_F0 + FOX_HEADS])
    cf_col, cf_row = cf4[:, :, None], cf4[:, None, :]
    q = _heads(proj[:, P_QKV:P_QKV + 256].astype(BF16))
    k = _heads(proj[:, P_QKV + 256:P_QKV + 512].astype(BF16))
    v = _heads(proj[:, P_QKV + 512:P_QKV + 768].astype(BF16))
    o, lse = _fox_fwd(q, k, v, cf_col, cf_row)
    y_sc = _shortconv_fwd(proj, sp["sconv_w"])
    ymix = jnp.concatenate([y_ssd, _unheads(o).astype(BF16), y_sc], axis=1)
    y = _matmul(ymix, w_out, "nn", F32, f"mix_out_{tag}")
    return y, (h, proj, pad_t, xbc, ypre, states, q, k, v, cf_col, cf_row, o, lse, ymix)


def _mix_bwd(dy, saved, wp, w_out, sp, tag):
    h, proj, pad_t, xbc, ypre, states, q, k, v, cf_col, cf_row, o, lse, ymix = saved
    dymix = _matmul(dy, w_out, "nt", F32, f"mix_out_dx_{tag}")
    dw_out = _matmul(ymix, dy, "tn", BF16, f"mix_out_dw_{tag}")
    dxbc, ddt, dz, ssd_acc, dng = _ssd_bwd(dymix, 0, xbc, proj, pad_t, ypre, states, sp["dtb_row"],
                                           sp["alog_row"], sp["d_row"], sp["dtb_col"], sp["alog_col"], sp["ng"])
    dxbc_raw, dconv_w, dconv_b = _ssd_conv_bwd(dxbc, proj, sp["conv_w"], sp["conv_b"])
    do = _heads(dymix[:, SSD_W:SSD_W + FOX_W])
    dq, dk, dv, dcq, dck = _fox_bwd(q, k, v, cf_col, cf_row, o, lse, do)
    dcf4 = dcq[:, :, 0] + dck[:, 0, :]
    dcf = jnp.zeros((h.shape[0], LANES), F32).at[:, PAD_F0:PAD_F0 + FOX_HEADS].set(jnp.transpose(dcf4))
    dpad, dfb = _pad_block_bwd(dcf, ddt, proj, sp["fbias_row"])
    dscb, dscc, dscx, dsconv_w = _shortconv_bwd(dymix, SSD_W + FOX_W, proj, sp["sconv_w"])
    dproj = jnp.concatenate([dz, dxbc_raw.astype(BF16), _unheads(dq).astype(BF16), _unheads(dk).astype(BF16),
                             _unheads(dv).astype(BF16), dscb, dscc, dscx, dpad], axis=1)
    dh = _matmul(dproj, wp, "nt", F32, f"mix_in_dx_{tag}")
    dwp = _matmul(h, dproj, "tn", BF16, f"mix_in_dw_{tag}")
    small = dict(conv_w=dconv_w, conv_b=dconv_b[0], dt_bias=ssd_acc[2, 0:8], a_log=ssd_acc[0, 0:8],
                 d=ssd_acc[1, 0:8], norm_g=dng[0], f_bias=dfb[0, PAD_F0:PAD_F0 + FOX_HEADS], sconv_w=dsconv_w)
    return dh, _unperm_mix_w_in(dwp), dw_out, small


BIG = ("ffn1_w_in", "ffn1_w_out", "mix_w_in", "mix_w_out", "ffn2_w_in", "ffn2_w_out")
SHARD_AXIS = {"ffn1_w_in": 1, "ffn1_w_out": 0, "mix_w_in": 1, "mix_w_out": 0, "ffn2_w_in": 1, "ffn2_w_out": 0}


def _local_step(x, tgt, mod, wfull, small_p):
    row = lambda v: v.reshape(1, -1)
    subs = [(l, j) for l in range(DEPTH) for j in range(N_SUB)]
    factor = (0.5, 1.0, 0.5)
    w_names = (("ffn1_w_in", "ffn1_w_out"), ("mix_w_in", "mix_w_out"), ("ffn2_w_in", "ffn2_w_out"))
    lg = [[row(small_p["ln_g"][l, j]) for j in range(N_SUB)] for l in range(DEPTH)]
    lb = [[row(small_p["ln_b"][l, j]) for j in range(N_SUB)] for l in range(DEPTH)]
    sps, wps = [], []
    for l in range(DEPTH):
        w = wfull[l]
        sp = dict(
            conv_w=small_p["ssd_conv_w"][l], conv_b=row(small_p["ssd_conv_b"][l]),
            dtb_row=_pad_lanes(small_p["ssd_dt_bias"][l], PAD_DT0), alog_row=_pad_lanes(small_p["ssd_a_log"][l], 0),
            d_row=_pad_lanes(small_p["ssd_d"][l], 0), ng=row(small_p["ssd_norm_g"][l]),
            fbias_row=_pad_lanes(small_p["fox_f_bias"][l], PAD_F0), sconv_w=small_p["sconv_w"][l])
        sp["dtb_col"] = jnp.transpose(sp["dtb_row"])
        sp["alog_col"] = jnp.transpose(sp["alog_row"])
        sps.append(sp)
        wps.append(_perm_mix_w_in(w["mix_w_in"]))
    tags = [f"l{l}{('f1', 'mx', 'f2')[j]}" for l, j in subs]

    x0, h = _ln_in_fwd(x, row(small_p["ln_in_g"]), row(small_p["ln_in_b"]), mod[0, 0])
    cur = x0
    xins, ys, inner = [], [], []
    for idx, (l, j) in enumerate(subs):
        w = wfull[l]
        if j == 1:
            y, sv = _mix_fwd(h, wps[l], w["mix_w_out"], sps[l], tags[idx])
        else:
            y, sv = _ffn_fwd(h, w[w_names[j][0]], w[w_names[j][1]], tags[idx])
        nxt = mod[subs[idx + 1]] if idx + 1 < len(subs) else None
        xins.append(cur)
        ys.append(y)
        inner.append(sv)
        cur, h = _res_ln(cur, y, mod[l, j], lg[l][j], lb[l][j], factor[j], f"res_ln_{tags[idx]}", nxt)

    dcur, loss_acc = _loss_head(cur, tgt)
    last = len(subs) - 1
    l, j = subs[last]
    dres, dy, acc = _res_ln_bwd(dcur, xins[last], ys[last], mod[l, j], lg[l][j], factor[j], f"res_ln_bwd_{tags[last]}")
    ln_acc = {last: acc}
    shift_scale = {}
    big_grads = [dict() for _ in range(DEPTH)]
    small_g = [None] * DEPTH
    for idx in reversed(range(len(subs))):
        l, j = subs[idx]
        w = wfull[l]
        if j == 1:
            dh, g_in, g_out, small_g[l] = _mix_bwd(dy, inner[idx], wps[l], w["mix_w_out"], sps[l], tags[idx])
        else:
            dh, g_in, g_out = _ffn_bwd(dy, inner[idx], w[w_names[j][0]], w[w_names[j][1]], tags[idx])
        big_grads[l][w_names[j][0]], big_grads[l][w_names[j][1]] = g_in, g_out
        if idx > 0:
            pl_, pj = subs[idx - 1]
            dres, dy, acc5 = _mod_res_bwd(dres, dh, mod[l, j], xins[idx - 1], ys[idx - 1], mod[pl_, pj], lg[pl_][pj],
                                          lb[pl_][pj], factor[pj], f"mod_res_bwd_{tags[idx]}")
            shift_scale[idx], ln_acc[idx - 1] = acc5[0:2], acc5[2:5]
        else:
            dx0, shift_scale[0] = _modulate_bwd(dres, dh, x0, mod[0, 0], "mod_bwd_first")
    dx, acc_in = _ln_in_bwd(dx0, x, row(small_p["ln_in_g"]))
    dmod = []
    for l in range(DEPTH):
        ids = [N_SUB * l + j for j in range(N_SUB)]
        small_g[l]["ln_g"] = jnp.stack([ln_acc[i][0] for i in ids])
        small_g[l]["ln_b"] = jnp.stack([ln_acc[i][1] for i in ids])
        dmod.append(jnp.stack([jnp.concatenate([shift_scale[i], ln_acc[i][2:3]], axis=0) for i in ids]))
    return loss_acc[0, 0], dx, big_grads, small_g, jnp.stack(dmod), acc_in


SMALL_ORDER = ("ssd_conv_w", "ssd_conv_b", "ssd_dt_bias", "ssd_a_log", "ssd_d", "ssd_norm_g", "fox_f_bias",
               "sconv_w", "ln_g", "ln_b")
SMALL_KEY = dict(ssd_conv_w="conv_w", ssd_conv_b="conv_b", ssd_dt_bias="dt_bias", ssd_a_log="a_log", ssd_d="d",
                 ssd_norm_g="norm_g", fox_f_bias="f_bias", sconv_w="sconv_w", ln_g="ln_g", ln_b="ln_b")
COL_SHARDED_SMALL = ("ssd_conv_w", "sconv_w", "ln_g", "ln_b")


def _pad_to(v, n):
    return jnp.concatenate([v, jnp.zeros((n - v.shape[0],), v.dtype)])


def kernel(x, c, ln_in_g, ln_in_b, ada_w, ada_b, ffn1_w_in, ffn1_w_out, mix_w_in, mix_w_out, ssd_conv_w, ssd_conv_b, ssd_dt_bias, ssd_a_log, ssd_d, ssd_norm_g, fox_f_bias, sconv_w, ffn2_w_in, ffn2_w_out, ln_g, ln_b, loss_target, m_ln_in_g, m_ln_in_b, m_ada_w, m_ada_b, m_ffn1_w_in, m_ffn1_w_out, m_mix_w_in, m_mix_w_out, m_ssd_conv_w, m_ssd_conv_b, m_ssd_dt_bias, m_ssd_a_log, m_ssd_d, m_ssd_norm_g, m_fox_f_bias, m_sconv_w, m_ffn2_w_in, m_ffn2_w_out, m_ln_g, m_ln_b, v_ln_in_g, v_ln_in_b, v_ada_w, v_ada_b, v_ffn1_w_in, v_ffn1_w_out, v_mix_w_in, v_mix_w_out, v_ssd_conv_w, v_ssd_conv_b, v_ssd_dt_bias, v_ssd_a_log, v_ssd_d, v_ssd_norm_g, v_fox_f_bias, v_sconv_w, v_ffn2_w_in, v_ffn2_w_out, v_ln_g, v_ln_b):
    names = ("ln_in_g", "ln_in_b", "ada_w", "ada_b", "ffn1_w_in", "ffn1_w_out", "mix_w_in", "mix_w_out",
             "ssd_conv_w", "ssd_conv_b", "ssd_dt_bias", "ssd_a_log", "ssd_d", "ssd_norm_g", "fox_f_bias", "sconv_w",
             "ffn2_w_in", "ffn2_w_out", "ln_g", "ln_b")
    w_loc = dict(zip(names, (ln_in_g, ln_in_b, ada_w, ada_b, ffn1_w_in, ffn1_w_out, mix_w_in, mix_w_out, ssd_conv_w,
                             ssd_conv_b, ssd_dt_bias, ssd_a_log, ssd_d, ssd_norm_g, fox_f_bias, sconv_w, ffn2_w_in,
                             ffn2_w_out, ln_g, ln_b)))
    m_loc = dict(zip(names, (m_ln_in_g, m_ln_in_b, m_ada_w, m_ada_b, m_ffn1_w_in, m_ffn1_w_out, m_mix_w_in,
                             m_mix_w_out, m_ssd_conv_w, m_ssd_conv_b, m_ssd_dt_bias, m_ssd_a_log, m_ssd_d,
                             m_ssd_norm_g, m_fox_f_bias, m_sconv_w, m_ffn2_w_in, m_ffn2_w_out, m_ln_g, m_ln_b)))
    v_loc = dict(zip(names, (v_ln_in_g, v_ln_in_b, v_ada_w, v_ada_b, v_ffn1_w_in, v_ffn1_w_out, v_mix_w_in,
                             v_mix_w_out, v_ssd_conv_w, v_ssd_conv_b, v_ssd_dt_bias, v_ssd_a_log, v_ssd_d,
                             v_ssd_norm_g, v_fox_f_bias, v_sconv_w, v_ffn2_w_in, v_ffn2_w_out, v_ln_g, v_ln_b)))

    xi, yi, ci = lax.axis_index("x"), lax.axis_index("y"), lax.axis_index("c")
    me = 2 * xi + yi
    dev = 2 * me + ci

    def gather8(v, tag):
        v2 = v.reshape(1, -1)
        got = _bcast_chips([v2], f"gather_chips_{tag}")
        same_c = jnp.concatenate(_by_chip(me, v2, [g[0] for g in got]), axis=0)
        other_c = _swap_sibling([same_c], f"gather_sibling_{tag}")[0]
        pair = lax.switch(ci, [lambda a, b: jnp.stack([a, b], axis=1), lambda a, b: jnp.stack([b, a], axis=1)],
                          same_c, other_c)
        return pair.reshape(8, -1)

    def chip_concat(own, got3, axis):
        return jnp.concatenate(_by_chip(me, own, got3), axis=axis)

    small_cols = [w_loc[n].reshape(-1, w_loc[n].shape[-1]) for n in COL_SHARDED_SMALL]
    got = _bcast_chips(small_cols, "gather_small_params")
    small_p = {n: w_loc[n] for n in ("ln_in_g", "ln_in_b", "ssd_conv_b", "ssd_dt_bias", "ssd_a_log", "ssd_d",
                                     "ssd_norm_g", "fox_f_bias")}
    for i, n in enumerate(COL_SHARDED_SMALL):
        full = chip_concat(small_cols[i], [g[i] for g in got], 1)
        small_p[n] = full.reshape(w_loc[n].shape[:-1] + (full.shape[-1],))

    mine = [lax.dynamic_index_in_dim(w_loc[n], ci, 0, keepdims=False).astype(BF16) for n in BIG]
    got = _bcast_chips(mine, "gather_weights_chips")
    my_layer = [chip_concat(mine[i], [g[i] for g in got], SHARD_AXIS[n]) for i, n in enumerate(BIG)]
    other_layer = _swap_sibling(my_layer, "gather_weights_sibling")
    layers = lax.switch(ci, [lambda a, b: (a, b), lambda a, b: (b, a)], my_layer, other_layer)
    wfull = [dict(zip(BIG, layers[l])) for l in range(DEPTH)]

    c_all = gather8(c[0], "c")
    c_act = _silu_bf16(c_all)
    ada_w_b = ada_w.astype(BF16)
    mod_loc = [_matmul(c_act, ada_w_b[l], "nn", F32, f"ada_fwd_l{l}") for l in range(DEPTH)]
    mod_loc = jnp.stack(mod_loc)
    got = _bcast_chips([mod_loc], "gather_mod")
    mod_all = chip_concat(mod_loc, [g[0] for g in got], 2)
    mod = lax.dynamic_index_in_dim(mod_all, dev, 1, keepdims=False) + ada_b
    mod = mod.reshape(DEPTH, N_SUB, 3, D)

    loss_part, dx, big_g, small_g, dmod, acc_in = _local_step(x[0], loss_target[0], mod, wfull, small_p)
    loss = lax.psum(loss_part, ("x", "y", "c"))

    pieces = [dmod.reshape(-1), acc_in[0], acc_in[1]]
    for n in SMALL_ORDER:
        pieces.append(jnp.stack([small_g[l][SMALL_KEY[n]] for l in range(DEPTH)]).reshape(-1))
    sizes = [p.shape[0] for p in pieces]
    total = sum(sizes)
    padded = -(-total // 1024) * 1024
    vec = _pad_to(jnp.concatenate(pieces), padded)
    all_rows = gather8(vec, "small_grads")
    summed = _sum_rows(all_rows, "sum_small_grads")[0]
    offs = [0]
    for s in sizes:
        offs.append(offs[-1] + s)
    n_mod = sizes[0]
    grads = {"ada_b": summed[0:n_mod].reshape(DEPTH, 3 * N_SUB * D),
             "ln_in_g": summed[offs[1]:offs[2]], "ln_in_b": summed[offs[2]:offs[3]]}
    for i, n in enumerate(SMALL_ORDER):
        full = summed[offs[3 + i]:offs[4 + i]].reshape(small_p[n].shape)
        if n in COL_SHARDED_SMALL:
            wcols = w_loc[n].shape[-1]
            full = lax.dynamic_slice_in_dim(full, me * wcols, wcols, axis=full.ndim - 1)
        grads[n] = full

    dmod_all = all_rows[:, 0:n_mod].reshape(8, DEPTH, 3 * N_SUB * D)
    ncol = ada_w.shape[-1]
    dmod_cols = lax.dynamic_slice_in_dim(dmod_all, me * ncol, ncol, axis=2).astype(BF16)
    grads["ada_w"] = jnp.stack([_matmul(c_act, dmod_cols[:, l], "tn", F32, f"ada_bwd_l{l}") for l in range(DEPTH)])

    give = lax.switch(ci, [lambda g0, g1: [g1[n] for n in BIG], lambda g0, g1: [g0[n] for n in BIG]],
                      big_g[0], big_g[1])
    keep = lax.switch(ci, [lambda g0, g1: [g0[n] for n in BIG], lambda g0, g1: [g1[n] for n in BIG]],
                      big_g[0], big_g[1])
    from_sib = _swap_sibling(give, "grad_swap_sibling")
    chip_part = []
    for i, n in enumerate(BIG):
        chip_part.append(_elementwise(lambda a, b: (a.astype(F32) + b.astype(F32),), f"grad_add_cores_{n}",
                                      [keep[i], from_sib[i]], [BF16])[0])

    def shards_for(mm):
        def f(parts):
            out = []
            for rel in (0, 2, 1, 3):
                tgt_chip = mm ^ rel
                row = []
                for i, n in enumerate(BIG):
                    ax = SHARD_AXIS[n]
                    w = parts[i].shape[ax] // 4
                    row.append(lax.slice_in_dim(parts[i], tgt_chip * w, (tgt_chip + 1) * w, axis=ax))
                out.append(row)
            return out
        return f

    split = lax.switch(me, [shards_for(mm) for mm in range(4)], chip_part)
    got = _scatter_chips(split[1:], "grad_scatter_chips")
    reduced = []
    for i, n in enumerate(BIG):
        four = _by_chip(me, split[0][i], [g[i] for g in got])
        reduced.append(_elementwise(
            lambda a, b, cc, d: (((a.astype(F32) + b.astype(F32)) + cc.astype(F32)) + d.astype(F32),),
            f"grad_add_chips_{n}", four, [F32])[0])
    other = _swap_sibling(reduced, "grad_swap_back")
    per_layer = lax.switch(ci, [lambda a, b: (a, b), lambda a, b: (b, a)], reduced, other)
    for i, n in enumerate(BIG):
        grads[n] = jnp.stack([per_layer[0][i], per_layer[1][i]])

    delta, new_m, new_v = {}, {}, {}
    for n in ("ada_w",) + BIG:
        delta[n], new_m[n], new_v[n] = _adamw(w_loc[n], grads[n], m_loc[n], v_loc[n], f"adamw_{n}")
    small_names = [n for n in names if n not in ("ada_w",) + BIG]
    flat = lambda d: jnp.concatenate([d[n].reshape(-1) for n in small_names])
    n_small = sum(w_loc[n].size for n in small_names)
    n_pad = -(-n_small // 1024) * 1024
    packed = [_pad_to(flat(d), n_pad).reshape(-1, LANES) for d in (w_loc, grads, m_loc, v_loc)]
    d_s, m_s, v_s = _adamw(*packed, "adamw_small")
    off = 0
    for n in small_names:
        sz = w_loc[n].size
        delta[n] = d_s.reshape(-1)[off:off + sz].reshape(w_loc[n].shape)
        new_m[n] = m_s.reshape(-1)[off:off + sz].reshape(w_loc[n].shape)
        new_v[n] = v_s.reshape(-1)[off:off + sz].reshape(w_loc[n].shape)
        off += sz

    return (loss, dx[None], *[grads[n] for n in names], *[delta[n] for n in names],
            *[new_m[n] for n in names], *[new_v[n] for n in names])
```

```python
import functools

import jax
import jax.numpy as jnp
from jax import lax
from jax.experimental import pallas as pl
from jax.experimental.pallas import tpu as pltpu

F32 = jnp.float32
BF16 = jnp.bfloat16
MESH = pl.DeviceIdType.MESH

D = 1024
DEPTH = 2
N_SUB = 3
D_FF = 2816
FF_TILE = D_FF // 2
ALPHA = (2 * DEPTH) ** 0.25
LN_EPS = 1e-5
RMS_EPS = 1e-5
SSD_W = 512
SSD_HEADS = 8
SSD_CONV_K = 4
SSD_CONV_DIM = 1024
FOX_W = 256
FOX_HEADS = 4
FOX_HD = 64
SC_W = 256
SC_K = 3
D_IN_PROJ = 3084
P_Z, P_XBC, P_QKV, P_SC, P_PAD = 0, 512, 1536, 2304, 3072
D_PROJ_PAD = 3200
PAD_DT0, PAD_F0 = 0, 8
SSD_CHUNK = 256
ATT_BLOCK = 512
CUM_BLOCK = 256
LANES = 128
VMEM_LIMIT = 56 * 1024 * 1024

ADAM_LR, ADAM_B1, ADAM_B2, ADAM_EPS, ADAM_WD, ADAM_STEP = 0.001, 0.9, 0.999, 1e-08, 0.01, 10
NEG = -1e30


def _cp(*sem):
    return pltpu.CompilerParams(dimension_semantics=sem, vmem_limit_bytes=VMEM_LIMIT)


def _pick(n, cands):
    for c in cands:
        if n % c == 0:
            return c
    return n


def _dot(a, b):
    return lax.dot_general(a, b, (((1,), (0,)), ((), ())), preferred_element_type=F32)


def _dot_nt(a, b):
    return lax.dot_general(a, b, (((1,), (1,)), ((), ())), preferred_element_type=F32)


def _dot_tn(a, b):
    return lax.dot_general(a, b, (((0,), (0,)), ((), ())), preferred_element_type=F32)


def _sigmoid(x):
    return 1.0 / (1.0 + jnp.exp(-x))


def _softplus(x):
    return jnp.maximum(x, 0.0) + jnp.log(1.0 + jnp.exp(-jnp.abs(x)))


def _split3(v):
    h = v.astype(BF16)
    r = v - h.astype(F32)
    m = r.astype(BF16)
    l = (r - m.astype(F32)).astype(BF16)
    return h, m, l


def _tri_left(tri, v):
    h, m, l = _split3(v)
    return _dot(tri, h) + _dot(tri, m) + _dot(tri, l)


def _tri_right(v, tri):
    h, m, l = _split3(v)
    return _dot(h, tri) + _dot(m, tri) + _dot(l, tri)


def _matmul(a, b, mode, out_dtype, name, halves=None):
    assert a.dtype == BF16 and b.dtype == BF16, (name, a.dtype, b.dtype)
    if halves == "a":
        assert mode == "nt" and a.shape[0] == 2 and a.shape[2] == D_FF
        m, k, n = a.shape[1], 2 * D_FF, b.shape[0]
    elif halves == "b":
        assert mode == "tn" and b.shape[0] == 2 and b.shape[2] == D_FF
        (k, m), n = a.shape, 2 * D_FF
    elif mode == "nn":
        (m, k), n = a.shape, b.shape[1]
    elif mode == "nt":
        (m, k), n = a.shape, b.shape[0]
    else:
        (k, m), n = a.shape, b.shape[1]
    tm = m if m <= 1024 else _pick(m, (1024, 1408, 512, 256, 128))
    tn = n if n <= 1024 else _pick(n, (1408, 640, 512, 256, 128))
    tk = k if k <= 1024 else _pick(k, (1408, 1024, 640, 512, 256, 128))
    nk = k // tk
    if mode == "nn":
        dn = (((1,), (0,)), ((), ()))
        a_spec = pl.BlockSpec((tm, tk), lambda i, j, kk: (i, kk))
        b_spec = pl.BlockSpec((tk, tn), lambda i, j, kk: (kk, j))
    elif mode == "nt":
        dn = (((1,), (1,)), ((), ()))
        a_spec = pl.BlockSpec((tm, tk), lambda i, j, kk: (i, kk))
        b_spec = pl.BlockSpec((tn, tk), lambda i, j, kk: (j, kk))
    else:
        dn = (((0,), (0,)), ((), ()))
        a_spec = pl.BlockSpec((tk, tm), lambda i, j, kk: (kk, i))
        b_spec = pl.BlockSpec((tk, tn), lambda i, j, kk: (kk, j))
    per_half = D_FF // FF_TILE
    if halves == "a":
        assert tk == FF_TILE
        a_spec = pl.BlockSpec((None, tm, tk), lambda i, j, kk: (kk // per_half, i, kk % per_half))
    elif halves == "b":
        assert tn == FF_TILE
        b_spec = pl.BlockSpec((None, tk, tn), lambda i, j, kk: (j // per_half, kk, j % per_half))

    def body(a_ref, b_ref, o_ref, *acc):
        d = lax.dot_general(a_ref[...], b_ref[...], dn, preferred_element_type=F32)
        if nk == 1:
            o_ref[...] = d.astype(o_ref.dtype)
            return
        acc_ref, = acc
        kk = pl.program_id(2)

        @pl.when(kk == 0)
        def _():
            acc_ref[...] = d

        @pl.when((kk > 0) & (kk < nk - 1))
        def _():
            acc_ref[...] += d

        @pl.when(kk == nk - 1)
        def _():
            o_ref[...] = (acc_ref[...] + d).astype(o_ref.dtype)

    return pl.pallas_call(
        body, name=name, grid=(m // tm, n // tn, nk),
        in_specs=[a_spec, b_spec],
        out_specs=pl.BlockSpec((tm, tn), lambda i, j, kk: (i, j)),
        out_shape=jax.ShapeDtypeStruct((m, n), out_dtype),
        scratch_shapes=[pltpu.VMEM((tm, tn), F32)] if nk > 1 else [],
        compiler_params=_cp("parallel", "parallel", "arbitrary"),
    )(a, b)


def _rows(body, name, t, tb, row_in, full_in, row_out, acc_out):
    in_specs, args = [], []
    for r in row_in:
        if isinstance(r, tuple):
            arr, w, j = r
            in_specs.append(pl.BlockSpec((tb, w), functools.partial(lambda i, jj: (i, jj), jj=j)))
            args.append(arr)
        else:
            in_specs.append(pl.BlockSpec((tb, r.shape[1]), lambda i: (i, 0)))
            args.append(r)
    for f in full_in:
        in_specs.append(pl.BlockSpec(f.shape, functools.partial(lambda i, nd: (0,) * nd, nd=f.ndim)))
        args.append(f)
    out_specs = [pl.BlockSpec((tb, c), lambda i: (i, 0)) for c, _ in row_out]
    out_specs += [pl.BlockSpec(s, functools.partial(lambda i, nd: (0,) * nd, nd=len(s))) for s in acc_out]
    out_shape = [jax.ShapeDtypeStruct((t, c), dt) for c, dt in row_out]
    out_shape += [jax.ShapeDtypeStruct(s, F32) for s in acc_out]
    return pl.pallas_call(
        body, name=name, grid=(t // tb,), in_specs=in_specs, out_specs=out_specs, out_shape=out_shape,
        compiler_params=_cp("arbitrary"),
    )(*args)


def _ln_stats(r):
    mu = jnp.mean(r, axis=-1, keepdims=True)
    xc = r - mu
    var = jnp.mean(xc * xc, axis=-1, keepdims=True)
    rstd = lax.rsqrt(var + LN_EPS)
    return xc * rstd, rstd


def _ln_bwd(dout, xhat, rstd, g):
    dxh = dout * g
    m1 = jnp.mean(dxh, axis=-1, keepdims=True)
    m2 = jnp.mean(dxh * xhat, axis=-1, keepdims=True)
    return rstd * (dxh - m1 - xhat * m2)


def _first(i, acc_refs):
    @pl.when(i == 0)
    def _():
        for a in acc_refs:
            a[...] = jnp.zeros_like(a)


def _modulated(xv, m_ref):
    return (xv * (1.0 + m_ref[1:2, :]) + m_ref[0:1, :]).astype(BF16)


def _ln_in_fwd(x, g, b, next_mod3):
    t = x.shape[0]

    def body(x_ref, g_ref, b_ref, m_ref, o_ref, h_ref):
        xhat, _ = _ln_stats(x_ref[...])
        out = xhat * g_ref[...] + b_ref[...]
        o_ref[...] = out
        h_ref[...] = _modulated(out, m_ref)

    return _rows(body, "ln_in_fwd", t, 256, [x], [g, b, next_mod3], [(D, F32), (D, BF16)], [])


def _ln_in_bwd(dx0, x, g):
    t = x.shape[0]

    def body(d_ref, x_ref, g_ref, o_ref, acc_ref):
        _first(pl.program_id(0), [acc_ref])
        xhat, rstd = _ln_stats(x_ref[...])
        d = d_ref[...]
        o_ref[...] = _ln_bwd(d, xhat, rstd, g_ref[...])
        acc_ref[0:1, :] += jnp.sum(d * xhat, axis=0, keepdims=True)
        acc_ref[1:2, :] += jnp.sum(d, axis=0, keepdims=True)

    return _rows(body, "ln_in_bwd", t, 256, [dx0, x], [g], [(D, F32)], [(2, D)])


def _modulate_bwd(dxres, dh, xin, mod3, name):
    t = xin.shape[0]

    def body(r_ref, dh_ref, x_ref, m_ref, o_ref, acc_ref):
        _first(pl.program_id(0), [acc_ref])
        dh_v = dh_ref[...]
        o_ref[...] = r_ref[...] + dh_v * (1.0 + m_ref[1:2, :])
        acc_ref[0:1, :] += jnp.sum(dh_v, axis=0, keepdims=True)
        acc_ref[1:2, :] += jnp.sum(dh_v * x_ref[...], axis=0, keepdims=True)

    return _rows(body, name, t, 256, [dxres, dh, xin], [mod3], [(D, F32)], [(2, D)])


def _ffn_in_swiglu(h, w_in, name):
    t = h.shape[0]
    tm = _pick(t, (512, 256, 128))
    nj = D_FF // FF_TILE

    def body(h_ref, wg_ref, wu_ref, u_ref, a_ref):
        hv = h_ref[...]
        gate = _dot(hv, wg_ref[...])
        up = _dot(hv, wu_ref[...])
        u_ref[0] = gate.astype(BF16)
        u_ref[1] = up.astype(BF16)
        a_ref[...] = (gate * _sigmoid(gate) * up).astype(BF16)

    return pl.pallas_call(
        body, name=name, grid=(nj, t // tm),
        in_specs=[pl.BlockSpec((tm, D), lambda j, i: (i, 0)),
                  pl.BlockSpec((D, FF_TILE), lambda j, i: (0, j)),
                  pl.BlockSpec((D, FF_TILE), lambda j, i: (0, nj + j))],
        out_specs=[pl.BlockSpec((2, tm, FF_TILE), lambda j, i: (0, i, j)),
                   pl.BlockSpec((tm, FF_TILE), lambda j, i: (i, j))],
        out_shape=[jax.ShapeDtypeStruct((2, t, D_FF), BF16), jax.ShapeDtypeStruct((t, D_FF), BF16)],
        compiler_params=_cp("parallel", "parallel"),
    )(h, w_in, w_in)


def _ffn_out_dx_swiglu(dy, w_out, u, name):
    t = dy.shape[0]
    tm = _pick(t, (512, 256, 128))
    nj = D_FF // FF_TILE

    def body(dy_ref, w_ref, u_ref, du_ref):
        da = _dot_nt(dy_ref[...], w_ref[...])
        gate = u_ref[0].astype(F32)
        up = u_ref[1].astype(F32)
        sg = _sigmoid(gate)
        du_ref[0] = (da * up * (sg * (1.0 + gate * (1.0 - sg)))).astype(BF16)
        du_ref[1] = (da * gate * sg).astype(BF16)

    blk3 = pl.BlockSpec((2, tm, FF_TILE), lambda j, i: (0, i, j))
    return pl.pallas_call(
        body, name=name, grid=(nj, t // tm),
        in_specs=[pl.BlockSpec((tm, D), lambda j, i: (i, 0)),
                  pl.BlockSpec((FF_TILE, D), lambda j, i: (j, 0)), blk3],
        out_specs=blk3,
        out_shape=jax.ShapeDtypeStruct((2, t, D_FF), BF16),
        compiler_params=_cp("parallel", "parallel"),
    )(dy, w_out, u)


def _res_ln(xin, y, mod3, lg, lb, factor, name, next_mod3=None):
    t = xin.shape[0]

    def body(x_ref, y_ref, m_ref, g_ref, b_ref, *rest):
        r = ALPHA * x_ref[...] + (factor * m_ref[2:3, :]) * y_ref[...]
        xhat, _ = _ln_stats(r)
        out = xhat * g_ref[...] + b_ref[...]
        if next_mod3 is None:
            rest[0][...] = out
        else:
            rest[1][...] = out
            rest[2][...] = _modulated(out, rest[0])

    if next_mod3 is None:
        return _rows(body, name, t, 256, [xin, y], [mod3, lg, lb], [(D, F32)], [])[0], None
    return _rows(body, name, t, 256, [xin, y], [mod3, lg, lb, next_mod3], [(D, F32), (D, BF16)], [])


def _mod_res_bwd(dxres, dh, mod3, xin_p, y_p, mod3_p, lg_p, lb_p, factor_p, name):
    t = dxres.shape[0]

    def body(r_ref, dh_ref, xp_ref, yp_ref, m_ref, mp_ref, g_ref, b_ref, dres_ref, dy_ref, acc_ref):
        _first(pl.program_id(0), [acc_ref])
        gate = factor_p * mp_ref[2:3, :]
        yv = yp_ref[...]
        xhat, rstd = _ln_stats(ALPHA * xp_ref[...] + gate * yv)
        xin = xhat * g_ref[...] + b_ref[...]
        dh_v = dh_ref[...]
        d = r_ref[...] + dh_v * (1.0 + m_ref[1:2, :])
        dr = _ln_bwd(d, xhat, rstd, g_ref[...])
        dres_ref[...] = ALPHA * dr
        dy_ref[...] = (gate * dr).astype(BF16)
        acc_ref[0:1, :] += jnp.sum(dh_v, axis=0, keepdims=True)
        acc_ref[1:2, :] += jnp.sum(dh_v * xin, axis=0, keepdims=True)
        acc_ref[2:3, :] += jnp.sum(d * xhat, axis=0, keepdims=True)
        acc_ref[3:4, :] += jnp.sum(d, axis=0, keepdims=True)
        acc_ref[4:5, :] += jnp.sum(factor_p * yv * dr, axis=0, keepdims=True)

    return _rows(body, name, t, 256, [dxres, dh, xin_p, y_p], [mod3, mod3_p, lg_p, lb_p],
                 [(D, F32), (D, BF16)], [(5, D)])


def _res_ln_bwd(dout, xin, y, mod3, lg, factor, name):
    t = xin.shape[0]

    def body(d_ref, x_ref, y_ref, m_ref, g_ref, dres_ref, dy_ref, acc_ref):
        _first(pl.program_id(0), [acc_ref])
        gate = factor * m_ref[2:3, :]
        yv = y_ref[...]
        r = ALPHA * x_ref[...] + gate * yv
        xhat, rstd = _ln_stats(r)
        d = d_ref[...]
        dr = _ln_bwd(d, xhat, rstd, g_ref[...])
        dres_ref[...] = ALPHA * dr
        dy_ref[...] = (gate * dr).astype(BF16)
        acc_ref[0:1, :] += jnp.sum(d * xhat, axis=0, keepdims=True)
        acc_ref[1:2, :] += jnp.sum(d, axis=0, keepdims=True)
        acc_ref[2:3, :] += jnp.sum(factor * yv * dr, axis=0, keepdims=True)

    return _rows(body, name, t, 256, [dout, xin, y], [mod3, lg], [(D, F32), (D, BF16)], [(3, D)])


def _loss_head(xf, tgt):
    t = xf.shape[0]

    def body(x_ref, t_ref, d_ref, acc_ref):
        _first(pl.program_id(0), [acc_ref])
        e = x_ref[...] - t_ref[...]
        d_ref[...] = e * (1.0 / D)
        part = 0.5 * jnp.sum(jnp.mean(e * e, axis=-1, keepdims=True), axis=0, keepdims=True)
        acc_ref[...] += jnp.broadcast_to(part, acc_ref.shape)

    return _rows(body, "loss_head", t, 256, [xf, tgt], [], [(D, F32)], [(1, LANES)])


def _silu_bf16(c_all):
    def body(c_ref, o_ref):
        v = c_ref[...]
        o_ref[...] = (v * _sigmoid(v)).astype(BF16)

    return _rows(body, "silu_c", c_all.shape[0], c_all.shape[0], [c_all], [], [(c_all.shape[1], BF16)], [])[0]


def _sum_rows(v, name):
    r, n = v.shape
    tn = _pick(n, (8192, 4096, 2048, 1024, 512, 256, 128))

    def body(v_ref, o_ref):
        acc = v_ref[0:1, :]
        for k in range(1, r):
            acc = acc + v_ref[k:k + 1, :]
        o_ref[...] = acc

    return pl.pallas_call(
        body, name=name, grid=(n // tn,),
        in_specs=[pl.BlockSpec((r, tn), lambda j: (0, j))],
        out_specs=pl.BlockSpec((1, tn), lambda j: (0, j)),
        out_shape=jax.ShapeDtypeStruct((1, n), F32),
        compiler_params=_cp("parallel"),
    )(v)


def _elementwise(fn, name, ins, out_dtypes):
    r, c = ins[0].shape
    tb = _pick(r, (128, 64, 32, 16, 8))
    n_in = len(ins)

    def body(*refs):
        outs = fn(*[x[...] for x in refs[:n_in]])
        for o_ref, o in zip(refs[n_in:], outs):
            o_ref[...] = o.astype(o_ref.dtype)

    spec = pl.BlockSpec((tb, c), lambda i: (i, 0))
    return pl.pallas_call(
        body, name=name, grid=(r // tb,), in_specs=[spec] * n_in, out_specs=[spec] * len(out_dtypes),
        out_shape=[jax.ShapeDtypeStruct((r, c), dt) for dt in out_dtypes],
        compiler_params=_cp("parallel"),
    )(*ins)


def _adamw_math(w, g, m, v):
    m = ADAM_B1 * m + (1.0 - ADAM_B1) * g
    v = ADAM_B2 * v + (1.0 - ADAM_B2) * (g * g)
    m_hat = m / (1.0 - ADAM_B1 ** ADAM_STEP)
    v_hat = v / (1.0 - ADAM_B2 ** ADAM_STEP)
    delta = -ADAM_LR * (m_hat / (jnp.sqrt(v_hat) + ADAM_EPS) + ADAM_WD * w)
    return delta, m, v


def _adamw(w, g, m, v, name):
    shape = w.shape
    c = shape[-1]
    w2, g2, m2, v2 = (a.reshape(-1, c) for a in (w, g, m, v))
    outs = _elementwise(_adamw_math, name, [w2, g2, m2, v2], [F32, F32, F32])
    return tuple(o.reshape(shape) for o in outs)


def _remote_exchange(ins, plan, peers_of, name):
    n_in, n_out = len(ins), len(plan)

    def body(*refs):
        in_refs, out_refs = refs[:n_in], refs[n_in:n_in + n_out]
        send_sems, recv_sems = refs[n_in + n_out], refs[n_in + n_out + 1]
        peers = peers_of(lax.axis_index("x"), lax.axis_index("y"), lax.axis_index("c"))
        copies = [
            pltpu.make_async_remote_copy(
                src_ref=in_refs[src], dst_ref=out_refs[k], send_sem=send_sems.at[k], recv_sem=recv_sems.at[k],
                device_id=peers[peer], device_id_type=MESH)
            for k, (peer, src) in enumerate(plan)
        ]
        for cp in copies:
            cp.start()
        for cp in copies:
            cp.wait()

    any_spec = pl.BlockSpec(memory_space=pl.ANY)
    return list(pl.pallas_call(
        body, name=name,
        in_specs=[any_spec] * n_in, out_specs=[any_spec] * n_out,
        out_shape=[jax.ShapeDtypeStruct(ins[src].shape, ins[src].dtype) for _, src in plan],
        scratch_shapes=[pltpu.SemaphoreType.DMA((n_out,)), pltpu.SemaphoreType.DMA((n_out,))],
    )(*ins))


def _sibling(x, y, c):
    return [(x, y, 1 - c)]


def _other_chips(x, y, c):
    return [(1 - x, y, c), (x, 1 - y, c), (1 - x, 1 - y, c)]


def _swap_sibling(arrs, name):
    return _remote_exchange(arrs, [(0, i) for i in range(len(arrs))], _sibling, name)


def _bcast_chips(arrs, name):
    n = len(arrs)
    out = _remote_exchange(arrs, [(p, i) for p in range(3) for i in range(n)], _other_chips, name)
    return [out[p * n:(p + 1) * n] for p in range(3)]


def _by_chip(me, own, got3):
    by_rel = [own, got3[0], got3[1], got3[2]]
    rel_bits = (0, 2, 1, 3)

    def branch(m):
        def f(ops):
            return [ops[rel_bits.index(i ^ m)] for i in range(4)]
        return f

    return lax.switch(me, [branch(m) for m in range(4)], by_rel)


BIG = ("ffn1_w_in", "ffn1_w_out", "mix_w_in", "mix_w_out", "ffn2_w_in", "ffn2_w_out")
BY_COLUMNS = ("ffn1_w_in", "ffn2_w_in")


def _layer_shape(n, shard_shape):
    r, cs = shard_shape
    return (r, 4 * cs) if n in BY_COLUMNS else (4, r, cs)


def _slot(ref, n, chip):
    if n in BY_COLUMNS:
        w = ref.shape[-1] // 4
        return ref.at[:, pl.ds(pl.multiple_of(chip * w, LANES), w)]
    return ref.at[chip]


def _chip_ids(x, y):
    chips = [(1 - x, y), (x, 1 - y), (1 - x, 1 - y)]
    return chips, [2 * cx + cy for cx, cy in chips]


def _gather_weights(shards):
    n_w = len(BIG)
    ins = [shards[n] for n in BIG]

    def body(*refs):
        in_refs = refs[:n_w]
        outs = [refs[n_w:2 * n_w], refs[2 * n_w:3 * n_w]]
        send1, recv1, local, send2, recv2 = refs[3 * n_w:]
        x, y, c = lax.axis_index("x"), lax.axis_index("y"), lax.axis_index("c")
        me = 2 * x + y
        chips, chip_idx = _chip_ids(x, y)
        for l in range(DEPTH):
            @pl.when(c == l)
            def _(l=l):
                firsts, own_copies = [], []
                for i, n in enumerate(BIG):
                    src = in_refs[i].at[l]
                    own = pltpu.make_async_copy(src, _slot(outs[l][i], n, me), local.at[i])
                    own.start()
                    own_copies.append(own)
                    for p in range(3):
                        k = p * n_w + i
                        cp = pltpu.make_async_remote_copy(
                            src_ref=src, dst_ref=_slot(outs[l][i], n, me), send_sem=send1.at[k],
                            recv_sem=recv1.at[k], device_id=(*chips[p], c), device_id_type=MESH)
                        cp.start()
                        firsts.append(cp)
                for i, n in enumerate(BIG):
                    for p in range(3):
                        k = p * n_w + i
                        pltpu.make_async_remote_copy(
                            src_ref=in_refs[i].at[l], dst_ref=_slot(outs[l][i], n, chip_idx[p]),
                            send_sem=send1.at[k], recv_sem=recv1.at[k], device_id=(*chips[p], c),
                            device_id_type=MESH).wait_recv()
                for own in own_copies:
                    own.wait()
                for cp in firsts:
                    cp.wait_send()
                seconds = []
                for i in range(n_w):
                    cp = pltpu.make_async_remote_copy(
                        src_ref=outs[l][i], dst_ref=outs[l][i], send_sem=send2.at[i], recv_sem=recv2.at[i],
                        device_id=(x, y, 1 - c), device_id_type=MESH)
                    cp.start()
                    seconds.append(cp)
                for i in range(n_w):
                    pltpu.make_async_remote_copy(
                        src_ref=outs[1 - l][i], dst_ref=outs[1 - l][i], send_sem=send2.at[i], recv_sem=recv2.at[i],
                        device_id=(x, y, 1 - c), device_id_type=MESH).wait_recv()
                for cp in seconds:
                    cp.wait_send()

    any_spec = pl.BlockSpec(memory_space=pl.ANY)
    layer = [jax.ShapeDtypeStruct(_layer_shape(n, shards[n].shape[1:]), BF16) for n in BIG]
    out = pl.pallas_call(
        body, name="gather_weights",
        in_specs=[any_spec] * n_w, out_specs=[any_spec] * (2 * n_w), out_shape=layer + layer,
        scratch_shapes=[pltpu.SemaphoreType.DMA((3 * n_w,)), pltpu.SemaphoreType.DMA((3 * n_w,)),
                        pltpu.SemaphoreType.DMA((n_w,)), pltpu.SemaphoreType.DMA((n_w,)),
                        pltpu.SemaphoreType.DMA((n_w,))],
    )(*ins)
    return [dict(zip(BIG, out[:n_w])), dict(zip(BIG, out[n_w:]))]


def _grad_swap_sibling(g):
    n_w = len(BIG)

    def body(*refs):
        in_refs = [refs[:n_w], refs[n_w:2 * n_w]]
        outs = refs[2 * n_w:3 * n_w]
        send, recv = refs[3 * n_w:]
        x, y, c = lax.axis_index("x"), lax.axis_index("y"), lax.axis_index("c")
        for l in range(DEPTH):
            @pl.when(c == l)
            def _(l=l):
                cps = [pltpu.make_async_remote_copy(
                    src_ref=in_refs[1 - l][i], dst_ref=outs[i], send_sem=send.at[i], recv_sem=recv.at[i],
                    device_id=(x, y, 1 - c), device_id_type=MESH) for i in range(n_w)]
                for cp in cps:
                    cp.start()
                for cp in cps:
                    cp.wait()

    any_spec = pl.BlockSpec(memory_space=pl.ANY)
    return list(pl.pallas_call(
        body, name="grad_swap_sibling",
        in_specs=[any_spec] * (2 * n_w), out_specs=[any_spec] * n_w,
        out_shape=[jax.ShapeDtypeStruct(a.shape, a.dtype) for a in g[0]],
        scratch_shapes=[pltpu.SemaphoreType.DMA((n_w,)), pltpu.SemaphoreType.DMA((n_w,))],
    )(*g[0], *g[1]))


def _grad_add_cores(g0, g1, got, ci, name):
    r, c = got.shape
    tb = _pick(r, (256, 352, 128, 64))

    def body(s_ref, a_ref, b_ref, o_ref, out_ref):
        own = jnp.where(s_ref[0] == 0, a_ref[...].astype(F32), b_ref[...].astype(F32))
        out_ref[...] = (own + o_ref[...].astype(F32)).astype(BF16)

    spec = lambda layer: pl.BlockSpec((tb, c), lambda i, s: (jnp.where(s[0] == layer, i, 0), 0))
    return pl.pallas_call(
        body, name=name,
        grid_spec=pltpu.PrefetchScalarGridSpec(
            num_scalar_prefetch=1, grid=(r // tb,),
            in_specs=[spec(0), spec(1), pl.BlockSpec((tb, c), lambda i, s: (i, 0))],
            out_specs=pl.BlockSpec((tb, c), lambda i, s: (i, 0))),
        out_shape=jax.ShapeDtypeStruct((r, c), BF16),
        compiler_params=_cp("arbitrary"),
    )(ci, g0, g1, got)


def _grad_scatter_chips(parts):
    n_w = len(BIG)

    def shard_shape(i, n):
        s = parts[i].shape
        return (s[0], s[1] // 4) if n in BY_COLUMNS else s[1:]

    def body(*refs):
        in_refs, outs = refs[:n_w], refs[n_w:4 * n_w]
        send, recv = refs[4 * n_w:]
        x, y, c = lax.axis_index("x"), lax.axis_index("y"), lax.axis_index("c")
        chips, chip_idx = _chip_ids(x, y)
        cps = []
        for p in range(3):
            for i, n in enumerate(BIG):
                k = p * n_w + i
                cps.append(pltpu.make_async_remote_copy(
                    src_ref=_slot(in_refs[i], n, chip_idx[p]), dst_ref=outs[k], send_sem=send.at[k],
                    recv_sem=recv.at[k], device_id=(*chips[p], c), device_id_type=MESH))
        for cp in cps:
            cp.start()
        for cp in cps:
            cp.wait()

    any_spec = pl.BlockSpec(memory_space=pl.ANY)
    out = pl.pallas_call(
        body, name="grad_scatter_chips",
        in_specs=[any_spec] * n_w, out_specs=[any_spec] * (3 * n_w),
        out_shape=[jax.ShapeDtypeStruct(shard_shape(i, n), BF16) for _ in range(3) for i, n in enumerate(BIG)],
        scratch_shapes=[pltpu.SemaphoreType.DMA((3 * n_w,)), pltpu.SemaphoreType.DMA((3 * n_w,))],
    )(*parts)
    return [out[p * n_w:(p + 1) * n_w] for p in range(3)]


def _grad_add_chips(part, got3, n, me_ci, name):
    rs, cs = got3[0].shape
    tb = _pick(rs, (256, 352, 128, 64))

    def body(s_ref, own_ref, a_ref, b_ref, c_ref, out_ref):
        acc = own_ref[...].astype(F32) + a_ref[...].astype(F32)
        out_ref[...] = (acc + b_ref[...].astype(F32)) + c_ref[...].astype(F32)

    if n in BY_COLUMNS:
        own_spec = pl.BlockSpec((tb, cs), lambda i, s: (i, s[0]))
    else:
        own_spec = pl.BlockSpec((None, tb, cs), lambda i, s: (s[0], i, 0))
    plain = pl.BlockSpec((tb, cs), lambda i, s: (i, 0))
    return pl.pallas_call(
        body, name=name,
        grid_spec=pltpu.PrefetchScalarGridSpec(
            num_scalar_prefetch=1, grid=(rs // tb,),
            in_specs=[own_spec, plain, plain, plain],
            out_specs=pl.BlockSpec((None, tb, cs), lambda i, s: (s[1], i, 0))),
        out_shape=jax.ShapeDtypeStruct((DEPTH, rs, cs), F32),
        compiler_params=_cp("arbitrary"),
    )(me_ci, part, *got3)


def _grad_swap_back(reduced):
    n_w = len(reduced)

    def body(*refs):
        outs = refs[n_w:2 * n_w]
        send, recv = refs[2 * n_w:]
        x, y, c = lax.axis_index("x"), lax.axis_index("y"), lax.axis_index("c")
        for l in range(DEPTH):
            @pl.when(c == l)
            def _(l=l):
                cps = [pltpu.make_async_remote_copy(
                    src_ref=outs[i].at[l], dst_ref=outs[i].at[l], send_sem=send.at[i], recv_sem=recv.at[i],
                    device_id=(x, y, 1 - c), device_id_type=MESH) for i in range(n_w)]
                for cp in cps:
                    cp.start()
                for i, cp in enumerate(cps):
                    cp.wait_send()
                    pltpu.make_async_remote_copy(
                        src_ref=outs[i].at[1 - l], dst_ref=outs[i].at[1 - l], send_sem=send.at[i],
                        recv_sem=recv.at[i], device_id=(x, y, 1 - c), device_id_type=MESH).wait_recv()

    any_spec = pl.BlockSpec(memory_space=pl.ANY)
    return list(pl.pallas_call(
        body, name="grad_swap_back",
        in_specs=[any_spec] * n_w, out_specs=[any_spec] * n_w,
        out_shape=[jax.ShapeDtypeStruct(a.shape, a.dtype) for a in reduced],
        input_output_aliases={i: i for i in range(n_w)},
        scratch_shapes=[pltpu.SemaphoreType.DMA((n_w,)), pltpu.SemaphoreType.DMA((n_w,))],
    )(*reduced))


def _shift_down(v, s, t_iota):
    return jnp.where(t_iota >= s, pltpu.roll(v, s, 0), 0.0)


def _shift_up(v, s, t_iota, t):
    return jnp.where(t_iota < t - s, pltpu.roll(v, t - s, 0), 0.0)


def _ssd_conv_fwd(proj, w, b):
    t = proj.shape[0]
    k_w = SSD_CONV_K

    def body(x_ref, w_ref, b_ref, o_ref):
        x = x_ref[...]
        ti = lax.broadcasted_iota(jnp.int32, x.shape, 0)
        pre = x * w_ref[k_w - 1:k_w, :] + b_ref[...]
        for s in range(1, k_w):
            pre = pre + _shift_down(x, s, ti) * w_ref[k_w - 1 - s:k_w - s, :]
        o_ref[...] = pre * _sigmoid(pre)

    off = P_XBC // LANES
    return pl.pallas_call(
        body, name="ssd_conv_fwd", grid=(SSD_CONV_DIM // LANES,),
        in_specs=[pl.BlockSpec((t, LANES), lambda j: (0, off + j)),
                  pl.BlockSpec((k_w, LANES), lambda j: (0, j)),
                  pl.BlockSpec((1, LANES), lambda j: (0, j))],
        out_specs=pl.BlockSpec((t, LANES), lambda j: (0, j)),
        out_shape=jax.ShapeDtypeStruct((t, SSD_CONV_DIM), F32),
        compiler_params=_cp("parallel"),
    )(proj, w, b)


def _ssd_conv_bwd(dxbc, proj, w, b):
    t = proj.shape[0]
    k_w = SSD_CONV_K

    def body(d_ref, x_ref, w_ref, b_ref, dx_ref, dw_ref, db_ref):
        x = x_ref[...]
        ti = lax.broadcasted_iota(jnp.int32, x.shape, 0)
        shifted = [x] + [_shift_down(x, s, ti) for s in range(1, k_w)]
        pre = b_ref[...] + shifted[0] * w_ref[k_w - 1:k_w, :]
        for s in range(1, k_w):
            pre = pre + shifted[s] * w_ref[k_w - 1 - s:k_w - s, :]
        sg = _sigmoid(pre)
        dpre = d_ref[...] * (sg * (1.0 + pre * (1.0 - sg)))
        db_ref[...] = jnp.sum(dpre, axis=0, keepdims=True)
        dx = dpre * w_ref[k_w - 1:k_w, :]
        for s in range(k_w):
            dw_ref[k_w - 1 - s:k_w - s, :] = jnp.sum(dpre * shifted[s], axis=0, keepdims=True)
            if s:
                dx = dx + _shift_up(dpre, s, ti, t) * w_ref[k_w - 1 - s:k_w - s, :]
        dx_ref[...] = dx

    off = P_XBC // LANES
    return pl.pallas_call(
        body, name="ssd_conv_bwd", grid=(SSD_CONV_DIM // LANES,),
        in_specs=[pl.BlockSpec((t, LANES), lambda j: (0, j)),
                  pl.BlockSpec((t, LANES), lambda j: (0, off + j)),
                  pl.BlockSpec((k_w, LANES), lambda j: (0, j)),
                  pl.BlockSpec((1, LANES), lambda j: (0, j))],
        out_specs=[pl.BlockSpec((t, LANES), lambda j: (0, j)),
                   pl.BlockSpec((k_w, LANES), lambda j: (0, j)),
                   pl.BlockSpec((1, LANES), lambda j: (0, j))],
        out_shape=[jax.ShapeDtypeStruct((t, SSD_CONV_DIM), F32),
                   jax.ShapeDtypeStruct((k_w, SSD_CONV_DIM), F32),
                   jax.ShapeDtypeStruct((1, SSD_CONV_DIM), F32)],
        compiler_params=_cp("parallel"),
    )(dxbc, proj, w, b)


def _shortconv_fwd(proj, w):
    t = proj.shape[0]
    nb = SC_W // LANES
    off = P_SC // LANES

    def body(b_ref, c_ref, x_ref, w_ref, o_ref):
        u = c_ref[...] * x_ref[...]
        ti = lax.broadcasted_iota(jnp.int32, u.shape, 0)
        cv = u * w_ref[SC_K - 1:SC_K, :]
        for s in range(1, SC_K):
            cv = cv + _shift_down(u, s, ti) * w_ref[SC_K - 1 - s:SC_K - s, :]
        o_ref[...] = (b_ref[...] * cv).astype(BF16)

    return pl.pallas_call(
        body, name="shortconv_fwd", grid=(nb,),
        in_specs=[pl.BlockSpec((t, LANES), lambda j: (0, off + j)),
                  pl.BlockSpec((t, LANES), lambda j: (0, off + nb + j)),
                  pl.BlockSpec((t, LANES), lambda j: (0, off + 2 * nb + j)),
                  pl.BlockSpec((SC_K, LANES), lambda j: (0, j))],
        out_specs=pl.BlockSpec((t, LANES), lambda j: (0, j)),
        out_shape=jax.ShapeDtypeStruct((t, SC_W), BF16),
        compiler_params=_cp("parallel"),
    )(proj, proj, proj, w)


def _shortconv_bwd(dy, dy_off, proj, w):
    t = proj.shape[0]
    nb = SC_W // LANES
    off = P_SC // LANES
    doff = dy_off // LANES

    def body(d_ref, b_ref, c_ref, x_ref, w_ref, db_ref, dc_ref, dx_ref, dw_ref):
        cg, xin = c_ref[...], x_ref[...]
        u = cg * xin
        ti = lax.broadcasted_iota(jnp.int32, u.shape, 0)
        shifted = [u] + [_shift_down(u, s, ti) for s in range(1, SC_K)]
        cv = shifted[0] * w_ref[SC_K - 1:SC_K, :]
        for s in range(1, SC_K):
            cv = cv + shifted[s] * w_ref[SC_K - 1 - s:SC_K - s, :]
        d = d_ref[...]
        db_ref[...] = (d * cv).astype(BF16)
        dcv = d * b_ref[...]
        du = dcv * w_ref[SC_K - 1:SC_K, :]
        for s in range(SC_K):
            dw_ref[SC_K - 1 - s:SC_K - s, :] = jnp.sum(dcv * shifted[s], axis=0, keepdims=True)
            if s:
                du = du + _shift_up(dcv, s, ti, t) * w_ref[SC_K - 1 - s:SC_K - s, :]
        dc_ref[...] = (du * xin).astype(BF16)
        dx_ref[...] = (du * cg).astype(BF16)

    tile = pl.BlockSpec((t, LANES), lambda j: (0, j))
    outs = pl.pallas_call(
        body, name="shortconv_bwd", grid=(nb,),
        in_specs=[pl.BlockSpec((t, LANES), lambda j: (0, doff + j)),
                  pl.BlockSpec((t, LANES), lambda j: (0, off + j)),
                  pl.BlockSpec((t, LANES), lambda j: (0, off + nb + j)),
                  pl.BlockSpec((t, LANES), lambda j: (0, off + 2 * nb + j)),
                  pl.BlockSpec((SC_K, LANES), lambda j: (0, j))],
        out_specs=[tile, tile, tile, pl.BlockSpec((SC_K, LANES), lambda j: (0, j))],
        out_shape=[jax.ShapeDtypeStruct((t, SC_W), BF16)] * 3 + [jax.ShapeDtypeStruct((SC_K, SC_W), F32)],
        compiler_params=_cp("parallel"),
    )(dy, proj, proj, proj, w)
    return outs


def _cum_logf(proj, fbias_row):
    t = proj.shape[0]
    blk = CUM_BLOCK

    def body(p_ref, b_ref, o_ref, carry_ref):
        i = pl.program_id(0)

        @pl.when(i == 0)
        def _():
            carry_ref[...] = jnp.zeros_like(carry_ref)

        lf = -_softplus(-(p_ref[...] + b_ref[...]))
        r = lax.broadcasted_iota(jnp.int32, (blk, blk), 0)
        c = lax.broadcasted_iota(jnp.int32, (blk, blk), 1)
        tri = (r >= c).astype(BF16)
        o_ref[...] = _tri_left(tri, lf) + carry_ref[...]
        carry_ref[...] = o_ref[blk - 1:blk, :]

    return pl.pallas_call(
        body, name="cum_logf", grid=(t // blk,),
        in_specs=[pl.BlockSpec((blk, LANES), lambda i: (i, P_PAD // LANES)),
                  pl.BlockSpec((1, LANES), lambda i: (0, 0))],
        out_specs=pl.BlockSpec((blk, LANES), lambda i: (i, 0)),
        out_shape=jax.ShapeDtypeStruct((t, LANES), F32),
        scratch_shapes=[pltpu.VMEM((1, LANES), F32)],
        compiler_params=_cp("arbitrary"),
    )(proj, fbias_row)


def _pad_block_bwd(dcf, ddt, proj, fbias_row):
    t = proj.shape[0]
    blk = CUM_BLOCK
    nb = t // blk

    def body(dcf_ref, ddt_ref, p_ref, b_ref, o_ref, db_ref, carry_ref):
        i = pl.program_id(0)

        @pl.when(i == 0)
        def _():
            carry_ref[...] = jnp.zeros_like(carry_ref)
            db_ref[...] = jnp.zeros_like(db_ref)

        r = lax.broadcasted_iota(jnp.int32, (blk, blk), 0)
        c = lax.broadcasted_iota(jnp.int32, (blk, blk), 1)
        tri = (r <= c).astype(BF16)
        rev = _tri_left(tri, dcf_ref[...]) + carry_ref[...]
        carry_ref[...] = jnp.sum(dcf_ref[...], axis=0, keepdims=True) + carry_ref[...]
        lane = lax.broadcasted_iota(jnp.int32, (blk, LANES), 1)
        is_f = (lane >= PAD_F0) & (lane < PAD_F0 + FOX_HEADS)
        df = jnp.where(is_f, rev * _sigmoid(-(p_ref[...] + b_ref[...])), 0.0)
        db_ref[...] += jnp.sum(df, axis=0, keepdims=True)
        o_ref[...] = jnp.where(lane < PAD_DT0 + SSD_HEADS, ddt_ref[...], df).astype(BF16)

    return pl.pallas_call(
        body, name="pad_block_bwd", grid=(nb,),
        in_specs=[pl.BlockSpec((blk, LANES), lambda i: (nb - 1 - i, 0)),
                  pl.BlockSpec((blk, LANES), lambda i: (nb - 1 - i, 0)),
                  pl.BlockSpec((blk, LANES), lambda i: (nb - 1 - i, P_PAD // LANES)),
                  pl.BlockSpec((1, LANES), lambda i: (0, 0))],
        out_specs=[pl.BlockSpec((blk, LANES), lambda i: (nb - 1 - i, 0)),
                   pl.BlockSpec((1, LANES), lambda i: (0, 0))],
        out_shape=[jax.ShapeDtypeStruct((t, LANES), BF16), jax.ShapeDtypeStruct((1, LANES), F32)],
        scratch_shapes=[pltpu.VMEM((1, LANES), F32)],
        compiler_params=_cp("arbitrary"),
    )(dcf, ddt, proj, fbias_row)


def _att_scores(q, k, cq, ck, diagonal, blk):
    s = _dot_nt(q, k) * (FOX_HD ** -0.5) + cq - ck
    if diagonal:
        r = lax.broadcasted_iota(jnp.int32, (blk, blk), 0)
        c = lax.broadcasted_iota(jnp.int32, (blk, blk), 1)
        s = jnp.where(r >= c, s, NEG)
    return s


def _fox_fwd(q, k, v, cf_col, cf_row):
    h, t, hd = q.shape
    blk = min(ATT_BLOCK, t)
    nb = t // blk

    def body(q_ref, k_ref, v_ref, cq_ref, ck_ref, o_ref, lse_ref):
        qi = pl.program_id(1)
        qv, cq = q_ref[...], cq_ref[...]

        def step(j, carry, diagonal):
            m, l, acc = carry
            off = pl.multiple_of(j * blk, blk)
            s = _att_scores(qv, k_ref[pl.ds(off, blk), :], cq, ck_ref[:, pl.ds(off, blk)], diagonal, blk)
            m_new = jnp.maximum(m, jnp.max(s, axis=1, keepdims=True))
            alpha = jnp.exp(m - m_new)
            p = jnp.exp(s - m_new)
            l = alpha * l + jnp.sum(p, axis=1, keepdims=True)
            acc = alpha * acc + _dot(p.astype(BF16), v_ref[pl.ds(off, blk), :])
            return m_new, l, acc

        init = (jnp.full((blk, 1), NEG, F32), jnp.zeros((blk, 1), F32), jnp.zeros((blk, hd), F32))
        carry = lax.fori_loop(0, qi, lambda j, cr: step(j, cr, False), init)
        m, l, acc = step(qi, carry, True)
        o_ref[...] = acc / l
        lse_ref[...] = m + jnp.log(l)

    qmap = lambda hh, i: (hh, i, 0)
    whole = lambda hh, i: (hh, 0, 0)
    return pl.pallas_call(
        body, name="fox_fwd", grid=(h, nb),
        in_specs=[pl.BlockSpec((None, blk, hd), qmap), pl.BlockSpec((None, t, hd), whole),
                  pl.BlockSpec((None, t, hd), whole), pl.BlockSpec((None, blk, 1), qmap),
                  pl.BlockSpec((None, 1, t), whole)],
        out_specs=[pl.BlockSpec((None, blk, hd), qmap), pl.BlockSpec((None, blk, 1), qmap)],
        out_shape=[jax.ShapeDtypeStruct((h, t, hd), F32), jax.ShapeDtypeStruct((h, t, 1), F32)],
        compiler_params=_cp("parallel", "arbitrary"),
    )(q, k, v, cf_col, cf_row)


def _fox_bwd(q, k, v, cf_col, cf_row, o, lse, do):
    h, t, hd = q.shape
    blk = min(ATT_BLOCK, t)
    nb = t // blk
    scale = FOX_HD ** -0.5

    def body(q_ref, k_ref, v_ref, cq_ref, ck_ref, o_ref, lse_ref, do_ref,
             dq_ref, dk_ref, dv_ref, dcq_ref, dck_ref, delta_s):
        kj = pl.program_id(1)

        @pl.when(kj == 0)
        def _():
            dq_ref[...] = jnp.zeros_like(dq_ref)
            dcq_ref[...] = jnp.zeros_like(dcq_ref)

            def fill(i, _):
                rows = pl.ds(pl.multiple_of(i * blk, blk), blk)
                delta_s[rows, :] = jnp.sum(do_ref[rows, :] * o_ref[rows, :], axis=1, keepdims=True)
                return 0

            lax.fori_loop(0, nb, fill, 0)

        kb, vb, ck = k_ref[...], v_ref[...], ck_ref[...]

        def step(i, carry, diagonal):
            dk, dv, dck = carry
            rows = pl.ds(pl.multiple_of(i * blk, blk), blk)
            qb = q_ref[rows, :]
            do_b = do_ref[rows, :].astype(BF16)
            s = _att_scores(qb, kb, cq_ref[rows, :], ck, diagonal, blk)
            p = jnp.exp(s - lse_ref[rows, :])
            dv = dv + _dot_tn(p.astype(BF16), do_b)
            ds = p * (_dot_nt(do_b, vb) - delta_s[rows, :])
            ds_b = ds.astype(BF16)
            dk = dk + _dot_tn(ds_b, qb)
            dq_ref[rows, :] += _dot(ds_b, kb) * scale
            dcq_ref[rows, :] += jnp.sum(ds, axis=1, keepdims=True)
            dck = dck - jnp.sum(ds, axis=0, keepdims=True)
            return dk, dv, dck

        init = (jnp.zeros((blk, hd), F32), jnp.zeros((blk, hd), F32), jnp.zeros((1, blk), F32))
        carry = step(kj, init, True)
        dk, dv, dck = lax.fori_loop(kj + 1, nb, lambda i, cr: step(i, cr, False), carry)
        dk_ref[...] = dk * scale
        dv_ref[...] = dv
        dck_ref[...] = dck

    kmap = lambda hh, j: (hh, j, 0)
    whole = lambda hh, j: (hh, 0, 0)
    return pl.pallas_call(
        body, name="fox_bwd", grid=(h, nb),
        in_specs=[pl.BlockSpec((None, t, hd), whole), pl.BlockSpec((None, blk, hd), kmap),
                  pl.BlockSpec((None, blk, hd), kmap), pl.BlockSpec((None, t, 1), whole),
                  pl.BlockSpec((None, 1, blk), lambda hh, j: (hh, 0, j)),
                  pl.BlockSpec((None, t, hd), whole), pl.BlockSpec((None, t, 1), whole),
                  pl.BlockSpec((None, t, hd), whole)],
        out_specs=[pl.BlockSpec((None, t, hd), whole), pl.BlockSpec((None, blk, hd), kmap),
                   pl.BlockSpec((None, blk, hd), kmap), pl.BlockSpec((None, t, 1), whole),
                   pl.BlockSpec((None, 1, blk), lambda hh, j: (hh, 0, j))],
        out_shape=[jax.ShapeDtypeStruct((h, t, hd), F32), jax.ShapeDtypeStruct((h, t, hd), F32),
                   jax.ShapeDtypeStruct((h, t, hd), F32), jax.ShapeDtypeStruct((h, t, 1), F32),
                   jax.ShapeDtypeStruct((h, 1, t), F32)],
        scratch_shapes=[pltpu.VMEM((t, 1), F32)],
        compiler_params=_cp("parallel", "arbitrary"),
    )(q, k, v, cf_col, cf_row, o, lse, do)


def _lane_col(v, h):
    lane = lax.broadcasted_iota(jnp.int32, v.shape, 1)
    return jnp.sum(jnp.where(lane == h, v, 0.0), axis=1, keepdims=True)


def _sub_row(v, h):
    sub = lax.broadcasted_iota(jnp.int32, v.shape, 0)
    return jnp.sum(jnp.where(sub == h, v, 0.0), axis=0, keepdims=True)


def _ssd_decays(pad, pad_t, dtb_row, alog_row, dtb_col, alog_col, blk):
    r = lax.broadcasted_iota(jnp.int32, (blk, blk), 0)
    c = lax.broadcasted_iota(jnp.int32, (blk, blk), 1)
    tril = r >= c
    dt_c = _softplus(pad + dtb_row)
    acs_c = _tri_left(tril.astype(BF16), dt_c * (-jnp.exp(alog_row)))
    dt_r = _softplus(pad_t + dtb_col)
    acs_r = _tri_right(dt_r * (-jnp.exp(alog_col)), (r <= c).astype(BF16))
    rows = lax.broadcasted_iota(jnp.int32, acs_c.shape, 0)
    acs_last = jnp.sum(jnp.where(rows == blk - 1, acs_c, 0.0), axis=0, keepdims=True)
    return dt_c, acs_c, acs_r, acs_last, tril


def _pair_terms(pair, dt_c, acs_c, acs_r, acs_last, d_row, blk):
    lane = lax.broadcasted_iota(jnp.int32, (blk, LANES), 1)
    lo = lane < 64
    lo_row = lax.broadcasted_iota(jnp.int32, (1, LANES), 1) < 64
    h0, h1 = 2 * pair, 2 * pair + 1
    col = [_lane_col(acs_c, h0), _lane_col(acs_c, h1)]
    row = [_sub_row(acs_r, h0), _sub_row(acs_r, h1)]
    last = [_lane_col(acs_last, h0), _lane_col(acs_last, h1)]
    dt_p = jnp.where(lo, _lane_col(dt_c, h0), _lane_col(dt_c, h1))
    e_p = jnp.where(lo, jnp.exp(col[0]), jnp.exp(col[1]))
    w_p = jnp.where(lo, jnp.exp(last[0] - col[0]), jnp.exp(last[1] - col[1]))
    decay_p = jnp.where(lo_row, jnp.exp(last[0]), jnp.exp(last[1]))
    d_p = jnp.where(lo_row, _lane_col(d_row, h0), _lane_col(d_row, h1))
    return lo, lo_row, col, row, last, dt_p, e_p, w_p, decay_p, d_p


def _ssd_specs(t, blk, rev):
    nc = t // blk
    ix = (lambda i: nc - 1 - i) if rev else (lambda i: i)
    xbc = pl.BlockSpec((blk, SSD_CONV_DIM), lambda i: (ix(i), 0))
    pad = pl.BlockSpec((blk, LANES), lambda i: (ix(i), P_PAD // LANES))
    pad_t = pl.BlockSpec((LANES, blk), lambda i: (0, ix(i)))
    z = pl.BlockSpec((blk, SSD_W), lambda i: (ix(i), 0))
    row = pl.BlockSpec((1, LANES), lambda i: (0, 0))
    colv = pl.BlockSpec((LANES, 1), lambda i: (0, 0))
    ng = pl.BlockSpec((1, SSD_W), lambda i: (0, 0))
    y = pl.BlockSpec((blk, SSD_W), lambda i: (ix(i), 0))
    st = pl.BlockSpec((None, 4, LANES, LANES), lambda i: (ix(i), 0, 0, 0))
    return nc, xbc, pad, pad_t, z, row, colv, ng, y, st


def _ssd_fwd(xbc, proj, pad_t, dtb_row, alog_row, d_row, dtb_col, alog_col, ng):
    t = xbc.shape[0]
    blk = min(SSD_CHUNK, t)
    nc, s_xbc, s_pad, s_padt, s_z, s_row, s_col, s_ng, s_y, s_st = _ssd_specs(t, blk, False)

    def body(xbc_ref, pad_ref, padt_ref, z_ref, dtb_ref, alog_ref, d_ref, dtbc_ref, alogc_ref, ng_ref,
             out_ref, ypre_ref, st_ref, state):
        @pl.when(pl.program_id(0) == 0)
        def _():
            state[...] = jnp.zeros_like(state)

        dt_c, acs_c, acs_r, acs_last, tril = _ssd_decays(
            pad_ref[...], padt_ref[...], dtb_ref[...], alog_ref[...], dtbc_ref[...], alogc_ref[...], blk)
        ys = []
        g_mat = {}
        for pair in range(4):
            g = pair // 2
            bg = xbc_ref[:, SSD_W + LANES * g:SSD_W + LANES * (g + 1)].astype(BF16)
            cg = xbc_ref[:, SSD_W + 2 * LANES + LANES * g:SSD_W + 2 * LANES + LANES * (g + 1)].astype(BF16)
            if g not in g_mat:
                g_mat[g] = _dot_nt(cg, bg)
            xs_p = xbc_ref[:, LANES * pair:LANES * (pair + 1)]
            lo, _, col, row, _, dt_p, e_p, w_p, decay_p, d_p = _pair_terms(
                pair, dt_c, acs_c, acs_r, acs_last, d_ref[...], blk)
            x_p = xs_p * dt_p
            y = None
            for hh in range(2):
                lm = jnp.exp(jnp.where(tril, col[hh] - row[hh], NEG))
                m_h = (g_mat[g] * lm).astype(BF16)
                x_h = jnp.where(lo if hh == 0 else ~lo, x_p, 0.0).astype(BF16)
                y_h = _dot(m_h, x_h)
                y = y_h if y is None else y + y_h
            st_in = state[pair]
            st_ref[pair] = st_in
            y = y + e_p * _dot(cg, st_in.astype(BF16))
            state[pair] = decay_p * st_in + _dot_tn(bg, (x_p * w_p).astype(BF16))
            ys.append(y + d_p * xs_p)
        y_all = jnp.concatenate(ys, axis=1)
        ypre_ref[...] = y_all
        z = z_ref[...]
        y2 = y_all * (z * _sigmoid(z))
        outs = []
        for g in range(2):
            seg = y2[:, 256 * g:256 * (g + 1)]
            rr = lax.rsqrt(jnp.mean(seg * seg, axis=-1, keepdims=True) + RMS_EPS)
            outs.append(seg * rr * ng_ref[:, 256 * g:256 * (g + 1)])
        out_ref[...] = jnp.concatenate(outs, axis=1).astype(BF16)

    return pl.pallas_call(
        body, name="ssd_scan_fwd", grid=(nc,),
        in_specs=[s_xbc, s_pad, s_padt, s_z, s_row, s_row, s_row, s_col, s_col, s_ng],
        out_specs=[s_y, s_y, s_st],
        out_shape=[jax.ShapeDtypeStruct((t, SSD_W), BF16), jax.ShapeDtypeStruct((t, SSD_W), F32),
                   jax.ShapeDtypeStruct((nc, 4, LANES, LANES), F32)],
        scratch_shapes=[pltpu.VMEM((4, LANES, LANES), F32)],
        compiler_params=_cp("arbitrary"),
    )(xbc, proj, pad_t, proj, dtb_row, alog_row, d_row, dtb_col, alog_col, ng)


def _ssd_bwd(dout, dout_off, xbc, proj, pad_t, ypre, states, dtb_row, alog_row, d_row, dtb_col, alog_col, ng):
    t = xbc.shape[0]
    blk = min(SSD_CHUNK, t)
    nc, s_xbc, s_pad, s_padt, s_z, s_row, s_col, s_ng, s_y, s_st = _ssd_specs(t, blk, True)
    s_dout = pl.BlockSpec((blk, SSD_W), lambda i: (nc - 1 - i, dout_off // SSD_W))

    def body(dout_ref, xbc_ref, pad_ref, padt_ref, z_ref, ypre_ref, st_ref, dtb_ref, alog_ref, d_ref,
             dtbc_ref, alogc_ref, ng_ref, dxbc_ref, ddt_ref, dz_ref, acc_ref, dng_ref, dstate):
        @pl.when(pl.program_id(0) == 0)
        def _():
            dstate[...] = jnp.zeros_like(dstate)
            acc_ref[...] = jnp.zeros_like(acc_ref)
            dng_ref[...] = jnp.zeros_like(dng_ref)

        pad = pad_ref[...]
        dt_c, acs_c, acs_r, acs_last, tril = _ssd_decays(
            pad, padt_ref[...], dtb_ref[...], alog_ref[...], dtbc_ref[...], alogc_ref[...], blk)
        a_row = -jnp.exp(alog_ref[...])

        z = z_ref[...]
        sz = _sigmoid(z)
        silu_z = z * sz
        y_pre = ypre_ref[...]
        y2 = y_pre * silu_z
        dy2 = []
        for g in range(2):
            sl = slice(256 * g, 256 * (g + 1))
            seg = y2[:, sl]
            rr = lax.rsqrt(jnp.mean(seg * seg, axis=-1, keepdims=True) + RMS_EPS)
            nrm = seg * rr
            d_seg = dout_ref[:, sl]
            dng_ref[:, sl] += jnp.sum(d_seg * nrm, axis=0, keepdims=True)
            dn = d_seg * ng_ref[:, sl]
            dy2.append(rr * (dn - nrm * jnp.mean(dn * nrm, axis=-1, keepdims=True)))
        dy2 = jnp.concatenate(dy2, axis=1)
        dz_ref[...] = (dy2 * y_pre * (sz * (1.0 + z * (1.0 - sz)))).astype(BF16)
        dy_all = dy2 * silu_z

        lane_row = lax.broadcasted_iota(jnp.int32, (1, LANES), 1)
        lane_blk = lax.broadcasted_iota(jnp.int32, (blk, LANES), 1)
        row_col = lax.broadcasted_iota(jnp.int32, (blk, 1), 0)
        ddt = jnp.zeros((blk, LANES), F32)
        dacs = jnp.zeros((blk, LANES), F32)
        dd_row = jnp.zeros((1, LANES), F32)
        ones_b = jnp.ones((blk, LANES), BF16)
        dxs = []
        d_b = [None, None]
        d_c = [None, None]
        d_g = [None, None]
        bgs, cgs = {}, {}
        g_mat = {}
        for pair in range(4):
            g = pair // 2
            if g not in g_mat:
                bgs[g] = xbc_ref[:, SSD_W + LANES * g:SSD_W + LANES * (g + 1)].astype(BF16)
                cgs[g] = xbc_ref[:, SSD_W + 2 * LANES + LANES * g:SSD_W + 2 * LANES + LANES * (g + 1)].astype(BF16)
                g_mat[g] = _dot_nt(cgs[g], bgs[g])
            bg, cg = bgs[g], cgs[g]
            xs_p = xbc_ref[:, LANES * pair:LANES * (pair + 1)]
            lo, lo_row, col, row, last, dt_p, e_p, w_p, decay_p, d_p = _pair_terms(
                pair, dt_c, acs_c, acs_r, acs_last, d_ref[...], blk)
            x_p = xs_p * dt_p
            dy_p = dy_all[:, LANES * pair:LANES * (pair + 1)]
            st_in = st_ref[pair]
            dst = dstate[pair]
            dx_diag = None
            for hh in range(2):
                sel = lo if hh == 0 else ~lo
                lm = jnp.exp(jnp.where(tril, col[hh] - row[hh], NEG))
                m_f = g_mat[g] * lm
                m_h = m_f.astype(BF16)
                x_h = jnp.where(sel, x_p, 0.0).astype(BF16)
                dy_h = jnp.where(sel, dy_p, 0.0).astype(BF16)
                dxd = _dot_tn(m_h, dy_h)
                dm = _dot_nt(dy_h, x_h)
                dg_h = dm * lm
                p_b = (dm * m_f).astype(BF16)
                dacs = dacs + jnp.where(lane_blk == 2 * pair + hh, _dot(p_b, ones_b) - _dot_tn(p_b, ones_b), 0.0)
                dx_diag = dxd if dx_diag is None else dx_diag + dxd
                d_g[g] = dg_h if d_g[g] is None else d_g[g] + dg_h
            st_b = st_in.astype(BF16)
            dst_b = dst.astype(BF16)
            y_off = e_p * _dot(cg, st_b)
            edy = (e_p * dy_p).astype(BF16)
            dc_off = _dot_nt(edy, st_b)
            d_c[g] = dc_off if d_c[g] is None else d_c[g] + dc_off
            dstate[pair] = decay_p * dst + _dot_tn(cg, edy)
            dx_state = _dot(bg, dst_b) * w_p
            db_st = _dot_nt((x_p * w_p).astype(BF16), dst_b)
            d_b[g] = db_st if d_b[g] is None else d_b[g] + db_st
            dx = dx_diag + dx_state
            dxs.append(dx * dt_p + d_p * dy_p)
            prod_dt = dx * xs_p
            prod_acs = dy_p * y_off - x_p * dx_state
            prod_st = x_p * dx_state
            prod_d = dy_p * xs_p
            st_prod = jnp.sum(dst * st_in, axis=0, keepdims=True)
            for hh in range(2):
                h = 2 * pair + hh
                sel = lo if hh == 0 else ~lo
                sel_row = lo_row if hh == 0 else ~lo_row
                ddt_h = jnp.sum(jnp.where(sel, prod_dt, 0.0), axis=1, keepdims=True)
                dacs_h = jnp.sum(jnp.where(sel, prod_acs, 0.0), axis=1, keepdims=True)
                tail = jnp.sum(jnp.sum(jnp.where(sel, prod_st, 0.0), axis=1, keepdims=True), axis=0, keepdims=True)
                tail = tail + jnp.exp(last[hh]) * jnp.sum(jnp.where(sel_row, st_prod, 0.0), axis=1, keepdims=True)
                dacs_h = dacs_h + jnp.where(row_col == blk - 1, tail, 0.0)
                dd_h = jnp.sum(jnp.sum(jnp.where(sel, prod_d, 0.0), axis=1, keepdims=True), axis=0, keepdims=True)
                ddt = ddt + jnp.where(lane_blk == h, ddt_h, 0.0)
                dacs = dacs + jnp.where(lane_blk == h, dacs_h, 0.0)
                dd_row = dd_row + jnp.where(lane_row == h, dd_h, 0.0)
        for g in range(2):
            dg_b = d_g[g].astype(BF16)
            d_c[g] = d_c[g] + _dot(dg_b, bgs[g])
            d_b[g] = d_b[g] + _dot_tn(dg_b, cgs[g])
        r = lax.broadcasted_iota(jnp.int32, (blk, blk), 0)
        c = lax.broadcasted_iota(jnp.int32, (blk, blk), 1)
        da = _tri_left((r <= c).astype(BF16), dacs)
        ddt = ddt + da * a_row
        d_raw = ddt * _sigmoid(pad + dtb_ref[...])
        ddt_ref[...] = d_raw
        acc_ref[0:1, :] += jnp.sum(da * dt_c, axis=0, keepdims=True) * a_row
        acc_ref[1:2, :] += dd_row
        acc_ref[2:3, :] += jnp.sum(d_raw, axis=0, keepdims=True)
        dxbc_ref[...] = jnp.concatenate(dxs + d_b + d_c, axis=1)

    return pl.pallas_call(
        body, name="ssd_scan_bwd", grid=(nc,),
        in_specs=[s_dout, s_xbc, s_pad, s_padt, s_z, s_y, s_st, s_row, s_row, s_row, s_col, s_col, s_ng],
        out_specs=[s_xbc, pl.BlockSpec((blk, LANES), lambda i: (nc - 1 - i, 0)), s_y,
                   pl.BlockSpec((8, LANES), lambda i: (0, 0)), s_ng],
        out_shape=[jax.ShapeDtypeStruct((t, SSD_CONV_DIM), F32), jax.ShapeDtypeStruct((t, LANES), F32),
                   jax.ShapeDtypeStruct((t, SSD_W), BF16), jax.ShapeDtypeStruct((8, LANES), F32),
                   jax.ShapeDtypeStruct((1, SSD_W), F32)],
        scratch_shapes=[pltpu.VMEM((4, LANES, LANES), F32)],
        compiler_params=_cp("arbitrary"),
    )(dout, xbc, proj, pad_t, proj, ypre, states, dtb_row, alog_row, d_row, dtb_col, alog_col, ng)


def _pad_lanes(v, off):
    return jnp.zeros((1, LANES), F32).at[0, off:off + v.shape[0]].set(v)


def _perm_mix_w_in(w):
    z, xbc, dt = w[:, 0:512], w[:, 512:1536], w[:, 1536:1544]
    qkv, f, sc = w[:, 1544:2312], w[:, 2312:2316], w[:, 2316:3084]
    padblk = jnp.zeros((w.shape[0], LANES), w.dtype).at[:, PAD_DT0:PAD_DT0 + 8].set(dt).at[:, PAD_F0:PAD_F0 + 4].set(f)
    return jnp.concatenate([z, xbc, qkv, sc, padblk], axis=1)


def _unperm_mix_w_in(wp):
    z, xbc, qkv, sc = wp[:, 0:512], wp[:, 512:1536], wp[:, 1536:2304], wp[:, 2304:3072]
    dt, f = wp[:, P_PAD + PAD_DT0:P_PAD + PAD_DT0 + 8], wp[:, P_PAD + PAD_F0:P_PAD + PAD_F0 + 4]
    return jnp.concatenate([z, xbc, dt, qkv, f, sc], axis=1)


def _heads(m):
    return jnp.transpose(m.reshape(m.shape[0], FOX_HEADS, FOX_HD), (1, 0, 2))


def _unheads(m):
    return jnp.transpose(m, (1, 0, 2)).reshape(m.shape[1], FOX_W)


def _ffn_fwd(h, w_in, w_out, tag):
    u, a = _ffn_in_swiglu(h, w_in, f"ffn_in_{tag}")
    y = _matmul(a, w_out, "nn", F32, f"ffn_out_{tag}")
    return y, (h, u, a)


def _ffn_bwd(dy, saved, w_in, w_out, tag):
    h, u, a = saved
    du = _ffn_out_dx_swiglu(dy, w_out, u, f"ffn_out_dx_{tag}")
    dw_out = _matmul(a, dy, "tn", BF16, f"ffn_out_dw_{tag}")
    dh = _matmul(du, w_in, "nt", F32, f"ffn_in_dx_{tag}", halves="a")
    dw_in = _matmul(h, du, "tn", BF16, f"ffn_in_dw_{tag}", halves="b")
    return dh, dw_in, dw_out


def _mix_fwd(h, wp, w_out, sp, tag):
    proj = _matmul(h, wp, "nn", F32, f"mix_in_{tag}")
    pad_t = jnp.transpose(proj[:, P_PAD:P_PAD + LANES])
    xbc = _ssd_conv_fwd(proj, sp["conv_w"], sp["conv_b"])
    y_ssd, ypre, states = _ssd_fwd(xbc, proj, pad_t, sp["dtb_row"], sp["alog_row"], sp["d_row"],
                                   sp["dtb_col"], sp["alog_col"], sp["ng"])
    cf = _cum_logf(proj, sp["fbias_row"])
    cf4 = jnp.transpose(cf[:, PAD_F0:PAD_F0 + FOX_HEADS])
    cf_col, cf_row = cf4[:, :, None], cf4[:, None, :]
    q = _heads(proj[:, P_QKV:P_QKV + 256].astype(BF16))
    k = _heads(proj[:, P_QKV + 256:P_QKV + 512].astype(BF16))
    v = _heads(proj[:, P_QKV + 512:P_QKV + 768].astype(BF16))
    o, lse = _fox_fwd(q, k, v, cf_col, cf_row)
    y_sc = _shortconv_fwd(proj, sp["sconv_w"])
    ymix = jnp.concatenate([y_ssd, _unheads(o).astype(BF16), y_sc], axis=1)
    y = _matmul(ymix, w_out, "nn", F32, f"mix_out_{tag}")
    return y, (h, proj, pad_t, xbc, ypre, states, q, k, v, cf_col, cf_row, o, lse, ymix)


def _mix_bwd(dy, saved, wp, w_out, sp, tag):
    h, proj, pad_t, xbc, ypre, states, q, k, v, cf_col, cf_row, o, lse, ymix = saved
    dymix = _matmul(dy, w_out, "nt", F32, f"mix_out_dx_{tag}")
    dw_out = _matmul(ymix, dy, "tn", BF16, f"mix_out_dw_{tag}")
    dxbc, ddt, dz, ssd_acc, dng = _ssd_bwd(dymix, 0, xbc, proj, pad_t, ypre, states, sp["dtb_row"],
                                           sp["alog_row"], sp["d_row"], sp["dtb_col"], sp["alog_col"], sp["ng"])
    dxbc_raw, dconv_w, dconv_b = _ssd_conv_bwd(dxbc, proj, sp["conv_w"], sp["conv_b"])
    do = _heads(dymix[:, SSD_W:SSD_W + FOX_W])
    dq, dk, dv, dcq, dck = _fox_bwd(q, k, v, cf_col, cf_row, o, lse, do)
    dcf4 = dcq[:, :, 0] + dck[:, 0, :]
    dcf = jnp.zeros((h.shape[0], LANES), F32).at[:, PAD_F0:PAD_F0 + FOX_HEADS].set(jnp.transpose(dcf4))
    dpad, dfb = _pad_block_bwd(dcf, ddt, proj, sp["fbias_row"])
    dscb, dscc, dscx, dsconv_w = _shortconv_bwd(dymix, SSD_W + FOX_W, proj, sp["sconv_w"])
    dproj = jnp.concatenate([dz, dxbc_raw.astype(BF16), _unheads(dq).astype(BF16), _unheads(dk).astype(BF16),
                             _unheads(dv).astype(BF16), dscb, dscc, dscx, dpad], axis=1)
    dh = _matmul(dproj, wp, "nt", F32, f"mix_in_dx_{tag}")
    dwp = _matmul(h, dproj, "tn", BF16, f"mix_in_dw_{tag}")
    small = dict(conv_w=dconv_w, conv_b=dconv_b[0], dt_bias=ssd_acc[2, 0:8], a_log=ssd_acc[0, 0:8],
                 d=ssd_acc[1, 0:8], norm_g=dng[0], f_bias=dfb[0, PAD_F0:PAD_F0 + FOX_HEADS], sconv_w=dsconv_w)
    return dh, _unperm_mix_w_in(dwp), dw_out, small


def _local_step(x, tgt, mod, wfull, small_p):
    row = lambda v: v.reshape(1, -1)
    subs = [(l, j) for l in range(DEPTH) for j in range(N_SUB)]
    factor = (0.5, 1.0, 0.5)
    w_names = (("ffn1_w_in", "ffn1_w_out"), ("mix_w_in", "mix_w_out"), ("ffn2_w_in", "ffn2_w_out"))
    lg = [[row(small_p["ln_g"][l, j]) for j in range(N_SUB)] for l in range(DEPTH)]
    lb = [[row(small_p["ln_b"][l, j]) for j in range(N_SUB)] for l in range(DEPTH)]
    sps, wps = [], []
    for l in range(DEPTH):
        w = wfull[l]
        sp = dict(
            conv_w=small_p["ssd_conv_w"][l], conv_b=row(small_p["ssd_conv_b"][l]),
            dtb_row=_pad_lanes(small_p["ssd_dt_bias"][l], PAD_DT0), alog_row=_pad_lanes(small_p["ssd_a_log"][l], 0),
            d_row=_pad_lanes(small_p["ssd_d"][l], 0), ng=row(small_p["ssd_norm_g"][l]),
            fbias_row=_pad_lanes(small_p["fox_f_bias"][l], PAD_F0), sconv_w=small_p["sconv_w"][l])
        sp["dtb_col"] = jnp.transpose(sp["dtb_row"])
        sp["alog_col"] = jnp.transpose(sp["alog_row"])
        sps.append(sp)
        wps.append(_perm_mix_w_in(w["mix_w_in"]))
    tags = [f"l{l}{('f1', 'mx', 'f2')[j]}" for l, j in subs]

    x0, h = _ln_in_fwd(x, row(small_p["ln_in_g"]), row(small_p["ln_in_b"]), mod[0, 0])
    cur = x0
    xins, ys, inner = [], [], []
    for idx, (l, j) in enumerate(subs):
        w = wfull[l]
        if j == 1:
            y, sv = _mix_fwd(h, wps[l], w["mix_w_out"], sps[l], tags[idx])
        else:
            y, sv = _ffn_fwd(h, w[w_names[j][0]], w[w_names[j][1]], tags[idx])
        nxt = mod[subs[idx + 1]] if idx + 1 < len(subs) else None
        xins.append(cur)
        ys.append(y)
        inner.append(sv)
        cur, h = _res_ln(cur, y, mod[l, j], lg[l][j], lb[l][j], factor[j], f"res_ln_{tags[idx]}", nxt)

    dcur, loss_acc = _loss_head(cur, tgt)
    last = len(subs) - 1
    l, j = subs[last]
    dres, dy, acc = _res_ln_bwd(dcur, xins[last], ys[last], mod[l, j], lg[l][j], factor[j], f"res_ln_bwd_{tags[last]}")
    ln_acc = {last: acc}
    shift_scale = {}
    big_grads = [dict() for _ in range(DEPTH)]
    small_g = [None] * DEPTH
    for idx in reversed(range(len(subs))):
        l, j = subs[idx]
        w = wfull[l]
        if j == 1:
            dh, g_in, g_out, small_g[l] = _mix_bwd(dy, inner[idx], wps[l], w["mix_w_out"], sps[l], tags[idx])
        else:
            dh, g_in, g_out = _ffn_bwd(dy, inner[idx], w[w_names[j][0]], w[w_names[j][1]], tags[idx])
        big_grads[l][w_names[j][0]], big_grads[l][w_names[j][1]] = g_in, g_out
        if idx > 0:
            pl_, pj = subs[idx - 1]
            dres, dy, acc5 = _mod_res_bwd(dres, dh, mod[l, j], xins[idx - 1], ys[idx - 1], mod[pl_, pj], lg[pl_][pj],
                                          lb[pl_][pj], factor[pj], f"mod_res_bwd_{tags[idx]}")
            shift_scale[idx], ln_acc[idx - 1] = acc5[0:2], acc5[2:5]
        else:
            dx0, shift_scale[0] = _modulate_bwd(dres, dh, x0, mod[0, 0], "mod_bwd_first")
    dx, acc_in = _ln_in_bwd(dx0, x, row(small_p["ln_in_g"]))
    dmod = []
    for l in range(DEPTH):
        ids = [N_SUB * l + j for j in range(N_SUB)]
        small_g[l]["ln_g"] = jnp.stack([ln_acc[i][0] for i in ids])
        small_g[l]["ln_b"] = jnp.stack([ln_acc[i][1] for i in ids])
        dmod.append(jnp.stack([jnp.concatenate([shift_scale[i], ln_acc[i][2:3]], axis=0) for i in ids]))
    return loss_acc[0, 0], dx, big_grads, small_g, jnp.stack(dmod), acc_in


SMALL_ORDER = ("ssd_conv_w", "ssd_conv_b", "ssd_dt_bias", "ssd_a_log", "ssd_d", "ssd_norm_g", "fox_f_bias",
               "sconv_w", "ln_g", "ln_b")
SMALL_KEY = dict(ssd_conv_w="conv_w", ssd_conv_b="conv_b", ssd_dt_bias="dt_bias", ssd_a_log="a_log", ssd_d="d",
                 ssd_norm_g="norm_g", fox_f_bias="f_bias", sconv_w="sconv_w", ln_g="ln_g", ln_b="ln_b")
COL_SHARDED_SMALL = ("ssd_conv_w", "sconv_w", "ln_g", "ln_b")


def _pad_to(v, n):
    return jnp.concatenate([v, jnp.zeros((n - v.shape[0],), v.dtype)])


def kernel(x, c, ln_in_g, ln_in_b, ada_w, ada_b, ffn1_w_in, ffn1_w_out, mix_w_in, mix_w_out, ssd_conv_w, ssd_conv_b, ssd_dt_bias, ssd_a_log, ssd_d, ssd_norm_g, fox_f_bias, sconv_w, ffn2_w_in, ffn2_w_out, ln_g, ln_b, loss_target, m_ln_in_g, m_ln_in_b, m_ada_w, m_ada_b, m_ffn1_w_in, m_ffn1_w_out, m_mix_w_in, m_mix_w_out, m_ssd_conv_w, m_ssd_conv_b, m_ssd_dt_bias, m_ssd_a_log, m_ssd_d, m_ssd_norm_g, m_fox_f_bias, m_sconv_w, m_ffn2_w_in, m_ffn2_w_out, m_ln_g, m_ln_b, v_ln_in_g, v_ln_in_b, v_ada_w, v_ada_b, v_ffn1_w_in, v_ffn1_w_out, v_mix_w_in, v_mix_w_out, v_ssd_conv_w, v_ssd_conv_b, v_ssd_dt_bias, v_ssd_a_log, v_ssd_d, v_ssd_norm_g, v_fox_f_bias, v_sconv_w, v_ffn2_w_in, v_ffn2_w_out, v_ln_g, v_ln_b):
    names = ("ln_in_g", "ln_in_b", "ada_w", "ada_b", "ffn1_w_in", "ffn1_w_out", "mix_w_in", "mix_w_out",
             "ssd_conv_w", "ssd_conv_b", "ssd_dt_bias", "ssd_a_log", "ssd_d", "ssd_norm_g", "fox_f_bias", "sconv_w",
             "ffn2_w_in", "ffn2_w_out", "ln_g", "ln_b")
    w_loc = dict(zip(names, (ln_in_g, ln_in_b, ada_w, ada_b, ffn1_w_in, ffn1_w_out, mix_w_in, mix_w_out, ssd_conv_w,
                             ssd_conv_b, ssd_dt_bias, ssd_a_log, ssd_d, ssd_norm_g, fox_f_bias, sconv_w, ffn2_w_in,
                             ffn2_w_out, ln_g, ln_b)))
    m_loc = dict(zip(names, (m_ln_in_g, m_ln_in_b, m_ada_w, m_ada_b, m_ffn1_w_in, m_ffn1_w_out, m_mix_w_in,
                             m_mix_w_out, m_ssd_conv_w, m_ssd_conv_b, m_ssd_dt_bias, m_ssd_a_log, m_ssd_d,
                             m_ssd_norm_g, m_fox_f_bias, m_sconv_w, m_ffn2_w_in, m_ffn2_w_out, m_ln_g, m_ln_b)))
    v_loc = dict(zip(names, (v_ln_in_g, v_ln_in_b, v_ada_w, v_ada_b, v_ffn1_w_in, v_ffn1_w_out, v_mix_w_in,
                             v_mix_w_out, v_ssd_conv_w, v_ssd_conv_b, v_ssd_dt_bias, v_ssd_a_log, v_ssd_d,
                             v_ssd_norm_g, v_fox_f_bias, v_sconv_w, v_ffn2_w_in, v_ffn2_w_out, v_ln_g, v_ln_b)))

    xi, yi, ci = lax.axis_index("x"), lax.axis_index("y"), lax.axis_index("c")
    me = 2 * xi + yi
    dev = 2 * me + ci

    def gather8(v, tag):
        v2 = v.reshape(1, -1)
        got = _bcast_chips([v2], f"gather_chips_{tag}")
        same_c = jnp.concatenate(_by_chip(me, v2, [g[0] for g in got]), axis=0)
        other_c = _swap_sibling([same_c], f"gather_sibling_{tag}")[0]
        pair = lax.switch(ci, [lambda a, b: jnp.stack([a, b], axis=1), lambda a, b: jnp.stack([b, a], axis=1)],
                          same_c, other_c)
        return pair.reshape(8, -1)

    def chip_concat(own, got3, axis):
        return jnp.concatenate(_by_chip(me, own, got3), axis=axis)

    small_cols = [w_loc[n].reshape(-1, w_loc[n].shape[-1]) for n in COL_SHARDED_SMALL]
    got = _bcast_chips(small_cols, "gather_small_params")
    small_p = {n: w_loc[n] for n in ("ln_in_g", "ln_in_b", "ssd_conv_b", "ssd_dt_bias", "ssd_a_log", "ssd_d",
                                     "ssd_norm_g", "fox_f_bias")}
    for i, n in enumerate(COL_SHARDED_SMALL):
        full = chip_concat(small_cols[i], [g[i] for g in got], 1)
        small_p[n] = full.reshape(w_loc[n].shape[:-1] + (full.shape[-1],))

    gathered = _gather_weights({n: w_loc[n].astype(BF16) for n in BIG})
    wfull = []
    for l in range(DEPTH):
        g = gathered[l]
        wl = {n: g[n] if n in BY_COLUMNS else g[n].reshape(-1, g[n].shape[-1]) for n in BIG if n != "mix_w_in"}
        wl["mix_w_in"] = jnp.concatenate([g["mix_w_in"][s] for s in range(4)], axis=1)
        wfull.append(wl)

    c_all = gather8(c[0], "c")
    c_act = _silu_bf16(c_all)
    ada_w_b = ada_w.astype(BF16)
    mod_loc = [_matmul(c_act, ada_w_b[l], "nn", F32, f"ada_fwd_l{l}") for l in range(DEPTH)]
    mod_loc = jnp.stack(mod_loc)
    got = _bcast_chips([mod_loc], "gather_mod")
    mod_all = chip_concat(mod_loc, [g[0] for g in got], 2)
    mod = lax.dynamic_index_in_dim(mod_all, dev, 1, keepdims=False) + ada_b
    mod = mod.reshape(DEPTH, N_SUB, 3, D)

    loss_part, dx, big_g, small_g, dmod, acc_in = _local_step(x[0], loss_target[0], mod, wfull, small_p)
    loss = lax.psum(loss_part, ("x", "y", "c"))

    pieces = [dmod.reshape(-1), acc_in[0], acc_in[1]]
    for n in SMALL_ORDER:
        pieces.append(jnp.stack([small_g[l][SMALL_KEY[n]] for l in range(DEPTH)]).reshape(-1))
    sizes = [p.shape[0] for p in pieces]
    total = sum(sizes)
    padded = -(-total // 1024) * 1024
    vec = _pad_to(jnp.concatenate(pieces), padded)
    all_rows = gather8(vec, "small_grads")
    summed = _sum_rows(all_rows, "sum_small_grads")[0]
    offs = [0]
    for s in sizes:
        offs.append(offs[-1] + s)
    n_mod = sizes[0]
    grads = {"ada_b": summed[0:n_mod].reshape(DEPTH, 3 * N_SUB * D),
             "ln_in_g": summed[offs[1]:offs[2]], "ln_in_b": summed[offs[2]:offs[3]]}
    for i, n in enumerate(SMALL_ORDER):
        full = summed[offs[3 + i]:offs[4 + i]].reshape(small_p[n].shape)
        if n in COL_SHARDED_SMALL:
            wcols = w_loc[n].shape[-1]
            full = lax.dynamic_slice_in_dim(full, me * wcols, wcols, axis=full.ndim - 1)
        grads[n] = full

    dmod_all = all_rows[:, 0:n_mod].reshape(8, DEPTH, 3 * N_SUB * D)
    ncol = ada_w.shape[-1]
    dmod_cols = lax.dynamic_slice_in_dim(dmod_all, me * ncol, ncol, axis=2).astype(BF16)
    grads["ada_w"] = jnp.stack([_matmul(c_act, dmod_cols[:, l], "tn", F32, f"ada_bwd_l{l}") for l in range(DEPTH)])

    def wire_form(n, g):
        if n in BY_COLUMNS:
            return g
        if n == "mix_w_in":
            return jnp.transpose(g.reshape(g.shape[0], 4, g.shape[1] // 4), (1, 0, 2))
        return g.reshape(4, g.shape[0] // 4, g.shape[1])

    g_wire = [[wire_form(n, big_g[l][n]) for n in BIG] for l in range(DEPTH)]
    from_sib = _grad_swap_sibling(g_wire)
    ci_arr = jnp.reshape(ci, (1,)).astype(jnp.int32)
    chip_part = []
    for i, n in enumerate(BIG):
        shape = from_sib[i].shape
        flat = lambda a: a.reshape(-1, a.shape[-1])
        part = _grad_add_cores(flat(g_wire[0][i]), flat(g_wire[1][i]), flat(from_sib[i]), ci_arr,
                               f"grad_add_cores_{n}")
        chip_part.append(part.reshape(shape))
    got = _grad_scatter_chips(chip_part)
    me_ci = jnp.stack([me, ci]).astype(jnp.int32)
    reduced = [_grad_add_chips(chip_part[i], [g[i] for g in got], n, me_ci, f"grad_add_chips_{n}")
               for i, n in enumerate(BIG)]
    reduced = _grad_swap_back(reduced)
    for i, n in enumerate(BIG):
        grads[n] = reduced[i]

    delta, new_m, new_v = {}, {}, {}
    for n in ("ada_w",) + BIG:
        delta[n], new_m[n], new_v[n] = _adamw(w_loc[n], grads[n], m_loc[n], v_loc[n], f"adamw_{n}")
    small_names = [n for n in names if n not in ("ada_w",) + BIG]
    flat = lambda d: jnp.concatenate([d[n].reshape(-1) for n in small_names])
    n_small = sum(w_loc[n].size for n in small_names)
    n_pad = -(-n_small // 1024) * 1024
    packed = [_pad_to(flat(d), n_pad).reshape(-1, LANES) for d in (w_loc, grads, m_loc, v_loc)]
    d_s, m_s, v_s = _adamw(*packed, "adamw_small")
    off = 0
    for n in small_names:
        sz = w_loc[n].size
        delta[n] = d_s.reshape(-1)[off:off + sz].reshape(w_loc[n].shape)
        new_m[n] = m_s.reshape(-1)[off:off + sz].reshape(w_loc[n].shape)
        new_v[n] = v_s.reshape(-1)[off:off + sz].reshape(w_loc[n].shape)
        off += sz

    return (loss, dx[None], *[grads[n] for n in names], *[delta[n] for n in names],
            *[new_m[n] for n in names], *[new_v[n] for n in names])
```

```python
import functools

import jax
import jax.numpy as jnp
from jax import lax
from jax.experimental import pallas as pl
from jax.experimental.pallas import tpu as pltpu
from jax.experimental.pallas import tpu_sc as plsc

F32 = jnp.float32
BF16 = jnp.bfloat16
MESH = pl.DeviceIdType.MESH

D = 1024
DEPTH = 2
N_SUB = 3
D_FF = 2816
FF_TILE = D_FF // 2
ALPHA = (2 * DEPTH) ** 0.25
LN_EPS = 1e-5
RMS_EPS = 1e-5
SSD_W = 512
SSD_HEADS = 8
SSD_CONV_K = 4
SSD_CONV_DIM = 1024
FOX_W = 256
FOX_HEADS = 4
FOX_HD = 64
SC_W = 256
SC_K = 3
D_IN_PROJ = 3084
P_Z, P_XBC, P_QKV, P_SC, P_PAD = 0, 512, 1536, 2304, 3072
D_PROJ_PAD = 3200
PAD_DT0, PAD_F0 = 0, 8
SSD_CHUNK = 256
ATT_BLOCK = 512
CUM_BLOCK = 256
LANES = 128
VMEM_LIMIT = 56 * 1024 * 1024

ADAM_LR, ADAM_B1, ADAM_B2, ADAM_EPS, ADAM_WD, ADAM_STEP = 0.001, 0.9, 0.999, 1e-08, 0.01, 10
NEG = -1e30


def _cp(*sem):
    return pltpu.CompilerParams(dimension_semantics=sem, vmem_limit_bytes=VMEM_LIMIT)


def _pick(n, cands):
    for c in cands:
        if n % c == 0:
            return c
    return n


def _dot(a, b):
    return lax.dot_general(a, b, (((1,), (0,)), ((), ())), preferred_element_type=F32)


def _dot_nt(a, b):
    return lax.dot_general(a, b, (((1,), (1,)), ((), ())), preferred_element_type=F32)


def _dot_tn(a, b):
    return lax.dot_general(a, b, (((0,), (0,)), ((), ())), preferred_element_type=F32)


def _sigmoid(x):
    return 1.0 / (1.0 + jnp.exp(-x))


def _softplus(x):
    return jnp.maximum(x, 0.0) + jnp.log(1.0 + jnp.exp(-jnp.abs(x)))


def _split3(v):
    h = v.astype(BF16)
    r = v - h.astype(F32)
    m = r.astype(BF16)
    l = (r - m.astype(F32)).astype(BF16)
    return h, m, l


def _tri_left(tri, v):
    h, m, l = _split3(v)
    return _dot(tri, h) + _dot(tri, m) + _dot(tri, l)


def _tri_right(v, tri):
    h, m, l = _split3(v)
    return _dot(h, tri) + _dot(m, tri) + _dot(l, tri)


def _matmul(a, b, mode, out_dtype, name, halves=None):
    assert a.dtype == BF16 and b.dtype == BF16, (name, a.dtype, b.dtype)
    if halves == "a":
        assert mode == "nt" and a.shape[0] == 2 and a.shape[2] == D_FF
        m, k, n = a.shape[1], 2 * D_FF, b.shape[0]
    elif halves == "b":
        assert mode == "tn" and b.shape[0] == 2 and b.shape[2] == D_FF
        (k, m), n = a.shape, 2 * D_FF
    elif mode == "nn":
        (m, k), n = a.shape, b.shape[1]
    elif mode == "nt":
        (m, k), n = a.shape, b.shape[0]
    else:
        (k, m), n = a.shape, b.shape[1]
    tm = m if m <= 1024 else _pick(m, (1024, 1408, 512, 256, 128))
    tn = n if n <= 1024 else _pick(n, (1408, 640, 512, 256, 128))
    tk = k if k <= 1024 else _pick(k, (1408, 1024, 640, 512, 256, 128))
    nk = k // tk
    if mode == "nn":
        dn = (((1,), (0,)), ((), ()))
        a_spec = pl.BlockSpec((tm, tk), lambda i, j, kk: (i, kk))
        b_spec = pl.BlockSpec((tk, tn), lambda i, j, kk: (kk, j))
    elif mode == "nt":
        dn = (((1,), (1,)), ((), ()))
        a_spec = pl.BlockSpec((tm, tk), lambda i, j, kk: (i, kk))
        b_spec = pl.BlockSpec((tn, tk), lambda i, j, kk: (j, kk))
    else:
        dn = (((0,), (0,)), ((), ()))
        a_spec = pl.BlockSpec((tk, tm), lambda i, j, kk: (kk, i))
        b_spec = pl.BlockSpec((tk, tn), lambda i, j, kk: (kk, j))
    per_half = D_FF // FF_TILE
    if halves == "a":
        assert tk == FF_TILE
        a_spec = pl.BlockSpec((None, tm, tk), lambda i, j, kk: (kk // per_half, i, kk % per_half))
    elif halves == "b":
        assert tn == FF_TILE
        b_spec = pl.BlockSpec((None, tk, tn), lambda i, j, kk: (j // per_half, kk, j % per_half))

    def body(a_ref, b_ref, o_ref, *acc):
        d = lax.dot_general(a_ref[...], b_ref[...], dn, preferred_element_type=F32)
        if nk == 1:
            o_ref[...] = d.astype(o_ref.dtype)
            return
        acc_ref, = acc
        kk = pl.program_id(2)

        @pl.when(kk == 0)
        def _():
            acc_ref[...] = d

        @pl.when((kk > 0) & (kk < nk - 1))
        def _():
            acc_ref[...] += d

        @pl.when(kk == nk - 1)
        def _():
            o_ref[...] = (acc_ref[...] + d).astype(o_ref.dtype)

    return pl.pallas_call(
        body, name=name, grid=(m // tm, n // tn, nk),
        in_specs=[a_spec, b_spec],
        out_specs=pl.BlockSpec((tm, tn), lambda i, j, kk: (i, j)),
        out_shape=jax.ShapeDtypeStruct((m, n), out_dtype),
        scratch_shapes=[pltpu.VMEM((tm, tn), F32)] if nk > 1 else [],
        compiler_params=_cp("parallel", "parallel", "arbitrary"),
    )(a, b)


def _rows(body, name, t, tb, row_in, full_in, row_out, acc_out):
    in_specs, args = [], []
    for r in row_in:
        if isinstance(r, tuple):
            arr, w, j = r
            in_specs.append(pl.BlockSpec((tb, w), functools.partial(lambda i, jj: (i, jj), jj=j)))
            args.append(arr)
        else:
            in_specs.append(pl.BlockSpec((tb, r.shape[1]), lambda i: (i, 0)))
            args.append(r)
    for f in full_in:
        in_specs.append(pl.BlockSpec(f.shape, functools.partial(lambda i, nd: (0,) * nd, nd=f.ndim)))
        args.append(f)
    out_specs = [pl.BlockSpec((tb, c), lambda i: (i, 0)) for c, _ in row_out]
    out_specs += [pl.BlockSpec(s, functools.partial(lambda i, nd: (0,) * nd, nd=len(s))) for s in acc_out]
    out_shape = [jax.ShapeDtypeStruct((t, c), dt) for c, dt in row_out]
    out_shape += [jax.ShapeDtypeStruct(s, F32) for s in acc_out]
    return pl.pallas_call(
        body, name=name, grid=(t // tb,), in_specs=in_specs, out_specs=out_specs, out_shape=out_shape,
        compiler_params=_cp("arbitrary"),
    )(*args)


def _ln_stats(r):
    mu = jnp.mean(r, axis=-1, keepdims=True)
    xc = r - mu
    var = jnp.mean(xc * xc, axis=-1, keepdims=True)
    rstd = lax.rsqrt(var + LN_EPS)
    return xc * rstd, rstd


def _ln_bwd(dout, xhat, rstd, g):
    dxh = dout * g
    m1 = jnp.mean(dxh, axis=-1, keepdims=True)
    m2 = jnp.mean(dxh * xhat, axis=-1, keepdims=True)
    return rstd * (dxh - m1 - xhat * m2)


def _first(i, acc_refs):
    @pl.when(i == 0)
    def _():
        for a in acc_refs:
            a[...] = jnp.zeros_like(a)


def _modulated(xv, m_ref):
    return (xv * (1.0 + m_ref[1:2, :]) + m_ref[0:1, :]).astype(BF16)


def _ln_in_fwd(x, g, b, next_mod3):
    t = x.shape[0]

    def body(x_ref, g_ref, b_ref, m_ref, o_ref, h_ref):
        xhat, _ = _ln_stats(x_ref[...])
        out = xhat * g_ref[...] + b_ref[...]
        o_ref[...] = out
        h_ref[...] = _modulated(out, m_ref)

    return _rows(body, "ln_in_fwd", t, 256, [x], [g, b, next_mod3], [(D, F32), (D, BF16)], [])


def _ln_in_bwd(dx0, x, g):
    t = x.shape[0]

    def body(d_ref, x_ref, g_ref, o_ref, acc_ref):
        _first(pl.program_id(0), [acc_ref])
        xhat, rstd = _ln_stats(x_ref[...])
        d = d_ref[...]
        o_ref[...] = _ln_bwd(d, xhat, rstd, g_ref[...])
        acc_ref[0:1, :] += jnp.sum(d * xhat, axis=0, keepdims=True)
        acc_ref[1:2, :] += jnp.sum(d, axis=0, keepdims=True)

    return _rows(body, "ln_in_bwd", t, 256, [dx0, x], [g], [(D, F32)], [(2, D)])


def _modulate_bwd(dxres, dh, xin, mod3, name):
    t = xin.shape[0]

    def body(r_ref, dh_ref, x_ref, m_ref, o_ref, acc_ref):
        _first(pl.program_id(0), [acc_ref])
        dh_v = dh_ref[...]
        o_ref[...] = r_ref[...] + dh_v * (1.0 + m_ref[1:2, :])
        acc_ref[0:1, :] += jnp.sum(dh_v, axis=0, keepdims=True)
        acc_ref[1:2, :] += jnp.sum(dh_v * x_ref[...], axis=0, keepdims=True)

    return _rows(body, name, t, 256, [dxres, dh, xin], [mod3], [(D, F32)], [(2, D)])


def _ffn_in_swiglu(h, w_in, name):
    t = h.shape[0]
    tm = _pick(t, (512, 256, 128))
    nj = D_FF // FF_TILE

    def body(h_ref, wg_ref, wu_ref, u_ref, a_ref):
        hv = h_ref[...]
        gate = _dot(hv, wg_ref[...])
        up = _dot(hv, wu_ref[...])
        u_ref[0] = gate.astype(BF16)
        u_ref[1] = up.astype(BF16)
        a_ref[...] = (gate * _sigmoid(gate) * up).astype(BF16)

    return pl.pallas_call(
        body, name=name, grid=(nj, t // tm),
        in_specs=[pl.BlockSpec((tm, D), lambda j, i: (i, 0)),
                  pl.BlockSpec((D, FF_TILE), lambda j, i: (0, j)),
                  pl.BlockSpec((D, FF_TILE), lambda j, i: (0, nj + j))],
        out_specs=[pl.BlockSpec((2, tm, FF_TILE), lambda j, i: (0, i, j)),
                   pl.BlockSpec((tm, FF_TILE), lambda j, i: (i, j))],
        out_shape=[jax.ShapeDtypeStruct((2, t, D_FF), BF16), jax.ShapeDtypeStruct((t, D_FF), BF16)],
        compiler_params=_cp("parallel", "parallel"),
    )(h, w_in, w_in)


def _ffn_out_dx_swiglu(dy, w_out, u, name):
    t = dy.shape[0]
    tm = _pick(t, (512, 256, 128))
    nj = D_FF // FF_TILE

    def body(dy_ref, w_ref, u_ref, du_ref):
        da = _dot_nt(dy_ref[...], w_ref[...])
        gate = u_ref[0].astype(F32)
        up = u_ref[1].astype(F32)
        sg = _sigmoid(gate)
        du_ref[0] = (da * up * (sg * (1.0 + gate * (1.0 - sg)))).astype(BF16)
        du_ref[1] = (da * gate * sg).astype(BF16)

    blk3 = pl.BlockSpec((2, tm, FF_TILE), lambda j, i: (0, i, j))
    return pl.pallas_call(
        body, name=name, grid=(nj, t // tm),
        in_specs=[pl.BlockSpec((tm, D), lambda j, i: (i, 0)),
                  pl.BlockSpec((FF_TILE, D), lambda j, i: (j, 0)), blk3],
        out_specs=blk3,
        out_shape=jax.ShapeDtypeStruct((2, t, D_FF), BF16),
        compiler_params=_cp("parallel", "parallel"),
    )(dy, w_out, u)


def _res_ln(xin, y, mod3, lg, lb, factor, name, next_mod3=None):
    t = xin.shape[0]

    def body(x_ref, y_ref, m_ref, g_ref, b_ref, *rest):
        r = ALPHA * x_ref[...] + (factor * m_ref[2:3, :]) * y_ref[...]
        xhat, _ = _ln_stats(r)
        out = xhat * g_ref[...] + b_ref[...]
        if next_mod3 is None:
            rest[0][...] = out
        else:
            rest[1][...] = out
            rest[2][...] = _modulated(out, rest[0])

    if next_mod3 is None:
        return _rows(body, name, t, 256, [xin, y], [mod3, lg, lb], [(D, F32)], [])[0], None
    return _rows(body, name, t, 256, [xin, y], [mod3, lg, lb, next_mod3], [(D, F32), (D, BF16)], [])


def _mod_res_bwd(dxres, dh, mod3, xin_p, y_p, mod3_p, lg_p, lb_p, factor_p, name):
    t = dxres.shape[0]

    def body(r_ref, dh_ref, xp_ref, yp_ref, m_ref, mp_ref, g_ref, b_ref, dres_ref, dy_ref, acc_ref):
        _first(pl.program_id(0), [acc_ref])
        gate = factor_p * mp_ref[2:3, :]
        yv = yp_ref[...]
        xhat, rstd = _ln_stats(ALPHA * xp_ref[...] + gate * yv)
        xin = xhat * g_ref[...] + b_ref[...]
        dh_v = dh_ref[...]
        d = r_ref[...] + dh_v * (1.0 + m_ref[1:2, :])
        dr = _ln_bwd(d, xhat, rstd, g_ref[...])
        dres_ref[...] = ALPHA * dr
        dy_ref[...] = (gate * dr).astype(BF16)
        acc_ref[0:1, :] += jnp.sum(dh_v, axis=0, keepdims=True)
        acc_ref[1:2, :] += jnp.sum(dh_v * xin, axis=0, keepdims=True)
        acc_ref[2:3, :] += jnp.sum(d * xhat, axis=0, keepdims=True)
        acc_ref[3:4, :] += jnp.sum(d, axis=0, keepdims=True)
        acc_ref[4:5, :] += jnp.sum(factor_p * yv * dr, axis=0, keepdims=True)

    return _rows(body, name, t, 256, [dxres, dh, xin_p, y_p], [mod3, mod3_p, lg_p, lb_p],
                 [(D, F32), (D, BF16)], [(5, D)])


def _res_ln_bwd(dout, xin, y, mod3, lg, factor, name):
    t = xin.shape[0]

    def body(d_ref, x_ref, y_ref, m_ref, g_ref, dres_ref, dy_ref, acc_ref):
        _first(pl.program_id(0), [acc_ref])
        gate = factor * m_ref[2:3, :]
        yv = y_ref[...]
        r = ALPHA * x_ref[...] + gate * yv
        xhat, rstd = _ln_stats(r)
        d = d_ref[...]
        dr = _ln_bwd(d, xhat, rstd, g_ref[...])
        dres_ref[...] = ALPHA * dr
        dy_ref[...] = (gate * dr).astype(BF16)
        acc_ref[0:1, :] += jnp.sum(d * xhat, axis=0, keepdims=True)
        acc_ref[1:2, :] += jnp.sum(d, axis=0, keepdims=True)
        acc_ref[2:3, :] += jnp.sum(factor * yv * dr, axis=0, keepdims=True)

    return _rows(body, name, t, 256, [dout, xin, y], [mod3, lg], [(D, F32), (D, BF16)], [(3, D)])


def _loss_head(xf, tgt):
    t = xf.shape[0]

    def body(x_ref, t_ref, d_ref, acc_ref):
        _first(pl.program_id(0), [acc_ref])
        e = x_ref[...] - t_ref[...]
        d_ref[...] = e * (1.0 / D)
        part = 0.5 * jnp.sum(jnp.mean(e * e, axis=-1, keepdims=True), axis=0, keepdims=True)
        acc_ref[...] += jnp.broadcast_to(part, acc_ref.shape)

    return _rows(body, "loss_head", t, 256, [xf, tgt], [], [(D, F32)], [(1, LANES)])


def _silu_bf16(c_all):
    def body(c_ref, o_ref):
        v = c_ref[...]
        o_ref[...] = (v * _sigmoid(v)).astype(BF16)

    return _rows(body, "silu_c", c_all.shape[0], c_all.shape[0], [c_all], [], [(c_all.shape[1], BF16)], [])[0]


def _sum_rows(v, name):
    r, n = v.shape
    tn = _pick(n, (8192, 4096, 2048, 1024, 512, 256, 128))

    def body(v_ref, o_ref):
        acc = v_ref[0:1, :]
        for k in range(1, r):
            acc = acc + v_ref[k:k + 1, :]
        o_ref[...] = acc

    return pl.pallas_call(
        body, name=name, grid=(n // tn,),
        in_specs=[pl.BlockSpec((r, tn), lambda j: (0, j))],
        out_specs=pl.BlockSpec((1, tn), lambda j: (0, j)),
        out_shape=jax.ShapeDtypeStruct((1, n), F32),
        compiler_params=_cp("parallel"),
    )(v)


def _elementwise(fn, name, ins, out_dtypes):
    r, c = ins[0].shape
    tb = _pick(r, (128, 64, 32, 16, 8))
    n_in = len(ins)

    def body(*refs):
        outs = fn(*[x[...] for x in refs[:n_in]])
        for o_ref, o in zip(refs[n_in:], outs):
            o_ref[...] = o.astype(o_ref.dtype)

    spec = pl.BlockSpec((tb, c), lambda i: (i, 0))
    return pl.pallas_call(
        body, name=name, grid=(r // tb,), in_specs=[spec] * n_in, out_specs=[spec] * len(out_dtypes),
        out_shape=[jax.ShapeDtypeStruct((r, c), dt) for dt in out_dtypes],
        compiler_params=_cp("parallel"),
    )(*ins)


def _adamw_math(w, g, m, v):
    m = ADAM_B1 * m + (1.0 - ADAM_B1) * g
    v = ADAM_B2 * v + (1.0 - ADAM_B2) * (g * g)
    m_hat = m / (1.0 - ADAM_B1 ** ADAM_STEP)
    v_hat = v / (1.0 - ADAM_B2 ** ADAM_STEP)
    delta = -ADAM_LR * (m_hat / (jnp.sqrt(v_hat) + ADAM_EPS) + ADAM_WD * w)
    return delta, m, v


def _adamw(w, g, m, v, name):
    shape = w.shape
    c = shape[-1]
    w2, g2, m2, v2 = (a.reshape(-1, c) for a in (w, g, m, v))
    outs = _elementwise(_adamw_math, name, [w2, g2, m2, v2], [F32, F32, F32])
    return tuple(o.reshape(shape) for o in outs)


def _remote_exchange(ins, plan, peers_of, name):
    n_in, n_out = len(ins), len(plan)

    def body(*refs):
        in_refs, out_refs = refs[:n_in], refs[n_in:n_in + n_out]
        send_sems, recv_sems = refs[n_in + n_out], refs[n_in + n_out + 1]
        peers = peers_of(lax.axis_index("x"), lax.axis_index("y"), lax.axis_index("c"))
        copies = [
            pltpu.make_async_remote_copy(
                src_ref=in_refs[src], dst_ref=out_refs[k], send_sem=send_sems.at[k], recv_sem=recv_sems.at[k],
                device_id=peers[peer], device_id_type=MESH)
            for k, (peer, src) in enumerate(plan)
        ]
        for cp in copies:
            cp.start()
        for cp in copies:
            cp.wait()

    any_spec = pl.BlockSpec(memory_space=pl.ANY)
    return list(pl.pallas_call(
        body, name=name,
        in_specs=[any_spec] * n_in, out_specs=[any_spec] * n_out,
        out_shape=[jax.ShapeDtypeStruct(ins[src].shape, ins[src].dtype) for _, src in plan],
        scratch_shapes=[pltpu.SemaphoreType.DMA((n_out,)), pltpu.SemaphoreType.DMA((n_out,))],
    )(*ins))


def _sibling(x, y, c):
    return [(x, y, 1 - c)]


def _other_chips(x, y, c):
    return [(1 - x, y, c), (x, 1 - y, c), (1 - x, 1 - y, c)]


def _swap_sibling(arrs, name):
    return _remote_exchange(arrs, [(0, i) for i in range(len(arrs))], _sibling, name)


def _bcast_chips(arrs, name):
    n = len(arrs)
    out = _remote_exchange(arrs, [(p, i) for p in range(3) for i in range(n)], _other_chips, name)
    return [out[p * n:(p + 1) * n] for p in range(3)]


def _by_chip(me, own, got3):
    by_rel = [own, got3[0], got3[1], got3[2]]
    rel_bits = (0, 2, 1, 3)

    def branch(m):
        def f(ops):
            return [ops[rel_bits.index(i ^ m)] for i in range(4)]
        return f

    return lax.switch(me, [branch(m) for m in range(4)], by_rel)


BIG = ("ffn1_w_in", "ffn1_w_out", "mix_w_in", "mix_w_out", "ffn2_w_in", "ffn2_w_out")
BY_COLUMNS = ("ffn1_w_in", "ffn2_w_in")


def _layer_shape(n, shard_shape):
    r, cs = shard_shape
    return (r, 4 * cs) if n in BY_COLUMNS else (4, r, cs)


def _slot(ref, n, chip):
    if n in BY_COLUMNS:
        w = ref.shape[-1] // 4
        return ref.at[:, pl.ds(pl.multiple_of(chip * w, LANES), w)]
    return ref.at[chip]


def _chip_ids(x, y):
    chips = [(1 - x, y), (x, 1 - y), (1 - x, 1 - y)]
    return chips, [2 * cx + cy for cx, cy in chips]


def _gather_weights(shards):
    n_w = len(BIG)
    ins = [shards[n] for n in BIG]

    def body(*refs):
        in_refs = refs[:n_w]
        outs = [refs[n_w:2 * n_w], refs[2 * n_w:3 * n_w]]
        send1, recv1, local, send2, recv2 = refs[3 * n_w:]
        x, y, c = lax.axis_index("x"), lax.axis_index("y"), lax.axis_index("c")
        me = 2 * x + y
        chips, chip_idx = _chip_ids(x, y)
        for l in range(DEPTH):
            @pl.when(c == l)
            def _(l=l):
                firsts, own_copies = [], []
                for i, n in enumerate(BIG):
                    src = in_refs[i].at[l]
                    own = pltpu.make_async_copy(src, _slot(outs[l][i], n, me), local.at[i])
                    own.start()
                    own_copies.append(own)
                    for p in range(3):
                        k = p * n_w + i
                        cp = pltpu.make_async_remote_copy(
                            src_ref=src, dst_ref=_slot(outs[l][i], n, me), send_sem=send1.at[k],
                            recv_sem=recv1.at[k], device_id=(*chips[p], c), device_id_type=MESH)
                        cp.start()
                        firsts.append(cp)
                for i, n in enumerate(BIG):
                    for p in range(3):
                        k = p * n_w + i
                        pltpu.make_async_remote_copy(
                            src_ref=in_refs[i].at[l], dst_ref=_slot(outs[l][i], n, chip_idx[p]),
                            send_sem=send1.at[k], recv_sem=recv1.at[k], device_id=(*chips[p], c),
                            device_id_type=MESH).wait_recv()
                for own in own_copies:
                    own.wait()
                for cp in firsts:
                    cp.wait_send()
                seconds = []
                for i in range(n_w):
                    cp = pltpu.make_async_remote_copy(
                        src_ref=outs[l][i], dst_ref=outs[l][i], send_sem=send2.at[i], recv_sem=recv2.at[i],
                        device_id=(x, y, 1 - c), device_id_type=MESH)
                    cp.start()
                    seconds.append(cp)
                for i in range(n_w):
                    pltpu.make_async_remote_copy(
                        src_ref=outs[1 - l][i], dst_ref=outs[1 - l][i], send_sem=send2.at[i], recv_sem=recv2.at[i],
                        device_id=(x, y, 1 - c), device_id_type=MESH).wait_recv()
                for cp in seconds:
                    cp.wait_send()

    any_spec = pl.BlockSpec(memory_space=pl.ANY)
    layer = [jax.ShapeDtypeStruct(_layer_shape(n, shards[n].shape[1:]), BF16) for n in BIG]
    out = pl.pallas_call(
        body, name="gather_weights",
        in_specs=[any_spec] * n_w, out_specs=[any_spec] * (2 * n_w), out_shape=layer + layer,
        scratch_shapes=[pltpu.SemaphoreType.DMA((3 * n_w,)), pltpu.SemaphoreType.DMA((3 * n_w,)),
                        pltpu.SemaphoreType.DMA((n_w,)), pltpu.SemaphoreType.DMA((n_w,)),
                        pltpu.SemaphoreType.DMA((n_w,))],
    )(*ins)
    return [dict(zip(BIG, out[:n_w])), dict(zip(BIG, out[n_w:]))]


def _half_slot(ref, n, chip, h):
    if n in BY_COLUMNS:
        hr, w = ref.shape[0] // 2, ref.shape[1] // 4
        return ref.at[pl.ds(pl.multiple_of(h * hr, 16), hr), pl.ds(pl.multiple_of(chip * w, LANES), w)]
    hr = ref.shape[1] // 2
    return ref.at[chip, pl.ds(pl.multiple_of(h * hr, 16), hr)]


def _gather_layer_body(layer, in_refs, out_refs, sems, handshake):
    send_chip, recv_chip, send_sib, recv_sib, local, send_fwd, recv_fwd = sems
    n_w = len(BIG)
    x, y, c = lax.axis_index("x"), lax.axis_index("y"), lax.axis_index("c")
    me = 2 * x + y
    chips, chip_idx = _chip_ids(x, y)
    sibling = (x, y, 1 - c)
    if handshake:
        barrier = pltpu.get_barrier_semaphore()
        for peer in [sibling] + [(*ch, c) for ch in chips]:
            pl.semaphore_signal(barrier, inc=1, device_id=peer, device_id_type=MESH)
        pl.semaphore_wait(barrier, 4)

    def remote(src, dst, s_sem, r_sem, to):
        return pltpu.make_async_remote_copy(src_ref=src, dst_ref=dst, send_sem=s_sem, recv_sem=r_sem,
                                            device_id=to, device_id_type=MESH)

    sends, own_copies = [], []
    for i, n in enumerate(BIG):
        shard = in_refs[i].at[layer]
        hr = shard.shape[0] // 2
        src = shard.at[pl.ds(pl.multiple_of(c * hr, 16), hr)]
        mine = _half_slot(out_refs[i], n, me, c)
        own = pltpu.make_async_copy(src, mine, local.at[i])
        own.start()
        own_copies.append(own)
        sends.append(remote(src, mine, send_sib.at[i], recv_sib.at[i], sibling))
        sends[-1].start()
        for p in range(3):
            k = p * n_w + i
            sends.append(remote(src, mine, send_chip.at[k], recv_chip.at[k], (*chips[p], c)))
            sends[-1].start()
    for p in range(3):
        for i, n in enumerate(BIG):
            k = p * n_w + i
            landed = _half_slot(out_refs[i], n, chip_idx[p], c)
            remote(landed, landed, send_chip.at[k], recv_chip.at[k], sibling).wait_recv()
            sends.append(remote(landed, landed, send_fwd.at[k], recv_fwd.at[k], sibling))
            sends[-1].start()
    for i, n in enumerate(BIG):
        theirs = _half_slot(out_refs[i], n, me, 1 - c)
        remote(theirs, theirs, send_sib.at[i], recv_sib.at[i], sibling).wait_recv()
        for p in range(3):
            k = p * n_w + i
            theirs = _half_slot(out_refs[i], n, chip_idx[p], 1 - c)
            remote(theirs, theirs, send_fwd.at[k], recv_fwd.at[k], sibling).wait_recv()
    for own in own_copies:
        own.wait()
    for cp in sends:
        cp.wait_send()


def _gather_layer_sems():
    n_w = len(BIG)
    dma = pltpu.SemaphoreType.DMA
    return [dma((3 * n_w,)), dma((3 * n_w,)), dma((n_w,)), dma((n_w,)), dma((n_w,)), dma((3 * n_w,)), dma((3 * n_w,))]


def _gather_layer(shards, layer, name):
    n_w = len(BIG)

    def body(*refs):
        _gather_layer_body(layer, refs[:n_w], refs[n_w:2 * n_w], refs[2 * n_w:], handshake=False)

    any_spec = pl.BlockSpec(memory_space=pl.ANY)
    return list(pl.pallas_call(
        body, name=name, in_specs=[any_spec] * n_w, out_specs=[any_spec] * n_w,
        out_shape=[jax.ShapeDtypeStruct(_layer_shape(n, s.shape[1:]), BF16) for n, s in zip(BIG, shards)],
        scratch_shapes=_gather_layer_sems(),
    )(*shards))


def _gather_layer_sequencer(shards, layer, after, name):
    n_w = len(BIG)

    def body(*refs):
        _gather_layer_body(layer, refs[:n_w], refs[n_w + 1:2 * n_w + 1], refs[2 * n_w + 1:], handshake=True)

    return list(pl.kernel(
        body, name=name,
        out_type=[jax.ShapeDtypeStruct(_layer_shape(n, s.shape[1:]), BF16) for n, s in zip(BIG, shards)],
        mesh=plsc.ScalarSubcoreMesh(axis_name="sequencer", num_cores=1),
        scratch_types=_gather_layer_sems(),
        compiler_params=pltpu.CompilerParams(collective_id=1),
    )(*shards, after))


def _grad_swap_sibling(g):
    n_w = len(BIG)

    def body(*refs):
        in_refs = [refs[:n_w], refs[n_w:2 * n_w]]
        outs = refs[2 * n_w:3 * n_w]
        send, recv = refs[3 * n_w:]
        x, y, c = lax.axis_index("x"), lax.axis_index("y"), lax.axis_index("c")
        for l in range(DEPTH):
            @pl.when(c == l)
            def _(l=l):
                cps = [pltpu.make_async_remote_copy(
                    src_ref=in_refs[1 - l][i], dst_ref=outs[i], send_sem=send.at[i], recv_sem=recv.at[i],
                    device_id=(x, y, 1 - c), device_id_type=MESH) for i in range(n_w)]
                for cp in cps:
                    cp.start()
                for cp in cps:
                    cp.wait()

    any_spec = pl.BlockSpec(memory_space=pl.ANY)
    return list(pl.pallas_call(
        body, name="grad_swap_sibling",
        in_specs=[any_spec] * (2 * n_w), out_specs=[any_spec] * n_w,
        out_shape=[jax.ShapeDtypeStruct(a.shape, a.dtype) for a in g[0]],
        scratch_shapes=[pltpu.SemaphoreType.DMA((n_w,)), pltpu.SemaphoreType.DMA((n_w,))],
    )(*g[0], *g[1]))


def _grad_add_cores(g0, g1, got, ci, name):
    r, c = got.shape
    tb = _pick(r, (256, 352, 128, 64))

    def body(s_ref, a_ref, b_ref, o_ref, out_ref):
        own = jnp.where(s_ref[0] == 0, a_ref[...].astype(F32), b_ref[...].astype(F32))
        out_ref[...] = (own + o_ref[...].astype(F32)).astype(BF16)

    spec = lambda layer: pl.BlockSpec((tb, c), lambda i, s: (jnp.where(s[0] == layer, i, 0), 0))
    return pl.pallas_call(
        body, name=name,
        grid_spec=pltpu.PrefetchScalarGridSpec(
            num_scalar_prefetch=1, grid=(r // tb,),
            in_specs=[spec(0), spec(1), pl.BlockSpec((tb, c), lambda i, s: (i, 0))],
            out_specs=pl.BlockSpec((tb, c), lambda i, s: (i, 0))),
        out_shape=jax.ShapeDtypeStruct((r, c), BF16),
        compiler_params=_cp("arbitrary"),
    )(ci, g0, g1, got)


def _grad_scatter_chips(parts):
    n_w = len(BIG)

    def shard_shape(i, n):
        s = parts[i].shape
        return (s[0], s[1] // 4) if n in BY_COLUMNS else s[1:]

    def body(*refs):
        in_refs, outs = refs[:n_w], refs[n_w:4 * n_w]
        send, recv = refs[4 * n_w:]
        x, y, c = lax.axis_index("x"), lax.axis_index("y"), lax.axis_index("c")
        chips, chip_idx = _chip_ids(x, y)
        cps = []
        for p in range(3):
            for i, n in enumerate(BIG):
                k = p * n_w + i
                cps.append(pltpu.make_async_remote_copy(
                    src_ref=_slot(in_refs[i], n, chip_idx[p]), dst_ref=outs[k], send_sem=send.at[k],
                    recv_sem=recv.at[k], device_id=(*chips[p], c), device_id_type=MESH))
        for cp in cps:
            cp.start()
        for cp in cps:
            cp.wait()

    any_spec = pl.BlockSpec(memory_space=pl.ANY)
    out = pl.pallas_call(
        body, name="grad_scatter_chips",
        in_specs=[any_spec] * n_w, out_specs=[any_spec] * (3 * n_w),
        out_shape=[jax.ShapeDtypeStruct(shard_shape(i, n), BF16) for _ in range(3) for i, n in enumerate(BIG)],
        scratch_shapes=[pltpu.SemaphoreType.DMA((3 * n_w,)), pltpu.SemaphoreType.DMA((3 * n_w,))],
    )(*parts)
    return [out[p * n_w:(p + 1) * n_w] for p in range(3)]


def _grad_add_chips(part, got3, n, me_ci, name):
    rs, cs = got3[0].shape
    tb = _pick(rs, (256, 352, 128, 64))

    def body(s_ref, own_ref, a_ref, b_ref, c_ref, out_ref):
        acc = own_ref[...].astype(F32) + a_ref[...].astype(F32)
        out_ref[...] = (acc + b_ref[...].astype(F32)) + c_ref[...].astype(F32)

    if n in BY_COLUMNS:
        own_spec = pl.BlockSpec((tb, cs), lambda i, s: (i, s[0]))
    else:
        own_spec = pl.BlockSpec((None, tb, cs), lambda i, s: (s[0], i, 0))
    plain = pl.BlockSpec((tb, cs), lambda i, s: (i, 0))
    return pl.pallas_call(
        body, name=name,
        grid_spec=pltpu.PrefetchScalarGridSpec(
            num_scalar_prefetch=1, grid=(rs // tb,),
            in_specs=[own_spec, plain, plain, plain],
            out_specs=pl.BlockSpec((None, tb, cs), lambda i, s: (s[1], i, 0))),
        out_shape=jax.ShapeDtypeStruct((DEPTH, rs, cs), F32),
        compiler_params=_cp("arbitrary"),
    )(me_ci, part, *got3)


def _grad_swap_back(reduced):
    n_w = len(reduced)

    def body(*refs):
        outs = refs[n_w:2 * n_w]
        send, recv = refs[2 * n_w:]
        x, y, c = lax.axis_index("x"), lax.axis_index("y"), lax.axis_index("c")
        for l in range(DEPTH):
            @pl.when(c == l)
            def _(l=l):
                cps = [pltpu.make_async_remote_copy(
                    src_ref=outs[i].at[l], dst_ref=outs[i].at[l], send_sem=send.at[i], recv_sem=recv.at[i],
                    device_id=(x, y, 1 - c), device_id_type=MESH) for i in range(n_w)]
                for cp in cps:
                    cp.start()
                for i, cp in enumerate(cps):
                    cp.wait_send()
                    pltpu.make_async_remote_copy(
                        src_ref=outs[i].at[1 - l], dst_ref=outs[i].at[1 - l], send_sem=send.at[i],
                        recv_sem=recv.at[i], device_id=(x, y, 1 - c), device_id_type=MESH).wait_recv()

    any_spec = pl.BlockSpec(memory_space=pl.ANY)
    return list(pl.pallas_call(
        body, name="grad_swap_back",
        in_specs=[any_spec] * n_w, out_specs=[any_spec] * n_w,
        out_shape=[jax.ShapeDtypeStruct(a.shape, a.dtype) for a in reduced],
        input_output_aliases={i: i for i in range(n_w)},
        scratch_shapes=[pltpu.SemaphoreType.DMA((n_w,)), pltpu.SemaphoreType.DMA((n_w,))],
    )(*reduced))


def _shift_down(v, s, t_iota):
    return jnp.where(t_iota >= s, pltpu.roll(v, s, 0), 0.0)


def _shift_up(v, s, t_iota, t):
    return jnp.where(t_iota < t - s, pltpu.roll(v, t - s, 0), 0.0)


def _ssd_conv_fwd(proj, w, b):
    t = proj.shape[0]
    k_w = SSD_CONV_K

    def body(x_ref, w_ref, b_ref, o_ref):
        x = x_ref[...]
        ti = lax.broadcasted_iota(jnp.int32, x.shape, 0)
        pre = x * w_ref[k_w - 1:k_w, :] + b_ref[...]
        for s in range(1, k_w):
            pre = pre + _shift_down(x, s, ti) * w_ref[k_w - 1 - s:k_w - s, :]
        o_ref[...] = pre * _sigmoid(pre)

    off = P_XBC // LANES
    return pl.pallas_call(
        body, name="ssd_conv_fwd", grid=(SSD_CONV_DIM // LANES,),
        in_specs=[pl.BlockSpec((t, LANES), lambda j: (0, off + j)),
                  pl.BlockSpec((k_w, LANES), lambda j: (0, j)),
                  pl.BlockSpec((1, LANES), lambda j: (0, j))],
        out_specs=pl.BlockSpec((t, LANES), lambda j: (0, j)),
        out_shape=jax.ShapeDtypeStruct((t, SSD_CONV_DIM), F32),
        compiler_params=_cp("parallel"),
    )(proj, w, b)


def _ssd_conv_bwd(dxbc, proj, w, b):
    t = proj.shape[0]
    k_w = SSD_CONV_K

    def body(d_ref, x_ref, w_ref, b_ref, dx_ref, dw_ref, db_ref):
        x = x_ref[...]
        ti = lax.broadcasted_iota(jnp.int32, x.shape, 0)
        shifted = [x] + [_shift_down(x, s, ti) for s in range(1, k_w)]
        pre = b_ref[...] + shifted[0] * w_ref[k_w - 1:k_w, :]
        for s in range(1, k_w):
            pre = pre + shifted[s] * w_ref[k_w - 1 - s:k_w - s, :]
        sg = _sigmoid(pre)
        dpre = d_ref[...] * (sg * (1.0 + pre * (1.0 - sg)))
        db_ref[...] = jnp.sum(dpre, axis=0, keepdims=True)
        dx = dpre * w_ref[k_w - 1:k_w, :]
        for s in range(k_w):
            dw_ref[k_w - 1 - s:k_w - s, :] = jnp.sum(dpre * shifted[s], axis=0, keepdims=True)
            if s:
                dx = dx + _shift_up(dpre, s, ti, t) * w_ref[k_w - 1 - s:k_w - s, :]
        dx_ref[...] = dx

    off = P_XBC // LANES
    return pl.pallas_call(
        body, name="ssd_conv_bwd", grid=(SSD_CONV_DIM // LANES,),
        in_specs=[pl.BlockSpec((t, LANES), lambda j: (0, j)),
                  pl.BlockSpec((t, LANES), lambda j: (0, off + j)),
                  pl.BlockSpec((k_w, LANES), lambda j: (0, j)),
                  pl.BlockSpec((1, LANES), lambda j: (0, j))],
        out_specs=[pl.BlockSpec((t, LANES), lambda j: (0, j)),
                   pl.BlockSpec((k_w, LANES), lambda j: (0, j)),
                   pl.BlockSpec((1, LANES), lambda j: (0, j))],
        out_shape=[jax.ShapeDtypeStruct((t, SSD_CONV_DIM), F32),
                   jax.ShapeDtypeStruct((k_w, SSD_CONV_DIM), F32),
                   jax.ShapeDtypeStruct((1, SSD_CONV_DIM), F32)],
        compiler_params=_cp("parallel"),
    )(dxbc, proj, w, b)


def _shortconv_fwd(proj, w):
    t = proj.shape[0]
    nb = SC_W // LANES
    off = P_SC // LANES

    def body(b_ref, c_ref, x_ref, w_ref, o_ref):
        u = c_ref[...] * x_ref[...]
        ti = lax.broadcasted_iota(jnp.int32, u.shape, 0)
        cv = u * w_ref[SC_K - 1:SC_K, :]
        for s in range(1, SC_K):
            cv = cv + _shift_down(u, s, ti) * w_ref[SC_K - 1 - s:SC_K - s, :]
        o_ref[...] = (b_ref[...] * cv).astype(BF16)

    return pl.pallas_call(
        body, name="shortconv_fwd", grid=(nb,),
        in_specs=[pl.BlockSpec((t, LANES), lambda j: (0, off + j)),
                  pl.BlockSpec((t, LANES), lambda j: (0, off + nb + j)),
                  pl.BlockSpec((t, LANES), lambda j: (0, off + 2 * nb + j)),
                  pl.BlockSpec((SC_K, LANES), lambda j: (0, j))],
        out_specs=pl.BlockSpec((t, LANES), lambda j: (0, j)),
        out_shape=jax.ShapeDtypeStruct((t, SC_W), BF16),
        compiler_params=_cp("parallel"),
    )(proj, proj, proj, w)


def _shortconv_bwd(dy, dy_off, proj, w):
    t = proj.shape[0]
    nb = SC_W // LANES
    off = P_SC // LANES
    doff = dy_off // LANES

    def body(d_ref, b_ref, c_ref, x_ref, w_ref, db_ref, dc_ref, dx_ref, dw_ref):
        cg, xin = c_ref[...], x_ref[...]
        u = cg * xin
        ti = lax.broadcasted_iota(jnp.int32, u.shape, 0)
        shifted = [u] + [_shift_down(u, s, ti) for s in range(1, SC_K)]
        cv = shifted[0] * w_ref[SC_K - 1:SC_K, :]
        for s in range(1, SC_K):
            cv = cv + shifted[s] * w_ref[SC_K - 1 - s:SC_K - s, :]
        d = d_ref[...]
        db_ref[...] = (d * cv).astype(BF16)
        dcv = d * b_ref[...]
        du = dcv * w_ref[SC_K - 1:SC_K, :]
        for s in range(SC_K):
            dw_ref[SC_K - 1 - s:SC_K - s, :] = jnp.sum(dcv * shifted[s], axis=0, keepdims=True)
            if s:
                du = du + _shift_up(dcv, s, ti, t) * w_ref[SC_K - 1 - s:SC_K - s, :]
        dc_ref[...] = (du * xin).astype(BF16)
        dx_ref[...] = (du * cg).astype(BF16)

    tile = pl.BlockSpec((t, LANES), lambda j: (0, j))
    outs = pl.pallas_call(
        body, name="shortconv_bwd", grid=(nb,),
        in_specs=[pl.BlockSpec((t, LANES), lambda j: (0, doff + j)),
                  pl.BlockSpec((t, LANES), lambda j: (0, off + j)),
                  pl.BlockSpec((t, LANES), lambda j: (0, off + nb + j)),
                  pl.BlockSpec((t, LANES), lambda j: (0, off + 2 * nb + j)),
                  pl.BlockSpec((SC_K, LANES), lambda j: (0, j))],
        out_specs=[tile, tile, tile, pl.BlockSpec((SC_K, LANES), lambda j: (0, j))],
        out_shape=[jax.ShapeDtypeStruct((t, SC_W), BF16)] * 3 + [jax.ShapeDtypeStruct((SC_K, SC_W), F32)],
        compiler_params=_cp("parallel"),
    )(dy, proj, proj, proj, w)
    return outs


def _cum_logf(proj, fbias_row):
    t = proj.shape[0]
    blk = CUM_BLOCK

    def body(p_ref, b_ref, o_ref, carry_ref):
        i = pl.program_id(0)

        @pl.when(i == 0)
        def _():
            carry_ref[...] = jnp.zeros_like(carry_ref)

        lf = -_softplus(-(p_ref[...] + b_ref[...]))
        r = lax.broadcasted_iota(jnp.int32, (blk, blk), 0)
        c = lax.broadcasted_iota(jnp.int32, (blk, blk), 1)
        tri = (r >= c).astype(BF16)
        o_ref[...] = _tri_left(tri, lf) + carry_ref[...]
        carry_ref[...] = o_ref[blk - 1:blk, :]

    return pl.pallas_call(
        body, name="cum_logf", grid=(t // blk,),
        in_specs=[pl.BlockSpec((blk, LANES), lambda i: (i, P_PAD // LANES)),
                  pl.BlockSpec((1, LANES), lambda i: (0, 0))],
        out_specs=pl.BlockSpec((blk, LANES), lambda i: (i, 0)),
        out_shape=jax.ShapeDtypeStruct((t, LANES), F32),
        scratch_shapes=[pltpu.VMEM((1, LANES), F32)],
        compiler_params=_cp("arbitrary"),
    )(proj, fbias_row)


def _pad_block_bwd(dcf, ddt, proj, fbias_row):
    t = proj.shape[0]
    blk = CUM_BLOCK
    nb = t // blk

    def body(dcf_ref, ddt_ref, p_ref, b_ref, o_ref, db_ref, carry_ref):
        i = pl.program_id(0)

        @pl.when(i == 0)
        def _():
            carry_ref[...] = jnp.zeros_like(carry_ref)
            db_ref[...] = jnp.zeros_like(db_ref)

        r = lax.broadcasted_iota(jnp.int32, (blk, blk), 0)
        c = lax.broadcasted_iota(jnp.int32, (blk, blk), 1)
        tri = (r <= c).astype(BF16)
        rev = _tri_left(tri, dcf_ref[...]) + carry_ref[...]
        carry_ref[...] = jnp.sum(dcf_ref[...], axis=0, keepdims=True) + carry_ref[...]
        lane = lax.broadcasted_iota(jnp.int32, (blk, LANES), 1)
        is_f = (lane >= PAD_F0) & (lane < PAD_F0 + FOX_HEADS)
        df = jnp.where(is_f, rev * _sigmoid(-(p_ref[...] + b_ref[...])), 0.0)
        db_ref[...] += jnp.sum(df, axis=0, keepdims=True)
        o_ref[...] = jnp.where(lane < PAD_DT0 + SSD_HEADS, ddt_ref[...], df).astype(BF16)

    return pl.pallas_call(
        body, name="pad_block_bwd", grid=(nb,),
        in_specs=[pl.BlockSpec((blk, LANES), lambda i: (nb - 1 - i, 0)),
                  pl.BlockSpec((blk, LANES), lambda i: (nb - 1 - i, 0)),
                  pl.BlockSpec((blk, LANES), lambda i: (nb - 1 - i, P_PAD // LANES)),
                  pl.BlockSpec((1, LANES), lambda i: (0, 0))],
        out_specs=[pl.BlockSpec((blk, LANES), lambda i: (nb - 1 - i, 0)),
                   pl.BlockSpec((1, LANES), lambda i: (0, 0))],
        out_shape=[jax.ShapeDtypeStruct((t, LANES), BF16), jax.ShapeDtypeStruct((1, LANES), F32)],
        scratch_shapes=[pltpu.VMEM((1, LANES), F32)],
        compiler_params=_cp("arbitrary"),
    )(dcf, ddt, proj, fbias_row)


def _att_scores(q, k, cq, ck, diagonal, blk):
    s = _dot_nt(q, k) * (FOX_HD ** -0.5) + cq - ck
    if diagonal:
        r = lax.broadcasted_iota(jnp.int32, (blk, blk), 0)
        c = lax.broadcasted_iota(jnp.int32, (blk, blk), 1)
        s = jnp.where(r >= c, s, NEG)
    return s


def _fox_fwd(q, k, v, cf_col, cf_row):
    h, t, hd = q.shape
    blk = min(ATT_BLOCK, t)
    nb = t // blk

    def body(q_ref, k_ref, v_ref, cq_ref, ck_ref, o_ref, lse_ref):
        qi = pl.program_id(1)
        qv, cq = q_ref[...], cq_ref[...]

        def step(j, carry, diagonal):
            m, l, acc = carry
            off = pl.multiple_of(j * blk, blk)
            s = _att_scores(qv, k_ref[pl.ds(off, blk), :], cq, ck_ref[:, pl.ds(off, blk)], diagonal, blk)
            m_new = jnp.maximum(m, jnp.max(s, axis=1, keepdims=True))
            alpha = jnp.exp(m - m_new)
            p = jnp.exp(s - m_new)
            l = alpha * l + jnp.sum(p, axis=1, keepdims=True)
            acc = alpha * acc + _dot(p.astype(BF16), v_ref[pl.ds(off, blk), :])
            return m_new, l, acc

        init = (jnp.full((blk, 1), NEG, F32), jnp.zeros((blk, 1), F32), jnp.zeros((blk, hd), F32))
        carry = lax.fori_loop(0, qi, lambda j, cr: step(j, cr, False), init)
        m, l, acc = step(qi, carry, True)
        o_ref[...] = acc / l
        lse_ref[...] = m + jnp.log(l)

    qmap = lambda hh, i: (hh, i, 0)
    whole = lambda hh, i: (hh, 0, 0)
    return pl.pallas_call(
        body, name="fox_fwd", grid=(h, nb),
        in_specs=[pl.BlockSpec((None, blk, hd), qmap), pl.BlockSpec((None, t, hd), whole),
                  pl.BlockSpec((None, t, hd), whole), pl.BlockSpec((None, blk, 1), qmap),
                  pl.BlockSpec((None, 1, t), whole)],
        out_specs=[pl.BlockSpec((None, blk, hd), qmap), pl.BlockSpec((None, blk, 1), qmap)],
        out_shape=[jax.ShapeDtypeStruct((h, t, hd), F32), jax.ShapeDtypeStruct((h, t, 1), F32)],
        compiler_params=_cp("parallel", "arbitrary"),
    )(q, k, v, cf_col, cf_row)


def _fox_bwd(q, k, v, cf_col, cf_row, o, lse, do):
    h, t, hd = q.shape
    blk = min(ATT_BLOCK, t)
    nb = t // blk
    scale = FOX_HD ** -0.5

    def body(q_ref, k_ref, v_ref, cq_ref, ck_ref, o_ref, lse_ref, do_ref,
             dq_ref, dk_ref, dv_ref, dcq_ref, dck_ref, delta_s):
        kj = pl.program_id(1)

        @pl.when(kj == 0)
        def _():
            dq_ref[...] = jnp.zeros_like(dq_ref)
            dcq_ref[...] = jnp.zeros_like(dcq_ref)

            def fill(i, _):
                rows = pl.ds(pl.multiple_of(i * blk, blk), blk)
                delta_s[rows, :] = jnp.sum(do_ref[rows, :] * o_ref[rows, :], axis=1, keepdims=True)
                return 0

            lax.fori_loop(0, nb, fill, 0)

        kb, vb, ck = k_ref[...], v_ref[...], ck_ref[...]

        def step(i, carry, diagonal):
            dk, dv, dck = carry
            rows = pl.ds(pl.multiple_of(i * blk, blk), blk)
            qb = q_ref[rows, :]
            do_b = do_ref[rows, :].astype(BF16)
            s = _att_scores(qb, kb, cq_ref[rows, :], ck, diagonal, blk)
            p = jnp.exp(s - lse_ref[rows, :])
            dv = dv + _dot_tn(p.astype(BF16), do_b)
            ds = p * (_dot_nt(do_b, vb) - delta_s[rows, :])
            ds_b = ds.astype(BF16)
            dk = dk + _dot_tn(ds_b, qb)
            dq_ref[rows, :] += _dot(ds_b, kb) * scale
            dcq_ref[rows, :] += jnp.sum(ds, axis=1, keepdims=True)
            dck = dck - jnp.sum(ds, axis=0, keepdims=True)
            return dk, dv, dck

        init = (jnp.zeros((blk, hd), F32), jnp.zeros((blk, hd), F32), jnp.zeros((1, blk), F32))
        carry = step(kj, init, True)
        dk, dv, dck = lax.fori_loop(kj + 1, nb, lambda i, cr: step(i, cr, False), carry)
        dk_ref[...] = dk * scale
        dv_ref[...] = dv
        dck_ref[...] = dck

    kmap = lambda hh, j: (hh, j, 0)
    whole = lambda hh, j: (hh, 0, 0)
    return pl.pallas_call(
        body, name="fox_bwd", grid=(h, nb),
        in_specs=[pl.BlockSpec((None, t, hd), whole), pl.BlockSpec((None, blk, hd), kmap),
                  pl.BlockSpec((None, blk, hd), kmap), pl.BlockSpec((None, t, 1), whole),
                  pl.BlockSpec((None, 1, blk), lambda hh, j: (hh, 0, j)),
                  pl.BlockSpec((None, t, hd), whole), pl.BlockSpec((None, t, 1), whole),
                  pl.BlockSpec((None, t, hd), whole)],
        out_specs=[pl.BlockSpec((None, t, hd), whole), pl.BlockSpec((None, blk, hd), kmap),
                   pl.BlockSpec((None, blk, hd), kmap), pl.BlockSpec((None, t, 1), whole),
                   pl.BlockSpec((None, 1, blk), lambda hh, j: (hh, 0, j))],
        out_shape=[jax.ShapeDtypeStruct((h, t, hd), F32), jax.ShapeDtypeStruct((h, t, hd), F32),
                   jax.ShapeDtypeStruct((h, t, hd), F32), jax.ShapeDtypeStruct((h, t, 1), F32),
                   jax.ShapeDtypeStruct((h, 1, t), F32)],
        scratch_shapes=[pltpu.VMEM((t, 1), F32)],
        compiler_params=_cp("parallel", "arbitrary"),
    )(q, k, v, cf_col, cf_row, o, lse, do)


def _lane_col(v, h):
    lane = lax.broadcasted_iota(jnp.int32, v.shape, 1)
    return jnp.sum(jnp.where(lane == h, v, 0.0), axis=1, keepdims=True)


def _sub_row(v, h):
    sub = lax.broadcasted_iota(jnp.int32, v.shape, 0)
    return jnp.sum(jnp.where(sub == h, v, 0.0), axis=0, keepdims=True)


def _ssd_decays(pad, pad_t, dtb_row, alog_row, dtb_col, alog_col, blk):
    r = lax.broadcasted_iota(jnp.int32, (blk, blk), 0)
    c = lax.broadcasted_iota(jnp.int32, (blk, blk), 1)
    tril = r >= c
    dt_c = _softplus(pad + dtb_row)
    acs_c = _tri_left(tril.astype(BF16), dt_c * (-jnp.exp(alog_row)))
    dt_r = _softplus(pad_t + dtb_col)
    acs_r = _tri_right(dt_r * (-jnp.exp(alog_col)), (r <= c).astype(BF16))
    rows = lax.broadcasted_iota(jnp.int32, acs_c.shape, 0)
    acs_last = jnp.sum(jnp.where(rows == blk - 1, acs_c, 0.0), axis=0, keepdims=True)
    return dt_c, acs_c, acs_r, acs_last, tril


def _pair_terms(pair, dt_c, acs_c, acs_r, acs_last, d_row, blk):
    lane = lax.broadcasted_iota(jnp.int32, (blk, LANES), 1)
    lo = lane < 64
    lo_row = lax.broadcasted_iota(jnp.int32, (1, LANES), 1) < 64
    h0, h1 = 2 * pair, 2 * pair + 1
    col = [_lane_col(acs_c, h0), _lane_col(acs_c, h1)]
    row = [_sub_row(acs_r, h0), _sub_row(acs_r, h1)]
    last = [_lane_col(acs_last, h0), _lane_col(acs_last, h1)]
    dt_p = jnp.where(lo, _lane_col(dt_c, h0), _lane_col(dt_c, h1))
    e_p = jnp.where(lo, jnp.exp(col[0]), jnp.exp(col[1]))
    w_p = jnp.where(lo, jnp.exp(last[0] - col[0]), jnp.exp(last[1] - col[1]))
    decay_p = jnp.where(lo_row, jnp.exp(last[0]), jnp.exp(last[1]))
    d_p = jnp.where(lo_row, _lane_col(d_row, h0), _lane_col(d_row, h1))
    return lo, lo_row, col, row, last, dt_p, e_p, w_p, decay_p, d_p


def _ssd_specs(t, blk, rev):
    nc = t // blk
    ix = (lambda i: nc - 1 - i) if rev else (lambda i: i)
    xbc = pl.BlockSpec((blk, SSD_CONV_DIM), lambda i: (ix(i), 0))
    pad = pl.BlockSpec((blk, LANES), lambda i: (ix(i), P_PAD // LANES))
    pad_t = pl.BlockSpec((LANES, blk), lambda i: (0, ix(i)))
    z = pl.BlockSpec((blk, SSD_W), lambda i: (ix(i), 0))
    row = pl.BlockSpec((1, LANES), lambda i: (0, 0))
    colv = pl.BlockSpec((LANES, 1), lambda i: (0, 0))
    ng = pl.BlockSpec((1, SSD_W), lambda i: (0, 0))
    y = pl.BlockSpec((blk, SSD_W), lambda i: (ix(i), 0))
    st = pl.BlockSpec((None, 4, LANES, LANES), lambda i: (ix(i), 0, 0, 0))
    return nc, xbc, pad, pad_t, z, row, colv, ng, y, st


def _ssd_fwd(xbc, proj, pad_t, dtb_row, alog_row, d_row, dtb_col, alog_col, ng):
    t = xbc.shape[0]
    blk = min(SSD_CHUNK, t)
    nc, s_xbc, s_pad, s_padt, s_z, s_row, s_col, s_ng, s_y, s_st = _ssd_specs(t, blk, False)

    def body(xbc_ref, pad_ref, padt_ref, z_ref, dtb_ref, alog_ref, d_ref, dtbc_ref, alogc_ref, ng_ref,
             out_ref, ypre_ref, st_ref, state):
        @pl.when(pl.program_id(0) == 0)
        def _():
            state[...] = jnp.zeros_like(state)

        dt_c, acs_c, acs_r, acs_last, tril = _ssd_decays(
            pad_ref[...], padt_ref[...], dtb_ref[...], alog_ref[...], dtbc_ref[...], alogc_ref[...], blk)
        ys = []
        g_mat = {}
        for pair in range(4):
            g = pair // 2
            bg = xbc_ref[:, SSD_W + LANES * g:SSD_W + LANES * (g + 1)].astype(BF16)
            cg = xbc_ref[:, SSD_W + 2 * LANES + LANES * g:SSD_W + 2 * LANES + LANES * (g + 1)].astype(BF16)
            if g not in g_mat:
                g_mat[g] = _dot_nt(cg, bg)
            xs_p = xbc_ref[:, LANES * pair:LANES * (pair + 1)]
            lo, _, col, row, _, dt_p, e_p, w_p, decay_p, d_p = _pair_terms(
                pair, dt_c, acs_c, acs_r, acs_last, d_ref[...], blk)
            x_p = xs_p * dt_p
            y = None
            for hh in range(2):
                lm = jnp.exp(jnp.where(tril, col[hh] - row[hh], NEG))
                m_h = (g_mat[g] * lm).astype(BF16)
                x_h = jnp.where(lo if hh == 0 else ~lo, x_p, 0.0).astype(BF16)
                y_h = _dot(m_h, x_h)
                y = y_h if y is None else y + y_h
            st_in = state[pair]
            st_ref[pair] = st_in
            y = y + e_p * _dot(cg, st_in.astype(BF16))
            state[pair] = decay_p * st_in + _dot_tn(bg, (x_p * w_p).astype(BF16))
            ys.append(y + d_p * xs_p)
        y_all = jnp.concatenate(ys, axis=1)
        ypre_ref[...] = y_all
        z = z_ref[...]
        y2 = y_all * (z * _sigmoid(z))
        outs = []
        for g in range(2):
            seg = y2[:, 256 * g:256 * (g + 1)]
            rr = lax.rsqrt(jnp.mean(seg * seg, axis=-1, keepdims=True) + RMS_EPS)
            outs.append(seg * rr * ng_ref[:, 256 * g:256 * (g + 1)])
        out_ref[...] = jnp.concatenate(outs, axis=1).astype(BF16)

    return pl.pallas_call(
        body, name="ssd_scan_fwd", grid=(nc,),
        in_specs=[s_xbc, s_pad, s_padt, s_z, s_row, s_row, s_row, s_col, s_col, s_ng],
        out_specs=[s_y, s_y, s_st],
        out_shape=[jax.ShapeDtypeStruct((t, SSD_W), BF16), jax.ShapeDtypeStruct((t, SSD_W), F32),
                   jax.ShapeDtypeStruct((nc, 4, LANES, LANES), F32)],
        scratch_shapes=[pltpu.VMEM((4, LANES, LANES), F32)],
        compiler_params=_cp("arbitrary"),
    )(xbc, proj, pad_t, proj, dtb_row, alog_row, d_row, dtb_col, alog_col, ng)


def _ssd_bwd(dout, dout_off, xbc, proj, pad_t, ypre, states, dtb_row, alog_row, d_row, dtb_col, alog_col, ng):
    t = xbc.shape[0]
    blk = min(SSD_CHUNK, t)
    nc, s_xbc, s_pad, s_padt, s_z, s_row, s_col, s_ng, s_y, s_st = _ssd_specs(t, blk, True)
    s_dout = pl.BlockSpec((blk, SSD_W), lambda i: (nc - 1 - i, dout_off // SSD_W))

    def body(dout_ref, xbc_ref, pad_ref, padt_ref, z_ref, ypre_ref, st_ref, dtb_ref, alog_ref, d_ref,
             dtbc_ref, alogc_ref, ng_ref, dxbc_ref, ddt_ref, dz_ref, acc_ref, dng_ref, dstate):
        @pl.when(pl.program_id(0) == 0)
        def _():
            dstate[...] = jnp.zeros_like(dstate)
            acc_ref[...] = jnp.zeros_like(acc_ref)
            dng_ref[...] = jnp.zeros_like(dng_ref)

        pad = pad_ref[...]
        dt_c, acs_c, acs_r, acs_last, tril = _ssd_decays(
            pad, padt_ref[...], dtb_ref[...], alog_ref[...], dtbc_ref[...], alogc_ref[...], blk)
        a_row = -jnp.exp(alog_ref[...])

        z = z_ref[...]
        sz = _sigmoid(z)
        silu_z = z * sz
        y_pre = ypre_ref[...]
        y2 = y_pre * silu_z
        dy2 = []
        for g in range(2):
            sl = slice(256 * g, 256 * (g + 1))
            seg = y2[:, sl]
            rr = lax.rsqrt(jnp.mean(seg * seg, axis=-1, keepdims=True) + RMS_EPS)
            nrm = seg * rr
            d_seg = dout_ref[:, sl]
            dng_ref[:, sl] += jnp.sum(d_seg * nrm, axis=0, keepdims=True)
            dn = d_seg * ng_ref[:, sl]
            dy2.append(rr * (dn - nrm * jnp.mean(dn * nrm, axis=-1, keepdims=True)))
        dy2 = jnp.concatenate(dy2, axis=1)
        dz_ref[...] = (dy2 * y_pre * (sz * (1.0 + z * (1.0 - sz)))).astype(BF16)
        dy_all = dy2 * silu_z

        lane_row = lax.broadcasted_iota(jnp.int32, (1, LANES), 1)
        lane_blk = lax.broadcasted_iota(jnp.int32, (blk, LANES), 1)
        row_col = lax.broadcasted_iota(jnp.int32, (blk, 1), 0)
        ddt = jnp.zeros((blk, LANES), F32)
        dacs = jnp.zeros((blk, LANES), F32)
        dd_row = jnp.zeros((1, LANES), F32)
        ones_b = jnp.ones((blk, LANES), BF16)
        dxs = []
        d_b = [None, None]
        d_c = [None, None]
        d_g = [None, None]
        bgs, cgs = {}, {}
        g_mat = {}
        for pair in range(4):
            g = pair // 2
            if g not in g_mat:
                bgs[g] = xbc_ref[:, SSD_W + LANES * g:SSD_W + LANES * (g + 1)].astype(BF16)
                cgs[g] = xbc_ref[:, SSD_W + 2 * LANES + LANES * g:SSD_W + 2 * LANES + LANES * (g + 1)].astype(BF16)
                g_mat[g] = _dot_nt(cgs[g], bgs[g])
            bg, cg = bgs[g], cgs[g]
            xs_p = xbc_ref[:, LANES * pair:LANES * (pair + 1)]
            lo, lo_row, col, row, last, dt_p, e_p, w_p, decay_p, d_p = _pair_terms(
                pair, dt_c, acs_c, acs_r, acs_last, d_ref[...], blk)
            x_p = xs_p * dt_p
            dy_p = dy_all[:, LANES * pair:LANES * (pair + 1)]
            st_in = st_ref[pair]
            dst = dstate[pair]
            dx_diag = None
            for hh in range(2):
                sel = lo if hh == 0 else ~lo
                lm = jnp.exp(jnp.where(tril, col[hh] - row[hh], NEG))
                m_f = g_mat[g] * lm
                m_h = m_f.astype(BF16)
                x_h = jnp.where(sel, x_p, 0.0).astype(BF16)
                dy_h = jnp.where(sel, dy_p, 0.0).astype(BF16)
                dxd = _dot_tn(m_h, dy_h)
                dm = _dot_nt(dy_h, x_h)
                dg_h = dm * lm
                p_b = (dm * m_f).astype(BF16)
                dacs = dacs + jnp.where(lane_blk == 2 * pair + hh, _dot(p_b, ones_b) - _dot_tn(p_b, ones_b), 0.0)
                dx_diag = dxd if dx_diag is None else dx_diag + dxd
                d_g[g] = dg_h if d_g[g] is None else d_g[g] + dg_h
            st_b = st_in.astype(BF16)
            dst_b = dst.astype(BF16)
            y_off = e_p * _dot(cg, st_b)
            edy = (e_p * dy_p).astype(BF16)
            dc_off = _dot_nt(edy, st_b)
            d_c[g] = dc_off if d_c[g] is None else d_c[g] + dc_off
            dstate[pair] = decay_p * dst + _dot_tn(cg, edy)
            dx_state = _dot(bg, dst_b) * w_p
            db_st = _dot_nt((x_p * w_p).astype(BF16), dst_b)
            d_b[g] = db_st if d_b[g] is None else d_b[g] + db_st
            dx = dx_diag + dx_state
            dxs.append(dx * dt_p + d_p * dy_p)
            prod_dt = dx * xs_p
            prod_acs = dy_p * y_off - x_p * dx_state
            prod_st = x_p * dx_state
            prod_d = dy_p * xs_p
            st_prod = jnp.sum(dst * st_in, axis=0, keepdims=True)
            for hh in range(2):
                h = 2 * pair + hh
                sel = lo if hh == 0 else ~lo
                sel_row = lo_row if hh == 0 else ~lo_row
                ddt_h = jnp.sum(jnp.where(sel, prod_dt, 0.0), axis=1, keepdims=True)
                dacs_h = jnp.sum(jnp.where(sel, prod_acs, 0.0), axis=1, keepdims=True)
                tail = jnp.sum(jnp.sum(jnp.where(sel, prod_st, 0.0), axis=1, keepdims=True), axis=0, keepdims=True)
                tail = tail + jnp.exp(last[hh]) * jnp.sum(jnp.where(sel_row, st_prod, 0.0), axis=1, keepdims=True)
                dacs_h = dacs_h + jnp.where(row_col == blk - 1, tail, 0.0)
                dd_h = jnp.sum(jnp.sum(jnp.where(sel, prod_d, 0.0), axis=1, keepdims=True), axis=0, keepdims=True)
                ddt = ddt + jnp.where(lane_blk == h, ddt_h, 0.0)
                dacs = dacs + jnp.where(lane_blk == h, dacs_h, 0.0)
                dd_row = dd_row + jnp.where(lane_row == h, dd_h, 0.0)
        for g in range(2):
            dg_b = d_g[g].astype(BF16)
            d_c[g] = d_c[g] + _dot(dg_b, bgs[g])
            d_b[g] = d_b[g] + _dot_tn(dg_b, cgs[g])
        r = lax.broadcasted_iota(jnp.int32, (blk, blk), 0)
        c = lax.broadcasted_iota(jnp.int32, (blk, blk), 1)
        da = _tri_left((r <= c).astype(BF16), dacs)
        ddt = ddt + da * a_row
        d_raw = ddt * _sigmoid(pad + dtb_ref[...])
        ddt_ref[...] = d_raw
        acc_ref[0:1, :] += jnp.sum(da * dt_c, axis=0, keepdims=True) * a_row
        acc_ref[1:2, :] += dd_row
        acc_ref[2:3, :] += jnp.sum(d_raw, axis=0, keepdims=True)
        dxbc_ref[...] = jnp.concatenate(dxs + d_b + d_c, axis=1)

    return pl.pallas_call(
        body, name="ssd_scan_bwd", grid=(nc,),
        in_specs=[s_dout, s_xbc, s_pad, s_padt, s_z, s_y, s_st, s_row, s_row, s_row, s_col, s_col, s_ng],
        out_specs=[s_xbc, pl.BlockSpec((blk, LANES), lambda i: (nc - 1 - i, 0)), s_y,
                   pl.BlockSpec((8, LANES), lambda i: (0, 0)), s_ng],
        out_shape=[jax.ShapeDtypeStruct((t, SSD_CONV_DIM), F32), jax.ShapeDtypeStruct((t, LANES), F32),
                   jax.ShapeDtypeStruct((t, SSD_W), BF16), jax.ShapeDtypeStruct((8, LANES), F32),
                   jax.ShapeDtypeStruct((1, SSD_W), F32)],
        scratch_shapes=[pltpu.VMEM((4, LANES, LANES), F32)],
        compiler_params=_cp("arbitrary"),
    )(dout, xbc, proj, pad_t, proj, ypre, states, dtb_row, alog_row, d_row, dtb_col, alog_col, ng)


def _pad_lanes(v, off):
    return jnp.zeros((1, LANES), F32).at[0, off:off + v.shape[0]].set(v)


def _perm_mix_w_in(w):
    z, xbc, dt = w[:, 0:512], w[:, 512:1536], w[:, 1536:1544]
    qkv, f, sc = w[:, 1544:2312], w[:, 2312:2316], w[:, 2316:3084]
    padblk = jnp.zeros((w.shape[0], LANES), w.dtype).at[:, PAD_DT0:PAD_DT0 + 8].set(dt).at[:, PAD_F0:PAD_F0 + 4].set(f)
    return jnp.concatenate([z, xbc, qkv, sc, padblk], axis=1)


def _unperm_mix_w_in(wp):
    z, xbc, qkv, sc = wp[:, 0:512], wp[:, 512:1536], wp[:, 1536:2304], wp[:, 2304:3072]
    dt, f = wp[:, P_PAD + PAD_DT0:P_PAD + PAD_DT0 + 8], wp[:, P_PAD + PAD_F0:P_PAD + PAD_F0 + 4]
    return jnp.concatenate([z, xbc, dt, qkv, f, sc], axis=1)


def _heads(m):
    return jnp.transpose(m.reshape(m.shape[0], FOX_HEADS, FOX_HD), (1, 0, 2))


def _unheads(m):
    return jnp.transpose(m, (1, 0, 2)).reshape(m.shape[1], FOX_W)


def _ffn_fwd(h, w_in, w_out, tag):
    u, a = _ffn_in_swiglu(h, w_in, f"ffn_in_{tag}")
    y = _matmul(a, w_out, "nn", F32, f"ffn_out_{tag}")
    return y, (h, u, a)


def _ffn_bwd(dy, saved, w_in, w_out, tag):
    h, u, a = saved
    du = _ffn_out_dx_swiglu(dy, w_out, u, f"ffn_out_dx_{tag}")
    dw_out = _matmul(a, dy, "tn", BF16, f"ffn_out_dw_{tag}")
    dh = _matmul(du, w_in, "nt", F32, f"ffn_in_dx_{tag}", halves="a")
    dw_in = _matmul(h, du, "tn", BF16, f"ffn_in_dw_{tag}", halves="b")
    return dh, dw_in, dw_out


def _mix_fwd(h, wp, w_out, sp, tag):
    proj = _matmul(h, wp, "nn", F32, f"mix_in_{tag}")
    pad_t = jnp.transpose(proj[:, P_PAD:P_PAD + LANES])
    xbc = _ssd_conv_fwd(proj, sp["conv_w"], sp["conv_b"])
    y_ssd, ypre, states = _ssd_fwd(xbc, proj, pad_t, sp["dtb_row"], sp["alog_row"], sp["d_row"],
                                   sp["dtb_col"], sp["alog_col"], sp["ng"])
    cf = _cum_logf(proj, sp["fbias_row"])
    cf4 = jnp.transpose(cf[:, PAD_F0:PAD_F0 + FOX_HEADS])
    cf_col, cf_row = cf4[:, :, None], cf4[:, None, :]
    q = _heads(proj[:, P_QKV:P_QKV + 256].astype(BF16))
    k = _heads(proj[:, P_QKV + 256:P_QKV + 512].astype(BF16))
    v = _heads(proj[:, P_QKV + 512:P_QKV + 768].astype(BF16))
    o, lse = _fox_fwd(q, k, v, cf_col, cf_row)
    y_sc = _shortconv_fwd(proj, sp["sconv_w"])
    ymix = jnp.concatenate([y_ssd, _unheads(o).astype(BF16), y_sc], axis=1)
    y = _matmul(ymix, w_out, "nn", F32, f"mix_out_{tag}")
    return y, (h, proj, pad_t, xbc, ypre, states, q, k, v, cf_col, cf_row, o, lse, ymix)


def _mix_bwd(dy, saved, wp, w_out, sp, tag):
    h, proj, pad_t, xbc, ypre, states, q, k, v, cf_col, cf_row, o, lse, ymix = saved
    dymix = _matmul(dy, w_out, "nt", F32, f"mix_out_dx_{tag}")
    dw_out = _matmul(ymix, dy, "tn", BF16, f"mix_out_dw_{tag}")
    dxbc, ddt, dz, ssd_acc, dng = _ssd_bwd(dymix, 0, xbc, proj, pad_t, ypre, states, sp["dtb_row"],
                                           sp["alog_row"], sp["d_row"], sp["dtb_col"], sp["alog_col"], sp["ng"])
    dxbc_raw, dconv_w, dconv_b = _ssd_conv_bwd(dxbc, proj, sp["conv_w"], sp["conv_b"])
    do = _heads(dymix[:, SSD_W:SSD_W + FOX_W])
    dq, dk, dv, dcq, dck = _fox_bwd(q, k, v, cf_col, cf_row, o, lse, do)
    dcf4 = dcq[:, :, 0] + dck[:, 0, :]
    dcf = jnp.zeros((h.shape[0], LANES), F32).at[:, PAD_F0:PAD_F0 + FOX_HEADS].set(jnp.transpose(dcf4))
    dpad, dfb = _pad_block_bwd(dcf, ddt, proj, sp["fbias_row"])
    dscb, dscc, dscx, dsconv_w = _shortconv_bwd(dymix, SSD_W + FOX_W, proj, sp["sconv_w"])
    dproj = jnp.concatenate([dz, dxbc_raw.astype(BF16), _unheads(dq).astype(BF16), _unheads(dk).astype(BF16),
                             _unheads(dv).astype(BF16), dscb, dscc, dscx, dpad], axis=1)
    dh = _matmul(dproj, wp, "nt", F32, f"mix_in_dx_{tag}")
    dwp = _matmul(h, dproj, "tn", BF16, f"mix_in_dw_{tag}")
    small = dict(conv_w=dconv_w, conv_b=dconv_b[0], dt_bias=ssd_acc[2, 0:8], a_log=ssd_acc[0, 0:8],
                 d=ssd_acc[1, 0:8], norm_g=dng[0], f_bias=dfb[0, PAD_F0:PAD_F0 + FOX_HEADS], sconv_w=dsconv_w)
    return dh, _unperm_mix_w_in(dwp), dw_out, small


def _local_step(x, tgt, mod, wfull, small_p):
    row = lambda v: v.reshape(1, -1)
    subs = [(l, j) for l in range(DEPTH) for j in range(N_SUB)]
    factor = (0.5, 1.0, 0.5)
    w_names = (("ffn1_w_in", "ffn1_w_out"), ("mix_w_in", "mix_w_out"), ("ffn2_w_in", "ffn2_w_out"))
    lg = [[row(small_p["ln_g"][l, j]) for j in range(N_SUB)] for l in range(DEPTH)]
    lb = [[row(small_p["ln_b"][l, j]) for j in range(N_SUB)] for l in range(DEPTH)]
    sps, wps = [], []
    for l in range(DEPTH):
        w = wfull[l]
        sp = dict(
            conv_w=small_p["ssd_conv_w"][l], conv_b=row(small_p["ssd_conv_b"][l]),
            dtb_row=_pad_lanes(small_p["ssd_dt_bias"][l], PAD_DT0), alog_row=_pad_lanes(small_p["ssd_a_log"][l], 0),
            d_row=_pad_lanes(small_p["ssd_d"][l], 0), ng=row(small_p["ssd_norm_g"][l]),
            fbias_row=_pad_lanes(small_p["fox_f_bias"][l], PAD_F0), sconv_w=small_p["sconv_w"][l])
        sp["dtb_col"] = jnp.transpose(sp["dtb_row"])
        sp["alog_col"] = jnp.transpose(sp["alog_row"])
        sps.append(sp)
        wps.append(_perm_mix_w_in(w["mix_w_in"]))
    tags = [f"l{l}{('f1', 'mx', 'f2')[j]}" for l, j in subs]

    x0, h = _ln_in_fwd(x, row(small_p["ln_in_g"]), row(small_p["ln_in_b"]), mod[0, 0])
    cur = x0
    xins, ys, inner = [], [], []
    for idx, (l, j) in enumerate(subs):
        w = wfull[l]
        if j == 1:
            y, sv = _mix_fwd(h, wps[l], w["mix_w_out"], sps[l], tags[idx])
        else:
            y, sv = _ffn_fwd(h, w[w_names[j][0]], w[w_names[j][1]], tags[idx])
        nxt = mod[subs[idx + 1]] if idx + 1 < len(subs) else None
        xins.append(cur)
        ys.append(y)
        inner.append(sv)
        cur, h = _res_ln(cur, y, mod[l, j], lg[l][j], lb[l][j], factor[j], f"res_ln_{tags[idx]}", nxt)

    dcur, loss_acc = _loss_head(cur, tgt)
    last = len(subs) - 1
    l, j = subs[last]
    dres, dy, acc = _res_ln_bwd(dcur, xins[last], ys[last], mod[l, j], lg[l][j], factor[j], f"res_ln_bwd_{tags[last]}")
    ln_acc = {last: acc}
    shift_scale = {}
    big_grads = [dict() for _ in range(DEPTH)]
    small_g = [None] * DEPTH
    for idx in reversed(range(len(subs))):
        l, j = subs[idx]
        w = wfull[l]
        if j == 1:
            dh, g_in, g_out, small_g[l] = _mix_bwd(dy, inner[idx], wps[l], w["mix_w_out"], sps[l], tags[idx])
        else:
            dh, g_in, g_out = _ffn_bwd(dy, inner[idx], w[w_names[j][0]], w[w_names[j][1]], tags[idx])
        big_grads[l][w_names[j][0]], big_grads[l][w_names[j][1]] = g_in, g_out
        if idx > 0:
            pl_, pj = subs[idx - 1]
            dres, dy, acc5 = _mod_res_bwd(dres, dh, mod[l, j], xins[idx - 1], ys[idx - 1], mod[pl_, pj], lg[pl_][pj],
                                          lb[pl_][pj], factor[pj], f"mod_res_bwd_{tags[idx]}")
            shift_scale[idx], ln_acc[idx - 1] = acc5[0:2], acc5[2:5]
        else:
            dx0, shift_scale[0] = _modulate_bwd(dres, dh, x0, mod[0, 0], "mod_bwd_first")
    dx, acc_in = _ln_in_bwd(dx0, x, row(small_p["ln_in_g"]))
    dmod = []
    for l in range(DEPTH):
        ids = [N_SUB * l + j for j in range(N_SUB)]
        small_g[l]["ln_g"] = jnp.stack([ln_acc[i][0] for i in ids])
        small_g[l]["ln_b"] = jnp.stack([ln_acc[i][1] for i in ids])
        dmod.append(jnp.stack([jnp.concatenate([shift_scale[i], ln_acc[i][2:3]], axis=0) for i in ids]))
    return loss_acc[0, 0], dx, big_grads, small_g, jnp.stack(dmod), acc_in


SMALL_ORDER = ("ssd_conv_w", "ssd_conv_b", "ssd_dt_bias", "ssd_a_log", "ssd_d", "ssd_norm_g", "fox_f_bias",
               "sconv_w", "ln_g", "ln_b")
SMALL_KEY = dict(ssd_conv_w="conv_w", ssd_conv_b="conv_b", ssd_dt_bias="dt_bias", ssd_a_log="a_log", ssd_d="d",
                 ssd_norm_g="norm_g", fox_f_bias="f_bias", sconv_w="sconv_w", ln_g="ln_g", ln_b="ln_b")
COL_SHARDED_SMALL = ("ssd_conv_w", "sconv_w", "ln_g", "ln_b")


def _pad_to(v, n):
    return jnp.concatenate([v, jnp.zeros((n - v.shape[0],), v.dtype)])


def kernel(x, c, ln_in_g, ln_in_b, ada_w, ada_b, ffn1_w_in, ffn1_w_out, mix_w_in, mix_w_out, ssd_conv_w, ssd_conv_b, ssd_dt_bias, ssd_a_log, ssd_d, ssd_norm_g, fox_f_bias, sconv_w, ffn2_w_in, ffn2_w_out, ln_g, ln_b, loss_target, m_ln_in_g, m_ln_in_b, m_ada_w, m_ada_b, m_ffn1_w_in, m_ffn1_w_out, m_mix_w_in, m_mix_w_out, m_ssd_conv_w, m_ssd_conv_b, m_ssd_dt_bias, m_ssd_a_log, m_ssd_d, m_ssd_norm_g, m_fox_f_bias, m_sconv_w, m_ffn2_w_in, m_ffn2_w_out, m_ln_g, m_ln_b, v_ln_in_g, v_ln_in_b, v_ada_w, v_ada_b, v_ffn1_w_in, v_ffn1_w_out, v_mix_w_in, v_mix_w_out, v_ssd_conv_w, v_ssd_conv_b, v_ssd_dt_bias, v_ssd_a_log, v_ssd_d, v_ssd_norm_g, v_fox_f_bias, v_sconv_w, v_ffn2_w_in, v_ffn2_w_out, v_ln_g, v_ln_b):
    names = ("ln_in_g", "ln_in_b", "ada_w", "ada_b", "ffn1_w_in", "ffn1_w_out", "mix_w_in", "mix_w_out",
             "ssd_conv_w", "ssd_conv_b", "ssd_dt_bias", "ssd_a_log", "ssd_d", "ssd_norm_g", "fox_f_bias", "sconv_w",
             "ffn2_w_in", "ffn2_w_out", "ln_g", "ln_b")
    w_loc = dict(zip(names, (ln_in_g, ln_in_b, ada_w, ada_b, ffn1_w_in, ffn1_w_out, mix_w_in, mix_w_out, ssd_conv_w,
                             ssd_conv_b, ssd_dt_bias, ssd_a_log, ssd_d, ssd_norm_g, fox_f_bias, sconv_w, ffn2_w_in,
                             ffn2_w_out, ln_g, ln_b)))
    m_loc = dict(zip(names, (m_ln_in_g, m_ln_in_b, m_ada_w, m_ada_b, m_ffn1_w_in, m_ffn1_w_out, m_mix_w_in,
                             m_mix_w_out, m_ssd_conv_w, m_ssd_conv_b, m_ssd_dt_bias, m_ssd_a_log, m_ssd_d,
                             m_ssd_norm_g, m_fox_f_bias, m_sconv_w, m_ffn2_w_in, m_ffn2_w_out, m_ln_g, m_ln_b)))
    v_loc = dict(zip(names, (v_ln_in_g, v_ln_in_b, v_ada_w, v_ada_b, v_ffn1_w_in, v_ffn1_w_out, v_mix_w_in,
                             v_mix_w_out, v_ssd_conv_w, v_ssd_conv_b, v_ssd_dt_bias, v_ssd_a_log, v_ssd_d,
                             v_ssd_norm_g, v_fox_f_bias, v_sconv_w, v_ffn2_w_in, v_ffn2_w_out, v_ln_g, v_ln_b)))

    xi, yi, ci = lax.axis_index("x"), lax.axis_index("y"), lax.axis_index("c")
    me = 2 * xi + yi
    dev = 2 * me + ci

    def gather8(v, tag):
        v2 = v.reshape(1, -1)
        got = _bcast_chips([v2], f"gather_chips_{tag}")
        same_c = jnp.concatenate(_by_chip(me, v2, [g[0] for g in got]), axis=0)
        other_c = _swap_sibling([same_c], f"gather_sibling_{tag}")[0]
        pair = lax.switch(ci, [lambda a, b: jnp.stack([a, b], axis=1), lambda a, b: jnp.stack([b, a], axis=1)],
                          same_c, other_c)
        return pair.reshape(8, -1)

    def chip_concat(own, got3, axis):
        return jnp.concatenate(_by_chip(me, own, got3), axis=axis)

    small_cols = [w_loc[n].reshape(-1, w_loc[n].shape[-1]) for n in COL_SHARDED_SMALL]
    got = _bcast_chips(small_cols, "gather_small_params")
    small_p = {n: w_loc[n] for n in ("ln_in_g", "ln_in_b", "ssd_conv_b", "ssd_dt_bias", "ssd_a_log", "ssd_d",
                                     "ssd_norm_g", "fox_f_bias")}
    for i, n in enumerate(COL_SHARDED_SMALL):
        full = chip_concat(small_cols[i], [g[i] for g in got], 1)
        small_p[n] = full.reshape(w_loc[n].shape[:-1] + (full.shape[-1],))

    shards = [w_loc[n].astype(BF16) for n in BIG]
    layer0 = _gather_layer(shards, 0, "gather_weights_l0")
    layer1 = _gather_layer_sequencer(shards, 1, layer0[BIG.index("mix_w_out")], "gather_weights_l1")
    gathered = [dict(zip(BIG, layer0)), dict(zip(BIG, layer1))]
    wfull = []
    for l in range(DEPTH):
        g = gathered[l]
        wl = {n: g[n] if n in BY_COLUMNS else g[n].reshape(-1, g[n].shape[-1]) for n in BIG if n != "mix_w_in"}
        wl["mix_w_in"] = jnp.concatenate([g["mix_w_in"][s] for s in range(4)], axis=1)
        wfull.append(wl)

    c_all = gather8(c[0], "c")
    c_act = _silu_bf16(c_all)
    ada_w_b = ada_w.astype(BF16)
    mod_loc = [_matmul(c_act, ada_w_b[l], "nn", F32, f"ada_fwd_l{l}") for l in range(DEPTH)]
    mod_loc = jnp.stack(mod_loc)
    got = _bcast_chips([mod_loc], "gather_mod")
    mod_all = chip_concat(mod_loc, [g[0] for g in got], 2)
    mod = lax.dynamic_index_in_dim(mod_all, dev, 1, keepdims=False) + ada_b
    mod = mod.reshape(DEPTH, N_SUB, 3, D)

    loss_part, dx, big_g, small_g, dmod, acc_in = _local_step(x[0], loss_target[0], mod, wfull, small_p)
    loss = lax.psum(loss_part, ("x", "y", "c"))

    pieces = [dmod.reshape(-1), acc_in[0], acc_in[1]]
    for n in SMALL_ORDER:
        pieces.append(jnp.stack([small_g[l][SMALL_KEY[n]] for l in range(DEPTH)]).reshape(-1))
    sizes = [p.shape[0] for p in pieces]
    total = sum(sizes)
    padded = -(-total // 1024) * 1024
    vec = _pad_to(jnp.concatenate(pieces), padded)
    all_rows = gather8(vec, "small_grads")
    summed = _sum_rows(all_rows, "sum_small_grads")[0]
    offs = [0]
    for s in sizes:
        offs.append(offs[-1] + s)
    n_mod = sizes[0]
    grads = {"ada_b": summed[0:n_mod].reshape(DEPTH, 3 * N_SUB * D),
             "ln_in_g": summed[offs[1]:offs[2]], "ln_in_b": summed[offs[2]:offs[3]]}
    for i, n in enumerate(SMALL_ORDER):
        full = summed[offs[3 + i]:offs[4 + i]].reshape(small_p[n].shape)
        if n in COL_SHARDED_SMALL:
            wcols = w_loc[n].shape[-1]
            full = lax.dynamic_slice_in_dim(full, me * wcols, wcols, axis=full.ndim - 1)
        grads[n] = full

    dmod_all = all_rows[:, 0:n_mod].reshape(8, DEPTH, 3 * N_SUB * D)
    ncol = ada_w.shape[-1]
    dmod_cols = lax.dynamic_slice_in_dim(dmod_all, me * ncol, ncol, axis=2).astype(BF16)
    grads["ada_w"] = jnp.stack([_matmul(c_act, dmod_cols[:, l], "tn", F32, f"ada_bwd_l{l}") for l in range(DEPTH)])

    def wire_form(n, g):
        if n in BY_COLUMNS:
            return g
        if n == "mix_w_in":
            return jnp.transpose(g.reshape(g.shape[0], 4, g.shape[1] // 4), (1, 0, 2))
        return g.reshape(4, g.shape[0] // 4, g.shape[1])

    g_wire = [[wire_form(n, big_g[l][n]) for n in BIG] for l in range(DEPTH)]
    from_sib = _grad_swap_sibling(g_wire)
    ci_arr = jnp.reshape(ci, (1,)).astype(jnp.int32)
    chip_part = []
    for i, n in enumerate(BIG):
        shape = from_sib[i].shape
        flat = lambda a: a.reshape(-1, a.shape[-1])
        part = _grad_add_cores(flat(g_wire[0][i]), flat(g_wire[1][i]), flat(from_sib[i]), ci_arr,
                               f"grad_add_cores_{n}")
        chip_part.append(part.reshape(shape))
    got = _grad_scatter_chips(chip_part)
    me_ci = jnp.stack([me, ci]).astype(jnp.int32)
    reduced = [_grad_add_chips(chip_part[i], [g[i] for g in got], n, me_ci, f"grad_add_chips_{n}")
               for i, n in enumerate(BIG)]
    reduced = _grad_swap_back(reduced)
    for i, n in enumerate(BIG):
        grads[n] = reduced[i]

    delta, new_m, new_v = {}, {}, {}
    for n in ("ada_w",) + BIG:
        delta[n], new_m[n], new_v[n] = _adamw(w_loc[n], grads[n], m_loc[n], v_loc[n], f"adamw_{n}")
    small_names = [n for n in names if n not in ("ada_w",) + BIG]
    flat = lambda d: jnp.concatenate([d[n].reshape(-1) for n in small_names])
    n_small = sum(w_loc[n].size for n in small_names)
    n_pad = -(-n_small // 1024) * 1024
    packed = [_pad_to(flat(d), n_pad).reshape(-1, LANES) for d in (w_loc, grads, m_loc, v_loc)]
    d_s, m_s, v_s = _adamw(*packed, "adamw_small")
    off = 0
    for n in small_names:
        sz = w_loc[n].size
        delta[n] = d_s.reshape(-1)[off:off + sz].reshape(w_loc[n].shape)
        new_m[n] = m_s.reshape(-1)[off:off + sz].reshape(w_loc[n].shape)
        new_v[n] = v_s.reshape(-1)[off:off + sz].reshape(w_loc[n].shape)
        off += sz

    return (loss, dx[None], *[grads[n] for n in names], *[delta[n] for n in names],
            *[new_m[n] for n in names], *[new_v[n] for n in names])
```

```python
import functools

import jax
import jax.numpy as jnp
from jax import lax
from jax.experimental import pallas as pl
from jax.experimental.pallas import tpu as pltpu
from jax.experimental.pallas import tpu_sc as plsc

F32 = jnp.float32
BF16 = jnp.bfloat16
MESH = pl.DeviceIdType.MESH

D = 1024
DEPTH = 2
N_SUB = 3
D_FF = 2816
FF_TILE = D_FF // 2
ALPHA = (2 * DEPTH) ** 0.25
LN_EPS = 1e-5
RMS_EPS = 1e-5
SSD_W = 512
SSD_HEADS = 8
SSD_CONV_K = 4
SSD_CONV_DIM = 1024
FOX_W = 256
FOX_HEADS = 4
FOX_HD = 64
SC_W = 256
SC_K = 3
D_IN_PROJ = 3084
P_Z, P_XBC, P_QKV, P_SC, P_PAD = 0, 512, 1536, 2304, 3072
D_PROJ_PAD = 3200
PAD_DT0, PAD_F0 = 0, 8
SSD_CHUNK = 256
ATT_BLOCK = 512
CUM_BLOCK = 256
LANES = 128
VMEM_LIMIT = 56 * 1024 * 1024

ADAM_LR, ADAM_B1, ADAM_B2, ADAM_EPS, ADAM_WD, ADAM_STEP = 0.001, 0.9, 0.999, 1e-08, 0.01, 10
NEG = -1e30


def _cp(*sem):
    return pltpu.CompilerParams(dimension_semantics=sem, vmem_limit_bytes=VMEM_LIMIT)


def _pick(n, cands):
    for c in cands:
        if n % c == 0:
            return c
    return n


def _dot(a, b):
    return lax.dot_general(a, b, (((1,), (0,)), ((), ())), preferred_element_type=F32)


def _dot_nt(a, b):
    return lax.dot_general(a, b, (((1,), (1,)), ((), ())), preferred_element_type=F32)


def _dot_tn(a, b):
    return lax.dot_general(a, b, (((0,), (0,)), ((), ())), preferred_element_type=F32)


def _sigmoid(x):
    return 1.0 / (1.0 + jnp.exp(-x))


def _softplus(x):
    return jnp.maximum(x, 0.0) + jnp.log(1.0 + jnp.exp(-jnp.abs(x)))


def _split3(v):
    h = v.astype(BF16)
    r = v - h.astype(F32)
    m = r.astype(BF16)
    l = (r - m.astype(F32)).astype(BF16)
    return h, m, l


def _tri_left(tri, v):
    h, m, l = _split3(v)
    return _dot(tri, h) + _dot(tri, m) + _dot(tri, l)


def _tri_right(v, tri):
    h, m, l = _split3(v)
    return _dot(h, tri) + _dot(m, tri) + _dot(l, tri)


def _matmul(a, b, mode, out_dtype, name, halves=None):
    assert a.dtype == BF16 and b.dtype == BF16, (name, a.dtype, b.dtype)
    if halves == "a":
        assert mode == "nt" and a.shape[0] == 2 and a.shape[2] == D_FF
        m, k, n = a.shape[1], 2 * D_FF, b.shape[0]
    elif halves == "b":
        assert mode == "tn" and b.shape[0] == 2 and b.shape[2] == D_FF
        (k, m), n = a.shape, 2 * D_FF
    elif mode == "nn":
        (m, k), n = a.shape, b.shape[1]
    elif mode == "nt":
        (m, k), n = a.shape, b.shape[0]
    else:
        (k, m), n = a.shape, b.shape[1]
    tm = m if m <= 1024 else _pick(m, (1024, 1408, 512, 256, 128))
    tn = n if n <= 1024 else _pick(n, (1408, 640, 512, 256, 128))
    tk = k if k <= 1024 else _pick(k, (1408, 1024, 640, 512, 256, 128))
    nk = k // tk
    if mode == "nn":
        dn = (((1,), (0,)), ((), ()))
        a_spec = pl.BlockSpec((tm, tk), lambda i, j, kk: (i, kk))
        b_spec = pl.BlockSpec((tk, tn), lambda i, j, kk: (kk, j))
    elif mode == "nt":
        dn = (((1,), (1,)), ((), ()))
        a_spec = pl.BlockSpec((tm, tk), lambda i, j, kk: (i, kk))
        b_spec = pl.BlockSpec((tn, tk), lambda i, j, kk: (j, kk))
    else:
        dn = (((0,), (0,)), ((), ()))
        a_spec = pl.BlockSpec((tk, tm), lambda i, j, kk: (kk, i))
        b_spec = pl.BlockSpec((tk, tn), lambda i, j, kk: (kk, j))
    per_half = D_FF // FF_TILE
    if halves == "a":
        assert tk == FF_TILE
        a_spec = pl.BlockSpec((None, tm, tk), lambda i, j, kk: (kk // per_half, i, kk % per_half))
    elif halves == "b":
        assert tn == FF_TILE
        b_spec = pl.BlockSpec((None, tk, tn), lambda i, j, kk: (j // per_half, kk, j % per_half))

    def body(a_ref, b_ref, o_ref, *acc):
        d = lax.dot_general(a_ref[...], b_ref[...], dn, preferred_element_type=F32)
        if nk == 1:
            o_ref[...] = d.astype(o_ref.dtype)
            return
        acc_ref, = acc
        kk = pl.program_id(2)

        @pl.when(kk == 0)
        def _():
            acc_ref[...] = d

        @pl.when((kk > 0) & (kk < nk - 1))
        def _():
            acc_ref[...] += d

        @pl.when(kk == nk - 1)
        def _():
            o_ref[...] = (acc_ref[...] + d).astype(o_ref.dtype)

    return pl.pallas_call(
        body, name=name, grid=(m // tm, n // tn, nk),
        in_specs=[a_spec, b_spec],
        out_specs=pl.BlockSpec((tm, tn), lambda i, j, kk: (i, j)),
        out_shape=jax.ShapeDtypeStruct((m, n), out_dtype),
        scratch_shapes=[pltpu.VMEM((tm, tn), F32)] if nk > 1 else [],
        compiler_params=_cp("parallel", "parallel", "arbitrary"),
    )(a, b)


def _rows(body, name, t, tb, row_in, full_in, row_out, acc_out):
    in_specs, args = [], []
    for r in row_in:
        if isinstance(r, tuple):
            arr, w, j = r
            in_specs.append(pl.BlockSpec((tb, w), functools.partial(lambda i, jj: (i, jj), jj=j)))
            args.append(arr)
        else:
            in_specs.append(pl.BlockSpec((tb, r.shape[1]), lambda i: (i, 0)))
            args.append(r)
    for f in full_in:
        in_specs.append(pl.BlockSpec(f.shape, functools.partial(lambda i, nd: (0,) * nd, nd=f.ndim)))
        args.append(f)
    out_specs = [pl.BlockSpec((tb, c), lambda i: (i, 0)) for c, _ in row_out]
    out_specs += [pl.BlockSpec(s, functools.partial(lambda i, nd: (0,) * nd, nd=len(s))) for s in acc_out]
    out_shape = [jax.ShapeDtypeStruct((t, c), dt) for c, dt in row_out]
    out_shape += [jax.ShapeDtypeStruct(s, F32) for s in acc_out]
    return pl.pallas_call(
        body, name=name, grid=(t // tb,), in_specs=in_specs, out_specs=out_specs, out_shape=out_shape,
        compiler_params=_cp("arbitrary"),
    )(*args)


def _ln_stats(r):
    mu = jnp.mean(r, axis=-1, keepdims=True)
    xc = r - mu
    var = jnp.mean(xc * xc, axis=-1, keepdims=True)
    rstd = lax.rsqrt(var + LN_EPS)
    return xc * rstd, rstd


def _ln_bwd(dout, xhat, rstd, g):
    dxh = dout * g
    m1 = jnp.mean(dxh, axis=-1, keepdims=True)
    m2 = jnp.mean(dxh * xhat, axis=-1, keepdims=True)
    return rstd * (dxh - m1 - xhat * m2)


def _first(i, acc_refs):
    @pl.when(i == 0)
    def _():
        for a in acc_refs:
            a[...] = jnp.zeros_like(a)


def _modulated(xv, m_ref):
    return (xv * (1.0 + m_ref[1:2, :]) + m_ref[0:1, :]).astype(BF16)


def _ln_in_fwd(x, g, b, next_mod3):
    t = x.shape[0]

    def body(x_ref, g_ref, b_ref, m_ref, o_ref, h_ref):
        xhat, _ = _ln_stats(x_ref[...])
        out = xhat * g_ref[...] + b_ref[...]
        o_ref[...] = out
        h_ref[...] = _modulated(out, m_ref)

    return _rows(body, "ln_in_fwd", t, 256, [x], [g, b, next_mod3], [(D, F32), (D, BF16)], [])


def _ln_in_bwd(dx0, x, g):
    t = x.shape[0]

    def body(d_ref, x_ref, g_ref, o_ref, acc_ref):
        _first(pl.program_id(0), [acc_ref])
        xhat, rstd = _ln_stats(x_ref[...])
        d = d_ref[...]
        o_ref[...] = _ln_bwd(d, xhat, rstd, g_ref[...])
        acc_ref[0:1, :] += jnp.sum(d * xhat, axis=0, keepdims=True)
        acc_ref[1:2, :] += jnp.sum(d, axis=0, keepdims=True)

    return _rows(body, "ln_in_bwd", t, 256, [dx0, x], [g], [(D, F32)], [(2, D)])


def _modulate_bwd(dxres, dh, xin, mod3, name):
    t = xin.shape[0]

    def body(r_ref, dh_ref, x_ref, m_ref, o_ref, acc_ref):
        _first(pl.program_id(0), [acc_ref])
        dh_v = dh_ref[...]
        o_ref[...] = r_ref[...] + dh_v * (1.0 + m_ref[1:2, :])
        acc_ref[0:1, :] += jnp.sum(dh_v, axis=0, keepdims=True)
        acc_ref[1:2, :] += jnp.sum(dh_v * x_ref[...], axis=0, keepdims=True)

    return _rows(body, name, t, 256, [dxres, dh, xin], [mod3], [(D, F32)], [(2, D)])


def _ffn_in_swiglu(h, w_in, name):
    t = h.shape[0]
    tm = _pick(t, (512, 256, 128))
    nj = D_FF // FF_TILE

    def body(h_ref, wg_ref, wu_ref, u_ref, a_ref):
        hv = h_ref[...]
        gate = _dot(hv, wg_ref[...])
        up = _dot(hv, wu_ref[...])
        u_ref[0] = gate.astype(BF16)
        u_ref[1] = up.astype(BF16)
        a_ref[...] = (gate * _sigmoid(gate) * up).astype(BF16)

    return pl.pallas_call(
        body, name=name, grid=(nj, t // tm),
        in_specs=[pl.BlockSpec((tm, D), lambda j, i: (i, 0)),
                  pl.BlockSpec((D, FF_TILE), lambda j, i: (0, j)),
                  pl.BlockSpec((D, FF_TILE), lambda j, i: (0, nj + j))],
        out_specs=[pl.BlockSpec((2, tm, FF_TILE), lambda j, i: (0, i, j)),
                   pl.BlockSpec((tm, FF_TILE), lambda j, i: (i, j))],
        out_shape=[jax.ShapeDtypeStruct((2, t, D_FF), BF16), jax.ShapeDtypeStruct((t, D_FF), BF16)],
        compiler_params=_cp("parallel", "parallel"),
    )(h, w_in, w_in)


def _ffn_out_dx_swiglu(dy, w_out, u, name):
    t = dy.shape[0]
    tm = _pick(t, (512, 256, 128))
    nj = D_FF // FF_TILE

    def body(dy_ref, w_ref, u_ref, du_ref):
        da = _dot_nt(dy_ref[...], w_ref[...])
        gate = u_ref[0].astype(F32)
        up = u_ref[1].astype(F32)
        sg = _sigmoid(gate)
        du_ref[0] = (da * up * (sg * (1.0 + gate * (1.0 - sg)))).astype(BF16)
        du_ref[1] = (da * gate * sg).astype(BF16)

    blk3 = pl.BlockSpec((2, tm, FF_TILE), lambda j, i: (0, i, j))
    return pl.pallas_call(
        body, name=name, grid=(nj, t // tm),
        in_specs=[pl.BlockSpec((tm, D), lambda j, i: (i, 0)),
                  pl.BlockSpec((FF_TILE, D), lambda j, i: (j, 0)), blk3],
        out_specs=blk3,
        out_shape=jax.ShapeDtypeStruct((2, t, D_FF), BF16),
        compiler_params=_cp("parallel", "parallel"),
    )(dy, w_out, u)


def _res_ln(xin, y, mod3, lg, lb, factor, name, next_mod3=None):
    t = xin.shape[0]

    def body(x_ref, y_ref, m_ref, g_ref, b_ref, *rest):
        r = ALPHA * x_ref[...] + (factor * m_ref[2:3, :]) * y_ref[...]
        xhat, _ = _ln_stats(r)
        out = xhat * g_ref[...] + b_ref[...]
        if next_mod3 is None:
            rest[0][...] = out
        else:
            rest[1][...] = out
            rest[2][...] = _modulated(out, rest[0])

    if next_mod3 is None:
        return _rows(body, name, t, 256, [xin, y], [mod3, lg, lb], [(D, F32)], [])[0], None
    return _rows(body, name, t, 256, [xin, y], [mod3, lg, lb, next_mod3], [(D, F32), (D, BF16)], [])


def _mod_res_bwd(dxres, dh, mod3, xin_p, y_p, mod3_p, lg_p, lb_p, factor_p, name):
    t = dxres.shape[0]

    def body(r_ref, dh_ref, xp_ref, yp_ref, m_ref, mp_ref, g_ref, b_ref, dres_ref, dy_ref, acc_ref):
        _first(pl.program_id(0), [acc_ref])
        gate = factor_p * mp_ref[2:3, :]
        yv = yp_ref[...]
        xhat, rstd = _ln_stats(ALPHA * xp_ref[...] + gate * yv)
        xin = xhat * g_ref[...] + b_ref[...]
        dh_v = dh_ref[...]
        d = r_ref[...] + dh_v * (1.0 + m_ref[1:2, :])
        dr = _ln_bwd(d, xhat, rstd, g_ref[...])
        dres_ref[...] = ALPHA * dr
        dy_ref[...] = (gate * dr).astype(BF16)
        acc_ref[0:1, :] += jnp.sum(dh_v, axis=0, keepdims=True)
        acc_ref[1:2, :] += jnp.sum(dh_v * xin, axis=0, keepdims=True)
        acc_ref[2:3, :] += jnp.sum(d * xhat, axis=0, keepdims=True)
        acc_ref[3:4, :] += jnp.sum(d, axis=0, keepdims=True)
        acc_ref[4:5, :] += jnp.sum(factor_p * yv * dr, axis=0, keepdims=True)

    return _rows(body, name, t, 256, [dxres, dh, xin_p, y_p], [mod3, mod3_p, lg_p, lb_p],
                 [(D, F32), (D, BF16)], [(5, D)])


def _res_ln_bwd(dout, xin, y, mod3, lg, factor, name):
    t = xin.shape[0]

    def body(d_ref, x_ref, y_ref, m_ref, g_ref, dres_ref, dy_ref, acc_ref):
        _first(pl.program_id(0), [acc_ref])
        gate = factor * m_ref[2:3, :]
        yv = y_ref[...]
        r = ALPHA * x_ref[...] + gate * yv
        xhat, rstd = _ln_stats(r)
        d = d_ref[...]
        dr = _ln_bwd(d, xhat, rstd, g_ref[...])
        dres_ref[...] = ALPHA * dr
        dy_ref[...] = (gate * dr).astype(BF16)
        acc_ref[0:1, :] += jnp.sum(d * xhat, axis=0, keepdims=True)
        acc_ref[1:2, :] += jnp.sum(d, axis=0, keepdims=True)
        acc_ref[2:3, :] += jnp.sum(factor * yv * dr, axis=0, keepdims=True)

    return _rows(body, name, t, 256, [dout, xin, y], [mod3, lg], [(D, F32), (D, BF16)], [(3, D)])


def _loss_head(xf, tgt):
    t = xf.shape[0]

    def body(x_ref, t_ref, d_ref, acc_ref):
        _first(pl.program_id(0), [acc_ref])
        e = x_ref[...] - t_ref[...]
        d_ref[...] = e * (1.0 / D)
        part = 0.5 * jnp.sum(jnp.mean(e * e, axis=-1, keepdims=True), axis=0, keepdims=True)
        acc_ref[...] += jnp.broadcast_to(part, acc_ref.shape)

    return _rows(body, "loss_head", t, 256, [xf, tgt], [], [(D, F32)], [(1, LANES)])


def _silu_bf16(c_all):
    def body(c_ref, o_ref):
        v = c_ref[...]
        o_ref[...] = (v * _sigmoid(v)).astype(BF16)

    return _rows(body, "silu_c", c_all.shape[0], c_all.shape[0], [c_all], [], [(c_all.shape[1], BF16)], [])[0]


def _sum_rows(v, name):
    r, n = v.shape
    tn = _pick(n, (8192, 4096, 2048, 1024, 512, 256, 128))

    def body(v_ref, o_ref):
        acc = v_ref[0:1, :]
        for k in range(1, r):
            acc = acc + v_ref[k:k + 1, :]
        o_ref[...] = acc

    return pl.pallas_call(
        body, name=name, grid=(n // tn,),
        in_specs=[pl.BlockSpec((r, tn), lambda j: (0, j))],
        out_specs=pl.BlockSpec((1, tn), lambda j: (0, j)),
        out_shape=jax.ShapeDtypeStruct((1, n), F32),
        compiler_params=_cp("parallel"),
    )(v)


def _elementwise(fn, name, ins, out_dtypes):
    r, c = ins[0].shape
    tb = _pick(r, (128, 64, 32, 16, 8))
    n_in = len(ins)

    def body(*refs):
        outs = fn(*[x[...] for x in refs[:n_in]])
        for o_ref, o in zip(refs[n_in:], outs):
            o_ref[...] = o.astype(o_ref.dtype)

    spec = pl.BlockSpec((tb, c), lambda i: (i, 0))
    return pl.pallas_call(
        body, name=name, grid=(r // tb,), in_specs=[spec] * n_in, out_specs=[spec] * len(out_dtypes),
        out_shape=[jax.ShapeDtypeStruct((r, c), dt) for dt in out_dtypes],
        compiler_params=_cp("parallel"),
    )(*ins)


def _adamw_math(w, g, m, v):
    m = ADAM_B1 * m + (1.0 - ADAM_B1) * g
    v = ADAM_B2 * v + (1.0 - ADAM_B2) * (g * g)
    m_hat = m / (1.0 - ADAM_B1 ** ADAM_STEP)
    v_hat = v / (1.0 - ADAM_B2 ** ADAM_STEP)
    delta = -ADAM_LR * (m_hat / (jnp.sqrt(v_hat) + ADAM_EPS) + ADAM_WD * w)
    return delta, m, v


def _adamw(w, g, m, v, name):
    shape = w.shape
    c = shape[-1]
    w2, g2, m2, v2 = (a.reshape(-1, c) for a in (w, g, m, v))
    outs = _elementwise(_adamw_math, name, [w2, g2, m2, v2], [F32, F32, F32])
    return tuple(o.reshape(shape) for o in outs)


def _remote_exchange(ins, plan, peers_of, name):
    n_in, n_out = len(ins), len(plan)

    def body(*refs):
        in_refs, out_refs = refs[:n_in], refs[n_in:n_in + n_out]
        send_sems, recv_sems = refs[n_in + n_out], refs[n_in + n_out + 1]
        peers = peers_of(lax.axis_index("x"), lax.axis_index("y"), lax.axis_index("c"))
        copies = [
            pltpu.make_async_remote_copy(
                src_ref=in_refs[src], dst_ref=out_refs[k], send_sem=send_sems.at[k], recv_sem=recv_sems.at[k],
                device_id=peers[peer], device_id_type=MESH)
            for k, (peer, src) in enumerate(plan)
        ]
        for cp in copies:
            cp.start()
        for cp in copies:
            cp.wait()

    any_spec = pl.BlockSpec(memory_space=pl.ANY)
    return list(pl.pallas_call(
        body, name=name,
        in_specs=[any_spec] * n_in, out_specs=[any_spec] * n_out,
        out_shape=[jax.ShapeDtypeStruct(ins[src].shape, ins[src].dtype) for _, src in plan],
        scratch_shapes=[pltpu.SemaphoreType.DMA((n_out,)), pltpu.SemaphoreType.DMA((n_out,))],
    )(*ins))


def _sibling(x, y, c):
    return [(x, y, 1 - c)]


def _other_chips(x, y, c):
    return [(1 - x, y, c), (x, 1 - y, c), (1 - x, 1 - y, c)]


def _swap_sibling(arrs, name):
    return _remote_exchange(arrs, [(0, i) for i in range(len(arrs))], _sibling, name)


def _bcast_chips(arrs, name):
    n = len(arrs)
    out = _remote_exchange(arrs, [(p, i) for p in range(3) for i in range(n)], _other_chips, name)
    return [out[p * n:(p + 1) * n] for p in range(3)]


def _by_chip(me, own, got3):
    by_rel = [own, got3[0], got3[1], got3[2]]
    rel_bits = (0, 2, 1, 3)

    def branch(m):
        def f(ops):
            return [ops[rel_bits.index(i ^ m)] for i in range(4)]
        return f

    return lax.switch(me, [branch(m) for m in range(4)], by_rel)


BIG = ("ffn1_w_in", "ffn1_w_out", "mix_w_in", "mix_w_out", "ffn2_w_in", "ffn2_w_out")
BY_COLUMNS = ("ffn1_w_in", "ffn2_w_in")


def _layer_shape(n, shard_shape):
    r, cs = shard_shape
    return (r, 4 * cs) if n in BY_COLUMNS else (4, r, cs)


def _chip_ids(x, y):
    chips = [(1 - x, y), (x, 1 - y), (1 - x, 1 - y)]
    return chips, [2 * cx + cy for cx, cy in chips]


def _half_slot(ref, n, chip, h):
    if n in BY_COLUMNS:
        hr, w = ref.shape[0] // 2, ref.shape[1] // 4
        return ref.at[pl.ds(pl.multiple_of(h * hr, 16), hr), pl.ds(pl.multiple_of(chip * w, LANES), w)]
    hr = ref.shape[1] // 2
    return ref.at[chip, pl.ds(pl.multiple_of(h * hr, 16), hr)]


def _gather_layer_body(layer, in_refs, out_refs, sems, handshake):
    send_chip, recv_chip, send_sib, recv_sib, local, send_fwd, recv_fwd = sems
    n_w = len(BIG)
    x, y, c = lax.axis_index("x"), lax.axis_index("y"), lax.axis_index("c")
    me = 2 * x + y
    chips, chip_idx = _chip_ids(x, y)
    sibling = (x, y, 1 - c)
    if handshake:
        barrier = pltpu.get_barrier_semaphore()
        for peer in [sibling] + [(*ch, c) for ch in chips]:
            pl.semaphore_signal(barrier, inc=1, device_id=peer, device_id_type=MESH)
        pl.semaphore_wait(barrier, 4)

    def remote(src, dst, s_sem, r_sem, to):
        return pltpu.make_async_remote_copy(src_ref=src, dst_ref=dst, send_sem=s_sem, recv_sem=r_sem,
                                            device_id=to, device_id_type=MESH)

    sends, own_copies = [], []
    for i, n in enumerate(BIG):
        shard = in_refs[i].at[layer]
        hr = shard.shape[0] // 2
        src = shard.at[pl.ds(pl.multiple_of(c * hr, 16), hr)]
        mine = _half_slot(out_refs[i], n, me, c)
        own = pltpu.make_async_copy(src, mine, local.at[i])
        own.start()
        own_copies.append(own)
        sends.append(remote(src, mine, send_sib.at[i], recv_sib.at[i], sibling))
        sends[-1].start()
        for p in range(3):
            k = p * n_w + i
            sends.append(remote(src, mine, send_chip.at[k], recv_chip.at[k], (*chips[p], c)))
            sends[-1].start()
    for p in range(3):
        for i, n in enumerate(BIG):
            k = p * n_w + i
            landed = _half_slot(out_refs[i], n, chip_idx[p], c)
            remote(landed, landed, send_chip.at[k], recv_chip.at[k], sibling).wait_recv()
            sends.append(remote(landed, landed, send_fwd.at[k], recv_fwd.at[k], sibling))
            sends[-1].start()
    for i, n in enumerate(BIG):
        theirs = _half_slot(out_refs[i], n, me, 1 - c)
        remote(theirs, theirs, send_sib.at[i], recv_sib.at[i], sibling).wait_recv()
        for p in range(3):
            k = p * n_w + i
            theirs = _half_slot(out_refs[i], n, chip_idx[p], 1 - c)
            remote(theirs, theirs, send_fwd.at[k], recv_fwd.at[k], sibling).wait_recv()
    for own in own_copies:
        own.wait()
    for cp in sends:
        cp.wait_send()


def _gather_layer_sems():
    n_w = len(BIG)
    dma = pltpu.SemaphoreType.DMA
    return [dma((3 * n_w,)), dma((3 * n_w,)), dma((n_w,)), dma((n_w,)), dma((n_w,)), dma((3 * n_w,)), dma((3 * n_w,))]


def _gather_layer(shards, layer, name):
    n_w = len(BIG)

    def body(*refs):
        _gather_layer_body(layer, refs[:n_w], refs[n_w:2 * n_w], refs[2 * n_w:], handshake=False)

    any_spec = pl.BlockSpec(memory_space=pl.ANY)
    return list(pl.pallas_call(
        body, name=name, in_specs=[any_spec] * n_w, out_specs=[any_spec] * n_w,
        out_shape=[jax.ShapeDtypeStruct(_layer_shape(n, s.shape[1:]), BF16) for n, s in zip(BIG, shards)],
        scratch_shapes=_gather_layer_sems(),
    )(*shards))


def _gather_layer_sequencer(shards, layer, after, name):
    n_w = len(BIG)

    def body(*refs):
        _gather_layer_body(layer, refs[:n_w], refs[n_w + 1:2 * n_w + 1], refs[2 * n_w + 1:], handshake=True)

    return list(pl.kernel(
        body, name=name,
        out_type=[jax.ShapeDtypeStruct(_layer_shape(n, s.shape[1:]), BF16) for n, s in zip(BIG, shards)],
        mesh=plsc.ScalarSubcoreMesh(axis_name="sequencer", num_cores=1),
        scratch_types=_gather_layer_sems(),
        compiler_params=pltpu.CompilerParams(collective_id=1),
    )(*shards, after))


def _comm_call(body_fn, ins, out_shapes, sem_types, name, sequencer, collective_id):
    n_in, n_out = len(ins), len(out_shapes)

    def body(*refs):
        body_fn(refs[:n_in], refs[n_in:n_in + n_out], refs[n_in + n_out:], sequencer)

    if sequencer:
        return list(pl.kernel(
            body, name=name, out_type=out_shapes,
            mesh=plsc.ScalarSubcoreMesh(axis_name="sequencer", num_cores=1), scratch_types=sem_types,
            compiler_params=pltpu.CompilerParams(collective_id=collective_id),
        )(*ins))
    any_spec = pl.BlockSpec(memory_space=pl.ANY)
    return list(pl.pallas_call(
        body, name=name, in_specs=[any_spec] * n_in, out_specs=[any_spec] * n_out, out_shape=out_shapes,
        scratch_shapes=sem_types,
    )(*ins))


def _shake_hands(peers):
    barrier = pltpu.get_barrier_semaphore()
    for peer in peers:
        pl.semaphore_signal(barrier, inc=1, device_id=peer, device_id_type=MESH)
    pl.semaphore_wait(barrier, len(peers))


def _remote(src, dst, s_sem, r_sem, to):
    return pltpu.make_async_remote_copy(src_ref=src, dst_ref=dst, send_sem=s_sem, recv_sem=r_sem,
                                        device_id=to, device_id_type=MESH)


def _rows_half(ref, n, h):
    if n in BY_COLUMNS:
        hr = ref.shape[0] // 2
        return ref.at[pl.ds(pl.multiple_of(h * hr, 16), hr)]
    hr = ref.shape[1] // 2
    return ref.at[:, pl.ds(pl.multiple_of(h * hr, 16), hr)]


def _half_form_shape(n, wire_shape):
    if n in BY_COLUMNS:
        return (wire_shape[0] // 2, wire_shape[1])
    return (wire_shape[0], wire_shape[1] // 2, wire_shape[2])


def _grad_swap_halves(wire, name, sequencer):
    n_w = len(BIG)

    def body(in_refs, out_refs, sems, handshake):
        send, recv = sems
        x, y, c = lax.axis_index("x"), lax.axis_index("y"), lax.axis_index("c")
        sibling = (x, y, 1 - c)
        if handshake:
            _shake_hands([sibling])
        cps = [_remote(_rows_half(in_refs[i], n, 1 - c), out_refs[i], send.at[i], recv.at[i], sibling)
               for i, n in enumerate(BIG)]
        for cp in cps:
            cp.start()
        for cp in cps:
            cp.wait()

    shapes = [jax.ShapeDtypeStruct(_half_form_shape(n, w.shape), w.dtype) for n, w in zip(BIG, wire)]
    dma = pltpu.SemaphoreType.DMA
    return _comm_call(body, list(wire), shapes, [dma((n_w,)), dma((n_w,))], name, sequencer, 2)


def _grad_add_halves(g, got, ci_arr, name):
    g3 = g if g.ndim == 3 else g[None]
    r3 = got if got.ndim == 3 else got[None]
    s, hr, cc = r3.shape
    tb = _pick(hr, (256, 176, 128))
    nb = hr // tb

    def body(s_ref, g_ref, r_ref, o_ref):
        o_ref[...] = (g_ref[...].astype(F32) + r_ref[...].astype(F32)).astype(BF16)

    out = pl.pallas_call(
        body, name=name,
        grid_spec=pltpu.PrefetchScalarGridSpec(
            num_scalar_prefetch=1, grid=(nb,),
            in_specs=[pl.BlockSpec((s, tb, cc), lambda i, sc: (0, sc[0] * nb + i, 0)),
                      pl.BlockSpec((s, tb, cc), lambda i, sc: (0, i, 0))],
            out_specs=pl.BlockSpec((s, tb, cc), lambda i, sc: (0, i, 0))),
        out_shape=jax.ShapeDtypeStruct(r3.shape, BF16),
        compiler_params=_cp("arbitrary"),
    )(ci_arr, g3, r3)
    return out.reshape(got.shape)


def _grad_scatter_halves(parts, name, sequencer):
    n_w = len(BIG)

    def slot(ref, n, chip):
        if n in BY_COLUMNS:
            w = ref.shape[1] // 4
            return ref.at[:, pl.ds(pl.multiple_of(chip * w, LANES), w)]
        return ref.at[chip]

    def body(in_refs, out_refs, sems, handshake):
        send, recv = sems
        x, y, c = lax.axis_index("x"), lax.axis_index("y"), lax.axis_index("c")
        chips, chip_idx = _chip_ids(x, y)
        if handshake:
            _shake_hands([(*ch, c) for ch in chips])
        cps = []
        for p in range(3):
            for i, n in enumerate(BIG):
                k = p * n_w + i
                cps.append(_remote(slot(in_refs[i], n, chip_idx[p]), out_refs[k], send.at[k], recv.at[k],
                                   (*chips[p], c)))
        for cp in cps:
            cp.start()
        for cp in cps:
            cp.wait()

    def slot_shape(n, a):
        return (a.shape[0], a.shape[1] // 4) if n in BY_COLUMNS else a.shape[1:]

    shapes = [jax.ShapeDtypeStruct(slot_shape(n, a), BF16) for _ in range(3) for n, a in zip(BIG, parts)]
    dma = pltpu.SemaphoreType.DMA
    out = _comm_call(body, list(parts), shapes, [dma((3 * n_w,)), dma((3 * n_w,))], name, sequencer, 3)
    return [out[p * n_w:(p + 1) * n_w] for p in range(3)]


def _grad_add_slots(part, got3, n, me_arr, after, name):
    hr, cs = got3[0].shape
    tb = _pick(hr, (256, 176, 128))

    def body(s_ref, own_ref, a_ref, b_ref, c_ref, after_ref, out_ref):
        acc = own_ref[...].astype(F32) + a_ref[...].astype(F32)
        out_ref[...] = (acc + b_ref[...].astype(F32)) + c_ref[...].astype(F32)

    if n in BY_COLUMNS:
        own_spec = pl.BlockSpec((tb, cs), lambda i, s: (i, s[0]))
    else:
        own_spec = pl.BlockSpec((None, tb, cs), lambda i, s: (s[0], i, 0))
    plain = pl.BlockSpec((tb, cs), lambda i, s: (i, 0))
    return pl.pallas_call(
        body, name=name,
        grid_spec=pltpu.PrefetchScalarGridSpec(
            num_scalar_prefetch=1, grid=(hr // tb,),
            in_specs=[own_spec, plain, plain, plain, pl.BlockSpec(memory_space=pl.ANY)],
            out_specs=plain),
        out_shape=jax.ShapeDtypeStruct((hr, cs), F32),
        compiler_params=_cp("arbitrary"),
    )(me_arr, part, *got3, after)


def _grad_swap_reduced(halves, name, sequencer):
    n_w = len(BIG)

    def body(in_refs, out_refs, sems, handshake):
        send, recv = sems
        x, y, c = lax.axis_index("x"), lax.axis_index("y"), lax.axis_index("c")
        sibling = (x, y, 1 - c)
        if handshake:
            _shake_hands([sibling])
        cps = [_remote(in_refs[i], out_refs[i], send.at[i], recv.at[i], sibling) for i in range(n_w)]
        for cp in cps:
            cp.start()
        for cp in cps:
            cp.wait()

    shapes = [jax.ShapeDtypeStruct(h.shape, F32) for h in halves]
    dma = pltpu.SemaphoreType.DMA
    return _comm_call(body, list(halves), shapes, [dma((n_w,)), dma((n_w,))], name, sequencer, 4)


def _adamw_layers(w, own, other, m, v, ci_arr, name):
    _, rs, cs = w.shape
    hr = rs // 2
    tb = _pick(hr, (256, 176, 128))
    nbh = hr // tb

    def body(s_ref, w_ref, o0, t0, o1, t1, m_ref, v_ref, g_out, d_out, m_out, v_out):
        mine = (pl.program_id(1) // nbh) == s_ref[0]
        g = jnp.where(pl.program_id(0) == 0, jnp.where(mine, o0[...], t0[...]), jnp.where(mine, o1[...], t1[...]))
        g_out[...] = g
        d_out[...], m_out[...], v_out[...] = _adamw_math(w_ref[...], g, m_ref[...], v_ref[...])

    both = pl.BlockSpec((None, tb, cs), lambda l, i, s: (l, i, 0))

    def half(layer, is_own):
        def index(l, i, s):
            own_block = (i // nbh) == s[0]
            use = (l == layer) & (own_block if is_own else jnp.logical_not(own_block))
            return (jnp.where(use, i % nbh, 0), 0)
        return pl.BlockSpec((tb, cs), index)

    return pl.pallas_call(
        body, name=name,
        grid_spec=pltpu.PrefetchScalarGridSpec(
            num_scalar_prefetch=1, grid=(DEPTH, rs // tb),
            in_specs=[both, half(0, True), half(0, False), half(1, True), half(1, False), both, both],
            out_specs=[both] * 4),
        out_shape=[jax.ShapeDtypeStruct(w.shape, F32)] * 4,
        compiler_params=_cp("arbitrary", "arbitrary"),
    )(ci_arr, w, own[0], other[0], own[1], other[1], m, v)


def _shift_down(v, s, t_iota):
    return jnp.where(t_iota >= s, pltpu.roll(v, s, 0), 0.0)


def _shift_up(v, s, t_iota, t):
    return jnp.where(t_iota < t - s, pltpu.roll(v, t - s, 0), 0.0)


def _ssd_conv_fwd(proj, w, b):
    t = proj.shape[0]
    k_w = SSD_CONV_K

    def body(x_ref, w_ref, b_ref, o_ref):
        x = x_ref[...]
        ti = lax.broadcasted_iota(jnp.int32, x.shape, 0)
        pre = x * w_ref[k_w - 1:k_w, :] + b_ref[...]
        for s in range(1, k_w):
            pre = pre + _shift_down(x, s, ti) * w_ref[k_w - 1 - s:k_w - s, :]
        o_ref[...] = pre * _sigmoid(pre)

    off = P_XBC // LANES
    return pl.pallas_call(
        body, name="ssd_conv_fwd", grid=(SSD_CONV_DIM // LANES,),
        in_specs=[pl.BlockSpec((t, LANES), lambda j: (0, off + j)),
                  pl.BlockSpec((k_w, LANES), lambda j: (0, j)),
                  pl.BlockSpec((1, LANES), lambda j: (0, j))],
        out_specs=pl.BlockSpec((t, LANES), lambda j: (0, j)),
        out_shape=jax.ShapeDtypeStruct((t, SSD_CONV_DIM), F32),
        compiler_params=_cp("parallel"),
    )(proj, w, b)


def _ssd_conv_bwd(dxbc, proj, w, b):
    t = proj.shape[0]
    k_w = SSD_CONV_K

    def body(d_ref, x_ref, w_ref, b_ref, dx_ref, dw_ref, db_ref):
        x = x_ref[...]
        ti = lax.broadcasted_iota(jnp.int32, x.shape, 0)
        shifted = [x] + [_shift_down(x, s, ti) for s in range(1, k_w)]
        pre = b_ref[...] + shifted[0] * w_ref[k_w - 1:k_w, :]
        for s in range(1, k_w):
            pre = pre + shifted[s] * w_ref[k_w - 1 - s:k_w - s, :]
        sg = _sigmoid(pre)
        dpre = d_ref[...] * (sg * (1.0 + pre * (1.0 - sg)))
        db_ref[...] = jnp.sum(dpre, axis=0, keepdims=True)
        dx = dpre * w_ref[k_w - 1:k_w, :]
        for s in range(k_w):
            dw_ref[k_w - 1 - s:k_w - s, :] = jnp.sum(dpre * shifted[s], axis=0, keepdims=True)
            if s:
                dx = dx + _shift_up(dpre, s, ti, t) * w_ref[k_w - 1 - s:k_w - s, :]
        dx_ref[...] = dx

    off = P_XBC // LANES
    return pl.pallas_call(
        body, name="ssd_conv_bwd", grid=(SSD_CONV_DIM // LANES,),
        in_specs=[pl.BlockSpec((t, LANES), lambda j: (0, j)),
                  pl.BlockSpec((t, LANES), lambda j: (0, off + j)),
                  pl.BlockSpec((k_w, LANES), lambda j: (0, j)),
                  pl.BlockSpec((1, LANES), lambda j: (0, j))],
        out_specs=[pl.BlockSpec((t, LANES), lambda j: (0, j)),
                   pl.BlockSpec((k_w, LANES), lambda j: (0, j)),
                   pl.BlockSpec((1, LANES), lambda j: (0, j))],
        out_shape=[jax.ShapeDtypeStruct((t, SSD_CONV_DIM), F32),
                   jax.ShapeDtypeStruct((k_w, SSD_CONV_DIM), F32),
                   jax.ShapeDtypeStruct((1, SSD_CONV_DIM), F32)],
        compiler_params=_cp("parallel"),
    )(dxbc, proj, w, b)


def _shortconv_fwd(proj, w):
    t = proj.shape[0]
    nb = SC_W // LANES
    off = P_SC // LANES

    def body(b_ref, c_ref, x_ref, w_ref, o_ref):
        u = c_ref[...] * x_ref[...]
        ti = lax.broadcasted_iota(jnp.int32, u.shape, 0)
        cv = u * w_ref[SC_K - 1:SC_K, :]
        for s in range(1, SC_K):
            cv = cv + _shift_down(u, s, ti) * w_ref[SC_K - 1 - s:SC_K - s, :]
        o_ref[...] = (b_ref[...] * cv).astype(BF16)

    return pl.pallas_call(
        body, name="shortconv_fwd", grid=(nb,),
        in_specs=[pl.BlockSpec((t, LANES), lambda j: (0, off + j)),
                  pl.BlockSpec((t, LANES), lambda j: (0, off + nb + j)),
                  pl.BlockSpec((t, LANES), lambda j: (0, off + 2 * nb + j)),
                  pl.BlockSpec((SC_K, LANES), lambda j: (0, j))],
        out_specs=pl.BlockSpec((t, LANES), lambda j: (0, j)),
        out_shape=jax.ShapeDtypeStruct((t, SC_W), BF16),
        compiler_params=_cp("parallel"),
    )(proj, proj, proj, w)


def _shortconv_bwd(dy, dy_off, proj, w):
    t = proj.shape[0]
    nb = SC_W // LANES
    off = P_SC // LANES
    doff = dy_off // LANES

    def body(d_ref, b_ref, c_ref, x_ref, w_ref, db_ref, dc_ref, dx_ref, dw_ref):
        cg, xin = c_ref[...], x_ref[...]
        u = cg * xin
        ti = lax.broadcasted_iota(jnp.int32, u.shape, 0)
        shifted = [u] + [_shift_down(u, s, ti) for s in range(1, SC_K)]
        cv = shifted[0] * w_ref[SC_K - 1:SC_K, :]
        for s in range(1, SC_K):
            cv = cv + shifted[s] * w_ref[SC_K - 1 - s:SC_K - s, :]
        d = d_ref[...]
        db_ref[...] = (d * cv).astype(BF16)
        dcv = d * b_ref[...]
        du = dcv * w_ref[SC_K - 1:SC_K, :]
        for s in range(SC_K):
            dw_ref[SC_K - 1 - s:SC_K - s, :] = jnp.sum(dcv * shifted[s], axis=0, keepdims=True)
            if s:
                du = du + _shift_up(dcv, s, ti, t) * w_ref[SC_K - 1 - s:SC_K - s, :]
        dc_ref[...] = (du * xin).astype(BF16)
        dx_ref[...] = (du * cg).astype(BF16)

    tile = pl.BlockSpec((t, LANES), lambda j: (0, j))
    outs = pl.pallas_call(
        body, name="shortconv_bwd", grid=(nb,),
        in_specs=[pl.BlockSpec((t, LANES), lambda j: (0, doff + j)),
                  pl.BlockSpec((t, LANES), lambda j: (0, off + j)),
                  pl.BlockSpec((t, LANES), lambda j: (0, off + nb + j)),
                  pl.BlockSpec((t, LANES), lambda j: (0, off + 2 * nb + j)),
                  pl.BlockSpec((SC_K, LANES), lambda j: (0, j))],
        out_specs=[tile, tile, tile, pl.BlockSpec((SC_K, LANES), lambda j: (0, j))],
        out_shape=[jax.ShapeDtypeStruct((t, SC_W), BF16)] * 3 + [jax.ShapeDtypeStruct((SC_K, SC_W), F32)],
        compiler_params=_cp("parallel"),
    )(dy, proj, proj, proj, w)
    return outs


def _cum_logf(proj, fbias_row):
    t = proj.shape[0]
    blk = CUM_BLOCK

    def body(p_ref, b_ref, o_ref, carry_ref):
        i = pl.program_id(0)

        @pl.when(i == 0)
        def _():
            carry_ref[...] = jnp.zeros_like(carry_ref)

        lf = -_softplus(-(p_ref[...] + b_ref[...]))
        r = lax.broadcasted_iota(jnp.int32, (blk, blk), 0)
        c = lax.broadcasted_iota(jnp.int32, (blk, blk), 1)
        tri = (r >= c).astype(BF16)
        o_ref[...] = _tri_left(tri, lf) + carry_ref[...]
        carry_ref[...] = o_ref[blk - 1:blk, :]

    return pl.pallas_call(
        body, name="cum_logf", grid=(t // blk,),
        in_specs=[pl.BlockSpec((blk, LANES), lambda i: (i, P_PAD // LANES)),
                  pl.BlockSpec((1, LANES), lambda i: (0, 0))],
        out_specs=pl.BlockSpec((blk, LANES), lambda i: (i, 0)),
        out_shape=jax.ShapeDtypeStruct((t, LANES), F32),
        scratch_shapes=[pltpu.VMEM((1, LANES), F32)],
        compiler_params=_cp("arbitrary"),
    )(proj, fbias_row)


def _pad_block_bwd(dcf, ddt, proj, fbias_row):
    t = proj.shape[0]
    blk = CUM_BLOCK
    nb = t // blk

    def body(dcf_ref, ddt_ref, p_ref, b_ref, o_ref, db_ref, carry_ref):
        i = pl.program_id(0)

        @pl.when(i == 0)
        def _():
            carry_ref[...] = jnp.zeros_like(carry_ref)
            db_ref[...] = jnp.zeros_like(db_ref)

        r = lax.broadcasted_iota(jnp.int32, (blk, blk), 0)
        c = lax.broadcasted_iota(jnp.int32, (blk, blk), 1)
        tri = (r <= c).astype(BF16)
        rev = _tri_left(tri, dcf_ref[...]) + carry_ref[...]
        carry_ref[...] = jnp.sum(dcf_ref[...], axis=0, keepdims=True) + carry_ref[...]
        lane = lax.broadcasted_iota(jnp.int32, (blk, LANES), 1)
        is_f = (lane >= PAD_F0) & (lane < PAD_F0 + FOX_HEADS)
        df = jnp.where(is_f, rev * _sigmoid(-(p_ref[...] + b_ref[...])), 0.0)
        db_ref[...] += jnp.sum(df, axis=0, keepdims=True)
        o_ref[...] = jnp.where(lane < PAD_DT0 + SSD_HEADS, ddt_ref[...], df).astype(BF16)

    return pl.pallas_call(
        body, name="pad_block_bwd", grid=(nb,),
        in_specs=[pl.BlockSpec((blk, LANES), lambda i: (nb - 1 - i, 0)),
                  pl.BlockSpec((blk, LANES), lambda i: (nb - 1 - i, 0)),
                  pl.BlockSpec((blk, LANES), lambda i: (nb - 1 - i, P_PAD // LANES)),
                  pl.BlockSpec((1, LANES), lambda i: (0, 0))],
        out_specs=[pl.BlockSpec((blk, LANES), lambda i: (nb - 1 - i, 0)),
                   pl.BlockSpec((1, LANES), lambda i: (0, 0))],
        out_shape=[jax.ShapeDtypeStruct((t, LANES), BF16), jax.ShapeDtypeStruct((1, LANES), F32)],
        scratch_shapes=[pltpu.VMEM((1, LANES), F32)],
        compiler_params=_cp("arbitrary"),
    )(dcf, ddt, proj, fbias_row)


def _att_scores(q, k, cq, ck, diagonal, blk):
    s = _dot_nt(q, k) * (FOX_HD ** -0.5) + cq - ck
    if diagonal:
        r = lax.broadcasted_iota(jnp.int32, (blk, blk), 0)
        c = lax.broadcasted_iota(jnp.int32, (blk, blk), 1)
        s = jnp.where(r >= c, s, NEG)
    return s


def _fox_fwd(q, k, v, cf_col, cf_row):
    h, t, hd = q.shape
    blk = min(ATT_BLOCK, t)
    nb = t // blk

    def body(q_ref, k_ref, v_ref, cq_ref, ck_ref, o_ref, lse_ref):
        qi = pl.program_id(1)
        qv, cq = q_ref[...], cq_ref[...]

        def step(j, carry, diagonal):
            m, l, acc = carry
            off = pl.multiple_of(j * blk, blk)
            s = _att_scores(qv, k_ref[pl.ds(off, blk), :], cq, ck_ref[:, pl.ds(off, blk)], diagonal, blk)
            m_new = jnp.maximum(m, jnp.max(s, axis=1, keepdims=True))
            alpha = jnp.exp(m - m_new)
            p = jnp.exp(s - m_new)
            l = alpha * l + jnp.sum(p, axis=1, keepdims=True)
            acc = alpha * acc + _dot(p.astype(BF16), v_ref[pl.ds(off, blk), :])
            return m_new, l, acc

        init = (jnp.full((blk, 1), NEG, F32), jnp.zeros((blk, 1), F32), jnp.zeros((blk, hd), F32))
        carry = lax.fori_loop(0, qi, lambda j, cr: step(j, cr, False), init)
        m, l, acc = step(qi, carry, True)
        o_ref[...] = acc / l
        lse_ref[...] = m + jnp.log(l)

    qmap = lambda hh, i: (hh, i, 0)
    whole = lambda hh, i: (hh, 0, 0)
    return pl.pallas_call(
        body, name="fox_fwd", grid=(h, nb),
        in_specs=[pl.BlockSpec((None, blk, hd), qmap), pl.BlockSpec((None, t, hd), whole),
                  pl.BlockSpec((None, t, hd), whole), pl.BlockSpec((None, blk, 1), qmap),
                  pl.BlockSpec((None, 1, t), whole)],
        out_specs=[pl.BlockSpec((None, blk, hd), qmap), pl.BlockSpec((None, blk, 1), qmap)],
        out_shape=[jax.ShapeDtypeStruct((h, t, hd), F32), jax.ShapeDtypeStruct((h, t, 1), F32)],
        compiler_params=_cp("parallel", "arbitrary"),
    )(q, k, v, cf_col, cf_row)


def _fox_bwd(q, k, v, cf_col, cf_row, o, lse, do):
    h, t, hd = q.shape
    blk = min(ATT_BLOCK, t)
    nb = t // blk
    scale = FOX_HD ** -0.5

    def body(q_ref, k_ref, v_ref, cq_ref, ck_ref, o_ref, lse_ref, do_ref,
             dq_ref, dk_ref, dv_ref, dcq_ref, dck_ref, delta_s):
        kj = pl.program_id(1)

        @pl.when(kj == 0)
        def _():
            dq_ref[...] = jnp.zeros_like(dq_ref)
            dcq_ref[...] = jnp.zeros_like(dcq_ref)

            def fill(i, _):
                rows = pl.ds(pl.multiple_of(i * blk, blk), blk)
                delta_s[rows, :] = jnp.sum(do_ref[rows, :] * o_ref[rows, :], axis=1, keepdims=True)
                return 0

            lax.fori_loop(0, nb, fill, 0)

        kb, vb, ck = k_ref[...], v_ref[...], ck_ref[...]

        def step(i, carry, diagonal):
            dk, dv, dck = carry
            rows = pl.ds(pl.multiple_of(i * blk, blk), blk)
            qb = q_ref[rows, :]
            do_b = do_ref[rows, :].astype(BF16)
            s = _att_scores(qb, kb, cq_ref[rows, :], ck, diagonal, blk)
            p = jnp.exp(s - lse_ref[rows, :])
            dv = dv + _dot_tn(p.astype(BF16), do_b)
            ds = p * (_dot_nt(do_b, vb) - delta_s[rows, :])
            ds_b = ds.astype(BF16)
            dk = dk + _dot_tn(ds_b, qb)
            dq_ref[rows, :] += _dot(ds_b, kb) * scale
            dcq_ref[rows, :] += jnp.sum(ds, axis=1, keepdims=True)
            dck = dck - jnp.sum(ds, axis=0, keepdims=True)
            return dk, dv, dck

        init = (jnp.zeros((blk, hd), F32), jnp.zeros((blk, hd), F32), jnp.zeros((1, blk), F32))
        carry = step(kj, init, True)
        dk, dv, dck = lax.fori_loop(kj + 1, nb, lambda i, cr: step(i, cr, False), carry)
        dk_ref[...] = dk * scale
        dv_ref[...] = dv
        dck_ref[...] = dck

    kmap = lambda hh, j: (hh, j, 0)
    whole = lambda hh, j: (hh, 0, 0)
    return pl.pallas_call(
        body, name="fox_bwd", grid=(h, nb),
        in_specs=[pl.BlockSpec((None, t, hd), whole), pl.BlockSpec((None, blk, hd), kmap),
                  pl.BlockSpec((None, blk, hd), kmap), pl.BlockSpec((None, t, 1), whole),
                  pl.BlockSpec((None, 1, blk), lambda hh, j: (hh, 0, j)),
                  pl.BlockSpec((None, t, hd), whole), pl.BlockSpec((None, t, 1), whole),
                  pl.BlockSpec((None, t, hd), whole)],
        out_specs=[pl.BlockSpec((None, t, hd), whole), pl.BlockSpec((None, blk, hd), kmap),
                   pl.BlockSpec((None, blk, hd), kmap), pl.BlockSpec((None, t, 1), whole),
                   pl.BlockSpec((None, 1, blk), lambda hh, j: (hh, 0, j))],
        out_shape=[jax.ShapeDtypeStruct((h, t, hd), F32), jax.ShapeDtypeStruct((h, t, hd), F32),
                   jax.ShapeDtypeStruct((h, t, hd), F32), jax.ShapeDtypeStruct((h, t, 1), F32),
                   jax.ShapeDtypeStruct((h, 1, t), F32)],
        scratch_shapes=[pltpu.VMEM((t, 1), F32)],
        compiler_params=_cp("parallel", "arbitrary"),
    )(q, k, v, cf_col, cf_row, o, lse, do)


def _lane_col(v, h):
    lane = lax.broadcasted_iota(jnp.int32, v.shape, 1)
    return jnp.sum(jnp.where(lane == h, v, 0.0), axis=1, keepdims=True)


def _sub_row(v, h):
    sub = lax.broadcasted_iota(jnp.int32, v.shape, 0)
    return jnp.sum(jnp.where(sub == h, v, 0.0), axis=0, keepdims=True)


def _ssd_decays(pad, pad_t, dtb_row, alog_row, dtb_col, alog_col, blk):
    r = lax.broadcasted_iota(jnp.int32, (blk, blk), 0)
    c = lax.broadcasted_iota(jnp.int32, (blk, blk), 1)
    tril = r >= c
    dt_c = _softplus(pad + dtb_row)
    acs_c = _tri_left(tril.astype(BF16), dt_c * (-jnp.exp(alog_row)))
    dt_r = _softplus(pad_t + dtb_col)
    acs_r = _tri_right(dt_r * (-jnp.exp(alog_col)), (r <= c).astype(BF16))
    rows = lax.broadcasted_iota(jnp.int32, acs_c.shape, 0)
    acs_last = jnp.sum(jnp.where(rows == blk - 1, acs_c, 0.0), axis=0, keepdims=True)
    return dt_c, acs_c, acs_r, acs_last, tril


def _pair_terms(pair, dt_c, acs_c, acs_r, acs_last, d_row, blk):
    lane = lax.broadcasted_iota(jnp.int32, (blk, LANES), 1)
    lo = lane < 64
    lo_row = lax.broadcasted_iota(jnp.int32, (1, LANES), 1) < 64
    h0, h1 = 2 * pair, 2 * pair + 1
    col = [_lane_col(acs_c, h0), _lane_col(acs_c, h1)]
    row = [_sub_row(acs_r, h0), _sub_row(acs_r, h1)]
    last = [_lane_col(acs_last, h0), _lane_col(acs_last, h1)]
    dt_p = jnp.where(lo, _lane_col(dt_c, h0), _lane_col(dt_c, h1))
    e_p = jnp.where(lo, jnp.exp(col[0]), jnp.exp(col[1]))
    w_p = jnp.where(lo, jnp.exp(last[0] - col[0]), jnp.exp(last[1] - col[1]))
    decay_p = jnp.where(lo_row, jnp.exp(last[0]), jnp.exp(last[1]))
    d_p = jnp.where(lo_row, _lane_col(d_row, h0), _lane_col(d_row, h1))
    return lo, lo_row, col, row, last, dt_p, e_p, w_p, decay_p, d_p


def _ssd_specs(t, blk, rev):
    nc = t // blk
    ix = (lambda i: nc - 1 - i) if rev else (lambda i: i)
    xbc = pl.BlockSpec((blk, SSD_CONV_DIM), lambda i: (ix(i), 0))
    pad = pl.BlockSpec((blk, LANES), lambda i: (ix(i), P_PAD // LANES))
    pad_t = pl.BlockSpec((LANES, blk), lambda i: (0, ix(i)))
    z = pl.BlockSpec((blk, SSD_W), lambda i: (ix(i), 0))
    row = pl.BlockSpec((1, LANES), lambda i: (0, 0))
    colv = pl.BlockSpec((LANES, 1), lambda i: (0, 0))
    ng = pl.BlockSpec((1, SSD_W), lambda i: (0, 0))
    y = pl.BlockSpec((blk, SSD_W), lambda i: (ix(i), 0))
    st = pl.BlockSpec((None, 4, LANES, LANES), lambda i: (ix(i), 0, 0, 0))
    return nc, xbc, pad, pad_t, z, row, colv, ng, y, st


def _ssd_fwd(xbc, proj, pad_t, dtb_row, alog_row, d_row, dtb_col, alog_col, ng):
    t = xbc.shape[0]
    blk = min(SSD_CHUNK, t)
    nc, s_xbc, s_pad, s_padt, s_z, s_row, s_col, s_ng, s_y, s_st = _ssd_specs(t, blk, False)

    def body(xbc_ref, pad_ref, padt_ref, z_ref, dtb_ref, alog_ref, d_ref, dtbc_ref, alogc_ref, ng_ref,
             out_ref, ypre_ref, st_ref, state):
        @pl.when(pl.program_id(0) == 0)
        def _():
            state[...] = jnp.zeros_like(state)

        dt_c, acs_c, acs_r, acs_last, tril = _ssd_decays(
            pad_ref[...], padt_ref[...], dtb_ref[...], alog_ref[...], dtbc_ref[...], alogc_ref[...], blk)
        ys = []
        g_mat = {}
        for pair in range(4):
            g = pair // 2
            bg = xbc_ref[:, SSD_W + LANES * g:SSD_W + LANES * (g + 1)].astype(BF16)
            cg = xbc_ref[:, SSD_W + 2 * LANES + LANES * g:SSD_W + 2 * LANES + LANES * (g + 1)].astype(BF16)
            if g not in g_mat:
                g_mat[g] = _dot_nt(cg, bg)
            xs_p = xbc_ref[:, LANES * pair:LANES * (pair + 1)]
            lo, _, col, row, _, dt_p, e_p, w_p, decay_p, d_p = _pair_terms(
                pair, dt_c, acs_c, acs_r, acs_last, d_ref[...], blk)
            x_p = xs_p * dt_p
            y = None
            for hh in range(2):
                lm = jnp.exp(jnp.where(tril, col[hh] - row[hh], NEG))
                m_h = (g_mat[g] * lm).astype(BF16)
                x_h = jnp.where(lo if hh == 0 else ~lo, x_p, 0.0).astype(BF16)
                y_h = _dot(m_h, x_h)
                y = y_h if y is None else y + y_h
            st_in = state[pair]
            st_ref[pair] = st_in
            y = y + e_p * _dot(cg, st_in.astype(BF16))
            state[pair] = decay_p * st_in + _dot_tn(bg, (x_p * w_p).astype(BF16))
            ys.append(y + d_p * xs_p)
        y_all = jnp.concatenate(ys, axis=1)
        ypre_ref[...] = y_all
        z = z_ref[...]
        y2 = y_all * (z * _sigmoid(z))
        outs = []
        for g in range(2):
            seg = y2[:, 256 * g:256 * (g + 1)]
            rr = lax.rsqrt(jnp.mean(seg * seg, axis=-1, keepdims=True) + RMS_EPS)
            outs.append(seg * rr * ng_ref[:, 256 * g:256 * (g + 1)])
        out_ref[...] = jnp.concatenate(outs, axis=1).astype(BF16)

    return pl.pallas_call(
        body, name="ssd_scan_fwd", grid=(nc,),
        in_specs=[s_xbc, s_pad, s_padt, s_z, s_row, s_row, s_row, s_col, s_col, s_ng],
        out_specs=[s_y, s_y, s_st],
        out_shape=[jax.ShapeDtypeStruct((t, SSD_W), BF16), jax.ShapeDtypeStruct((t, SSD_W), F32),
                   jax.ShapeDtypeStruct((nc, 4, LANES, LANES), F32)],
        scratch_shapes=[pltpu.VMEM((4, LANES, LANES), F32)],
        compiler_params=_cp("arbitrary"),
    )(xbc, proj, pad_t, proj, dtb_row, alog_row, d_row, dtb_col, alog_col, ng)


def _ssd_bwd(dout, dout_off, xbc, proj, pad_t, ypre, states, dtb_row, alog_row, d_row, dtb_col, alog_col, ng):
    t = xbc.shape[0]
    blk = min(SSD_CHUNK, t)
    nc, s_xbc, s_pad, s_padt, s_z, s_row, s_col, s_ng, s_y, s_st = _ssd_specs(t, blk, True)
    s_dout = pl.BlockSpec((blk, SSD_W), lambda i: (nc - 1 - i, dout_off // SSD_W))

    def body(dout_ref, xbc_ref, pad_ref, padt_ref, z_ref, ypre_ref, st_ref, dtb_ref, alog_ref, d_ref,
             dtbc_ref, alogc_ref, ng_ref, dxbc_ref, ddt_ref, dz_ref, acc_ref, dng_ref, dstate):
        @pl.when(pl.program_id(0) == 0)
        def _():
            dstate[...] = jnp.zeros_like(dstate)
            acc_ref[...] = jnp.zeros_like(acc_ref)
            dng_ref[...] = jnp.zeros_like(dng_ref)

        pad = pad_ref[...]
        dt_c, acs_c, acs_r, acs_last, tril = _ssd_decays(
            pad, padt_ref[...], dtb_ref[...], alog_ref[...], dtbc_ref[...], alogc_ref[...], blk)
        a_row = -jnp.exp(alog_ref[...])

        z = z_ref[...]
        sz = _sigmoid(z)
        silu_z = z * sz
        y_pre = ypre_ref[...]
        y2 = y_pre * silu_z
        dy2 = []
        for g in range(2):
            sl = slice(256 * g, 256 * (g + 1))
            seg = y2[:, sl]
            rr = lax.rsqrt(jnp.mean(seg * seg, axis=-1, keepdims=True) + RMS_EPS)
            nrm = seg * rr
            d_seg = dout_ref[:, sl]
            dng_ref[:, sl] += jnp.sum(d_seg * nrm, axis=0, keepdims=True)
            dn = d_seg * ng_ref[:, sl]
            dy2.append(rr * (dn - nrm * jnp.mean(dn * nrm, axis=-1, keepdims=True)))
        dy2 = jnp.concatenate(dy2, axis=1)
        dz_ref[...] = (dy2 * y_pre * (sz * (1.0 + z * (1.0 - sz)))).astype(BF16)
        dy_all = dy2 * silu_z

        lane_row = lax.broadcasted_iota(jnp.int32, (1, LANES), 1)
        lane_blk = lax.broadcasted_iota(jnp.int32, (blk, LANES), 1)
        row_col = lax.broadcasted_iota(jnp.int32, (blk, 1), 0)
        ddt = jnp.zeros((blk, LANES), F32)
        dacs = jnp.zeros((blk, LANES), F32)
        dd_row = jnp.zeros((1, LANES), F32)
        ones_b = jnp.ones((blk, LANES), BF16)
        dxs = []
        d_b = [None, None]
        d_c = [None, None]
        d_g = [None, None]
        bgs, cgs = {}, {}
        g_mat = {}
        for pair in range(4):
            g = pair // 2
            if g not in g_mat:
                bgs[g] = xbc_ref[:, SSD_W + LANES * g:SSD_W + LANES * (g + 1)].astype(BF16)
                cgs[g] = xbc_ref[:, SSD_W + 2 * LANES + LANES * g:SSD_W + 2 * LANES + LANES * (g + 1)].astype(BF16)
                g_mat[g] = _dot_nt(cgs[g], bgs[g])
            bg, cg = bgs[g], cgs[g]
            xs_p = xbc_ref[:, LANES * pair:LANES * (pair + 1)]
            lo, lo_row, col, row, last, dt_p, e_p, w_p, decay_p, d_p = _pair_terms(
                pair, dt_c, acs_c, acs_r, acs_last, d_ref[...], blk)
            x_p = xs_p * dt_p
            dy_p = dy_all[:, LANES * pair:LANES * (pair + 1)]
            st_in = st_ref[pair]
            dst = dstate[pair]
            dx_diag = None
            for hh in range(2):
                sel = lo if hh == 0 else ~lo
                lm = jnp.exp(jnp.where(tril, col[hh] - row[hh], NEG))
                m_f = g_mat[g] * lm
                m_h = m_f.astype(BF16)
                x_h = jnp.where(sel, x_p, 0.0).astype(BF16)
                dy_h = jnp.where(sel, dy_p, 0.0).astype(BF16)
                dxd = _dot_tn(m_h, dy_h)
                dm = _dot_nt(dy_h, x_h)
                dg_h = dm * lm
                p_b = (dm * m_f).astype(BF16)
                dacs = dacs + jnp.where(lane_blk == 2 * pair + hh, _dot(p_b, ones_b) - _dot_tn(p_b, ones_b), 0.0)
                dx_diag = dxd if dx_diag is None else dx_diag + dxd
                d_g[g] = dg_h if d_g[g] is None else d_g[g] + dg_h
            st_b = st_in.astype(BF16)
            dst_b = dst.astype(BF16)
            y_off = e_p * _dot(cg, st_b)
            edy = (e_p * dy_p).astype(BF16)
            dc_off = _dot_nt(edy, st_b)
            d_c[g] = dc_off if d_c[g] is None else d_c[g] + dc_off
            dstate[pair] = decay_p * dst + _dot_tn(cg, edy)
            dx_state = _dot(bg, dst_b) * w_p
            db_st = _dot_nt((x_p * w_p).astype(BF16), dst_b)
            d_b[g] = db_st if d_b[g] is None else d_b[g] + db_st
            dx = dx_diag + dx_state
            dxs.append(dx * dt_p + d_p * dy_p)
            prod_dt = dx * xs_p
            prod_acs = dy_p * y_off - x_p * dx_state
            prod_st = x_p * dx_state
            prod_d = dy_p * xs_p
            st_prod = jnp.sum(dst * st_in, axis=0, keepdims=True)
            for hh in range(2):
                h = 2 * pair + hh
                sel = lo if hh == 0 else ~lo
                sel_row = lo_row if hh == 0 else ~lo_row
                ddt_h = jnp.sum(jnp.where(sel, prod_dt, 0.0), axis=1, keepdims=True)
                dacs_h = jnp.sum(jnp.where(sel, prod_acs, 0.0), axis=1, keepdims=True)
                tail = jnp.sum(jnp.sum(jnp.where(sel, prod_st, 0.0), axis=1, keepdims=True), axis=0, keepdims=True)
                tail = tail + jnp.exp(last[hh]) * jnp.sum(jnp.where(sel_row, st_prod, 0.0), axis=1, keepdims=True)
                dacs_h = dacs_h + jnp.where(row_col == blk - 1, tail, 0.0)
                dd_h = jnp.sum(jnp.sum(jnp.where(sel, prod_d, 0.0), axis=1, keepdims=True), axis=0, keepdims=True)
                ddt = ddt + jnp.where(lane_blk == h, ddt_h, 0.0)
                dacs = dacs + jnp.where(lane_blk == h, dacs_h, 0.0)
                dd_row = dd_row + jnp.where(lane_row == h, dd_h, 0.0)
        for g in range(2):
            dg_b = d_g[g].astype(BF16)
            d_c[g] = d_c[g] + _dot(dg_b, bgs[g])
            d_b[g] = d_b[g] + _dot_tn(dg_b, cgs[g])
        r = lax.broadcasted_iota(jnp.int32, (blk, blk), 0)
        c = lax.broadcasted_iota(jnp.int32, (blk, blk), 1)
        da = _tri_left((r <= c).astype(BF16), dacs)
        ddt = ddt + da * a_row
        d_raw = ddt * _sigmoid(pad + dtb_ref[...])
        ddt_ref[...] = d_raw
        acc_ref[0:1, :] += jnp.sum(da * dt_c, axis=0, keepdims=True) * a_row
        acc_ref[1:2, :] += dd_row
        acc_ref[2:3, :] += jnp.sum(d_raw, axis=0, keepdims=True)
        dxbc_ref[...] = jnp.concatenate(dxs + d_b + d_c, axis=1)

    return pl.pallas_call(
        body, name="ssd_scan_bwd", grid=(nc,),
        in_specs=[s_dout, s_xbc, s_pad, s_padt, s_z, s_y, s_st, s_row, s_row, s_row, s_col, s_col, s_ng],
        out_specs=[s_xbc, pl.BlockSpec((blk, LANES), lambda i: (nc - 1 - i, 0)), s_y,
                   pl.BlockSpec((8, LANES), lambda i: (0, 0)), s_ng],
        out_shape=[jax.ShapeDtypeStruct((t, SSD_CONV_DIM), F32), jax.ShapeDtypeStruct((t, LANES), F32),
                   jax.ShapeDtypeStruct((t, SSD_W), BF16), jax.ShapeDtypeStruct((8, LANES), F32),
                   jax.ShapeDtypeStruct((1, SSD_W), F32)],
        scratch_shapes=[pltpu.VMEM((4, LANES, LANES), F32)],
        compiler_params=_cp("arbitrary"),
    )(dout, xbc, proj, pad_t, proj, ypre, states, dtb_row, alog_row, d_row, dtb_col, alog_col, ng)


def _pad_lanes(v, off):
    return jnp.zeros((1, LANES), F32).at[0, off:off + v.shape[0]].set(v)


def _perm_mix_w_in(w):
    z, xbc, dt = w[:, 0:512], w[:, 512:1536], w[:, 1536:1544]
    qkv, f, sc = w[:, 1544:2312], w[:, 2312:2316], w[:, 2316:3084]
    padblk = jnp.zeros((w.shape[0], LANES), w.dtype).at[:, PAD_DT0:PAD_DT0 + 8].set(dt).at[:, PAD_F0:PAD_F0 + 4].set(f)
    return jnp.concatenate([z, xbc, qkv, sc, padblk], axis=1)


def _unperm_mix_w_in(wp):
    z, xbc, qkv, sc = wp[:, 0:512], wp[:, 512:1536], wp[:, 1536:2304], wp[:, 2304:3072]
    dt, f = wp[:, P_PAD + PAD_DT0:P_PAD + PAD_DT0 + 8], wp[:, P_PAD + PAD_F0:P_PAD + PAD_F0 + 4]
    return jnp.concatenate([z, xbc, dt, qkv, f, sc], axis=1)


def _heads(m):
    return jnp.transpose(m.reshape(m.shape[0], FOX_HEADS, FOX_HD), (1, 0, 2))


def _unheads(m):
    return jnp.transpose(m, (1, 0, 2)).reshape(m.shape[1], FOX_W)


def _ffn_fwd(h, w_in, w_out, tag):
    u, a = _ffn_in_swiglu(h, w_in, f"ffn_in_{tag}")
    y = _matmul(a, w_out, "nn", F32, f"ffn_out_{tag}")
    return y, (h, u, a)


def _ffn_bwd(dy, saved, w_in, w_out, tag):
    h, u, a = saved
    du = _ffn_out_dx_swiglu(dy, w_out, u, f"ffn_out_dx_{tag}")
    dw_out = _matmul(a, dy, "tn", BF16, f"ffn_out_dw_{tag}")
    dh = _matmul(du, w_in, "nt", F32, f"ffn_in_dx_{tag}", halves="a")
    dw_in = _matmul(h, du, "tn", BF16, f"ffn_in_dw_{tag}", halves="b")
    return dh, dw_in, dw_out


def _mix_fwd(h, wp, w_out, sp, tag):
    proj = _matmul(h, wp, "nn", F32, f"mix_in_{tag}")
    pad_t = jnp.transpose(proj[:, P_PAD:P_PAD + LANES])
    xbc = _ssd_conv_fwd(proj, sp["conv_w"], sp["conv_b"])
    y_ssd, ypre, states = _ssd_fwd(xbc, proj, pad_t, sp["dtb_row"], sp["alog_row"], sp["d_row"],
                                   sp["dtb_col"], sp["alog_col"], sp["ng"])
    cf = _cum_logf(proj, sp["fbias_row"])
    cf4 = jnp.transpose(cf[:, PAD_F0:PAD_F0 + FOX_HEADS])
    cf_col, cf_row = cf4[:, :, None], cf4[:, None, :]
    q = _heads(proj[:, P_QKV:P_QKV + 256].astype(BF16))
    k = _heads(proj[:, P_QKV + 256:P_QKV + 512].astype(BF16))
    v = _heads(proj[:, P_QKV + 512:P_QKV + 768].astype(BF16))
    o, lse = _fox_fwd(q, k, v, cf_col, cf_row)
    y_sc = _shortconv_fwd(proj, sp["sconv_w"])
    ymix = jnp.concatenate([y_ssd, _unheads(o).astype(BF16), y_sc], axis=1)
    y = _matmul(ymix, w_out, "nn", F32, f"mix_out_{tag}")
    return y, (h, proj, pad_t, xbc, ypre, states, q, k, v, cf_col, cf_row, o, lse, ymix)


def _mix_bwd(dy, saved, wp, w_out, sp, tag):
    h, proj, pad_t, xbc, ypre, states, q, k, v, cf_col, cf_row, o, lse, ymix = saved
    dymix = _matmul(dy, w_out, "nt", F32, f"mix_out_dx_{tag}")
    dw_out = _matmul(ymix, dy, "tn", BF16, f"mix_out_dw_{tag}")
    dxbc, ddt, dz, ssd_acc, dng = _ssd_bwd(dymix, 0, xbc, proj, pad_t, ypre, states, sp["dtb_row"],
                                           sp["alog_row"], sp["d_row"], sp["dtb_col"], sp["alog_col"], sp["ng"])
    dxbc_raw, dconv_w, dconv_b = _ssd_conv_bwd(dxbc, proj, sp["conv_w"], sp["conv_b"])
    do = _heads(dymix[:, SSD_W:SSD_W + FOX_W])
    dq, dk, dv, dcq, dck = _fox_bwd(q, k, v, cf_col, cf_row, o, lse, do)
    dcf4 = dcq[:, :, 0] + dck[:, 0, :]
    dcf = jnp.zeros((h.shape[0], LANES), F32).at[:, PAD_F0:PAD_F0 + FOX_HEADS].set(jnp.transpose(dcf4))
    dpad, dfb = _pad_block_bwd(dcf, ddt, proj, sp["fbias_row"])
    dscb, dscc, dscx, dsconv_w = _shortconv_bwd(dymix, SSD_W + FOX_W, proj, sp["sconv_w"])
    dproj = jnp.concatenate([dz, dxbc_raw.astype(BF16), _unheads(dq).astype(BF16), _unheads(dk).astype(BF16),
                             _unheads(dv).astype(BF16), dscb, dscc, dscx, dpad], axis=1)
    dh = _matmul(dproj, wp, "nt", F32, f"mix_in_dx_{tag}")
    dwp = _matmul(h, dproj, "tn", BF16, f"mix_in_dw_{tag}")
    small = dict(conv_w=dconv_w, conv_b=dconv_b[0], dt_bias=ssd_acc[2, 0:8], a_log=ssd_acc[0, 0:8],
                 d=ssd_acc[1, 0:8], norm_g=dng[0], f_bias=dfb[0, PAD_F0:PAD_F0 + FOX_HEADS], sconv_w=dsconv_w)
    return dh, _unperm_mix_w_in(dwp), dw_out, small


def _local_step(x, tgt, mod, wfull, small_p, on_layer_grads=None, on_layer0_started=None):
    row = lambda v: v.reshape(1, -1)
    subs = [(l, j) for l in range(DEPTH) for j in range(N_SUB)]
    factor = (0.5, 1.0, 0.5)
    w_names = (("ffn1_w_in", "ffn1_w_out"), ("mix_w_in", "mix_w_out"), ("ffn2_w_in", "ffn2_w_out"))
    lg = [[row(small_p["ln_g"][l, j]) for j in range(N_SUB)] for l in range(DEPTH)]
    lb = [[row(small_p["ln_b"][l, j]) for j in range(N_SUB)] for l in range(DEPTH)]
    sps, wps = [], []
    for l in range(DEPTH):
        w = wfull[l]
        sp = dict(
            conv_w=small_p["ssd_conv_w"][l], conv_b=row(small_p["ssd_conv_b"][l]),
            dtb_row=_pad_lanes(small_p["ssd_dt_bias"][l], PAD_DT0), alog_row=_pad_lanes(small_p["ssd_a_log"][l], 0),
            d_row=_pad_lanes(small_p["ssd_d"][l], 0), ng=row(small_p["ssd_norm_g"][l]),
            fbias_row=_pad_lanes(small_p["fox_f_bias"][l], PAD_F0), sconv_w=small_p["sconv_w"][l])
        sp["dtb_col"] = jnp.transpose(sp["dtb_row"])
        sp["alog_col"] = jnp.transpose(sp["alog_row"])
        sps.append(sp)
        wps.append(_perm_mix_w_in(w["mix_w_in"]))
    tags = [f"l{l}{('f1', 'mx', 'f2')[j]}" for l, j in subs]

    x0, h = _ln_in_fwd(x, row(small_p["ln_in_g"]), row(small_p["ln_in_b"]), mod[0, 0])
    cur = x0
    xins, ys, inner = [], [], []
    for idx, (l, j) in enumerate(subs):
        w = wfull[l]
        if j == 1:
            y, sv = _mix_fwd(h, wps[l], w["mix_w_out"], sps[l], tags[idx])
        else:
            y, sv = _ffn_fwd(h, w[w_names[j][0]], w[w_names[j][1]], tags[idx])
        nxt = mod[subs[idx + 1]] if idx + 1 < len(subs) else None
        xins.append(cur)
        ys.append(y)
        inner.append(sv)
        cur, h = _res_ln(cur, y, mod[l, j], lg[l][j], lb[l][j], factor[j], f"res_ln_{tags[idx]}", nxt)

    dcur, loss_acc = _loss_head(cur, tgt)
    last = len(subs) - 1
    l, j = subs[last]
    dres, dy, acc = _res_ln_bwd(dcur, xins[last], ys[last], mod[l, j], lg[l][j], factor[j], f"res_ln_bwd_{tags[last]}")
    ln_acc = {last: acc}
    shift_scale = {}
    big_grads = [dict() for _ in range(DEPTH)]
    small_g = [None] * DEPTH
    for idx in reversed(range(len(subs))):
        l, j = subs[idx]
        w = wfull[l]
        if j == 1:
            dh, g_in, g_out, small_g[l] = _mix_bwd(dy, inner[idx], wps[l], w["mix_w_out"], sps[l], tags[idx])
        else:
            dh, g_in, g_out = _ffn_bwd(dy, inner[idx], w[w_names[j][0]], w[w_names[j][1]], tags[idx])
        big_grads[l][w_names[j][0]], big_grads[l][w_names[j][1]] = g_in, g_out
        if j == 0 and on_layer_grads is not None:
            on_layer_grads(l, big_grads[l])
        if (l, j) == (0, N_SUB - 1) and on_layer0_started is not None:
            on_layer0_started()
        if idx > 0:
            pl_, pj = subs[idx - 1]
            dres, dy, acc5 = _mod_res_bwd(dres, dh, mod[l, j], xins[idx - 1], ys[idx - 1], mod[pl_, pj], lg[pl_][pj],
                                          lb[pl_][pj], factor[pj], f"mod_res_bwd_{tags[idx]}")
            shift_scale[idx], ln_acc[idx - 1] = acc5[0:2], acc5[2:5]
        else:
            dx0, shift_scale[0] = _modulate_bwd(dres, dh, x0, mod[0, 0], "mod_bwd_first")
    dx, acc_in = _ln_in_bwd(dx0, x, row(small_p["ln_in_g"]))
    dmod = []
    for l in range(DEPTH):
        ids = [N_SUB * l + j for j in range(N_SUB)]
        small_g[l]["ln_g"] = jnp.stack([ln_acc[i][0] for i in ids])
        small_g[l]["ln_b"] = jnp.stack([ln_acc[i][1] for i in ids])
        dmod.append(jnp.stack([jnp.concatenate([shift_scale[i], ln_acc[i][2:3]], axis=0) for i in ids]))
    return loss_acc[0, 0], dx, big_grads, small_g, jnp.stack(dmod), acc_in


SMALL_ORDER = ("ssd_conv_w", "ssd_conv_b", "ssd_dt_bias", "ssd_a_log", "ssd_d", "ssd_norm_g", "fox_f_bias",
               "sconv_w", "ln_g", "ln_b")
SMALL_KEY = dict(ssd_conv_w="conv_w", ssd_conv_b="conv_b", ssd_dt_bias="dt_bias", ssd_a_log="a_log", ssd_d="d",
                 ssd_norm_g="norm_g", fox_f_bias="f_bias", sconv_w="sconv_w", ln_g="ln_g", ln_b="ln_b")
COL_SHARDED_SMALL = ("ssd_conv_w", "sconv_w", "ln_g", "ln_b")


def _pad_to(v, n):
    return jnp.concatenate([v, jnp.zeros((n - v.shape[0],), v.dtype)])


def kernel(x, c, ln_in_g, ln_in_b, ada_w, ada_b, ffn1_w_in, ffn1_w_out, mix_w_in, mix_w_out, ssd_conv_w, ssd_conv_b, ssd_dt_bias, ssd_a_log, ssd_d, ssd_norm_g, fox_f_bias, sconv_w, ffn2_w_in, ffn2_w_out, ln_g, ln_b, loss_target, m_ln_in_g, m_ln_in_b, m_ada_w, m_ada_b, m_ffn1_w_in, m_ffn1_w_out, m_mix_w_in, m_mix_w_out, m_ssd_conv_w, m_ssd_conv_b, m_ssd_dt_bias, m_ssd_a_log, m_ssd_d, m_ssd_norm_g, m_fox_f_bias, m_sconv_w, m_ffn2_w_in, m_ffn2_w_out, m_ln_g, m_ln_b, v_ln_in_g, v_ln_in_b, v_ada_w, v_ada_b, v_ffn1_w_in, v_ffn1_w_out, v_mix_w_in, v_mix_w_out, v_ssd_conv_w, v_ssd_conv_b, v_ssd_dt_bias, v_ssd_a_log, v_ssd_d, v_ssd_norm_g, v_fox_f_bias, v_sconv_w, v_ffn2_w_in, v_ffn2_w_out, v_ln_g, v_ln_b):
    names = ("ln_in_g", "ln_in_b", "ada_w", "ada_b", "ffn1_w_in", "ffn1_w_out", "mix_w_in", "mix_w_out",
             "ssd_conv_w", "ssd_conv_b", "ssd_dt_bias", "ssd_a_log", "ssd_d", "ssd_norm_g", "fox_f_bias", "sconv_w",
             "ffn2_w_in", "ffn2_w_out", "ln_g", "ln_b")
    w_loc = dict(zip(names, (ln_in_g, ln_in_b, ada_w, ada_b, ffn1_w_in, ffn1_w_out, mix_w_in, mix_w_out, ssd_conv_w,
                             ssd_conv_b, ssd_dt_bias, ssd_a_log, ssd_d, ssd_norm_g, fox_f_bias, sconv_w, ffn2_w_in,
                             ffn2_w_out, ln_g, ln_b)))
    m_loc = dict(zip(names, (m_ln_in_g, m_ln_in_b, m_ada_w, m_ada_b, m_ffn1_w_in, m_ffn1_w_out, m_mix_w_in,
                             m_mix_w_out, m_ssd_conv_w, m_ssd_conv_b, m_ssd_dt_bias, m_ssd_a_log, m_ssd_d,
                             m_ssd_norm_g, m_fox_f_bias, m_sconv_w, m_ffn2_w_in, m_ffn2_w_out, m_ln_g, m_ln_b)))
    v_loc = dict(zip(names, (v_ln_in_g, v_ln_in_b, v_ada_w, v_ada_b, v_ffn1_w_in, v_ffn1_w_out, v_mix_w_in,
                             v_mix_w_out, v_ssd_conv_w, v_ssd_conv_b, v_ssd_dt_bias, v_ssd_a_log, v_ssd_d,
                             v_ssd_norm_g, v_fox_f_bias, v_sconv_w, v_ffn2_w_in, v_ffn2_w_out, v_ln_g, v_ln_b)))

    xi, yi, ci = lax.axis_index("x"), lax.axis_index("y"), lax.axis_index("c")
    me = 2 * xi + yi
    dev = 2 * me + ci

    def gather8(v, tag):
        v2 = v.reshape(1, -1)
        got = _bcast_chips([v2], f"gather_chips_{tag}")
        same_c = jnp.concatenate(_by_chip(me, v2, [g[0] for g in got]), axis=0)
        other_c = _swap_sibling([same_c], f"gather_sibling_{tag}")[0]
        pair = lax.switch(ci, [lambda a, b: jnp.stack([a, b], axis=1), lambda a, b: jnp.stack([b, a], axis=1)],
                          same_c, other_c)
        return pair.reshape(8, -1)

    def chip_concat(own, got3, axis):
        return jnp.concatenate(_by_chip(me, own, got3), axis=axis)

    small_cols = [w_loc[n].reshape(-1, w_loc[n].shape[-1]) for n in COL_SHARDED_SMALL]
    got = _bcast_chips(small_cols, "gather_small_params")
    small_p = {n: w_loc[n] for n in ("ln_in_g", "ln_in_b", "ssd_conv_b", "ssd_dt_bias", "ssd_a_log", "ssd_d",
                                     "ssd_norm_g", "fox_f_bias")}
    for i, n in enumerate(COL_SHARDED_SMALL):
        full = chip_concat(small_cols[i], [g[i] for g in got], 1)
        small_p[n] = full.reshape(w_loc[n].shape[:-1] + (full.shape[-1],))

    shards = [w_loc[n].astype(BF16) for n in BIG]
    layer0 = _gather_layer(shards, 0, "gather_weights_l0")
    layer1 = _gather_layer_sequencer(shards, 1, layer0[BIG.index("mix_w_out")], "gather_weights_l1")
    gathered = [dict(zip(BIG, layer0)), dict(zip(BIG, layer1))]
    wfull = []
    for l in range(DEPTH):
        g = gathered[l]
        wl = {n: g[n] if n in BY_COLUMNS else g[n].reshape(-1, g[n].shape[-1]) for n in BIG if n != "mix_w_in"}
        wl["mix_w_in"] = jnp.concatenate([g["mix_w_in"][s] for s in range(4)], axis=1)
        wfull.append(wl)

    c_all = gather8(c[0], "c")
    c_act = _silu_bf16(c_all)
    ada_w_b = ada_w.astype(BF16)
    mod_loc = [_matmul(c_act, ada_w_b[l], "nn", F32, f"ada_fwd_l{l}") for l in range(DEPTH)]
    mod_loc = jnp.stack(mod_loc)
    got = _bcast_chips([mod_loc], "gather_mod")
    mod_all = chip_concat(mod_loc, [g[0] for g in got], 2)
    mod = lax.dynamic_index_in_dim(mod_all, dev, 1, keepdims=False) + ada_b
    mod = mod.reshape(DEPTH, N_SUB, 3, D)

    ci_arr = jnp.reshape(ci, (1,)).astype(jnp.int32)
    me_arr = jnp.reshape(me, (1,)).astype(jnp.int32)

    def wire_form(n, g):
        if n in BY_COLUMNS:
            return g
        if n == "mix_w_in":
            return jnp.transpose(g.reshape(g.shape[0], 4, g.shape[1] // 4), (1, 0, 2))
        return g.reshape(4, g.shape[0] // 4, g.shape[1])

    reduce_state = {}

    def on_layer_grads(l, g):
        wire = [wire_form(n, g[n]) for n in BIG]
        reduce_state[l] = dict(wire=wire)
        if l == DEPTH - 1:
            reduce_state[l]["got"] = _grad_swap_halves(wire, f"grad_swap_halves_l{l}", True)

    def add_and_scatter(l, sequencer):
        st = reduce_state[l]
        st["part"] = [_grad_add_halves(st["wire"][i], st["got"][i], ci_arr, f"grad_add_halves_l{l}_{n}")
                      for i, n in enumerate(BIG)]
        st["from_chips"] = _grad_scatter_halves(st["part"], f"grad_scatter_halves_l{l}", sequencer)

    def finish_reduce(l, after, sequencer):
        st = reduce_state[l]
        halves = [_grad_add_slots(st["part"][i], [g[i] for g in st["from_chips"]], n, me_arr, after,
                                  f"grad_add_slots_l{l}_{n}") for i, n in enumerate(BIG)]
        return halves, _grad_swap_reduced(halves, f"grad_swap_reduced_l{l}", sequencer)

    loss_part, dx, big_g, small_g, dmod, acc_in = _local_step(
        x[0], loss_target[0], mod, wfull, small_p, on_layer_grads, lambda: add_and_scatter(DEPTH - 1, True))
    loss = lax.psum(loss_part, ("x", "y", "c"))

    pieces = [dmod.reshape(-1), acc_in[0], acc_in[1]]
    for n in SMALL_ORDER:
        pieces.append(jnp.stack([small_g[l][SMALL_KEY[n]] for l in range(DEPTH)]).reshape(-1))
    sizes = [p.shape[0] for p in pieces]
    total = sum(sizes)
    padded = -(-total // 1024) * 1024
    vec = _pad_to(jnp.concatenate(pieces), padded)
    all_rows = gather8(vec, "small_grads")
    summed = _sum_rows(all_rows, "sum_small_grads")[0]
    offs = [0]
    for s in sizes:
        offs.append(offs[-1] + s)
    n_mod = sizes[0]
    grads = {"ada_b": summed[0:n_mod].reshape(DEPTH, 3 * N_SUB * D),
             "ln_in_g": summed[offs[1]:offs[2]], "ln_in_b": summed[offs[2]:offs[3]]}
    for i, n in enumerate(SMALL_ORDER):
        full = summed[offs[3 + i]:offs[4 + i]].reshape(small_p[n].shape)
        if n in COL_SHARDED_SMALL:
            wcols = w_loc[n].shape[-1]
            full = lax.dynamic_slice_in_dim(full, me * wcols, wcols, axis=full.ndim - 1)
        grads[n] = full

    dmod_all = all_rows[:, 0:n_mod].reshape(8, DEPTH, 3 * N_SUB * D)
    ncol = ada_w.shape[-1]
    dmod_cols = lax.dynamic_slice_in_dim(dmod_all, me * ncol, ncol, axis=2).astype(BF16)
    grads["ada_w"] = jnp.stack([_matmul(c_act, dmod_cols[:, l], "tn", F32, f"ada_bwd_l{l}") for l in range(DEPTH)])

    reduced = [None] * DEPTH
    reduced[DEPTH - 1] = finish_reduce(DEPTH - 1, dx, True)
    for l in range(DEPTH - 1):
        reduce_state[l]["got"] = _grad_swap_halves(reduce_state[l]["wire"], f"grad_swap_halves_l{l}", False)
        add_and_scatter(l, False)
        reduced[l] = finish_reduce(l, dx, False)

    delta, new_m, new_v = {}, {}, {}
    delta["ada_w"], new_m["ada_w"], new_v["ada_w"] = _adamw(w_loc["ada_w"], grads["ada_w"], m_loc["ada_w"],
                                                           v_loc["ada_w"], "adamw_ada_w")
    for i, n in enumerate(BIG):
        grads[n], delta[n], new_m[n], new_v[n] = _adamw_layers(
            w_loc[n], [reduced[l][0][i] for l in range(DEPTH)], [reduced[l][1][i] for l in range(DEPTH)],
            m_loc[n], v_loc[n], ci_arr, f"adamw_{n}")
    small_names = [n for n in names if n not in ("ada_w",) + BIG]
    flat = lambda d: jnp.concatenate([d[n].reshape(-1) for n in small_names])
    n_small = sum(w_loc[n].size for n in small_names)
    n_pad = -(-n_small // 1024) * 1024
    packed = [_pad_to(flat(d), n_pad).reshape(-1, LANES) for d in (w_loc, grads, m_loc, v_loc)]
    d_s, m_s, v_s = _adamw(*packed, "adamw_small")
    off = 0
    for n in small_names:
        sz = w_loc[n].size
        delta[n] = d_s.reshape(-1)[off:off + sz].reshape(w_loc[n].shape)
        new_m[n] = m_s.reshape(-1)[off:off + sz].reshape(w_loc[n].shape)
        new_v[n] = v_s.reshape(-1)[off:off + sz].reshape(w_loc[n].shape)
        off += sz

    return (loss, dx[None], *[grads[n] for n in names], *[delta[n] for n in names],
            *[new_m[n] for n in names], *[new_v[n] for n in names])
```

```python
import functools

import jax
import jax.numpy as jnp
from jax import lax
from jax.experimental import pallas as pl
from jax.experimental.pallas import tpu as pltpu
from jax.experimental.pallas import tpu_sc as plsc

F32 = jnp.float32
BF16 = jnp.bfloat16
MESH = pl.DeviceIdType.MESH

D = 1024
DEPTH = 2
N_SUB = 3
D_FF = 2816
FF_TILE = D_FF // 2
ALPHA = (2 * DEPTH) ** 0.25
LN_EPS = 1e-5
RMS_EPS = 1e-5
SSD_W = 512
SSD_HEADS = 8
SSD_CONV_K = 4
SSD_CONV_DIM = 1024
FOX_W = 256
FOX_HEADS = 4
FOX_HD = 64
SC_W = 256
SC_K = 3
D_IN_PROJ = 3084
P_Z, P_XBC, P_QKV, P_SC, P_PAD = 0, 512, 1536, 2304, 3072
D_PROJ_PAD = 3200
PAD_DT0, PAD_F0 = 0, 8
SSD_CHUNK = 256
ATT_BLOCK = 512
CUM_BLOCK = 256
LANES = 128
VMEM_LIMIT = 56 * 1024 * 1024

ADAM_LR, ADAM_B1, ADAM_B2, ADAM_EPS, ADAM_WD, ADAM_STEP = 0.001, 0.9, 0.999, 1e-08, 0.01, 10
NEG = -1e30


def _cp(*sem):
    return pltpu.CompilerParams(dimension_semantics=sem, vmem_limit_bytes=VMEM_LIMIT)


def _pick(n, cands):
    for c in cands:
        if n % c == 0:
            return c
    return n


def _dot(a, b):
    return lax.dot_general(a, b, (((1,), (0,)), ((), ())), preferred_element_type=F32)


def _dot_nt(a, b):
    return lax.dot_general(a, b, (((1,), (1,)), ((), ())), preferred_element_type=F32)


def _dot_tn(a, b):
    return lax.dot_general(a, b, (((0,), (0,)), ((), ())), preferred_element_type=F32)


def _sigmoid(x):
    return 1.0 / (1.0 + jnp.exp(-x))


def _softplus(x):
    return jnp.maximum(x, 0.0) + jnp.log(1.0 + jnp.exp(-jnp.abs(x)))


def _split3(v):
    h = v.astype(BF16)
    r = v - h.astype(F32)
    m = r.astype(BF16)
    l = (r - m.astype(F32)).astype(BF16)
    return h, m, l


def _tri_left(tri, v):
    h, m, l = _split3(v)
    return _dot(tri, h) + _dot(tri, m) + _dot(tri, l)


def _tri_right(v, tri):
    h, m, l = _split3(v)
    return _dot(h, tri) + _dot(m, tri) + _dot(l, tri)


def _matmul(a, b, mode, out_dtype, name, halves=None):
    assert a.dtype == BF16 and b.dtype == BF16, (name, a.dtype, b.dtype)
    if halves == "a":
        assert mode == "nt" and a.shape[0] == 2 and a.shape[2] == D_FF
        m, k, n = a.shape[1], 2 * D_FF, b.shape[0]
    elif halves == "b":
        assert mode == "tn" and b.shape[0] == 2 and b.shape[2] == D_FF
        (k, m), n = a.shape, 2 * D_FF
    elif mode == "nn":
        (m, k), n = a.shape, b.shape[1]
    elif mode == "nt":
        (m, k), n = a.shape, b.shape[0]
    else:
        (k, m), n = a.shape, b.shape[1]
    tm = m if m <= 1024 else _pick(m, (1024, 1408, 512, 256, 128))
    tn = n if n <= 1024 else _pick(n, (1408, 640, 512, 256, 128))
    tk = k if k <= 1024 else _pick(k, (1408, 1024, 640, 512, 256, 128))
    nk = k // tk
    if mode == "nn":
        dn = (((1,), (0,)), ((), ()))
        a_spec = pl.BlockSpec((tm, tk), lambda i, j, kk: (i, kk))
        b_spec = pl.BlockSpec((tk, tn), lambda i, j, kk: (kk, j))
    elif mode == "nt":
        dn = (((1,), (1,)), ((), ()))
        a_spec = pl.BlockSpec((tm, tk), lambda i, j, kk: (i, kk))
        b_spec = pl.BlockSpec((tn, tk), lambda i, j, kk: (j, kk))
    else:
        dn = (((0,), (0,)), ((), ()))
        a_spec = pl.BlockSpec((tk, tm), lambda i, j, kk: (kk, i))
        b_spec = pl.BlockSpec((tk, tn), lambda i, j, kk: (kk, j))
    per_half = D_FF // FF_TILE
    if halves == "a":
        assert tk == FF_TILE
        a_spec = pl.BlockSpec((None, tm, tk), lambda i, j, kk: (kk // per_half, i, kk % per_half))
    elif halves == "b":
        assert tn == FF_TILE
        b_spec = pl.BlockSpec((None, tk, tn), lambda i, j, kk: (j // per_half, kk, j % per_half))

    def body(a_ref, b_ref, o_ref, *acc):
        d = lax.dot_general(a_ref[...], b_ref[...], dn, preferred_element_type=F32)
        if nk == 1:
            o_ref[...] = d.astype(o_ref.dtype)
            return
        acc_ref, = acc
        kk = pl.program_id(2)

        @pl.when(kk == 0)
        def _():
            acc_ref[...] = d

        @pl.when((kk > 0) & (kk < nk - 1))
        def _():
            acc_ref[...] += d

        @pl.when(kk == nk - 1)
        def _():
            o_ref[...] = (acc_ref[...] + d).astype(o_ref.dtype)

    return pl.pallas_call(
        body, name=name, grid=(m // tm, n // tn, nk),
        in_specs=[a_spec, b_spec],
        out_specs=pl.BlockSpec((tm, tn), lambda i, j, kk: (i, j)),
        out_shape=jax.ShapeDtypeStruct((m, n), out_dtype),
        scratch_shapes=[pltpu.VMEM((tm, tn), F32)] if nk > 1 else [],
        compiler_params=_cp("parallel", "parallel", "arbitrary"),
    )(a, b)


def _rows(body, name, t, tb, row_in, full_in, row_out, acc_out):
    in_specs, args = [], []
    for r in row_in:
        if isinstance(r, tuple):
            arr, w, j = r
            in_specs.append(pl.BlockSpec((tb, w), functools.partial(lambda i, jj: (i, jj), jj=j)))
            args.append(arr)
        else:
            in_specs.append(pl.BlockSpec((tb, r.shape[1]), lambda i: (i, 0)))
            args.append(r)
    for f in full_in:
        in_specs.append(pl.BlockSpec(f.shape, functools.partial(lambda i, nd: (0,) * nd, nd=f.ndim)))
        args.append(f)
    out_specs = [pl.BlockSpec((tb, c), lambda i: (i, 0)) for c, _ in row_out]
    out_specs += [pl.BlockSpec(s, functools.partial(lambda i, nd: (0,) * nd, nd=len(s))) for s in acc_out]
    out_shape = [jax.ShapeDtypeStruct((t, c), dt) for c, dt in row_out]
    out_shape += [jax.ShapeDtypeStruct(s, F32) for s in acc_out]
    return pl.pallas_call(
        body, name=name, grid=(t // tb,), in_specs=in_specs, out_specs=out_specs, out_shape=out_shape,
        compiler_params=_cp("arbitrary"),
    )(*args)


def _ln_stats(r):
    mu = jnp.mean(r, axis=-1, keepdims=True)
    xc = r - mu
    var = jnp.mean(xc * xc, axis=-1, keepdims=True)
    rstd = lax.rsqrt(var + LN_EPS)
    return xc * rstd, rstd


def _ln_bwd(dout, xhat, rstd, g):
    dxh = dout * g
    m1 = jnp.mean(dxh, axis=-1, keepdims=True)
    m2 = jnp.mean(dxh * xhat, axis=-1, keepdims=True)
    return rstd * (dxh - m1 - xhat * m2)


def _first(i, acc_refs):
    @pl.when(i == 0)
    def _():
        for a in acc_refs:
            a[...] = jnp.zeros_like(a)


def _modulated(xv, m_ref):
    return (xv * (1.0 + m_ref[1:2, :]) + m_ref[0:1, :]).astype(BF16)


def _ln_in_fwd(x, g, b, next_mod3):
    t = x.shape[0]

    def body(x_ref, g_ref, b_ref, m_ref, o_ref, h_ref):
        xhat, _ = _ln_stats(x_ref[...])
        out = xhat * g_ref[...] + b_ref[...]
        o_ref[...] = out
        h_ref[...] = _modulated(out, m_ref)

    return _rows(body, "ln_in_fwd", t, 256, [x], [g, b, next_mod3], [(D, F32), (D, BF16)], [])


def _ln_in_bwd(dx0, x, g):
    t = x.shape[0]

    def body(d_ref, x_ref, g_ref, o_ref, acc_ref):
        _first(pl.program_id(0), [acc_ref])
        xhat, rstd = _ln_stats(x_ref[...])
        d = d_ref[...]
        o_ref[...] = _ln_bwd(d, xhat, rstd, g_ref[...])
        acc_ref[0:1, :] += jnp.sum(d * xhat, axis=0, keepdims=True)
        acc_ref[1:2, :] += jnp.sum(d, axis=0, keepdims=True)

    return _rows(body, "ln_in_bwd", t, 256, [dx0, x], [g], [(D, F32)], [(2, D)])


def _modulate_bwd(dxres, dh, xin, mod3, name):
    t = xin.shape[0]

    def body(r_ref, dh_ref, x_ref, m_ref, o_ref, acc_ref):
        _first(pl.program_id(0), [acc_ref])
        dh_v = dh_ref[...]
        o_ref[...] = r_ref[...] + dh_v * (1.0 + m_ref[1:2, :])
        acc_ref[0:1, :] += jnp.sum(dh_v, axis=0, keepdims=True)
        acc_ref[1:2, :] += jnp.sum(dh_v * x_ref[...], axis=0, keepdims=True)

    return _rows(body, name, t, 256, [dxres, dh, xin], [mod3], [(D, F32)], [(2, D)])


def _ffn_in_swiglu(h, w_in, name):
    t = h.shape[0]
    tm = _pick(t, (512, 256, 128))
    nj = D_FF // FF_TILE

    def body(h_ref, wg_ref, wu_ref, u_ref, a_ref):
        hv = h_ref[...]
        gate = _dot(hv, wg_ref[...])
        up = _dot(hv, wu_ref[...])
        u_ref[0] = gate.astype(BF16)
        u_ref[1] = up.astype(BF16)
        a_ref[...] = (gate * _sigmoid(gate) * up).astype(BF16)

    return pl.pallas_call(
        body, name=name, grid=(nj, t // tm),
        in_specs=[pl.BlockSpec((tm, D), lambda j, i: (i, 0)),
                  pl.BlockSpec((D, FF_TILE), lambda j, i: (0, j)),
                  pl.BlockSpec((D, FF_TILE), lambda j, i: (0, nj + j))],
        out_specs=[pl.BlockSpec((2, tm, FF_TILE), lambda j, i: (0, i, j)),
                   pl.BlockSpec((tm, FF_TILE), lambda j, i: (i, j))],
        out_shape=[jax.ShapeDtypeStruct((2, t, D_FF), BF16), jax.ShapeDtypeStruct((t, D_FF), BF16)],
        compiler_params=_cp("parallel", "parallel"),
    )(h, w_in, w_in)


def _ffn_out_dx_swiglu(dy, w_out, u, name):
    t = dy.shape[0]
    tm = _pick(t, (512, 256, 128))
    nj = D_FF // FF_TILE

    def body(dy_ref, w_ref, u_ref, du_ref):
        da = _dot_nt(dy_ref[...], w_ref[...])
        gate = u_ref[0].astype(F32)
        up = u_ref[1].astype(F32)
        sg = _sigmoid(gate)
        du_ref[0] = (da * up * (sg * (1.0 + gate * (1.0 - sg)))).astype(BF16)
        du_ref[1] = (da * gate * sg).astype(BF16)

    blk3 = pl.BlockSpec((2, tm, FF_TILE), lambda j, i: (0, i, j))
    return pl.pallas_call(
        body, name=name, grid=(nj, t // tm),
        in_specs=[pl.BlockSpec((tm, D), lambda j, i: (i, 0)),
                  pl.BlockSpec((FF_TILE, D), lambda j, i: (j, 0)), blk3],
        out_specs=blk3,
        out_shape=jax.ShapeDtypeStruct((2, t, D_FF), BF16),
        compiler_params=_cp("parallel", "parallel"),
    )(dy, w_out, u)


def _res_ln(xin, y, mod3, lg, lb, factor, name, next_mod3=None):
    t = xin.shape[0]

    def body(x_ref, y_ref, m_ref, g_ref, b_ref, *rest):
        r = ALPHA * x_ref[...] + (factor * m_ref[2:3, :]) * y_ref[...]
        xhat, _ = _ln_stats(r)
        out = xhat * g_ref[...] + b_ref[...]
        if next_mod3 is None:
            rest[0][...] = out
        else:
            rest[1][...] = out
            rest[2][...] = _modulated(out, rest[0])

    if next_mod3 is None:
        return _rows(body, name, t, 256, [xin, y], [mod3, lg, lb], [(D, F32)], [])[0], None
    return _rows(body, name, t, 256, [xin, y], [mod3, lg, lb, next_mod3], [(D, F32), (D, BF16)], [])


def _mod_res_bwd(dxres, dh, mod3, xin_p, y_p, mod3_p, lg_p, lb_p, factor_p, name):
    t = dxres.shape[0]

    def body(r_ref, dh_ref, xp_ref, yp_ref, m_ref, mp_ref, g_ref, b_ref, dres_ref, dy_ref, acc_ref):
        _first(pl.program_id(0), [acc_ref])
        gate = factor_p * mp_ref[2:3, :]
        yv = yp_ref[...]
        xhat, rstd = _ln_stats(ALPHA * xp_ref[...] + gate * yv)
        xin = xhat * g_ref[...] + b_ref[...]
        dh_v = dh_ref[...]
        d = r_ref[...] + dh_v * (1.0 + m_ref[1:2, :])
        dr = _ln_bwd(d, xhat, rstd, g_ref[...])
        dres_ref[...] = ALPHA * dr
        dy_ref[...] = (gate * dr).astype(BF16)
        acc_ref[0:1, :] += jnp.sum(dh_v, axis=0, keepdims=True)
        acc_ref[1:2, :] += jnp.sum(dh_v * xin, axis=0, keepdims=True)
        acc_ref[2:3, :] += jnp.sum(d * xhat, axis=0, keepdims=True)
        acc_ref[3:4, :] += jnp.sum(d, axis=0, keepdims=True)
        acc_ref[4:5, :] += jnp.sum(factor_p * yv * dr, axis=0, keepdims=True)

    return _rows(body, name, t, 256, [dxres, dh, xin_p, y_p], [mod3, mod3_p, lg_p, lb_p],
                 [(D, F32), (D, BF16)], [(5, D)])


def _res_ln_bwd(dout, xin, y, mod3, lg, factor, name):
    t = xin.shape[0]

    def body(d_ref, x_ref, y_ref, m_ref, g_ref, dres_ref, dy_ref, acc_ref):
        _first(pl.program_id(0), [acc_ref])
        gate = factor * m_ref[2:3, :]
        yv = y_ref[...]
        r = ALPHA * x_ref[...] + gate * yv
        xhat, rstd = _ln_stats(r)
        d = d_ref[...]
        dr = _ln_bwd(d, xhat, rstd, g_ref[...])
        dres_ref[...] = ALPHA * dr
        dy_ref[...] = (gate * dr).astype(BF16)
        acc_ref[0:1, :] += jnp.sum(d * xhat, axis=0, keepdims=True)
        acc_ref[1:2, :] += jnp.sum(d, axis=0, keepdims=True)
        acc_ref[2:3, :] += jnp.sum(factor * yv * dr, axis=0, keepdims=True)

    return _rows(body, name, t, 256, [dout, xin, y], [mod3, lg], [(D, F32), (D, BF16)], [(3, D)])


def _loss_head(xf, tgt):
    t = xf.shape[0]

    def body(x_ref, t_ref, d_ref, acc_ref):
        _first(pl.program_id(0), [acc_ref])
        e = x_ref[...] - t_ref[...]
        d_ref[...] = e * (1.0 / D)
        part = 0.5 * jnp.sum(jnp.mean(e * e, axis=-1, keepdims=True), axis=0, keepdims=True)
        acc_ref[...] += jnp.broadcast_to(part, acc_ref.shape)

    return _rows(body, "loss_head", t, 256, [xf, tgt], [], [(D, F32)], [(1, LANES)])


def _silu_bf16(c_all):
    def body(c_ref, o_ref):
        v = c_ref[...]
        o_ref[...] = (v * _sigmoid(v)).astype(BF16)

    return _rows(body, "silu_c", c_all.shape[0], c_all.shape[0], [c_all], [], [(c_all.shape[1], BF16)], [])[0]


def _sum_rows(v, name):
    r, n = v.shape
    tn = _pick(n, (8192, 4096, 2048, 1024, 512, 256, 128))

    def body(v_ref, o_ref):
        acc = v_ref[0:1, :]
        for k in range(1, r):
            acc = acc + v_ref[k:k + 1, :]
        o_ref[...] = acc

    return pl.pallas_call(
        body, name=name, grid=(n // tn,),
        in_specs=[pl.BlockSpec((r, tn), lambda j: (0, j))],
        out_specs=pl.BlockSpec((1, tn), lambda j: (0, j)),
        out_shape=jax.ShapeDtypeStruct((1, n), F32),
        compiler_params=_cp("parallel"),
    )(v)


def _elementwise(fn, name, ins, out_dtypes):
    r, c = ins[0].shape
    tb = _pick(r, (128, 64, 32, 16, 8))
    n_in = len(ins)

    def body(*refs):
        outs = fn(*[x[...] for x in refs[:n_in]])
        for o_ref, o in zip(refs[n_in:], outs):
            o_ref[...] = o.astype(o_ref.dtype)

    spec = pl.BlockSpec((tb, c), lambda i: (i, 0))
    return pl.pallas_call(
        body, name=name, grid=(r // tb,), in_specs=[spec] * n_in, out_specs=[spec] * len(out_dtypes),
        out_shape=[jax.ShapeDtypeStruct((r, c), dt) for dt in out_dtypes],
        compiler_params=_cp("parallel"),
    )(*ins)


def _adamw_math(w, g, m, v):
    m = ADAM_B1 * m + (1.0 - ADAM_B1) * g
    v = ADAM_B2 * v + (1.0 - ADAM_B2) * (g * g)
    m_hat = m / (1.0 - ADAM_B1 ** ADAM_STEP)
    v_hat = v / (1.0 - ADAM_B2 ** ADAM_STEP)
    delta = -ADAM_LR * (m_hat / (jnp.sqrt(v_hat) + ADAM_EPS) + ADAM_WD * w)
    return delta, m, v


def _adamw(w, g, m, v, name):
    shape = w.shape
    c = shape[-1]
    w2, g2, m2, v2 = (a.reshape(-1, c) for a in (w, g, m, v))
    outs = _elementwise(_adamw_math, name, [w2, g2, m2, v2], [F32, F32, F32])
    return tuple(o.reshape(shape) for o in outs)


def _remote_exchange(ins, plan, peers_of, name):
    n_in, n_out = len(ins), len(plan)

    def body(*refs):
        in_refs, out_refs = refs[:n_in], refs[n_in:n_in + n_out]
        send_sems, recv_sems = refs[n_in + n_out], refs[n_in + n_out + 1]
        peers = peers_of(lax.axis_index("x"), lax.axis_index("y"), lax.axis_index("c"))
        copies = [
            pltpu.make_async_remote_copy(
                src_ref=in_refs[src], dst_ref=out_refs[k], send_sem=send_sems.at[k], recv_sem=recv_sems.at[k],
                device_id=peers[peer], device_id_type=MESH)
            for k, (peer, src) in enumerate(plan)
        ]
        for cp in copies:
            cp.start()
        for cp in copies:
            cp.wait()

    any_spec = pl.BlockSpec(memory_space=pl.ANY)
    return list(pl.pallas_call(
        body, name=name,
        in_specs=[any_spec] * n_in, out_specs=[any_spec] * n_out,
        out_shape=[jax.ShapeDtypeStruct(ins[src].shape, ins[src].dtype) for _, src in plan],
        scratch_shapes=[pltpu.SemaphoreType.DMA((n_out,)), pltpu.SemaphoreType.DMA((n_out,))],
    )(*ins))


def _sibling(x, y, c):
    return [(x, y, 1 - c)]


def _other_chips(x, y, c):
    return [(1 - x, y, c), (x, 1 - y, c), (1 - x, 1 - y, c)]


def _swap_sibling(arrs, name):
    return _remote_exchange(arrs, [(0, i) for i in range(len(arrs))], _sibling, name)


def _bcast_chips(arrs, name):
    n = len(arrs)
    out = _remote_exchange(arrs, [(p, i) for p in range(3) for i in range(n)], _other_chips, name)
    return [out[p * n:(p + 1) * n] for p in range(3)]


def _by_chip(me, own, got3):
    by_rel = [own, got3[0], got3[1], got3[2]]
    rel_bits = (0, 2, 1, 3)

    def branch(m):
        def f(ops):
            return [ops[rel_bits.index(i ^ m)] for i in range(4)]
        return f

    return lax.switch(me, [branch(m) for m in range(4)], by_rel)


BIG = ("ffn1_w_in", "ffn1_w_out", "mix_w_in", "mix_w_out", "ffn2_w_in", "ffn2_w_out")
BY_COLUMNS = ("ffn1_w_in", "ffn2_w_in")


def _layer_shape(n, shard_shape):
    r, cs = shard_shape
    return (r, 4 * cs) if n in BY_COLUMNS else (4, r, cs)


def _chip_ids(x, y):
    chips = [(1 - x, y), (x, 1 - y), (1 - x, 1 - y)]
    return chips, [2 * cx + cy for cx, cy in chips]


def _half_slot(ref, n, chip, h):
    if n in BY_COLUMNS:
        hr, w = ref.shape[0] // 2, ref.shape[1] // 4
        return ref.at[pl.ds(pl.multiple_of(h * hr, 16), hr), pl.ds(pl.multiple_of(chip * w, LANES), w)]
    hr = ref.shape[1] // 2
    return ref.at[chip, pl.ds(pl.multiple_of(h * hr, 16), hr)]


def _gather_body(names, layer, in_refs, out_refs, sems, handshake):
    send_chip, recv_chip, send_sib, recv_sib, local, send_fwd, recv_fwd = sems
    n_w = len(names)
    x, y, c = lax.axis_index("x"), lax.axis_index("y"), lax.axis_index("c")
    me = 2 * x + y
    chips, chip_idx = _chip_ids(x, y)
    sibling = (x, y, 1 - c)
    if handshake:
        _shake_hands([sibling] + [(*ch, c) for ch in chips])
    remote = _remote

    sends, own_copies = [], []
    for i, n in enumerate(names):
        shard = in_refs[i].at[layer]
        hr = shard.shape[0] // 2
        src = shard.at[pl.ds(pl.multiple_of(c * hr, 16), hr)]
        mine = _half_slot(out_refs[i], n, me, c)
        own = pltpu.make_async_copy(src, mine, local.at[i])
        own.start()
        own_copies.append(own)
        sends.append(remote(src, mine, send_sib.at[i], recv_sib.at[i], sibling))
        sends[-1].start()
        for p in range(3):
            k = p * n_w + i
            sends.append(remote(src, mine, send_chip.at[k], recv_chip.at[k], (*chips[p], c)))
            sends[-1].start()
    for p in range(3):
        for i, n in enumerate(names):
            k = p * n_w + i
            landed = _half_slot(out_refs[i], n, chip_idx[p], c)
            remote(landed, landed, send_chip.at[k], recv_chip.at[k], sibling).wait_recv()
            sends.append(remote(landed, landed, send_fwd.at[k], recv_fwd.at[k], sibling))
            sends[-1].start()
    for i, n in enumerate(names):
        theirs = _half_slot(out_refs[i], n, me, 1 - c)
        remote(theirs, theirs, send_sib.at[i], recv_sib.at[i], sibling).wait_recv()
        for p in range(3):
            k = p * n_w + i
            theirs = _half_slot(out_refs[i], n, chip_idx[p], 1 - c)
            remote(theirs, theirs, send_fwd.at[k], recv_fwd.at[k], sibling).wait_recv()
    for own in own_copies:
        own.wait()
    for cp in sends:
        cp.wait_send()


def _gather_weights(names, shards, layer, after, name, sequencer_id):
    n_w = len(names)

    def body(in_refs, out_refs, sems, handshake):
        _gather_body(names, layer, in_refs, out_refs, sems, handshake)

    dma = pltpu.SemaphoreType.DMA
    sems = [dma((3 * n_w,)), dma((3 * n_w,)), dma((n_w,)), dma((n_w,)), dma((n_w,)), dma((3 * n_w,)), dma((3 * n_w,))]
    shapes = [jax.ShapeDtypeStruct(_layer_shape(n, s.shape[1:]), BF16) for n, s in zip(names, shards)]
    return _comm_call(body, list(shards) + list(after), shapes, sems, name, sequencer_id)


def _comm_call(body_fn, ins, out_shapes, sem_types, name, sequencer_id):
    n_in, n_out = len(ins), len(out_shapes)
    sequencer = sequencer_id is not None

    def body(*refs):
        body_fn(refs[:n_in], refs[n_in:n_in + n_out], refs[n_in + n_out:], sequencer)

    if sequencer:
        return list(pl.kernel(
            body, name=name, out_type=out_shapes,
            mesh=plsc.ScalarSubcoreMesh(axis_name="sequencer", num_cores=1), scratch_types=sem_types,
            compiler_params=pltpu.CompilerParams(collective_id=sequencer_id),
        )(*ins))
    any_spec = pl.BlockSpec(memory_space=pl.ANY)
    return list(pl.pallas_call(
        body, name=name, in_specs=[any_spec] * n_in, out_specs=[any_spec] * n_out, out_shape=out_shapes,
        scratch_shapes=sem_types,
    )(*ins))


def _shake_hands(peers):
    barrier = pltpu.get_barrier_semaphore()
    for peer in peers:
        pl.semaphore_signal(barrier, inc=1, device_id=peer, device_id_type=MESH)
    pl.semaphore_wait(barrier, len(peers))


def _remote(src, dst, s_sem, r_sem, to):
    return pltpu.make_async_remote_copy(src_ref=src, dst_ref=dst, send_sem=s_sem, recv_sem=r_sem,
                                        device_id=to, device_id_type=MESH)


def _rows_half(ref, n, h):
    if n in BY_COLUMNS:
        hr = ref.shape[0] // 2
        return ref.at[pl.ds(pl.multiple_of(h * hr, 16), hr)]
    hr = ref.shape[1] // 2
    return ref.at[:, pl.ds(pl.multiple_of(h * hr, 16), hr)]


def _half_form_shape(n, wire_shape):
    if n in BY_COLUMNS:
        return (wire_shape[0] // 2, wire_shape[1])
    return (wire_shape[0], wire_shape[1] // 2, wire_shape[2])


def _grad_swap_halves(names, wire, name, sequencer_id):
    n_w = len(names)

    def body(in_refs, out_refs, sems, handshake):
        send, recv = sems
        x, y, c = lax.axis_index("x"), lax.axis_index("y"), lax.axis_index("c")
        sibling = (x, y, 1 - c)
        if handshake:
            _shake_hands([sibling])
        cps = [_remote(_rows_half(in_refs[i], n, 1 - c), out_refs[i], send.at[i], recv.at[i], sibling)
               for i, n in enumerate(names)]
        for cp in cps:
            cp.start()
        for cp in cps:
            cp.wait()

    shapes = [jax.ShapeDtypeStruct(_half_form_shape(n, w.shape), w.dtype) for n, w in zip(names, wire)]
    dma = pltpu.SemaphoreType.DMA
    return _comm_call(body, list(wire), shapes, [dma((n_w,)), dma((n_w,))], name, sequencer_id)


def _grad_add_halves(g, got, ci_arr, after, name):
    g3 = g if g.ndim == 3 else g[None]
    r3 = got if got.ndim == 3 else got[None]
    s, hr, cc = r3.shape
    tb = _pick(hr, (256, 176, 128))
    nb = hr // tb

    def body(s_ref, g_ref, r_ref, after_ref, o_ref):
        o_ref[...] = (g_ref[...].astype(F32) + r_ref[...].astype(F32)).astype(BF16)

    out = pl.pallas_call(
        body, name=name,
        grid_spec=pltpu.PrefetchScalarGridSpec(
            num_scalar_prefetch=1, grid=(nb,),
            in_specs=[pl.BlockSpec((s, tb, cc), lambda i, sc: (0, sc[0] * nb + i, 0)),
                      pl.BlockSpec((s, tb, cc), lambda i, sc: (0, i, 0)),
                      pl.BlockSpec(memory_space=pl.ANY)],
            out_specs=pl.BlockSpec((s, tb, cc), lambda i, sc: (0, i, 0))),
        out_shape=jax.ShapeDtypeStruct(r3.shape, BF16),
        compiler_params=_cp("arbitrary"),
    )(ci_arr, g3, r3, after)
    return out.reshape(got.shape)


def _grad_scatter_halves(names, parts, name, sequencer_id):
    n_w = len(names)

    def slot(ref, n, chip):
        if n in BY_COLUMNS:
            w = ref.shape[1] // 4
            return ref.at[:, pl.ds(pl.multiple_of(chip * w, LANES), w)]
        return ref.at[chip]

    def body(in_refs, out_refs, sems, handshake):
        send, recv = sems
        x, y, c = lax.axis_index("x"), lax.axis_index("y"), lax.axis_index("c")
        chips, chip_idx = _chip_ids(x, y)
        if handshake:
            _shake_hands([(*ch, c) for ch in chips])
        cps = []
        for p in range(3):
            for i, n in enumerate(names):
                k = p * n_w + i
                cps.append(_remote(slot(in_refs[i], n, chip_idx[p]), out_refs[k], send.at[k], recv.at[k],
                                   (*chips[p], c)))
        for cp in cps:
            cp.start()
        for cp in cps:
            cp.wait()

    def slot_shape(n, a):
        return (a.shape[0], a.shape[1] // 4) if n in BY_COLUMNS else a.shape[1:]

    shapes = [jax.ShapeDtypeStruct(slot_shape(n, a), BF16) for _ in range(3) for n, a in zip(names, parts)]
    dma = pltpu.SemaphoreType.DMA
    out = _comm_call(body, list(parts), shapes, [dma((3 * n_w,)), dma((3 * n_w,))], name, sequencer_id)
    return [out[p * n_w:(p + 1) * n_w] for p in range(3)]


def _grad_add_slots(part, got3, n, me_arr, after, name):
    hr, cs = got3[0].shape
    tb = _pick(hr, (256, 176, 128))

    def body(s_ref, own_ref, a_ref, b_ref, c_ref, after_ref, out_ref):
        acc = own_ref[...].astype(F32) + a_ref[...].astype(F32)
        out_ref[...] = (acc + b_ref[...].astype(F32)) + c_ref[...].astype(F32)

    if n in BY_COLUMNS:
        own_spec = pl.BlockSpec((tb, cs), lambda i, s: (i, s[0]))
    else:
        own_spec = pl.BlockSpec((None, tb, cs), lambda i, s: (s[0], i, 0))
    plain = pl.BlockSpec((tb, cs), lambda i, s: (i, 0))
    return pl.pallas_call(
        body, name=name,
        grid_spec=pltpu.PrefetchScalarGridSpec(
            num_scalar_prefetch=1, grid=(hr // tb,),
            in_specs=[own_spec, plain, plain, plain, pl.BlockSpec(memory_space=pl.ANY)],
            out_specs=plain),
        out_shape=jax.ShapeDtypeStruct((hr, cs), F32),
        compiler_params=_cp("arbitrary"),
    )(me_arr, part, *got3, after)


def _grad_swap_reduced(halves, name, sequencer_id):
    n_w = len(halves)

    def body(in_refs, out_refs, sems, handshake):
        send, recv = sems
        x, y, c = lax.axis_index("x"), lax.axis_index("y"), lax.axis_index("c")
        sibling = (x, y, 1 - c)
        if handshake:
            _shake_hands([sibling])
        cps = [_remote(in_refs[i], out_refs[i], send.at[i], recv.at[i], sibling) for i in range(n_w)]
        for cp in cps:
            cp.start()
        for cp in cps:
            cp.wait()

    shapes = [jax.ShapeDtypeStruct(h.shape, F32) for h in halves]
    dma = pltpu.SemaphoreType.DMA
    return _comm_call(body, list(halves), shapes, [dma((n_w,)), dma((n_w,))], name, sequencer_id)


def _adamw_layers(w, own, other, m, v, ci_arr, name):
    _, rs, cs = w.shape
    hr = rs // 2
    tb = _pick(hr, (256, 176, 128))
    nbh = hr // tb

    def body(s_ref, w_ref, o0, t0, o1, t1, m_ref, v_ref, g_out, d_out, m_out, v_out):
        mine = (pl.program_id(1) // nbh) == s_ref[0]
        g = jnp.where(pl.program_id(0) == 0, jnp.where(mine, o0[...], t0[...]), jnp.where(mine, o1[...], t1[...]))
        g_out[...] = g
        d_out[...], m_out[...], v_out[...] = _adamw_math(w_ref[...], g, m_ref[...], v_ref[...])

    both = pl.BlockSpec((None, tb, cs), lambda l, i, s: (l, i, 0))

    def half(layer, is_own):
        def index(l, i, s):
            own_block = (i // nbh) == s[0]
            use = (l == layer) & (own_block if is_own else jnp.logical_not(own_block))
            return (jnp.where(use, i % nbh, 0), 0)
        return pl.BlockSpec((tb, cs), index)

    return pl.pallas_call(
        body, name=name,
        grid_spec=pltpu.PrefetchScalarGridSpec(
            num_scalar_prefetch=1, grid=(DEPTH, rs // tb),
            in_specs=[both, half(0, True), half(0, False), half(1, True), half(1, False), both, both],
            out_specs=[both] * 4),
        out_shape=[jax.ShapeDtypeStruct(w.shape, F32)] * 4,
        compiler_params=_cp("arbitrary", "arbitrary"),
    )(ci_arr, w, own[0], other[0], own[1], other[1], m, v)


def _shift_down(v, s, t_iota):
    return jnp.where(t_iota >= s, pltpu.roll(v, s, 0), 0.0)


def _shift_up(v, s, t_iota, t):
    return jnp.where(t_iota < t - s, pltpu.roll(v, t - s, 0), 0.0)


def _ssd_conv_fwd(proj, w, b):
    t = proj.shape[0]
    k_w = SSD_CONV_K

    def body(x_ref, w_ref, b_ref, o_ref):
        x = x_ref[...]
        ti = lax.broadcasted_iota(jnp.int32, x.shape, 0)
        pre = x * w_ref[k_w - 1:k_w, :] + b_ref[...]
        for s in range(1, k_w):
            pre = pre + _shift_down(x, s, ti) * w_ref[k_w - 1 - s:k_w - s, :]
        o_ref[...] = pre * _sigmoid(pre)

    off = P_XBC // LANES
    return pl.pallas_call(
        body, name="ssd_conv_fwd", grid=(SSD_CONV_DIM // LANES,),
        in_specs=[pl.BlockSpec((t, LANES), lambda j: (0, off + j)),
                  pl.BlockSpec((k_w, LANES), lambda j: (0, j)),
                  pl.BlockSpec((1, LANES), lambda j: (0, j))],
        out_specs=pl.BlockSpec((t, LANES), lambda j: (0, j)),
        out_shape=jax.ShapeDtypeStruct((t, SSD_CONV_DIM), F32),
        compiler_params=_cp("parallel"),
    )(proj, w, b)


def _ssd_conv_bwd(dxbc, proj, w, b):
    t = proj.shape[0]
    k_w = SSD_CONV_K

    def body(d_ref, x_ref, w_ref, b_ref, dx_ref, dw_ref, db_ref):
        x = x_ref[...]
        ti = lax.broadcasted_iota(jnp.int32, x.shape, 0)
        shifted = [x] + [_shift_down(x, s, ti) for s in range(1, k_w)]
        pre = b_ref[...] + shifted[0] * w_ref[k_w - 1:k_w, :]
        for s in range(1, k_w):
            pre = pre + shifted[s] * w_ref[k_w - 1 - s:k_w - s, :]
        sg = _sigmoid(pre)
        dpre = d_ref[...] * (sg * (1.0 + pre * (1.0 - sg)))
        db_ref[...] = jnp.sum(dpre, axis=0, keepdims=True)
        dx = dpre * w_ref[k_w - 1:k_w, :]
        for s in range(k_w):
            dw_ref[k_w - 1 - s:k_w - s, :] = jnp.sum(dpre * shifted[s], axis=0, keepdims=True)
            if s:
                dx = dx + _shift_up(dpre, s, ti, t) * w_ref[k_w - 1 - s:k_w - s, :]
        dx_ref[...] = dx

    off = P_XBC // LANES
    return pl.pallas_call(
        body, name="ssd_conv_bwd", grid=(SSD_CONV_DIM // LANES,),
        in_specs=[pl.BlockSpec((t, LANES), lambda j: (0, j)),
                  pl.BlockSpec((t, LANES), lambda j: (0, off + j)),
                  pl.BlockSpec((k_w, LANES), lambda j: (0, j)),
                  pl.BlockSpec((1, LANES), lambda j: (0, j))],
        out_specs=[pl.BlockSpec((t, LANES), lambda j: (0, j)),
                   pl.BlockSpec((k_w, LANES), lambda j: (0, j)),
                   pl.BlockSpec((1, LANES), lambda j: (0, j))],
        out_shape=[jax.ShapeDtypeStruct((t, SSD_CONV_DIM), F32),
                   jax.ShapeDtypeStruct((k_w, SSD_CONV_DIM), F32),
                   jax.ShapeDtypeStruct((1, SSD_CONV_DIM), F32)],
        compiler_params=_cp("parallel"),
    )(dxbc, proj, w, b)


def _shortconv_fwd(proj, w):
    t = proj.shape[0]
    nb = SC_W // LANES
    off = P_SC // LANES

    def body(b_ref, c_ref, x_ref, w_ref, o_ref):
        u = c_ref[...] * x_ref[...]
        ti = lax.broadcasted_iota(jnp.int32, u.shape, 0)
        cv = u * w_ref[SC_K - 1:SC_K, :]
        for s in range(1, SC_K):
            cv = cv + _shift_down(u, s, ti) * w_ref[SC_K - 1 - s:SC_K - s, :]
        o_ref[...] = (b_ref[...] * cv).astype(BF16)

    return pl.pallas_call(
        body, name="shortconv_fwd", grid=(nb,),
        in_specs=[pl.BlockSpec((t, LANES), lambda j: (0, off + j)),
                  pl.BlockSpec((t, LANES), lambda j: (0, off + nb + j)),
                  pl.BlockSpec((t, LANES), lambda j: (0, off + 2 * nb + j)),
                  pl.BlockSpec((SC_K, LANES), lambda j: (0, j))],
        out_specs=pl.BlockSpec((t, LANES), lambda j: (0, j)),
        out_shape=jax.ShapeDtypeStruct((t, SC_W), BF16),
        compiler_params=_cp("parallel"),
    )(proj, proj, proj, w)


def _shortconv_bwd(dy, dy_off, proj, w):
    t = proj.shape[0]
    nb = SC_W // LANES
    off = P_SC // LANES
    doff = dy_off // LANES

    def body(d_ref, b_ref, c_ref, x_ref, w_ref, db_ref, dc_ref, dx_ref, dw_ref):
        cg, xin = c_ref[...], x_ref[...]
        u = cg * xin
        ti = lax.broadcasted_iota(jnp.int32, u.shape, 0)
        shifted = [u] + [_shift_down(u, s, ti) for s in range(1, SC_K)]
        cv = shifted[0] * w_ref[SC_K - 1:SC_K, :]
        for s in range(1, SC_K):
            cv = cv + shifted[s] * w_ref[SC_K - 1 - s:SC_K - s, :]
        d = d_ref[...]
        db_ref[...] = (d * cv).astype(BF16)
        dcv = d * b_ref[...]
        du = dcv * w_ref[SC_K - 1:SC_K, :]
        for s in range(SC_K):
            dw_ref[SC_K - 1 - s:SC_K - s, :] = jnp.sum(dcv * shifted[s], axis=0, keepdims=True)
            if s:
                du = du + _shift_up(dcv, s, ti, t) * w_ref[SC_K - 1 - s:SC_K - s, :]
        dc_ref[...] = (du * xin).astype(BF16)
        dx_ref[...] = (du * cg).astype(BF16)

    tile = pl.BlockSpec((t, LANES), lambda j: (0, j))
    outs = pl.pallas_call(
        body, name="shortconv_bwd", grid=(nb,),
        in_specs=[pl.BlockSpec((t, LANES), lambda j: (0, doff + j)),
                  pl.BlockSpec((t, LANES), lambda j: (0, off + j)),
                  pl.BlockSpec((t, LANES), lambda j: (0, off + nb + j)),
                  pl.BlockSpec((t, LANES), lambda j: (0, off + 2 * nb + j)),
                  pl.BlockSpec((SC_K, LANES), lambda j: (0, j))],
        out_specs=[tile, tile, tile, pl.BlockSpec((SC_K, LANES), lambda j: (0, j))],
        out_shape=[jax.ShapeDtypeStruct((t, SC_W), BF16)] * 3 + [jax.ShapeDtypeStruct((SC_K, SC_W), F32)],
        compiler_params=_cp("parallel"),
    )(dy, proj, proj, proj, w)
    return outs


def _cum_logf(proj, fbias_row):
    t = proj.shape[0]
    blk = CUM_BLOCK

    def body(p_ref, b_ref, o_ref, carry_ref):
        i = pl.program_id(0)

        @pl.when(i == 0)
        def _():
            carry_ref[...] = jnp.zeros_like(carry_ref)

        lf = -_softplus(-(p_ref[...] + b_ref[...]))
        r = lax.broadcasted_iota(jnp.int32, (blk, blk), 0)
        c = lax.broadcasted_iota(jnp.int32, (blk, blk), 1)
        tri = (r >= c).astype(BF16)
        o_ref[...] = _tri_left(tri, lf) + carry_ref[...]
        carry_ref[...] = o_ref[blk - 1:blk, :]

    return pl.pallas_call(
        body, name="cum_logf", grid=(t // blk,),
        in_specs=[pl.BlockSpec((blk, LANES), lambda i: (i, P_PAD // LANES)),
                  pl.BlockSpec((1, LANES), lambda i: (0, 0))],
        out_specs=pl.BlockSpec((blk, LANES), lambda i: (i, 0)),
        out_shape=jax.ShapeDtypeStruct((t, LANES), F32),
        scratch_shapes=[pltpu.VMEM((1, LANES), F32)],
        compiler_params=_cp("arbitrary"),
    )(proj, fbias_row)


def _pad_block_bwd(dcf, ddt, proj, fbias_row):
    t = proj.shape[0]
    blk = CUM_BLOCK
    nb = t // blk

    def body(dcf_ref, ddt_ref, p_ref, b_ref, o_ref, db_ref, carry_ref):
        i = pl.program_id(0)

        @pl.when(i == 0)
        def _():
            carry_ref[...] = jnp.zeros_like(carry_ref)
            db_ref[...] = jnp.zeros_like(db_ref)

        r = lax.broadcasted_iota(jnp.int32, (blk, blk), 0)
        c = lax.broadcasted_iota(jnp.int32, (blk, blk), 1)
        tri = (r <= c).astype(BF16)
        rev = _tri_left(tri, dcf_ref[...]) + carry_ref[...]
        carry_ref[...] = jnp.sum(dcf_ref[...], axis=0, keepdims=True) + carry_ref[...]
        lane = lax.broadcasted_iota(jnp.int32, (blk, LANES), 1)
        is_f = (lane >= PAD_F0) & (lane < PAD_F0 + FOX_HEADS)
        df = jnp.where(is_f, rev * _sigmoid(-(p_ref[...] + b_ref[...])), 0.0)
        db_ref[...] += jnp.sum(df, axis=0, keepdims=True)
        o_ref[...] = jnp.where(lane < PAD_DT0 + SSD_HEADS, ddt_ref[...], df).astype(BF16)

    return pl.pallas_call(
        body, name="pad_block_bwd", grid=(nb,),
        in_specs=[pl.BlockSpec((blk, LANES), lambda i: (nb - 1 - i, 0)),
                  pl.BlockSpec((blk, LANES), lambda i: (nb - 1 - i, 0)),
                  pl.BlockSpec((blk, LANES), lambda i: (nb - 1 - i, P_PAD // LANES)),
                  pl.BlockSpec((1, LANES), lambda i: (0, 0))],
        out_specs=[pl.BlockSpec((blk, LANES), lambda i: (nb - 1 - i, 0)),
                   pl.BlockSpec((1, LANES), lambda i: (0, 0))],
        out_shape=[jax.ShapeDtypeStruct((t, LANES), BF16), jax.ShapeDtypeStruct((1, LANES), F32)],
        scratch_shapes=[pltpu.VMEM((1, LANES), F32)],
        compiler_params=_cp("arbitrary"),
    )(dcf, ddt, proj, fbias_row)


def _att_scores(q, k, cq, ck, diagonal, blk):
    s = _dot_nt(q, k) * (FOX_HD ** -0.5) + cq - ck
    if diagonal:
        r = lax.broadcasted_iota(jnp.int32, (blk, blk), 0)
        c = lax.broadcasted_iota(jnp.int32, (blk, blk), 1)
        s = jnp.where(r >= c, s, NEG)
    return s


def _fox_fwd(q, k, v, cf_col, cf_row):
    h, t, hd = q.shape
    blk = min(ATT_BLOCK, t)
    nb = t // blk

    def body(q_ref, k_ref, v_ref, cq_ref, ck_ref, o_ref, lse_ref):
        qi = pl.program_id(1)
        qv, cq = q_ref[...], cq_ref[...]

        def step(j, carry, diagonal):
            m, l, acc = carry
            off = pl.multiple_of(j * blk, blk)
            s = _att_scores(qv, k_ref[pl.ds(off, blk), :], cq, ck_ref[:, pl.ds(off, blk)], diagonal, blk)
            m_new = jnp.maximum(m, jnp.max(s, axis=1, keepdims=True))
            alpha = jnp.exp(m - m_new)
            p = jnp.exp(s - m_new)
            l = alpha * l + jnp.sum(p, axis=1, keepdims=True)
            acc = alpha * acc + _dot(p.astype(BF16), v_ref[pl.ds(off, blk), :])
            return m_new, l, acc

        init = (jnp.full((blk, 1), NEG, F32), jnp.zeros((blk, 1), F32), jnp.zeros((blk, hd), F32))
        carry = lax.fori_loop(0, qi, lambda j, cr: step(j, cr, False), init)
        m, l, acc = step(qi, carry, True)
        o_ref[...] = acc / l
        lse_ref[...] = m + jnp.log(l)

    qmap = lambda hh, i: (hh, i, 0)
    whole = lambda hh, i: (hh, 0, 0)
    return pl.pallas_call(
        body, name="fox_fwd", grid=(h, nb),
        in_specs=[pl.BlockSpec((None, blk, hd), qmap), pl.BlockSpec((None, t, hd), whole),
                  pl.BlockSpec((None, t, hd), whole), pl.BlockSpec((None, blk, 1), qmap),
                  pl.BlockSpec((None, 1, t), whole)],
        out_specs=[pl.BlockSpec((None, blk, hd), qmap), pl.BlockSpec((None, blk, 1), qmap)],
        out_shape=[jax.ShapeDtypeStruct((h, t, hd), F32), jax.ShapeDtypeStruct((h, t, 1), F32)],
        compiler_params=_cp("parallel", "arbitrary"),
    )(q, k, v, cf_col, cf_row)


def _fox_bwd(q, k, v, cf_col, cf_row, o, lse, do):
    h, t, hd = q.shape
    blk = min(ATT_BLOCK, t)
    nb = t // blk
    scale = FOX_HD ** -0.5

    def body(q_ref, k_ref, v_ref, cq_ref, ck_ref, o_ref, lse_ref, do_ref,
             dq_ref, dk_ref, dv_ref, dcq_ref, dck_ref, delta_s):
        kj = pl.program_id(1)

        @pl.when(kj == 0)
        def _():
            dq_ref[...] = jnp.zeros_like(dq_ref)
            dcq_ref[...] = jnp.zeros_like(dcq_ref)

            def fill(i, _):
                rows = pl.ds(pl.multiple_of(i * blk, blk), blk)
                delta_s[rows, :] = jnp.sum(do_ref[rows, :] * o_ref[rows, :], axis=1, keepdims=True)
                return 0

            lax.fori_loop(0, nb, fill, 0)

        kb, vb, ck = k_ref[...], v_ref[...], ck_ref[...]

        def step(i, carry, diagonal):
            dk, dv, dck = carry
            rows = pl.ds(pl.multiple_of(i * blk, blk), blk)
            qb = q_ref[rows, :]
            do_b = do_ref[rows, :].astype(BF16)
            s = _att_scores(qb, kb, cq_ref[rows, :], ck, diagonal, blk)
            p = jnp.exp(s - lse_ref[rows, :])
            dv = dv + _dot_tn(p.astype(BF16), do_b)
            ds = p * (_dot_nt(do_b, vb) - delta_s[rows, :])
            ds_b = ds.astype(BF16)
            dk = dk + _dot_tn(ds_b, qb)
            dq_ref[rows, :] += _dot(ds_b, kb) * scale
            dcq_ref[rows, :] += jnp.sum(ds, axis=1, keepdims=True)
            dck = dck - jnp.sum(ds, axis=0, keepdims=True)
            return dk, dv, dck

        init = (jnp.zeros((blk, hd), F32), jnp.zeros((blk, hd), F32), jnp.zeros((1, blk), F32))
        carry = step(kj, init, True)
        dk, dv, dck = lax.fori_loop(kj + 1, nb, lambda i, cr: step(i, cr, False), carry)
        dk_ref[...] = dk * scale
        dv_ref[...] = dv
        dck_ref[...] = dck

    kmap = lambda hh, j: (hh, j, 0)
    whole = lambda hh, j: (hh, 0, 0)
    return pl.pallas_call(
        body, name="fox_bwd", grid=(h, nb),
        in_specs=[pl.BlockSpec((None, t, hd), whole), pl.BlockSpec((None, blk, hd), kmap),
                  pl.BlockSpec((None, blk, hd), kmap), pl.BlockSpec((None, t, 1), whole),
                  pl.BlockSpec((None, 1, blk), lambda hh, j: (hh, 0, j)),
                  pl.BlockSpec((None, t, hd), whole), pl.BlockSpec((None, t, 1), whole),
                  pl.BlockSpec((None, t, hd), whole)],
        out_specs=[pl.BlockSpec((None, t, hd), whole), pl.BlockSpec((None, blk, hd), kmap),
                   pl.BlockSpec((None, blk, hd), kmap), pl.BlockSpec((None, t, 1), whole),
                   pl.BlockSpec((None, 1, blk), lambda hh, j: (hh, 0, j))],
        out_shape=[jax.ShapeDtypeStruct((h, t, hd), F32), jax.ShapeDtypeStruct((h, t, hd), F32),
                   jax.ShapeDtypeStruct((h, t, hd), F32), jax.ShapeDtypeStruct((h, t, 1), F32),
                   jax.ShapeDtypeStruct((h, 1, t), F32)],
        scratch_shapes=[pltpu.VMEM((t, 1), F32)],
        compiler_params=_cp("parallel", "arbitrary"),
    )(q, k, v, cf_col, cf_row, o, lse, do)


def _lane_col(v, h):
    lane = lax.broadcasted_iota(jnp.int32, v.shape, 1)
    return jnp.sum(jnp.where(lane == h, v, 0.0), axis=1, keepdims=True)


def _sub_row(v, h):
    sub = lax.broadcasted_iota(jnp.int32, v.shape, 0)
    return jnp.sum(jnp.where(sub == h, v, 0.0), axis=0, keepdims=True)


def _ssd_decays(pad, pad_t, dtb_row, alog_row, dtb_col, alog_col, blk):
    r = lax.broadcasted_iota(jnp.int32, (blk, blk), 0)
    c = lax.broadcasted_iota(jnp.int32, (blk, blk), 1)
    tril = r >= c
    dt_c = _softplus(pad + dtb_row)
    acs_c = _tri_left(tril.astype(BF16), dt_c * (-jnp.exp(alog_row)))
    dt_r = _softplus(pad_t + dtb_col)
    acs_r = _tri_right(dt_r * (-jnp.exp(alog_col)), (r <= c).astype(BF16))
    rows = lax.broadcasted_iota(jnp.int32, acs_c.shape, 0)
    acs_last = jnp.sum(jnp.where(rows == blk - 1, acs_c, 0.0), axis=0, keepdims=True)
    return dt_c, acs_c, acs_r, acs_last, tril


def _pair_terms(pair, dt_c, acs_c, acs_r, acs_last, d_row, blk):
    lane = lax.broadcasted_iota(jnp.int32, (blk, LANES), 1)
    lo = lane < 64
    lo_row = lax.broadcasted_iota(jnp.int32, (1, LANES), 1) < 64
    h0, h1 = 2 * pair, 2 * pair + 1
    col = [_lane_col(acs_c, h0), _lane_col(acs_c, h1)]
    row = [_sub_row(acs_r, h0), _sub_row(acs_r, h1)]
    last = [_lane_col(acs_last, h0), _lane_col(acs_last, h1)]
    dt_p = jnp.where(lo, _lane_col(dt_c, h0), _lane_col(dt_c, h1))
    e_p = jnp.where(lo, jnp.exp(col[0]), jnp.exp(col[1]))
    w_p = jnp.where(lo, jnp.exp(last[0] - col[0]), jnp.exp(last[1] - col[1]))
    decay_p = jnp.where(lo_row, jnp.exp(last[0]), jnp.exp(last[1]))
    d_p = jnp.where(lo_row, _lane_col(d_row, h0), _lane_col(d_row, h1))
    return lo, lo_row, col, row, last, dt_p, e_p, w_p, decay_p, d_p


def _ssd_specs(t, blk, rev):
    nc = t // blk
    ix = (lambda i: nc - 1 - i) if rev else (lambda i: i)
    xbc = pl.BlockSpec((blk, SSD_CONV_DIM), lambda i: (ix(i), 0))
    pad = pl.BlockSpec((blk, LANES), lambda i: (ix(i), P_PAD // LANES))
    pad_t = pl.BlockSpec((LANES, blk), lambda i: (0, ix(i)))
    z = pl.BlockSpec((blk, SSD_W), lambda i: (ix(i), 0))
    row = pl.BlockSpec((1, LANES), lambda i: (0, 0))
    colv = pl.BlockSpec((LANES, 1), lambda i: (0, 0))
    ng = pl.BlockSpec((1, SSD_W), lambda i: (0, 0))
    y = pl.BlockSpec((blk, SSD_W), lambda i: (ix(i), 0))
    st = pl.BlockSpec((None, 4, LANES, LANES), lambda i: (ix(i), 0, 0, 0))
    return nc, xbc, pad, pad_t, z, row, colv, ng, y, st


def _ssd_fwd(xbc, proj, pad_t, dtb_row, alog_row, d_row, dtb_col, alog_col, ng):
    t = xbc.shape[0]
    blk = min(SSD_CHUNK, t)
    nc, s_xbc, s_pad, s_padt, s_z, s_row, s_col, s_ng, s_y, s_st = _ssd_specs(t, blk, False)

    def body(xbc_ref, pad_ref, padt_ref, z_ref, dtb_ref, alog_ref, d_ref, dtbc_ref, alogc_ref, ng_ref,
             out_ref, ypre_ref, st_ref, state):
        @pl.when(pl.program_id(0) == 0)
        def _():
            state[...] = jnp.zeros_like(state)

        dt_c, acs_c, acs_r, acs_last, tril = _ssd_decays(
            pad_ref[...], padt_ref[...], dtb_ref[...], alog_ref[...], dtbc_ref[...], alogc_ref[...], blk)
        ys = []
        g_mat = {}
        for pair in range(4):
            g = pair // 2
            bg = xbc_ref[:, SSD_W + LANES * g:SSD_W + LANES * (g + 1)].astype(BF16)
            cg = xbc_ref[:, SSD_W + 2 * LANES + LANES * g:SSD_W + 2 * LANES + LANES * (g + 1)].astype(BF16)
            if g not in g_mat:
                g_mat[g] = _dot_nt(cg, bg)
            xs_p = xbc_ref[:, LANES * pair:LANES * (pair + 1)]
            lo, _, col, row, _, dt_p, e_p, w_p, decay_p, d_p = _pair_terms(
                pair, dt_c, acs_c, acs_r, acs_last, d_ref[...], blk)
            x_p = xs_p * dt_p
            y = None
            for hh in range(2):
                lm = jnp.exp(jnp.where(tril, col[hh] - row[hh], NEG))
                m_h = (g_mat[g] * lm).astype(BF16)
                x_h = jnp.where(lo if hh == 0 else ~lo, x_p, 0.0).astype(BF16)
                y_h = _dot(m_h, x_h)
                y = y_h if y is None else y + y_h
            st_in = state[pair]
            st_ref[pair] = st_in
            y = y + e_p * _dot(cg, st_in.astype(BF16))
            state[pair] = decay_p * st_in + _dot_tn(bg, (x_p * w_p).astype(BF16))
            ys.append(y + d_p * xs_p)
        y_all = jnp.concatenate(ys, axis=1)
        ypre_ref[...] = y_all
        z = z_ref[...]
        y2 = y_all * (z * _sigmoid(z))
        outs = []
        for g in range(2):
            seg = y2[:, 256 * g:256 * (g + 1)]
            rr = lax.rsqrt(jnp.mean(seg * seg, axis=-1, keepdims=True) + RMS_EPS)
            outs.append(seg * rr * ng_ref[:, 256 * g:256 * (g + 1)])
        out_ref[...] = jnp.concatenate(outs, axis=1).astype(BF16)

    return pl.pallas_call(
        body, name="ssd_scan_fwd", grid=(nc,),
        in_specs=[s_xbc, s_pad, s_padt, s_z, s_row, s_row, s_row, s_col, s_col, s_ng],
        out_specs=[s_y, s_y, s_st],
        out_shape=[jax.ShapeDtypeStruct((t, SSD_W), BF16), jax.ShapeDtypeStruct((t, SSD_W), F32),
                   jax.ShapeDtypeStruct((nc, 4, LANES, LANES), F32)],
        scratch_shapes=[pltpu.VMEM((4, LANES, LANES), F32)],
        compiler_params=_cp("arbitrary"),
    )(xbc, proj, pad_t, proj, dtb_row, alog_row, d_row, dtb_col, alog_col, ng)


def _ssd_bwd(dout, dout_off, xbc, proj, pad_t, ypre, states, dtb_row, alog_row, d_row, dtb_col, alog_col, ng):
    t = xbc.shape[0]
    blk = min(SSD_CHUNK, t)
    nc, s_xbc, s_pad, s_padt, s_z, s_row, s_col, s_ng, s_y, s_st = _ssd_specs(t, blk, True)
    s_dout = pl.BlockSpec((blk, SSD_W), lambda i: (nc - 1 - i, dout_off // SSD_W))

    def body(dout_ref, xbc_ref, pad_ref, padt_ref, z_ref, ypre_ref, st_ref, dtb_ref, alog_ref, d_ref,
             dtbc_ref, alogc_ref, ng_ref, dxbc_ref, ddt_ref, dz_ref, acc_ref, dng_ref, dstate):
        @pl.when(pl.program_id(0) == 0)
        def _():
            dstate[...] = jnp.zeros_like(dstate)
            acc_ref[...] = jnp.zeros_like(acc_ref)
            dng_ref[...] = jnp.zeros_like(dng_ref)

        pad = pad_ref[...]
        dt_c, acs_c, acs_r, acs_last, tril = _ssd_decays(
            pad, padt_ref[...], dtb_ref[...], alog_ref[...], dtbc_ref[...], alogc_ref[...], blk)
        a_row = -jnp.exp(alog_ref[...])

        z = z_ref[...]
        sz = _sigmoid(z)
        silu_z = z * sz
        y_pre = ypre_ref[...]
        y2 = y_pre * silu_z
        dy2 = []
        for g in range(2):
            sl = slice(256 * g, 256 * (g + 1))
            seg = y2[:, sl]
            rr = lax.rsqrt(jnp.mean(seg * seg, axis=-1, keepdims=True) + RMS_EPS)
            nrm = seg * rr
            d_seg = dout_ref[:, sl]
            dng_ref[:, sl] += jnp.sum(d_seg * nrm, axis=0, keepdims=True)
            dn = d_seg * ng_ref[:, sl]
            dy2.append(rr * (dn - nrm * jnp.mean(dn * nrm, axis=-1, keepdims=True)))
        dy2 = jnp.concatenate(dy2, axis=1)
        dz_ref[...] = (dy2 * y_pre * (sz * (1.0 + z * (1.0 - sz)))).astype(BF16)
        dy_all = dy2 * silu_z

        lane_row = lax.broadcasted_iota(jnp.int32, (1, LANES), 1)
        lane_blk = lax.broadcasted_iota(jnp.int32, (blk, LANES), 1)
        row_col = lax.broadcasted_iota(jnp.int32, (blk, 1), 0)
        ddt = jnp.zeros((blk, LANES), F32)
        dacs = jnp.zeros((blk, LANES), F32)
        dd_row = jnp.zeros((1, LANES), F32)
        ones_b = jnp.ones((blk, LANES), BF16)
        dxs = []
        d_b = [None, None]
        d_c = [None, None]
        d_g = [None, None]
        bgs, cgs = {}, {}
        g_mat = {}
        for pair in range(4):
            g = pair // 2
            if g not in g_mat:
                bgs[g] = xbc_ref[:, SSD_W + LANES * g:SSD_W + LANES * (g + 1)].astype(BF16)
                cgs[g] = xbc_ref[:, SSD_W + 2 * LANES + LANES * g:SSD_W + 2 * LANES + LANES * (g + 1)].astype(BF16)
                g_mat[g] = _dot_nt(cgs[g], bgs[g])
            bg, cg = bgs[g], cgs[g]
            xs_p = xbc_ref[:, LANES * pair:LANES * (pair + 1)]
            lo, lo_row, col, row, last, dt_p, e_p, w_p, decay_p, d_p = _pair_terms(
                pair, dt_c, acs_c, acs_r, acs_last, d_ref[...], blk)
            x_p = xs_p * dt_p
            dy_p = dy_all[:, LANES * pair:LANES * (pair + 1)]
            st_in = st_ref[pair]
            dst = dstate[pair]
            dx_diag = None
            for hh in range(2):
                sel = lo if hh == 0 else ~lo
                lm = jnp.exp(jnp.where(tril, col[hh] - row[hh], NEG))
                m_f = g_mat[g] * lm
                m_h = m_f.astype(BF16)
                x_h = jnp.where(sel, x_p, 0.0).astype(BF16)
                dy_h = jnp.where(sel, dy_p, 0.0).astype(BF16)
                dxd = _dot_tn(m_h, dy_h)
                dm = _dot_nt(dy_h, x_h)
                dg_h = dm * lm
                p_b = (dm * m_f).astype(BF16)
                dacs = dacs + jnp.where(lane_blk == 2 * pair + hh, _dot(p_b, ones_b) - _dot_tn(p_b, ones_b), 0.0)
                dx_diag = dxd if dx_diag is None else dx_diag + dxd
                d_g[g] = dg_h if d_g[g] is None else d_g[g] + dg_h
            st_b = st_in.astype(BF16)
            dst_b = dst.astype(BF16)
            y_off = e_p * _dot(cg, st_b)
            edy = (e_p * dy_p).astype(BF16)
            dc_off = _dot_nt(edy, st_b)
            d_c[g] = dc_off if d_c[g] is None else d_c[g] + dc_off
            dstate[pair] = decay_p * dst + _dot_tn(cg, edy)
            dx_state = _dot(bg, dst_b) * w_p
            db_st = _dot_nt((x_p * w_p).astype(BF16), dst_b)
            d_b[g] = db_st if d_b[g] is None else d_b[g] + db_st
            dx = dx_diag + dx_state
            dxs.append(dx * dt_p + d_p * dy_p)
            prod_dt = dx * xs_p
            prod_acs = dy_p * y_off - x_p * dx_state
            prod_st = x_p * dx_state
            prod_d = dy_p * xs_p
            st_prod = jnp.sum(dst * st_in, axis=0, keepdims=True)
            for hh in range(2):
                h = 2 * pair + hh
                sel = lo if hh == 0 else ~lo
                sel_row = lo_row if hh == 0 else ~lo_row
                ddt_h = jnp.sum(jnp.where(sel, prod_dt, 0.0), axis=1, keepdims=True)
                dacs_h = jnp.sum(jnp.where(sel, prod_acs, 0.0), axis=1, keepdims=True)
                tail = jnp.sum(jnp.sum(jnp.where(sel, prod_st, 0.0), axis=1, keepdims=True), axis=0, keepdims=True)
                tail = tail + jnp.exp(last[hh]) * jnp.sum(jnp.where(sel_row, st_prod, 0.0), axis=1, keepdims=True)
                dacs_h = dacs_h + jnp.where(row_col == blk - 1, tail, 0.0)
                dd_h = jnp.sum(jnp.sum(jnp.where(sel, prod_d, 0.0), axis=1, keepdims=True), axis=0, keepdims=True)
                ddt = ddt + jnp.where(lane_blk == h, ddt_h, 0.0)
                dacs = dacs + jnp.where(lane_blk == h, dacs_h, 0.0)
                dd_row = dd_row + jnp.where(lane_row == h, dd_h, 0.0)
        for g in range(2):
            dg_b = d_g[g].astype(BF16)
            d_c[g] = d_c[g] + _dot(dg_b, bgs[g])
            d_b[g] = d_b[g] + _dot_tn(dg_b, cgs[g])
        r = lax.broadcasted_iota(jnp.int32, (blk, blk), 0)
        c = lax.broadcasted_iota(jnp.int32, (blk, blk), 1)
        da = _tri_left((r <= c).astype(BF16), dacs)
        ddt = ddt + da * a_row
        d_raw = ddt * _sigmoid(pad + dtb_ref[...])
        ddt_ref[...] = d_raw
        acc_ref[0:1, :] += jnp.sum(da * dt_c, axis=0, keepdims=True) * a_row
        acc_ref[1:2, :] += dd_row
        acc_ref[2:3, :] += jnp.sum(d_raw, axis=0, keepdims=True)
        dxbc_ref[...] = jnp.concatenate(dxs + d_b + d_c, axis=1)

    return pl.pallas_call(
        body, name="ssd_scan_bwd", grid=(nc,),
        in_specs=[s_dout, s_xbc, s_pad, s_padt, s_z, s_y, s_st, s_row, s_row, s_row, s_col, s_col, s_ng],
        out_specs=[s_xbc, pl.BlockSpec((blk, LANES), lambda i: (nc - 1 - i, 0)), s_y,
                   pl.BlockSpec((8, LANES), lambda i: (0, 0)), s_ng],
        out_shape=[jax.ShapeDtypeStruct((t, SSD_CONV_DIM), F32), jax.ShapeDtypeStruct((t, LANES), F32),
                   jax.ShapeDtypeStruct((t, SSD_W), BF16), jax.ShapeDtypeStruct((8, LANES), F32),
                   jax.ShapeDtypeStruct((1, SSD_W), F32)],
        scratch_shapes=[pltpu.VMEM((4, LANES, LANES), F32)],
        compiler_params=_cp("arbitrary"),
    )(dout, xbc, proj, pad_t, proj, ypre, states, dtb_row, alog_row, d_row, dtb_col, alog_col, ng)


def _pad_lanes(v, off):
    return jnp.zeros((1, LANES), F32).at[0, off:off + v.shape[0]].set(v)


def _perm_mix_w_in(w):
    z, xbc, dt = w[:, 0:512], w[:, 512:1536], w[:, 1536:1544]
    qkv, f, sc = w[:, 1544:2312], w[:, 2312:2316], w[:, 2316:3084]
    padblk = jnp.zeros((w.shape[0], LANES), w.dtype).at[:, PAD_DT0:PAD_DT0 + 8].set(dt).at[:, PAD_F0:PAD_F0 + 4].set(f)
    return jnp.concatenate([z, xbc, qkv, sc, padblk], axis=1)


def _unperm_mix_w_in(wp):
    z, xbc, qkv, sc = wp[:, 0:512], wp[:, 512:1536], wp[:, 1536:2304], wp[:, 2304:3072]
    dt, f = wp[:, P_PAD + PAD_DT0:P_PAD + PAD_DT0 + 8], wp[:, P_PAD + PAD_F0:P_PAD + PAD_F0 + 4]
    return jnp.concatenate([z, xbc, dt, qkv, f, sc], axis=1)


def _heads(m):
    return jnp.transpose(m.reshape(m.shape[0], FOX_HEADS, FOX_HD), (1, 0, 2))


def _unheads(m):
    return jnp.transpose(m, (1, 0, 2)).reshape(m.shape[1], FOX_W)


def _ffn_fwd(h, w_in, w_out, tag):
    u, a = _ffn_in_swiglu(h, w_in, f"ffn_in_{tag}")
    y = _matmul(a, w_out, "nn", F32, f"ffn_out_{tag}")
    return y, (h, u, a)


def _ffn_bwd(dy, saved, w_in, w_out, tag):
    h, u, a = saved
    du = _ffn_out_dx_swiglu(dy, w_out, u, f"ffn_out_dx_{tag}")
    dw_out = _matmul(a, dy, "tn", BF16, f"ffn_out_dw_{tag}")
    dh = _matmul(du, w_in, "nt", F32, f"ffn_in_dx_{tag}", halves="a")
    dw_in = _matmul(h, du, "tn", BF16, f"ffn_in_dw_{tag}", halves="b")
    return dh, dw_in, dw_out


def _mix_fwd(h, wp, w_out, sp, tag):
    proj = _matmul(h, wp, "nn", F32, f"mix_in_{tag}")
    pad_t = jnp.transpose(proj[:, P_PAD:P_PAD + LANES])
    xbc = _ssd_conv_fwd(proj, sp["conv_w"], sp["conv_b"])
    y_ssd, ypre, states = _ssd_fwd(xbc, proj, pad_t, sp["dtb_row"], sp["alog_row"], sp["d_row"],
                                   sp["dtb_col"], sp["alog_col"], sp["ng"])
    cf = _cum_logf(proj, sp["fbias_row"])
    cf4 = jnp.transpose(cf[:, PAD_F0:PAD_F0 + FOX_HEADS])
    cf_col, cf_row = cf4[:, :, None], cf4[:, None, :]
    q = _heads(proj[:, P_QKV:P_QKV + 256].astype(BF16))
    k = _heads(proj[:, P_QKV + 256:P_QKV + 512].astype(BF16))
    v = _heads(proj[:, P_QKV + 512:P_QKV + 768].astype(BF16))
    o, lse = _fox_fwd(q, k, v, cf_col, cf_row)
    y_sc = _shortconv_fwd(proj, sp["sconv_w"])
    ymix = jnp.concatenate([y_ssd, _unheads(o).astype(BF16), y_sc], axis=1)
    y = _matmul(ymix, w_out, "nn", F32, f"mix_out_{tag}")
    return y, (h, proj, pad_t, xbc, ypre, states, q, k, v, cf_col, cf_row, o, lse, ymix)


def _mix_bwd(dy, saved, wp, w_out, sp, tag):
    h, proj, pad_t, xbc, ypre, states, q, k, v, cf_col, cf_row, o, lse, ymix = saved
    dymix = _matmul(dy, w_out, "nt", F32, f"mix_out_dx_{tag}")
    dw_out = _matmul(ymix, dy, "tn", BF16, f"mix_out_dw_{tag}")
    dxbc, ddt, dz, ssd_acc, dng = _ssd_bwd(dymix, 0, xbc, proj, pad_t, ypre, states, sp["dtb_row"],
                                           sp["alog_row"], sp["d_row"], sp["dtb_col"], sp["alog_col"], sp["ng"])
    dxbc_raw, dconv_w, dconv_b = _ssd_conv_bwd(dxbc, proj, sp["conv_w"], sp["conv_b"])
    do = _heads(dymix[:, SSD_W:SSD_W + FOX_W])
    dq, dk, dv, dcq, dck = _fox_bwd(q, k, v, cf_col, cf_row, o, lse, do)
    dcf4 = dcq[:, :, 0] + dck[:, 0, :]
    dcf = jnp.zeros((h.shape[0], LANES), F32).at[:, PAD_F0:PAD_F0 + FOX_HEADS].set(jnp.transpose(dcf4))
    dpad, dfb = _pad_block_bwd(dcf, ddt, proj, sp["fbias_row"])
    dscb, dscc, dscx, dsconv_w = _shortconv_bwd(dymix, SSD_W + FOX_W, proj, sp["sconv_w"])
    dproj = jnp.concatenate([dz, dxbc_raw.astype(BF16), _unheads(dq).astype(BF16), _unheads(dk).astype(BF16),
                             _unheads(dv).astype(BF16), dscb, dscc, dscx, dpad], axis=1)
    dh = _matmul(dproj, wp, "nt", F32, f"mix_in_dx_{tag}")
    dwp = _matmul(h, dproj, "tn", BF16, f"mix_in_dw_{tag}")
    small = dict(conv_w=dconv_w, conv_b=dconv_b[0], dt_bias=ssd_acc[2, 0:8], a_log=ssd_acc[0, 0:8],
                 d=ssd_acc[1, 0:8], norm_g=dng[0], f_bias=dfb[0, PAD_F0:PAD_F0 + FOX_HEADS], sconv_w=dsconv_w)
    return dh, _unperm_mix_w_in(dwp), dw_out, small


def _local_step(x, tgt, mod, wfull, small_p, after_sub_backward=None):
    row = lambda v: v.reshape(1, -1)
    subs = [(l, j) for l in range(DEPTH) for j in range(N_SUB)]
    factor = (0.5, 1.0, 0.5)
    w_names = (("ffn1_w_in", "ffn1_w_out"), ("mix_w_in", "mix_w_out"), ("ffn2_w_in", "ffn2_w_out"))
    lg = [[row(small_p["ln_g"][l, j]) for j in range(N_SUB)] for l in range(DEPTH)]
    lb = [[row(small_p["ln_b"][l, j]) for j in range(N_SUB)] for l in range(DEPTH)]
    sps = []
    for l in range(DEPTH):
        sp = dict(
            conv_w=small_p["ssd_conv_w"][l], conv_b=row(small_p["ssd_conv_b"][l]),
            dtb_row=_pad_lanes(small_p["ssd_dt_bias"][l], PAD_DT0), alog_row=_pad_lanes(small_p["ssd_a_log"][l], 0),
            d_row=_pad_lanes(small_p["ssd_d"][l], 0), ng=row(small_p["ssd_norm_g"][l]),
            fbias_row=_pad_lanes(small_p["fox_f_bias"][l], PAD_F0), sconv_w=small_p["sconv_w"][l])
        sp["dtb_col"] = jnp.transpose(sp["dtb_row"])
        sp["alog_col"] = jnp.transpose(sp["alog_row"])
        sps.append(sp)
    tags = [f"l{l}{('f1', 'mx', 'f2')[j]}" for l, j in subs]

    x0, h = _ln_in_fwd(x, row(small_p["ln_in_g"]), row(small_p["ln_in_b"]), mod[0, 0])
    cur = x0
    xins, ys, inner, weights = [], [], [], []
    wfull = list(wfull)
    for idx, (l, j) in enumerate(subs):
        if idx == 1 and callable(wfull[DEPTH - 1]):
            wfull[DEPTH - 1] = wfull[DEPTH - 1](cur)
        w_in, w_out = wfull[l][w_names[j][0]], wfull[l][w_names[j][1]]
        if idx > 0:
            (w_in, w_out), (cur, h) = lax.optimization_barrier(((w_in, w_out), (cur, h)))
        w_out = w_out.reshape(-1, w_out.shape[-1])
        if j == 1:
            w_in = _perm_mix_w_in(jnp.concatenate([w_in[s] for s in range(4)], axis=1))
        weights.append((w_in, w_out))
        if j == 1:
            y, sv = _mix_fwd(h, w_in, w_out, sps[l], tags[idx])
        else:
            y, sv = _ffn_fwd(h, w_in, w_out, tags[idx])
        nxt = mod[subs[idx + 1]] if idx + 1 < len(subs) else None
        xins.append(cur)
        ys.append(y)
        inner.append(sv)
        cur, h = _res_ln(cur, y, mod[l, j], lg[l][j], lb[l][j], factor[j], f"res_ln_{tags[idx]}", nxt)

    dcur, loss_acc = _loss_head(cur, tgt)
    last = len(subs) - 1
    l, j = subs[last]
    dres, dy, acc = _res_ln_bwd(dcur, xins[last], ys[last], mod[l, j], lg[l][j], factor[j], f"res_ln_bwd_{tags[last]}")
    ln_acc = {last: acc}
    shift_scale = {}
    big_grads = [dict() for _ in range(DEPTH)]
    small_g = [None] * DEPTH
    for idx in reversed(range(len(subs))):
        l, j = subs[idx]
        w_in, w_out = weights[idx]
        if j == 1:
            dh, g_in, g_out, small_g[l] = _mix_bwd(dy, inner[idx], w_in, w_out, sps[l], tags[idx])
        else:
            dh, g_in, g_out = _ffn_bwd(dy, inner[idx], w_in, w_out, tags[idx])
        big_grads[l][w_names[j][0]], big_grads[l][w_names[j][1]] = g_in, g_out
        if after_sub_backward is not None:
            after_sub_backward(l, j, big_grads[l], dh)
        if idx > 0:
            pl_, pj = subs[idx - 1]
            dres, dy, acc5 = _mod_res_bwd(dres, dh, mod[l, j], xins[idx - 1], ys[idx - 1], mod[pl_, pj], lg[pl_][pj],
                                          lb[pl_][pj], factor[pj], f"mod_res_bwd_{tags[idx]}")
            shift_scale[idx], ln_acc[idx - 1] = acc5[0:2], acc5[2:5]
        else:
            dx0, shift_scale[0] = _modulate_bwd(dres, dh, x0, mod[0, 0], "mod_bwd_first")
    dx, acc_in = _ln_in_bwd(dx0, x, row(small_p["ln_in_g"]))
    dmod = []
    for l in range(DEPTH):
        ids = [N_SUB * l + j for j in range(N_SUB)]
        small_g[l]["ln_g"] = jnp.stack([ln_acc[i][0] for i in ids])
        small_g[l]["ln_b"] = jnp.stack([ln_acc[i][1] for i in ids])
        dmod.append(jnp.stack([jnp.concatenate([shift_scale[i], ln_acc[i][2:3]], axis=0) for i in ids]))
    return loss_acc[0, 0], dx, big_grads, small_g, jnp.stack(dmod), acc_in


SMALL_ORDER = ("ssd_conv_w", "ssd_conv_b", "ssd_dt_bias", "ssd_a_log", "ssd_d", "ssd_norm_g", "fox_f_bias",
               "sconv_w", "ln_g", "ln_b")
SMALL_KEY = dict(ssd_conv_w="conv_w", ssd_conv_b="conv_b", ssd_dt_bias="dt_bias", ssd_a_log="a_log", ssd_d="d",
                 ssd_norm_g="norm_g", fox_f_bias="f_bias", sconv_w="sconv_w", ln_g="ln_g", ln_b="ln_b")
COL_SHARDED_SMALL = ("ssd_conv_w", "sconv_w", "ln_g", "ln_b")


def _pad_to(v, n):
    return jnp.concatenate([v, jnp.zeros((n - v.shape[0],), v.dtype)])


def kernel(x, c, ln_in_g, ln_in_b, ada_w, ada_b, ffn1_w_in, ffn1_w_out, mix_w_in, mix_w_out, ssd_conv_w, ssd_conv_b, ssd_dt_bias, ssd_a_log, ssd_d, ssd_norm_g, fox_f_bias, sconv_w, ffn2_w_in, ffn2_w_out, ln_g, ln_b, loss_target, m_ln_in_g, m_ln_in_b, m_ada_w, m_ada_b, m_ffn1_w_in, m_ffn1_w_out, m_mix_w_in, m_mix_w_out, m_ssd_conv_w, m_ssd_conv_b, m_ssd_dt_bias, m_ssd_a_log, m_ssd_d, m_ssd_norm_g, m_fox_f_bias, m_sconv_w, m_ffn2_w_in, m_ffn2_w_out, m_ln_g, m_ln_b, v_ln_in_g, v_ln_in_b, v_ada_w, v_ada_b, v_ffn1_w_in, v_ffn1_w_out, v_mix_w_in, v_mix_w_out, v_ssd_conv_w, v_ssd_conv_b, v_ssd_dt_bias, v_ssd_a_log, v_ssd_d, v_ssd_norm_g, v_fox_f_bias, v_sconv_w, v_ffn2_w_in, v_ffn2_w_out, v_ln_g, v_ln_b):
    names = ("ln_in_g", "ln_in_b", "ada_w", "ada_b", "ffn1_w_in", "ffn1_w_out", "mix_w_in", "mix_w_out",
             "ssd_conv_w", "ssd_conv_b", "ssd_dt_bias", "ssd_a_log", "ssd_d", "ssd_norm_g", "fox_f_bias", "sconv_w",
             "ffn2_w_in", "ffn2_w_out", "ln_g", "ln_b")
    w_loc = dict(zip(names, (ln_in_g, ln_in_b, ada_w, ada_b, ffn1_w_in, ffn1_w_out, mix_w_in, mix_w_out, ssd_conv_w,
                             ssd_conv_b, ssd_dt_bias, ssd_a_log, ssd_d, ssd_norm_g, fox_f_bias, sconv_w, ffn2_w_in,
                             ffn2_w_out, ln_g, ln_b)))
    m_loc = dict(zip(names, (m_ln_in_g, m_ln_in_b, m_ada_w, m_ada_b, m_ffn1_w_in, m_ffn1_w_out, m_mix_w_in,
                             m_mix_w_out, m_ssd_conv_w, m_ssd_conv_b, m_ssd_dt_bias, m_ssd_a_log, m_ssd_d,
                             m_ssd_norm_g, m_fox_f_bias, m_sconv_w, m_ffn2_w_in, m_ffn2_w_out, m_ln_g, m_ln_b)))
    v_loc = dict(zip(names, (v_ln_in_g, v_ln_in_b, v_ada_w, v_ada_b, v_ffn1_w_in, v_ffn1_w_out, v_mix_w_in,
                             v_mix_w_out, v_ssd_conv_w, v_ssd_conv_b, v_ssd_dt_bias, v_ssd_a_log, v_ssd_d,
                             v_ssd_norm_g, v_fox_f_bias, v_sconv_w, v_ffn2_w_in, v_ffn2_w_out, v_ln_g, v_ln_b)))

    xi, yi, ci = lax.axis_index("x"), lax.axis_index("y"), lax.axis_index("c")
    me = 2 * xi + yi
    dev = 2 * me + ci

    def gather8(v, tag):
        v2 = v.reshape(1, -1)
        got = _bcast_chips([v2], f"gather_chips_{tag}")
        same_c = jnp.concatenate(_by_chip(me, v2, [g[0] for g in got]), axis=0)
        other_c = _swap_sibling([same_c], f"gather_sibling_{tag}")[0]
        pair = lax.switch(ci, [lambda a, b: jnp.stack([a, b], axis=1), lambda a, b: jnp.stack([b, a], axis=1)],
                          same_c, other_c)
        return pair.reshape(8, -1)

    def chip_concat(own, got3, axis):
        return jnp.concatenate(_by_chip(me, own, got3), axis=axis)

    small_cols = [w_loc[n].reshape(-1, w_loc[n].shape[-1]) for n in COL_SHARDED_SMALL]
    got = _bcast_chips(small_cols, "gather_small_params")
    small_p = {n: w_loc[n] for n in ("ln_in_g", "ln_in_b", "ssd_conv_b", "ssd_dt_bias", "ssd_a_log", "ssd_d",
                                     "ssd_norm_g", "fox_f_bias")}
    for i, n in enumerate(COL_SHARDED_SMALL):
        full = chip_concat(small_cols[i], [g[i] for g in got], 1)
        small_p[n] = full.reshape(w_loc[n].shape[:-1] + (full.shape[-1],))

    assert DEPTH == 2
    first, rest = BIG[:2], BIG[2:]
    shard_of = {n: w_loc[n].astype(BF16) for n in BIG}

    def gather(names, layer, after, tag, sequencer_id):
        out = _gather_weights(names, [shard_of[n] for n in names], layer, after, f"gather_weights_{tag}", sequencer_id)
        return dict(zip(names, out))

    g0a = gather(first, 0, [], "l0a", None)
    g0b = gather(rest, 0, [g0a[first[1]]], "l0b", 1)
    wfull = [{**g0a, **g0b}, lambda marker: gather(BIG, 1, [marker], "l1", 5)]

    c_all = gather8(c[0], "c")
    c_act = _silu_bf16(c_all)
    ada_w_b = ada_w.astype(BF16)
    mod_loc = [_matmul(c_act, ada_w_b[l], "nn", F32, f"ada_fwd_l{l}") for l in range(DEPTH)]
    mod_loc = jnp.stack(mod_loc)
    got = _bcast_chips([mod_loc], "gather_mod")
    mod_all = chip_concat(mod_loc, [g[0] for g in got], 2)
    mod = lax.dynamic_index_in_dim(mod_all, dev, 1, keepdims=False) + ada_b
    mod = mod.reshape(DEPTH, N_SUB, 3, D)

    ci_arr = jnp.reshape(ci, (1,)).astype(jnp.int32)
    me_arr = jnp.reshape(me, (1,)).astype(jnp.int32)

    def wire_form(n, g):
        if n in BY_COLUMNS:
            return g
        if n == "mix_w_in":
            return jnp.transpose(g.reshape(g.shape[0], 4, g.shape[1] // 4), (1, 0, 2))
        return g.reshape(4, g.shape[0] // 4, g.shape[1])

    units, reduced = {}, {}

    def start_unit(tag, l, names, g, ids):
        wire = [wire_form(n, g[n]) for n in names]
        units[tag] = dict(l=l, names=names, wire=wire, ids=ids,
                          got=_grad_swap_halves(names, wire, f"grad_swap_halves_{tag}", ids[0]))

    def add_and_scatter(tag, after):
        st = units[tag]
        st["part"] = [_grad_add_halves(st["wire"][i], st["got"][i], ci_arr, after, f"grad_add_halves_{tag}_{n}")
                      for i, n in enumerate(st["names"])]
        st["from_chips"] = _grad_scatter_halves(st["names"], st["part"], f"grad_scatter_halves_{tag}", st["ids"][1])

    def finish_unit(tag, after):
        st = units[tag]
        halves = [_grad_add_slots(st["part"][i], [g[i] for g in st["from_chips"]], n, me_arr, after,
                                  f"grad_add_slots_{tag}_{n}") for i, n in enumerate(st["names"])]
        other = _grad_swap_reduced(halves, f"grad_swap_reduced_{tag}", st["ids"][2])
        for i, n in enumerate(st["names"]):
            reduced[(st["l"], n)] = (halves[i], other[i])

    def after_sub_backward(l, j, g, marker):
        if (l, j) == (1, 0):
            start_unit("l1", 1, BIG, g, (2, 3, 4))
        elif (l, j) == (0, 2):
            add_and_scatter("l1", marker)
        elif (l, j) == (0, 1):
            start_unit("l0b", 0, rest, g, (6, 7, 8))
            add_and_scatter("l0b", marker)

    loss_part, dx, big_g, small_g, dmod, acc_in = _local_step(
        x[0], loss_target[0], mod, wfull, small_p, after_sub_backward)
    loss = lax.psum(loss_part, ("x", "y", "c"))

    pieces = [dmod.reshape(-1), acc_in[0], acc_in[1]]
    for n in SMALL_ORDER:
        pieces.append(jnp.stack([small_g[l][SMALL_KEY[n]] for l in range(DEPTH)]).reshape(-1))
    sizes = [p.shape[0] for p in pieces]
    total = sum(sizes)
    padded = -(-total // 1024) * 1024
    vec = _pad_to(jnp.concatenate(pieces), padded)
    all_rows = gather8(vec, "small_grads")
    summed = _sum_rows(all_rows, "sum_small_grads")[0]
    offs = [0]
    for s in sizes:
        offs.append(offs[-1] + s)
    n_mod = sizes[0]
    grads = {"ada_b": summed[0:n_mod].reshape(DEPTH, 3 * N_SUB * D),
             "ln_in_g": summed[offs[1]:offs[2]], "ln_in_b": summed[offs[2]:offs[3]]}
    for i, n in enumerate(SMALL_ORDER):
        full = summed[offs[3 + i]:offs[4 + i]].reshape(small_p[n].shape)
        if n in COL_SHARDED_SMALL:
            wcols = w_loc[n].shape[-1]
            full = lax.dynamic_slice_in_dim(full, me * wcols, wcols, axis=full.ndim - 1)
        grads[n] = full

    dmod_all = all_rows[:, 0:n_mod].reshape(8, DEPTH, 3 * N_SUB * D)
    ncol = ada_w.shape[-1]
    dmod_cols = lax.dynamic_slice_in_dim(dmod_all, me * ncol, ncol, axis=2).astype(BF16)
    grads["ada_w"] = jnp.stack([_matmul(c_act, dmod_cols[:, l], "tn", F32, f"ada_bwd_l{l}") for l in range(DEPTH)])

    finish_unit("l1", dx)
    finish_unit("l0b", dx)
    start_unit("l0a", 0, first, big_g[0], (None, None, None))
    add_and_scatter("l0a", dx)
    finish_unit("l0a", dx)

    delta, new_m, new_v = {}, {}, {}
    delta["ada_w"], new_m["ada_w"], new_v["ada_w"] = _adamw(w_loc["ada_w"], grads["ada_w"], m_loc["ada_w"],
                                                           v_loc["ada_w"], "adamw_ada_w")
    for i, n in enumerate(BIG):
        grads[n], delta[n], new_m[n], new_v[n] = _adamw_layers(
            w_loc[n], [reduced[(l, n)][0] for l in range(DEPTH)], [reduced[(l, n)][1] for l in range(DEPTH)],
            m_loc[n], v_loc[n], ci_arr, f"adamw_{n}")
    small_names = [n for n in names if n not in ("ada_w",) + BIG]
    flat = lambda d: jnp.concatenate([d[n].reshape(-1) for n in small_names])
    n_small = sum(w_loc[n].size for n in small_names)
    n_pad = -(-n_small // 1024) * 1024
    packed = [_pad_to(flat(d), n_pad).reshape(-1, LANES) for d in (w_loc, grads, m_loc, v_loc)]
    d_s, m_s, v_s = _adamw(*packed, "adamw_small")
    off = 0
    for n in small_names:
        sz = w_loc[n].size
        delta[n] = d_s.reshape(-1)[off:off + sz].reshape(w_loc[n].shape)
        new_m[n] = m_s.reshape(-1)[off:off + sz].reshape(w_loc[n].shape)
        new_v[n] = v_s.reshape(-1)[off:off + sz].reshape(w_loc[n].shape)
        off += sz

    return (loss, dx[None], *[grads[n] for n in names], *[delta[n] for n in names],
            *[new_m[n] for n in names], *[new_v[n] for n in names])
```

```python
import functools

import jax
import jax.numpy as jnp
from jax import lax
from jax.experimental import pallas as pl
from jax.experimental.pallas import tpu as pltpu
from jax.experimental.pallas import tpu_sc as plsc

F32 = jnp.float32
BF16 = jnp.bfloat16
MESH = pl.DeviceIdType.MESH

D = 1024
DEPTH = 2
N_SUB = 3
D_FF = 2816
FF_TILE = D_FF // 2
ALPHA = (2 * DEPTH) ** 0.25
LN_EPS = 1e-5
RMS_EPS = 1e-5
SSD_W = 512
SSD_HEADS = 8
SSD_CONV_K = 4
SSD_CONV_DIM = 1024
FOX_W = 256
FOX_HEADS = 4
FOX_HD = 64
FOX_SCALE = FOX_HD ** -0.5
SC_W = 256
SC_K = 3
D_IN_PROJ = 3084
P_Z, P_XBC, P_QKV, P_SC, P_PAD = 0, 512, 1536, 2304, 3072
D_PROJ_PAD = 3200
PAD_DT0, PAD_F0 = 0, 8
SSD_CHUNK = 256
ATT_BLOCK = 512
CUM_BLOCK = 256
LANES = 128
VMEM_LIMIT = 56 * 1024 * 1024

ADAM_LR, ADAM_B1, ADAM_B2, ADAM_EPS, ADAM_WD, ADAM_STEP = 0.001, 0.9, 0.999, 1e-08, 0.01, 10
NEG = -1e30


def _cp(*sem):
    return pltpu.CompilerParams(dimension_semantics=sem, vmem_limit_bytes=VMEM_LIMIT)


def _pick(n, cands):
    for c in cands:
        if n % c == 0:
            return c
    return n


def _dot(a, b):
    return lax.dot_general(a, b, (((1,), (0,)), ((), ())), preferred_element_type=F32)


def _dot_nt(a, b):
    return lax.dot_general(a, b, (((1,), (1,)), ((), ())), preferred_element_type=F32)


def _dot_tn(a, b):
    return lax.dot_general(a, b, (((0,), (0,)), ((), ())), preferred_element_type=F32)


def _sigmoid(x):
    return 0.5 * jnp.tanh(0.5 * x) + 0.5


def _softplus(x):
    return jnp.maximum(x, 0.0) + jnp.log(1.0 + jnp.exp(-jnp.abs(x)))


def _split3(v):
    h = v.astype(BF16)
    r = v - h.astype(F32)
    m = r.astype(BF16)
    l = (r - m.astype(F32)).astype(BF16)
    return h, m, l


def _tri_left(tri, v):
    h, m, l = _split3(v)
    return _dot(tri, h) + _dot(tri, m) + _dot(tri, l)


def _tri_right(v, tri):
    h, m, l = _split3(v)
    return _dot(h, tri) + _dot(m, tri) + _dot(l, tri)


def _matmul(a, b, mode, out_dtype, name, halves=None):
    assert a.dtype == BF16 and b.dtype == BF16, (name, a.dtype, b.dtype)
    if halves == "a":
        assert mode == "nt" and a.shape[0] == 2 and a.shape[2] == D_FF
        m, k, n = a.shape[1], 2 * D_FF, b.shape[0]
    elif halves == "b":
        assert mode == "tn" and b.shape[0] == 2 and b.shape[2] == D_FF
        (k, m), n = a.shape, 2 * D_FF
    elif mode == "nn":
        (m, k), n = a.shape, b.shape[1]
    elif mode == "nt":
        (m, k), n = a.shape, b.shape[0]
    else:
        (k, m), n = a.shape, b.shape[1]
    tm = m if m <= 1024 else _pick(m, (1024, 1408, 512, 256, 128))
    tn = n if n <= 1024 else _pick(n, (1408, 640, 512, 256, 128))
    tk = k if k <= 1024 else _pick(k, (1408, 1024, 640, 512, 256, 128))
    nk = k // tk
    if mode == "nn":
        dn = (((1,), (0,)), ((), ()))
        a_spec = pl.BlockSpec((tm, tk), lambda i, j, kk: (i, kk))
        b_spec = pl.BlockSpec((tk, tn), lambda i, j, kk: (kk, j))
    elif mode == "nt":
        dn = (((1,), (1,)), ((), ()))
        a_spec = pl.BlockSpec((tm, tk), lambda i, j, kk: (i, kk))
        b_spec = pl.BlockSpec((tn, tk), lambda i, j, kk: (j, kk))
    else:
        dn = (((0,), (0,)), ((), ()))
        a_spec = pl.BlockSpec((tk, tm), lambda i, j, kk: (kk, i))
        b_spec = pl.BlockSpec((tk, tn), lambda i, j, kk: (kk, j))
    per_half = D_FF // FF_TILE
    if halves == "a":
        assert tk == FF_TILE
        a_spec = pl.BlockSpec((None, tm, tk), lambda i, j, kk: (kk // per_half, i, kk % per_half))
    elif halves == "b":
        assert tn == FF_TILE
        b_spec = pl.BlockSpec((None, tk, tn), lambda i, j, kk: (j // per_half, kk, j % per_half))

    def body(a_ref, b_ref, o_ref, *acc):
        d = lax.dot_general(a_ref[...], b_ref[...], dn, preferred_element_type=F32)
        if nk == 1:
            o_ref[...] = d.astype(o_ref.dtype)
            return
        acc_ref, = acc
        kk = pl.program_id(2)

        @pl.when(kk == 0)
        def _():
            acc_ref[...] = d

        @pl.when((kk > 0) & (kk < nk - 1))
        def _():
            acc_ref[...] += d

        @pl.when(kk == nk - 1)
        def _():
            o_ref[...] = (acc_ref[...] + d).astype(o_ref.dtype)

    return pl.pallas_call(
        body, name=name, grid=(m // tm, n // tn, nk),
        in_specs=[a_spec, b_spec],
        out_specs=pl.BlockSpec((tm, tn), lambda i, j, kk: (i, j)),
        out_shape=jax.ShapeDtypeStruct((m, n), out_dtype),
        scratch_shapes=[pltpu.VMEM((tm, tn), F32)] if nk > 1 else [],
        compiler_params=_cp("parallel", "parallel", "arbitrary"),
    )(a, b)


def _rows(body, name, t, tb, row_in, full_in, row_out, acc_out):
    in_specs, args = [], []
    for r in row_in:
        if isinstance(r, tuple):
            arr, w, j = r
            in_specs.append(pl.BlockSpec((tb, w), functools.partial(lambda i, jj: (i, jj), jj=j)))
            args.append(arr)
        else:
            in_specs.append(pl.BlockSpec((tb, r.shape[1]), lambda i: (i, 0)))
            args.append(r)
    for f in full_in:
        in_specs.append(pl.BlockSpec(f.shape, functools.partial(lambda i, nd: (0,) * nd, nd=f.ndim)))
        args.append(f)
    out_specs = [pl.BlockSpec((tb, c), lambda i: (i, 0)) for c, _ in row_out]
    out_specs += [pl.BlockSpec(s, functools.partial(lambda i, nd: (0,) * nd, nd=len(s))) for s in acc_out]
    out_shape = [jax.ShapeDtypeStruct((t, c), dt) for c, dt in row_out]
    out_shape += [jax.ShapeDtypeStruct(s, F32) for s in acc_out]
    return pl.pallas_call(
        body, name=name, grid=(t // tb,), in_specs=in_specs, out_specs=out_specs, out_shape=out_shape,
        compiler_params=_cp("arbitrary"),
    )(*args)


def _ln_stats(r):
    mu = jnp.mean(r, axis=-1, keepdims=True)
    xc = r - mu
    var = jnp.mean(xc * xc, axis=-1, keepdims=True)
    rstd = lax.rsqrt(var + LN_EPS)
    return xc * rstd, rstd


def _ln_bwd(dout, xhat, rstd, g):
    dxh = dout * g
    m1 = jnp.mean(dxh, axis=-1, keepdims=True)
    m2 = jnp.mean(dxh * xhat, axis=-1, keepdims=True)
    return rstd * (dxh - m1 - xhat * m2)


def _first(i, acc_refs):
    @pl.when(i == 0)
    def _():
        for a in acc_refs:
            a[...] = jnp.zeros_like(a)


def _modulated(xv, m_ref):
    return (xv * (1.0 + m_ref[1:2, :]) + m_ref[0:1, :]).astype(BF16)


def _ln_in_fwd(x, g, b, next_mod3):
    t = x.shape[0]

    def body(x_ref, g_ref, b_ref, m_ref, o_ref, h_ref):
        xhat, _ = _ln_stats(x_ref[...])
        out = xhat * g_ref[...] + b_ref[...]
        o_ref[...] = out
        h_ref[...] = _modulated(out, m_ref)

    return _rows(body, "ln_in_fwd", t, 256, [x], [g, b, next_mod3], [(D, F32), (D, BF16)], [])


def _ln_in_bwd(dx0, x, g):
    t = x.shape[0]

    def body(d_ref, x_ref, g_ref, o_ref, acc_ref):
        _first(pl.program_id(0), [acc_ref])
        xhat, rstd = _ln_stats(x_ref[...])
        d = d_ref[...]
        o_ref[...] = _ln_bwd(d, xhat, rstd, g_ref[...])
        acc_ref[0:1, :] += jnp.sum(d * xhat, axis=0, keepdims=True)
        acc_ref[1:2, :] += jnp.sum(d, axis=0, keepdims=True)

    return _rows(body, "ln_in_bwd", t, 256, [dx0, x], [g], [(D, F32)], [(2, D)])


def _modulate_bwd(dxres, dh, xin, mod3, name):
    t = xin.shape[0]

    def body(r_ref, dh_ref, x_ref, m_ref, o_ref, acc_ref):
        _first(pl.program_id(0), [acc_ref])
        dh_v = dh_ref[...]
        o_ref[...] = r_ref[...] + dh_v * (1.0 + m_ref[1:2, :])
        acc_ref[0:1, :] += jnp.sum(dh_v, axis=0, keepdims=True)
        acc_ref[1:2, :] += jnp.sum(dh_v * x_ref[...], axis=0, keepdims=True)

    return _rows(body, name, t, 256, [dxres, dh, xin], [mod3], [(D, F32)], [(2, D)])


def _ffn_in_swiglu(h, w_in, name):
    t = h.shape[0]
    tm = _pick(t, (512, 256, 128))
    nj = D_FF // FF_TILE

    def body(h_ref, wg_ref, wu_ref, u_ref, a_ref):
        hv = h_ref[...]
        gate = _dot(hv, wg_ref[...])
        up = _dot(hv, wu_ref[...])
        u_ref[0] = gate.astype(BF16)
        u_ref[1] = up.astype(BF16)
        a_ref[...] = (gate * _sigmoid(gate) * up).astype(BF16)

    return pl.pallas_call(
        body, name=name, grid=(nj, t // tm),
        in_specs=[pl.BlockSpec((tm, D), lambda j, i: (i, 0)),
                  pl.BlockSpec((D, FF_TILE), lambda j, i: (0, j)),
                  pl.BlockSpec((D, FF_TILE), lambda j, i: (0, nj + j))],
        out_specs=[pl.BlockSpec((2, tm, FF_TILE), lambda j, i: (0, i, j)),
                   pl.BlockSpec((tm, FF_TILE), lambda j, i: (i, j))],
        out_shape=[jax.ShapeDtypeStruct((2, t, D_FF), BF16), jax.ShapeDtypeStruct((t, D_FF), BF16)],
        compiler_params=_cp("parallel", "parallel"),
    )(h, w_in, w_in)


def _ffn_out_dx_swiglu(dy, w_out, u, name):
    t = dy.shape[0]
    tm = _pick(t, (512, 256, 128))
    nj = D_FF // FF_TILE

    def body(dy_ref, w_ref, u_ref, du_ref):
        da = _dot_nt(dy_ref[...], w_ref[...])
        gate = u_ref[0].astype(F32)
        up = u_ref[1].astype(F32)
        sg = _sigmoid(gate)
        du_ref[0] = (da * up * (sg * (1.0 + gate * (1.0 - sg)))).astype(BF16)
        du_ref[1] = (da * gate * sg).astype(BF16)

    blk3 = pl.BlockSpec((2, tm, FF_TILE), lambda j, i: (0, i, j))
    return pl.pallas_call(
        body, name=name, grid=(nj, t // tm),
        in_specs=[pl.BlockSpec((tm, D), lambda j, i: (i, 0)),
                  pl.BlockSpec((FF_TILE, D), lambda j, i: (j, 0)), blk3],
        out_specs=blk3,
        out_shape=jax.ShapeDtypeStruct((2, t, D_FF), BF16),
        compiler_params=_cp("parallel", "parallel"),
    )(dy, w_out, u)


def _res_ln(xin, y, mod3, lg, lb, factor, name, next_mod3=None):
    t = xin.shape[0]

    def body(x_ref, y_ref, m_ref, g_ref, b_ref, *rest):
        r = ALPHA * x_ref[...] + (factor * m_ref[2:3, :]) * y_ref[...]
        xhat, _ = _ln_stats(r)
        out = xhat * g_ref[...] + b_ref[...]
        if next_mod3 is None:
            rest[0][...] = out
        else:
            rest[1][...] = out
            rest[2][...] = _modulated(out, rest[0])

    if next_mod3 is None:
        return _rows(body, name, t, 256, [xin, y], [mod3, lg, lb], [(D, F32)], [])[0], None
    return _rows(body, name, t, 256, [xin, y], [mod3, lg, lb, next_mod3], [(D, F32), (D, BF16)], [])


def _mod_res_bwd(dxres, dh, mod3, xin_p, y_p, mod3_p, lg_p, lb_p, factor_p, name):
    t = dxres.shape[0]

    def body(r_ref, dh_ref, xp_ref, yp_ref, m_ref, mp_ref, g_ref, b_ref, dres_ref, dy_ref, acc_ref):
        _first(pl.program_id(0), [acc_ref])
        gate = factor_p * mp_ref[2:3, :]
        yv = yp_ref[...]
        xhat, rstd = _ln_stats(ALPHA * xp_ref[...] + gate * yv)
        xin = xhat * g_ref[...] + b_ref[...]
        dh_v = dh_ref[...]
        d = r_ref[...] + dh_v * (1.0 + m_ref[1:2, :])
        dr = _ln_bwd(d, xhat, rstd, g_ref[...])
        dres_ref[...] = ALPHA * dr
        dy_ref[...] = (gate * dr).astype(BF16)
        acc_ref[0:1, :] += jnp.sum(dh_v, axis=0, keepdims=True)
        acc_ref[1:2, :] += jnp.sum(dh_v * xin, axis=0, keepdims=True)
        acc_ref[2:3, :] += jnp.sum(d * xhat, axis=0, keepdims=True)
        acc_ref[3:4, :] += jnp.sum(d, axis=0, keepdims=True)
        acc_ref[4:5, :] += jnp.sum(factor_p * yv * dr, axis=0, keepdims=True)

    return _rows(body, name, t, 256, [dxres, dh, xin_p, y_p], [mod3, mod3_p, lg_p, lb_p],
                 [(D, F32), (D, BF16)], [(5, D)])


def _res_ln_bwd(dout, xin, y, mod3, lg, factor, name):
    t = xin.shape[0]

    def body(d_ref, x_ref, y_ref, m_ref, g_ref, dres_ref, dy_ref, acc_ref):
        _first(pl.program_id(0), [acc_ref])
        gate = factor * m_ref[2:3, :]
        yv = y_ref[...]
        r = ALPHA * x_ref[...] + gate * yv
        xhat, rstd = _ln_stats(r)
        d = d_ref[...]
        dr = _ln_bwd(d, xhat, rstd, g_ref[...])
        dres_ref[...] = ALPHA * dr
        dy_ref[...] = (gate * dr).astype(BF16)
        acc_ref[0:1, :] += jnp.sum(d * xhat, axis=0, keepdims=True)
        acc_ref[1:2, :] += jnp.sum(d, axis=0, keepdims=True)
        acc_ref[2:3, :] += jnp.sum(factor * yv * dr, axis=0, keepdims=True)

    return _rows(body, name, t, 256, [dout, xin, y], [mod3, lg], [(D, F32), (D, BF16)], [(3, D)])


def _loss_head(xf, tgt):
    t = xf.shape[0]

    def body(x_ref, t_ref, d_ref, acc_ref):
        _first(pl.program_id(0), [acc_ref])
        e = x_ref[...] - t_ref[...]
        d_ref[...] = e * (1.0 / D)
        part = 0.5 * jnp.sum(jnp.mean(e * e, axis=-1, keepdims=True), axis=0, keepdims=True)
        acc_ref[...] += jnp.broadcast_to(part, acc_ref.shape)

    return _rows(body, "loss_head", t, 256, [xf, tgt], [], [(D, F32)], [(1, LANES)])


def _silu_bf16(c_all):
    def body(c_ref, o_ref):
        v = c_ref[...]
        o_ref[...] = (v * _sigmoid(v)).astype(BF16)

    return _rows(body, "silu_c", c_all.shape[0], c_all.shape[0], [c_all], [], [(c_all.shape[1], BF16)], [])[0]


def _sum_rows(v, name):
    r, n = v.shape
    tn = _pick(n, (8192, 4096, 2048, 1024, 512, 256, 128))

    def body(v_ref, o_ref):
        acc = v_ref[0:1, :]
        for k in range(1, r):
            acc = acc + v_ref[k:k + 1, :]
        o_ref[...] = acc

    return pl.pallas_call(
        body, name=name, grid=(n // tn,),
        in_specs=[pl.BlockSpec((r, tn), lambda j: (0, j))],
        out_specs=pl.BlockSpec((1, tn), lambda j: (0, j)),
        out_shape=jax.ShapeDtypeStruct((1, n), F32),
        compiler_params=_cp("parallel"),
    )(v)


def _elementwise(fn, name, ins, out_dtypes):
    r, c = ins[0].shape
    tb = _pick(r, (128, 64, 32, 16, 8))
    n_in = len(ins)

    def body(*refs):
        outs = fn(*[x[...] for x in refs[:n_in]])
        for o_ref, o in zip(refs[n_in:], outs):
            o_ref[...] = o.astype(o_ref.dtype)

    spec = pl.BlockSpec((tb, c), lambda i: (i, 0))
    return pl.pallas_call(
        body, name=name, grid=(r // tb,), in_specs=[spec] * n_in, out_specs=[spec] * len(out_dtypes),
        out_shape=[jax.ShapeDtypeStruct((r, c), dt) for dt in out_dtypes],
        compiler_params=_cp("parallel"),
    )(*ins)


def _adamw_math(w, g, m, v):
    m = ADAM_B1 * m + (1.0 - ADAM_B1) * g
    v = ADAM_B2 * v + (1.0 - ADAM_B2) * (g * g)
    m_hat = m / (1.0 - ADAM_B1 ** ADAM_STEP)
    v_hat = v / (1.0 - ADAM_B2 ** ADAM_STEP)
    delta = -ADAM_LR * (m_hat / (jnp.sqrt(v_hat) + ADAM_EPS) + ADAM_WD * w)
    return delta, m, v


def _adamw(w, g, m, v, name):
    shape = w.shape
    c = shape[-1]
    w2, g2, m2, v2 = (a.reshape(-1, c) for a in (w, g, m, v))
    outs = _elementwise(_adamw_math, name, [w2, g2, m2, v2], [F32, F32, F32])
    return tuple(o.reshape(shape) for o in outs)


def _remote_exchange(ins, plan, peers_of, name):
    n_in, n_out = len(ins), len(plan)

    def body(*refs):
        in_refs, out_refs = refs[:n_in], refs[n_in:n_in + n_out]
        send_sems, recv_sems = refs[n_in + n_out], refs[n_in + n_out + 1]
        peers = peers_of(lax.axis_index("x"), lax.axis_index("y"), lax.axis_index("c"))
        copies = [
            pltpu.make_async_remote_copy(
                src_ref=in_refs[src], dst_ref=out_refs[k], send_sem=send_sems.at[k], recv_sem=recv_sems.at[k],
                device_id=peers[peer], device_id_type=MESH)
            for k, (peer, src) in enumerate(plan)
        ]
        for cp in copies:
            cp.start()
        for cp in copies:
            cp.wait()

    any_spec = pl.BlockSpec(memory_space=pl.ANY)
    return list(pl.pallas_call(
        body, name=name,
        in_specs=[any_spec] * n_in, out_specs=[any_spec] * n_out,
        out_shape=[jax.ShapeDtypeStruct(ins[src].shape, ins[src].dtype) for _, src in plan],
        scratch_shapes=[pltpu.SemaphoreType.DMA((n_out,)), pltpu.SemaphoreType.DMA((n_out,))],
    )(*ins))


def _sibling(x, y, c):
    return [(x, y, 1 - c)]


def _other_chips(x, y, c):
    return [(1 - x, y, c), (x, 1 - y, c), (1 - x, 1 - y, c)]


def _swap_sibling(arrs, name):
    return _remote_exchange(arrs, [(0, i) for i in range(len(arrs))], _sibling, name)


def _bcast_chips(arrs, name):
    n = len(arrs)
    out = _remote_exchange(arrs, [(p, i) for p in range(3) for i in range(n)], _other_chips, name)
    return [out[p * n:(p + 1) * n] for p in range(3)]


def _by_chip(me, own, got3):
    by_rel = [own, got3[0], got3[1], got3[2]]
    rel_bits = (0, 2, 1, 3)

    def branch(m):
        def f(ops):
            return [ops[rel_bits.index(i ^ m)] for i in range(4)]
        return f

    return lax.switch(me, [branch(m) for m in range(4)], by_rel)


BIG = ("ffn1_w_in", "ffn1_w_out", "mix_w_in", "mix_w_out", "ffn2_w_in", "ffn2_w_out")
BY_COLUMNS = ("ffn1_w_in", "ffn2_w_in")


def _layer_shape(n, shard_shape):
    r, cs = shard_shape
    return (r, 4 * cs) if n in BY_COLUMNS else (4, r, cs)


def _chip_ids(x, y):
    chips = [(1 - x, y), (x, 1 - y), (1 - x, 1 - y)]
    return chips, [2 * cx + cy for cx, cy in chips]


def _half_slot(ref, n, chip, h):
    if n in BY_COLUMNS:
        hr, w = ref.shape[0] // 2, ref.shape[1] // 4
        return ref.at[pl.ds(pl.multiple_of(h * hr, 16), hr), pl.ds(pl.multiple_of(chip * w, LANES), w)]
    hr = ref.shape[1] // 2
    return ref.at[chip, pl.ds(pl.multiple_of(h * hr, 16), hr)]


def _gather_body(names, layer, in_refs, out_refs, sems, handshake):
    send_chip, recv_chip, send_sib, recv_sib, local, send_fwd, recv_fwd = sems
    n_w = len(names)
    x, y, c = lax.axis_index("x"), lax.axis_index("y"), lax.axis_index("c")
    me = 2 * x + y
    chips, chip_idx = _chip_ids(x, y)
    sibling = (x, y, 1 - c)
    if handshake:
        _shake_hands([sibling] + [(*ch, c) for ch in chips])
    remote = _remote

    sends, own_copies = [], []
    for i, n in enumerate(names):
        shard = in_refs[i].at[layer]
        hr = shard.shape[0] // 2
        src = shard.at[pl.ds(pl.multiple_of(c * hr, 16), hr)]
        mine = _half_slot(out_refs[i], n, me, c)
        own = pltpu.make_async_copy(src, mine, local.at[i])
        own.start()
        own_copies.append(own)
        sends.append(remote(src, mine, send_sib.at[i], recv_sib.at[i], sibling))
        sends[-1].start()
        for p in range(3):
            k = p * n_w + i
            sends.append(remote(src, mine, send_chip.at[k], recv_chip.at[k], (*chips[p], c)))
            sends[-1].start()
    for p in range(3):
        for i, n in enumerate(names):
            k = p * n_w + i
            landed = _half_slot(out_refs[i], n, chip_idx[p], c)
            remote(landed, landed, send_chip.at[k], recv_chip.at[k], sibling).wait_recv()
            sends.append(remote(landed, landed, send_fwd.at[k], recv_fwd.at[k], sibling))
            sends[-1].start()
    for i, n in enumerate(names):
        theirs = _half_slot(out_refs[i], n, me, 1 - c)
        remote(theirs, theirs, send_sib.at[i], recv_sib.at[i], sibling).wait_recv()
        for p in range(3):
            k = p * n_w + i
            theirs = _half_slot(out_refs[i], n, chip_idx[p], 1 - c)
            remote(theirs, theirs, send_fwd.at[k], recv_fwd.at[k], sibling).wait_recv()
    for own in own_copies:
        own.wait()
    for cp in sends:
        cp.wait_send()


def _gather_weights(names, shards, layer, after, name, sequencer_id):
    n_w = len(names)

    def body(in_refs, out_refs, sems, handshake):
        _gather_body(names, layer, in_refs, out_refs, sems, handshake)

    dma = pltpu.SemaphoreType.DMA
    sems = [dma((3 * n_w,)), dma((3 * n_w,)), dma((n_w,)), dma((n_w,)), dma((n_w,)), dma((3 * n_w,)), dma((3 * n_w,))]
    shapes = [jax.ShapeDtypeStruct(_layer_shape(n, s.shape[1:]), BF16) for n, s in zip(names, shards)]
    return _comm_call(body, list(shards) + list(after), shapes, sems, name, sequencer_id)


def _comm_call(body_fn, ins, out_shapes, sem_types, name, sequencer_id):
    n_in, n_out = len(ins), len(out_shapes)
    sequencer = sequencer_id is not None

    def body(*refs):
        body_fn(refs[:n_in], refs[n_in:n_in + n_out], refs[n_in + n_out:], sequencer)

    if sequencer:
        return list(pl.kernel(
            body, name=name, out_type=out_shapes,
            mesh=plsc.ScalarSubcoreMesh(axis_name="sequencer", num_cores=1), scratch_types=sem_types,
            compiler_params=pltpu.CompilerParams(collective_id=sequencer_id),
        )(*ins))
    any_spec = pl.BlockSpec(memory_space=pl.ANY)
    return list(pl.pallas_call(
        body, name=name, in_specs=[any_spec] * n_in, out_specs=[any_spec] * n_out, out_shape=out_shapes,
        scratch_shapes=sem_types,
    )(*ins))


def _shake_hands(peers):
    barrier = pltpu.get_barrier_semaphore()
    for peer in peers:
        pl.semaphore_signal(barrier, inc=1, device_id=peer, device_id_type=MESH)
    pl.semaphore_wait(barrier, len(peers))


def _remote(src, dst, s_sem, r_sem, to):
    return pltpu.make_async_remote_copy(src_ref=src, dst_ref=dst, send_sem=s_sem, recv_sem=r_sem,
                                        device_id=to, device_id_type=MESH)


def _rows_half(ref, n, h):
    if n in BY_COLUMNS:
        hr = ref.shape[0] // 2
        return ref.at[pl.ds(pl.multiple_of(h * hr, 16), hr)]
    hr = ref.shape[1] // 2
    return ref.at[:, pl.ds(pl.multiple_of(h * hr, 16), hr)]


def _half_form_shape(n, wire_shape):
    if n in BY_COLUMNS:
        return (wire_shape[0] // 2, wire_shape[1])
    return (wire_shape[0], wire_shape[1] // 2, wire_shape[2])


def _grad_swap_halves(names, wire, name, sequencer_id):
    n_w = len(names)

    def body(in_refs, out_refs, sems, handshake):
        send, recv = sems
        x, y, c = lax.axis_index("x"), lax.axis_index("y"), lax.axis_index("c")
        sibling = (x, y, 1 - c)
        if handshake:
            _shake_hands([sibling])
        cps = [_remote(_rows_half(in_refs[i], n, 1 - c), out_refs[i], send.at[i], recv.at[i], sibling)
               for i, n in enumerate(names)]
        for cp in cps:
            cp.start()
        for cp in cps:
            cp.wait()

    shapes = [jax.ShapeDtypeStruct(_half_form_shape(n, w.shape), w.dtype) for n, w in zip(names, wire)]
    dma = pltpu.SemaphoreType.DMA
    return _comm_call(body, list(wire), shapes, [dma((n_w,)), dma((n_w,))], name, sequencer_id)


def _grad_add_halves(g, got, ci_arr, after, name):
    g3 = g if g.ndim == 3 else g[None]
    r3 = got if got.ndim == 3 else got[None]
    s, hr, cc = r3.shape
    tb = _pick(hr, (256, 176, 128))
    nb = hr // tb

    def body(s_ref, g_ref, r_ref, after_ref, o_ref):
        o_ref[...] = (g_ref[...].astype(F32) + r_ref[...].astype(F32)).astype(BF16)

    out = pl.pallas_call(
        body, name=name,
        grid_spec=pltpu.PrefetchScalarGridSpec(
            num_scalar_prefetch=1, grid=(nb,),
            in_specs=[pl.BlockSpec((s, tb, cc), lambda i, sc: (0, sc[0] * nb + i, 0)),
                      pl.BlockSpec((s, tb, cc), lambda i, sc: (0, i, 0)),
                      pl.BlockSpec(memory_space=pl.ANY)],
            out_specs=pl.BlockSpec((s, tb, cc), lambda i, sc: (0, i, 0))),
        out_shape=jax.ShapeDtypeStruct(r3.shape, BF16),
        compiler_params=_cp("arbitrary"),
    )(ci_arr, g3, r3, after)
    return out.reshape(got.shape)


def _grad_scatter_halves(names, parts, name, sequencer_id):
    n_w = len(names)

    def slot(ref, n, chip):
        if n in BY_COLUMNS:
            w = ref.shape[1] // 4
            return ref.at[:, pl.ds(pl.multiple_of(chip * w, LANES), w)]
        return ref.at[chip]

    def body(in_refs, out_refs, sems, handshake):
        send, recv = sems
        x, y, c = lax.axis_index("x"), lax.axis_index("y"), lax.axis_index("c")
        chips, chip_idx = _chip_ids(x, y)
        if handshake:
            _shake_hands([(*ch, c) for ch in chips])
        cps = []
        for p in range(3):
            for i, n in enumerate(names):
                k = p * n_w + i
                cps.append(_remote(slot(in_refs[i], n, chip_idx[p]), out_refs[k], send.at[k], recv.at[k],
                                   (*chips[p], c)))
        for cp in cps:
            cp.start()
        for cp in cps:
            cp.wait()

    def slot_shape(n, a):
        return (a.shape[0], a.shape[1] // 4) if n in BY_COLUMNS else a.shape[1:]

    shapes = [jax.ShapeDtypeStruct(slot_shape(n, a), BF16) for _ in range(3) for n, a in zip(names, parts)]
    dma = pltpu.SemaphoreType.DMA
    out = _comm_call(body, list(parts), shapes, [dma((3 * n_w,)), dma((3 * n_w,))], name, sequencer_id)
    return [out[p * n_w:(p + 1) * n_w] for p in range(3)]


def _grad_add_slots(part, got3, n, me_arr, after, name):
    hr, cs = got3[0].shape
    tb = _pick(hr, (256, 176, 128))

    def body(s_ref, own_ref, a_ref, b_ref, c_ref, after_ref, out_ref):
        acc = own_ref[...].astype(F32) + a_ref[...].astype(F32)
        out_ref[...] = (acc + b_ref[...].astype(F32)) + c_ref[...].astype(F32)

    if n in BY_COLUMNS:
        own_spec = pl.BlockSpec((tb, cs), lambda i, s: (i, s[0]))
    else:
        own_spec = pl.BlockSpec((None, tb, cs), lambda i, s: (s[0], i, 0))
    plain = pl.BlockSpec((tb, cs), lambda i, s: (i, 0))
    return pl.pallas_call(
        body, name=name,
        grid_spec=pltpu.PrefetchScalarGridSpec(
            num_scalar_prefetch=1, grid=(hr // tb,),
            in_specs=[own_spec, plain, plain, plain, pl.BlockSpec(memory_space=pl.ANY)],
            out_specs=plain),
        out_shape=jax.ShapeDtypeStruct((hr, cs), F32),
        compiler_params=_cp("arbitrary"),
    )(me_arr, part, *got3, after)


def _grad_swap_reduced(halves, name, sequencer_id):
    n_w = len(halves)

    def body(in_refs, out_refs, sems, handshake):
        send, recv = sems
        x, y, c = lax.axis_index("x"), lax.axis_index("y"), lax.axis_index("c")
        sibling = (x, y, 1 - c)
        if handshake:
            _shake_hands([sibling])
        cps = [_remote(in_refs[i], out_refs[i], send.at[i], recv.at[i], sibling) for i in range(n_w)]
        for cp in cps:
            cp.start()
        for cp in cps:
            cp.wait()

    shapes = [jax.ShapeDtypeStruct(h.shape, F32) for h in halves]
    dma = pltpu.SemaphoreType.DMA
    return _comm_call(body, list(halves), shapes, [dma((n_w,)), dma((n_w,))], name, sequencer_id)


def _adamw_layers(w, own, other, m, v, ci_arr, name):
    _, rs, cs = w.shape
    hr = rs // 2
    tb = _pick(hr, (256, 176, 128))
    nbh = hr // tb

    def body(s_ref, w_ref, o0, t0, o1, t1, m_ref, v_ref, g_out, d_out, m_out, v_out):
        mine = (pl.program_id(1) // nbh) == s_ref[0]
        g = jnp.where(pl.program_id(0) == 0, jnp.where(mine, o0[...], t0[...]), jnp.where(mine, o1[...], t1[...]))
        g_out[...] = g
        d_out[...], m_out[...], v_out[...] = _adamw_math(w_ref[...], g, m_ref[...], v_ref[...])

    both = pl.BlockSpec((None, tb, cs), lambda l, i, s: (l, i, 0))

    def half(layer, is_own):
        def index(l, i, s):
            own_block = (i // nbh) == s[0]
            use = (l == layer) & (own_block if is_own else jnp.logical_not(own_block))
            return (jnp.where(use, i % nbh, 0), 0)
        return pl.BlockSpec((tb, cs), index)

    return pl.pallas_call(
        body, name=name,
        grid_spec=pltpu.PrefetchScalarGridSpec(
            num_scalar_prefetch=1, grid=(DEPTH, rs // tb),
            in_specs=[both, half(0, True), half(0, False), half(1, True), half(1, False), both, both],
            out_specs=[both] * 4),
        out_shape=[jax.ShapeDtypeStruct(w.shape, F32)] * 4,
        compiler_params=_cp("arbitrary", "arbitrary"),
    )(ci_arr, w, own[0], other[0], own[1], other[1], m, v)


def _shift_down(v, s, t_iota):
    return jnp.where(t_iota >= s, pltpu.roll(v, s, 0), 0.0)


def _shift_up(v, s, t_iota, t):
    return jnp.where(t_iota < t - s, pltpu.roll(v, t - s, 0), 0.0)


def _ssd_conv_fwd(proj, w, b):
    t = proj.shape[0]
    k_w = SSD_CONV_K

    def body(x_ref, w_ref, b_ref, o_ref):
        x = x_ref[...]
        ti = lax.broadcasted_iota(jnp.int32, x.shape, 0)
        pre = x * w_ref[k_w - 1:k_w, :] + b_ref[...]
        for s in range(1, k_w):
            pre = pre + _shift_down(x, s, ti) * w_ref[k_w - 1 - s:k_w - s, :]
        o_ref[...] = pre * _sigmoid(pre)

    off = P_XBC // LANES
    return pl.pallas_call(
        body, name="ssd_conv_fwd", grid=(SSD_CONV_DIM // LANES,),
        in_specs=[pl.BlockSpec((t, LANES), lambda j: (0, off + j)),
                  pl.BlockSpec((k_w, LANES), lambda j: (0, j)),
                  pl.BlockSpec((1, LANES), lambda j: (0, j))],
        out_specs=pl.BlockSpec((t, LANES), lambda j: (0, j)),
        out_shape=jax.ShapeDtypeStruct((t, SSD_CONV_DIM), F32),
        compiler_params=_cp("parallel"),
    )(proj, w, b)


def _ssd_conv_bwd(dxbc, proj, w, b):
    t = proj.shape[0]
    k_w = SSD_CONV_K

    def body(d_ref, x_ref, w_ref, b_ref, dx_ref, dw_ref, db_ref):
        x = x_ref[...]
        ti = lax.broadcasted_iota(jnp.int32, x.shape, 0)
        shifted = [x] + [_shift_down(x, s, ti) for s in range(1, k_w)]
        pre = b_ref[...] + shifted[0] * w_ref[k_w - 1:k_w, :]
        for s in range(1, k_w):
            pre = pre + shifted[s] * w_ref[k_w - 1 - s:k_w - s, :]
        sg = _sigmoid(pre)
        dpre = d_ref[...] * (sg * (1.0 + pre * (1.0 - sg)))
        db_ref[...] = jnp.sum(dpre, axis=0, keepdims=True)
        dx = dpre * w_ref[k_w - 1:k_w, :]
        for s in range(k_w):
            dw_ref[k_w - 1 - s:k_w - s, :] = jnp.sum(dpre * shifted[s], axis=0, keepdims=True)
            if s:
                dx = dx + _shift_up(dpre, s, ti, t) * w_ref[k_w - 1 - s:k_w - s, :]
        dx_ref[...] = dx

    off = P_XBC // LANES
    return pl.pallas_call(
        body, name="ssd_conv_bwd", grid=(SSD_CONV_DIM // LANES,),
        in_specs=[pl.BlockSpec((t, LANES), lambda j: (0, j)),
                  pl.BlockSpec((t, LANES), lambda j: (0, off + j)),
                  pl.BlockSpec((k_w, LANES), lambda j: (0, j)),
                  pl.BlockSpec((1, LANES), lambda j: (0, j))],
        out_specs=[pl.BlockSpec((t, LANES), lambda j: (0, j)),
                   pl.BlockSpec((k_w, LANES), lambda j: (0, j)),
                   pl.BlockSpec((1, LANES), lambda j: (0, j))],
        out_shape=[jax.ShapeDtypeStruct((t, SSD_CONV_DIM), F32),
                   jax.ShapeDtypeStruct((k_w, SSD_CONV_DIM), F32),
                   jax.ShapeDtypeStruct((1, SSD_CONV_DIM), F32)],
        compiler_params=_cp("parallel"),
    )(dxbc, proj, w, b)


def _shortconv_fwd(proj, w):
    t = proj.shape[0]
    nb = SC_W // LANES
    off = P_SC // LANES

    def body(b_ref, c_ref, x_ref, w_ref, o_ref):
        u = c_ref[...] * x_ref[...]
        ti = lax.broadcasted_iota(jnp.int32, u.shape, 0)
        cv = u * w_ref[SC_K - 1:SC_K, :]
        for s in range(1, SC_K):
            cv = cv + _shift_down(u, s, ti) * w_ref[SC_K - 1 - s:SC_K - s, :]
        o_ref[...] = (b_ref[...] * cv).astype(BF16)

    return pl.pallas_call(
        body, name="shortconv_fwd", grid=(nb,),
        in_specs=[pl.BlockSpec((t, LANES), lambda j: (0, off + j)),
                  pl.BlockSpec((t, LANES), lambda j: (0, off + nb + j)),
                  pl.BlockSpec((t, LANES), lambda j: (0, off + 2 * nb + j)),
                  pl.BlockSpec((SC_K, LANES), lambda j: (0, j))],
        out_specs=pl.BlockSpec((t, LANES), lambda j: (0, j)),
        out_shape=jax.ShapeDtypeStruct((t, SC_W), BF16),
        compiler_params=_cp("parallel"),
    )(proj, proj, proj, w)


def _shortconv_bwd(dy, dy_off, proj, w):
    t = proj.shape[0]
    nb = SC_W // LANES
    off = P_SC // LANES
    doff = dy_off // LANES

    def body(d_ref, b_ref, c_ref, x_ref, w_ref, db_ref, dc_ref, dx_ref, dw_ref):
        cg, xin = c_ref[...], x_ref[...]
        u = cg * xin
        ti = lax.broadcasted_iota(jnp.int32, u.shape, 0)
        shifted = [u] + [_shift_down(u, s, ti) for s in range(1, SC_K)]
        cv = shifted[0] * w_ref[SC_K - 1:SC_K, :]
        for s in range(1, SC_K):
            cv = cv + shifted[s] * w_ref[SC_K - 1 - s:SC_K - s, :]
        d = d_ref[...]
        db_ref[...] = (d * cv).astype(BF16)
        dcv = d * b_ref[...]
        du = dcv * w_ref[SC_K - 1:SC_K, :]
        for s in range(SC_K):
            dw_ref[SC_K - 1 - s:SC_K - s, :] = jnp.sum(dcv * shifted[s], axis=0, keepdims=True)
            if s:
                du = du + _shift_up(dcv, s, ti, t) * w_ref[SC_K - 1 - s:SC_K - s, :]
        dc_ref[...] = (du * xin).astype(BF16)
        dx_ref[...] = (du * cg).astype(BF16)

    tile = pl.BlockSpec((t, LANES), lambda j: (0, j))
    outs = pl.pallas_call(
        body, name="shortconv_bwd", grid=(nb,),
        in_specs=[pl.BlockSpec((t, LANES), lambda j: (0, doff + j)),
                  pl.BlockSpec((t, LANES), lambda j: (0, off + j)),
                  pl.BlockSpec((t, LANES), lambda j: (0, off + nb + j)),
                  pl.BlockSpec((t, LANES), lambda j: (0, off + 2 * nb + j)),
                  pl.BlockSpec((SC_K, LANES), lambda j: (0, j))],
        out_specs=[tile, tile, tile, pl.BlockSpec((SC_K, LANES), lambda j: (0, j))],
        out_shape=[jax.ShapeDtypeStruct((t, SC_W), BF16)] * 3 + [jax.ShapeDtypeStruct((SC_K, SC_W), F32)],
        compiler_params=_cp("parallel"),
    )(dy, proj, proj, proj, w)
    return outs


def _cum_logf(proj, fbias_row):
    t = proj.shape[0]
    blk = CUM_BLOCK

    def body(p_ref, b_ref, o_ref, carry_ref):
        i = pl.program_id(0)

        @pl.when(i == 0)
        def _():
            carry_ref[...] = jnp.zeros_like(carry_ref)

        lf = -_softplus(-(p_ref[...] + b_ref[...]))
        r = lax.broadcasted_iota(jnp.int32, (blk, blk), 0)
        c = lax.broadcasted_iota(jnp.int32, (blk, blk), 1)
        tri = (r >= c).astype(BF16)
        o_ref[...] = _tri_left(tri, lf) + carry_ref[...]
        carry_ref[...] = o_ref[blk - 1:blk, :]

    return pl.pallas_call(
        body, name="cum_logf", grid=(t // blk,),
        in_specs=[pl.BlockSpec((blk, LANES), lambda i: (i, P_PAD // LANES)),
                  pl.BlockSpec((1, LANES), lambda i: (0, 0))],
        out_specs=pl.BlockSpec((blk, LANES), lambda i: (i, 0)),
        out_shape=jax.ShapeDtypeStruct((t, LANES), F32),
        scratch_shapes=[pltpu.VMEM((1, LANES), F32)],
        compiler_params=_cp("arbitrary"),
    )(proj, fbias_row)


def _pad_block_bwd(dcf, ddt, proj, fbias_row):
    t = proj.shape[0]
    blk = CUM_BLOCK
    nb = t // blk

    def body(dcf_ref, ddt_ref, p_ref, b_ref, o_ref, db_ref, carry_ref):
        i = pl.program_id(0)

        @pl.when(i == 0)
        def _():
            carry_ref[...] = jnp.zeros_like(carry_ref)
            db_ref[...] = jnp.zeros_like(db_ref)

        r = lax.broadcasted_iota(jnp.int32, (blk, blk), 0)
        c = lax.broadcasted_iota(jnp.int32, (blk, blk), 1)
        tri = (r <= c).astype(BF16)
        rev = _tri_left(tri, dcf_ref[...]) + carry_ref[...]
        carry_ref[...] = jnp.sum(dcf_ref[...], axis=0, keepdims=True) + carry_ref[...]
        lane = lax.broadcasted_iota(jnp.int32, (blk, LANES), 1)
        is_f = (lane >= PAD_F0) & (lane < PAD_F0 + FOX_HEADS)
        df = jnp.where(is_f, rev * _sigmoid(-(p_ref[...] + b_ref[...])), 0.0)
        db_ref[...] += jnp.sum(df, axis=0, keepdims=True)
        o_ref[...] = jnp.where(lane < PAD_DT0 + SSD_HEADS, ddt_ref[...], df).astype(BF16)

    return pl.pallas_call(
        body, name="pad_block_bwd", grid=(nb,),
        in_specs=[pl.BlockSpec((blk, LANES), lambda i: (nb - 1 - i, 0)),
                  pl.BlockSpec((blk, LANES), lambda i: (nb - 1 - i, 0)),
                  pl.BlockSpec((blk, LANES), lambda i: (nb - 1 - i, P_PAD // LANES)),
                  pl.BlockSpec((1, LANES), lambda i: (0, 0))],
        out_specs=[pl.BlockSpec((blk, LANES), lambda i: (nb - 1 - i, 0)),
                   pl.BlockSpec((1, LANES), lambda i: (0, 0))],
        out_shape=[jax.ShapeDtypeStruct((t, LANES), BF16), jax.ShapeDtypeStruct((1, LANES), F32)],
        scratch_shapes=[pltpu.VMEM((1, LANES), F32)],
        compiler_params=_cp("arbitrary"),
    )(dcf, ddt, proj, fbias_row)


def _att_scores(q, k, ck, diagonal, blk):
    s = _dot_nt(q, k) - ck
    if diagonal:
        r = lax.broadcasted_iota(jnp.int32, (blk, blk), 0)
        c = lax.broadcasted_iota(jnp.int32, (blk, blk), 1)
        s = jnp.where(r >= c, s, NEG)
    return s


def _fox_fwd(q, k, v, cf_row):
    h, t, hd = q.shape
    blk = min(ATT_BLOCK, t)
    nb = t // blk

    def body(q_ref, k_ref, v_ref, ck_ref, o_ref, lse_ref):
        qi = pl.program_id(1)
        qv = q_ref[...]

        def step(j, carry, diagonal):
            m, l, acc = carry
            off = pl.multiple_of(j * blk, blk)
            s = _att_scores(qv, k_ref[pl.ds(off, blk), :], ck_ref[:, pl.ds(off, blk)], diagonal, blk)
            m_new = jnp.maximum(m, jnp.max(s, axis=1, keepdims=True))
            alpha = jnp.exp(m - m_new)
            p = jnp.exp(s - m_new)
            l = alpha * l + jnp.sum(p, axis=1, keepdims=True)
            acc = alpha * acc + _dot(p.astype(BF16), v_ref[pl.ds(off, blk), :])
            return m_new, l, acc

        init = (jnp.full((blk, 1), NEG, F32), jnp.zeros((blk, 1), F32), jnp.zeros((blk, hd), F32))
        carry = lax.fori_loop(0, qi, lambda j, cr: step(j, cr, False), init)
        m, l, acc = step(qi, carry, True)
        o_ref[...] = acc / l
        lse_ref[...] = m + jnp.log(l)

    qmap = lambda hh, i: (hh, i, 0)
    whole = lambda hh, i: (hh, 0, 0)
    return pl.pallas_call(
        body, name="fox_fwd", grid=(h, nb),
        in_specs=[pl.BlockSpec((None, blk, hd), qmap), pl.BlockSpec((None, t, hd), whole),
                  pl.BlockSpec((None, t, hd), whole), pl.BlockSpec((None, 1, t), whole)],
        out_specs=[pl.BlockSpec((None, blk, hd), qmap), pl.BlockSpec((None, blk, 1), qmap)],
        out_shape=[jax.ShapeDtypeStruct((h, t, hd), F32), jax.ShapeDtypeStruct((h, t, 1), F32)],
        compiler_params=_cp("parallel", "arbitrary"),
    )(q, k, v, cf_row)


def _fox_bwd(q, k, v, cf_row, o, lse, do):
    h, t, hd = q.shape
    blk = min(ATT_BLOCK, t)
    nb = t // blk

    def body(q_ref, k_ref, v_ref, ck_ref, o_ref, lse_ref, do_ref,
             dq_ref, dk_ref, dv_ref, dcq_ref, dck_ref, delta_s):
        kj = pl.program_id(1)

        @pl.when(kj == 0)
        def _():
            dq_ref[...] = jnp.zeros_like(dq_ref)
            dcq_ref[...] = jnp.zeros_like(dcq_ref)

            def fill(i, _):
                rows = pl.ds(pl.multiple_of(i * blk, blk), blk)
                delta_s[rows, :] = jnp.sum(do_ref[rows, :] * o_ref[rows, :], axis=1, keepdims=True)
                return 0

            lax.fori_loop(0, nb, fill, 0)

        kb, vb, ck = k_ref[...], v_ref[...], ck_ref[...]

        def step(i, carry, diagonal):
            dk, dv, dck = carry
            rows = pl.ds(pl.multiple_of(i * blk, blk), blk)
            qb = q_ref[rows, :]
            do_b = do_ref[rows, :].astype(BF16)
            s = _att_scores(qb, kb, ck, diagonal, blk)
            p = jnp.exp(s - lse_ref[rows, :])
            dv = dv + _dot_tn(p.astype(BF16), do_b)
            ds = p * (_dot_nt(do_b, vb) - delta_s[rows, :])
            ds_b = ds.astype(BF16)
            dk = dk + _dot_tn(ds_b, qb)
            dq_ref[rows, :] += _dot(ds_b, kb)
            dcq_ref[rows, :] += jnp.sum(ds, axis=1, keepdims=True)
            dck = dck - jnp.sum(ds, axis=0, keepdims=True)
            return dk, dv, dck

        init = (jnp.zeros((blk, hd), F32), jnp.zeros((blk, hd), F32), jnp.zeros((1, blk), F32))
        carry = step(kj, init, True)
        dk, dv, dck = lax.fori_loop(kj + 1, nb, lambda i, cr: step(i, cr, False), carry)
        dk_ref[...] = dk
        dv_ref[...] = dv
        dck_ref[...] = dck

    kmap = lambda hh, j: (hh, j, 0)
    whole = lambda hh, j: (hh, 0, 0)
    return pl.pallas_call(
        body, name="fox_bwd", grid=(h, nb),
        in_specs=[pl.BlockSpec((None, t, hd), whole), pl.BlockSpec((None, blk, hd), kmap),
                  pl.BlockSpec((None, blk, hd), kmap), pl.BlockSpec((None, 1, blk), lambda hh, j: (hh, 0, j)),
                  pl.BlockSpec((None, t, hd), whole), pl.BlockSpec((None, t, 1), whole),
                  pl.BlockSpec((None, t, hd), whole)],
        out_specs=[pl.BlockSpec((None, t, hd), whole), pl.BlockSpec((None, blk, hd), kmap),
                   pl.BlockSpec((None, blk, hd), kmap), pl.BlockSpec((None, t, 1), whole),
                   pl.BlockSpec((None, 1, blk), lambda hh, j: (hh, 0, j))],
        out_shape=[jax.ShapeDtypeStruct((h, t, hd), F32), jax.ShapeDtypeStruct((h, t, hd), F32),
                   jax.ShapeDtypeStruct((h, t, hd), F32), jax.ShapeDtypeStruct((h, t, 1), F32),
                   jax.ShapeDtypeStruct((h, 1, t), F32)],
        scratch_shapes=[pltpu.VMEM((t, 1), F32)],
        compiler_params=_cp("parallel", "arbitrary"),
    )(q, k, v, cf_row, o, lse, do)


def _lane_col(v, h):
    lane = lax.broadcasted_iota(jnp.int32, v.shape, 1)
    return jnp.sum(jnp.where(lane == h, v, 0.0), axis=1, keepdims=True)


def _sub_row(v, h):
    sub = lax.broadcasted_iota(jnp.int32, v.shape, 0)
    return jnp.sum(jnp.where(sub == h, v, 0.0), axis=0, keepdims=True)


def _ssd_decays(pad, pad_t, dtb_row, alog_row, dtb_col, alog_col, blk):
    r = lax.broadcasted_iota(jnp.int32, (blk, blk), 0)
    c = lax.broadcasted_iota(jnp.int32, (blk, blk), 1)
    tril = r >= c
    dt_c = _softplus(pad + dtb_row)
    acs_c = _tri_left(tril.astype(BF16), dt_c * (-jnp.exp(alog_row)))
    dt_r = _softplus(pad_t + dtb_col)
    acs_r = _tri_right(dt_r * (-jnp.exp(alog_col)), (r <= c).astype(BF16))
    rows = lax.broadcasted_iota(jnp.int32, acs_c.shape, 0)
    acs_last = jnp.sum(jnp.where(rows == blk - 1, acs_c, 0.0), axis=0, keepdims=True)
    return dt_c, acs_c, acs_r, acs_last, tril


def _pair_terms(pair, dt_c, acs_c, acs_r, acs_last, d_row, blk):
    lane = lax.broadcasted_iota(jnp.int32, (blk, LANES), 1)
    lo = lane < 64
    lo_row = lax.broadcasted_iota(jnp.int32, (1, LANES), 1) < 64
    h0, h1 = 2 * pair, 2 * pair + 1
    col = [_lane_col(acs_c, h0), _lane_col(acs_c, h1)]
    row = [_sub_row(acs_r, h0), _sub_row(acs_r, h1)]
    last = [_lane_col(acs_last, h0), _lane_col(acs_last, h1)]
    dt_p = jnp.where(lo, _lane_col(dt_c, h0), _lane_col(dt_c, h1))
    e_p = jnp.where(lo, jnp.exp(col[0]), jnp.exp(col[1]))
    w_p = jnp.where(lo, jnp.exp(last[0] - col[0]), jnp.exp(last[1] - col[1]))
    decay_p = jnp.where(lo_row, jnp.exp(last[0]), jnp.exp(last[1]))
    d_p = jnp.where(lo_row, _lane_col(d_row, h0), _lane_col(d_row, h1))
    return lo, lo_row, col, row, last, dt_p, e_p, w_p, decay_p, d_p


def _ssd_specs(t, blk, rev):
    nc = t // blk
    ix = (lambda i: nc - 1 - i) if rev else (lambda i: i)
    xbc = pl.BlockSpec((blk, SSD_CONV_DIM), lambda i: (ix(i), 0))
    pad = pl.BlockSpec((blk, LANES), lambda i: (ix(i), P_PAD // LANES))
    pad_t = pl.BlockSpec((LANES, blk), lambda i: (0, ix(i)))
    z = pl.BlockSpec((blk, SSD_W), lambda i: (ix(i), 0))
    row = pl.BlockSpec((1, LANES), lambda i: (0, 0))
    colv = pl.BlockSpec((LANES, 1), lambda i: (0, 0))
    ng = pl.BlockSpec((1, SSD_W), lambda i: (0, 0))
    y = pl.BlockSpec((blk, SSD_W), lambda i: (ix(i), 0))
    st = pl.BlockSpec((None, 4, LANES, LANES), lambda i: (ix(i), 0, 0, 0))
    return nc, xbc, pad, pad_t, z, row, colv, ng, y, st


def _ssd_fwd(xbc, proj, pad_t, dtb_row, alog_row, d_row, dtb_col, alog_col, ng):
    t = xbc.shape[0]
    blk = min(SSD_CHUNK, t)
    nc, s_xbc, s_pad, s_padt, s_z, s_row, s_col, s_ng, s_y, s_st = _ssd_specs(t, blk, False)

    def body(xbc_ref, pad_ref, padt_ref, z_ref, dtb_ref, alog_ref, d_ref, dtbc_ref, alogc_ref, ng_ref,
             out_ref, ypre_ref, st_ref, state):
        @pl.when(pl.program_id(0) == 0)
        def _():
            state[...] = jnp.zeros_like(state)

        dt_c, acs_c, acs_r, acs_last, tril = _ssd_decays(
            pad_ref[...], padt_ref[...], dtb_ref[...], alog_ref[...], dtbc_ref[...], alogc_ref[...], blk)
        ys = []
        g_mat = {}
        for pair in range(4):
            g = pair // 2
            bg = xbc_ref[:, SSD_W + LANES * g:SSD_W + LANES * (g + 1)].astype(BF16)
            cg = xbc_ref[:, SSD_W + 2 * LANES + LANES * g:SSD_W + 2 * LANES + LANES * (g + 1)].astype(BF16)
            if g not in g_mat:
                g_mat[g] = _dot_nt(cg, bg)
            xs_p = xbc_ref[:, LANES * pair:LANES * (pair + 1)]
            lo, _, col, row, _, dt_p, e_p, w_p, decay_p, d_p = _pair_terms(
                pair, dt_c, acs_c, acs_r, acs_last, d_ref[...], blk)
            x_p = xs_p * dt_p
            y = None
            for hh in range(2):
                lm = jnp.exp(jnp.where(tril, col[hh] - row[hh], NEG))
                m_h = (g_mat[g] * lm).astype(BF16)
                x_h = jnp.where(lo if hh == 0 else ~lo, x_p, 0.0).astype(BF16)
                y_h = _dot(m_h, x_h)
                y = y_h if y is None else y + y_h
            st_in = state[pair]
            st_ref[pair] = st_in
            y = y + e_p * _dot(cg, st_in.astype(BF16))
            state[pair] = decay_p * st_in + _dot_tn(bg, (x_p * w_p).astype(BF16))
            ys.append(y + d_p * xs_p)
        y_all = jnp.concatenate(ys, axis=1)
        ypre_ref[...] = y_all
        z = z_ref[...]
        y2 = y_all * (z * _sigmoid(z))
        outs = []
        for g in range(2):
            seg = y2[:, 256 * g:256 * (g + 1)]
            rr = lax.rsqrt(jnp.mean(seg * seg, axis=-1, keepdims=True) + RMS_EPS)
            outs.append(seg * rr * ng_ref[:, 256 * g:256 * (g + 1)])
        out_ref[...] = jnp.concatenate(outs, axis=1).astype(BF16)

    return pl.pallas_call(
        body, name="ssd_scan_fwd", grid=(nc,),
        in_specs=[s_xbc, s_pad, s_padt, s_z, s_row, s_row, s_row, s_col, s_col, s_ng],
        out_specs=[s_y, s_y, s_st],
        out_shape=[jax.ShapeDtypeStruct((t, SSD_W), BF16), jax.ShapeDtypeStruct((t, SSD_W), F32),
                   jax.ShapeDtypeStruct((nc, 4, LANES, LANES), F32)],
        scratch_shapes=[pltpu.VMEM((4, LANES, LANES), F32)],
        compiler_params=_cp("arbitrary"),
    )(xbc, proj, pad_t, proj, dtb_row, alog_row, d_row, dtb_col, alog_col, ng)


def _ssd_bwd(dout, dout_off, xbc, proj, pad_t, ypre, states, dtb_row, alog_row, d_row, dtb_col, alog_col, ng):
    t = xbc.shape[0]
    blk = min(SSD_CHUNK, t)
    nc, s_xbc, s_pad, s_padt, s_z, s_row, s_col, s_ng, s_y, s_st = _ssd_specs(t, blk, True)
    s_dout = pl.BlockSpec((blk, SSD_W), lambda i: (nc - 1 - i, dout_off // SSD_W))

    def body(dout_ref, xbc_ref, pad_ref, padt_ref, z_ref, ypre_ref, st_ref, dtb_ref, alog_ref, d_ref,
             dtbc_ref, alogc_ref, ng_ref, dxbc_ref, ddt_ref, dz_ref, acc_ref, dng_ref, dstate):
        @pl.when(pl.program_id(0) == 0)
        def _():
            dstate[...] = jnp.zeros_like(dstate)
            acc_ref[...] = jnp.zeros_like(acc_ref)
            dng_ref[...] = jnp.zeros_like(dng_ref)

        pad = pad_ref[...]
        dt_c, acs_c, acs_r, acs_last, tril = _ssd_decays(
            pad, padt_ref[...], dtb_ref[...], alog_ref[...], dtbc_ref[...], alogc_ref[...], blk)
        a_row = -jnp.exp(alog_ref[...])

        z = z_ref[...]
        sz = _sigmoid(z)
        silu_z = z * sz
        y_pre = ypre_ref[...]
        y2 = y_pre * silu_z
        dy2 = []
        for g in range(2):
            sl = slice(256 * g, 256 * (g + 1))
            seg = y2[:, sl]
            rr = lax.rsqrt(jnp.mean(seg * seg, axis=-1, keepdims=True) + RMS_EPS)
            nrm = seg * rr
            d_seg = dout_ref[:, sl]
            dng_ref[:, sl] += jnp.sum(d_seg * nrm, axis=0, keepdims=True)
            dn = d_seg * ng_ref[:, sl]
            dy2.append(rr * (dn - nrm * jnp.mean(dn * nrm, axis=-1, keepdims=True)))
        dy2 = jnp.concatenate(dy2, axis=1)
        dz_ref[...] = (dy2 * y_pre * (sz * (1.0 + z * (1.0 - sz)))).astype(BF16)
        dy_all = dy2 * silu_z

        lane_row = lax.broadcasted_iota(jnp.int32, (1, LANES), 1)
        lane_blk = lax.broadcasted_iota(jnp.int32, (blk, LANES), 1)
        row_col = lax.broadcasted_iota(jnp.int32, (blk, 1), 0)
        ddt = jnp.zeros((blk, LANES), F32)
        dacs = jnp.zeros((blk, LANES), F32)
        dd_row = jnp.zeros((1, LANES), F32)
        ones_b = jnp.ones((blk, LANES), BF16)
        dxs = []
        d_b = [None, None]
        d_c = [None, None]
        d_g = [None, None]
        bgs, cgs = {}, {}
        g_mat = {}
        for pair in range(4):
            g = pair // 2
            if g not in g_mat:
                bgs[g] = xbc_ref[:, SSD_W + LANES * g:SSD_W + LANES * (g + 1)].astype(BF16)
                cgs[g] = xbc_ref[:, SSD_W + 2 * LANES + LANES * g:SSD_W + 2 * LANES + LANES * (g + 1)].astype(BF16)
                g_mat[g] = _dot_nt(cgs[g], bgs[g])
            bg, cg = bgs[g], cgs[g]
            xs_p = xbc_ref[:, LANES * pair:LANES * (pair + 1)]
            lo, lo_row, col, row, last, dt_p, e_p, w_p, decay_p, d_p = _pair_terms(
                pair, dt_c, acs_c, acs_r, acs_last, d_ref[...], blk)
            x_p = xs_p * dt_p
            dy_p = dy_all[:, LANES * pair:LANES * (pair + 1)]
            st_in = st_ref[pair]
            dst = dstate[pair]
            dx_diag = None
            for hh in range(2):
                sel = lo if hh == 0 else ~lo
                lm = jnp.exp(jnp.where(tril, col[hh] - row[hh], NEG))
                m_f = g_mat[g] * lm
                m_h = m_f.astype(BF16)
                x_h = jnp.where(sel, x_p, 0.0).astype(BF16)
                dy_h = jnp.where(sel, dy_p, 0.0).astype(BF16)
                dxd = _dot_tn(m_h, dy_h)
                dm = _dot_nt(dy_h, x_h)
                dg_h = dm * lm
                p_b = (dm * m_f).astype(BF16)
                dacs = dacs + jnp.where(lane_blk == 2 * pair + hh, _dot(p_b, ones_b) - _dot_tn(p_b, ones_b), 0.0)
                dx_diag = dxd if dx_diag is None else dx_diag + dxd
                d_g[g] = dg_h if d_g[g] is None else d_g[g] + dg_h
            st_b = st_in.astype(BF16)
            dst_b = dst.astype(BF16)
            y_off = e_p * _dot(cg, st_b)
            edy = (e_p * dy_p).astype(BF16)
            dc_off = _dot_nt(edy, st_b)
            d_c[g] = dc_off if d_c[g] is None else d_c[g] + dc_off
            dstate[pair] = decay_p * dst + _dot_tn(cg, edy)
            dx_state = _dot(bg, dst_b) * w_p
            db_st = _dot_nt((x_p * w_p).astype(BF16), dst_b)
            d_b[g] = db_st if d_b[g] is None else d_b[g] + db_st
            dx = dx_diag + dx_state
            dxs.append(dx * dt_p + d_p * dy_p)
            prod_dt = dx * xs_p
            prod_acs = dy_p * y_off - x_p * dx_state
            prod_st = x_p * dx_state
            prod_d = dy_p * xs_p
            st_prod = jnp.sum(dst * st_in, axis=0, keepdims=True)
            for hh in range(2):
                h = 2 * pair + hh
                sel = lo if hh == 0 else ~lo
                sel_row = lo_row if hh == 0 else ~lo_row
                ddt_h = jnp.sum(jnp.where(sel, prod_dt, 0.0), axis=1, keepdims=True)
                dacs_h = jnp.sum(jnp.where(sel, prod_acs, 0.0), axis=1, keepdims=True)
                tail = jnp.sum(jnp.sum(jnp.where(sel, prod_st, 0.0), axis=1, keepdims=True), axis=0, keepdims=True)
                tail = tail + jnp.exp(last[hh]) * jnp.sum(jnp.where(sel_row, st_prod, 0.0), axis=1, keepdims=True)
                dacs_h = dacs_h + jnp.where(row_col == blk - 1, tail, 0.0)
                dd_h = jnp.sum(jnp.sum(jnp.where(sel, prod_d, 0.0), axis=1, keepdims=True), axis=0, keepdims=True)
                ddt = ddt + jnp.where(lane_blk == h, ddt_h, 0.0)
                dacs = dacs + jnp.where(lane_blk == h, dacs_h, 0.0)
                dd_row = dd_row + jnp.where(lane_row == h, dd_h, 0.0)
        for g in range(2):
            dg_b = d_g[g].astype(BF16)
            d_c[g] = d_c[g] + _dot(dg_b, bgs[g])
            d_b[g] = d_b[g] + _dot_tn(dg_b, cgs[g])
        r = lax.broadcasted_iota(jnp.int32, (blk, blk), 0)
        c = lax.broadcasted_iota(jnp.int32, (blk, blk), 1)
        da = _tri_left((r <= c).astype(BF16), dacs)
        ddt = ddt + da * a_row
        d_raw = ddt * _sigmoid(pad + dtb_ref[...])
        ddt_ref[...] = d_raw
        acc_ref[0:1, :] += jnp.sum(da * dt_c, axis=0, keepdims=True) * a_row
        acc_ref[1:2, :] += dd_row
        acc_ref[2:3, :] += jnp.sum(d_raw, axis=0, keepdims=True)
        dxbc_ref[...] = jnp.concatenate(dxs + d_b + d_c, axis=1)

    return pl.pallas_call(
        body, name="ssd_scan_bwd", grid=(nc,),
        in_specs=[s_dout, s_xbc, s_pad, s_padt, s_z, s_y, s_st, s_row, s_row, s_row, s_col, s_col, s_ng],
        out_specs=[s_xbc, pl.BlockSpec((blk, LANES), lambda i: (nc - 1 - i, 0)), s_y,
                   pl.BlockSpec((8, LANES), lambda i: (0, 0)), s_ng],
        out_shape=[jax.ShapeDtypeStruct((t, SSD_CONV_DIM), F32), jax.ShapeDtypeStruct((t, LANES), F32),
                   jax.ShapeDtypeStruct((t, SSD_W), BF16), jax.ShapeDtypeStruct((8, LANES), F32),
                   jax.ShapeDtypeStruct((1, SSD_W), F32)],
        scratch_shapes=[pltpu.VMEM((4, LANES, LANES), F32)],
        compiler_params=_cp("arbitrary"),
    )(dout, xbc, proj, pad_t, proj, ypre, states, dtb_row, alog_row, d_row, dtb_col, alog_col, ng)


def _pad_lanes(v, off):
    return jnp.zeros((1, LANES), F32).at[0, off:off + v.shape[0]].set(v)


def _perm_mix_w_in(w):
    z, xbc, dt = w[:, 0:512], w[:, 512:1536], w[:, 1536:1544]
    qkv, f, sc = w[:, 1544:2312], w[:, 2312:2316], w[:, 2316:3084]
    padblk = jnp.zeros((w.shape[0], LANES), w.dtype).at[:, PAD_DT0:PAD_DT0 + 8].set(dt).at[:, PAD_F0:PAD_F0 + 4].set(f)
    return jnp.concatenate([z, xbc, qkv, sc, padblk], axis=1)


def _unperm_mix_w_in(wp):
    z, xbc, qkv, sc = wp[:, 0:512], wp[:, 512:1536], wp[:, 1536:2304], wp[:, 2304:3072]
    dt, f = wp[:, P_PAD + PAD_DT0:P_PAD + PAD_DT0 + 8], wp[:, P_PAD + PAD_F0:P_PAD + PAD_F0 + 4]
    return jnp.concatenate([z, xbc, dt, qkv, f, sc], axis=1)


def _heads(m):
    return jnp.transpose(m.reshape(m.shape[0], FOX_HEADS, FOX_HD), (1, 0, 2))


def _unheads(m):
    return jnp.transpose(m, (1, 0, 2)).reshape(m.shape[1], FOX_W)


def _ffn_fwd(h, w_in, w_out, tag):
    u, a = _ffn_in_swiglu(h, w_in, f"ffn_in_{tag}")
    y = _matmul(a, w_out, "nn", F32, f"ffn_out_{tag}")
    return y, (h, u, a)


def _ffn_bwd(dy, saved, w_in, w_out, tag):
    h, u, a = saved
    du = _ffn_out_dx_swiglu(dy, w_out, u, f"ffn_out_dx_{tag}")
    dw_out = _matmul(a, dy, "tn", BF16, f"ffn_out_dw_{tag}")
    dh = _matmul(du, w_in, "nt", F32, f"ffn_in_dx_{tag}", halves="a")
    dw_in = _matmul(h, du, "tn", BF16, f"ffn_in_dw_{tag}", halves="b")
    return dh, dw_in, dw_out


def _mix_fwd(h, wp, w_out, sp, tag):
    proj = _matmul(h, wp, "nn", F32, f"mix_in_{tag}")
    pad_t = jnp.transpose(proj[:, P_PAD:P_PAD + LANES])
    xbc = _ssd_conv_fwd(proj, sp["conv_w"], sp["conv_b"])
    y_ssd, ypre, states = _ssd_fwd(xbc, proj, pad_t, sp["dtb_row"], sp["alog_row"], sp["d_row"],
                                   sp["dtb_col"], sp["alog_col"], sp["ng"])
    cf = _cum_logf(proj, sp["fbias_row"])
    cf_row = jnp.transpose(cf[:, PAD_F0:PAD_F0 + FOX_HEADS])[:, None, :]
    q = _heads((proj[:, P_QKV:P_QKV + 256] * FOX_SCALE).astype(BF16))
    k = _heads(proj[:, P_QKV + 256:P_QKV + 512].astype(BF16))
    v = _heads(proj[:, P_QKV + 512:P_QKV + 768].astype(BF16))
    o, lse = _fox_fwd(q, k, v, cf_row)
    y_sc = _shortconv_fwd(proj, sp["sconv_w"])
    ymix = jnp.concatenate([y_ssd, _unheads(o).astype(BF16), y_sc], axis=1)
    y = _matmul(ymix, w_out, "nn", F32, f"mix_out_{tag}")
    return y, (h, proj, pad_t, xbc, ypre, states, q, k, v, cf_row, o, lse, ymix)


def _mix_bwd(dy, saved, wp, w_out, sp, tag):
    h, proj, pad_t, xbc, ypre, states, q, k, v, cf_row, o, lse, ymix = saved
    dymix = _matmul(dy, w_out, "nt", F32, f"mix_out_dx_{tag}")
    dw_out = _matmul(ymix, dy, "tn", BF16, f"mix_out_dw_{tag}")
    dxbc, ddt, dz, ssd_acc, dng = _ssd_bwd(dymix, 0, xbc, proj, pad_t, ypre, states, sp["dtb_row"],
                                           sp["alog_row"], sp["d_row"], sp["dtb_col"], sp["alog_col"], sp["ng"])
    dxbc_raw, dconv_w, dconv_b = _ssd_conv_bwd(dxbc, proj, sp["conv_w"], sp["conv_b"])
    do = _heads(dymix[:, SSD_W:SSD_W + FOX_W])
    dq, dk, dv, dcq, dck = _fox_bwd(q, k, v, cf_row, o, lse, do)
    dq = dq * FOX_SCALE
    dcf4 = dcq[:, :, 0] + dck[:, 0, :]
    dcf = jnp.zeros((h.shape[0], LANES), F32).at[:, PAD_F0:PAD_F0 + FOX_HEADS].set(jnp.transpose(dcf4))
    dpad, dfb = _pad_block_bwd(dcf, ddt, proj, sp["fbias_row"])
    dscb, dscc, dscx, dsconv_w = _shortconv_bwd(dymix, SSD_W + FOX_W, proj, sp["sconv_w"])
    dproj = jnp.concatenate([dz, dxbc_raw.astype(BF16), _unheads(dq).astype(BF16), _unheads(dk).astype(BF16),
                             _unheads(dv).astype(BF16), dscb, dscc, dscx, dpad], axis=1)
    dh = _matmul(dproj, wp, "nt", F32, f"mix_in_dx_{tag}")
    dwp = _matmul(h, dproj, "tn", BF16, f"mix_in_dw_{tag}")
    small = dict(conv_w=dconv_w, conv_b=dconv_b[0], dt_bias=ssd_acc[2, 0:8], a_log=ssd_acc[0, 0:8],
                 d=ssd_acc[1, 0:8], norm_g=dng[0], f_bias=dfb[0, PAD_F0:PAD_F0 + FOX_HEADS], sconv_w=dsconv_w)
    return dh, _unperm_mix_w_in(dwp), dw_out, small


def _local_step(x, tgt, mod, wfull, small_p, before_sub_backward=None, after_sub_backward=None):
    row = lambda v: v.reshape(1, -1)
    subs = [(l, j) for l in range(DEPTH) for j in range(N_SUB)]
    factor = (0.5, 1.0, 0.5)
    w_names = (("ffn1_w_in", "ffn1_w_out"), ("mix_w_in", "mix_w_out"), ("ffn2_w_in", "ffn2_w_out"))
    lg = [[row(small_p["ln_g"][l, j]) for j in range(N_SUB)] for l in range(DEPTH)]
    lb = [[row(small_p["ln_b"][l, j]) for j in range(N_SUB)] for l in range(DEPTH)]
    sps = []
    for l in range(DEPTH):
        sp = dict(
            conv_w=small_p["ssd_conv_w"][l], conv_b=row(small_p["ssd_conv_b"][l]),
            dtb_row=_pad_lanes(small_p["ssd_dt_bias"][l], PAD_DT0), alog_row=_pad_lanes(small_p["ssd_a_log"][l], 0),
            d_row=_pad_lanes(small_p["ssd_d"][l], 0), ng=row(small_p["ssd_norm_g"][l]),
            fbias_row=_pad_lanes(small_p["fox_f_bias"][l], PAD_F0), sconv_w=small_p["sconv_w"][l])
        sp["dtb_col"] = jnp.transpose(sp["dtb_row"])
        sp["alog_col"] = jnp.transpose(sp["alog_row"])
        sps.append(sp)
    tags = [f"l{l}{('f1', 'mx', 'f2')[j]}" for l, j in subs]

    x0, h = _ln_in_fwd(x, row(small_p["ln_in_g"]), row(small_p["ln_in_b"]), mod[0, 0])
    cur = x0
    xins, ys, inner, weights = [], [], [], []
    wfull = list(wfull)
    for idx, (l, j) in enumerate(subs):
        if idx == 1 and callable(wfull[DEPTH - 1]):
            wfull[DEPTH - 1] = wfull[DEPTH - 1](cur)
        w_in, w_out = wfull[l][w_names[j][0]], wfull[l][w_names[j][1]]
        if idx > 0:
            (w_in, w_out), (cur, h) = lax.optimization_barrier(((w_in, w_out), (cur, h)))
        w_out = w_out.reshape(-1, w_out.shape[-1])
        if j == 1:
            w_in = _perm_mix_w_in(jnp.concatenate([w_in[s] for s in range(4)], axis=1))
        weights.append((w_in, w_out))
        if j == 1:
            y, sv = _mix_fwd(h, w_in, w_out, sps[l], tags[idx])
        else:
            y, sv = _ffn_fwd(h, w_in, w_out, tags[idx])
        nxt = mod[subs[idx + 1]] if idx + 1 < len(subs) else None
        xins.append(cur)
        ys.append(y)
        inner.append(sv)
        cur, h = _res_ln(cur, y, mod[l, j], lg[l][j], lb[l][j], factor[j], f"res_ln_{tags[idx]}", nxt)

    dcur, loss_acc = _loss_head(cur, tgt)
    last = len(subs) - 1
    l, j = subs[last]
    dres, dy, acc = _res_ln_bwd(dcur, xins[last], ys[last], mod[l, j], lg[l][j], factor[j], f"res_ln_bwd_{tags[last]}")
    ln_acc = {last: acc}
    shift_scale = {}
    big_grads = [dict() for _ in range(DEPTH)]
    small_g = [None] * DEPTH
    for idx in reversed(range(len(subs))):
        l, j = subs[idx]
        w_in, w_out = weights[idx]
        if before_sub_backward is not None:
            before_sub_backward(l, j, dy)
        if j == 1:
            dh, g_in, g_out, small_g[l] = _mix_bwd(dy, inner[idx], w_in, w_out, sps[l], tags[idx])
        else:
            dh, g_in, g_out = _ffn_bwd(dy, inner[idx], w_in, w_out, tags[idx])
        big_grads[l][w_names[j][0]], big_grads[l][w_names[j][1]] = g_in, g_out
        if after_sub_backward is not None:
            after_sub_backward(l, j, big_grads[l], dh)
        if idx > 0:
            pl_, pj = subs[idx - 1]
            dres, dy, acc5 = _mod_res_bwd(dres, dh, mod[l, j], xins[idx - 1], ys[idx - 1], mod[pl_, pj], lg[pl_][pj],
                                          lb[pl_][pj], factor[pj], f"mod_res_bwd_{tags[idx]}")
            shift_scale[idx], ln_acc[idx - 1] = acc5[0:2], acc5[2:5]
        else:
            dx0, shift_scale[0] = _modulate_bwd(dres, dh, x0, mod[0, 0], "mod_bwd_first")
    dx, acc_in = _ln_in_bwd(dx0, x, row(small_p["ln_in_g"]))
    dmod = []
    for l in range(DEPTH):
        ids = [N_SUB * l + j for j in range(N_SUB)]
        small_g[l]["ln_g"] = jnp.stack([ln_acc[i][0] for i in ids])
        small_g[l]["ln_b"] = jnp.stack([ln_acc[i][1] for i in ids])
        dmod.append(jnp.stack([jnp.concatenate([shift_scale[i], ln_acc[i][2:3]], axis=0) for i in ids]))
    return loss_acc[0, 0], dx, big_grads, small_g, jnp.stack(dmod), acc_in


SMALL_ORDER = ("ssd_conv_w", "ssd_conv_b", "ssd_dt_bias", "ssd_a_log", "ssd_d", "ssd_norm_g", "fox_f_bias",
               "sconv_w", "ln_g", "ln_b")
SMALL_KEY = dict(ssd_conv_w="conv_w", ssd_conv_b="conv_b", ssd_dt_bias="dt_bias", ssd_a_log="a_log", ssd_d="d",
                 ssd_norm_g="norm_g", fox_f_bias="f_bias", sconv_w="sconv_w", ln_g="ln_g", ln_b="ln_b")
COL_SHARDED_SMALL = ("ssd_conv_w", "sconv_w", "ln_g", "ln_b")


def _pad_to(v, n):
    return jnp.concatenate([v, jnp.zeros((n - v.shape[0],), v.dtype)])


def kernel(x, c, ln_in_g, ln_in_b, ada_w, ada_b, ffn1_w_in, ffn1_w_out, mix_w_in, mix_w_out, ssd_conv_w, ssd_conv_b, ssd_dt_bias, ssd_a_log, ssd_d, ssd_norm_g, fox_f_bias, sconv_w, ffn2_w_in, ffn2_w_out, ln_g, ln_b, loss_target, m_ln_in_g, m_ln_in_b, m_ada_w, m_ada_b, m_ffn1_w_in, m_ffn1_w_out, m_mix_w_in, m_mix_w_out, m_ssd_conv_w, m_ssd_conv_b, m_ssd_dt_bias, m_ssd_a_log, m_ssd_d, m_ssd_norm_g, m_fox_f_bias, m_sconv_w, m_ffn2_w_in, m_ffn2_w_out, m_ln_g, m_ln_b, v_ln_in_g, v_ln_in_b, v_ada_w, v_ada_b, v_ffn1_w_in, v_ffn1_w_out, v_mix_w_in, v_mix_w_out, v_ssd_conv_w, v_ssd_conv_b, v_ssd_dt_bias, v_ssd_a_log, v_ssd_d, v_ssd_norm_g, v_fox_f_bias, v_sconv_w, v_ffn2_w_in, v_ffn2_w_out, v_ln_g, v_ln_b):
    names = ("ln_in_g", "ln_in_b", "ada_w", "ada_b", "ffn1_w_in", "ffn1_w_out", "mix_w_in", "mix_w_out",
             "ssd_conv_w", "ssd_conv_b", "ssd_dt_bias", "ssd_a_log", "ssd_d", "ssd_norm_g", "fox_f_bias", "sconv_w",
             "ffn2_w_in", "ffn2_w_out", "ln_g", "ln_b")
    w_loc = dict(zip(names, (ln_in_g, ln_in_b, ada_w, ada_b, ffn1_w_in, ffn1_w_out, mix_w_in, mix_w_out, ssd_conv_w,
                             ssd_conv_b, ssd_dt_bias, ssd_a_log, ssd_d, ssd_norm_g, fox_f_bias, sconv_w, ffn2_w_in,
                             ffn2_w_out, ln_g, ln_b)))
    m_loc = dict(zip(names, (m_ln_in_g, m_ln_in_b, m_ada_w, m_ada_b, m_ffn1_w_in, m_ffn1_w_out, m_mix_w_in,
                             m_mix_w_out, m_ssd_conv_w, m_ssd_conv_b, m_ssd_dt_bias, m_ssd_a_log, m_ssd_d,
                             m_ssd_norm_g, m_fox_f_bias, m_sconv_w, m_ffn2_w_in, m_ffn2_w_out, m_ln_g, m_ln_b)))
    v_loc = dict(zip(names, (v_ln_in_g, v_ln_in_b, v_ada_w, v_ada_b, v_ffn1_w_in, v_ffn1_w_out, v_mix_w_in,
                             v_mix_w_out, v_ssd_conv_w, v_ssd_conv_b, v_ssd_dt_bias, v_ssd_a_log, v_ssd_d,
                             v_ssd_norm_g, v_fox_f_bias, v_sconv_w, v_ffn2_w_in, v_ffn2_w_out, v_ln_g, v_ln_b)))

    xi, yi, ci = lax.axis_index("x"), lax.axis_index("y"), lax.axis_index("c")
    me = 2 * xi + yi
    dev = 2 * me + ci

    def gather8(v, tag):
        v2 = v.reshape(1, -1)
        got = _bcast_chips([v2], f"gather_chips_{tag}")
        same_c = jnp.concatenate(_by_chip(me, v2, [g[0] for g in got]), axis=0)
        other_c = _swap_sibling([same_c], f"gather_sibling_{tag}")[0]
        pair = lax.switch(ci, [lambda a, b: jnp.stack([a, b], axis=1), lambda a, b: jnp.stack([b, a], axis=1)],
                          same_c, other_c)
        return pair.reshape(8, -1)

    def chip_concat(own, got3, axis):
        return jnp.concatenate(_by_chip(me, own, got3), axis=axis)

    small_cols = [w_loc[n].reshape(-1, w_loc[n].shape[-1]) for n in COL_SHARDED_SMALL]
    got = _bcast_chips(small_cols, "gather_small_params")
    small_p = {n: w_loc[n] for n in ("ln_in_g", "ln_in_b", "ssd_conv_b", "ssd_dt_bias", "ssd_a_log", "ssd_d",
                                     "ssd_norm_g", "fox_f_bias")}
    for i, n in enumerate(COL_SHARDED_SMALL):
        full = chip_concat(small_cols[i], [g[i] for g in got], 1)
        small_p[n] = full.reshape(w_loc[n].shape[:-1] + (full.shape[-1],))

    assert DEPTH == 2
    first, rest = BIG[:2], BIG[2:]
    shard_of = {n: w_loc[n].astype(BF16) for n in BIG}

    def gather(names, layer, after, tag, sequencer_id):
        out = _gather_weights(names, [shard_of[n] for n in names], layer, after, f"gather_weights_{tag}", sequencer_id)
        return dict(zip(names, out))

    g0a = gather(first, 0, [], "l0a", None)
    g0b = gather(rest, 0, [g0a[first[1]]], "l0b", 1)
    wfull = [{**g0a, **g0b}, lambda marker: gather(BIG, 1, [marker], "l1", 5)]

    c_all = gather8(c[0], "c")
    c_act = _silu_bf16(c_all)
    ada_w_b = ada_w.astype(BF16)
    mod_loc = [_matmul(c_act, ada_w_b[l], "nn", F32, f"ada_fwd_l{l}") for l in range(DEPTH)]
    mod_loc = jnp.stack(mod_loc)
    got = _bcast_chips([mod_loc], "gather_mod")
    mod_all = chip_concat(mod_loc, [g[0] for g in got], 2)
    mod = lax.dynamic_index_in_dim(mod_all, dev, 1, keepdims=False) + ada_b
    mod = mod.reshape(DEPTH, N_SUB, 3, D)

    ci_arr = jnp.reshape(ci, (1,)).astype(jnp.int32)
    me_arr = jnp.reshape(me, (1,)).astype(jnp.int32)

    def wire_form(n, g):
        if n in BY_COLUMNS:
            return g
        if n == "mix_w_in":
            return jnp.transpose(g.reshape(g.shape[0], 4, g.shape[1] // 4), (1, 0, 2))
        return g.reshape(4, g.shape[0] // 4, g.shape[1])

    units, reduced = {}, {}

    def start_unit(tag, l, names, g, ids):
        wire = [wire_form(n, g[n]) for n in names]
        units[tag] = dict(l=l, names=names, wire=wire, ids=ids,
                          got=_grad_swap_halves(names, wire, f"grad_swap_halves_{tag}", ids[0]))

    def add_and_scatter(tag, after):
        st = units[tag]
        st["part"] = [_grad_add_halves(st["wire"][i], st["got"][i], ci_arr, after, f"grad_add_halves_{tag}_{n}")
                      for i, n in enumerate(st["names"])]
        st["from_chips"] = _grad_scatter_halves(st["names"], st["part"], f"grad_scatter_halves_{tag}", st["ids"][1])

    def finish_unit(tag, after):
        st = units[tag]
        halves = [_grad_add_slots(st["part"][i], [g[i] for g in st["from_chips"]], n, me_arr, after,
                                  f"grad_add_slots_{tag}_{n}") for i, n in enumerate(st["names"])]
        other = _grad_swap_reduced(halves, f"grad_swap_reduced_{tag}", st["ids"][2])
        for i, n in enumerate(st["names"]):
            reduced[(st["l"], n)] = (halves[i], other[i])

    second, third = BIG[2:4], BIG[4:]

    def after_sub_backward(l, j, g, marker):
        if (l, j) == (1, 0):
            start_unit("l1", 1, BIG, g, (2, 3, 4))
        elif (l, j) == (0, 2):
            start_unit("l0f2", 0, third, g, (6, 7, 8))
        elif (l, j) == (0, 1):
            start_unit("l0mx", 0, second, g, (9, 10, 11))

    def before_sub_backward(l, j, marker):
        if (l, j) == (0, 2):
            add_and_scatter("l1", marker)
        elif (l, j) == (0, 1):
            add_and_scatter("l0f2", marker)
        elif (l, j) == (0, 0):
            add_and_scatter("l0mx", marker)

    loss_part, dx, big_g, small_g, dmod, acc_in = _local_step(
        x[0], loss_target[0], mod, wfull, small_p, before_sub_backward, after_sub_backward)
    loss = lax.psum(loss_part, ("x", "y", "c"))

    pieces = [dmod.reshape(-1), acc_in[0], acc_in[1]]
    for n in SMALL_ORDER:
        pieces.append(jnp.stack([small_g[l][SMALL_KEY[n]] for l in range(DEPTH)]).reshape(-1))
    sizes = [p.shape[0] for p in pieces]
    total = sum(sizes)
    padded = -(-total // 1024) * 1024
    vec = _pad_to(jnp.concatenate(pieces), padded)
    all_rows = gather8(vec, "small_grads")
    summed = _sum_rows(all_rows, "sum_small_grads")[0]
    offs = [0]
    for s in sizes:
        offs.append(offs[-1] + s)
    n_mod = sizes[0]
    grads = {"ada_b": summed[0:n_mod].reshape(DEPTH, 3 * N_SUB * D),
             "ln_in_g": summed[offs[1]:offs[2]], "ln_in_b": summed[offs[2]:offs[3]]}
    for i, n in enumerate(SMALL_ORDER):
        full = summed[offs[3 + i]:offs[4 + i]].reshape(small_p[n].shape)
        if n in COL_SHARDED_SMALL:
            wcols = w_loc[n].shape[-1]
            full = lax.dynamic_slice_in_dim(full, me * wcols, wcols, axis=full.ndim - 1)
        grads[n] = full

    dmod_all = all_rows[:, 0:n_mod].reshape(8, DEPTH, 3 * N_SUB * D)
    ncol = ada_w.shape[-1]
    dmod_cols = lax.dynamic_slice_in_dim(dmod_all, me * ncol, ncol, axis=2).astype(BF16)
    grads["ada_w"] = jnp.stack([_matmul(c_act, dmod_cols[:, l], "tn", F32, f"ada_bwd_l{l}") for l in range(DEPTH)])

    finish_unit("l1", dx)
    finish_unit("l0f2", dx)
    finish_unit("l0mx", dx)
    start_unit("l0f1", 0, first, big_g[0], (None, None, None))
    add_and_scatter("l0f1", dx)
    finish_unit("l0f1", dx)

    delta, new_m, new_v = {}, {}, {}
    delta["ada_w"], new_m["ada_w"], new_v["ada_w"] = _adamw(w_loc["ada_w"], grads["ada_w"], m_loc["ada_w"],
                                                           v_loc["ada_w"], "adamw_ada_w")
    for i, n in enumerate(BIG):
        grads[n], delta[n], new_m[n], new_v[n] = _adamw_layers(
            w_loc[n], [reduced[(l, n)][0] for l in range(DEPTH)], [reduced[(l, n)][1] for l in range(DEPTH)],
            m_loc[n], v_loc[n], ci_arr, f"adamw_{n}")
    small_names = [n for n in names if n not in ("ada_w",) + BIG]
    flat = lambda d: jnp.concatenate([d[n].reshape(-1) for n in small_names])
    n_small = sum(w_loc[n].size for n in small_names)
    n_pad = -(-n_small // 1024) * 1024
    packed = [_pad_to(flat(d), n_pad).reshape(-1, LANES) for d in (w_loc, grads, m_loc, v_loc)]
    d_s, m_s, v_s = _adamw(*packed, "adamw_small")
    off = 0
    for n in small_names:
        sz = w_loc[n].size
        delta[n] = d_s.reshape(-1)[off:off + sz].reshape(w_loc[n].shape)
        new_m[n] = m_s.reshape(-1)[off:off + sz].reshape(w_loc[n].shape)
        new_v[n] = v_s.reshape(-1)[off:off + sz].reshape(w_loc[n].shape)
        off += sz

    return (loss, dx[None], *[grads[n] for n in names], *[delta[n] for n in names],
            *[new_m[n] for n in names], *[new_v[n] for n in names])
```

```python
import functools

import jax
import jax.numpy as jnp
from jax import lax
from jax.experimental import pallas as pl
from jax.experimental.pallas import tpu as pltpu
from jax.experimental.pallas import tpu_sc as plsc

F32 = jnp.float32
BF16 = jnp.bfloat16
MESH = pl.DeviceIdType.MESH

D = 1024
DEPTH = 2
N_SUB = 3
D_FF = 2816
FF_TILE = D_FF // 2
ALPHA = (2 * DEPTH) ** 0.25
LN_EPS = 1e-5
RMS_EPS = 1e-5
SSD_W = 512
SSD_HEADS = 8
SSD_CONV_K = 4
SSD_CONV_DIM = 1024
FOX_W = 256
FOX_HEADS = 4
FOX_HD = 64
FOX_SCALE = FOX_HD ** -0.5
SC_W = 256
SC_K = 3
D_IN_PROJ = 3084
P_Z, P_XBC, P_QKV, P_SC, P_PAD = 0, 512, 1536, 2304, 3072
D_PROJ_PAD = 3200
PAD_DT0, PAD_F0 = 0, 8
SSD_CHUNK = 256
ATT_BLOCK = 512
CUM_BLOCK = 256
LANES = 128
VMEM_LIMIT = 56 * 1024 * 1024

ADAM_LR, ADAM_B1, ADAM_B2, ADAM_EPS, ADAM_WD, ADAM_STEP = 0.001, 0.9, 0.999, 1e-08, 0.01, 10
NEG = -1e30


def _cp(*sem):
    return pltpu.CompilerParams(dimension_semantics=sem, vmem_limit_bytes=VMEM_LIMIT)


def _pick(n, cands):
    for c in cands:
        if n % c == 0:
            return c
    return n


def _dot(a, b):
    return lax.dot_general(a, b, (((1,), (0,)), ((), ())), preferred_element_type=F32)


def _dot_nt(a, b):
    return lax.dot_general(a, b, (((1,), (1,)), ((), ())), preferred_element_type=F32)


def _dot_tn(a, b):
    return lax.dot_general(a, b, (((0,), (0,)), ((), ())), preferred_element_type=F32)


def _sigmoid(x):
    return 0.5 * jnp.tanh(0.5 * x) + 0.5


def _softplus(x):
    return jnp.maximum(x, 0.0) + jnp.log(1.0 + jnp.exp(-jnp.abs(x)))


def _split3(v):
    h = v.astype(BF16)
    r = v - h.astype(F32)
    m = r.astype(BF16)
    l = (r - m.astype(F32)).astype(BF16)
    return h, m, l


def _tri_left(tri, v):
    h, m, l = _split3(v)
    return _dot(tri, h) + _dot(tri, m) + _dot(tri, l)


def _tri_right(v, tri):
    h, m, l = _split3(v)
    return _dot(h, tri) + _dot(m, tri) + _dot(l, tri)


def _matmul(a, b, mode, out_dtype, name, halves=None):
    assert a.dtype == BF16 and b.dtype == BF16, (name, a.dtype, b.dtype)
    if halves == "a":
        assert mode == "nt" and a.shape[0] == 2 and a.shape[2] == D_FF
        m, k, n = a.shape[1], 2 * D_FF, b.shape[0]
    elif halves == "b":
        assert mode == "tn" and b.shape[0] == 2 and b.shape[2] == D_FF
        (k, m), n = a.shape, 2 * D_FF
    elif mode == "nn":
        (m, k), n = a.shape, b.shape[1]
    elif mode == "nt":
        (m, k), n = a.shape, b.shape[0]
    else:
        (k, m), n = a.shape, b.shape[1]
    tm = m if m <= 1024 else _pick(m, (1024, 1408, 512, 256, 128))
    tn = n if n <= 1024 else _pick(n, (1408, 640, 512, 256, 128))
    tk = k if k <= 1024 else _pick(k, (1408, 1024, 640, 512, 256, 128))
    nk = k // tk
    if mode == "nn":
        dn = (((1,), (0,)), ((), ()))
        a_spec = pl.BlockSpec((tm, tk), lambda i, j, kk: (i, kk))
        b_spec = pl.BlockSpec((tk, tn), lambda i, j, kk: (kk, j))
    elif mode == "nt":
        dn = (((1,), (1,)), ((), ()))
        a_spec = pl.BlockSpec((tm, tk), lambda i, j, kk: (i, kk))
        b_spec = pl.BlockSpec((tn, tk), lambda i, j, kk: (j, kk))
    else:
        dn = (((0,), (0,)), ((), ()))
        a_spec = pl.BlockSpec((tk, tm), lambda i, j, kk: (kk, i))
        b_spec = pl.BlockSpec((tk, tn), lambda i, j, kk: (kk, j))
    per_half = D_FF // FF_TILE
    if halves == "a":
        assert tk == FF_TILE
        a_spec = pl.BlockSpec((None, tm, tk), lambda i, j, kk: (kk // per_half, i, kk % per_half))
    elif halves == "b":
        assert tn == FF_TILE
        b_spec = pl.BlockSpec((None, tk, tn), lambda i, j, kk: (j // per_half, kk, j % per_half))

    def body(a_ref, b_ref, o_ref, *acc):
        d = lax.dot_general(a_ref[...], b_ref[...], dn, preferred_element_type=F32)
        if nk == 1:
            o_ref[...] = d.astype(o_ref.dtype)
            return
        acc_ref, = acc
        kk = pl.program_id(2)

        @pl.when(kk == 0)
        def _():
            acc_ref[...] = d

        @pl.when((kk > 0) & (kk < nk - 1))
        def _():
            acc_ref[...] += d

        @pl.when(kk == nk - 1)
        def _():
            o_ref[...] = (acc_ref[...] + d).astype(o_ref.dtype)

    return pl.pallas_call(
        body, name=name, grid=(m // tm, n // tn, nk),
        in_specs=[a_spec, b_spec],
        out_specs=pl.BlockSpec((tm, tn), lambda i, j, kk: (i, j)),
        out_shape=jax.ShapeDtypeStruct((m, n), out_dtype),
        scratch_shapes=[pltpu.VMEM((tm, tn), F32)] if nk > 1 else [],
        compiler_params=_cp("parallel", "parallel", "arbitrary"),
    )(a, b)


def _rows(body, name, t, tb, row_in, full_in, row_out, acc_out):
    in_specs, args = [], []
    for r in row_in:
        if isinstance(r, tuple):
            arr, w, j = r
            in_specs.append(pl.BlockSpec((tb, w), functools.partial(lambda i, jj: (i, jj), jj=j)))
            args.append(arr)
        else:
            in_specs.append(pl.BlockSpec((tb, r.shape[1]), lambda i: (i, 0)))
            args.append(r)
    for f in full_in:
        in_specs.append(pl.BlockSpec(f.shape, functools.partial(lambda i, nd: (0,) * nd, nd=f.ndim)))
        args.append(f)
    out_specs = [pl.BlockSpec((tb, c), lambda i: (i, 0)) for c, _ in row_out]
    out_specs += [pl.BlockSpec(s, functools.partial(lambda i, nd: (0,) * nd, nd=len(s))) for s in acc_out]
    out_shape = [jax.ShapeDtypeStruct((t, c), dt) for c, dt in row_out]
    out_shape += [jax.ShapeDtypeStruct(s, F32) for s in acc_out]
    return pl.pallas_call(
        body, name=name, grid=(t // tb,), in_specs=in_specs, out_specs=out_specs, out_shape=out_shape,
        compiler_params=_cp("arbitrary"),
    )(*args)


def _ln_stats(r):
    mu = jnp.mean(r, axis=-1, keepdims=True)
    xc = r - mu
    var = jnp.mean(xc * xc, axis=-1, keepdims=True)
    rstd = lax.rsqrt(var + LN_EPS)
    return xc * rstd, rstd


def _ln_bwd(dout, xhat, rstd, g):
    dxh = dout * g
    m1 = jnp.mean(dxh, axis=-1, keepdims=True)
    m2 = jnp.mean(dxh * xhat, axis=-1, keepdims=True)
    return rstd * (dxh - m1 - xhat * m2)


def _first(i, acc_refs):
    @pl.when(i == 0)
    def _():
        for a in acc_refs:
            a[...] = jnp.zeros_like(a)


def _modulated(xv, m_ref):
    return (xv * (1.0 + m_ref[1:2, :]) + m_ref[0:1, :]).astype(BF16)


def _ln_in_fwd(x, g, b, next_mod3):
    t = x.shape[0]

    def body(x_ref, g_ref, b_ref, m_ref, o_ref, h_ref):
        xhat, _ = _ln_stats(x_ref[...])
        out = xhat * g_ref[...] + b_ref[...]
        o_ref[...] = out
        h_ref[...] = _modulated(out, m_ref)

    return _rows(body, "ln_in_fwd", t, 256, [x], [g, b, next_mod3], [(D, F32), (D, BF16)], [])


def _ln_in_bwd(dx0, x, g):
    t = x.shape[0]

    def body(d_ref, x_ref, g_ref, o_ref, acc_ref):
        _first(pl.program_id(0), [acc_ref])
        xhat, rstd = _ln_stats(x_ref[...])
        d = d_ref[...]
        o_ref[...] = _ln_bwd(d, xhat, rstd, g_ref[...])
        acc_ref[0:1, :] += jnp.sum(d * xhat, axis=0, keepdims=True)
        acc_ref[1:2, :] += jnp.sum(d, axis=0, keepdims=True)

    return _rows(body, "ln_in_bwd", t, 256, [dx0, x], [g], [(D, F32)], [(2, D)])


def _modulate_bwd(dxres, dh, xin, mod3, name):
    t = xin.shape[0]

    def body(r_ref, dh_ref, x_ref, m_ref, o_ref, acc_ref):
        _first(pl.program_id(0), [acc_ref])
        dh_v = dh_ref[...]
        o_ref[...] = r_ref[...] + dh_v * (1.0 + m_ref[1:2, :])
        acc_ref[0:1, :] += jnp.sum(dh_v, axis=0, keepdims=True)
        acc_ref[1:2, :] += jnp.sum(dh_v * x_ref[...], axis=0, keepdims=True)

    return _rows(body, name, t, 256, [dxres, dh, xin], [mod3], [(D, F32)], [(2, D)])


def _ffn_in_swiglu(h, w_in, name):
    t = h.shape[0]
    tm = _pick(t, (512, 256, 128))
    nj = D_FF // FF_TILE

    def body(h_ref, wg_ref, wu_ref, u_ref, a_ref):
        hv = h_ref[...]
        gate = _dot(hv, wg_ref[...])
        up = _dot(hv, wu_ref[...])
        sg = _sigmoid(gate)
        silu = gate * sg
        u_ref[0] = (up * (sg * (1.0 + gate * (1.0 - sg)))).astype(BF16)
        u_ref[1] = silu.astype(BF16)
        a_ref[...] = (silu * up).astype(BF16)

    return pl.pallas_call(
        body, name=name, grid=(nj, t // tm),
        in_specs=[pl.BlockSpec((tm, D), lambda j, i: (i, 0)),
                  pl.BlockSpec((D, FF_TILE), lambda j, i: (0, j)),
                  pl.BlockSpec((D, FF_TILE), lambda j, i: (0, nj + j))],
        out_specs=[pl.BlockSpec((2, tm, FF_TILE), lambda j, i: (0, i, j)),
                   pl.BlockSpec((tm, FF_TILE), lambda j, i: (i, j))],
        out_shape=[jax.ShapeDtypeStruct((2, t, D_FF), BF16), jax.ShapeDtypeStruct((t, D_FF), BF16)],
        compiler_params=_cp("parallel", "parallel"),
    )(h, w_in, w_in)


def _ffn_out_dx_swiglu(dy, w_out, u, name):
    t = dy.shape[0]
    tm = _pick(t, (512, 256, 128))
    nj = D_FF // FF_TILE

    def body(dy_ref, w_ref, u_ref, du_ref):
        da = _dot_nt(dy_ref[...], w_ref[...])
        du_ref[0] = (da * u_ref[0].astype(F32)).astype(BF16)
        du_ref[1] = (da * u_ref[1].astype(F32)).astype(BF16)

    blk3 = pl.BlockSpec((2, tm, FF_TILE), lambda j, i: (0, i, j))
    return pl.pallas_call(
        body, name=name, grid=(nj, t // tm),
        in_specs=[pl.BlockSpec((tm, D), lambda j, i: (i, 0)),
                  pl.BlockSpec((FF_TILE, D), lambda j, i: (j, 0)), blk3],
        out_specs=blk3,
        out_shape=jax.ShapeDtypeStruct((2, t, D_FF), BF16),
        compiler_params=_cp("parallel", "parallel"),
    )(dy, w_out, u)


def _res_ln(xin, y, mod3, lg, lb, factor, name, next_mod3=None):
    t = xin.shape[0]

    def body(x_ref, y_ref, m_ref, g_ref, b_ref, *rest):
        r = ALPHA * x_ref[...] + (factor * m_ref[2:3, :]) * y_ref[...]
        xhat, _ = _ln_stats(r)
        out = xhat * g_ref[...] + b_ref[...]
        if next_mod3 is None:
            rest[0][...] = out
        else:
            rest[1][...] = out
            rest[2][...] = _modulated(out, rest[0])

    if next_mod3 is None:
        return _rows(body, name, t, 256, [xin, y], [mod3, lg, lb], [(D, F32)], [])[0], None
    return _rows(body, name, t, 256, [xin, y], [mod3, lg, lb, next_mod3], [(D, F32), (D, BF16)], [])


def _mod_res_bwd(dxres, dh, mod3, xin_p, y_p, mod3_p, lg_p, lb_p, factor_p, name):
    t = dxres.shape[0]

    def body(r_ref, dh_ref, xp_ref, yp_ref, m_ref, mp_ref, g_ref, b_ref, dres_ref, dy_ref, acc_ref):
        _first(pl.program_id(0), [acc_ref])
        gate = factor_p * mp_ref[2:3, :]
        yv = yp_ref[...]
        xhat, rstd = _ln_stats(ALPHA * xp_ref[...] + gate * yv)
        xin = xhat * g_ref[...] + b_ref[...]
        dh_v = dh_ref[...]
        d = r_ref[...] + dh_v * (1.0 + m_ref[1:2, :])
        dr = _ln_bwd(d, xhat, rstd, g_ref[...])
        dres_ref[...] = ALPHA * dr
        dy_ref[...] = (gate * dr).astype(BF16)
        acc_ref[0:1, :] += jnp.sum(dh_v, axis=0, keepdims=True)
        acc_ref[1:2, :] += jnp.sum(dh_v * xin, axis=0, keepdims=True)
        acc_ref[2:3, :] += jnp.sum(d * xhat, axis=0, keepdims=True)
        acc_ref[3:4, :] += jnp.sum(d, axis=0, keepdims=True)
        acc_ref[4:5, :] += jnp.sum(factor_p * yv * dr, axis=0, keepdims=True)

    return _rows(body, name, t, 256, [dxres, dh, xin_p, y_p], [mod3, mod3_p, lg_p, lb_p],
                 [(D, F32), (D, BF16)], [(5, D)])


def _res_ln_bwd(dout, xin, y, mod3, lg, factor, name):
    t = xin.shape[0]

    def body(d_ref, x_ref, y_ref, m_ref, g_ref, dres_ref, dy_ref, acc_ref):
        _first(pl.program_id(0), [acc_ref])
        gate = factor * m_ref[2:3, :]
        yv = y_ref[...]
        r = ALPHA * x_ref[...] + gate * yv
        xhat, rstd = _ln_stats(r)
        d = d_ref[...]
        dr = _ln_bwd(d, xhat, rstd, g_ref[...])
        dres_ref[...] = ALPHA * dr
        dy_ref[...] = (gate * dr).astype(BF16)
        acc_ref[0:1, :] += jnp.sum(d * xhat, axis=0, keepdims=True)
        acc_ref[1:2, :] += jnp.sum(d, axis=0, keepdims=True)
        acc_ref[2:3, :] += jnp.sum(factor * yv * dr, axis=0, keepdims=True)

    return _rows(body, name, t, 256, [dout, xin, y], [mod3, lg], [(D, F32), (D, BF16)], [(3, D)])


def _loss_head(xf, tgt):
    t = xf.shape[0]

    def body(x_ref, t_ref, d_ref, acc_ref):
        _first(pl.program_id(0), [acc_ref])
        e = x_ref[...] - t_ref[...]
        d_ref[...] = e * (1.0 / D)
        part = 0.5 * jnp.sum(jnp.mean(e * e, axis=-1, keepdims=True), axis=0, keepdims=True)
        acc_ref[...] += jnp.broadcast_to(part, acc_ref.shape)

    return _rows(body, "loss_head", t, 256, [xf, tgt], [], [(D, F32)], [(1, LANES)])


def _silu_bf16(c_all):
    def body(c_ref, o_ref):
        v = c_ref[...]
        o_ref[...] = (v * _sigmoid(v)).astype(BF16)

    return _rows(body, "silu_c", c_all.shape[0], c_all.shape[0], [c_all], [], [(c_all.shape[1], BF16)], [])[0]


def _sum_rows(v, name):
    r, n = v.shape
    tn = _pick(n, (8192, 4096, 2048, 1024, 512, 256, 128))

    def body(v_ref, o_ref):
        acc = v_ref[0:1, :]
        for k in range(1, r):
            acc = acc + v_ref[k:k + 1, :]
        o_ref[...] = acc

    return pl.pallas_call(
        body, name=name, grid=(n // tn,),
        in_specs=[pl.BlockSpec((r, tn), lambda j: (0, j))],
        out_specs=pl.BlockSpec((1, tn), lambda j: (0, j)),
        out_shape=jax.ShapeDtypeStruct((1, n), F32),
        compiler_params=_cp("parallel"),
    )(v)


def _elementwise(fn, name, ins, out_dtypes):
    r, c = ins[0].shape
    tb = _pick(r, (128, 64, 32, 16, 8))
    n_in = len(ins)

    def body(*refs):
        outs = fn(*[x[...] for x in refs[:n_in]])
        for o_ref, o in zip(refs[n_in:], outs):
            o_ref[...] = o.astype(o_ref.dtype)

    spec = pl.BlockSpec((tb, c), lambda i: (i, 0))
    return pl.pallas_call(
        body, name=name, grid=(r // tb,), in_specs=[spec] * n_in, out_specs=[spec] * len(out_dtypes),
        out_shape=[jax.ShapeDtypeStruct((r, c), dt) for dt in out_dtypes],
        compiler_params=_cp("parallel"),
    )(*ins)


def _adamw_math(w, g, m, v):
    m = ADAM_B1 * m + (1.0 - ADAM_B1) * g
    v = ADAM_B2 * v + (1.0 - ADAM_B2) * (g * g)
    m_hat = m / (1.0 - ADAM_B1 ** ADAM_STEP)
    v_hat = v / (1.0 - ADAM_B2 ** ADAM_STEP)
    delta = -ADAM_LR * (m_hat / (jnp.sqrt(v_hat) + ADAM_EPS) + ADAM_WD * w)
    return delta, m, v


def _adamw(w, g, m, v, name):
    shape = w.shape
    c = shape[-1]
    w2, g2, m2, v2 = (a.reshape(-1, c) for a in (w, g, m, v))
    outs = _elementwise(_adamw_math, name, [w2, g2, m2, v2], [F32, F32, F32])
    return tuple(o.reshape(shape) for o in outs)


def _remote_exchange(ins, plan, peers_of, name):
    n_in, n_out = len(ins), len(plan)

    def body(*refs):
        in_refs, out_refs = refs[:n_in], refs[n_in:n_in + n_out]
        send_sems, recv_sems = refs[n_in + n_out], refs[n_in + n_out + 1]
        peers = peers_of(lax.axis_index("x"), lax.axis_index("y"), lax.axis_index("c"))
        copies = [
            pltpu.make_async_remote_copy(
                src_ref=in_refs[src], dst_ref=out_refs[k], send_sem=send_sems.at[k], recv_sem=recv_sems.at[k],
                device_id=peers[peer], device_id_type=MESH)
            for k, (peer, src) in enumerate(plan)
        ]
        for cp in copies:
            cp.start()
        for cp in copies:
            cp.wait()

    any_spec = pl.BlockSpec(memory_space=pl.ANY)
    return list(pl.pallas_call(
        body, name=name,
        in_specs=[any_spec] * n_in, out_specs=[any_spec] * n_out,
        out_shape=[jax.ShapeDtypeStruct(ins[src].shape, ins[src].dtype) for _, src in plan],
        scratch_shapes=[pltpu.SemaphoreType.DMA((n_out,)), pltpu.SemaphoreType.DMA((n_out,))],
    )(*ins))


def _sibling(x, y, c):
    return [(x, y, 1 - c)]


def _other_chips(x, y, c):
    return [(1 - x, y, c), (x, 1 - y, c), (1 - x, 1 - y, c)]


def _swap_sibling(arrs, name):
    return _remote_exchange(arrs, [(0, i) for i in range(len(arrs))], _sibling, name)


def _bcast_chips(arrs, name):
    n = len(arrs)
    out = _remote_exchange(arrs, [(p, i) for p in range(3) for i in range(n)], _other_chips, name)
    return [out[p * n:(p + 1) * n] for p in range(3)]


def _by_chip(me, own, got3):
    by_rel = [own, got3[0], got3[1], got3[2]]
    rel_bits = (0, 2, 1, 3)

    def branch(m):
        def f(ops):
            return [ops[rel_bits.index(i ^ m)] for i in range(4)]
        return f

    return lax.switch(me, [branch(m) for m in range(4)], by_rel)


BIG = ("ffn1_w_in", "ffn1_w_out", "mix_w_in", "mix_w_out", "ffn2_w_in", "ffn2_w_out")
BY_COLUMNS = ("ffn1_w_in", "ffn2_w_in")


def _layer_shape(n, shard_shape):
    r, cs = shard_shape
    return (r, 4 * cs) if n in BY_COLUMNS else (4, r, cs)


def _chip_ids(x, y):
    chips = [(1 - x, y), (x, 1 - y), (1 - x, 1 - y)]
    return chips, [2 * cx + cy for cx, cy in chips]


def _half_slot(ref, n, chip, h):
    if n in BY_COLUMNS:
        hr, w = ref.shape[0] // 2, ref.shape[1] // 4
        return ref.at[pl.ds(pl.multiple_of(h * hr, 16), hr), pl.ds(pl.multiple_of(chip * w, LANES), w)]
    hr = ref.shape[1] // 2
    return ref.at[chip, pl.ds(pl.multiple_of(h * hr, 16), hr)]


def _gather_body(names, layer, in_refs, out_refs, sems, handshake):
    send_chip, recv_chip, send_sib, recv_sib, local, send_fwd, recv_fwd = sems
    n_w = len(names)
    x, y, c = lax.axis_index("x"), lax.axis_index("y"), lax.axis_index("c")
    me = 2 * x + y
    chips, chip_idx = _chip_ids(x, y)
    sibling = (x, y, 1 - c)
    if handshake:
        _shake_hands([sibling] + [(*ch, c) for ch in chips])
    remote = _remote

    sends, own_copies = [], []
    for i, n in enumerate(names):
        shard = in_refs[i].at[layer]
        hr = shard.shape[0] // 2
        src = shard.at[pl.ds(pl.multiple_of(c * hr, 16), hr)]
        mine = _half_slot(out_refs[i], n, me, c)
        own = pltpu.make_async_copy(src, mine, local.at[i])
        own.start()
        own_copies.append(own)
        sends.append(remote(src, mine, send_sib.at[i], recv_sib.at[i], sibling))
        sends[-1].start()
        for p in range(3):
            k = p * n_w + i
            sends.append(remote(src, mine, send_chip.at[k], recv_chip.at[k], (*chips[p], c)))
            sends[-1].start()
    for p in range(3):
        for i, n in enumerate(names):
            k = p * n_w + i
            landed = _half_slot(out_refs[i], n, chip_idx[p], c)
            remote(landed, landed, send_chip.at[k], recv_chip.at[k], sibling).wait_recv()
            sends.append(remote(landed, landed, send_fwd.at[k], recv_fwd.at[k], sibling))
            sends[-1].start()
    for i, n in enumerate(names):
        theirs = _half_slot(out_refs[i], n, me, 1 - c)
        remote(theirs, theirs, send_sib.at[i], recv_sib.at[i], sibling).wait_recv()
        for p in range(3):
            k = p * n_w + i
            theirs = _half_slot(out_refs[i], n, chip_idx[p], 1 - c)
            remote(theirs, theirs, send_fwd.at[k], recv_fwd.at[k], sibling).wait_recv()
    for own in own_copies:
        own.wait()
    for cp in sends:
        cp.wait_send()


def _gather_weights(names, shards, layer, after, name, sequencer_id):
    n_w = len(names)

    def body(in_refs, out_refs, sems, handshake):
        _gather_body(names, layer, in_refs, out_refs, sems, handshake)

    dma = pltpu.SemaphoreType.DMA
    sems = [dma((3 * n_w,)), dma((3 * n_w,)), dma((n_w,)), dma((n_w,)), dma((n_w,)), dma((3 * n_w,)), dma((3 * n_w,))]
    shapes = [jax.ShapeDtypeStruct(_layer_shape(n, s.shape[1:]), BF16) for n, s in zip(names, shards)]
    return _comm_call(body, list(shards) + list(after), shapes, sems, name, sequencer_id)


def _comm_call(body_fn, ins, out_shapes, sem_types, name, sequencer_id):
    n_in, n_out = len(ins), len(out_shapes)
    sequencer = sequencer_id is not None

    def body(*refs):
        body_fn(refs[:n_in], refs[n_in:n_in + n_out], refs[n_in + n_out:], sequencer)

    if sequencer:
        return list(pl.kernel(
            body, name=name, out_type=out_shapes,
            mesh=plsc.ScalarSubcoreMesh(axis_name="sequencer", num_cores=1), scratch_types=sem_types,
            compiler_params=pltpu.CompilerParams(collective_id=sequencer_id),
        )(*ins))
    any_spec = pl.BlockSpec(memory_space=pl.ANY)
    return list(pl.pallas_call(
        body, name=name, in_specs=[any_spec] * n_in, out_specs=[any_spec] * n_out, out_shape=out_shapes,
        scratch_shapes=sem_types,
    )(*ins))


def _shake_hands(peers):
    barrier = pltpu.get_barrier_semaphore()
    for peer in peers:
        pl.semaphore_signal(barrier, inc=1, device_id=peer, device_id_type=MESH)
    pl.semaphore_wait(barrier, len(peers))


def _remote(src, dst, s_sem, r_sem, to):
    return pltpu.make_async_remote_copy(src_ref=src, dst_ref=dst, send_sem=s_sem, recv_sem=r_sem,
                                        device_id=to, device_id_type=MESH)


def _rows_half(ref, n, h):
    if n in BY_COLUMNS:
        hr = ref.shape[0] // 2
        return ref.at[pl.ds(pl.multiple_of(h * hr, 16), hr)]
    hr = ref.shape[1] // 2
    return ref.at[:, pl.ds(pl.multiple_of(h * hr, 16), hr)]


def _half_form_shape(n, wire_shape):
    if n in BY_COLUMNS:
        return (wire_shape[0] // 2, wire_shape[1])
    return (wire_shape[0], wire_shape[1] // 2, wire_shape[2])


def _grad_swap_halves(names, wire, name, sequencer_id):
    n_w = len(names)

    def body(in_refs, out_refs, sems, handshake):
        send, recv = sems
        x, y, c = lax.axis_index("x"), lax.axis_index("y"), lax.axis_index("c")
        sibling = (x, y, 1 - c)
        if handshake:
            _shake_hands([sibling])
        cps = [_remote(_rows_half(in_refs[i], n, 1 - c), out_refs[i], send.at[i], recv.at[i], sibling)
               for i, n in enumerate(names)]
        for cp in cps:
            cp.start()
        for cp in cps:
            cp.wait()

    shapes = [jax.ShapeDtypeStruct(_half_form_shape(n, w.shape), w.dtype) for n, w in zip(names, wire)]
    dma = pltpu.SemaphoreType.DMA
    return _comm_call(body, list(wire), shapes, [dma((n_w,)), dma((n_w,))], name, sequencer_id)


def _grad_add_halves(g, got, ci_arr, after, name):
    g3 = g if g.ndim == 3 else g[None]
    r3 = got if got.ndim == 3 else got[None]
    s, hr, cc = r3.shape
    tb = _pick(hr, (256, 176, 128))
    nb = hr // tb

    def body(s_ref, g_ref, r_ref, after_ref, o_ref):
        o_ref[...] = (g_ref[...].astype(F32) + r_ref[...].astype(F32)).astype(BF16)

    out = pl.pallas_call(
        body, name=name,
        grid_spec=pltpu.PrefetchScalarGridSpec(
            num_scalar_prefetch=1, grid=(nb,),
            in_specs=[pl.BlockSpec((s, tb, cc), lambda i, sc: (0, sc[0] * nb + i, 0)),
                      pl.BlockSpec((s, tb, cc), lambda i, sc: (0, i, 0)),
                      pl.BlockSpec(memory_space=pl.ANY)],
            out_specs=pl.BlockSpec((s, tb, cc), lambda i, sc: (0, i, 0))),
        out_shape=jax.ShapeDtypeStruct(r3.shape, BF16),
        compiler_params=_cp("arbitrary"),
    )(ci_arr, g3, r3, after)
    return out.reshape(got.shape)


def _grad_scatter_halves(names, parts, name, sequencer_id):
    n_w = len(names)

    def slot(ref, n, chip):
        if n in BY_COLUMNS:
            w = ref.shape[1] // 4
            return ref.at[:, pl.ds(pl.multiple_of(chip * w, LANES), w)]
        return ref.at[chip]

    def body(in_refs, out_refs, sems, handshake):
        send, recv = sems
        x, y, c = lax.axis_index("x"), lax.axis_index("y"), lax.axis_index("c")
        chips, chip_idx = _chip_ids(x, y)
        if handshake:
            _shake_hands([(*ch, c) for ch in chips])
        cps = []
        for p in range(3):
            for i, n in enumerate(names):
                k = p * n_w + i
                cps.append(_remote(slot(in_refs[i], n, chip_idx[p]), out_refs[k], send.at[k], recv.at[k],
                                   (*chips[p], c)))
        for cp in cps:
            cp.start()
        for cp in cps:
            cp.wait()

    def slot_shape(n, a):
        return (a.shape[0], a.shape[1] // 4) if n in BY_COLUMNS else a.shape[1:]

    shapes = [jax.ShapeDtypeStruct(slot_shape(n, a), BF16) for _ in range(3) for n, a in zip(names, parts)]
    dma = pltpu.SemaphoreType.DMA
    out = _comm_call(body, list(parts), shapes, [dma((3 * n_w,)), dma((3 * n_w,))], name, sequencer_id)
    return [out[p * n_w:(p + 1) * n_w] for p in range(3)]


def _grad_add_slots(part, got3, n, me_arr, after, name):
    hr, cs = got3[0].shape
    tb = _pick(hr, (256, 176, 128))

    def body(s_ref, own_ref, a_ref, b_ref, c_ref, after_ref, out_ref):
        acc = own_ref[...].astype(F32) + a_ref[...].astype(F32)
        out_ref[...] = (acc + b_ref[...].astype(F32)) + c_ref[...].astype(F32)

    if n in BY_COLUMNS:
        own_spec = pl.BlockSpec((tb, cs), lambda i, s: (i, s[0]))
    else:
        own_spec = pl.BlockSpec((None, tb, cs), lambda i, s: (s[0], i, 0))
    plain = pl.BlockSpec((tb, cs), lambda i, s: (i, 0))
    return pl.pallas_call(
        body, name=name,
        grid_spec=pltpu.PrefetchScalarGridSpec(
            num_scalar_prefetch=1, grid=(hr // tb,),
            in_specs=[own_spec, plain, plain, plain, pl.BlockSpec(memory_space=pl.ANY)],
            out_specs=plain),
        out_shape=jax.ShapeDtypeStruct((hr, cs), F32),
        compiler_params=_cp("arbitrary"),
    )(me_arr, part, *got3, after)


def _grad_swap_reduced(halves, name, sequencer_id):
    n_w = len(halves)

    def body(in_refs, out_refs, sems, handshake):
        send, recv = sems
        x, y, c = lax.axis_index("x"), lax.axis_index("y"), lax.axis_index("c")
        sibling = (x, y, 1 - c)
        if handshake:
            _shake_hands([sibling])
        cps = [_remote(in_refs[i], out_refs[i], send.at[i], recv.at[i], sibling) for i in range(n_w)]
        for cp in cps:
            cp.start()
        for cp in cps:
            cp.wait()

    shapes = [jax.ShapeDtypeStruct(h.shape, F32) for h in halves]
    dma = pltpu.SemaphoreType.DMA
    return _comm_call(body, list(halves), shapes, [dma((n_w,)), dma((n_w,))], name, sequencer_id)


def _adamw_layers(w, own, other, m, v, ci_arr, name):
    _, rs, cs = w.shape
    hr = rs // 2
    tb = _pick(hr, (256, 176, 128))
    nbh = hr // tb

    def body(s_ref, w_ref, o0, t0, o1, t1, m_ref, v_ref, g_out, d_out, m_out, v_out):
        mine = (pl.program_id(1) // nbh) == s_ref[0]
        g = jnp.where(pl.program_id(0) == 0, jnp.where(mine, o0[...], t0[...]), jnp.where(mine, o1[...], t1[...]))
        g_out[...] = g
        d_out[...], m_out[...], v_out[...] = _adamw_math(w_ref[...], g, m_ref[...], v_ref[...])

    both = pl.BlockSpec((None, tb, cs), lambda l, i, s: (l, i, 0))

    def half(layer, is_own):
        def index(l, i, s):
            own_block = (i // nbh) == s[0]
            use = (l == layer) & (own_block if is_own else jnp.logical_not(own_block))
            return (jnp.where(use, i % nbh, 0), 0)
        return pl.BlockSpec((tb, cs), index)

    return pl.pallas_call(
        body, name=name,
        grid_spec=pltpu.PrefetchScalarGridSpec(
            num_scalar_prefetch=1, grid=(DEPTH, rs // tb),
            in_specs=[both, half(0, True), half(0, False), half(1, True), half(1, False), both, both],
            out_specs=[both] * 4),
        out_shape=[jax.ShapeDtypeStruct(w.shape, F32)] * 4,
        compiler_params=_cp("arbitrary", "arbitrary"),
    )(ci_arr, w, own[0], other[0], own[1], other[1], m, v)


def _shift_down(v, s, t_iota):
    return jnp.where(t_iota >= s, pltpu.roll(v, s, 0), 0.0)


def _shift_up(v, s, t_iota, t):
    return jnp.where(t_iota < t - s, pltpu.roll(v, t - s, 0), 0.0)


def _ssd_conv_fwd(proj, w, b):
    t = proj.shape[0]
    k_w = SSD_CONV_K

    def body(x_ref, w_ref, b_ref, o_ref):
        x = x_ref[...]
        ti = lax.broadcasted_iota(jnp.int32, x.shape, 0)
        pre = x * w_ref[k_w - 1:k_w, :] + b_ref[...]
        for s in range(1, k_w):
            pre = pre + _shift_down(x, s, ti) * w_ref[k_w - 1 - s:k_w - s, :]
        o_ref[...] = pre * _sigmoid(pre)

    off = P_XBC // LANES
    return pl.pallas_call(
        body, name="ssd_conv_fwd", grid=(SSD_CONV_DIM // LANES,),
        in_specs=[pl.BlockSpec((t, LANES), lambda j: (0, off + j)),
                  pl.BlockSpec((k_w, LANES), lambda j: (0, j)),
                  pl.BlockSpec((1, LANES), lambda j: (0, j))],
        out_specs=pl.BlockSpec((t, LANES), lambda j: (0, j)),
        out_shape=jax.ShapeDtypeStruct((t, SSD_CONV_DIM), F32),
        compiler_params=_cp("parallel"),
    )(proj, w, b)


def _ssd_conv_bwd(dxbc, proj, w, b):
    t = proj.shape[0]
    k_w = SSD_CONV_K

    def body(d_ref, x_ref, w_ref, b_ref, dx_ref, dw_ref, db_ref):
        x = x_ref[...]
        ti = lax.broadcasted_iota(jnp.int32, x.shape, 0)
        shifted = [x] + [_shift_down(x, s, ti) for s in range(1, k_w)]
        pre = b_ref[...] + shifted[0] * w_ref[k_w - 1:k_w, :]
        for s in range(1, k_w):
            pre = pre + shifted[s] * w_ref[k_w - 1 - s:k_w - s, :]
        sg = _sigmoid(pre)
        dpre = d_ref[...] * (sg * (1.0 + pre * (1.0 - sg)))
        db_ref[...] = jnp.sum(dpre, axis=0, keepdims=True)
        dx = dpre * w_ref[k_w - 1:k_w, :]
        for s in range(k_w):
            dw_ref[k_w - 1 - s:k_w - s, :] = jnp.sum(dpre * shifted[s], axis=0, keepdims=True)
            if s:
                dx = dx + _shift_up(dpre, s, ti, t) * w_ref[k_w - 1 - s:k_w - s, :]
        dx_ref[...] = dx

    off = P_XBC // LANES
    return pl.pallas_call(
        body, name="ssd_conv_bwd", grid=(SSD_CONV_DIM // LANES,),
        in_specs=[pl.BlockSpec((t, LANES), lambda j: (0, j)),
                  pl.BlockSpec((t, LANES), lambda j: (0, off + j)),
                  pl.BlockSpec((k_w, LANES), lambda j: (0, j)),
                  pl.BlockSpec((1, LANES), lambda j: (0, j))],
        out_specs=[pl.BlockSpec((t, LANES), lambda j: (0, j)),
                   pl.BlockSpec((k_w, LANES), lambda j: (0, j)),
                   pl.BlockSpec((1, LANES), lambda j: (0, j))],
        out_shape=[jax.ShapeDtypeStruct((t, SSD_CONV_DIM), F32),
                   jax.ShapeDtypeStruct((k_w, SSD_CONV_DIM), F32),
                   jax.ShapeDtypeStruct((1, SSD_CONV_DIM), F32)],
        compiler_params=_cp("parallel"),
    )(dxbc, proj, w, b)


def _shortconv_fwd(proj, w):
    t = proj.shape[0]
    nb = SC_W // LANES
    off = P_SC // LANES

    def body(b_ref, c_ref, x_ref, w_ref, o_ref):
        u = c_ref[...] * x_ref[...]
        ti = lax.broadcasted_iota(jnp.int32, u.shape, 0)
        cv = u * w_ref[SC_K - 1:SC_K, :]
        for s in range(1, SC_K):
            cv = cv + _shift_down(u, s, ti) * w_ref[SC_K - 1 - s:SC_K - s, :]
        o_ref[...] = (b_ref[...] * cv).astype(BF16)

    return pl.pallas_call(
        body, name="shortconv_fwd", grid=(nb,),
        in_specs=[pl.BlockSpec((t, LANES), lambda j: (0, off + j)),
                  pl.BlockSpec((t, LANES), lambda j: (0, off + nb + j)),
                  pl.BlockSpec((t, LANES), lambda j: (0, off + 2 * nb + j)),
                  pl.BlockSpec((SC_K, LANES), lambda j: (0, j))],
        out_specs=pl.BlockSpec((t, LANES), lambda j: (0, j)),
        out_shape=jax.ShapeDtypeStruct((t, SC_W), BF16),
        compiler_params=_cp("parallel"),
    )(proj, proj, proj, w)


def _shortconv_bwd(dy, dy_off, proj, w):
    t = proj.shape[0]
    nb = SC_W // LANES
    off = P_SC // LANES
    doff = dy_off // LANES

    def body(d_ref, b_ref, c_ref, x_ref, w_ref, db_ref, dc_ref, dx_ref, dw_ref):
        cg, xin = c_ref[...], x_ref[...]
        u = cg * xin
        ti = lax.broadcasted_iota(jnp.int32, u.shape, 0)
        shifted = [u] + [_shift_down(u, s, ti) for s in range(1, SC_K)]
        cv = shifted[0] * w_ref[SC_K - 1:SC_K, :]
        for s in range(1, SC_K):
            cv = cv + shifted[s] * w_ref[SC_K - 1 - s:SC_K - s, :]
        d = d_ref[...]
        db_ref[...] = (d * cv).astype(BF16)
        dcv = d * b_ref[...]
        du = dcv * w_ref[SC_K - 1:SC_K, :]
        for s in range(SC_K):
            dw_ref[SC_K - 1 - s:SC_K - s, :] = jnp.sum(dcv * shifted[s], axis=0, keepdims=True)
            if s:
                du = du + _shift_up(dcv, s, ti, t) * w_ref[SC_K - 1 - s:SC_K - s, :]
        dc_ref[...] = (du * xin).astype(BF16)
        dx_ref[...] = (du * cg).astype(BF16)

    tile = pl.BlockSpec((t, LANES), lambda j: (0, j))
    outs = pl.pallas_call(
        body, name="shortconv_bwd", grid=(nb,),
        in_specs=[pl.BlockSpec((t, LANES), lambda j: (0, doff + j)),
                  pl.BlockSpec((t, LANES), lambda j: (0, off + j)),
                  pl.BlockSpec((t, LANES), lambda j: (0, off + nb + j)),
                  pl.BlockSpec((t, LANES), lambda j: (0, off + 2 * nb + j)),
                  pl.BlockSpec((SC_K, LANES), lambda j: (0, j))],
        out_specs=[tile, tile, tile, pl.BlockSpec((SC_K, LANES), lambda j: (0, j))],
        out_shape=[jax.ShapeDtypeStruct((t, SC_W), BF16)] * 3 + [jax.ShapeDtypeStruct((SC_K, SC_W), F32)],
        compiler_params=_cp("parallel"),
    )(dy, proj, proj, proj, w)
    return outs


def _cum_logf(proj, fbias_row):
    t = proj.shape[0]
    blk = CUM_BLOCK

    def body(p_ref, b_ref, o_ref, carry_ref):
        i = pl.program_id(0)

        @pl.when(i == 0)
        def _():
            carry_ref[...] = jnp.zeros_like(carry_ref)

        lf = -_softplus(-(p_ref[...] + b_ref[...]))
        r = lax.broadcasted_iota(jnp.int32, (blk, blk), 0)
        c = lax.broadcasted_iota(jnp.int32, (blk, blk), 1)
        tri = (r >= c).astype(BF16)
        o_ref[...] = _tri_left(tri, lf) + carry_ref[...]
        carry_ref[...] = o_ref[blk - 1:blk, :]

    return pl.pallas_call(
        body, name="cum_logf", grid=(t // blk,),
        in_specs=[pl.BlockSpec((blk, LANES), lambda i: (i, P_PAD // LANES)),
                  pl.BlockSpec((1, LANES), lambda i: (0, 0))],
        out_specs=pl.BlockSpec((blk, LANES), lambda i: (i, 0)),
        out_shape=jax.ShapeDtypeStruct((t, LANES), F32),
        scratch_shapes=[pltpu.VMEM((1, LANES), F32)],
        compiler_params=_cp("arbitrary"),
    )(proj, fbias_row)


def _pad_block_bwd(dcf, ddt, proj, fbias_row):
    t = proj.shape[0]
    blk = CUM_BLOCK
    nb = t // blk

    def body(dcf_ref, ddt_ref, p_ref, b_ref, o_ref, db_ref, carry_ref):
        i = pl.program_id(0)

        @pl.when(i == 0)
        def _():
            carry_ref[...] = jnp.zeros_like(carry_ref)
            db_ref[...] = jnp.zeros_like(db_ref)

        r = lax.broadcasted_iota(jnp.int32, (blk, blk), 0)
        c = lax.broadcasted_iota(jnp.int32, (blk, blk), 1)
        tri = (r <= c).astype(BF16)
        rev = _tri_left(tri, dcf_ref[...]) + carry_ref[...]
        carry_ref[...] = jnp.sum(dcf_ref[...], axis=0, keepdims=True) + carry_ref[...]
        lane = lax.broadcasted_iota(jnp.int32, (blk, LANES), 1)
        is_f = (lane >= PAD_F0) & (lane < PAD_F0 + FOX_HEADS)
        df = jnp.where(is_f, rev * _sigmoid(-(p_ref[...] + b_ref[...])), 0.0)
        db_ref[...] += jnp.sum(df, axis=0, keepdims=True)
        o_ref[...] = jnp.where(lane < PAD_DT0 + SSD_HEADS, ddt_ref[...], df).astype(BF16)

    return pl.pallas_call(
        body, name="pad_block_bwd", grid=(nb,),
        in_specs=[pl.BlockSpec((blk, LANES), lambda i: (nb - 1 - i, 0)),
                  pl.BlockSpec((blk, LANES), lambda i: (nb - 1 - i, 0)),
                  pl.BlockSpec((blk, LANES), lambda i: (nb - 1 - i, P_PAD // LANES)),
                  pl.BlockSpec((1, LANES), lambda i: (0, 0))],
        out_specs=[pl.BlockSpec((blk, LANES), lambda i: (nb - 1 - i, 0)),
                   pl.BlockSpec((1, LANES), lambda i: (0, 0))],
        out_shape=[jax.ShapeDtypeStruct((t, LANES), BF16), jax.ShapeDtypeStruct((1, LANES), F32)],
        scratch_shapes=[pltpu.VMEM((1, LANES), F32)],
        compiler_params=_cp("arbitrary"),
    )(dcf, ddt, proj, fbias_row)


def _att_scores(q, k, ck, diagonal, blk):
    s = _dot_nt(q, k) - ck
    if diagonal:
        r = lax.broadcasted_iota(jnp.int32, (blk, blk), 0)
        c = lax.broadcasted_iota(jnp.int32, (blk, blk), 1)
        s = jnp.where(r >= c, s, NEG)
    return s


def _fox_fwd(q, k, v, cf_row):
    h, t, hd = q.shape
    blk = min(ATT_BLOCK, t)
    nb = t // blk
    pair = 2

    def body(q_ref, k_ref, v_ref, ck_ref, o_ref, lse_ref):
        qi = pl.program_id(1)
        qv = [q_ref[hh] for hh in range(pair)]

        def step(j, carry, diagonal):
            off = pl.multiple_of(j * blk, blk)
            out = []
            for hh in range(pair):
                m, l, acc = carry[hh]
                s = _att_scores(qv[hh], k_ref[hh, pl.ds(off, blk), :], ck_ref[hh, :, pl.ds(off, blk)], diagonal, blk)
                m_new = jnp.maximum(m, jnp.max(s, axis=1, keepdims=True))
                alpha = jnp.exp(m - m_new)
                p = jnp.exp(s - m_new)
                l = alpha * l + jnp.sum(p, axis=1, keepdims=True)
                acc = alpha * acc + _dot(p.astype(BF16), v_ref[hh, pl.ds(off, blk), :])
                out.append((m_new, l, acc))
            return tuple(out)

        one = (jnp.full((blk, 1), NEG, F32), jnp.zeros((blk, 1), F32), jnp.zeros((blk, hd), F32))
        carry = lax.fori_loop(0, qi, lambda j, cr: step(j, cr, False), (one,) * pair)
        for hh, (m, l, acc) in enumerate(step(qi, carry, True)):
            o_ref[hh] = acc / l
            lse_ref[hh] = m + jnp.log(l)

    qmap = lambda hp, i: (hp, i, 0)
    whole = lambda hp, i: (hp, 0, 0)
    return pl.pallas_call(
        body, name="fox_fwd", grid=(h // pair, nb),
        in_specs=[pl.BlockSpec((pair, blk, hd), qmap), pl.BlockSpec((pair, t, hd), whole),
                  pl.BlockSpec((pair, t, hd), whole), pl.BlockSpec((pair, 1, t), whole)],
        out_specs=[pl.BlockSpec((pair, blk, hd), qmap), pl.BlockSpec((pair, blk, 1), qmap)],
        out_shape=[jax.ShapeDtypeStruct((h, t, hd), F32), jax.ShapeDtypeStruct((h, t, 1), F32)],
        compiler_params=_cp("parallel", "arbitrary"),
    )(q, k, v, cf_row)


def _fox_bwd(q, k, v, cf_row, o, lse, do):
    h, t, hd = q.shape
    blk = min(ATT_BLOCK, t)
    nb = t // blk

    def body(q_ref, k_ref, v_ref, ck_ref, o_ref, lse_ref, do_ref,
             dq_ref, dk_ref, dv_ref, dcq_ref, dck_ref, delta_s):
        kj = pl.program_id(1)

        @pl.when(kj == 0)
        def _():
            dq_ref[...] = jnp.zeros_like(dq_ref)
            dcq_ref[...] = jnp.zeros_like(dcq_ref)

            def fill(i, _):
                rows = pl.ds(pl.multiple_of(i * blk, blk), blk)
                delta_s[rows, :] = jnp.sum(do_ref[rows, :] * o_ref[rows, :], axis=1, keepdims=True)
                return 0

            lax.fori_loop(0, nb, fill, 0)

        kb, vb, ck = k_ref[...], v_ref[...], ck_ref[...]

        def step(i, carry, diagonal):
            dk, dv, dck = carry
            rows = pl.ds(pl.multiple_of(i * blk, blk), blk)
            qb = q_ref[rows, :]
            do_b = do_ref[rows, :].astype(BF16)
            s = _att_scores(qb, kb, ck, diagonal, blk)
            p = jnp.exp(s - lse_ref[rows, :])
            dv = dv + _dot_tn(p.astype(BF16), do_b)
            ds = p * (_dot_nt(do_b, vb) - delta_s[rows, :])
            ds_b = ds.astype(BF16)
            dk = dk + _dot_tn(ds_b, qb)
            dq_ref[rows, :] += _dot(ds_b, kb)
            dcq_ref[rows, :] += jnp.sum(ds, axis=1, keepdims=True)
            dck = dck - jnp.sum(ds, axis=0, keepdims=True)
            return dk, dv, dck

        init = (jnp.zeros((blk, hd), F32), jnp.zeros((blk, hd), F32), jnp.zeros((1, blk), F32))
        carry = step(kj, init, True)
        dk, dv, dck = lax.fori_loop(kj + 1, nb, lambda i, cr: step(i, cr, False), carry)
        dk_ref[...] = dk
        dv_ref[...] = dv
        dck_ref[...] = dck

    kmap = lambda hh, j: (hh, j, 0)
    whole = lambda hh, j: (hh, 0, 0)
    return pl.pallas_call(
        body, name="fox_bwd", grid=(h, nb),
        in_specs=[pl.BlockSpec((None, t, hd), whole), pl.BlockSpec((None, blk, hd), kmap),
                  pl.BlockSpec((None, blk, hd), kmap), pl.BlockSpec((None, 1, blk), lambda hh, j: (hh, 0, j)),
                  pl.BlockSpec((None, t, hd), whole), pl.BlockSpec((None, t, 1), whole),
                  pl.BlockSpec((None, t, hd), whole)],
        out_specs=[pl.BlockSpec((None, t, hd), whole), pl.BlockSpec((None, blk, hd), kmap),
                   pl.BlockSpec((None, blk, hd), kmap), pl.BlockSpec((None, t, 1), whole),
                   pl.BlockSpec((None, 1, blk), lambda hh, j: (hh, 0, j))],
        out_shape=[jax.ShapeDtypeStruct((h, t, hd), F32), jax.ShapeDtypeStruct((h, t, hd), F32),
                   jax.ShapeDtypeStruct((h, t, hd), F32), jax.ShapeDtypeStruct((h, t, 1), F32),
                   jax.ShapeDtypeStruct((h, 1, t), F32)],
        scratch_shapes=[pltpu.VMEM((t, 1), F32)],
        compiler_params=_cp("parallel", "arbitrary"),
    )(q, k, v, cf_row, o, lse, do)


def _lane_col(v, h):
    lane = lax.broadcasted_iota(jnp.int32, v.shape, 1)
    return jnp.sum(jnp.where(lane == h, v, 0.0), axis=1, keepdims=True)


def _sub_row(v, h):
    sub = lax.broadcasted_iota(jnp.int32, v.shape, 0)
    return jnp.sum(jnp.where(sub == h, v, 0.0), axis=0, keepdims=True)


def _ssd_decays(pad, pad_t, dtb_row, alog_row, dtb_col, alog_col, blk):
    r = lax.broadcasted_iota(jnp.int32, (blk, blk), 0)
    c = lax.broadcasted_iota(jnp.int32, (blk, blk), 1)
    tril = r >= c
    dt_c = _softplus(pad + dtb_row)
    acs_c = _tri_left(tril.astype(BF16), dt_c * (-jnp.exp(alog_row)))
    dt_r = _softplus(pad_t + dtb_col)
    acs_r = _tri_right(dt_r * (-jnp.exp(alog_col)), (r <= c).astype(BF16))
    rows = lax.broadcasted_iota(jnp.int32, acs_c.shape, 0)
    acs_last = jnp.sum(jnp.where(rows == blk - 1, acs_c, 0.0), axis=0, keepdims=True)
    return dt_c, acs_c, acs_r, acs_last, tril


def _pair_terms(pair, dt_c, acs_c, acs_r, acs_last, d_row, blk):
    lane = lax.broadcasted_iota(jnp.int32, (blk, LANES), 1)
    lo = lane < 64
    lo_row = lax.broadcasted_iota(jnp.int32, (1, LANES), 1) < 64
    h0, h1 = 2 * pair, 2 * pair + 1
    col = [_lane_col(acs_c, h0), _lane_col(acs_c, h1)]
    row = [_sub_row(acs_r, h0), _sub_row(acs_r, h1)]
    last = [_lane_col(acs_last, h0), _lane_col(acs_last, h1)]
    dt_p = jnp.where(lo, _lane_col(dt_c, h0), _lane_col(dt_c, h1))
    e_p = jnp.where(lo, jnp.exp(col[0]), jnp.exp(col[1]))
    w_p = jnp.where(lo, jnp.exp(last[0] - col[0]), jnp.exp(last[1] - col[1]))
    decay_p = jnp.where(lo_row, jnp.exp(last[0]), jnp.exp(last[1]))
    d_p = jnp.where(lo_row, _lane_col(d_row, h0), _lane_col(d_row, h1))
    return lo, lo_row, col, row, last, dt_p, e_p, w_p, decay_p, d_p


def _ssd_specs(t, blk, rev):
    nc = t // blk
    ix = (lambda i: nc - 1 - i) if rev else (lambda i: i)
    xbc = pl.BlockSpec((blk, SSD_CONV_DIM), lambda i: (ix(i), 0))
    pad = pl.BlockSpec((blk, LANES), lambda i: (ix(i), P_PAD // LANES))
    pad_t = pl.BlockSpec((LANES, blk), lambda i: (0, ix(i)))
    z = pl.BlockSpec((blk, SSD_W), lambda i: (ix(i), 0))
    row = pl.BlockSpec((1, LANES), lambda i: (0, 0))
    colv = pl.BlockSpec((LANES, 1), lambda i: (0, 0))
    ng = pl.BlockSpec((1, SSD_W), lambda i: (0, 0))
    y = pl.BlockSpec((blk, SSD_W), lambda i: (ix(i), 0))
    st = pl.BlockSpec((None, 4, LANES, LANES), lambda i: (ix(i), 0, 0, 0))
    return nc, xbc, pad, pad_t, z, row, colv, ng, y, st


def _ssd_fwd(xbc, proj, pad_t, dtb_row, alog_row, d_row, dtb_col, alog_col, ng):
    t = xbc.shape[0]
    blk = min(SSD_CHUNK, t)
    nc, s_xbc, s_pad, s_padt, s_z, s_row, s_col, s_ng, s_y, s_st = _ssd_specs(t, blk, False)

    def body(xbc_ref, pad_ref, padt_ref, z_ref, dtb_ref, alog_ref, d_ref, dtbc_ref, alogc_ref, ng_ref,
             out_ref, ypre_ref, st_ref, state):
        @pl.when(pl.program_id(0) == 0)
        def _():
            state[...] = jnp.zeros_like(state)

        dt_c, acs_c, acs_r, acs_last, tril = _ssd_decays(
            pad_ref[...], padt_ref[...], dtb_ref[...], alog_ref[...], dtbc_ref[...], alogc_ref[...], blk)
        ys = []
        g_mat = {}
        for pair in range(4):
            g = pair // 2
            bg = xbc_ref[:, SSD_W + LANES * g:SSD_W + LANES * (g + 1)].astype(BF16)
            cg = xbc_ref[:, SSD_W + 2 * LANES + LANES * g:SSD_W + 2 * LANES + LANES * (g + 1)].astype(BF16)
            if g not in g_mat:
                g_mat[g] = _dot_nt(cg, bg)
            xs_p = xbc_ref[:, LANES * pair:LANES * (pair + 1)]
            lo, _, col, row, _, dt_p, e_p, w_p, decay_p, d_p = _pair_terms(
                pair, dt_c, acs_c, acs_r, acs_last, d_ref[...], blk)
            x_p = xs_p * dt_p
            y = None
            for hh in range(2):
                lm = jnp.exp(jnp.where(tril, col[hh] - row[hh], NEG))
                m_h = (g_mat[g] * lm).astype(BF16)
                x_h = jnp.where(lo if hh == 0 else ~lo, x_p, 0.0).astype(BF16)
                y_h = _dot(m_h, x_h)
                y = y_h if y is None else y + y_h
            st_in = state[pair]
            st_ref[pair] = st_in
            y = y + e_p * _dot(cg, st_in.astype(BF16))
            state[pair] = decay_p * st_in + _dot_tn(bg, (x_p * w_p).astype(BF16))
            ys.append(y + d_p * xs_p)
        y_all = jnp.concatenate(ys, axis=1)
        ypre_ref[...] = y_all
        z = z_ref[...]
        y2 = y_all * (z * _sigmoid(z))
        outs = []
        for g in range(2):
            seg = y2[:, 256 * g:256 * (g + 1)]
            rr = lax.rsqrt(jnp.mean(seg * seg, axis=-1, keepdims=True) + RMS_EPS)
            outs.append(seg * rr * ng_ref[:, 256 * g:256 * (g + 1)])
        out_ref[...] = jnp.concatenate(outs, axis=1).astype(BF16)

    return pl.pallas_call(
        body, name="ssd_scan_fwd", grid=(nc,),
        in_specs=[s_xbc, s_pad, s_padt, s_z, s_row, s_row, s_row, s_col, s_col, s_ng],
        out_specs=[s_y, s_y, s_st],
        out_shape=[jax.ShapeDtypeStruct((t, SSD_W), BF16), jax.ShapeDtypeStruct((t, SSD_W), F32),
                   jax.ShapeDtypeStruct((nc, 4, LANES, LANES), F32)],
        scratch_shapes=[pltpu.VMEM((4, LANES, LANES), F32)],
        compiler_params=_cp("arbitrary"),
    )(xbc, proj, pad_t, proj, dtb_row, alog_row, d_row, dtb_col, alog_col, ng)


def _ssd_bwd(dout, dout_off, xbc, proj, pad_t, ypre, states, dtb_row, alog_row, d_row, dtb_col, alog_col, ng):
    t = xbc.shape[0]
    blk = min(SSD_CHUNK, t)
    nc, s_xbc, s_pad, s_padt, s_z, s_row, s_col, s_ng, s_y, s_st = _ssd_specs(t, blk, True)
    s_dout = pl.BlockSpec((blk, SSD_W), lambda i: (nc - 1 - i, dout_off // SSD_W))

    def body(dout_ref, xbc_ref, pad_ref, padt_ref, z_ref, ypre_ref, st_ref, dtb_ref, alog_ref, d_ref,
             dtbc_ref, alogc_ref, ng_ref, dxbc_ref, ddt_ref, dz_ref, acc_ref, dng_ref, dstate):
        @pl.when(pl.program_id(0) == 0)
        def _():
            dstate[...] = jnp.zeros_like(dstate)
            acc_ref[...] = jnp.zeros_like(acc_ref)
            dng_ref[...] = jnp.zeros_like(dng_ref)

        pad = pad_ref[...]
        dt_c, acs_c, acs_r, acs_last, tril = _ssd_decays(
            pad, padt_ref[...], dtb_ref[...], alog_ref[...], dtbc_ref[...], alogc_ref[...], blk)
        a_row = -jnp.exp(alog_ref[...])

        z = z_ref[...]
        sz = _sigmoid(z)
        silu_z = z * sz
        y_pre = ypre_ref[...]
        y2 = y_pre * silu_z
        dy2 = []
        for g in range(2):
            sl = slice(256 * g, 256 * (g + 1))
            seg = y2[:, sl]
            rr = lax.rsqrt(jnp.mean(seg * seg, axis=-1, keepdims=True) + RMS_EPS)
            nrm = seg * rr
            d_seg = dout_ref[:, sl]
            dng_ref[:, sl] += jnp.sum(d_seg * nrm, axis=0, keepdims=True)
            dn = d_seg * ng_ref[:, sl]
            dy2.append(rr * (dn - nrm * jnp.mean(dn * nrm, axis=-1, keepdims=True)))
        dy2 = jnp.concatenate(dy2, axis=1)
        dz_ref[...] = (dy2 * y_pre * (sz * (1.0 + z * (1.0 - sz)))).astype(BF16)
        dy_all = dy2 * silu_z

        lane_row = lax.broadcasted_iota(jnp.int32, (1, LANES), 1)
        lane_blk = lax.broadcasted_iota(jnp.int32, (blk, LANES), 1)
        row_col = lax.broadcasted_iota(jnp.int32, (blk, 1), 0)
        ddt = jnp.zeros((blk, LANES), F32)
        dacs = jnp.zeros((blk, LANES), F32)
        dd_row = jnp.zeros((1, LANES), F32)
        ones_b = jnp.ones((blk, LANES), BF16)
        dxs = []
        d_b = [None, None]
        d_c = [None, None]
        d_g = [None, None]
        bgs, cgs = {}, {}
        g_mat = {}
        for pair in range(4):
            g = pair // 2
            if g not in g_mat:
                bgs[g] = xbc_ref[:, SSD_W + LANES * g:SSD_W + LANES * (g + 1)].astype(BF16)
                cgs[g] = xbc_ref[:, SSD_W + 2 * LANES + LANES * g:SSD_W + 2 * LANES + LANES * (g + 1)].astype(BF16)
                g_mat[g] = _dot_nt(cgs[g], bgs[g])
            bg, cg = bgs[g], cgs[g]
            xs_p = xbc_ref[:, LANES * pair:LANES * (pair + 1)]
            lo, lo_row, col, row, last, dt_p, e_p, w_p, decay_p, d_p = _pair_terms(
                pair, dt_c, acs_c, acs_r, acs_last, d_ref[...], blk)
            x_p = xs_p * dt_p
            dy_p = dy_all[:, LANES * pair:LANES * (pair + 1)]
            st_in = st_ref[pair]
            dst = dstate[pair]
            dx_diag = None
            for hh in range(2):
                sel = lo if hh == 0 else ~lo
                lm = jnp.exp(jnp.where(tril, col[hh] - row[hh], NEG))
                m_f = g_mat[g] * lm
                m_h = m_f.astype(BF16)
                x_h = jnp.where(sel, x_p, 0.0).astype(BF16)
                dy_h = jnp.where(sel, dy_p, 0.0).astype(BF16)
                dxd = _dot_tn(m_h, dy_h)
                dm = _dot_nt(dy_h, x_h)
                dg_h = dm * lm
                p_b = (dm * m_f).astype(BF16)
                dacs = dacs + jnp.where(lane_blk == 2 * pair + hh, _dot(p_b, ones_b) - _dot_tn(p_b, ones_b), 0.0)
                dx_diag = dxd if dx_diag is None else dx_diag + dxd
                d_g[g] = dg_h if d_g[g] is None else d_g[g] + dg_h
            st_b = st_in.astype(BF16)
            dst_b = dst.astype(BF16)
            y_off = e_p * _dot(cg, st_b)
            edy = (e_p * dy_p).astype(BF16)
            dc_off = _dot_nt(edy, st_b)
            d_c[g] = dc_off if d_c[g] is None else d_c[g] + dc_off
            dstate[pair] = decay_p * dst + _dot_tn(cg, edy)
            dx_state = _dot(bg, dst_b) * w_p
            db_st = _dot_nt((x_p * w_p).astype(BF16), dst_b)
            d_b[g] = db_st if d_b[g] is None else d_b[g] + db_st
            dx = dx_diag + dx_state
            dxs.append(dx * dt_p + d_p * dy_p)
            prod_dt = dx * xs_p
            prod_acs = dy_p * y_off - x_p * dx_state
            prod_st = x_p * dx_state
            prod_d = dy_p * xs_p
            st_prod = jnp.sum(dst * st_in, axis=0, keepdims=True)
            for hh in range(2):
                h = 2 * pair + hh
                sel = lo if hh == 0 else ~lo
                sel_row = lo_row if hh == 0 else ~lo_row
                ddt_h = jnp.sum(jnp.where(sel, prod_dt, 0.0), axis=1, keepdims=True)
                dacs_h = jnp.sum(jnp.where(sel, prod_acs, 0.0), axis=1, keepdims=True)
                tail = jnp.sum(jnp.sum(jnp.where(sel, prod_st, 0.0), axis=1, keepdims=True), axis=0, keepdims=True)
                tail = tail + jnp.exp(last[hh]) * jnp.sum(jnp.where(sel_row, st_prod, 0.0), axis=1, keepdims=True)
                dacs_h = dacs_h + jnp.where(row_col == blk - 1, tail, 0.0)
                dd_h = jnp.sum(jnp.sum(jnp.where(sel, prod_d, 0.0), axis=1, keepdims=True), axis=0, keepdims=True)
                ddt = ddt + jnp.where(lane_blk == h, ddt_h, 0.0)
                dacs = dacs + jnp.where(lane_blk == h, dacs_h, 0.0)
                dd_row = dd_row + jnp.where(lane_row == h, dd_h, 0.0)
        for g in range(2):
            dg_b = d_g[g].astype(BF16)
            d_c[g] = d_c[g] + _dot(dg_b, bgs[g])
            d_b[g] = d_b[g] + _dot_tn(dg_b, cgs[g])
        r = lax.broadcasted_iota(jnp.int32, (blk, blk), 0)
        c = lax.broadcasted_iota(jnp.int32, (blk, blk), 1)
        da = _tri_left((r <= c).astype(BF16), dacs)
        ddt = ddt + da * a_row
        d_raw = ddt * _sigmoid(pad + dtb_ref[...])
        ddt_ref[...] = d_raw
        acc_ref[0:1, :] += jnp.sum(da * dt_c, axis=0, keepdims=True) * a_row
        acc_ref[1:2, :] += dd_row
        acc_ref[2:3, :] += jnp.sum(d_raw, axis=0, keepdims=True)
        dxbc_ref[...] = jnp.concatenate(dxs + d_b + d_c, axis=1)

    return pl.pallas_call(
        body, name="ssd_scan_bwd", grid=(nc,),
        in_specs=[s_dout, s_xbc, s_pad, s_padt, s_z, s_y, s_st, s_row, s_row, s_row, s_col, s_col, s_ng],
        out_specs=[s_xbc, pl.BlockSpec((blk, LANES), lambda i: (nc - 1 - i, 0)), s_y,
                   pl.BlockSpec((8, LANES), lambda i: (0, 0)), s_ng],
        out_shape=[jax.ShapeDtypeStruct((t, SSD_CONV_DIM), F32), jax.ShapeDtypeStruct((t, LANES), F32),
                   jax.ShapeDtypeStruct((t, SSD_W), BF16), jax.ShapeDtypeStruct((8, LANES), F32),
                   jax.ShapeDtypeStruct((1, SSD_W), F32)],
        scratch_shapes=[pltpu.VMEM((4, LANES, LANES), F32)],
        compiler_params=_cp("arbitrary"),
    )(dout, xbc, proj, pad_t, proj, ypre, states, dtb_row, alog_row, d_row, dtb_col, alog_col, ng)


def _pad_lanes(v, off):
    return jnp.zeros((1, LANES), F32).at[0, off:off + v.shape[0]].set(v)


def _perm_mix_w_in(w):
    z, xbc, dt = w[:, 0:512], w[:, 512:1536], w[:, 1536:1544]
    qkv, f, sc = w[:, 1544:2312], w[:, 2312:2316], w[:, 2316:3084]
    padblk = jnp.zeros((w.shape[0], LANES), w.dtype).at[:, PAD_DT0:PAD_DT0 + 8].set(dt).at[:, PAD_F0:PAD_F0 + 4].set(f)
    return jnp.concatenate([z, xbc, qkv, sc, padblk], axis=1)


def _unperm_mix_w_in(wp):
    z, xbc, qkv, sc = wp[:, 0:512], wp[:, 512:1536], wp[:, 1536:2304], wp[:, 2304:3072]
    dt, f = wp[:, P_PAD + PAD_DT0:P_PAD + PAD_DT0 + 8], wp[:, P_PAD + PAD_F0:P_PAD + PAD_F0 + 4]
    return jnp.concatenate([z, xbc, dt, qkv, f, sc], axis=1)


def _heads(m):
    return jnp.transpose(m.reshape(m.shape[0], FOX_HEADS, FOX_HD), (1, 0, 2))


def _unheads(m):
    return jnp.transpose(m, (1, 0, 2)).reshape(m.shape[1], FOX_W)


def _ffn_fwd(h, w_in, w_out, tag):
    u, a = _ffn_in_swiglu(h, w_in, f"ffn_in_{tag}")
    y = _matmul(a, w_out, "nn", F32, f"ffn_out_{tag}")
    return y, (h, u, a)


def _ffn_bwd(dy, saved, w_in, w_out, tag):
    h, u, a = saved
    du = _ffn_out_dx_swiglu(dy, w_out, u, f"ffn_out_dx_{tag}")
    dw_out = _matmul(a, dy, "tn", BF16, f"ffn_out_dw_{tag}")
    dh = _matmul(du, w_in, "nt", F32, f"ffn_in_dx_{tag}", halves="a")
    dw_in = _matmul(h, du, "tn", BF16, f"ffn_in_dw_{tag}", halves="b")
    return dh, dw_in, dw_out


def _mix_fwd(h, wp, w_out, sp, tag):
    proj = _matmul(h, wp, "nn", F32, f"mix_in_{tag}")
    pad_t = jnp.transpose(proj[:, P_PAD:P_PAD + LANES])
    xbc = _ssd_conv_fwd(proj, sp["conv_w"], sp["conv_b"])
    y_ssd, ypre, states = _ssd_fwd(xbc, proj, pad_t, sp["dtb_row"], sp["alog_row"], sp["d_row"],
                                   sp["dtb_col"], sp["alog_col"], sp["ng"])
    cf = _cum_logf(proj, sp["fbias_row"])
    cf_row = jnp.transpose(cf[:, PAD_F0:PAD_F0 + FOX_HEADS])[:, None, :]
    q = _heads((proj[:, P_QKV:P_QKV + 256] * FOX_SCALE).astype(BF16))
    k = _heads(proj[:, P_QKV + 256:P_QKV + 512].astype(BF16))
    v = _heads(proj[:, P_QKV + 512:P_QKV + 768].astype(BF16))
    o, lse = _fox_fwd(q, k, v, cf_row)
    y_sc = _shortconv_fwd(proj, sp["sconv_w"])
    ymix = jnp.concatenate([y_ssd, _unheads(o).astype(BF16), y_sc], axis=1)
    y = _matmul(ymix, w_out, "nn", F32, f"mix_out_{tag}")
    return y, (h, proj, pad_t, xbc, ypre, states, q, k, v, cf_row, o, lse, ymix)


def _mix_bwd(dy, saved, wp, w_out, sp, tag):
    h, proj, pad_t, xbc, ypre, states, q, k, v, cf_row, o, lse, ymix = saved
    dymix = _matmul(dy, w_out, "nt", F32, f"mix_out_dx_{tag}")
    dw_out = _matmul(ymix, dy, "tn", BF16, f"mix_out_dw_{tag}")
    dxbc, ddt, dz, ssd_acc, dng = _ssd_bwd(dymix, 0, xbc, proj, pad_t, ypre, states, sp["dtb_row"],
                                           sp["alog_row"], sp["d_row"], sp["dtb_col"], sp["alog_col"], sp["ng"])
    dxbc_raw, dconv_w, dconv_b = _ssd_conv_bwd(dxbc, proj, sp["conv_w"], sp["conv_b"])
    do = _heads(dymix[:, SSD_W:SSD_W + FOX_W])
    dq, dk, dv, dcq, dck = _fox_bwd(q, k, v, cf_row, o, lse, do)
    dq = dq * FOX_SCALE
    dcf4 = dcq[:, :, 0] + dck[:, 0, :]
    dcf = jnp.zeros((h.shape[0], LANES), F32).at[:, PAD_F0:PAD_F0 + FOX_HEADS].set(jnp.transpose(dcf4))
    dpad, dfb = _pad_block_bwd(dcf, ddt, proj, sp["fbias_row"])
    dscb, dscc, dscx, dsconv_w = _shortconv_bwd(dymix, SSD_W + FOX_W, proj, sp["sconv_w"])
    dproj = jnp.concatenate([dz, dxbc_raw.astype(BF16), _unheads(dq).astype(BF16), _unheads(dk).astype(BF16),
                             _unheads(dv).astype(BF16), dscb, dscc, dscx, dpad], axis=1)
    dh = _matmul(dproj, wp, "nt", F32, f"mix_in_dx_{tag}")
    dwp = _matmul(h, dproj, "tn", BF16, f"mix_in_dw_{tag}")
    small = dict(conv_w=dconv_w, conv_b=dconv_b[0], dt_bias=ssd_acc[2, 0:8], a_log=ssd_acc[0, 0:8],
                 d=ssd_acc[1, 0:8], norm_g=dng[0], f_bias=dfb[0, PAD_F0:PAD_F0 + FOX_HEADS], sconv_w=dsconv_w)
    return dh, _unperm_mix_w_in(dwp), dw_out, small


def _local_step(x, tgt, mod, wfull, small_p, before_sub_backward=None, after_sub_backward=None):
    row = lambda v: v.reshape(1, -1)
    subs = [(l, j) for l in range(DEPTH) for j in range(N_SUB)]
    factor = (0.5, 1.0, 0.5)
    w_names = (("ffn1_w_in", "ffn1_w_out"), ("mix_w_in", "mix_w_out"), ("ffn2_w_in", "ffn2_w_out"))
    lg = [[row(small_p["ln_g"][l, j]) for j in range(N_SUB)] for l in range(DEPTH)]
    lb = [[row(small_p["ln_b"][l, j]) for j in range(N_SUB)] for l in range(DEPTH)]
    sps = []
    for l in range(DEPTH):
        sp = dict(
            conv_w=small_p["ssd_conv_w"][l], conv_b=row(small_p["ssd_conv_b"][l]),
            dtb_row=_pad_lanes(small_p["ssd_dt_bias"][l], PAD_DT0), alog_row=_pad_lanes(small_p["ssd_a_log"][l], 0),
            d_row=_pad_lanes(small_p["ssd_d"][l], 0), ng=row(small_p["ssd_norm_g"][l]),
            fbias_row=_pad_lanes(small_p["fox_f_bias"][l], PAD_F0), sconv_w=small_p["sconv_w"][l])
        sp["dtb_col"] = jnp.transpose(sp["dtb_row"])
        sp["alog_col"] = jnp.transpose(sp["alog_row"])
        sps.append(sp)
    tags = [f"l{l}{('f1', 'mx', 'f2')[j]}" for l, j in subs]

    x0, h = _ln_in_fwd(x, row(small_p["ln_in_g"]), row(small_p["ln_in_b"]), mod[0, 0])
    cur = x0
    xins, ys, inner, weights = [], [], [], []
    wfull = list(wfull)
    for idx, (l, j) in enumerate(subs):
        if idx == 1 and callable(wfull[DEPTH - 1]):
            wfull[DEPTH - 1] = wfull[DEPTH - 1](cur)
        w_in, w_out = wfull[l][w_names[j][0]], wfull[l][w_names[j][1]]
        if idx > 0:
            (w_in, w_out), (cur, h) = lax.optimization_barrier(((w_in, w_out), (cur, h)))
        w_out = w_out.reshape(-1, w_out.shape[-1])
        if j == 1:
            w_in = _perm_mix_w_in(jnp.concatenate([w_in[s] for s in range(4)], axis=1))
        weights.append((w_in, w_out))
        if j == 1:
            y, sv = _mix_fwd(h, w_in, w_out, sps[l], tags[idx])
        else:
            y, sv = _ffn_fwd(h, w_in, w_out, tags[idx])
        nxt = mod[subs[idx + 1]] if idx + 1 < len(subs) else None
        xins.append(cur)
        ys.append(y)
        inner.append(sv)
        cur, h = _res_ln(cur, y, mod[l, j], lg[l][j], lb[l][j], factor[j], f"res_ln_{tags[idx]}", nxt)

    dcur, loss_acc = _loss_head(cur, tgt)
    last = len(subs) - 1
    l, j = subs[last]
    dres, dy, acc = _res_ln_bwd(dcur, xins[last], ys[last], mod[l, j], lg[l][j], factor[j], f"res_ln_bwd_{tags[last]}")
    ln_acc = {last: acc}
    shift_scale = {}
    big_grads = [dict() for _ in range(DEPTH)]
    small_g = [None] * DEPTH
    for idx in reversed(range(len(subs))):
        l, j = subs[idx]
        w_in, w_out = weights[idx]
        if before_sub_backward is not None:
            before_sub_backward(l, j, dy)
        if j == 1:
            dh, g_in, g_out, small_g[l] = _mix_bwd(dy, inner[idx], w_in, w_out, sps[l], tags[idx])
        else:
            dh, g_in, g_out = _ffn_bwd(dy, inner[idx], w_in, w_out, tags[idx])
        big_grads[l][w_names[j][0]], big_grads[l][w_names[j][1]] = g_in, g_out
        if after_sub_backward is not None:
            after_sub_backward(l, j, big_grads[l], dh)
        if idx > 0:
            pl_, pj = subs[idx - 1]
            dres, dy, acc5 = _mod_res_bwd(dres, dh, mod[l, j], xins[idx - 1], ys[idx - 1], mod[pl_, pj], lg[pl_][pj],
                                          lb[pl_][pj], factor[pj], f"mod_res_bwd_{tags[idx]}")
            shift_scale[idx], ln_acc[idx - 1] = acc5[0:2], acc5[2:5]
        else:
            dx0, shift_scale[0] = _modulate_bwd(dres, dh, x0, mod[0, 0], "mod_bwd_first")
    dx, acc_in = _ln_in_bwd(dx0, x, row(small_p["ln_in_g"]))
    dmod = []
    for l in range(DEPTH):
        ids = [N_SUB * l + j for j in range(N_SUB)]
        small_g[l]["ln_g"] = jnp.stack([ln_acc[i][0] for i in ids])
        small_g[l]["ln_b"] = jnp.stack([ln_acc[i][1] for i in ids])
        dmod.append(jnp.stack([jnp.concatenate([shift_scale[i], ln_acc[i][2:3]], axis=0) for i in ids]))
    return loss_acc[0, 0], dx, big_grads, small_g, jnp.stack(dmod), acc_in


SMALL_ORDER = ("ssd_conv_w", "ssd_conv_b", "ssd_dt_bias", "ssd_a_log", "ssd_d", "ssd_norm_g", "fox_f_bias",
               "sconv_w", "ln_g", "ln_b")
SMALL_KEY = dict(ssd_conv_w="conv_w", ssd_conv_b="conv_b", ssd_dt_bias="dt_bias", ssd_a_log="a_log", ssd_d="d",
                 ssd_norm_g="norm_g", fox_f_bias="f_bias", sconv_w="sconv_w", ln_g="ln_g", ln_b="ln_b")
COL_SHARDED_SMALL = ("ssd_conv_w", "sconv_w", "ln_g", "ln_b")


def _pad_to(v, n):
    return jnp.concatenate([v, jnp.zeros((n - v.shape[0],), v.dtype)])


def kernel(x, c, ln_in_g, ln_in_b, ada_w, ada_b, ffn1_w_in, ffn1_w_out, mix_w_in, mix_w_out, ssd_conv_w, ssd_conv_b, ssd_dt_bias, ssd_a_log, ssd_d, ssd_norm_g, fox_f_bias, sconv_w, ffn2_w_in, ffn2_w_out, ln_g, ln_b, loss_target, m_ln_in_g, m_ln_in_b, m_ada_w, m_ada_b, m_ffn1_w_in, m_ffn1_w_out, m_mix_w_in, m_mix_w_out, m_ssd_conv_w, m_ssd_conv_b, m_ssd_dt_bias, m_ssd_a_log, m_ssd_d, m_ssd_norm_g, m_fox_f_bias, m_sconv_w, m_ffn2_w_in, m_ffn2_w_out, m_ln_g, m_ln_b, v_ln_in_g, v_ln_in_b, v_ada_w, v_ada_b, v_ffn1_w_in, v_ffn1_w_out, v_mix_w_in, v_mix_w_out, v_ssd_conv_w, v_ssd_conv_b, v_ssd_dt_bias, v_ssd_a_log, v_ssd_d, v_ssd_norm_g, v_fox_f_bias, v_sconv_w, v_ffn2_w_in, v_ffn2_w_out, v_ln_g, v_ln_b):
    names = ("ln_in_g", "ln_in_b", "ada_w", "ada_b", "ffn1_w_in", "ffn1_w_out", "mix_w_in", "mix_w_out",
             "ssd_conv_w", "ssd_conv_b", "ssd_dt_bias", "ssd_a_log", "ssd_d", "ssd_norm_g", "fox_f_bias", "sconv_w",
             "ffn2_w_in", "ffn2_w_out", "ln_g", "ln_b")
    w_loc = dict(zip(names, (ln_in_g, ln_in_b, ada_w, ada_b, ffn1_w_in, ffn1_w_out, mix_w_in, mix_w_out, ssd_conv_w,
                             ssd_conv_b, ssd_dt_bias, ssd_a_log, ssd_d, ssd_norm_g, fox_f_bias, sconv_w, ffn2_w_in,
                             ffn2_w_out, ln_g, ln_b)))
    m_loc = dict(zip(names, (m_ln_in_g, m_ln_in_b, m_ada_w, m_ada_b, m_ffn1_w_in, m_ffn1_w_out, m_mix_w_in,
                             m_mix_w_out, m_ssd_conv_w, m_ssd_conv_b, m_ssd_dt_bias, m_ssd_a_log, m_ssd_d,
                             m_ssd_norm_g, m_fox_f_bias, m_sconv_w, m_ffn2_w_in, m_ffn2_w_out, m_ln_g, m_ln_b)))
    v_loc = dict(zip(names, (v_ln_in_g, v_ln_in_b, v_ada_w, v_ada_b, v_ffn1_w_in, v_ffn1_w_out, v_mix_w_in,
                             v_mix_w_out, v_ssd_conv_w, v_ssd_conv_b, v_ssd_dt_bias, v_ssd_a_log, v_ssd_d,
                             v_ssd_norm_g, v_fox_f_bias, v_sconv_w, v_ffn2_w_in, v_ffn2_w_out, v_ln_g, v_ln_b)))

    xi, yi, ci = lax.axis_index("x"), lax.axis_index("y"), lax.axis_index("c")
    me = 2 * xi + yi
    dev = 2 * me + ci

    def gather8(v, tag):
        v2 = v.reshape(1, -1)
        got = _bcast_chips([v2], f"gather_chips_{tag}")
        same_c = jnp.concatenate(_by_chip(me, v2, [g[0] for g in got]), axis=0)
        other_c = _swap_sibling([same_c], f"gather_sibling_{tag}")[0]
        pair = lax.switch(ci, [lambda a, b: jnp.stack([a, b], axis=1), lambda a, b: jnp.stack([b, a], axis=1)],
                          same_c, other_c)
        return pair.reshape(8, -1)

    def chip_concat(own, got3, axis):
        return jnp.concatenate(_by_chip(me, own, got3), axis=axis)

    small_cols = [w_loc[n].reshape(-1, w_loc[n].shape[-1]) for n in COL_SHARDED_SMALL]
    got = _bcast_chips(small_cols, "gather_small_params")
    small_p = {n: w_loc[n] for n in ("ln_in_g", "ln_in_b", "ssd_conv_b", "ssd_dt_bias", "ssd_a_log", "ssd_d",
                                     "ssd_norm_g", "fox_f_bias")}
    for i, n in enumerate(COL_SHARDED_SMALL):
        full = chip_concat(small_cols[i], [g[i] for g in got], 1)
        small_p[n] = full.reshape(w_loc[n].shape[:-1] + (full.shape[-1],))

    assert DEPTH == 2
    first, rest = BIG[:2], BIG[2:]
    shard_of = {n: w_loc[n].astype(BF16) for n in BIG}

    def gather(names, layer, after, tag, sequencer_id):
        out = _gather_weights(names, [shard_of[n] for n in names], layer, after, f"gather_weights_{tag}", sequencer_id)
        return dict(zip(names, out))

    g0a = gather(first, 0, [], "l0a", None)
    g0b = gather(rest, 0, [g0a[first[1]]], "l0b", 1)
    wfull = [{**g0a, **g0b}, lambda marker: gather(BIG, 1, [marker], "l1", 5)]

    c_all = gather8(c[0], "c")
    c_act = _silu_bf16(c_all)
    ada_w_b = ada_w.astype(BF16)
    mod_loc = [_matmul(c_act, ada_w_b[l], "nn", F32, f"ada_fwd_l{l}") for l in range(DEPTH)]
    mod_loc = jnp.stack(mod_loc)
    got = _bcast_chips([mod_loc], "gather_mod")
    mod_all = chip_concat(mod_loc, [g[0] for g in got], 2)
    mod = lax.dynamic_index_in_dim(mod_all, dev, 1, keepdims=False) + ada_b
    mod = mod.reshape(DEPTH, N_SUB, 3, D)

    ci_arr = jnp.reshape(ci, (1,)).astype(jnp.int32)
    me_arr = jnp.reshape(me, (1,)).astype(jnp.int32)

    def wire_form(n, g):
        if n in BY_COLUMNS:
            return g
        if n == "mix_w_in":
            return jnp.transpose(g.reshape(g.shape[0], 4, g.shape[1] // 4), (1, 0, 2))
        return g.reshape(4, g.shape[0] // 4, g.shape[1])

    units, reduced = {}, {}

    def start_unit(tag, l, names, g, ids):
        wire = [wire_form(n, g[n]) for n in names]
        units[tag] = dict(l=l, names=names, wire=wire, ids=ids,
                          got=_grad_swap_halves(names, wire, f"grad_swap_halves_{tag}", ids[0]))

    def add_and_scatter(tag, after):
        st = units[tag]
        st["part"] = [_grad_add_halves(st["wire"][i], st["got"][i], ci_arr, after, f"grad_add_halves_{tag}_{n}")
                      for i, n in enumerate(st["names"])]
        st["from_chips"] = _grad_scatter_halves(st["names"], st["part"], f"grad_scatter_halves_{tag}", st["ids"][1])

    def finish_unit(tag, after):
        st = units[tag]
        halves = [_grad_add_slots(st["part"][i], [g[i] for g in st["from_chips"]], n, me_arr, after,
                                  f"grad_add_slots_{tag}_{n}") for i, n in enumerate(st["names"])]
        other = _grad_swap_reduced(halves, f"grad_swap_reduced_{tag}", st["ids"][2])
        for i, n in enumerate(st["names"]):
            reduced[(st["l"], n)] = (halves[i], other[i])

    second, third = BIG[2:4], BIG[4:]

    def after_sub_backward(l, j, g, marker):
        if (l, j) == (1, 0):
            start_unit("l1", 1, BIG, g, (2, 3, 4))
        elif (l, j) == (0, 2):
            start_unit("l0f2", 0, third, g, (6, 7, 8))
        elif (l, j) == (0, 1):
            start_unit("l0mx", 0, second, g, (9, 10, 11))

    def before_sub_backward(l, j, marker):
        if (l, j) == (0, 2):
            add_and_scatter("l1", marker)
        elif (l, j) == (0, 1):
            add_and_scatter("l0f2", marker)
        elif (l, j) == (0, 0):
            add_and_scatter("l0mx", marker)

    loss_part, dx, big_g, small_g, dmod, acc_in = _local_step(
        x[0], loss_target[0], mod, wfull, small_p, before_sub_backward, after_sub_backward)
    loss = lax.psum(loss_part, ("x", "y", "c"))

    pieces = [dmod.reshape(-1), acc_in[0], acc_in[1]]
    for n in SMALL_ORDER:
        pieces.append(jnp.stack([small_g[l][SMALL_KEY[n]] for l in range(DEPTH)]).reshape(-1))
    sizes = [p.shape[0] for p in pieces]
    total = sum(sizes)
    padded = -(-total // 1024) * 1024
    vec = _pad_to(jnp.concatenate(pieces), padded)
    all_rows = gather8(vec, "small_grads")
    summed = _sum_rows(all_rows, "sum_small_grads")[0]
    offs = [0]
    for s in sizes:
        offs.append(offs[-1] + s)
    n_mod = sizes[0]
    grads = {"ada_b": summed[0:n_mod].reshape(DEPTH, 3 * N_SUB * D),
             "ln_in_g": summed[offs[1]:offs[2]], "ln_in_b": summed[offs[2]:offs[3]]}
    for i, n in enumerate(SMALL_ORDER):
        full = summed[offs[3 + i]:offs[4 + i]].reshape(small_p[n].shape)
        if n in COL_SHARDED_SMALL:
            wcols = w_loc[n].shape[-1]
            full = lax.dynamic_slice_in_dim(full, me * wcols, wcols, axis=full.ndim - 1)
        grads[n] = full

    dmod_all = all_rows[:, 0:n_mod].reshape(8, DEPTH, 3 * N_SUB * D)
    ncol = ada_w.shape[-1]
    dmod_cols = lax.dynamic_slice_in_dim(dmod_all, me * ncol, ncol, axis=2).astype(BF16)
    grads["ada_w"] = jnp.stack([_matmul(c_act, dmod_cols[:, l], "tn", F32, f"ada_bwd_l{l}") for l in range(DEPTH)])

    finish_unit("l1", dx)
    finish_unit("l0f2", dx)
    finish_unit("l0mx", dx)
    start_unit("l0f1", 0, first, big_g[0], (None, None, None))
    add_and_scatter("l0f1", dx)
    finish_unit("l0f1", dx)

    delta, new_m, new_v = {}, {}, {}
    delta["ada_w"], new_m["ada_w"], new_v["ada_w"] = _adamw(w_loc["ada_w"], grads["ada_w"], m_loc["ada_w"],
                                                           v_loc["ada_w"], "adamw_ada_w")
    for i, n in enumerate(BIG):
        grads[n], delta[n], new_m[n], new_v[n] = _adamw_layers(
            w_loc[n], [reduced[(l, n)][0] for l in range(DEPTH)], [reduced[(l, n)][1] for l in range(DEPTH)],
            m_loc[n], v_loc[n], ci_arr, f"adamw_{n}")
    small_names = [n for n in names if n not in ("ada_w",) + BIG]
    flat = lambda d: jnp.concatenate([d[n].reshape(-1) for n in small_names])
    n_small = sum(w_loc[n].size for n in small_names)
    n_pad = -(-n_small // 1024) * 1024
    packed = [_pad_to(flat(d), n_pad).reshape(-1, LANES) for d in (w_loc, grads, m_loc, v_loc)]
    d_s, m_s, v_s = _adamw(*packed, "adamw_small")
    off = 0
    for n in small_names:
        sz = w_loc[n].size
        delta[n] = d_s.reshape(-1)[off:off + sz].reshape(w_loc[n].shape)
        new_m[n] = m_s.reshape(-1)[off:off + sz].reshape(w_loc[n].shape)
        new_v[n] = v_s.reshape(-1)[off:off + sz].reshape(w_loc[n].shape)
        off += sz

    return (loss, dx[None], *[grads[n] for n in names], *[delta[n] for n in names],
            *[new_m[n] for n in names], *[new_v[n] for n in names])
```

```python
import functools

import jax
import jax.numpy as jnp
from jax import lax
from jax.experimental import pallas as pl
from jax.experimental.pallas import tpu as pltpu
from jax.experimental.pallas import tpu_sc as plsc

F32 = jnp.float32
BF16 = jnp.bfloat16
MESH = pl.DeviceIdType.MESH

D = 1024
DEPTH = 2
N_SUB = 3
D_FF = 2816
FF_TILE = D_FF // 2
ALPHA = (2 * DEPTH) ** 0.25
LN_EPS = 1e-5
RMS_EPS = 1e-5
SSD_W = 512
SSD_HEADS = 8
SSD_CONV_K = 4
SSD_CONV_DIM = 1024
FOX_W = 256
FOX_HEADS = 4
FOX_HD = 64
FOX_SCALE = FOX_HD ** -0.5
SC_W = 256
SC_K = 3
D_IN_PROJ = 3084
P_Z, P_XBC, P_QKV, P_SC, P_PAD = 0, 512, 1536, 2304, 3072
D_PROJ_PAD = 3200
PAD_DT0, PAD_F0 = 0, 8
SSD_CHUNK = 256
ATT_BLOCK = 512
CUM_BLOCK = 256
LANES = 128
VMEM_LIMIT = 56 * 1024 * 1024

ADAM_LR, ADAM_B1, ADAM_B2, ADAM_EPS, ADAM_WD, ADAM_STEP = 0.001, 0.9, 0.999, 1e-08, 0.01, 10
NEG = -1e30


def _cp(*sem):
    return pltpu.CompilerParams(dimension_semantics=sem, vmem_limit_bytes=VMEM_LIMIT)


def _pick(n, cands):
    for c in cands:
        if n % c == 0:
            return c
    return n


def _dot(a, b):
    return lax.dot_general(a, b, (((1,), (0,)), ((), ())), preferred_element_type=F32)


def _dot_nt(a, b):
    return lax.dot_general(a, b, (((1,), (1,)), ((), ())), preferred_element_type=F32)


def _dot_tn(a, b):
    return lax.dot_general(a, b, (((0,), (0,)), ((), ())), preferred_element_type=F32)


def _sigmoid(x):
    return 0.5 * jnp.tanh(0.5 * x) + 0.5


def _softplus(x):
    return jnp.maximum(x, 0.0) + jnp.log(1.0 + jnp.exp(-jnp.abs(x)))


def _split3(v):
    h = v.astype(BF16)
    r = v - h.astype(F32)
    m = r.astype(BF16)
    l = (r - m.astype(F32)).astype(BF16)
    return h, m, l


def _tri_left(tri, v):
    h, m, l = _split3(v)
    return _dot(tri, h) + _dot(tri, m) + _dot(tri, l)


def _tri_right(v, tri):
    h, m, l = _split3(v)
    return _dot(h, tri) + _dot(m, tri) + _dot(l, tri)


def _matmul(a, b, mode, out_dtype, name, halves=None):
    assert a.dtype == BF16 and b.dtype == BF16, (name, a.dtype, b.dtype)
    if halves == "a":
        assert mode == "nt" and a.shape[0] == 2 and a.shape[2] == D_FF
        m, k, n = a.shape[1], 2 * D_FF, b.shape[0]
    elif halves == "b":
        assert mode == "tn" and b.shape[0] == 2 and b.shape[2] == D_FF
        (k, m), n = a.shape, 2 * D_FF
    elif mode == "nn":
        (m, k), n = a.shape, b.shape[1]
    elif mode == "nt":
        (m, k), n = a.shape, b.shape[0]
    else:
        (k, m), n = a.shape, b.shape[1]
    tm = m if m <= 1024 else _pick(m, (1024, 1408, 512, 256, 128))
    tn = n if n <= 1024 else _pick(n, (1408, 640, 512, 256, 128))
    tk = k if k <= 1024 else _pick(k, (1408, 1024, 640, 512, 256, 128))
    nk = k // tk
    if mode == "nn":
        dn = (((1,), (0,)), ((), ()))
        a_spec = pl.BlockSpec((tm, tk), lambda i, j, kk: (i, kk))
        b_spec = pl.BlockSpec((tk, tn), lambda i, j, kk: (kk, j))
    elif mode == "nt":
        dn = (((1,), (1,)), ((), ()))
        a_spec = pl.BlockSpec((tm, tk), lambda i, j, kk: (i, kk))
        b_spec = pl.BlockSpec((tn, tk), lambda i, j, kk: (j, kk))
    else:
        dn = (((0,), (0,)), ((), ()))
        a_spec = pl.BlockSpec((tk, tm), lambda i, j, kk: (kk, i))
        b_spec = pl.BlockSpec((tk, tn), lambda i, j, kk: (kk, j))
    per_half = D_FF // FF_TILE
    if halves == "a":
        assert tk == FF_TILE
        a_spec = pl.BlockSpec((None, tm, tk), lambda i, j, kk: (kk // per_half, i, kk % per_half))
    elif halves == "b":
        assert tn == FF_TILE
        b_spec = pl.BlockSpec((None, tk, tn), lambda i, j, kk: (j // per_half, kk, j % per_half))

    def body(a_ref, b_ref, o_ref, *acc):
        d = lax.dot_general(a_ref[...], b_ref[...], dn, preferred_element_type=F32)
        if nk == 1:
            o_ref[...] = d.astype(o_ref.dtype)
            return
        acc_ref, = acc
        kk = pl.program_id(2)

        @pl.when(kk == 0)
        def _():
            acc_ref[...] = d

        @pl.when((kk > 0) & (kk < nk - 1))
        def _():
            acc_ref[...] += d

        @pl.when(kk == nk - 1)
        def _():
            o_ref[...] = (acc_ref[...] + d).astype(o_ref.dtype)

    return pl.pallas_call(
        body, name=name, grid=(m // tm, n // tn, nk),
        in_specs=[a_spec, b_spec],
        out_specs=pl.BlockSpec((tm, tn), lambda i, j, kk: (i, j)),
        out_shape=jax.ShapeDtypeStruct((m, n), out_dtype),
        scratch_shapes=[pltpu.VMEM((tm, tn), F32)] if nk > 1 else [],
        compiler_params=_cp("parallel", "parallel", "arbitrary"),
    )(a, b)


def _rows(body, name, t, tb, row_in, full_in, row_out, acc_out):
    in_specs, args = [], []
    for r in row_in:
        if isinstance(r, tuple):
            arr, w, j = r
            in_specs.append(pl.BlockSpec((tb, w), functools.partial(lambda i, jj: (i, jj), jj=j)))
            args.append(arr)
        else:
            in_specs.append(pl.BlockSpec((tb, r.shape[1]), lambda i: (i, 0)))
            args.append(r)
    for f in full_in:
        in_specs.append(pl.BlockSpec(f.shape, functools.partial(lambda i, nd: (0,) * nd, nd=f.ndim)))
        args.append(f)
    out_specs = [pl.BlockSpec((tb, c), lambda i: (i, 0)) for c, _ in row_out]
    out_specs += [pl.BlockSpec(s, functools.partial(lambda i, nd: (0,) * nd, nd=len(s))) for s in acc_out]
    out_shape = [jax.ShapeDtypeStruct((t, c), dt) for c, dt in row_out]
    out_shape += [jax.ShapeDtypeStruct(s, F32) for s in acc_out]
    return pl.pallas_call(
        body, name=name, grid=(t // tb,), in_specs=in_specs, out_specs=out_specs, out_shape=out_shape,
        compiler_params=_cp("arbitrary"),
    )(*args)


def _ln_stats(r):
    mu = jnp.mean(r, axis=-1, keepdims=True)
    xc = r - mu
    var = jnp.mean(xc * xc, axis=-1, keepdims=True)
    rstd = lax.rsqrt(var + LN_EPS)
    return xc * rstd, rstd


def _ln_bwd(dout, xhat, rstd, g):
    dxh = dout * g
    m1 = jnp.mean(dxh, axis=-1, keepdims=True)
    m2 = jnp.mean(dxh * xhat, axis=-1, keepdims=True)
    return rstd * (dxh - m1 - xhat * m2)


def _first(i, acc_refs):
    @pl.when(i == 0)
    def _():
        for a in acc_refs:
            a[...] = jnp.zeros_like(a)


def _modulated(xv, m_ref):
    return (xv * (1.0 + m_ref[1:2, :]) + m_ref[0:1, :]).astype(BF16)


def _ln_in_fwd(x, g, b, next_mod3):
    t = x.shape[0]

    def body(x_ref, g_ref, b_ref, m_ref, o_ref, h_ref):
        xhat, _ = _ln_stats(x_ref[...])
        out = xhat * g_ref[...] + b_ref[...]
        o_ref[...] = out
        h_ref[...] = _modulated(out, m_ref)

    return _rows(body, "ln_in_fwd", t, 256, [x], [g, b, next_mod3], [(D, F32), (D, BF16)], [])


def _ln_in_bwd(dx0, x, g):
    t = x.shape[0]

    def body(d_ref, x_ref, g_ref, o_ref, acc_ref):
        _first(pl.program_id(0), [acc_ref])
        xhat, rstd = _ln_stats(x_ref[...])
        d = d_ref[...]
        o_ref[...] = _ln_bwd(d, xhat, rstd, g_ref[...])
        acc_ref[0:1, :] += jnp.sum(d * xhat, axis=0, keepdims=True)
        acc_ref[1:2, :] += jnp.sum(d, axis=0, keepdims=True)

    return _rows(body, "ln_in_bwd", t, 256, [dx0, x], [g], [(D, F32)], [(2, D)])


def _modulate_bwd(dxres, dh, xin, mod3, name):
    t = xin.shape[0]

    def body(r_ref, dh_ref, x_ref, m_ref, o_ref, acc_ref):
        _first(pl.program_id(0), [acc_ref])
        dh_v = dh_ref[...]
        o_ref[...] = r_ref[...] + dh_v * (1.0 + m_ref[1:2, :])
        acc_ref[0:1, :] += jnp.sum(dh_v, axis=0, keepdims=True)
        acc_ref[1:2, :] += jnp.sum(dh_v * x_ref[...], axis=0, keepdims=True)

    return _rows(body, name, t, 256, [dxres, dh, xin], [mod3], [(D, F32)], [(2, D)])


def _ffn_in_swiglu(h, w_in, name):
    t = h.shape[0]
    tm = _pick(t, (512, 256, 128))
    nj = D_FF // FF_TILE

    def body(h_ref, wg_ref, wu_ref, u_ref, a_ref):
        hv = h_ref[...]
        gate = _dot(hv, wg_ref[...])
        up = _dot(hv, wu_ref[...])
        sg = _sigmoid(gate)
        silu = gate * sg
        u_ref[0] = (up * (sg + silu * (1.0 - sg))).astype(BF16)
        u_ref[1] = silu.astype(BF16)
        a_ref[...] = (silu * up).astype(BF16)

    return pl.pallas_call(
        body, name=name, grid=(nj, t // tm),
        in_specs=[pl.BlockSpec((tm, D), lambda j, i: (i, 0)),
                  pl.BlockSpec((D, FF_TILE), lambda j, i: (0, j)),
                  pl.BlockSpec((D, FF_TILE), lambda j, i: (0, nj + j))],
        out_specs=[pl.BlockSpec((2, tm, FF_TILE), lambda j, i: (0, i, j)),
                   pl.BlockSpec((tm, FF_TILE), lambda j, i: (i, j))],
        out_shape=[jax.ShapeDtypeStruct((2, t, D_FF), BF16), jax.ShapeDtypeStruct((t, D_FF), BF16)],
        compiler_params=_cp("parallel", "parallel"),
    )(h, w_in, w_in)


def _ffn_out_dx_swiglu(dy, w_out, u, name):
    t = dy.shape[0]
    tm = _pick(t, (512, 256, 128))
    nj = D_FF // FF_TILE

    def body(dy_ref, w_ref, u_ref, du_ref):
        da = _dot_nt(dy_ref[...], w_ref[...])
        du_ref[0] = (da * u_ref[0].astype(F32)).astype(BF16)
        du_ref[1] = (da * u_ref[1].astype(F32)).astype(BF16)

    blk3 = pl.BlockSpec((2, tm, FF_TILE), lambda j, i: (0, i, j))
    return pl.pallas_call(
        body, name=name, grid=(nj, t // tm),
        in_specs=[pl.BlockSpec((tm, D), lambda j, i: (i, 0)),
                  pl.BlockSpec((FF_TILE, D), lambda j, i: (j, 0)), blk3],
        out_specs=blk3,
        out_shape=jax.ShapeDtypeStruct((2, t, D_FF), BF16),
        compiler_params=_cp("parallel", "parallel"),
    )(dy, w_out, u)


def _res_ln(xin, y, mod3, lg, lb, factor, name, next_mod3=None):
    t = xin.shape[0]

    def body(x_ref, y_ref, m_ref, g_ref, b_ref, *rest):
        r = ALPHA * x_ref[...] + (factor * m_ref[2:3, :]) * y_ref[...]
        xhat, _ = _ln_stats(r)
        out = xhat * g_ref[...] + b_ref[...]
        if next_mod3 is None:
            rest[0][...] = out
        else:
            rest[1][...] = out
            rest[2][...] = _modulated(out, rest[0])

    if next_mod3 is None:
        return _rows(body, name, t, 256, [xin, y], [mod3, lg, lb], [(D, F32)], [])[0], None
    return _rows(body, name, t, 256, [xin, y], [mod3, lg, lb, next_mod3], [(D, F32), (D, BF16)], [])


def _mod_res_bwd(dxres, dh, mod3, xin_p, y_p, mod3_p, lg_p, lb_p, factor_p, name):
    t = dxres.shape[0]

    def body(r_ref, dh_ref, xp_ref, yp_ref, m_ref, mp_ref, g_ref, b_ref, dres_ref, dy_ref, acc_ref):
        _first(pl.program_id(0), [acc_ref])
        gate = factor_p * mp_ref[2:3, :]
        yv = yp_ref[...]
        xhat, rstd = _ln_stats(ALPHA * xp_ref[...] + gate * yv)
        xin = xhat * g_ref[...] + b_ref[...]
        dh_v = dh_ref[...]
        d = r_ref[...] + dh_v * (1.0 + m_ref[1:2, :])
        dr = _ln_bwd(d, xhat, rstd, g_ref[...])
        dres_ref[...] = ALPHA * dr
        dy_ref[...] = (gate * dr).astype(BF16)
        acc_ref[0:1, :] += jnp.sum(dh_v, axis=0, keepdims=True)
        acc_ref[1:2, :] += jnp.sum(dh_v * xin, axis=0, keepdims=True)
        acc_ref[2:3, :] += jnp.sum(d * xhat, axis=0, keepdims=True)
        acc_ref[3:4, :] += jnp.sum(d, axis=0, keepdims=True)
        acc_ref[4:5, :] += jnp.sum(factor_p * yv * dr, axis=0, keepdims=True)

    return _rows(body, name, t, 256, [dxres, dh, xin_p, y_p], [mod3, mod3_p, lg_p, lb_p],
                 [(D, F32), (D, BF16)], [(5, D)])


def _res_ln_bwd(dout, xin, y, mod3, lg, factor, name):
    t = xin.shape[0]

    def body(d_ref, x_ref, y_ref, m_ref, g_ref, dres_ref, dy_ref, acc_ref):
        _first(pl.program_id(0), [acc_ref])
        gate = factor * m_ref[2:3, :]
        yv = y_ref[...]
        r = ALPHA * x_ref[...] + gate * yv
        xhat, rstd = _ln_stats(r)
        d = d_ref[...]
        dr = _ln_bwd(d, xhat, rstd, g_ref[...])
        dres_ref[...] = ALPHA * dr
        dy_ref[...] = (gate * dr).astype(BF16)
        acc_ref[0:1, :] += jnp.sum(d * xhat, axis=0, keepdims=True)
        acc_ref[1:2, :] += jnp.sum(d, axis=0, keepdims=True)
        acc_ref[2:3, :] += jnp.sum(factor * yv * dr, axis=0, keepdims=True)

    return _rows(body, name, t, 256, [dout, xin, y], [mod3, lg], [(D, F32), (D, BF16)], [(3, D)])


def _loss_head(xf, tgt):
    t = xf.shape[0]

    def body(x_ref, t_ref, d_ref, acc_ref):
        _first(pl.program_id(0), [acc_ref])
        e = x_ref[...] - t_ref[...]
        d_ref[...] = e * (1.0 / D)
        part = 0.5 * jnp.sum(jnp.mean(e * e, axis=-1, keepdims=True), axis=0, keepdims=True)
        acc_ref[...] += jnp.broadcast_to(part, acc_ref.shape)

    return _rows(body, "loss_head", t, 256, [xf, tgt], [], [(D, F32)], [(1, LANES)])


def _silu_bf16(c_all):
    def body(c_ref, o_ref):
        v = c_ref[...]
        o_ref[...] = (v * _sigmoid(v)).astype(BF16)

    return _rows(body, "silu_c", c_all.shape[0], c_all.shape[0], [c_all], [], [(c_all.shape[1], BF16)], [])[0]


def _sum_rows(v, name):
    r, n = v.shape
    tn = _pick(n, (8192, 4096, 2048, 1024, 512, 256, 128))

    def body(v_ref, o_ref):
        acc = v_ref[0:1, :]
        for k in range(1, r):
            acc = acc + v_ref[k:k + 1, :]
        o_ref[...] = acc

    return pl.pallas_call(
        body, name=name, grid=(n // tn,),
        in_specs=[pl.BlockSpec((r, tn), lambda j: (0, j))],
        out_specs=pl.BlockSpec((1, tn), lambda j: (0, j)),
        out_shape=jax.ShapeDtypeStruct((1, n), F32),
        compiler_params=_cp("parallel"),
    )(v)


def _elementwise(fn, name, ins, out_dtypes):
    r, c = ins[0].shape
    tb = _pick(r, (128, 64, 32, 16, 8))
    n_in = len(ins)

    def body(*refs):
        outs = fn(*[x[...] for x in refs[:n_in]])
        for o_ref, o in zip(refs[n_in:], outs):
            o_ref[...] = o.astype(o_ref.dtype)

    spec = pl.BlockSpec((tb, c), lambda i: (i, 0))
    return pl.pallas_call(
        body, name=name, grid=(r // tb,), in_specs=[spec] * n_in, out_specs=[spec] * len(out_dtypes),
        out_shape=[jax.ShapeDtypeStruct((r, c), dt) for dt in out_dtypes],
        compiler_params=_cp("parallel"),
    )(*ins)


def _adamw_math(w, g, m, v):
    m = ADAM_B1 * m + (1.0 - ADAM_B1) * g
    v = ADAM_B2 * v + (1.0 - ADAM_B2) * (g * g)
    m_hat = m / (1.0 - ADAM_B1 ** ADAM_STEP)
    v_hat = v / (1.0 - ADAM_B2 ** ADAM_STEP)
    delta = -ADAM_LR * (m_hat / (jnp.sqrt(v_hat) + ADAM_EPS) + ADAM_WD * w)
    return delta, m, v


def _adamw(w, g, m, v, name):
    shape = w.shape
    c = shape[-1]
    w2, g2, m2, v2 = (a.reshape(-1, c) for a in (w, g, m, v))
    outs = _elementwise(_adamw_math, name, [w2, g2, m2, v2], [F32, F32, F32])
    return tuple(o.reshape(shape) for o in outs)


def _remote_exchange(ins, plan, peers_of, name):
    n_in, n_out = len(ins), len(plan)

    def body(*refs):
        in_refs, out_refs = refs[:n_in], refs[n_in:n_in + n_out]
        send_sems, recv_sems = refs[n_in + n_out], refs[n_in + n_out + 1]
        peers = peers_of(lax.axis_index("x"), lax.axis_index("y"), lax.axis_index("c"))
        copies = [
            pltpu.make_async_remote_copy(
                src_ref=in_refs[src], dst_ref=out_refs[k], send_sem=send_sems.at[k], recv_sem=recv_sems.at[k],
                device_id=peers[peer], device_id_type=MESH)
            for k, (peer, src) in enumerate(plan)
        ]
        for cp in copies:
            cp.start()
        for cp in copies:
            cp.wait()

    any_spec = pl.BlockSpec(memory_space=pl.ANY)
    return list(pl.pallas_call(
        body, name=name,
        in_specs=[any_spec] * n_in, out_specs=[any_spec] * n_out,
        out_shape=[jax.ShapeDtypeStruct(ins[src].shape, ins[src].dtype) for _, src in plan],
        scratch_shapes=[pltpu.SemaphoreType.DMA((n_out,)), pltpu.SemaphoreType.DMA((n_out,))],
    )(*ins))


def _sibling(x, y, c):
    return [(x, y, 1 - c)]


def _other_chips(x, y, c):
    return [(1 - x, y, c), (x, 1 - y, c), (1 - x, 1 - y, c)]


def _swap_sibling(arrs, name):
    return _remote_exchange(arrs, [(0, i) for i in range(len(arrs))], _sibling, name)


def _bcast_chips(arrs, name):
    n = len(arrs)
    out = _remote_exchange(arrs, [(p, i) for p in range(3) for i in range(n)], _other_chips, name)
    return [out[p * n:(p + 1) * n] for p in range(3)]


def _by_chip(me, own, got3):
    by_rel = [own, got3[0], got3[1], got3[2]]
    rel_bits = (0, 2, 1, 3)

    def branch(m):
        def f(ops):
            return [ops[rel_bits.index(i ^ m)] for i in range(4)]
        return f

    return lax.switch(me, [branch(m) for m in range(4)], by_rel)


BIG = ("ffn1_w_in", "ffn1_w_out", "mix_w_in", "mix_w_out", "ffn2_w_in", "ffn2_w_out")
BY_COLUMNS = ("ffn1_w_in", "ffn2_w_in")


def _layer_shape(n, shard_shape):
    r, cs = shard_shape
    return (r, 4 * cs) if n in BY_COLUMNS else (4, r, cs)


def _chip_ids(x, y):
    chips = [(1 - x, y), (x, 1 - y), (1 - x, 1 - y)]
    return chips, [2 * cx + cy for cx, cy in chips]


def _half_slot(ref, n, chip, h):
    if n in BY_COLUMNS:
        hr, w = ref.shape[0] // 2, ref.shape[1] // 4
        return ref.at[pl.ds(pl.multiple_of(h * hr, 16), hr), pl.ds(pl.multiple_of(chip * w, LANES), w)]
    hr = ref.shape[1] // 2
    return ref.at[chip, pl.ds(pl.multiple_of(h * hr, 16), hr)]


def _gather_body(names, layer, in_refs, out_refs, sems, handshake):
    send_chip, recv_chip, send_sib, recv_sib, local, send_fwd, recv_fwd = sems
    n_w = len(names)
    x, y, c = lax.axis_index("x"), lax.axis_index("y"), lax.axis_index("c")
    me = 2 * x + y
    chips, chip_idx = _chip_ids(x, y)
    sibling = (x, y, 1 - c)
    if handshake:
        _shake_hands([sibling] + [(*ch, c) for ch in chips])
    remote = _remote

    sends, own_copies = [], []
    for i, n in enumerate(names):
        shard = in_refs[i].at[layer]
        hr = shard.shape[0] // 2
        src = shard.at[pl.ds(pl.multiple_of(c * hr, 16), hr)]
        mine = _half_slot(out_refs[i], n, me, c)
        own = pltpu.make_async_copy(src, mine, local.at[i])
        own.start()
        own_copies.append(own)
        sends.append(remote(src, mine, send_sib.at[i], recv_sib.at[i], sibling))
        sends[-1].start()
        for p in range(3):
            k = p * n_w + i
            sends.append(remote(src, mine, send_chip.at[k], recv_chip.at[k], (*chips[p], c)))
            sends[-1].start()
    for p in range(3):
        for i, n in enumerate(names):
            k = p * n_w + i
            landed = _half_slot(out_refs[i], n, chip_idx[p], c)
            remote(landed, landed, send_chip.at[k], recv_chip.at[k], sibling).wait_recv()
            sends.append(remote(landed, landed, send_fwd.at[k], recv_fwd.at[k], sibling))
            sends[-1].start()
    for i, n in enumerate(names):
        theirs = _half_slot(out_refs[i], n, me, 1 - c)
        remote(theirs, theirs, send_sib.at[i], recv_sib.at[i], sibling).wait_recv()
        for p in range(3):
            k = p * n_w + i
            theirs = _half_slot(out_refs[i], n, chip_idx[p], 1 - c)
            remote(theirs, theirs, send_fwd.at[k], recv_fwd.at[k], sibling).wait_recv()
    for own in own_copies:
        own.wait()
    for cp in sends:
        cp.wait_send()


def _gather_weights(names, shards, layer, after, name, sequencer_id):
    n_w = len(names)

    def body(in_refs, out_refs, sems, handshake):
        _gather_body(names, layer, in_refs, out_refs, sems, handshake)

    dma = pltpu.SemaphoreType.DMA
    sems = [dma((3 * n_w,)), dma((3 * n_w,)), dma((n_w,)), dma((n_w,)), dma((n_w,)), dma((3 * n_w,)), dma((3 * n_w,))]
    shapes = [jax.ShapeDtypeStruct(_layer_shape(n, s.shape[1:]), BF16) for n, s in zip(names, shards)]
    return _comm_call(body, list(shards) + list(after), shapes, sems, name, sequencer_id)


def _comm_call(body_fn, ins, out_shapes, sem_types, name, sequencer_id):
    n_in, n_out = len(ins), len(out_shapes)
    sequencer = sequencer_id is not None

    def body(*refs):
        body_fn(refs[:n_in], refs[n_in:n_in + n_out], refs[n_in + n_out:], sequencer)

    if sequencer:
        return list(pl.kernel(
            body, name=name, out_type=out_shapes,
            mesh=plsc.ScalarSubcoreMesh(axis_name="sequencer", num_cores=1), scratch_types=sem_types,
            compiler_params=pltpu.CompilerParams(collective_id=sequencer_id),
        )(*ins))
    any_spec = pl.BlockSpec(memory_space=pl.ANY)
    return list(pl.pallas_call(
        body, name=name, in_specs=[any_spec] * n_in, out_specs=[any_spec] * n_out, out_shape=out_shapes,
        scratch_shapes=sem_types,
    )(*ins))


def _shake_hands(peers):
    barrier = pltpu.get_barrier_semaphore()
    for peer in peers:
        pl.semaphore_signal(barrier, inc=1, device_id=peer, device_id_type=MESH)
    pl.semaphore_wait(barrier, len(peers))


def _remote(src, dst, s_sem, r_sem, to):
    return pltpu.make_async_remote_copy(src_ref=src, dst_ref=dst, send_sem=s_sem, recv_sem=r_sem,
                                        device_id=to, device_id_type=MESH)


def _rows_half(ref, n, h):
    if n in BY_COLUMNS:
        hr = ref.shape[0] // 2
        return ref.at[pl.ds(pl.multiple_of(h * hr, 16), hr)]
    hr = ref.shape[1] // 2
    return ref.at[:, pl.ds(pl.multiple_of(h * hr, 16), hr)]


def _half_form_shape(n, wire_shape):
    if n in BY_COLUMNS:
        return (wire_shape[0] // 2, wire_shape[1])
    return (wire_shape[0], wire_shape[1] // 2, wire_shape[2])


def _grad_swap_halves(names, wire, name, sequencer_id):
    n_w = len(names)

    def body(in_refs, out_refs, sems, handshake):
        send, recv = sems
        x, y, c = lax.axis_index("x"), lax.axis_index("y"), lax.axis_index("c")
        sibling = (x, y, 1 - c)
        if handshake:
            _shake_hands([sibling])
        cps = [_remote(_rows_half(in_refs[i], n, 1 - c), out_refs[i], send.at[i], recv.at[i], sibling)
               for i, n in enumerate(names)]
        for cp in cps:
            cp.start()
        for cp in cps:
            cp.wait()

    shapes = [jax.ShapeDtypeStruct(_half_form_shape(n, w.shape), w.dtype) for n, w in zip(names, wire)]
    dma = pltpu.SemaphoreType.DMA
    return _comm_call(body, list(wire), shapes, [dma((n_w,)), dma((n_w,))], name, sequencer_id)


def _grad_add_halves(g, got, ci_arr, after, name):
    g3 = g if g.ndim == 3 else g[None]
    r3 = got if got.ndim == 3 else got[None]
    s, hr, cc = r3.shape
    tb = _pick(hr, (256, 176, 128))
    nb = hr // tb

    def body(s_ref, g_ref, r_ref, after_ref, o_ref):
        o_ref[...] = (g_ref[...].astype(F32) + r_ref[...].astype(F32)).astype(BF16)

    out = pl.pallas_call(
        body, name=name,
        grid_spec=pltpu.PrefetchScalarGridSpec(
            num_scalar_prefetch=1, grid=(nb,),
            in_specs=[pl.BlockSpec((s, tb, cc), lambda i, sc: (0, sc[0] * nb + i, 0)),
                      pl.BlockSpec((s, tb, cc), lambda i, sc: (0, i, 0)),
                      pl.BlockSpec(memory_space=pl.ANY)],
            out_specs=pl.BlockSpec((s, tb, cc), lambda i, sc: (0, i, 0))),
        out_shape=jax.ShapeDtypeStruct(r3.shape, BF16),
        compiler_params=_cp("arbitrary"),
    )(ci_arr, g3, r3, after)
    return out.reshape(got.shape)


def _grad_scatter_halves(names, parts, name, sequencer_id):
    n_w = len(names)

    def slot(ref, n, chip):
        if n in BY_COLUMNS:
            w = ref.shape[1] // 4
            return ref.at[:, pl.ds(pl.multiple_of(chip * w, LANES), w)]
        return ref.at[chip]

    def body(in_refs, out_refs, sems, handshake):
        send, recv = sems
        x, y, c = lax.axis_index("x"), lax.axis_index("y"), lax.axis_index("c")
        chips, chip_idx = _chip_ids(x, y)
        if handshake:
            _shake_hands([(*ch, c) for ch in chips])
        cps = []
        for p in range(3):
            for i, n in enumerate(names):
                k = p * n_w + i
                cps.append(_remote(slot(in_refs[i], n, chip_idx[p]), out_refs[k], send.at[k], recv.at[k],
                                   (*chips[p], c)))
        for cp in cps:
            cp.start()
        for cp in cps:
            cp.wait()

    def slot_shape(n, a):
        return (a.shape[0], a.shape[1] // 4) if n in BY_COLUMNS else a.shape[1:]

    shapes = [jax.ShapeDtypeStruct(slot_shape(n, a), BF16) for _ in range(3) for n, a in zip(names, parts)]
    dma = pltpu.SemaphoreType.DMA
    out = _comm_call(body, list(parts), shapes, [dma((3 * n_w,)), dma((3 * n_w,))], name, sequencer_id)
    return [out[p * n_w:(p + 1) * n_w] for p in range(3)]


def _grad_add_slots(part, got3, n, me_arr, after, name):
    hr, cs = got3[0].shape
    tb = _pick(hr, (256, 176, 128))

    def body(s_ref, own_ref, a_ref, b_ref, c_ref, after_ref, out_ref):
        acc = own_ref[...].astype(F32) + a_ref[...].astype(F32)
        out_ref[...] = (acc + b_ref[...].astype(F32)) + c_ref[...].astype(F32)

    if n in BY_COLUMNS:
        own_spec = pl.BlockSpec((tb, cs), lambda i, s: (i, s[0]))
    else:
        own_spec = pl.BlockSpec((None, tb, cs), lambda i, s: (s[0], i, 0))
    plain = pl.BlockSpec((tb, cs), lambda i, s: (i, 0))
    return pl.pallas_call(
        body, name=name,
        grid_spec=pltpu.PrefetchScalarGridSpec(
            num_scalar_prefetch=1, grid=(hr // tb,),
            in_specs=[own_spec, plain, plain, plain, pl.BlockSpec(memory_space=pl.ANY)],
            out_specs=plain),
        out_shape=jax.ShapeDtypeStruct((hr, cs), F32),
        compiler_params=_cp("arbitrary"),
    )(me_arr, part, *got3, after)


def _grad_swap_reduced(halves, name, sequencer_id):
    n_w = len(halves)

    def body(in_refs, out_refs, sems, handshake):
        send, recv = sems
        x, y, c = lax.axis_index("x"), lax.axis_index("y"), lax.axis_index("c")
        sibling = (x, y, 1 - c)
        if handshake:
            _shake_hands([sibling])
        cps = [_remote(in_refs[i], out_refs[i], send.at[i], recv.at[i], sibling) for i in range(n_w)]
        for cp in cps:
            cp.start()
        for cp in cps:
            cp.wait()

    shapes = [jax.ShapeDtypeStruct(h.shape, F32) for h in halves]
    dma = pltpu.SemaphoreType.DMA
    return _comm_call(body, list(halves), shapes, [dma((n_w,)), dma((n_w,))], name, sequencer_id)


def _adamw_layers(w, own, other, m, v, ci_arr, name):
    _, rs, cs = w.shape
    hr = rs // 2
    tb = _pick(hr, (256, 176, 128))
    nbh = hr // tb

    def body(s_ref, w_ref, o0, t0, o1, t1, m_ref, v_ref, g_out, d_out, m_out, v_out):
        mine = (pl.program_id(1) // nbh) == s_ref[0]
        g = jnp.where(pl.program_id(0) == 0, jnp.where(mine, o0[...], t0[...]), jnp.where(mine, o1[...], t1[...]))
        g_out[...] = g
        d_out[...], m_out[...], v_out[...] = _adamw_math(w_ref[...], g, m_ref[...], v_ref[...])

    both = pl.BlockSpec((None, tb, cs), lambda l, i, s: (l, i, 0))

    def half(layer, is_own):
        def index(l, i, s):
            own_block = (i // nbh) == s[0]
            use = (l == layer) & (own_block if is_own else jnp.logical_not(own_block))
            return (jnp.where(use, i % nbh, 0), 0)
        return pl.BlockSpec((tb, cs), index)

    return pl.pallas_call(
        body, name=name,
        grid_spec=pltpu.PrefetchScalarGridSpec(
            num_scalar_prefetch=1, grid=(DEPTH, rs // tb),
            in_specs=[both, half(0, True), half(0, False), half(1, True), half(1, False), both, both],
            out_specs=[both] * 4),
        out_shape=[jax.ShapeDtypeStruct(w.shape, F32)] * 4,
        compiler_params=_cp("arbitrary", "arbitrary"),
    )(ci_arr, w, own[0], other[0], own[1], other[1], m, v)


def _shift_down(v, s, t_iota):
    return jnp.where(t_iota >= s, pltpu.roll(v, s, 0), 0.0)


def _shift_up(v, s, t_iota, t):
    return jnp.where(t_iota < t - s, pltpu.roll(v, t - s, 0), 0.0)


def _ssd_conv_fwd(proj, w, b):
    t = proj.shape[0]
    k_w = SSD_CONV_K

    def body(x_ref, w_ref, b_ref, o_ref):
        x = x_ref[...]
        ti = lax.broadcasted_iota(jnp.int32, x.shape, 0)
        pre = x * w_ref[k_w - 1:k_w, :] + b_ref[...]
        for s in range(1, k_w):
            pre = pre + _shift_down(x, s, ti) * w_ref[k_w - 1 - s:k_w - s, :]
        o_ref[...] = pre * _sigmoid(pre)

    off = P_XBC // LANES
    return pl.pallas_call(
        body, name="ssd_conv_fwd", grid=(SSD_CONV_DIM // LANES,),
        in_specs=[pl.BlockSpec((t, LANES), lambda j: (0, off + j)),
                  pl.BlockSpec((k_w, LANES), lambda j: (0, j)),
                  pl.BlockSpec((1, LANES), lambda j: (0, j))],
        out_specs=pl.BlockSpec((t, LANES), lambda j: (0, j)),
        out_shape=jax.ShapeDtypeStruct((t, SSD_CONV_DIM), F32),
        compiler_params=_cp("parallel"),
    )(proj, w, b)


def _ssd_conv_bwd(dxbc, proj, w, b):
    t = proj.shape[0]
    k_w = SSD_CONV_K

    def body(d_ref, x_ref, w_ref, b_ref, dx_ref, dw_ref, db_ref):
        x = x_ref[...]
        ti = lax.broadcasted_iota(jnp.int32, x.shape, 0)
        shifted = [x] + [_shift_down(x, s, ti) for s in range(1, k_w)]
        pre = b_ref[...] + shifted[0] * w_ref[k_w - 1:k_w, :]
        for s in range(1, k_w):
            pre = pre + shifted[s] * w_ref[k_w - 1 - s:k_w - s, :]
        sg = _sigmoid(pre)
        dpre = d_ref[...] * (sg * (1.0 + pre * (1.0 - sg)))
        db_ref[...] = jnp.sum(dpre, axis=0, keepdims=True)
        dx = dpre * w_ref[k_w - 1:k_w, :]
        for s in range(k_w):
            dw_ref[k_w - 1 - s:k_w - s, :] = jnp.sum(dpre * shifted[s], axis=0, keepdims=True)
            if s:
                dx = dx + _shift_up(dpre, s, ti, t) * w_ref[k_w - 1 - s:k_w - s, :]
        dx_ref[...] = dx

    off = P_XBC // LANES
    return pl.pallas_call(
        body, name="ssd_conv_bwd", grid=(SSD_CONV_DIM // LANES,),
        in_specs=[pl.BlockSpec((t, LANES), lambda j: (0, j)),
                  pl.BlockSpec((t, LANES), lambda j: (0, off + j)),
                  pl.BlockSpec((k_w, LANES), lambda j: (0, j)),
                  pl.BlockSpec((1, LANES), lambda j: (0, j))],
        out_specs=[pl.BlockSpec((t, LANES), lambda j: (0, j)),
                   pl.BlockSpec((k_w, LANES), lambda j: (0, j)),
                   pl.BlockSpec((1, LANES), lambda j: (0, j))],
        out_shape=[jax.ShapeDtypeStruct((t, SSD_CONV_DIM), F32),
                   jax.ShapeDtypeStruct((k_w, SSD_CONV_DIM), F32),
                   jax.ShapeDtypeStruct((1, SSD_CONV_DIM), F32)],
        compiler_params=_cp("parallel"),
    )(dxbc, proj, w, b)


def _shortconv_fwd(proj, w):
    t = proj.shape[0]
    nb = SC_W // LANES
    off = P_SC // LANES

    def body(b_ref, c_ref, x_ref, w_ref, o_ref):
        u = c_ref[...] * x_ref[...]
        ti = lax.broadcasted_iota(jnp.int32, u.shape, 0)
        cv = u * w_ref[SC_K - 1:SC_K, :]
        for s in range(1, SC_K):
            cv = cv + _shift_down(u, s, ti) * w_ref[SC_K - 1 - s:SC_K - s, :]
        o_ref[...] = (b_ref[...] * cv).astype(BF16)

    return pl.pallas_call(
        body, name="shortconv_fwd", grid=(nb,),
        in_specs=[pl.BlockSpec((t, LANES), lambda j: (0, off + j)),
                  pl.BlockSpec((t, LANES), lambda j: (0, off + nb + j)),
                  pl.BlockSpec((t, LANES), lambda j: (0, off + 2 * nb + j)),
                  pl.BlockSpec((SC_K, LANES), lambda j: (0, j))],
        out_specs=pl.BlockSpec((t, LANES), lambda j: (0, j)),
        out_shape=jax.ShapeDtypeStruct((t, SC_W), BF16),
        compiler_params=_cp("parallel"),
    )(proj, proj, proj, w)


def _shortconv_bwd(dy, dy_off, proj, w):
    t = proj.shape[0]
    nb = SC_W // LANES
    off = P_SC // LANES
    doff = dy_off // LANES

    def body(d_ref, b_ref, c_ref, x_ref, w_ref, db_ref, dc_ref, dx_ref, dw_ref):
        cg, xin = c_ref[...], x_ref[...]
        u = cg * xin
        ti = lax.broadcasted_iota(jnp.int32, u.shape, 0)
        shifted = [u] + [_shift_down(u, s, ti) for s in range(1, SC_K)]
        cv = shifted[0] * w_ref[SC_K - 1:SC_K, :]
        for s in range(1, SC_K):
            cv = cv + shifted[s] * w_ref[SC_K - 1 - s:SC_K - s, :]
        d = d_ref[...]
        db_ref[...] = (d * cv).astype(BF16)
        dcv = d * b_ref[...]
        du = dcv * w_ref[SC_K - 1:SC_K, :]
        for s in range(SC_K):
            dw_ref[SC_K - 1 - s:SC_K - s, :] = jnp.sum(dcv * shifted[s], axis=0, keepdims=True)
            if s:
                du = du + _shift_up(dcv, s, ti, t) * w_ref[SC_K - 1 - s:SC_K - s, :]
        dc_ref[...] = (du * xin).astype(BF16)
        dx_ref[...] = (du * cg).astype(BF16)

    tile = pl.BlockSpec((t, LANES), lambda j: (0, j))
    outs = pl.pallas_call(
        body, name="shortconv_bwd", grid=(nb,),
        in_specs=[pl.BlockSpec((t, LANES), lambda j: (0, doff + j)),
                  pl.BlockSpec((t, LANES), lambda j: (0, off + j)),
                  pl.BlockSpec((t, LANES), lambda j: (0, off + nb + j)),
                  pl.BlockSpec((t, LANES), lambda j: (0, off + 2 * nb + j)),
                  pl.BlockSpec((SC_K, LANES), lambda j: (0, j))],
        out_specs=[tile, tile, tile, pl.BlockSpec((SC_K, LANES), lambda j: (0, j))],
        out_shape=[jax.ShapeDtypeStruct((t, SC_W), BF16)] * 3 + [jax.ShapeDtypeStruct((SC_K, SC_W), F32)],
        compiler_params=_cp("parallel"),
    )(dy, proj, proj, proj, w)
    return outs


def _cum_logf(proj, fbias_row):
    t = proj.shape[0]
    blk = CUM_BLOCK

    def body(p_ref, b_ref, o_ref, carry_ref):
        i = pl.program_id(0)

        @pl.when(i == 0)
        def _():
            carry_ref[...] = jnp.zeros_like(carry_ref)

        lf = -_softplus(-(p_ref[...] + b_ref[...]))
        r = lax.broadcasted_iota(jnp.int32, (blk, blk), 0)
        c = lax.broadcasted_iota(jnp.int32, (blk, blk), 1)
        tri = (r >= c).astype(BF16)
        o_ref[...] = _tri_left(tri, lf) + carry_ref[...]
        carry_ref[...] = o_ref[blk - 1:blk, :]

    return pl.pallas_call(
        body, name="cum_logf", grid=(t // blk,),
        in_specs=[pl.BlockSpec((blk, LANES), lambda i: (i, P_PAD // LANES)),
                  pl.BlockSpec((1, LANES), lambda i: (0, 0))],
        out_specs=pl.BlockSpec((blk, LANES), lambda i: (i, 0)),
        out_shape=jax.ShapeDtypeStruct((t, LANES), F32),
        scratch_shapes=[pltpu.VMEM((1, LANES), F32)],
        compiler_params=_cp("arbitrary"),
    )(proj, fbias_row)


def _pad_block_bwd(dcf, ddt, proj, fbias_row):
    t = proj.shape[0]
    blk = CUM_BLOCK
    nb = t // blk

    def body(dcf_ref, ddt_ref, p_ref, b_ref, o_ref, db_ref, carry_ref):
        i = pl.program_id(0)

        @pl.when(i == 0)
        def _():
            carry_ref[...] = jnp.zeros_like(carry_ref)
            db_ref[...] = jnp.zeros_like(db_ref)

        r = lax.broadcasted_iota(jnp.int32, (blk, blk), 0)
        c = lax.broadcasted_iota(jnp.int32, (blk, blk), 1)
        tri = (r <= c).astype(BF16)
        rev = _tri_left(tri, dcf_ref[...]) + carry_ref[...]
        carry_ref[...] = jnp.sum(dcf_ref[...], axis=0, keepdims=True) + carry_ref[...]
        lane = lax.broadcasted_iota(jnp.int32, (blk, LANES), 1)
        is_f = (lane >= PAD_F0) & (lane < PAD_F0 + FOX_HEADS)
        df = jnp.where(is_f, rev * _sigmoid(-(p_ref[...] + b_ref[...])), 0.0)
        db_ref[...] += jnp.sum(df, axis=0, keepdims=True)
        o_ref[...] = jnp.where(lane < PAD_DT0 + SSD_HEADS, ddt_ref[...], df).astype(BF16)

    return pl.pallas_call(
        body, name="pad_block_bwd", grid=(nb,),
        in_specs=[pl.BlockSpec((blk, LANES), lambda i: (nb - 1 - i, 0)),
                  pl.BlockSpec((blk, LANES), lambda i: (nb - 1 - i, 0)),
                  pl.BlockSpec((blk, LANES), lambda i: (nb - 1 - i, P_PAD // LANES)),
                  pl.BlockSpec((1, LANES), lambda i: (0, 0))],
        out_specs=[pl.BlockSpec((blk, LANES), lambda i: (nb - 1 - i, 0)),
                   pl.BlockSpec((1, LANES), lambda i: (0, 0))],
        out_shape=[jax.ShapeDtypeStruct((t, LANES), BF16), jax.ShapeDtypeStruct((1, LANES), F32)],
        scratch_shapes=[pltpu.VMEM((1, LANES), F32)],
        compiler_params=_cp("arbitrary"),
    )(dcf, ddt, proj, fbias_row)


def _att_scores(q, k, ck, diagonal, blk):
    s = _dot_nt(q, k) - ck
    if diagonal:
        r = lax.broadcasted_iota(jnp.int32, (blk, blk), 0)
        c = lax.broadcasted_iota(jnp.int32, (blk, blk), 1)
        s = jnp.where(r >= c, s, NEG)
    return s


def _fox_fwd(q, k, v, cf_row):
    h, t, hd = q.shape
    blk = min(ATT_BLOCK, t)
    nb = t // blk
    pair = 2

    def body(q_ref, k_ref, v_ref, ck_ref, o_ref, lse_ref):
        qi = pl.program_id(1)
        qv = [q_ref[hh] for hh in range(pair)]

        def step(j, carry, diagonal):
            off = pl.multiple_of(j * blk, blk)
            out = []
            for hh in range(pair):
                m, l, acc = carry[hh]
                s = _att_scores(qv[hh], k_ref[hh, pl.ds(off, blk), :], ck_ref[hh, :, pl.ds(off, blk)], diagonal, blk)
                m_new = jnp.maximum(m, jnp.max(s, axis=1, keepdims=True))
                alpha = jnp.exp(m - m_new)
                p = jnp.exp(s - m_new)
                l = alpha * l + jnp.sum(p, axis=1, keepdims=True)
                acc = alpha * acc + _dot(p.astype(BF16), v_ref[hh, pl.ds(off, blk), :])
                out.append((m_new, l, acc))
            return tuple(out)

        one = (jnp.full((blk, 1), NEG, F32), jnp.zeros((blk, 1), F32), jnp.zeros((blk, hd), F32))
        carry = lax.fori_loop(0, qi, lambda j, cr: step(j, cr, False), (one,) * pair)
        for hh, (m, l, acc) in enumerate(step(qi, carry, True)):
            o_ref[hh] = acc / l
            lse_ref[hh] = m + jnp.log(l)

    qmap = lambda hp, i: (hp, i, 0)
    whole = lambda hp, i: (hp, 0, 0)
    return pl.pallas_call(
        body, name="fox_fwd", grid=(h // pair, nb),
        in_specs=[pl.BlockSpec((pair, blk, hd), qmap), pl.BlockSpec((pair, t, hd), whole),
                  pl.BlockSpec((pair, t, hd), whole), pl.BlockSpec((pair, 1, t), whole)],
        out_specs=[pl.BlockSpec((pair, blk, hd), qmap), pl.BlockSpec((pair, blk, 1), qmap)],
        out_shape=[jax.ShapeDtypeStruct((h, t, hd), F32), jax.ShapeDtypeStruct((h, t, 1), F32)],
        compiler_params=_cp("parallel", "arbitrary"),
    )(q, k, v, cf_row)


def _fox_bwd(q, k, v, cf_row, o, lse, do):
    h, t, hd = q.shape
    blk = min(ATT_BLOCK, t)
    nb = t // blk

    def body(q_ref, k_ref, v_ref, ck_ref, o_ref, lse_ref, do_ref,
             dq_ref, dk_ref, dv_ref, dcq_ref, dck_ref, delta_s):
        kj = pl.program_id(1)

        @pl.when(kj == 0)
        def _():
            dq_ref[...] = jnp.zeros_like(dq_ref)
            dcq_ref[...] = jnp.zeros_like(dcq_ref)

            def fill(i, _):
                rows = pl.ds(pl.multiple_of(i * blk, blk), blk)
                delta_s[rows, :] = jnp.sum(do_ref[rows, :] * o_ref[rows, :], axis=1, keepdims=True)
                return 0

            lax.fori_loop(0, nb, fill, 0)

        kb, vb, ck = k_ref[...], v_ref[...], ck_ref[...]

        def step(i, carry, diagonal):
            dk, dv, dck = carry
            rows = pl.ds(pl.multiple_of(i * blk, blk), blk)
            qb = q_ref[rows, :]
            do_b = do_ref[rows, :].astype(BF16)
            s = _att_scores(qb, kb, ck, diagonal, blk)
            p = jnp.exp(s - lse_ref[rows, :])
            dv = dv + _dot_tn(p.astype(BF16), do_b)
            ds = p * (_dot_nt(do_b, vb) - delta_s[rows, :])
            ds_b = ds.astype(BF16)
            dk = dk + _dot_tn(ds_b, qb)
            dq_ref[rows, :] += _dot(ds_b, kb)
            dcq_ref[rows, :] += jnp.sum(ds, axis=1, keepdims=True)
            dck = dck - jnp.sum(ds, axis=0, keepdims=True)
            return dk, dv, dck

        init = (jnp.zeros((blk, hd), F32), jnp.zeros((blk, hd), F32), jnp.zeros((1, blk), F32))
        carry = step(kj, init, True)
        dk, dv, dck = lax.fori_loop(kj + 1, nb, lambda i, cr: step(i, cr, False), carry)
        dk_ref[...] = dk
        dv_ref[...] = dv
        dck_ref[...] = dck

    kmap = lambda hh, j: (hh, j, 0)
    whole = lambda hh, j: (hh, 0, 0)
    return pl.pallas_call(
        body, name="fox_bwd", grid=(h, nb),
        in_specs=[pl.BlockSpec((None, t, hd), whole), pl.BlockSpec((None, blk, hd), kmap),
                  pl.BlockSpec((None, blk, hd), kmap), pl.BlockSpec((None, 1, blk), lambda hh, j: (hh, 0, j)),
                  pl.BlockSpec((None, t, hd), whole), pl.BlockSpec((None, t, 1), whole),
                  pl.BlockSpec((None, t, hd), whole)],
        out_specs=[pl.BlockSpec((None, t, hd), whole), pl.BlockSpec((None, blk, hd), kmap),
                   pl.BlockSpec((None, blk, hd), kmap), pl.BlockSpec((None, t, 1), whole),
                   pl.BlockSpec((None, 1, blk), lambda hh, j: (hh, 0, j))],
        out_shape=[jax.ShapeDtypeStruct((h, t, hd), F32), jax.ShapeDtypeStruct((h, t, hd), F32),
                   jax.ShapeDtypeStruct((h, t, hd), F32), jax.ShapeDtypeStruct((h, t, 1), F32),
                   jax.ShapeDtypeStruct((h, 1, t), F32)],
        scratch_shapes=[pltpu.VMEM((t, 1), F32)],
        compiler_params=_cp("parallel", "arbitrary"),
    )(q, k, v, cf_row, o, lse, do)


def _lane_col(v, h):
    lane = lax.broadcasted_iota(jnp.int32, v.shape, 1)
    return jnp.sum(jnp.where(lane == h, v, 0.0), axis=1, keepdims=True)


def _sub_row(v, h):
    sub = lax.broadcasted_iota(jnp.int32, v.shape, 0)
    return jnp.sum(jnp.where(sub == h, v, 0.0), axis=0, keepdims=True)


def _ssd_decays(pad, pad_t, dtb_row, alog_row, dtb_col, alog_col, blk):
    r = lax.broadcasted_iota(jnp.int32, (blk, blk), 0)
    c = lax.broadcasted_iota(jnp.int32, (blk, blk), 1)
    tril = r >= c
    dt_c = _softplus(pad + dtb_row)
    acs_c = _tri_left(tril.astype(BF16), dt_c * (-jnp.exp(alog_row)))
    dt_r = _softplus(pad_t + dtb_col)
    acs_r = _tri_right(dt_r * (-jnp.exp(alog_col)), (r <= c).astype(BF16))
    rows = lax.broadcasted_iota(jnp.int32, acs_c.shape, 0)
    acs_last = jnp.sum(jnp.where(rows == blk - 1, acs_c, 0.0), axis=0, keepdims=True)
    return dt_c, acs_c, acs_r, acs_last, tril


def _pair_terms(pair, dt_c, acs_c, acs_r, acs_last, d_row, blk):
    lane = lax.broadcasted_iota(jnp.int32, (blk, LANES), 1)
    lo = lane < 64
    lo_row = lax.broadcasted_iota(jnp.int32, (1, LANES), 1) < 64
    h0, h1 = 2 * pair, 2 * pair + 1
    col = [_lane_col(acs_c, h0), _lane_col(acs_c, h1)]
    row = [_sub_row(acs_r, h0), _sub_row(acs_r, h1)]
    last = [_lane_col(acs_last, h0), _lane_col(acs_last, h1)]
    dt_p = jnp.where(lo, _lane_col(dt_c, h0), _lane_col(dt_c, h1))
    e_p = jnp.where(lo, jnp.exp(col[0]), jnp.exp(col[1]))
    w_p = jnp.where(lo, jnp.exp(last[0] - col[0]), jnp.exp(last[1] - col[1]))
    decay_p = jnp.where(lo_row, jnp.exp(last[0]), jnp.exp(last[1]))
    d_p = jnp.where(lo_row, _lane_col(d_row, h0), _lane_col(d_row, h1))
    return lo, lo_row, col, row, last, dt_p, e_p, w_p, decay_p, d_p


def _ssd_specs(t, blk, rev):
    nc = t // blk
    ix = (lambda i: nc - 1 - i) if rev else (lambda i: i)
    xbc = pl.BlockSpec((blk, SSD_CONV_DIM), lambda i: (ix(i), 0))
    pad = pl.BlockSpec((blk, LANES), lambda i: (ix(i), P_PAD // LANES))
    pad_t = pl.BlockSpec((LANES, blk), lambda i: (0, ix(i)))
    z = pl.BlockSpec((blk, SSD_W), lambda i: (ix(i), 0))
    row = pl.BlockSpec((1, LANES), lambda i: (0, 0))
    colv = pl.BlockSpec((LANES, 1), lambda i: (0, 0))
    ng = pl.BlockSpec((1, SSD_W), lambda i: (0, 0))
    y = pl.BlockSpec((blk, SSD_W), lambda i: (ix(i), 0))
    st = pl.BlockSpec((None, 4, LANES, LANES), lambda i: (ix(i), 0, 0, 0))
    return nc, xbc, pad, pad_t, z, row, colv, ng, y, st


def _ssd_fwd(xbc, proj, pad_t, dtb_row, alog_row, d_row, dtb_col, alog_col, ng):
    t = xbc.shape[0]
    blk = min(SSD_CHUNK, t)
    nc, s_xbc, s_pad, s_padt, s_z, s_row, s_col, s_ng, s_y, s_st = _ssd_specs(t, blk, False)

    def body(xbc_ref, pad_ref, padt_ref, z_ref, dtb_ref, alog_ref, d_ref, dtbc_ref, alogc_ref, ng_ref,
             out_ref, ypre_ref, st_ref, state):
        @pl.when(pl.program_id(0) == 0)
        def _():
            state[...] = jnp.zeros_like(state)

        dt_c, acs_c, acs_r, acs_last, tril = _ssd_decays(
            pad_ref[...], padt_ref[...], dtb_ref[...], alog_ref[...], dtbc_ref[...], alogc_ref[...], blk)
        ys = []
        g_mat = {}
        for pair in range(4):
            g = pair // 2
            bg = xbc_ref[:, SSD_W + LANES * g:SSD_W + LANES * (g + 1)].astype(BF16)
            cg = xbc_ref[:, SSD_W + 2 * LANES + LANES * g:SSD_W + 2 * LANES + LANES * (g + 1)].astype(BF16)
            if g not in g_mat:
                g_mat[g] = _dot_nt(cg, bg)
            xs_p = xbc_ref[:, LANES * pair:LANES * (pair + 1)]
            lo, _, col, row, _, dt_p, e_p, w_p, decay_p, d_p = _pair_terms(
                pair, dt_c, acs_c, acs_r, acs_last, d_ref[...], blk)
            x_p = xs_p * dt_p
            y = None
            for hh in range(2):
                lm = jnp.exp(jnp.where(tril, col[hh] - row[hh], NEG))
                m_h = (g_mat[g] * lm).astype(BF16)
                x_h = jnp.where(lo if hh == 0 else ~lo, x_p, 0.0).astype(BF16)
                y_h = _dot(m_h, x_h)
                y = y_h if y is None else y + y_h
            st_in = state[pair]
            st_ref[pair] = st_in
            y = y + e_p * _dot(cg, st_in.astype(BF16))
            state[pair] = decay_p * st_in + _dot_tn(bg, (x_p * w_p).astype(BF16))
            ys.append(y + d_p * xs_p)
        y_all = jnp.concatenate(ys, axis=1)
        ypre_ref[...] = y_all
        z = z_ref[...]
        y2 = y_all * (z * _sigmoid(z))
        outs = []
        for g in range(2):
            seg = y2[:, 256 * g:256 * (g + 1)]
            rr = lax.rsqrt(jnp.mean(seg * seg, axis=-1, keepdims=True) + RMS_EPS)
            outs.append(seg * rr * ng_ref[:, 256 * g:256 * (g + 1)])
        out_ref[...] = jnp.concatenate(outs, axis=1).astype(BF16)

    return pl.pallas_call(
        body, name="ssd_scan_fwd", grid=(nc,),
        in_specs=[s_xbc, s_pad, s_padt, s_z, s_row, s_row, s_row, s_col, s_col, s_ng],
        out_specs=[s_y, s_y, s_st],
        out_shape=[jax.ShapeDtypeStruct((t, SSD_W), BF16), jax.ShapeDtypeStruct((t, SSD_W), F32),
                   jax.ShapeDtypeStruct((nc, 4, LANES, LANES), F32)],
        scratch_shapes=[pltpu.VMEM((4, LANES, LANES), F32)],
        compiler_params=_cp("arbitrary"),
    )(xbc, proj, pad_t, proj, dtb_row, alog_row, d_row, dtb_col, alog_col, ng)


def _ssd_bwd(dout, dout_off, xbc, proj, pad_t, ypre, states, dtb_row, alog_row, d_row, dtb_col, alog_col, ng):
    t = xbc.shape[0]
    blk = min(SSD_CHUNK, t)
    nc, s_xbc, s_pad, s_padt, s_z, s_row, s_col, s_ng, s_y, s_st = _ssd_specs(t, blk, True)
    s_dout = pl.BlockSpec((blk, SSD_W), lambda i: (nc - 1 - i, dout_off // SSD_W))

    def body(dout_ref, xbc_ref, pad_ref, padt_ref, z_ref, ypre_ref, st_ref, dtb_ref, alog_ref, d_ref,
             dtbc_ref, alogc_ref, ng_ref, dxbc_ref, ddt_ref, dz_ref, acc_ref, dng_ref, dstate):
        @pl.when(pl.program_id(0) == 0)
        def _():
            dstate[...] = jnp.zeros_like(dstate)
            acc_ref[...] = jnp.zeros_like(acc_ref)
            dng_ref[...] = jnp.zeros_like(dng_ref)

        pad = pad_ref[...]
        dt_c, acs_c, acs_r, acs_last, tril = _ssd_decays(
            pad, padt_ref[...], dtb_ref[...], alog_ref[...], dtbc_ref[...], alogc_ref[...], blk)
        a_row = -jnp.exp(alog_ref[...])

        z = z_ref[...]
        sz = _sigmoid(z)
        silu_z = z * sz
        y_pre = ypre_ref[...]
        y2 = y_pre * silu_z
        dy2 = []
        for g in range(2):
            sl = slice(256 * g, 256 * (g + 1))
            seg = y2[:, sl]
            rr = lax.rsqrt(jnp.mean(seg * seg, axis=-1, keepdims=True) + RMS_EPS)
            nrm = seg * rr
            d_seg = dout_ref[:, sl]
            dng_ref[:, sl] += jnp.sum(d_seg * nrm, axis=0, keepdims=True)
            dn = d_seg * ng_ref[:, sl]
            dy2.append(rr * (dn - nrm * jnp.mean(dn * nrm, axis=-1, keepdims=True)))
        dy2 = jnp.concatenate(dy2, axis=1)
        dz_ref[...] = (dy2 * y_pre * (sz * (1.0 + z * (1.0 - sz)))).astype(BF16)
        dy_all = dy2 * silu_z

        lane_row = lax.broadcasted_iota(jnp.int32, (1, LANES), 1)
        lane_blk = lax.broadcasted_iota(jnp.int32, (blk, LANES), 1)
        row_col = lax.broadcasted_iota(jnp.int32, (blk, 1), 0)
        ddt = jnp.zeros((blk, LANES), F32)
        dacs = jnp.zeros((blk, LANES), F32)
        dd_row = jnp.zeros((1, LANES), F32)
        ones_b = jnp.ones((blk, LANES), BF16)
        dxs = []
        d_b = [None, None]
        d_c = [None, None]
        d_g = [None, None]
        bgs, cgs = {}, {}
        g_mat = {}
        for pair in range(4):
            g = pair // 2
            if g not in g_mat:
                bgs[g] = xbc_ref[:, SSD_W + LANES * g:SSD_W + LANES * (g + 1)].astype(BF16)
                cgs[g] = xbc_ref[:, SSD_W + 2 * LANES + LANES * g:SSD_W + 2 * LANES + LANES * (g + 1)].astype(BF16)
                g_mat[g] = _dot_nt(cgs[g], bgs[g])
            bg, cg = bgs[g], cgs[g]
            xs_p = xbc_ref[:, LANES * pair:LANES * (pair + 1)]
            lo, lo_row, col, row, last, dt_p, e_p, w_p, decay_p, d_p = _pair_terms(
                pair, dt_c, acs_c, acs_r, acs_last, d_ref[...], blk)
            x_p = xs_p * dt_p
            dy_p = dy_all[:, LANES * pair:LANES * (pair + 1)]
            st_in = st_ref[pair]
            dst = dstate[pair]
            dx_diag = None
            for hh in range(2):
                sel = lo if hh == 0 else ~lo
                lm = jnp.exp(jnp.where(tril, col[hh] - row[hh], NEG))
                m_f = g_mat[g] * lm
                m_h = m_f.astype(BF16)
                x_h = jnp.where(sel, x_p, 0.0).astype(BF16)
                dy_h = jnp.where(sel, dy_p, 0.0).astype(BF16)
                dxd = _dot_tn(m_h, dy_h)
                dm = _dot_nt(dy_h, x_h)
                dg_h = dm * lm
                p_b = (dm * m_f).astype(BF16)
                dacs = dacs + jnp.where(lane_blk == 2 * pair + hh, _dot(p_b, ones_b) - _dot_tn(p_b, ones_b), 0.0)
                dx_diag = dxd if dx_diag is None else dx_diag + dxd
                d_g[g] = dg_h if d_g[g] is None else d_g[g] + dg_h
            st_b = st_in.astype(BF16)
            dst_b = dst.astype(BF16)
            y_off = e_p * _dot(cg, st_b)
            edy = (e_p * dy_p).astype(BF16)
            dc_off = _dot_nt(edy, st_b)
            d_c[g] = dc_off if d_c[g] is None else d_c[g] + dc_off
            dstate[pair] = decay_p * dst + _dot_tn(cg, edy)
            dx_state = _dot(bg, dst_b) * w_p
            db_st = _dot_nt((x_p * w_p).astype(BF16), dst_b)
            d_b[g] = db_st if d_b[g] is None else d_b[g] + db_st
            dx = dx_diag + dx_state
            dxs.append(dx * dt_p + d_p * dy_p)
            prod_dt = dx * xs_p
            prod_acs = dy_p * y_off - x_p * dx_state
            prod_st = x_p * dx_state
            prod_d = dy_p * xs_p
            st_prod = jnp.sum(dst * st_in, axis=0, keepdims=True)
            for hh in range(2):
                h = 2 * pair + hh
                sel = lo if hh == 0 else ~lo
                sel_row = lo_row if hh == 0 else ~lo_row
                ddt_h = jnp.sum(jnp.where(sel, prod_dt, 0.0), axis=1, keepdims=True)
                dacs_h = jnp.sum(jnp.where(sel, prod_acs, 0.0), axis=1, keepdims=True)
                tail = jnp.sum(jnp.sum(jnp.where(sel, prod_st, 0.0), axis=1, keepdims=True), axis=0, keepdims=True)
                tail = tail + jnp.exp(last[hh]) * jnp.sum(jnp.where(sel_row, st_prod, 0.0), axis=1, keepdims=True)
                dacs_h = dacs_h + jnp.where(row_col == blk - 1, tail, 0.0)
                dd_h = jnp.sum(jnp.sum(jnp.where(sel, prod_d, 0.0), axis=1, keepdims=True), axis=0, keepdims=True)
                ddt = ddt + jnp.where(lane_blk == h, ddt_h, 0.0)
                dacs = dacs + jnp.where(lane_blk == h, dacs_h, 0.0)
                dd_row = dd_row + jnp.where(lane_row == h, dd_h, 0.0)
        for g in range(2):
            dg_b = d_g[g].astype(BF16)
            d_c[g] = d_c[g] + _dot(dg_b, bgs[g])
            d_b[g] = d_b[g] + _dot_tn(dg_b, cgs[g])
        r = lax.broadcasted_iota(jnp.int32, (blk, blk), 0)
        c = lax.broadcasted_iota(jnp.int32, (blk, blk), 1)
        da = _tri_left((r <= c).astype(BF16), dacs)
        ddt = ddt + da * a_row
        d_raw = ddt * _sigmoid(pad + dtb_ref[...])
        ddt_ref[...] = d_raw
        acc_ref[0:1, :] += jnp.sum(da * dt_c, axis=0, keepdims=True) * a_row
        acc_ref[1:2, :] += dd_row
        acc_ref[2:3, :] += jnp.sum(d_raw, axis=0, keepdims=True)
        dxbc_ref[...] = jnp.concatenate(dxs + d_b + d_c, axis=1)

    return pl.pallas_call(
        body, name="ssd_scan_bwd", grid=(nc,),
        in_specs=[s_dout, s_xbc, s_pad, s_padt, s_z, s_y, s_st, s_row, s_row, s_row, s_col, s_col, s_ng],
        out_specs=[s_xbc, pl.BlockSpec((blk, LANES), lambda i: (nc - 1 - i, 0)), s_y,
                   pl.BlockSpec((8, LANES), lambda i: (0, 0)), s_ng],
        out_shape=[jax.ShapeDtypeStruct((t, SSD_CONV_DIM), F32), jax.ShapeDtypeStruct((t, LANES), F32),
                   jax.ShapeDtypeStruct((t, SSD_W), BF16), jax.ShapeDtypeStruct((8, LANES), F32),
                   jax.ShapeDtypeStruct((1, SSD_W), F32)],
        scratch_shapes=[pltpu.VMEM((4, LANES, LANES), F32)],
        compiler_params=_cp("arbitrary"),
    )(dout, xbc, proj, pad_t, proj, ypre, states, dtb_row, alog_row, d_row, dtb_col, alog_col, ng)


def _pad_lanes(v, off):
    return jnp.zeros((1, LANES), F32).at[0, off:off + v.shape[0]].set(v)


def _perm_mix_w_in(w):
    z, xbc, dt = w[:, 0:512], w[:, 512:1536], w[:, 1536:1544]
    qkv, f, sc = w[:, 1544:2312], w[:, 2312:2316], w[:, 2316:3084]
    padblk = jnp.zeros((w.shape[0], LANES), w.dtype).at[:, PAD_DT0:PAD_DT0 + 8].set(dt).at[:, PAD_F0:PAD_F0 + 4].set(f)
    return jnp.concatenate([z, xbc, qkv, sc, padblk], axis=1)


def _unperm_mix_w_in(wp):
    z, xbc, qkv, sc = wp[:, 0:512], wp[:, 512:1536], wp[:, 1536:2304], wp[:, 2304:3072]
    dt, f = wp[:, P_PAD + PAD_DT0:P_PAD + PAD_DT0 + 8], wp[:, P_PAD + PAD_F0:P_PAD + PAD_F0 + 4]
    return jnp.concatenate([z, xbc, dt, qkv, f, sc], axis=1)


def _heads(m):
    return jnp.transpose(m.reshape(m.shape[0], FOX_HEADS, FOX_HD), (1, 0, 2))


def _unheads(m):
    return jnp.transpose(m, (1, 0, 2)).reshape(m.shape[1], FOX_W)


def _ffn_fwd(h, w_in, w_out, tag):
    u, a = _ffn_in_swiglu(h, w_in, f"ffn_in_{tag}")
    y = _matmul(a, w_out, "nn", F32, f"ffn_out_{tag}")
    return y, (h, u, a)


def _ffn_bwd(dy, saved, w_in, w_out, tag):
    h, u, a = saved
    du = _ffn_out_dx_swiglu(dy, w_out, u, f"ffn_out_dx_{tag}")
    dw_out = _matmul(a, dy, "tn", BF16, f"ffn_out_dw_{tag}")
    dh = _matmul(du, w_in, "nt", F32, f"ffn_in_dx_{tag}", halves="a")
    dw_in = _matmul(h, du, "tn", BF16, f"ffn_in_dw_{tag}", halves="b")
    return dh, dw_in, dw_out


def _mix_fwd(h, wp, w_out, sp, tag):
    proj = _matmul(h, wp, "nn", F32, f"mix_in_{tag}")
    pad_t = jnp.transpose(proj[:, P_PAD:P_PAD + LANES])
    xbc = _ssd_conv_fwd(proj, sp["conv_w"], sp["conv_b"])
    y_ssd, ypre, states = _ssd_fwd(xbc, proj, pad_t, sp["dtb_row"], sp["alog_row"], sp["d_row"],
                                   sp["dtb_col"], sp["alog_col"], sp["ng"])
    cf = _cum_logf(proj, sp["fbias_row"])
    cf_row = jnp.transpose(cf[:, PAD_F0:PAD_F0 + FOX_HEADS])[:, None, :]
    q = _heads((proj[:, P_QKV:P_QKV + 256] * FOX_SCALE).astype(BF16))
    k = _heads(proj[:, P_QKV + 256:P_QKV + 512].astype(BF16))
    v = _heads(proj[:, P_QKV + 512:P_QKV + 768].astype(BF16))
    o, lse = _fox_fwd(q, k, v, cf_row)
    y_sc = _shortconv_fwd(proj, sp["sconv_w"])
    ymix = jnp.concatenate([y_ssd, _unheads(o).astype(BF16), y_sc], axis=1)
    y = _matmul(ymix, w_out, "nn", F32, f"mix_out_{tag}")
    return y, (h, proj, pad_t, xbc, ypre, states, q, k, v, cf_row, o, lse, ymix)


def _mix_bwd(dy, saved, wp, w_out, sp, tag):
    h, proj, pad_t, xbc, ypre, states, q, k, v, cf_row, o, lse, ymix = saved
    dymix = _matmul(dy, w_out, "nt", F32, f"mix_out_dx_{tag}")
    dw_out = _matmul(ymix, dy, "tn", BF16, f"mix_out_dw_{tag}")
    dxbc, ddt, dz, ssd_acc, dng = _ssd_bwd(dymix, 0, xbc, proj, pad_t, ypre, states, sp["dtb_row"],
                                           sp["alog_row"], sp["d_row"], sp["dtb_col"], sp["alog_col"], sp["ng"])
    dxbc_raw, dconv_w, dconv_b = _ssd_conv_bwd(dxbc, proj, sp["conv_w"], sp["conv_b"])
    do = _heads(dymix[:, SSD_W:SSD_W + FOX_W])
    dq, dk, dv, dcq, dck = _fox_bwd(q, k, v, cf_row, o, lse, do)
    dq = dq * FOX_SCALE
    dcf4 = dcq[:, :, 0] + dck[:, 0, :]
    dcf = jnp.zeros((h.shape[0], LANES), F32).at[:, PAD_F0:PAD_F0 + FOX_HEADS].set(jnp.transpose(dcf4))
    dpad, dfb = _pad_block_bwd(dcf, ddt, proj, sp["fbias_row"])
    dscb, dscc, dscx, dsconv_w = _shortconv_bwd(dymix, SSD_W + FOX_W, proj, sp["sconv_w"])
    dproj = jnp.concatenate([dz, dxbc_raw.astype(BF16), _unheads(dq).astype(BF16), _unheads(dk).astype(BF16),
                             _unheads(dv).astype(BF16), dscb, dscc, dscx, dpad], axis=1)
    dh = _matmul(dproj, wp, "nt", F32, f"mix_in_dx_{tag}")
    dwp = _matmul(h, dproj, "tn", BF16, f"mix_in_dw_{tag}")
    small = dict(conv_w=dconv_w, conv_b=dconv_b[0], dt_bias=ssd_acc[2, 0:8], a_log=ssd_acc[0, 0:8],
                 d=ssd_acc[1, 0:8], norm_g=dng[0], f_bias=dfb[0, PAD_F0:PAD_F0 + FOX_HEADS], sconv_w=dsconv_w)
    return dh, _unperm_mix_w_in(dwp), dw_out, small


def _local_step(x, tgt, mod, wfull, small_p, before_sub_backward=None, after_sub_backward=None):
    row = lambda v: v.reshape(1, -1)
    subs = [(l, j) for l in range(DEPTH) for j in range(N_SUB)]
    factor = (0.5, 1.0, 0.5)
    w_names = (("ffn1_w_in", "ffn1_w_out"), ("mix_w_in", "mix_w_out"), ("ffn2_w_in", "ffn2_w_out"))
    lg = [[row(small_p["ln_g"][l, j]) for j in range(N_SUB)] for l in range(DEPTH)]
    lb = [[row(small_p["ln_b"][l, j]) for j in range(N_SUB)] for l in range(DEPTH)]
    sps = []
    for l in range(DEPTH):
        sp = dict(
            conv_w=small_p["ssd_conv_w"][l], conv_b=row(small_p["ssd_conv_b"][l]),
            dtb_row=_pad_lanes(small_p["ssd_dt_bias"][l], PAD_DT0), alog_row=_pad_lanes(small_p["ssd_a_log"][l], 0),
            d_row=_pad_lanes(small_p["ssd_d"][l], 0), ng=row(small_p["ssd_norm_g"][l]),
            fbias_row=_pad_lanes(small_p["fox_f_bias"][l], PAD_F0), sconv_w=small_p["sconv_w"][l])
        sp["dtb_col"] = jnp.transpose(sp["dtb_row"])
        sp["alog_col"] = jnp.transpose(sp["alog_row"])
        sps.append(sp)
    tags = [f"l{l}{('f1', 'mx', 'f2')[j]}" for l, j in subs]

    x0, h = _ln_in_fwd(x, row(small_p["ln_in_g"]), row(small_p["ln_in_b"]), mod[0, 0])
    cur = x0
    xins, ys, inner, weights = [], [], [], []
    wfull = list(wfull)
    for idx, (l, j) in enumerate(subs):
        if idx == 1 and callable(wfull[DEPTH - 1]):
            wfull[DEPTH - 1] = wfull[DEPTH - 1](cur)
        w_in, w_out = wfull[l][w_names[j][0]], wfull[l][w_names[j][1]]
        if idx > 0:
            (w_in, w_out), (cur, h) = lax.optimization_barrier(((w_in, w_out), (cur, h)))
        w_out = w_out.reshape(-1, w_out.shape[-1])
        if j == 1:
            w_in = _perm_mix_w_in(jnp.concatenate([w_in[s] for s in range(4)], axis=1))
        weights.append((w_in, w_out))
        if j == 1:
            y, sv = _mix_fwd(h, w_in, w_out, sps[l], tags[idx])
        else:
            y, sv = _ffn_fwd(h, w_in, w_out, tags[idx])
        nxt = mod[subs[idx + 1]] if idx + 1 < len(subs) else None
        xins.append(cur)
        ys.append(y)
        inner.append(sv)
        cur, h = _res_ln(cur, y, mod[l, j], lg[l][j], lb[l][j], factor[j], f"res_ln_{tags[idx]}", nxt)

    dcur, loss_acc = _loss_head(cur, tgt)
    last = len(subs) - 1
    l, j = subs[last]
    dres, dy, acc = _res_ln_bwd(dcur, xins[last], ys[last], mod[l, j], lg[l][j], factor[j], f"res_ln_bwd_{tags[last]}")
    ln_acc = {last: acc}
    shift_scale = {}
    big_grads = [dict() for _ in range(DEPTH)]
    small_g = [None] * DEPTH
    for idx in reversed(range(len(subs))):
        l, j = subs[idx]
        w_in, w_out = weights[idx]
        if before_sub_backward is not None:
            before_sub_backward(l, j, dy)
        if j == 1:
            dh, g_in, g_out, small_g[l] = _mix_bwd(dy, inner[idx], w_in, w_out, sps[l], tags[idx])
        else:
            dh, g_in, g_out = _ffn_bwd(dy, inner[idx], w_in, w_out, tags[idx])
        big_grads[l][w_names[j][0]], big_grads[l][w_names[j][1]] = g_in, g_out
        if after_sub_backward is not None:
            after_sub_backward(l, j, big_grads[l], dh)
        if idx > 0:
            pl_, pj = subs[idx - 1]
            dres, dy, acc5 = _mod_res_bwd(dres, dh, mod[l, j], xins[idx - 1], ys[idx - 1], mod[pl_, pj], lg[pl_][pj],
                                          lb[pl_][pj], factor[pj], f"mod_res_bwd_{tags[idx]}")
            shift_scale[idx], ln_acc[idx - 1] = acc5[0:2], acc5[2:5]
        else:
            dx0, shift_scale[0] = _modulate_bwd(dres, dh, x0, mod[0, 0], "mod_bwd_first")
    dx, acc_in = _ln_in_bwd(dx0, x, row(small_p["ln_in_g"]))
    dmod = []
    for l in range(DEPTH):
        ids = [N_SUB * l + j for j in range(N_SUB)]
        small_g[l]["ln_g"] = jnp.stack([ln_acc[i][0] for i in ids])
        small_g[l]["ln_b"] = jnp.stack([ln_acc[i][1] for i in ids])
        dmod.append(jnp.stack([jnp.concatenate([shift_scale[i], ln_acc[i][2:3]], axis=0) for i in ids]))
    return loss_acc[0, 0], dx, big_grads, small_g, jnp.stack(dmod), acc_in


SMALL_ORDER = ("ssd_conv_w", "ssd_conv_b", "ssd_dt_bias", "ssd_a_log", "ssd_d", "ssd_norm_g", "fox_f_bias",
               "sconv_w", "ln_g", "ln_b")
SMALL_KEY = dict(ssd_conv_w="conv_w", ssd_conv_b="conv_b", ssd_dt_bias="dt_bias", ssd_a_log="a_log", ssd_d="d",
                 ssd_norm_g="norm_g", fox_f_bias="f_bias", sconv_w="sconv_w", ln_g="ln_g", ln_b="ln_b")
COL_SHARDED_SMALL = ("ssd_conv_w", "sconv_w", "ln_g", "ln_b")


def _pad_to(v, n):
    return jnp.concatenate([v, jnp.zeros((n - v.shape[0],), v.dtype)])


def kernel(x, c, ln_in_g, ln_in_b, ada_w, ada_b, ffn1_w_in, ffn1_w_out, mix_w_in, mix_w_out, ssd_conv_w, ssd_conv_b, ssd_dt_bias, ssd_a_log, ssd_d, ssd_norm_g, fox_f_bias, sconv_w, ffn2_w_in, ffn2_w_out, ln_g, ln_b, loss_target, m_ln_in_g, m_ln_in_b, m_ada_w, m_ada_b, m_ffn1_w_in, m_ffn1_w_out, m_mix_w_in, m_mix_w_out, m_ssd_conv_w, m_ssd_conv_b, m_ssd_dt_bias, m_ssd_a_log, m_ssd_d, m_ssd_norm_g, m_fox_f_bias, m_sconv_w, m_ffn2_w_in, m_ffn2_w_out, m_ln_g, m_ln_b, v_ln_in_g, v_ln_in_b, v_ada_w, v_ada_b, v_ffn1_w_in, v_ffn1_w_out, v_mix_w_in, v_mix_w_out, v_ssd_conv_w, v_ssd_conv_b, v_ssd_dt_bias, v_ssd_a_log, v_ssd_d, v_ssd_norm_g, v_fox_f_bias, v_sconv_w, v_ffn2_w_in, v_ffn2_w_out, v_ln_g, v_ln_b):
    names = ("ln_in_g", "ln_in_b", "ada_w", "ada_b", "ffn1_w_in", "ffn1_w_out", "mix_w_in", "mix_w_out",
             "ssd_conv_w", "ssd_conv_b", "ssd_dt_bias", "ssd_a_log", "ssd_d", "ssd_norm_g", "fox_f_bias", "sconv_w",
             "ffn2_w_in", "ffn2_w_out", "ln_g", "ln_b")
    w_loc = dict(zip(names, (ln_in_g, ln_in_b, ada_w, ada_b, ffn1_w_in, ffn1_w_out, mix_w_in, mix_w_out, ssd_conv_w,
                             ssd_conv_b, ssd_dt_bias, ssd_a_log, ssd_d, ssd_norm_g, fox_f_bias, sconv_w, ffn2_w_in,
                             ffn2_w_out, ln_g, ln_b)))
    m_loc = dict(zip(names, (m_ln_in_g, m_ln_in_b, m_ada_w, m_ada_b, m_ffn1_w_in, m_ffn1_w_out, m_mix_w_in,
                             m_mix_w_out, m_ssd_conv_w, m_ssd_conv_b, m_ssd_dt_bias, m_ssd_a_log, m_ssd_d,
                             m_ssd_norm_g, m_fox_f_bias, m_sconv_w, m_ffn2_w_in, m_ffn2_w_out, m_ln_g, m_ln_b)))
    v_loc = dict(zip(names, (v_ln_in_g, v_ln_in_b, v_ada_w, v_ada_b, v_ffn1_w_in, v_ffn1_w_out, v_mix_w_in,
                             v_mix_w_out, v_ssd_conv_w, v_ssd_conv_b, v_ssd_dt_bias, v_ssd_a_log, v_ssd_d,
                             v_ssd_norm_g, v_fox_f_bias, v_sconv_w, v_ffn2_w_in, v_ffn2_w_out, v_ln_g, v_ln_b)))

    xi, yi, ci = lax.axis_index("x"), lax.axis_index("y"), lax.axis_index("c")
    me = 2 * xi + yi
    dev = 2 * me + ci

    def gather8(v, tag):
        v2 = v.reshape(1, -1)
        got = _bcast_chips([v2], f"gather_chips_{tag}")
        same_c = jnp.concatenate(_by_chip(me, v2, [g[0] for g in got]), axis=0)
        other_c = _swap_sibling([same_c], f"gather_sibling_{tag}")[0]
        pair = lax.switch(ci, [lambda a, b: jnp.stack([a, b], axis=1), lambda a, b: jnp.stack([b, a], axis=1)],
                          same_c, other_c)
        return pair.reshape(8, -1)

    def chip_concat(own, got3, axis):
        return jnp.concatenate(_by_chip(me, own, got3), axis=axis)

    small_cols = [w_loc[n].reshape(-1, w_loc[n].shape[-1]) for n in COL_SHARDED_SMALL]
    got = _bcast_chips(small_cols, "gather_small_params")
    small_p = {n: w_loc[n] for n in ("ln_in_g", "ln_in_b", "ssd_conv_b", "ssd_dt_bias", "ssd_a_log", "ssd_d",
                                     "ssd_norm_g", "fox_f_bias")}
    for i, n in enumerate(COL_SHARDED_SMALL):
        full = chip_concat(small_cols[i], [g[i] for g in got], 1)
        small_p[n] = full.reshape(w_loc[n].shape[:-1] + (full.shape[-1],))

    assert DEPTH == 2
    first, rest = BIG[:2], BIG[2:]
    shard_of = {n: w_loc[n].astype(BF16) for n in BIG}

    def gather(names, layer, after, tag, sequencer_id):
        out = _gather_weights(names, [shard_of[n] for n in names], layer, after, f"gather_weights_{tag}", sequencer_id)
        return dict(zip(names, out))

    g0a = gather(first, 0, [], "l0a", 12)
    g0b = gather(rest, 0, [g0a[first[1]]], "l0b", 1)
    wfull = [{**g0a, **g0b}, lambda marker: gather(BIG, 1, [marker], "l1", 5)]

    c_all = gather8(c[0], "c")
    c_act = _silu_bf16(c_all)
    ada_w_b = ada_w.astype(BF16)
    mod_loc = [_matmul(c_act, ada_w_b[l], "nn", F32, f"ada_fwd_l{l}") for l in range(DEPTH)]
    mod_loc = jnp.stack(mod_loc)
    got = _bcast_chips([mod_loc], "gather_mod")
    mod_all = chip_concat(mod_loc, [g[0] for g in got], 2)
    mod = lax.dynamic_index_in_dim(mod_all, dev, 1, keepdims=False) + ada_b
    mod = mod.reshape(DEPTH, N_SUB, 3, D)

    ci_arr = jnp.reshape(ci, (1,)).astype(jnp.int32)
    me_arr = jnp.reshape(me, (1,)).astype(jnp.int32)

    def wire_form(n, g):
        if n in BY_COLUMNS:
            return g
        if n == "mix_w_in":
            return jnp.transpose(g.reshape(g.shape[0], 4, g.shape[1] // 4), (1, 0, 2))
        return g.reshape(4, g.shape[0] // 4, g.shape[1])

    units, reduced = {}, {}

    def start_unit(tag, l, names, g, ids):
        wire = [wire_form(n, g[n]) for n in names]
        units[tag] = dict(l=l, names=names, wire=wire, ids=ids,
                          got=_grad_swap_halves(names, wire, f"grad_swap_halves_{tag}", ids[0]))

    def add_and_scatter(tag, after):
        st = units[tag]
        st["part"] = [_grad_add_halves(st["wire"][i], st["got"][i], ci_arr, after, f"grad_add_halves_{tag}_{n}")
                      for i, n in enumerate(st["names"])]
        st["from_chips"] = _grad_scatter_halves(st["names"], st["part"], f"grad_scatter_halves_{tag}", st["ids"][1])

    def finish_unit(tag, after):
        st = units[tag]
        halves = [_grad_add_slots(st["part"][i], [g[i] for g in st["from_chips"]], n, me_arr, after,
                                  f"grad_add_slots_{tag}_{n}") for i, n in enumerate(st["names"])]
        other = _grad_swap_reduced(halves, f"grad_swap_reduced_{tag}", st["ids"][2])
        for i, n in enumerate(st["names"]):
            reduced[(st["l"], n)] = (halves[i], other[i])

    second, third = BIG[2:4], BIG[4:]

    def after_sub_backward(l, j, g, marker):
        if (l, j) == (1, 0):
            start_unit("l1", 1, BIG, g, (2, 3, 4))
        elif (l, j) == (0, 2):
            start_unit("l0f2", 0, third, g, (6, 7, 8))
        elif (l, j) == (0, 1):
            start_unit("l0mx", 0, second, g, (9, 10, 11))

    def before_sub_backward(l, j, marker):
        if (l, j) == (0, 2):
            add_and_scatter("l1", marker)
        elif (l, j) == (0, 1):
            add_and_scatter("l0f2", marker)
        elif (l, j) == (0, 0):
            add_and_scatter("l0mx", marker)

    loss_part, dx, big_g, small_g, dmod, acc_in = _local_step(
        x[0], loss_target[0], mod, wfull, small_p, before_sub_backward, after_sub_backward)
    loss = lax.psum(loss_part, ("x", "y", "c"))

    pieces = [dmod.reshape(-1), acc_in[0], acc_in[1]]
    for n in SMALL_ORDER:
        pieces.append(jnp.stack([small_g[l][SMALL_KEY[n]] for l in range(DEPTH)]).reshape(-1))
    sizes = [p.shape[0] for p in pieces]
    total = sum(sizes)
    padded = -(-total // 1024) * 1024
    vec = _pad_to(jnp.concatenate(pieces), padded)
    all_rows = gather8(vec, "small_grads")
    summed = _sum_rows(all_rows, "sum_small_grads")[0]
    offs = [0]
    for s in sizes:
        offs.append(offs[-1] + s)
    n_mod = sizes[0]
    grads = {"ada_b": summed[0:n_mod].reshape(DEPTH, 3 * N_SUB * D),
             "ln_in_g": summed[offs[1]:offs[2]], "ln_in_b": summed[offs[2]:offs[3]]}
    for i, n in enumerate(SMALL_ORDER):
        full = summed[offs[3 + i]:offs[4 + i]].reshape(small_p[n].shape)
        if n in COL_SHARDED_SMALL:
            wcols = w_loc[n].shape[-1]
            full = lax.dynamic_slice_in_dim(full, me * wcols, wcols, axis=full.ndim - 1)
        grads[n] = full

    dmod_all = all_rows[:, 0:n_mod].reshape(8, DEPTH, 3 * N_SUB * D)
    ncol = ada_w.shape[-1]
    dmod_cols = lax.dynamic_slice_in_dim(dmod_all, me * ncol, ncol, axis=2).astype(BF16)
    grads["ada_w"] = jnp.stack([_matmul(c_act, dmod_cols[:, l], "tn", F32, f"ada_bwd_l{l}") for l in range(DEPTH)])

    finish_unit("l1", dx)
    finish_unit("l0f2", dx)
    finish_unit("l0mx", dx)
    start_unit("l0f1", 0, first, big_g[0], (None, None, None))
    add_and_scatter("l0f1", dx)
    finish_unit("l0f1", dx)

    delta, new_m, new_v = {}, {}, {}
    delta["ada_w"], new_m["ada_w"], new_v["ada_w"] = _adamw(w_loc["ada_w"], grads["ada_w"], m_loc["ada_w"],
                                                           v_loc["ada_w"], "adamw_ada_w")
    for i, n in enumerate(BIG):
        grads[n], delta[n], new_m[n], new_v[n] = _adamw_layers(
            w_loc[n], [reduced[(l, n)][0] for l in range(DEPTH)], [reduced[(l, n)][1] for l in range(DEPTH)],
            m_loc[n], v_loc[n], ci_arr, f"adamw_{n}")
    small_names = [n for n in names if n not in ("ada_w",) + BIG]
    flat = lambda d: jnp.concatenate([d[n].reshape(-1) for n in small_names])
    n_small = sum(w_loc[n].size for n in small_names)
    n_pad = -(-n_small // 1024) * 1024
    packed = [_pad_to(flat(d), n_pad).reshape(-1, LANES) for d in (w_loc, grads, m_loc, v_loc)]
    d_s, m_s, v_s = _adamw(*packed, "adamw_small")
    off = 0
    for n in small_names:
        sz = w_loc[n].size
        delta[n] = d_s.reshape(-1)[off:off + sz].reshape(w_loc[n].shape)
        new_m[n] = m_s.reshape(-1)[off:off + sz].reshape(w_loc[n].shape)
        new_v[n] = v_s.reshape(-1)[off:off + sz].reshape(w_loc[n].shape)
        off += sz

    return (loss, dx[None], *[grads[n] for n in names], *[delta[n] for n in names],
            *[new_m[n] for n in names], *[new_v[n] for n in names])
```

```python
import functools

import jax
import jax.numpy as jnp
from jax import lax
from jax.experimental import pallas as pl
from jax.experimental.pallas import tpu as pltpu
from jax.experimental.pallas import tpu_sc as plsc

F32 = jnp.float32
BF16 = jnp.bfloat16
MESH = pl.DeviceIdType.MESH

D = 1024
DEPTH = 2
N_SUB = 3
D_FF = 2816
FF_TILE = D_FF // 2
ALPHA = (2 * DEPTH) ** 0.25
LN_EPS = 1e-5
RMS_EPS = 1e-5
SSD_W = 512
SSD_HEADS = 8
SSD_CONV_K = 4
SSD_CONV_DIM = 1024
FOX_W = 256
FOX_HEADS = 4
FOX_HD = 64
FOX_SCALE = FOX_HD ** -0.5
SC_W = 256
SC_K = 3
D_IN_PROJ = 3084
P_Z, P_XBC, P_QKV, P_SC, P_PAD = 0, 512, 1536, 2304, 3072
D_PROJ_PAD = 3200
PAD_DT0, PAD_F0 = 0, 8
SSD_CHUNK = 256
ATT_BLOCK = 512
CUM_BLOCK = 256
LANES = 128
VMEM_LIMIT = 56 * 1024 * 1024

ADAM_LR, ADAM_B1, ADAM_B2, ADAM_EPS, ADAM_WD, ADAM_STEP = 0.001, 0.9, 0.999, 1e-08, 0.01, 10
NEG = -1e30


def _cp(*sem):
    return pltpu.CompilerParams(dimension_semantics=sem, vmem_limit_bytes=VMEM_LIMIT)


def _pick(n, cands):
    for c in cands:
        if n % c == 0:
            return c
    return n


def _dot(a, b):
    return lax.dot_general(a, b, (((1,), (0,)), ((), ())), preferred_element_type=F32)


def _dot_nt(a, b):
    return lax.dot_general(a, b, (((1,), (1,)), ((), ())), preferred_element_type=F32)


def _dot_tn(a, b):
    return lax.dot_general(a, b, (((0,), (0,)), ((), ())), preferred_element_type=F32)


def _sigmoid(x):
    return 0.5 * jnp.tanh(0.5 * x) + 0.5


def _softplus(x):
    return jnp.maximum(x, 0.0) + jnp.log(1.0 + jnp.exp(-jnp.abs(x)))


def _split3(v):
    h = v.astype(BF16)
    r = v - h.astype(F32)
    m = r.astype(BF16)
    l = (r - m.astype(F32)).astype(BF16)
    return h, m, l


def _tri_left(tri, v):
    h, m, l = _split3(v)
    return _dot(tri, h) + _dot(tri, m) + _dot(tri, l)


def _tri_right(v, tri):
    h, m, l = _split3(v)
    return _dot(h, tri) + _dot(m, tri) + _dot(l, tri)


def _matmul(a, b, mode, out_dtype, name, halves=None):
    assert a.dtype == BF16 and b.dtype == BF16, (name, a.dtype, b.dtype)
    if halves == "a":
        assert mode == "nt" and a.shape[0] == 2 and a.shape[2] == D_FF
        m, k, n = a.shape[1], 2 * D_FF, b.shape[0]
    elif halves == "b":
        assert mode == "tn" and b.shape[0] == 2 and b.shape[2] == D_FF
        (k, m), n = a.shape, 2 * D_FF
    elif mode == "nn":
        (m, k), n = a.shape, b.shape[1]
    elif mode == "nt":
        (m, k), n = a.shape, b.shape[0]
    else:
        (k, m), n = a.shape, b.shape[1]
    tm = m if m <= 1024 else _pick(m, (1024, 1408, 512, 256, 128))
    tn = n if n <= 1024 else _pick(n, (1408, 640, 512, 256, 128))
    tk = k if k <= 1024 else _pick(k, (1408, 1024, 640, 512, 256, 128))
    nk = k // tk
    if mode == "nn":
        dn = (((1,), (0,)), ((), ()))
        a_spec = pl.BlockSpec((tm, tk), lambda i, j, kk: (i, kk))
        b_spec = pl.BlockSpec((tk, tn), lambda i, j, kk: (kk, j))
    elif mode == "nt":
        dn = (((1,), (1,)), ((), ()))
        a_spec = pl.BlockSpec((tm, tk), lambda i, j, kk: (i, kk))
        b_spec = pl.BlockSpec((tn, tk), lambda i, j, kk: (j, kk))
    else:
        dn = (((0,), (0,)), ((), ()))
        a_spec = pl.BlockSpec((tk, tm), lambda i, j, kk: (kk, i))
        b_spec = pl.BlockSpec((tk, tn), lambda i, j, kk: (kk, j))
    per_half = D_FF // FF_TILE
    if halves == "a":
        assert tk == FF_TILE
        a_spec = pl.BlockSpec((None, tm, tk), lambda i, j, kk: (kk // per_half, i, kk % per_half))
    elif halves == "b":
        assert tn == FF_TILE
        b_spec = pl.BlockSpec((None, tk, tn), lambda i, j, kk: (j // per_half, kk, j % per_half))

    def body(a_ref, b_ref, o_ref, *acc):
        d = lax.dot_general(a_ref[...], b_ref[...], dn, preferred_element_type=F32)
        if nk == 1:
            o_ref[...] = d.astype(o_ref.dtype)
            return
        acc_ref, = acc
        kk = pl.program_id(2)

        @pl.when(kk == 0)
        def _():
            acc_ref[...] = d

        @pl.when((kk > 0) & (kk < nk - 1))
        def _():
            acc_ref[...] += d

        @pl.when(kk == nk - 1)
        def _():
            o_ref[...] = (acc_ref[...] + d).astype(o_ref.dtype)

    return pl.pallas_call(
        body, name=name, grid=(m // tm, n // tn, nk),
        in_specs=[a_spec, b_spec],
        out_specs=pl.BlockSpec((tm, tn), lambda i, j, kk: (i, j)),
        out_shape=jax.ShapeDtypeStruct((m, n), out_dtype),
        scratch_shapes=[pltpu.VMEM((tm, tn), F32)] if nk > 1 else [],
        compiler_params=_cp("parallel", "parallel", "arbitrary"),
    )(a, b)


def _rows(body, name, t, tb, row_in, full_in, row_out, acc_out):
    in_specs, args = [], []
    for r in row_in:
        if isinstance(r, tuple):
            arr, w, j = r
            in_specs.append(pl.BlockSpec((tb, w), functools.partial(lambda i, jj: (i, jj), jj=j)))
            args.append(arr)
        else:
            in_specs.append(pl.BlockSpec((tb, r.shape[1]), lambda i: (i, 0)))
            args.append(r)
    for f in full_in:
        in_specs.append(pl.BlockSpec(f.shape, functools.partial(lambda i, nd: (0,) * nd, nd=f.ndim)))
        args.append(f)
    out_specs = [pl.BlockSpec((tb, c), lambda i: (i, 0)) for c, _ in row_out]
    out_specs += [pl.BlockSpec(s, functools.partial(lambda i, nd: (0,) * nd, nd=len(s))) for s in acc_out]
    out_shape = [jax.ShapeDtypeStruct((t, c), dt) for c, dt in row_out]
    out_shape += [jax.ShapeDtypeStruct(s, F32) for s in acc_out]
    return pl.pallas_call(
        body, name=name, grid=(t // tb,), in_specs=in_specs, out_specs=out_specs, out_shape=out_shape,
        compiler_params=_cp("arbitrary"),
    )(*args)


def _ln_stats(r):
    mu = jnp.mean(r, axis=-1, keepdims=True)
    xc = r - mu
    var = jnp.mean(xc * xc, axis=-1, keepdims=True)
    rstd = lax.rsqrt(var + LN_EPS)
    return xc * rstd, rstd


def _ln_bwd(dout, xhat, rstd, g):
    dxh = dout * g
    m1 = jnp.mean(dxh, axis=-1, keepdims=True)
    m2 = jnp.mean(dxh * xhat, axis=-1, keepdims=True)
    return rstd * (dxh - m1 - xhat * m2)


def _first(i, acc_refs):
    @pl.when(i == 0)
    def _():
        for a in acc_refs:
            a[...] = jnp.zeros_like(a)


def _modulated(xv, m_ref):
    return (xv * (1.0 + m_ref[1:2, :]) + m_ref[0:1, :]).astype(BF16)


def _ln_in_fwd(x, g, b, next_mod3):
    t = x.shape[0]

    def body(x_ref, g_ref, b_ref, m_ref, o_ref, h_ref):
        xhat, _ = _ln_stats(x_ref[...])
        out = xhat * g_ref[...] + b_ref[...]
        o_ref[...] = out
        h_ref[...] = _modulated(out, m_ref)

    return _rows(body, "ln_in_fwd", t, 256, [x], [g, b, next_mod3], [(D, F32), (D, BF16)], [])


def _ln_in_bwd(dx0, x, g):
    t = x.shape[0]

    def body(d_ref, x_ref, g_ref, o_ref, acc_ref):
        _first(pl.program_id(0), [acc_ref])
        xhat, rstd = _ln_stats(x_ref[...])
        d = d_ref[...]
        o_ref[...] = _ln_bwd(d, xhat, rstd, g_ref[...])
        acc_ref[0:1, :] += jnp.sum(d * xhat, axis=0, keepdims=True)
        acc_ref[1:2, :] += jnp.sum(d, axis=0, keepdims=True)

    return _rows(body, "ln_in_bwd", t, 256, [dx0, x], [g], [(D, F32)], [(2, D)])


def _modulate_bwd(dxres, dh, xin, mod3, name):
    t = xin.shape[0]

    def body(r_ref, dh_ref, x_ref, m_ref, o_ref, acc_ref):
        _first(pl.program_id(0), [acc_ref])
        dh_v = dh_ref[...]
        o_ref[...] = r_ref[...] + dh_v * (1.0 + m_ref[1:2, :])
        acc_ref[0:1, :] += jnp.sum(dh_v, axis=0, keepdims=True)
        acc_ref[1:2, :] += jnp.sum(dh_v * x_ref[...], axis=0, keepdims=True)

    return _rows(body, name, t, 256, [dxres, dh, xin], [mod3], [(D, F32)], [(2, D)])


def _ffn_in_swiglu(h, w_in, name):
    t = h.shape[0]
    tm = _pick(t, (512, 256, 128))
    nj = D_FF // FF_TILE

    def body(h_ref, wg_ref, wu_ref, u_ref, a_ref):
        hv = h_ref[...]
        gate = _dot(hv, wg_ref[...])
        up = _dot(hv, wu_ref[...])
        sg = _sigmoid(gate)
        silu = gate * sg
        u_ref[0] = (up * (sg + silu * (1.0 - sg))).astype(BF16)
        u_ref[1] = silu.astype(BF16)
        a_ref[...] = (silu * up).astype(BF16)

    return pl.pallas_call(
        body, name=name, grid=(nj, t // tm),
        in_specs=[pl.BlockSpec((tm, D), lambda j, i: (i, 0)),
                  pl.BlockSpec((D, FF_TILE), lambda j, i: (0, j)),
                  pl.BlockSpec((D, FF_TILE), lambda j, i: (0, nj + j))],
        out_specs=[pl.BlockSpec((2, tm, FF_TILE), lambda j, i: (0, i, j)),
                   pl.BlockSpec((tm, FF_TILE), lambda j, i: (i, j))],
        out_shape=[jax.ShapeDtypeStruct((2, t, D_FF), BF16), jax.ShapeDtypeStruct((t, D_FF), BF16)],
        compiler_params=_cp("parallel", "parallel"),
    )(h, w_in, w_in)


def _ffn_out_dx_swiglu(dy, w_out, u, name):
    t = dy.shape[0]
    tm = _pick(t, (512, 256, 128))
    nj = D_FF // FF_TILE

    def body(dy_ref, w_ref, u_ref, du_ref):
        da = _dot_nt(dy_ref[...], w_ref[...])
        du_ref[0] = (da * u_ref[0].astype(F32)).astype(BF16)
        du_ref[1] = (da * u_ref[1].astype(F32)).astype(BF16)

    blk3 = pl.BlockSpec((2, tm, FF_TILE), lambda j, i: (0, i, j))
    return pl.pallas_call(
        body, name=name, grid=(nj, t // tm),
        in_specs=[pl.BlockSpec((tm, D), lambda j, i: (i, 0)),
                  pl.BlockSpec((FF_TILE, D), lambda j, i: (j, 0)), blk3],
        out_specs=blk3,
        out_shape=jax.ShapeDtypeStruct((2, t, D_FF), BF16),
        compiler_params=_cp("parallel", "parallel"),
    )(dy, w_out, u)


def _res_ln(xin, y, mod3, lg, lb, factor, name, next_mod3=None):
    t = xin.shape[0]

    def body(x_ref, y_ref, m_ref, g_ref, b_ref, *rest):
        r = ALPHA * x_ref[...] + (factor * m_ref[2:3, :]) * y_ref[...]
        xhat, _ = _ln_stats(r)
        out = xhat * g_ref[...] + b_ref[...]
        if next_mod3 is None:
            rest[0][...] = out
        else:
            rest[1][...] = out
            rest[2][...] = _modulated(out, rest[0])

    if next_mod3 is None:
        return _rows(body, name, t, 256, [xin, y], [mod3, lg, lb], [(D, F32)], [])[0], None
    return _rows(body, name, t, 256, [xin, y], [mod3, lg, lb, next_mod3], [(D, F32), (D, BF16)], [])


def _mod_res_bwd(dxres, dh, mod3, xin_p, y_p, mod3_p, lg_p, lb_p, factor_p, name):
    t = dxres.shape[0]

    def body(r_ref, dh_ref, xp_ref, yp_ref, m_ref, mp_ref, g_ref, b_ref, dres_ref, dy_ref, acc_ref):
        _first(pl.program_id(0), [acc_ref])
        gate = factor_p * mp_ref[2:3, :]
        yv = yp_ref[...]
        xhat, rstd = _ln_stats(ALPHA * xp_ref[...] + gate * yv)
        xin = xhat * g_ref[...] + b_ref[...]
        dh_v = dh_ref[...]
        d = r_ref[...] + dh_v * (1.0 + m_ref[1:2, :])
        dr = _ln_bwd(d, xhat, rstd, g_ref[...])
        dres_ref[...] = ALPHA * dr
        dy_ref[...] = (gate * dr).astype(BF16)
        acc_ref[0:1, :] += jnp.sum(dh_v, axis=0, keepdims=True)
        acc_ref[1:2, :] += jnp.sum(dh_v * xin, axis=0, keepdims=True)
        acc_ref[2:3, :] += jnp.sum(d * xhat, axis=0, keepdims=True)
        acc_ref[3:4, :] += jnp.sum(d, axis=0, keepdims=True)
        acc_ref[4:5, :] += jnp.sum(factor_p * yv * dr, axis=0, keepdims=True)

    return _rows(body, name, t, 256, [dxres, dh, xin_p, y_p], [mod3, mod3_p, lg_p, lb_p],
                 [(D, F32), (D, BF16)], [(5, D)])


def _res_ln_bwd(dout, xin, y, mod3, lg, factor, name):
    t = xin.shape[0]

    def body(d_ref, x_ref, y_ref, m_ref, g_ref, dres_ref, dy_ref, acc_ref):
        _first(pl.program_id(0), [acc_ref])
        gate = factor * m_ref[2:3, :]
        yv = y_ref[...]
        r = ALPHA * x_ref[...] + gate * yv
        xhat, rstd = _ln_stats(r)
        d = d_ref[...]
        dr = _ln_bwd(d, xhat, rstd, g_ref[...])
        dres_ref[...] = ALPHA * dr
        dy_ref[...] = (gate * dr).astype(BF16)
        acc_ref[0:1, :] += jnp.sum(d * xhat, axis=0, keepdims=True)
        acc_ref[1:2, :] += jnp.sum(d, axis=0, keepdims=True)
        acc_ref[2:3, :] += jnp.sum(factor * yv * dr, axis=0, keepdims=True)

    return _rows(body, name, t, 256, [dout, xin, y], [mod3, lg], [(D, F32), (D, BF16)], [(3, D)])


def _loss_head(xf, tgt):
    t = xf.shape[0]

    def body(x_ref, t_ref, d_ref, acc_ref):
        _first(pl.program_id(0), [acc_ref])
        e = x_ref[...] - t_ref[...]
        d_ref[...] = e * (1.0 / D)
        part = 0.5 * jnp.sum(jnp.mean(e * e, axis=-1, keepdims=True), axis=0, keepdims=True)
        acc_ref[...] += jnp.broadcast_to(part, acc_ref.shape)

    return _rows(body, "loss_head", t, 256, [xf, tgt], [], [(D, F32)], [(1, LANES)])


def _silu_bf16(c_all):
    def body(c_ref, o_ref):
        v = c_ref[...]
        o_ref[...] = (v * _sigmoid(v)).astype(BF16)

    return _rows(body, "silu_c", c_all.shape[0], c_all.shape[0], [c_all], [], [(c_all.shape[1], BF16)], [])[0]


def _sum_rows(v, name):
    r, n = v.shape
    tn = _pick(n, (8192, 4096, 2048, 1024, 512, 256, 128))

    def body(v_ref, o_ref):
        acc = v_ref[0:1, :]
        for k in range(1, r):
            acc = acc + v_ref[k:k + 1, :]
        o_ref[...] = acc

    return pl.pallas_call(
        body, name=name, grid=(n // tn,),
        in_specs=[pl.BlockSpec((r, tn), lambda j: (0, j))],
        out_specs=pl.BlockSpec((1, tn), lambda j: (0, j)),
        out_shape=jax.ShapeDtypeStruct((1, n), F32),
        compiler_params=_cp("parallel"),
    )(v)


def _elementwise(fn, name, ins, out_dtypes):
    r, c = ins[0].shape
    tb = _pick(r, (128, 64, 32, 16, 8))
    n_in = len(ins)

    def body(*refs):
        outs = fn(*[x[...] for x in refs[:n_in]])
        for o_ref, o in zip(refs[n_in:], outs):
            o_ref[...] = o.astype(o_ref.dtype)

    spec = pl.BlockSpec((tb, c), lambda i: (i, 0))
    return pl.pallas_call(
        body, name=name, grid=(r // tb,), in_specs=[spec] * n_in, out_specs=[spec] * len(out_dtypes),
        out_shape=[jax.ShapeDtypeStruct((r, c), dt) for dt in out_dtypes],
        compiler_params=_cp("parallel"),
    )(*ins)


def _adamw_math(w, g, m, v):
    m = ADAM_B1 * m + (1.0 - ADAM_B1) * g
    v = ADAM_B2 * v + (1.0 - ADAM_B2) * (g * g)
    m_hat = m / (1.0 - ADAM_B1 ** ADAM_STEP)
    v_hat = v / (1.0 - ADAM_B2 ** ADAM_STEP)
    delta = -ADAM_LR * (m_hat / (jnp.sqrt(v_hat) + ADAM_EPS) + ADAM_WD * w)
    return delta, m, v


def _adamw(w, g, m, v, name):
    shape = w.shape
    c = shape[-1]
    w2, g2, m2, v2 = (a.reshape(-1, c) for a in (w, g, m, v))
    outs = _elementwise(_adamw_math, name, [w2, g2, m2, v2], [F32, F32, F32])
    return tuple(o.reshape(shape) for o in outs)


def _remote_exchange(ins, plan, peers_of, name):
    n_in, n_out = len(ins), len(plan)

    def body(*refs):
        in_refs, out_refs = refs[:n_in], refs[n_in:n_in + n_out]
        send_sems, recv_sems = refs[n_in + n_out], refs[n_in + n_out + 1]
        peers = peers_of(lax.axis_index("x"), lax.axis_index("y"), lax.axis_index("c"))
        copies = [
            pltpu.make_async_remote_copy(
                src_ref=in_refs[src], dst_ref=out_refs[k], send_sem=send_sems.at[k], recv_sem=recv_sems.at[k],
                device_id=peers[peer], device_id_type=MESH)
            for k, (peer, src) in enumerate(plan)
        ]
        for cp in copies:
            cp.start()
        for cp in copies:
            cp.wait()

    any_spec = pl.BlockSpec(memory_space=pl.ANY)
    return list(pl.pallas_call(
        body, name=name,
        in_specs=[any_spec] * n_in, out_specs=[any_spec] * n_out,
        out_shape=[jax.ShapeDtypeStruct(ins[src].shape, ins[src].dtype) for _, src in plan],
        scratch_shapes=[pltpu.SemaphoreType.DMA((n_out,)), pltpu.SemaphoreType.DMA((n_out,))],
    )(*ins))


def _sibling(x, y, c):
    return [(x, y, 1 - c)]


def _other_chips(x, y, c):
    return [(1 - x, y, c), (x, 1 - y, c), (1 - x, 1 - y, c)]


def _swap_sibling(arrs, name):
    return _remote_exchange(arrs, [(0, i) for i in range(len(arrs))], _sibling, name)


def _bcast_chips(arrs, name):
    n = len(arrs)
    out = _remote_exchange(arrs, [(p, i) for p in range(3) for i in range(n)], _other_chips, name)
    return [out[p * n:(p + 1) * n] for p in range(3)]


def _by_chip(me, own, got3):
    by_rel = [own, got3[0], got3[1], got3[2]]
    rel_bits = (0, 2, 1, 3)

    def branch(m):
        def f(ops):
            return [ops[rel_bits.index(i ^ m)] for i in range(4)]
        return f

    return lax.switch(me, [branch(m) for m in range(4)], by_rel)


BIG = ("ffn1_w_in", "ffn1_w_out", "mix_w_in", "mix_w_out", "ffn2_w_in", "ffn2_w_out")
BY_COLUMNS = ("ffn1_w_in", "ffn2_w_in")


def _layer_shape(n, shard_shape):
    r, cs = shard_shape
    return (r, 4 * cs) if n in BY_COLUMNS else (4, r, cs)


def _chip_ids(x, y):
    chips = [(1 - x, y), (x, 1 - y), (1 - x, 1 - y)]
    return chips, [2 * cx + cy for cx, cy in chips]


def _half_slot(ref, n, chip, h):
    if n in BY_COLUMNS:
        hr, w = ref.shape[0] // 2, ref.shape[1] // 4
        return ref.at[pl.ds(pl.multiple_of(h * hr, 16), hr), pl.ds(pl.multiple_of(chip * w, LANES), w)]
    hr = ref.shape[1] // 2
    return ref.at[chip, pl.ds(pl.multiple_of(h * hr, 16), hr)]


def _gather_body(names, layer, in_refs, out_refs, sems, handshake):
    send_chip, recv_chip, send_sib, recv_sib, local, send_fwd, recv_fwd = sems
    n_w = len(names)
    x, y, c = lax.axis_index("x"), lax.axis_index("y"), lax.axis_index("c")
    me = 2 * x + y
    chips, chip_idx = _chip_ids(x, y)
    sibling = (x, y, 1 - c)
    if handshake:
        _shake_hands([sibling] + [(*ch, c) for ch in chips])
    remote = _remote

    sends, own_copies = [], []
    for i, n in enumerate(names):
        shard = in_refs[i].at[layer]
        hr = shard.shape[0] // 2
        src = shard.at[pl.ds(pl.multiple_of(c * hr, 16), hr)]
        mine = _half_slot(out_refs[i], n, me, c)
        own = pltpu.make_async_copy(src, mine, local.at[i])
        own.start()
        own_copies.append(own)
        sends.append(remote(src, mine, send_sib.at[i], recv_sib.at[i], sibling))
        sends[-1].start()
        for p in range(3):
            k = p * n_w + i
            sends.append(remote(src, mine, send_chip.at[k], recv_chip.at[k], (*chips[p], c)))
            sends[-1].start()
    for p in range(3):
        for i, n in enumerate(names):
            k = p * n_w + i
            landed = _half_slot(out_refs[i], n, chip_idx[p], c)
            remote(landed, landed, send_chip.at[k], recv_chip.at[k], sibling).wait_recv()
            sends.append(remote(landed, landed, send_fwd.at[k], recv_fwd.at[k], sibling))
            sends[-1].start()
    for i, n in enumerate(names):
        theirs = _half_slot(out_refs[i], n, me, 1 - c)
        remote(theirs, theirs, send_sib.at[i], recv_sib.at[i], sibling).wait_recv()
        for p in range(3):
            k = p * n_w + i
            theirs = _half_slot(out_refs[i], n, chip_idx[p], 1 - c)
            remote(theirs, theirs, send_fwd.at[k], recv_fwd.at[k], sibling).wait_recv()
    for own in own_copies:
        own.wait()
    for cp in sends:
        cp.wait_send()


def _gather_weights(names, shards, layer, after, name, sequencer_id):
    n_w = len(names)

    def body(in_refs, out_refs, sems, handshake):
        _gather_body(names, layer, in_refs, out_refs, sems, handshake)

    dma = pltpu.SemaphoreType.DMA
    sems = [dma((3 * n_w,)), dma((3 * n_w,)), dma((n_w,)), dma((n_w,)), dma((n_w,)), dma((3 * n_w,)), dma((3 * n_w,))]
    shapes = [jax.ShapeDtypeStruct(_layer_shape(n, s.shape[1:]), BF16) for n, s in zip(names, shards)]
    return _comm_call(body, list(shards) + list(after), shapes, sems, name, sequencer_id)


def _comm_call(body_fn, ins, out_shapes, sem_types, name, sequencer_id):
    n_in, n_out = len(ins), len(out_shapes)
    sequencer = sequencer_id is not None

    def body(*refs):
        body_fn(refs[:n_in], refs[n_in:n_in + n_out], refs[n_in + n_out:], sequencer)

    if sequencer:
        return list(pl.kernel(
            body, name=name, out_type=out_shapes,
            mesh=plsc.ScalarSubcoreMesh(axis_name="sequencer", num_cores=1), scratch_types=sem_types,
            compiler_params=pltpu.CompilerParams(collective_id=sequencer_id),
        )(*ins))
    any_spec = pl.BlockSpec(memory_space=pl.ANY)
    return list(pl.pallas_call(
        body, name=name, in_specs=[any_spec] * n_in, out_specs=[any_spec] * n_out, out_shape=out_shapes,
        scratch_shapes=sem_types,
    )(*ins))


def _shake_hands(peers):
    barrier = pltpu.get_barrier_semaphore()
    for peer in peers:
        pl.semaphore_signal(barrier, inc=1, device_id=peer, device_id_type=MESH)
    pl.semaphore_wait(barrier, len(peers))


def _remote(src, dst, s_sem, r_sem, to):
    return pltpu.make_async_remote_copy(src_ref=src, dst_ref=dst, send_sem=s_sem, recv_sem=r_sem,
                                        device_id=to, device_id_type=MESH)


def _rows_half(ref, n, h):
    if n in BY_COLUMNS:
        hr = ref.shape[0] // 2
        return ref.at[pl.ds(pl.multiple_of(h * hr, 16), hr)]
    hr = ref.shape[1] // 2
    return ref.at[:, pl.ds(pl.multiple_of(h * hr, 16), hr)]


def _half_form_shape(n, wire_shape):
    if n in BY_COLUMNS:
        return (wire_shape[0] // 2, wire_shape[1])
    return (wire_shape[0], wire_shape[1] // 2, wire_shape[2])


def _grad_swap_halves(names, wire, name, sequencer_id):
    n_w = len(names)

    def body(in_refs, out_refs, sems, handshake):
        send, recv = sems
        x, y, c = lax.axis_index("x"), lax.axis_index("y"), lax.axis_index("c")
        sibling = (x, y, 1 - c)
        if handshake:
            _shake_hands([sibling])
        cps = [_remote(_rows_half(in_refs[i], n, 1 - c), out_refs[i], send.at[i], recv.at[i], sibling)
               for i, n in enumerate(names)]
        for cp in cps:
            cp.start()
        for cp in cps:
            cp.wait()

    shapes = [jax.ShapeDtypeStruct(_half_form_shape(n, w.shape), w.dtype) for n, w in zip(names, wire)]
    dma = pltpu.SemaphoreType.DMA
    return _comm_call(body, list(wire), shapes, [dma((n_w,)), dma((n_w,))], name, sequencer_id)


def _grad_add_halves(g, got, ci_arr, after, name):
    g3 = g if g.ndim == 3 else g[None]
    r3 = got if got.ndim == 3 else got[None]
    s, hr, cc = r3.shape
    tb = _pick(hr, (256, 176, 128))
    nb = hr // tb

    def body(s_ref, g_ref, r_ref, after_ref, o_ref):
        o_ref[...] = (g_ref[...].astype(F32) + r_ref[...].astype(F32)).astype(BF16)

    out = pl.pallas_call(
        body, name=name,
        grid_spec=pltpu.PrefetchScalarGridSpec(
            num_scalar_prefetch=1, grid=(nb,),
            in_specs=[pl.BlockSpec((s, tb, cc), lambda i, sc: (0, sc[0] * nb + i, 0)),
                      pl.BlockSpec((s, tb, cc), lambda i, sc: (0, i, 0)),
                      pl.BlockSpec(memory_space=pl.ANY)],
            out_specs=pl.BlockSpec((s, tb, cc), lambda i, sc: (0, i, 0))),
        out_shape=jax.ShapeDtypeStruct(r3.shape, BF16),
        compiler_params=_cp("arbitrary"),
    )(ci_arr, g3, r3, after)
    return out.reshape(got.shape)


def _grad_scatter_halves(names, parts, name, sequencer_id):
    n_w = len(names)

    def slot(ref, n, chip):
        if n in BY_COLUMNS:
            w = ref.shape[1] // 4
            return ref.at[:, pl.ds(pl.multiple_of(chip * w, LANES), w)]
        return ref.at[chip]

    def body(in_refs, out_refs, sems, handshake):
        send, recv = sems
        x, y, c = lax.axis_index("x"), lax.axis_index("y"), lax.axis_index("c")
        chips, chip_idx = _chip_ids(x, y)
        if handshake:
            _shake_hands([(*ch, c) for ch in chips])
        cps = []
        for p in range(3):
            for i, n in enumerate(names):
                k = p * n_w + i
                cps.append(_remote(slot(in_refs[i], n, chip_idx[p]), out_refs[k], send.at[k], recv.at[k],
                                   (*chips[p], c)))
        for cp in cps:
            cp.start()
        for cp in cps:
            cp.wait()

    def slot_shape(n, a):
        return (a.shape[0], a.shape[1] // 4) if n in BY_COLUMNS else a.shape[1:]

    shapes = [jax.ShapeDtypeStruct(slot_shape(n, a), BF16) for _ in range(3) for n, a in zip(names, parts)]
    dma = pltpu.SemaphoreType.DMA
    out = _comm_call(body, list(parts), shapes, [dma((3 * n_w,)), dma((3 * n_w,))], name, sequencer_id)
    return [out[p * n_w:(p + 1) * n_w] for p in range(3)]


def _grad_add_slots(part, got3, n, me_arr, after, name):
    hr, cs = got3[0].shape
    tb = _pick(hr, (256, 176, 128))

    def body(s_ref, own_ref, a_ref, b_ref, c_ref, after_ref, out_ref):
        acc = own_ref[...].astype(F32) + a_ref[...].astype(F32)
        out_ref[...] = (acc + b_ref[...].astype(F32)) + c_ref[...].astype(F32)

    if n in BY_COLUMNS:
        own_spec = pl.BlockSpec((tb, cs), lambda i, s: (i, s[0]))
    else:
        own_spec = pl.BlockSpec((None, tb, cs), lambda i, s: (s[0], i, 0))
    plain = pl.BlockSpec((tb, cs), lambda i, s: (i, 0))
    return pl.pallas_call(
        body, name=name,
        grid_spec=pltpu.PrefetchScalarGridSpec(
            num_scalar_prefetch=1, grid=(hr // tb,),
            in_specs=[own_spec, plain, plain, plain, pl.BlockSpec(memory_space=pl.ANY)],
            out_specs=plain),
        out_shape=jax.ShapeDtypeStruct((hr, cs), F32),
        compiler_params=_cp("arbitrary"),
    )(me_arr, part, *got3, after)


def _grad_swap_reduced(halves, name, sequencer_id):
    n_w = len(halves)

    def body(in_refs, out_refs, sems, handshake):
        send, recv = sems
        x, y, c = lax.axis_index("x"), lax.axis_index("y"), lax.axis_index("c")
        sibling = (x, y, 1 - c)
        if handshake:
            _shake_hands([sibling])
        cps = [_remote(in_refs[i], out_refs[i], send.at[i], recv.at[i], sibling) for i in range(n_w)]
        for cp in cps:
            cp.start()
        for cp in cps:
            cp.wait()

    shapes = [jax.ShapeDtypeStruct(h.shape, F32) for h in halves]
    dma = pltpu.SemaphoreType.DMA
    return _comm_call(body, list(halves), shapes, [dma((n_w,)), dma((n_w,))], name, sequencer_id)


def _adamw_layers(w, own, other, m, v, ci_arr, name):
    _, rs, cs = w.shape
    hr = rs // 2
    tb = _pick(hr, (256, 176, 128))
    nbh = hr // tb

    def body(s_ref, w_ref, o0, t0, o1, t1, m_ref, v_ref, g_out, d_out, m_out, v_out):
        mine = (pl.program_id(1) // nbh) == s_ref[0]
        g = jnp.where(pl.program_id(0) == 0, jnp.where(mine, o0[...], t0[...]), jnp.where(mine, o1[...], t1[...]))
        g_out[...] = g
        d_out[...], m_out[...], v_out[...] = _adamw_math(w_ref[...], g, m_ref[...], v_ref[...])

    both = pl.BlockSpec((None, tb, cs), lambda l, i, s: (l, i, 0))

    def half(layer, is_own):
        def index(l, i, s):
            own_block = (i // nbh) == s[0]
            use = (l == layer) & (own_block if is_own else jnp.logical_not(own_block))
            return (jnp.where(use, i % nbh, 0), 0)
        return pl.BlockSpec((tb, cs), index)

    return pl.pallas_call(
        body, name=name,
        grid_spec=pltpu.PrefetchScalarGridSpec(
            num_scalar_prefetch=1, grid=(DEPTH, rs // tb),
            in_specs=[both, half(0, True), half(0, False), half(1, True), half(1, False), both, both],
            out_specs=[both] * 4),
        out_shape=[jax.ShapeDtypeStruct(w.shape, F32)] * 4,
        compiler_params=_cp("arbitrary", "arbitrary"),
    )(ci_arr, w, own[0], other[0], own[1], other[1], m, v)


def _shift_down(v, s, t_iota):
    return jnp.where(t_iota >= s, pltpu.roll(v, s, 0), 0.0)


def _shift_up(v, s, t_iota, t):
    return jnp.where(t_iota < t - s, pltpu.roll(v, t - s, 0), 0.0)


def _ssd_conv_fwd(proj, w, b):
    t = proj.shape[0]
    k_w = SSD_CONV_K

    def body(x_ref, w_ref, b_ref, o_ref):
        x = x_ref[...]
        ti = lax.broadcasted_iota(jnp.int32, x.shape, 0)
        pre = x * w_ref[k_w - 1:k_w, :] + b_ref[...]
        for s in range(1, k_w):
            pre = pre + _shift_down(x, s, ti) * w_ref[k_w - 1 - s:k_w - s, :]
        o_ref[...] = pre * _sigmoid(pre)

    off = P_XBC // LANES
    return pl.pallas_call(
        body, name="ssd_conv_fwd", grid=(SSD_CONV_DIM // LANES,),
        in_specs=[pl.BlockSpec((t, LANES), lambda j: (0, off + j)),
                  pl.BlockSpec((k_w, LANES), lambda j: (0, j)),
                  pl.BlockSpec((1, LANES), lambda j: (0, j))],
        out_specs=pl.BlockSpec((t, LANES), lambda j: (0, j)),
        out_shape=jax.ShapeDtypeStruct((t, SSD_CONV_DIM), F32),
        compiler_params=_cp("parallel"),
    )(proj, w, b)


def _ssd_conv_bwd(dxbc, proj, w, b):
    t = proj.shape[0]
    k_w = SSD_CONV_K

    def body(d_ref, x_ref, w_ref, b_ref, dx_ref, dw_ref, db_ref):
        x = x_ref[...]
        ti = lax.broadcasted_iota(jnp.int32, x.shape, 0)
        shifted = [x] + [_shift_down(x, s, ti) for s in range(1, k_w)]
        pre = b_ref[...] + shifted[0] * w_ref[k_w - 1:k_w, :]
        for s in range(1, k_w):
            pre = pre + shifted[s] * w_ref[k_w - 1 - s:k_w - s, :]
        sg = _sigmoid(pre)
        dpre = d_ref[...] * (sg * (1.0 + pre * (1.0 - sg)))
        db_ref[...] = jnp.sum(dpre, axis=0, keepdims=True)
        dx = dpre * w_ref[k_w - 1:k_w, :]
        for s in range(k_w):
            dw_ref[k_w - 1 - s:k_w - s, :] = jnp.sum(dpre * shifted[s], axis=0, keepdims=True)
            if s:
                dx = dx + _shift_up(dpre, s, ti, t) * w_ref[k_w - 1 - s:k_w - s, :]
        dx_ref[...] = dx

    off = P_XBC // LANES
    return pl.pallas_call(
        body, name="ssd_conv_bwd", grid=(SSD_CONV_DIM // LANES,),
        in_specs=[pl.BlockSpec((t, LANES), lambda j: (0, j)),
                  pl.BlockSpec((t, LANES), lambda j: (0, off + j)),
                  pl.BlockSpec((k_w, LANES), lambda j: (0, j)),
                  pl.BlockSpec((1, LANES), lambda j: (0, j))],
        out_specs=[pl.BlockSpec((t, LANES), lambda j: (0, j)),
                   pl.BlockSpec((k_w, LANES), lambda j: (0, j)),
                   pl.BlockSpec((1, LANES), lambda j: (0, j))],
        out_shape=[jax.ShapeDtypeStruct((t, SSD_CONV_DIM), F32),
                   jax.ShapeDtypeStruct((k_w, SSD_CONV_DIM), F32),
                   jax.ShapeDtypeStruct((1, SSD_CONV_DIM), F32)],
        compiler_params=_cp("parallel"),
    )(dxbc, proj, w, b)


def _shortconv_fwd(proj, w):
    t = proj.shape[0]
    nb = SC_W // LANES
    off = P_SC // LANES

    def body(b_ref, c_ref, x_ref, w_ref, o_ref):
        u = c_ref[...] * x_ref[...]
        ti = lax.broadcasted_iota(jnp.int32, u.shape, 0)
        cv = u * w_ref[SC_K - 1:SC_K, :]
        for s in range(1, SC_K):
            cv = cv + _shift_down(u, s, ti) * w_ref[SC_K - 1 - s:SC_K - s, :]
        o_ref[...] = (b_ref[...] * cv).astype(BF16)

    return pl.pallas_call(
        body, name="shortconv_fwd", grid=(nb,),
        in_specs=[pl.BlockSpec((t, LANES), lambda j: (0, off + j)),
                  pl.BlockSpec((t, LANES), lambda j: (0, off + nb + j)),
                  pl.BlockSpec((t, LANES), lambda j: (0, off + 2 * nb + j)),
                  pl.BlockSpec((SC_K, LANES), lambda j: (0, j))],
        out_specs=pl.BlockSpec((t, LANES), lambda j: (0, j)),
        out_shape=jax.ShapeDtypeStruct((t, SC_W), BF16),
        compiler_params=_cp("parallel"),
    )(proj, proj, proj, w)


def _shortconv_bwd(dy, dy_off, proj, w):
    t = proj.shape[0]
    nb = SC_W // LANES
    off = P_SC // LANES
    doff = dy_off // LANES

    def body(d_ref, b_ref, c_ref, x_ref, w_ref, db_ref, dc_ref, dx_ref, dw_ref):
        cg, xin = c_ref[...], x_ref[...]
        u = cg * xin
        ti = lax.broadcasted_iota(jnp.int32, u.shape, 0)
        shifted = [u] + [_shift_down(u, s, ti) for s in range(1, SC_K)]
        cv = shifted[0] * w_ref[SC_K - 1:SC_K, :]
        for s in range(1, SC_K):
            cv = cv + shifted[s] * w_ref[SC_K - 1 - s:SC_K - s, :]
        d = d_ref[...]
        db_ref[...] = (d * cv).astype(BF16)
        dcv = d * b_ref[...]
        du = dcv * w_ref[SC_K - 1:SC_K, :]
        for s in range(SC_K):
            dw_ref[SC_K - 1 - s:SC_K - s, :] = jnp.sum(dcv * shifted[s], axis=0, keepdims=True)
            if s:
                du = du + _shift_up(dcv, s, ti, t) * w_ref[SC_K - 1 - s:SC_K - s, :]
        dc_ref[...] = (du * xin).astype(BF16)
        dx_ref[...] = (du * cg).astype(BF16)

    tile = pl.BlockSpec((t, LANES), lambda j: (0, j))
    outs = pl.pallas_call(
        body, name="shortconv_bwd", grid=(nb,),
        in_specs=[pl.BlockSpec((t, LANES), lambda j: (0, doff + j)),
                  pl.BlockSpec((t, LANES), lambda j: (0, off + j)),
                  pl.BlockSpec((t, LANES), lambda j: (0, off + nb + j)),
                  pl.BlockSpec((t, LANES), lambda j: (0, off + 2 * nb + j)),
                  pl.BlockSpec((SC_K, LANES), lambda j: (0, j))],
        out_specs=[tile, tile, tile, pl.BlockSpec((SC_K, LANES), lambda j: (0, j))],
        out_shape=[jax.ShapeDtypeStruct((t, SC_W), BF16)] * 3 + [jax.ShapeDtypeStruct((SC_K, SC_W), F32)],
        compiler_params=_cp("parallel"),
    )(dy, proj, proj, proj, w)
    return outs


def _cum_logf(proj, fbias_row):
    t = proj.shape[0]
    blk = CUM_BLOCK

    def body(p_ref, b_ref, o_ref, carry_ref):
        i = pl.program_id(0)

        @pl.when(i == 0)
        def _():
            carry_ref[...] = jnp.zeros_like(carry_ref)

        lf = -_softplus(-(p_ref[...] + b_ref[...]))
        r = lax.broadcasted_iota(jnp.int32, (blk, blk), 0)
        c = lax.broadcasted_iota(jnp.int32, (blk, blk), 1)
        tri = (r >= c).astype(BF16)
        o_ref[...] = _tri_left(tri, lf) + carry_ref[...]
        carry_ref[...] = o_ref[blk - 1:blk, :]

    return pl.pallas_call(
        body, name="cum_logf", grid=(t // blk,),
        in_specs=[pl.BlockSpec((blk, LANES), lambda i: (i, P_PAD // LANES)),
                  pl.BlockSpec((1, LANES), lambda i: (0, 0))],
        out_specs=pl.BlockSpec((blk, LANES), lambda i: (i, 0)),
        out_shape=jax.ShapeDtypeStruct((t, LANES), F32),
        scratch_shapes=[pltpu.VMEM((1, LANES), F32)],
        compiler_params=_cp("arbitrary"),
    )(proj, fbias_row)


def _pad_block_bwd(dcf, ddt, proj, fbias_row):
    t = proj.shape[0]
    blk = CUM_BLOCK
    nb = t // blk

    def body(dcf_ref, ddt_ref, p_ref, b_ref, o_ref, db_ref, carry_ref):
        i = pl.program_id(0)

        @pl.when(i == 0)
        def _():
            carry_ref[...] = jnp.zeros_like(carry_ref)
            db_ref[...] = jnp.zeros_like(db_ref)

        r = lax.broadcasted_iota(jnp.int32, (blk, blk), 0)
        c = lax.broadcasted_iota(jnp.int32, (blk, blk), 1)
        tri = (r <= c).astype(BF16)
        rev = _tri_left(tri, dcf_ref[...]) + carry_ref[...]
        carry_ref[...] = jnp.sum(dcf_ref[...], axis=0, keepdims=True) + carry_ref[...]
        lane = lax.broadcasted_iota(jnp.int32, (blk, LANES), 1)
        is_f = (lane >= PAD_F0) & (lane < PAD_F0 + FOX_HEADS)
        df = jnp.where(is_f, rev * _sigmoid(-(p_ref[...] + b_ref[...])), 0.0)
        db_ref[...] += jnp.sum(df, axis=0, keepdims=True)
        o_ref[...] = jnp.where(lane < PAD_DT0 + SSD_HEADS, ddt_ref[...], df).astype(BF16)

    return pl.pallas_call(
        body, name="pad_block_bwd", grid=(nb,),
        in_specs=[pl.BlockSpec((blk, LANES), lambda i: (nb - 1 - i, 0)),
                  pl.BlockSpec((blk, LANES), lambda i: (nb - 1 - i, 0)),
                  pl.BlockSpec((blk, LANES), lambda i: (nb - 1 - i, P_PAD // LANES)),
                  pl.BlockSpec((1, LANES), lambda i: (0, 0))],
        out_specs=[pl.BlockSpec((blk, LANES), lambda i: (nb - 1 - i, 0)),
                   pl.BlockSpec((1, LANES), lambda i: (0, 0))],
        out_shape=[jax.ShapeDtypeStruct((t, LANES), BF16), jax.ShapeDtypeStruct((1, LANES), F32)],
        scratch_shapes=[pltpu.VMEM((1, LANES), F32)],
        compiler_params=_cp("arbitrary"),
    )(dcf, ddt, proj, fbias_row)


def _att_scores(q, k, ck, diagonal, blk):
    s = _dot_nt(q, k) - ck
    if diagonal:
        r = lax.broadcasted_iota(jnp.int32, (blk, blk), 0)
        c = lax.broadcasted_iota(jnp.int32, (blk, blk), 1)
        s = jnp.where(r >= c, s, NEG)
    return s


def _fox_fwd(q, k, v, cf_row):
    h, t, hd = q.shape
    blk = min(ATT_BLOCK, t)
    nb = t // blk
    pair = 2

    def body(q_ref, k_ref, v_ref, ck_ref, o_ref, lse_ref):
        qi = pl.program_id(1)
        qv = [q_ref[hh] for hh in range(pair)]

        def step(j, carry, diagonal):
            off = pl.multiple_of(j * blk, blk)
            out = []
            for hh in range(pair):
                m, l, acc = carry[hh]
                s = _att_scores(qv[hh], k_ref[hh, pl.ds(off, blk), :], ck_ref[hh, :, pl.ds(off, blk)], diagonal, blk)
                m_new = jnp.maximum(m, jnp.max(s, axis=1, keepdims=True))
                alpha = jnp.exp(m - m_new)
                p = jnp.exp(s - m_new)
                l = alpha * l + jnp.sum(p, axis=1, keepdims=True)
                acc = alpha * acc + _dot(p.astype(BF16), v_ref[hh, pl.ds(off, blk), :])
                out.append((m_new, l, acc))
            return tuple(out)

        one = (jnp.full((blk, 1), NEG, F32), jnp.zeros((blk, 1), F32), jnp.zeros((blk, hd), F32))
        carry = lax.fori_loop(0, qi, lambda j, cr: step(j, cr, False), (one,) * pair)
        for hh, (m, l, acc) in enumerate(step(qi, carry, True)):
            o_ref[hh] = acc / l
            lse_ref[hh] = m + jnp.log(l)

    qmap = lambda hp, i: (hp, i, 0)
    whole = lambda hp, i: (hp, 0, 0)
    return pl.pallas_call(
        body, name="fox_fwd", grid=(h // pair, nb),
        in_specs=[pl.BlockSpec((pair, blk, hd), qmap), pl.BlockSpec((pair, t, hd), whole),
                  pl.BlockSpec((pair, t, hd), whole), pl.BlockSpec((pair, 1, t), whole)],
        out_specs=[pl.BlockSpec((pair, blk, hd), qmap), pl.BlockSpec((pair, blk, 1), qmap)],
        out_shape=[jax.ShapeDtypeStruct((h, t, hd), F32), jax.ShapeDtypeStruct((h, t, 1), F32)],
        compiler_params=_cp("parallel", "arbitrary"),
    )(q, k, v, cf_row)


def _fox_bwd(q, k, v, cf_row, o, lse, do):
    h, t, hd = q.shape
    blk = min(ATT_BLOCK, t)
    nb = t // blk

    def body(q_ref, k_ref, v_ref, ck_ref, o_ref, lse_ref, do_ref,
             dq_ref, dk_ref, dv_ref, dcq_ref, dck_ref, delta_s):
        kj = pl.program_id(1)

        @pl.when(kj == 0)
        def _():
            dq_ref[...] = jnp.zeros_like(dq_ref)
            dcq_ref[...] = jnp.zeros_like(dcq_ref)

            def fill(i, _):
                rows = pl.ds(pl.multiple_of(i * blk, blk), blk)
                delta_s[rows, :] = jnp.sum(do_ref[rows, :] * o_ref[rows, :], axis=1, keepdims=True)
                return 0

            lax.fori_loop(0, nb, fill, 0)

        kb, vb, ck = k_ref[...], v_ref[...], ck_ref[...]

        def step(i, carry, diagonal):
            dk, dv, dck = carry
            rows = pl.ds(pl.multiple_of(i * blk, blk), blk)
            qb = q_ref[rows, :]
            do_b = do_ref[rows, :].astype(BF16)
            s = _att_scores(qb, kb, ck, diagonal, blk)
            p = jnp.exp(s - lse_ref[rows, :])
            dv = dv + _dot_tn(p.astype(BF16), do_b)
            ds = p * (_dot_nt(do_b, vb) - delta_s[rows, :])
            ds_b = ds.astype(BF16)
            dk = dk + _dot_tn(ds_b, qb)
            dq_ref[rows, :] += _dot(ds_b, kb)
            dcq_ref[rows, :] += jnp.sum(ds, axis=1, keepdims=True)
            dck = dck - jnp.sum(ds, axis=0, keepdims=True)
            return dk, dv, dck

        init = (jnp.zeros((blk, hd), F32), jnp.zeros((blk, hd), F32), jnp.zeros((1, blk), F32))
        carry = step(kj, init, True)
        dk, dv, dck = lax.fori_loop(kj + 1, nb, lambda i, cr: step(i, cr, False), carry)
        dk_ref[...] = dk
        dv_ref[...] = dv
        dck_ref[...] = dck

    kmap = lambda hh, j: (hh, j, 0)
    whole = lambda hh, j: (hh, 0, 0)
    return pl.pallas_call(
        body, name="fox_bwd", grid=(h, nb),
        in_specs=[pl.BlockSpec((None, t, hd), whole), pl.BlockSpec((None, blk, hd), kmap),
                  pl.BlockSpec((None, blk, hd), kmap), pl.BlockSpec((None, 1, blk), lambda hh, j: (hh, 0, j)),
                  pl.BlockSpec((None, t, hd), whole), pl.BlockSpec((None, t, 1), whole),
                  pl.BlockSpec((None, t, hd), whole)],
        out_specs=[pl.BlockSpec((None, t, hd), whole), pl.BlockSpec((None, blk, hd), kmap),
                   pl.BlockSpec((None, blk, hd), kmap), pl.BlockSpec((None, t, 1), whole),
                   pl.BlockSpec((None, 1, blk), lambda hh, j: (hh, 0, j))],
        out_shape=[jax.ShapeDtypeStruct((h, t, hd), F32), jax.ShapeDtypeStruct((h, t, hd), F32),
                   jax.ShapeDtypeStruct((h, t, hd), F32), jax.ShapeDtypeStruct((h, t, 1), F32),
                   jax.ShapeDtypeStruct((h, 1, t), F32)],
        scratch_shapes=[pltpu.VMEM((t, 1), F32)],
        compiler_params=_cp("parallel", "arbitrary"),
    )(q, k, v, cf_row, o, lse, do)


def _lane_col(v, h):
    lane = lax.broadcasted_iota(jnp.int32, v.shape, 1)
    return jnp.sum(jnp.where(lane == h, v, 0.0), axis=1, keepdims=True)


def _sub_row(v, h):
    sub = lax.broadcasted_iota(jnp.int32, v.shape, 0)
    return jnp.sum(jnp.where(sub == h, v, 0.0), axis=0, keepdims=True)


def _ssd_decays(pad, pad_t, dtb_row, alog_row, dtb_col, alog_col, blk):
    r = lax.broadcasted_iota(jnp.int32, (blk, blk), 0)
    c = lax.broadcasted_iota(jnp.int32, (blk, blk), 1)
    tril = r >= c
    dt_c = _softplus(pad + dtb_row)
    acs_c = _tri_left(tril.astype(BF16), dt_c * (-jnp.exp(alog_row)))
    dt_r = _softplus(pad_t + dtb_col)
    acs_r = _tri_right(dt_r * (-jnp.exp(alog_col)), (r <= c).astype(BF16))
    rows = lax.broadcasted_iota(jnp.int32, acs_c.shape, 0)
    acs_last = jnp.sum(jnp.where(rows == blk - 1, acs_c, 0.0), axis=0, keepdims=True)
    return dt_c, acs_c, acs_r, acs_last, tril


def _pair_terms(pair, dt_c, acs_c, acs_r, acs_last, d_row, blk):
    lane = lax.broadcasted_iota(jnp.int32, (blk, LANES), 1)
    lo = lane < 64
    lo_row = lax.broadcasted_iota(jnp.int32, (1, LANES), 1) < 64
    h0, h1 = 2 * pair, 2 * pair + 1
    col = [_lane_col(acs_c, h0), _lane_col(acs_c, h1)]
    row = [_sub_row(acs_r, h0), _sub_row(acs_r, h1)]
    last = [_lane_col(acs_last, h0), _lane_col(acs_last, h1)]
    dt_p = jnp.where(lo, _lane_col(dt_c, h0), _lane_col(dt_c, h1))
    e_p = jnp.where(lo, jnp.exp(col[0]), jnp.exp(col[1]))
    w_p = jnp.where(lo, jnp.exp(last[0] - col[0]), jnp.exp(last[1] - col[1]))
    decay_p = jnp.where(lo_row, jnp.exp(last[0]), jnp.exp(last[1]))
    d_p = jnp.where(lo_row, _lane_col(d_row, h0), _lane_col(d_row, h1))
    return lo, lo_row, col, row, last, dt_p, e_p, w_p, decay_p, d_p


def _ssd_specs(t, blk, rev):
    nc = t // blk
    ix = (lambda i: nc - 1 - i) if rev else (lambda i: i)
    xbc = pl.BlockSpec((blk, SSD_CONV_DIM), lambda i: (ix(i), 0))
    pad = pl.BlockSpec((blk, LANES), lambda i: (ix(i), P_PAD // LANES))
    pad_t = pl.BlockSpec((LANES, blk), lambda i: (0, ix(i)))
    z = pl.BlockSpec((blk, SSD_W), lambda i: (ix(i), 0))
    row = pl.BlockSpec((1, LANES), lambda i: (0, 0))
    colv = pl.BlockSpec((LANES, 1), lambda i: (0, 0))
    ng = pl.BlockSpec((1, SSD_W), lambda i: (0, 0))
    y = pl.BlockSpec((blk, SSD_W), lambda i: (ix(i), 0))
    st = pl.BlockSpec((None, 4, LANES, LANES), lambda i: (ix(i), 0, 0, 0))
    return nc, xbc, pad, pad_t, z, row, colv, ng, y, st


def _ssd_fwd(xbc, proj, pad_t, dtb_row, alog_row, d_row, dtb_col, alog_col, ng):
    t = xbc.shape[0]
    blk = min(SSD_CHUNK, t)
    nc, s_xbc, s_pad, s_padt, s_z, s_row, s_col, s_ng, s_y, s_st = _ssd_specs(t, blk, False)

    def body(xbc_ref, pad_ref, padt_ref, z_ref, dtb_ref, alog_ref, d_ref, dtbc_ref, alogc_ref, ng_ref,
             out_ref, ypre_ref, st_ref, state):
        @pl.when(pl.program_id(0) == 0)
        def _():
            state[...] = jnp.zeros_like(state)

        dt_c, acs_c, acs_r, acs_last, tril = _ssd_decays(
            pad_ref[...], padt_ref[...], dtb_ref[...], alog_ref[...], dtbc_ref[...], alogc_ref[...], blk)
        ys = []
        g_mat = {}
        for pair in range(4):
            g = pair // 2
            bg = xbc_ref[:, SSD_W + LANES * g:SSD_W + LANES * (g + 1)].astype(BF16)
            cg = xbc_ref[:, SSD_W + 2 * LANES + LANES * g:SSD_W + 2 * LANES + LANES * (g + 1)].astype(BF16)
            if g not in g_mat:
                g_mat[g] = _dot_nt(cg, bg)
            xs_p = xbc_ref[:, LANES * pair:LANES * (pair + 1)]
            lo, _, col, row, _, dt_p, e_p, w_p, decay_p, d_p = _pair_terms(
                pair, dt_c, acs_c, acs_r, acs_last, d_ref[...], blk)
            x_p = xs_p * dt_p
            y = None
            for hh in range(2):
                lm = jnp.exp(jnp.where(tril, col[hh] - row[hh], NEG))
                m_h = (g_mat[g] * lm).astype(BF16)
                x_h = jnp.where(lo if hh == 0 else ~lo, x_p, 0.0).astype(BF16)
                y_h = _dot(m_h, x_h)
                y = y_h if y is None else y + y_h
            st_in = state[pair]
            st_ref[pair] = st_in
            y = y + e_p * _dot(cg, st_in.astype(BF16))
            state[pair] = decay_p * st_in + _dot_tn(bg, (x_p * w_p).astype(BF16))
            ys.append(y + d_p * xs_p)
        y_all = jnp.concatenate(ys, axis=1)
        ypre_ref[...] = y_all
        z = z_ref[...]
        y2 = y_all * (z * _sigmoid(z))
        outs = []
        for g in range(2):
            seg = y2[:, 256 * g:256 * (g + 1)]
            rr = lax.rsqrt(jnp.mean(seg * seg, axis=-1, keepdims=True) + RMS_EPS)
            outs.append(seg * rr * ng_ref[:, 256 * g:256 * (g + 1)])
        out_ref[...] = jnp.concatenate(outs, axis=1).astype(BF16)

    return pl.pallas_call(
        body, name="ssd_scan_fwd", grid=(nc,),
        in_specs=[s_xbc, s_pad, s_padt, s_z, s_row, s_row, s_row, s_col, s_col, s_ng],
        out_specs=[s_y, s_y, s_st],
        out_shape=[jax.ShapeDtypeStruct((t, SSD_W), BF16), jax.ShapeDtypeStruct((t, SSD_W), F32),
                   jax.ShapeDtypeStruct((nc, 4, LANES, LANES), F32)],
        scratch_shapes=[pltpu.VMEM((4, LANES, LANES), F32)],
        compiler_params=_cp("arbitrary"),
    )(xbc, proj, pad_t, proj, dtb_row, alog_row, d_row, dtb_col, alog_col, ng)


def _ssd_bwd(dout, dout_off, xbc, proj, pad_t, ypre, states, dtb_row, alog_row, d_row, dtb_col, alog_col, ng):
    t = xbc.shape[0]
    blk = min(SSD_CHUNK, t)
    nc, s_xbc, s_pad, s_padt, s_z, s_row, s_col, s_ng, s_y, s_st = _ssd_specs(t, blk, True)
    s_dout = pl.BlockSpec((blk, SSD_W), lambda i: (nc - 1 - i, dout_off // SSD_W))

    def body(dout_ref, xbc_ref, pad_ref, padt_ref, z_ref, ypre_ref, st_ref, dtb_ref, alog_ref, d_ref,
             dtbc_ref, alogc_ref, ng_ref, dxbc_ref, ddt_ref, dz_ref, acc_ref, dng_ref, dstate):
        @pl.when(pl.program_id(0) == 0)
        def _():
            dstate[...] = jnp.zeros_like(dstate)
            acc_ref[...] = jnp.zeros_like(acc_ref)
            dng_ref[...] = jnp.zeros_like(dng_ref)

        pad = pad_ref[...]
        dt_c, acs_c, acs_r, acs_last, tril = _ssd_decays(
            pad, padt_ref[...], dtb_ref[...], alog_ref[...], dtbc_ref[...], alogc_ref[...], blk)
        a_row = -jnp.exp(alog_ref[...])

        z = z_ref[...]
        sz = _sigmoid(z)
        silu_z = z * sz
        y_pre = ypre_ref[...]
        y2 = y_pre * silu_z
        dy2 = []
        for g in range(2):
            sl = slice(256 * g, 256 * (g + 1))
            seg = y2[:, sl]
            rr = lax.rsqrt(jnp.mean(seg * seg, axis=-1, keepdims=True) + RMS_EPS)
            nrm = seg * rr
            d_seg = dout_ref[:, sl]
            dng_ref[:, sl] += jnp.sum(d_seg * nrm, axis=0, keepdims=True)
            dn = d_seg * ng_ref[:, sl]
            dy2.append(rr * (dn - nrm * jnp.mean(dn * nrm, axis=-1, keepdims=True)))
        dy2 = jnp.concatenate(dy2, axis=1)
        dz_ref[...] = (dy2 * y_pre * (sz * (1.0 + z * (1.0 - sz)))).astype(BF16)
        dy_all = dy2 * silu_z

        lane_row = lax.broadcasted_iota(jnp.int32, (1, LANES), 1)
        lane_blk = lax.broadcasted_iota(jnp.int32, (blk, LANES), 1)
        row_col = lax.broadcasted_iota(jnp.int32, (blk, 1), 0)
        ddt = jnp.zeros((blk, LANES), F32)
        dacs = jnp.zeros((blk, LANES), F32)
        dd_row = jnp.zeros((1, LANES), F32)
        ones_b = jnp.ones((blk, LANES), BF16)
        dxs = []
        d_b = [None, None]
        d_c = [None, None]
        d_g = [None, None]
        bgs, cgs = {}, {}
        g_mat = {}
        for pair in range(4):
            g = pair // 2
            if g not in g_mat:
                bgs[g] = xbc_ref[:, SSD_W + LANES * g:SSD_W + LANES * (g + 1)].astype(BF16)
                cgs[g] = xbc_ref[:, SSD_W + 2 * LANES + LANES * g:SSD_W + 2 * LANES + LANES * (g + 1)].astype(BF16)
                g_mat[g] = _dot_nt(cgs[g], bgs[g])
            bg, cg = bgs[g], cgs[g]
            xs_p = xbc_ref[:, LANES * pair:LANES * (pair + 1)]
            lo, lo_row, col, row, last, dt_p, e_p, w_p, decay_p, d_p = _pair_terms(
                pair, dt_c, acs_c, acs_r, acs_last, d_ref[...], blk)
            x_p = xs_p * dt_p
            dy_p = dy_all[:, LANES * pair:LANES * (pair + 1)]
            st_in = st_ref[pair]
            dst = dstate[pair]
            dx_diag = None
            for hh in range(2):
                sel = lo if hh == 0 else ~lo
                lm = jnp.exp(jnp.where(tril, col[hh] - row[hh], NEG))
                m_f = g_mat[g] * lm
                m_h = m_f.astype(BF16)
                x_h = jnp.where(sel, x_p, 0.0).astype(BF16)
                dy_h = jnp.where(sel, dy_p, 0.0).astype(BF16)
                dxd = _dot_tn(m_h, dy_h)
                dm = _dot_nt(dy_h, x_h)
                dg_h = dm * lm
                p_b = (dm * m_f).astype(BF16)
                dacs = dacs + jnp.where(lane_blk == 2 * pair + hh, _dot(p_b, ones_b) - _dot_tn(p_b, ones_b), 0.0)
                dx_diag = dxd if dx_diag is None else dx_diag + dxd
                d_g[g] = dg_h if d_g[g] is None else d_g[g] + dg_h
            st_b = st_in.astype(BF16)
            dst_b = dst.astype(BF16)
            y_off = e_p * _dot(cg, st_b)
            edy = (e_p * dy_p).astype(BF16)
            dc_off = _dot_nt(edy, st_b)
            d_c[g] = dc_off if d_c[g] is None else d_c[g] + dc_off
            dstate[pair] = decay_p * dst + _dot_tn(cg, edy)
            dx_state = _dot(bg, dst_b) * w_p
            db_st = _dot_nt((x_p * w_p).astype(BF16), dst_b)
            d_b[g] = db_st if d_b[g] is None else d_b[g] + db_st
            dx = dx_diag + dx_state
            dxs.append(dx * dt_p + d_p * dy_p)
            prod_dt = dx * xs_p
            prod_acs = dy_p * y_off - x_p * dx_state
            prod_st = x_p * dx_state
            prod_d = dy_p * xs_p
            st_prod = jnp.sum(dst * st_in, axis=0, keepdims=True)
            for hh in range(2):
                h = 2 * pair + hh
                sel = lo if hh == 0 else ~lo
                sel_row = lo_row if hh == 0 else ~lo_row
                ddt_h = jnp.sum(jnp.where(sel, prod_dt, 0.0), axis=1, keepdims=True)
                dacs_h = jnp.sum(jnp.where(sel, prod_acs, 0.0), axis=1, keepdims=True)
                tail = jnp.sum(jnp.sum(jnp.where(sel, prod_st, 0.0), axis=1, keepdims=True), axis=0, keepdims=True)
                tail = tail + jnp.exp(last[hh]) * jnp.sum(jnp.where(sel_row, st_prod, 0.0), axis=1, keepdims=True)
                dacs_h = dacs_h + jnp.where(row_col == blk - 1, tail, 0.0)
                dd_h = jnp.sum(jnp.sum(jnp.where(sel, prod_d, 0.0), axis=1, keepdims=True), axis=0, keepdims=True)
                ddt = ddt + jnp.where(lane_blk == h, ddt_h, 0.0)
                dacs = dacs + jnp.where(lane_blk == h, dacs_h, 0.0)
                dd_row = dd_row + jnp.where(lane_row == h, dd_h, 0.0)
        for g in range(2):
            dg_b = d_g[g].astype(BF16)
            d_c[g] = d_c[g] + _dot(dg_b, bgs[g])
            d_b[g] = d_b[g] + _dot_tn(dg_b, cgs[g])
        r = lax.broadcasted_iota(jnp.int32, (blk, blk), 0)
        c = lax.broadcasted_iota(jnp.int32, (blk, blk), 1)
        da = _tri_left((r <= c).astype(BF16), dacs)
        ddt = ddt + da * a_row
        d_raw = ddt * _sigmoid(pad + dtb_ref[...])
        ddt_ref[...] = d_raw
        acc_ref[0:1, :] += jnp.sum(da * dt_c, axis=0, keepdims=True) * a_row
        acc_ref[1:2, :] += dd_row
        acc_ref[2:3, :] += jnp.sum(d_raw, axis=0, keepdims=True)
        dxbc_ref[...] = jnp.concatenate(dxs + d_b + d_c, axis=1)

    return pl.pallas_call(
        body, name="ssd_scan_bwd", grid=(nc,),
        in_specs=[s_dout, s_xbc, s_pad, s_padt, s_z, s_y, s_st, s_row, s_row, s_row, s_col, s_col, s_ng],
        out_specs=[s_xbc, pl.BlockSpec((blk, LANES), lambda i: (nc - 1 - i, 0)), s_y,
                   pl.BlockSpec((8, LANES), lambda i: (0, 0)), s_ng],
        out_shape=[jax.ShapeDtypeStruct((t, SSD_CONV_DIM), F32), jax.ShapeDtypeStruct((t, LANES), F32),
                   jax.ShapeDtypeStruct((t, SSD_W), BF16), jax.ShapeDtypeStruct((8, LANES), F32),
                   jax.ShapeDtypeStruct((1, SSD_W), F32)],
        scratch_shapes=[pltpu.VMEM((4, LANES, LANES), F32)],
        compiler_params=_cp("arbitrary"),
    )(dout, xbc, proj, pad_t, proj, ypre, states, dtb_row, alog_row, d_row, dtb_col, alog_col, ng)


def _pad_lanes(v, off):
    return jnp.zeros((1, LANES), F32).at[0, off:off + v.shape[0]].set(v)


def _perm_mix_w_in(w):
    z, xbc, dt = w[:, 0:512], w[:, 512:1536], w[:, 1536:1544]
    qkv, f, sc = w[:, 1544:2312], w[:, 2312:2316], w[:, 2316:3084]
    padblk = jnp.zeros((w.shape[0], LANES), w.dtype).at[:, PAD_DT0:PAD_DT0 + 8].set(dt).at[:, PAD_F0:PAD_F0 + 4].set(f)
    return jnp.concatenate([z, xbc, qkv, sc, padblk], axis=1)


def _unperm_mix_w_in(wp):
    z, xbc, qkv, sc = wp[:, 0:512], wp[:, 512:1536], wp[:, 1536:2304], wp[:, 2304:3072]
    dt, f = wp[:, P_PAD + PAD_DT0:P_PAD + PAD_DT0 + 8], wp[:, P_PAD + PAD_F0:P_PAD + PAD_F0 + 4]
    return jnp.concatenate([z, xbc, dt, qkv, f, sc], axis=1)


def _heads(m):
    return jnp.transpose(m.reshape(m.shape[0], FOX_HEADS, FOX_HD), (1, 0, 2))


def _unheads(m):
    return jnp.transpose(m, (1, 0, 2)).reshape(m.shape[1], FOX_W)


def _ffn_fwd(h, w_in, w_out, tag):
    u, a = _ffn_in_swiglu(h, w_in, f"ffn_in_{tag}")
    y = _matmul(a, w_out, "nn", F32, f"ffn_out_{tag}")
    return y, (h, u, a)


def _ffn_bwd(dy, saved, w_in, w_out, tag):
    h, u, a = saved
    du = _ffn_out_dx_swiglu(dy, w_out, u, f"ffn_out_dx_{tag}")
    dw_out = _matmul(a, dy, "tn", BF16, f"ffn_out_dw_{tag}")
    dh = _matmul(du, w_in, "nt", F32, f"ffn_in_dx_{tag}", halves="a")
    dw_in = _matmul(h, du, "tn", BF16, f"ffn_in_dw_{tag}", halves="b")
    return dh, dw_in, dw_out


def _mix_fwd(h, wp, w_out, sp, tag):
    proj = _matmul(h, wp, "nn", F32, f"mix_in_{tag}")
    pad_t = jnp.transpose(proj[:, P_PAD:P_PAD + LANES])
    xbc = _ssd_conv_fwd(proj, sp["conv_w"], sp["conv_b"])
    y_ssd, ypre, states = _ssd_fwd(xbc, proj, pad_t, sp["dtb_row"], sp["alog_row"], sp["d_row"],
                                   sp["dtb_col"], sp["alog_col"], sp["ng"])
    cf = _cum_logf(proj, sp["fbias_row"])
    cf_row = jnp.transpose(cf[:, PAD_F0:PAD_F0 + FOX_HEADS])[:, None, :]
    q = _heads((proj[:, P_QKV:P_QKV + 256] * FOX_SCALE).astype(BF16))
    k = _heads(proj[:, P_QKV + 256:P_QKV + 512].astype(BF16))
    v = _heads(proj[:, P_QKV + 512:P_QKV + 768].astype(BF16))
    o, lse = _fox_fwd(q, k, v, cf_row)
    y_sc = _shortconv_fwd(proj, sp["sconv_w"])
    ymix = jnp.concatenate([y_ssd, _unheads(o).astype(BF16), y_sc], axis=1)
    y = _matmul(ymix, w_out, "nn", F32, f"mix_out_{tag}")
    return y, (h, proj, pad_t, xbc, ypre, states, q, k, v, cf_row, o, lse, ymix)


def _mix_bwd(dy, saved, wp, w_out, sp, tag):
    h, proj, pad_t, xbc, ypre, states, q, k, v, cf_row, o, lse, ymix = saved
    dymix = _matmul(dy, w_out, "nt", F32, f"mix_out_dx_{tag}")
    dw_out = _matmul(ymix, dy, "tn", BF16, f"mix_out_dw_{tag}")
    dxbc, ddt, dz, ssd_acc, dng = _ssd_bwd(dymix, 0, xbc, proj, pad_t, ypre, states, sp["dtb_row"],
                                           sp["alog_row"], sp["d_row"], sp["dtb_col"], sp["alog_col"], sp["ng"])
    dxbc_raw, dconv_w, dconv_b = _ssd_conv_bwd(dxbc, proj, sp["conv_w"], sp["conv_b"])
    do = _heads(dymix[:, SSD_W:SSD_W + FOX_W])
    dq, dk, dv, dcq, dck = _fox_bwd(q, k, v, cf_row, o, lse, do)
    dq = dq * FOX_SCALE
    dcf4 = dcq[:, :, 0] + dck[:, 0, :]
    dcf = jnp.zeros((h.shape[0], LANES), F32).at[:, PAD_F0:PAD_F0 + FOX_HEADS].set(jnp.transpose(dcf4))
    dpad, dfb = _pad_block_bwd(dcf, ddt, proj, sp["fbias_row"])
    dscb, dscc, dscx, dsconv_w = _shortconv_bwd(dymix, SSD_W + FOX_W, proj, sp["sconv_w"])
    dproj = jnp.concatenate([dz, dxbc_raw.astype(BF16), _unheads(dq).astype(BF16), _unheads(dk).astype(BF16),
                             _unheads(dv).astype(BF16), dscb, dscc, dscx, dpad], axis=1)
    dh = _matmul(dproj, wp, "nt", F32, f"mix_in_dx_{tag}")
    dwp = _matmul(h, dproj, "tn", BF16, f"mix_in_dw_{tag}")
    small = dict(conv_w=dconv_w, conv_b=dconv_b[0], dt_bias=ssd_acc[2, 0:8], a_log=ssd_acc[0, 0:8],
                 d=ssd_acc[1, 0:8], norm_g=dng[0], f_bias=dfb[0, PAD_F0:PAD_F0 + FOX_HEADS], sconv_w=dsconv_w)
    return dh, _unperm_mix_w_in(dwp), dw_out, small


def _local_step(x, tgt, mod, weights_of, small_p, before_sub_backward=None, after_sub_backward=None):
    row = lambda v: v.reshape(1, -1)
    subs = [(l, j) for l in range(DEPTH) for j in range(N_SUB)]
    factor = (0.5, 1.0, 0.5)
    w_names = (("ffn1_w_in", "ffn1_w_out"), ("mix_w_in", "mix_w_out"), ("ffn2_w_in", "ffn2_w_out"))
    lg = [[row(small_p["ln_g"][l, j]) for j in range(N_SUB)] for l in range(DEPTH)]
    lb = [[row(small_p["ln_b"][l, j]) for j in range(N_SUB)] for l in range(DEPTH)]
    sps = []
    for l in range(DEPTH):
        sp = dict(
            conv_w=small_p["ssd_conv_w"][l], conv_b=row(small_p["ssd_conv_b"][l]),
            dtb_row=_pad_lanes(small_p["ssd_dt_bias"][l], PAD_DT0), alog_row=_pad_lanes(small_p["ssd_a_log"][l], 0),
            d_row=_pad_lanes(small_p["ssd_d"][l], 0), ng=row(small_p["ssd_norm_g"][l]),
            fbias_row=_pad_lanes(small_p["fox_f_bias"][l], PAD_F0), sconv_w=small_p["sconv_w"][l])
        sp["dtb_col"] = jnp.transpose(sp["dtb_row"])
        sp["alog_col"] = jnp.transpose(sp["alog_row"])
        sps.append(sp)
    tags = [f"l{l}{('f1', 'mx', 'f2')[j]}" for l, j in subs]

    x0, h = _ln_in_fwd(x, row(small_p["ln_in_g"]), row(small_p["ln_in_b"]), mod[0, 0])
    cur = x0
    xins, ys, inner, weights = [], [], [], []
    for idx, (l, j) in enumerate(subs):
        w_in, w_out = weights_of(l, j, h)
        if idx > 0:
            (w_in, w_out), (cur, h) = lax.optimization_barrier(((w_in, w_out), (cur, h)))
        w_out = w_out.reshape(-1, w_out.shape[-1])
        if j == 1:
            w_in = _perm_mix_w_in(jnp.concatenate([w_in[s] for s in range(4)], axis=1))
        weights.append((w_in, w_out))
        if j == 1:
            y, sv = _mix_fwd(h, w_in, w_out, sps[l], tags[idx])
        else:
            y, sv = _ffn_fwd(h, w_in, w_out, tags[idx])
        nxt = mod[subs[idx + 1]] if idx + 1 < len(subs) else None
        xins.append(cur)
        ys.append(y)
        inner.append(sv)
        cur, h = _res_ln(cur, y, mod[l, j], lg[l][j], lb[l][j], factor[j], f"res_ln_{tags[idx]}", nxt)

    dcur, loss_acc = _loss_head(cur, tgt)
    last = len(subs) - 1
    l, j = subs[last]
    dres, dy, acc = _res_ln_bwd(dcur, xins[last], ys[last], mod[l, j], lg[l][j], factor[j], f"res_ln_bwd_{tags[last]}")
    ln_acc = {last: acc}
    shift_scale = {}
    big_grads = [dict() for _ in range(DEPTH)]
    small_g = [None] * DEPTH
    for idx in reversed(range(len(subs))):
        l, j = subs[idx]
        w_in, w_out = weights[idx]
        if before_sub_backward is not None:
            before_sub_backward(l, j, dy)
        if j == 1:
            dh, g_in, g_out, small_g[l] = _mix_bwd(dy, inner[idx], w_in, w_out, sps[l], tags[idx])
        else:
            dh, g_in, g_out = _ffn_bwd(dy, inner[idx], w_in, w_out, tags[idx])
        big_grads[l][w_names[j][0]], big_grads[l][w_names[j][1]] = g_in, g_out
        if after_sub_backward is not None:
            after_sub_backward(l, j, big_grads[l], dh)
        if idx > 0:
            pl_, pj = subs[idx - 1]
            dres, dy, acc5 = _mod_res_bwd(dres, dh, mod[l, j], xins[idx - 1], ys[idx - 1], mod[pl_, pj], lg[pl_][pj],
                                          lb[pl_][pj], factor[pj], f"mod_res_bwd_{tags[idx]}")
            shift_scale[idx], ln_acc[idx - 1] = acc5[0:2], acc5[2:5]
        else:
            dx0, shift_scale[0] = _modulate_bwd(dres, dh, x0, mod[0, 0], "mod_bwd_first")
    dx, acc_in = _ln_in_bwd(dx0, x, row(small_p["ln_in_g"]))
    dmod = []
    for l in range(DEPTH):
        ids = [N_SUB * l + j for j in range(N_SUB)]
        small_g[l]["ln_g"] = jnp.stack([ln_acc[i][0] for i in ids])
        small_g[l]["ln_b"] = jnp.stack([ln_acc[i][1] for i in ids])
        dmod.append(jnp.stack([jnp.concatenate([shift_scale[i], ln_acc[i][2:3]], axis=0) for i in ids]))
    return loss_acc[0, 0], dx, big_grads, small_g, jnp.stack(dmod), acc_in


SMALL_ORDER = ("ssd_conv_w", "ssd_conv_b", "ssd_dt_bias", "ssd_a_log", "ssd_d", "ssd_norm_g", "fox_f_bias",
               "sconv_w", "ln_g", "ln_b")
SMALL_KEY = dict(ssd_conv_w="conv_w", ssd_conv_b="conv_b", ssd_dt_bias="dt_bias", ssd_a_log="a_log", ssd_d="d",
                 ssd_norm_g="norm_g", fox_f_bias="f_bias", sconv_w="sconv_w", ln_g="ln_g", ln_b="ln_b")
COL_SHARDED_SMALL = ("ssd_conv_w", "sconv_w", "ln_g", "ln_b")


def _pad_to(v, n):
    return jnp.concatenate([v, jnp.zeros((n - v.shape[0],), v.dtype)])


def kernel(x, c, ln_in_g, ln_in_b, ada_w, ada_b, ffn1_w_in, ffn1_w_out, mix_w_in, mix_w_out, ssd_conv_w, ssd_conv_b, ssd_dt_bias, ssd_a_log, ssd_d, ssd_norm_g, fox_f_bias, sconv_w, ffn2_w_in, ffn2_w_out, ln_g, ln_b, loss_target, m_ln_in_g, m_ln_in_b, m_ada_w, m_ada_b, m_ffn1_w_in, m_ffn1_w_out, m_mix_w_in, m_mix_w_out, m_ssd_conv_w, m_ssd_conv_b, m_ssd_dt_bias, m_ssd_a_log, m_ssd_d, m_ssd_norm_g, m_fox_f_bias, m_sconv_w, m_ffn2_w_in, m_ffn2_w_out, m_ln_g, m_ln_b, v_ln_in_g, v_ln_in_b, v_ada_w, v_ada_b, v_ffn1_w_in, v_ffn1_w_out, v_mix_w_in, v_mix_w_out, v_ssd_conv_w, v_ssd_conv_b, v_ssd_dt_bias, v_ssd_a_log, v_ssd_d, v_ssd_norm_g, v_fox_f_bias, v_sconv_w, v_ffn2_w_in, v_ffn2_w_out, v_ln_g, v_ln_b):
    names = ("ln_in_g", "ln_in_b", "ada_w", "ada_b", "ffn1_w_in", "ffn1_w_out", "mix_w_in", "mix_w_out",
             "ssd_conv_w", "ssd_conv_b", "ssd_dt_bias", "ssd_a_log", "ssd_d", "ssd_norm_g", "fox_f_bias", "sconv_w",
             "ffn2_w_in", "ffn2_w_out", "ln_g", "ln_b")
    w_loc = dict(zip(names, (ln_in_g, ln_in_b, ada_w, ada_b, ffn1_w_in, ffn1_w_out, mix_w_in, mix_w_out, ssd_conv_w,
                             ssd_conv_b, ssd_dt_bias, ssd_a_log, ssd_d, ssd_norm_g, fox_f_bias, sconv_w, ffn2_w_in,
                             ffn2_w_out, ln_g, ln_b)))
    m_loc = dict(zip(names, (m_ln_in_g, m_ln_in_b, m_ada_w, m_ada_b, m_ffn1_w_in, m_ffn1_w_out, m_mix_w_in,
                             m_mix_w_out, m_ssd_conv_w, m_ssd_conv_b, m_ssd_dt_bias, m_ssd_a_log, m_ssd_d,
                             m_ssd_norm_g, m_fox_f_bias, m_sconv_w, m_ffn2_w_in, m_ffn2_w_out, m_ln_g, m_ln_b)))
    v_loc = dict(zip(names, (v_ln_in_g, v_ln_in_b, v_ada_w, v_ada_b, v_ffn1_w_in, v_ffn1_w_out, v_mix_w_in,
                             v_mix_w_out, v_ssd_conv_w, v_ssd_conv_b, v_ssd_dt_bias, v_ssd_a_log, v_ssd_d,
                             v_ssd_norm_g, v_fox_f_bias, v_sconv_w, v_ffn2_w_in, v_ffn2_w_out, v_ln_g, v_ln_b)))

    xi, yi, ci = lax.axis_index("x"), lax.axis_index("y"), lax.axis_index("c")
    me = 2 * xi + yi
    dev = 2 * me + ci

    def gather8(v, tag):
        v2 = v.reshape(1, -1)
        got = _bcast_chips([v2], f"gather_chips_{tag}")
        same_c = jnp.concatenate(_by_chip(me, v2, [g[0] for g in got]), axis=0)
        other_c = _swap_sibling([same_c], f"gather_sibling_{tag}")[0]
        pair = lax.switch(ci, [lambda a, b: jnp.stack([a, b], axis=1), lambda a, b: jnp.stack([b, a], axis=1)],
                          same_c, other_c)
        return pair.reshape(8, -1)

    def chip_concat(own, got3, axis):
        return jnp.concatenate(_by_chip(me, own, got3), axis=axis)

    small_cols = [w_loc[n].reshape(-1, w_loc[n].shape[-1]) for n in COL_SHARDED_SMALL]
    got = _bcast_chips(small_cols, "gather_small_params")
    small_p = {n: w_loc[n] for n in ("ln_in_g", "ln_in_b", "ssd_conv_b", "ssd_dt_bias", "ssd_a_log", "ssd_d",
                                     "ssd_norm_g", "fox_f_bias")}
    for i, n in enumerate(COL_SHARDED_SMALL):
        full = chip_concat(small_cols[i], [g[i] for g in got], 1)
        small_p[n] = full.reshape(w_loc[n].shape[:-1] + (full.shape[-1],))

    assert DEPTH == 2
    first, rest = BIG[:2], BIG[2:]
    shard_of = {n: w_loc[n].astype(BF16) for n in BIG}

    def gather(names, layer, after, tag, sequencer_id):
        out = _gather_weights(names, [shard_of[n] for n in names], layer, after, f"gather_weights_{tag}", sequencer_id)
        return dict(zip(names, out))

    gathered = {0: gather(first, 0, [], "l0a", 12)}

    def weights_of(l, j, marker):
        if (l, j) == (0, 0):
            gathered[0].update(gather(rest, 0, [marker], "l0b", 1))
        elif (l, j) == (0, 1):
            gathered[1] = gather(BIG, 1, [marker], "l1", 5)
        pair = BIG[2 * j:2 * j + 2]
        return gathered[l][pair[0]], gathered[l][pair[1]]

    c_all = gather8(c[0], "c")
    c_act = _silu_bf16(c_all)
    ada_w_b = ada_w.astype(BF16)
    mod_loc = [_matmul(c_act, ada_w_b[l], "nn", F32, f"ada_fwd_l{l}") for l in range(DEPTH)]
    mod_loc = jnp.stack(mod_loc)
    got = _bcast_chips([mod_loc], "gather_mod")
    mod_all = chip_concat(mod_loc, [g[0] for g in got], 2)
    mod = lax.dynamic_index_in_dim(mod_all, dev, 1, keepdims=False) + ada_b
    mod = mod.reshape(DEPTH, N_SUB, 3, D)

    ci_arr = jnp.reshape(ci, (1,)).astype(jnp.int32)
    me_arr = jnp.reshape(me, (1,)).astype(jnp.int32)

    def wire_form(n, g):
        if n in BY_COLUMNS:
            return g
        if n == "mix_w_in":
            return jnp.transpose(g.reshape(g.shape[0], 4, g.shape[1] // 4), (1, 0, 2))
        return g.reshape(4, g.shape[0] // 4, g.shape[1])

    units, reduced = {}, {}

    def start_unit(tag, l, names, g, ids):
        wire = [wire_form(n, g[n]) for n in names]
        units[tag] = dict(l=l, names=names, wire=wire, ids=ids,
                          got=_grad_swap_halves(names, wire, f"grad_swap_halves_{tag}", ids[0]))

    def add_and_scatter(tag, after):
        st = units[tag]
        st["part"] = [_grad_add_halves(st["wire"][i], st["got"][i], ci_arr, after, f"grad_add_halves_{tag}_{n}")
                      for i, n in enumerate(st["names"])]
        st["from_chips"] = _grad_scatter_halves(st["names"], st["part"], f"grad_scatter_halves_{tag}", st["ids"][1])

    def finish_unit(tag, after):
        st = units[tag]
        halves = [_grad_add_slots(st["part"][i], [g[i] for g in st["from_chips"]], n, me_arr, after,
                                  f"grad_add_slots_{tag}_{n}") for i, n in enumerate(st["names"])]
        other = _grad_swap_reduced(halves, f"grad_swap_reduced_{tag}", st["ids"][2])
        for i, n in enumerate(st["names"]):
            reduced[(st["l"], n)] = (halves[i], other[i])

    second, third = BIG[2:4], BIG[4:]

    def after_sub_backward(l, j, g, marker):
        if (l, j) == (1, 0):
            start_unit("l1", 1, BIG, g, (2, 3, 4))
        elif (l, j) == (0, 2):
            start_unit("l0f2", 0, third, g, (6, 7, 8))
        elif (l, j) == (0, 1):
            start_unit("l0mx", 0, second, g, (9, 10, 11))

    def before_sub_backward(l, j, marker):
        if (l, j) == (0, 2):
            add_and_scatter("l1", marker)
        elif (l, j) == (0, 1):
            add_and_scatter("l0f2", marker)
        elif (l, j) == (0, 0):
            add_and_scatter("l0mx", marker)

    loss_part, dx, big_g, small_g, dmod, acc_in = _local_step(
        x[0], loss_target[0], mod, weights_of, small_p, before_sub_backward, after_sub_backward)
    loss = lax.psum(loss_part, ("x", "y", "c"))

    pieces = [dmod.reshape(-1), acc_in[0], acc_in[1]]
    for n in SMALL_ORDER:
        pieces.append(jnp.stack([small_g[l][SMALL_KEY[n]] for l in range(DEPTH)]).reshape(-1))
    sizes = [p.shape[0] for p in pieces]
    total = sum(sizes)
    padded = -(-total // 1024) * 1024
    vec = _pad_to(jnp.concatenate(pieces), padded)
    all_rows = gather8(vec, "small_grads")
    summed = _sum_rows(all_rows, "sum_small_grads")[0]
    offs = [0]
    for s in sizes:
        offs.append(offs[-1] + s)
    n_mod = sizes[0]
    grads = {"ada_b": summed[0:n_mod].reshape(DEPTH, 3 * N_SUB * D),
             "ln_in_g": summed[offs[1]:offs[2]], "ln_in_b": summed[offs[2]:offs[3]]}
    for i, n in enumerate(SMALL_ORDER):
        full = summed[offs[3 + i]:offs[4 + i]].reshape(small_p[n].shape)
        if n in COL_SHARDED_SMALL:
            wcols = w_loc[n].shape[-1]
            full = lax.dynamic_slice_in_dim(full, me * wcols, wcols, axis=full.ndim - 1)
        grads[n] = full

    dmod_all = all_rows[:, 0:n_mod].reshape(8, DEPTH, 3 * N_SUB * D)
    ncol = ada_w.shape[-1]
    dmod_cols = lax.dynamic_slice_in_dim(dmod_all, me * ncol, ncol, axis=2).astype(BF16)
    grads["ada_w"] = jnp.stack([_matmul(c_act, dmod_cols[:, l], "tn", F32, f"ada_bwd_l{l}") for l in range(DEPTH)])

    finish_unit("l1", dx)
    finish_unit("l0f2", dx)
    finish_unit("l0mx", dx)
    start_unit("l0f1", 0, first, big_g[0], (None, None, None))
    add_and_scatter("l0f1", dx)
    finish_unit("l0f1", dx)

    delta, new_m, new_v = {}, {}, {}
    delta["ada_w"], new_m["ada_w"], new_v["ada_w"] = _adamw(w_loc["ada_w"], grads["ada_w"], m_loc["ada_w"],
                                                           v_loc["ada_w"], "adamw_ada_w")
    for i, n in enumerate(BIG):
        grads[n], delta[n], new_m[n], new_v[n] = _adamw_layers(
            w_loc[n], [reduced[(l, n)][0] for l in range(DEPTH)], [reduced[(l, n)][1] for l in range(DEPTH)],
            m_loc[n], v_loc[n], ci_arr, f"adamw_{n}")
    small_names = [n for n in names if n not in ("ada_w",) + BIG]
    flat = lambda d: jnp.concatenate([d[n].reshape(-1) for n in small_names])
    n_small = sum(w_loc[n].size for n in small_names)
    n_pad = -(-n_small // 1024) * 1024
    packed = [_pad_to(flat(d), n_pad).reshape(-1, LANES) for d in (w_loc, grads, m_loc, v_loc)]
    d_s, m_s, v_s = _adamw(*packed, "adamw_small")
    off = 0
    for n in small_names:
        sz = w_loc[n].size
        delta[n] = d_s.reshape(-1)[off:off + sz].reshape(w_loc[n].shape)
        new_m[n] = m_s.reshape(-1)[off:off + sz].reshape(w_loc[n].shape)
        new_v[n] = v_s.reshape(-1)[off:off + sz].reshape(w_loc[n].shape)
        off += sz

    return (loss, dx[None], *[grads[n] for n in names], *[delta[n] for n in names],
            *[new_m[n] for n in names], *[new_v[n] for n in names])
```

```python
import functools

import jax
import jax.numpy as jnp
from jax import lax
from jax.experimental import pallas as pl
from jax.experimental.pallas import tpu as pltpu
from jax.experimental.pallas import tpu_sc as plsc

F32 = jnp.float32
BF16 = jnp.bfloat16
MESH = pl.DeviceIdType.MESH

D = 1024
DEPTH = 2
N_SUB = 3
D_FF = 2816
FF_TILE = D_FF // 2
ALPHA = (2 * DEPTH) ** 0.25
LN_EPS = 1e-5
RMS_EPS = 1e-5
SSD_W = 512
SSD_HEADS = 8
SSD_CONV_K = 4
SSD_CONV_DIM = 1024
FOX_W = 256
FOX_HEADS = 4
FOX_HD = 64
FOX_SCALE = FOX_HD ** -0.5
SC_W = 256
SC_K = 3
D_IN_PROJ = 3084
P_Z, P_XBC, P_QKV, P_SC, P_PAD = 0, 512, 1536, 2304, 3072
D_PROJ_PAD = 3200
PAD_DT0, PAD_F0 = 0, 8
SSD_CHUNK = 256
ATT_BLOCK = 512
CUM_BLOCK = 256
LANES = 128
VMEM_LIMIT = 56 * 1024 * 1024

ADAM_LR, ADAM_B1, ADAM_B2, ADAM_EPS, ADAM_WD, ADAM_STEP = 0.001, 0.9, 0.999, 1e-08, 0.01, 10
NEG = -1e30


def _cp(*sem):
    return pltpu.CompilerParams(dimension_semantics=sem, vmem_limit_bytes=VMEM_LIMIT)


def _pick(n, cands):
    for c in cands:
        if n % c == 0:
            return c
    return n


def _dot(a, b):
    return lax.dot_general(a, b, (((1,), (0,)), ((), ())), preferred_element_type=F32)


def _dot_nt(a, b):
    return lax.dot_general(a, b, (((1,), (1,)), ((), ())), preferred_element_type=F32)


def _dot_tn(a, b):
    return lax.dot_general(a, b, (((0,), (0,)), ((), ())), preferred_element_type=F32)


def _sigmoid(x):
    return 0.5 * jnp.tanh(0.5 * x) + 0.5


def _softplus(x):
    return jnp.maximum(x, 0.0) + jnp.log(1.0 + jnp.exp(-jnp.abs(x)))


def _split3(v):
    h = v.astype(BF16)
    r = v - h.astype(F32)
    m = r.astype(BF16)
    l = (r - m.astype(F32)).astype(BF16)
    return h, m, l


def _tri_left(tri, v):
    h, m, l = _split3(v)
    return _dot(tri, h) + _dot(tri, m) + _dot(tri, l)


def _tri_right(v, tri):
    h, m, l = _split3(v)
    return _dot(h, tri) + _dot(m, tri) + _dot(l, tri)


def _matmul(a, b, mode, out_dtype, name, halves=None):
    assert a.dtype == BF16 and b.dtype == BF16, (name, a.dtype, b.dtype)
    if halves == "a":
        assert mode == "nt" and a.shape[0] == 2 and a.shape[2] == D_FF
        m, k, n = a.shape[1], 2 * D_FF, b.shape[0]
    elif halves == "b":
        assert mode == "tn" and b.shape[0] == 2 and b.shape[2] == D_FF
        (k, m), n = a.shape, 2 * D_FF
    elif mode == "nn":
        (m, k), n = a.shape, b.shape[1]
    elif mode == "nt":
        (m, k), n = a.shape, b.shape[0]
    else:
        (k, m), n = a.shape, b.shape[1]
    tm = m if m <= 1024 else _pick(m, (1024, 1408, 512, 256, 128))
    tn = n if n <= 1024 else _pick(n, (1408, 640, 512, 256, 128))
    tk = k if k <= 1024 else _pick(k, (1408, 1024, 640, 512, 256, 128))
    nk = k // tk
    if mode == "nn":
        dn = (((1,), (0,)), ((), ()))
        a_spec = pl.BlockSpec((tm, tk), lambda i, j, kk: (i, kk))
        b_spec = pl.BlockSpec((tk, tn), lambda i, j, kk: (kk, j))
    elif mode == "nt":
        dn = (((1,), (1,)), ((), ()))
        a_spec = pl.BlockSpec((tm, tk), lambda i, j, kk: (i, kk))
        b_spec = pl.BlockSpec((tn, tk), lambda i, j, kk: (j, kk))
    else:
        dn = (((0,), (0,)), ((), ()))
        a_spec = pl.BlockSpec((tk, tm), lambda i, j, kk: (kk, i))
        b_spec = pl.BlockSpec((tk, tn), lambda i, j, kk: (kk, j))
    per_half = D_FF // FF_TILE
    if halves == "a":
        assert tk == FF_TILE
        a_spec = pl.BlockSpec((None, tm, tk), lambda i, j, kk: (kk // per_half, i, kk % per_half))
    elif halves == "b":
        assert tn == FF_TILE
        b_spec = pl.BlockSpec((None, tk, tn), lambda i, j, kk: (j // per_half, kk, j % per_half))

    def body(a_ref, b_ref, o_ref, *acc):
        d = lax.dot_general(a_ref[...], b_ref[...], dn, preferred_element_type=F32)
        if nk == 1:
            o_ref[...] = d.astype(o_ref.dtype)
            return
        acc_ref, = acc
        kk = pl.program_id(2)

        @pl.when(kk == 0)
        def _():
            acc_ref[...] = d

        @pl.when((kk > 0) & (kk < nk - 1))
        def _():
            acc_ref[...] += d

        @pl.when(kk == nk - 1)
        def _():
            o_ref[...] = (acc_ref[...] + d).astype(o_ref.dtype)

    return pl.pallas_call(
        body, name=name, grid=(m // tm, n // tn, nk),
        in_specs=[a_spec, b_spec],
        out_specs=pl.BlockSpec((tm, tn), lambda i, j, kk: (i, j)),
        out_shape=jax.ShapeDtypeStruct((m, n), out_dtype),
        scratch_shapes=[pltpu.VMEM((tm, tn), F32)] if nk > 1 else [],
        compiler_params=_cp("parallel", "parallel", "arbitrary"),
    )(a, b)


def _rows(body, name, t, tb, row_in, full_in, row_out, acc_out):
    in_specs, args = [], []
    for r in row_in:
        if isinstance(r, tuple):
            arr, w, j = r
            in_specs.append(pl.BlockSpec((tb, w), functools.partial(lambda i, jj: (i, jj), jj=j)))
            args.append(arr)
        else:
            in_specs.append(pl.BlockSpec((tb, r.shape[1]), lambda i: (i, 0)))
            args.append(r)
    for f in full_in:
        in_specs.append(pl.BlockSpec(f.shape, functools.partial(lambda i, nd: (0,) * nd, nd=f.ndim)))
        args.append(f)
    out_specs = [pl.BlockSpec((tb, c), lambda i: (i, 0)) for c, _ in row_out]
    out_specs += [pl.BlockSpec(s, functools.partial(lambda i, nd: (0,) * nd, nd=len(s))) for s in acc_out]
    out_shape = [jax.ShapeDtypeStruct((t, c), dt) for c, dt in row_out]
    out_shape += [jax.ShapeDtypeStruct(s, F32) for s in acc_out]
    return pl.pallas_call(
        body, name=name, grid=(t // tb,), in_specs=in_specs, out_specs=out_specs, out_shape=out_shape,
        compiler_params=_cp("arbitrary"),
    )(*args)


def _ln_stats(r):
    mu = jnp.mean(r, axis=-1, keepdims=True)
    xc = r - mu
    var = jnp.mean(xc * xc, axis=-1, keepdims=True)
    rstd = lax.rsqrt(var + LN_EPS)
    return xc * rstd, rstd


def _ln_bwd(dout, xhat, rstd, g):
    dxh = dout * g
    m1 = jnp.mean(dxh, axis=-1, keepdims=True)
    m2 = jnp.mean(dxh * xhat, axis=-1, keepdims=True)
    return rstd * (dxh - m1 - xhat * m2)


def _first(i, acc_refs):
    @pl.when(i == 0)
    def _():
        for a in acc_refs:
            a[...] = jnp.zeros_like(a)


def _modulated(xv, m_ref):
    return (xv * (1.0 + m_ref[1:2, :]) + m_ref[0:1, :]).astype(BF16)


def _ln_in_fwd(x, g, b, next_mod3):
    t = x.shape[0]

    def body(x_ref, g_ref, b_ref, m_ref, o_ref, h_ref):
        xhat, _ = _ln_stats(x_ref[...])
        out = xhat * g_ref[...] + b_ref[...]
        o_ref[...] = out
        h_ref[...] = _modulated(out, m_ref)

    return _rows(body, "ln_in_fwd", t, 256, [x], [g, b, next_mod3], [(D, F32), (D, BF16)], [])


def _ln_in_bwd(dx0, x, g):
    t = x.shape[0]

    def body(d_ref, x_ref, g_ref, o_ref, acc_ref):
        _first(pl.program_id(0), [acc_ref])
        xhat, rstd = _ln_stats(x_ref[...])
        d = d_ref[...]
        o_ref[...] = _ln_bwd(d, xhat, rstd, g_ref[...])
        acc_ref[0:1, :] += jnp.sum(d * xhat, axis=0, keepdims=True)
        acc_ref[1:2, :] += jnp.sum(d, axis=0, keepdims=True)

    return _rows(body, "ln_in_bwd", t, 256, [dx0, x], [g], [(D, F32)], [(2, D)])


def _modulate_bwd(dxres, dh, xin, mod3, name):
    t = xin.shape[0]

    def body(r_ref, dh_ref, x_ref, m_ref, o_ref, acc_ref):
        _first(pl.program_id(0), [acc_ref])
        dh_v = dh_ref[...]
        o_ref[...] = r_ref[...] + dh_v * (1.0 + m_ref[1:2, :])
        acc_ref[0:1, :] += jnp.sum(dh_v, axis=0, keepdims=True)
        acc_ref[1:2, :] += jnp.sum(dh_v * x_ref[...], axis=0, keepdims=True)

    return _rows(body, name, t, 256, [dxres, dh, xin], [mod3], [(D, F32)], [(2, D)])


def _ffn_in_swiglu(h, w_in, name):
    t = h.shape[0]
    tm = _pick(t, (512, 256, 128))
    nj = D_FF // FF_TILE

    def body(h_ref, wg_ref, wu_ref, u_ref, a_ref):
        hv = h_ref[...]
        gate = _dot(hv, wg_ref[...])
        up = _dot(hv, wu_ref[...])
        sg = _sigmoid(gate)
        silu = gate * sg
        u_ref[0] = (up * (sg + silu * (1.0 - sg))).astype(BF16)
        u_ref[1] = silu.astype(BF16)
        a_ref[...] = (silu * up).astype(BF16)

    return pl.pallas_call(
        body, name=name, grid=(nj, t // tm),
        in_specs=[pl.BlockSpec((tm, D), lambda j, i: (i, 0)),
                  pl.BlockSpec((D, FF_TILE), lambda j, i: (0, j)),
                  pl.BlockSpec((D, FF_TILE), lambda j, i: (0, nj + j))],
        out_specs=[pl.BlockSpec((2, tm, FF_TILE), lambda j, i: (0, i, j)),
                   pl.BlockSpec((tm, FF_TILE), lambda j, i: (i, j))],
        out_shape=[jax.ShapeDtypeStruct((2, t, D_FF), BF16), jax.ShapeDtypeStruct((t, D_FF), BF16)],
        compiler_params=_cp("parallel", "parallel"),
    )(h, w_in, w_in)


def _ffn_out_dx_swiglu(dy, w_out, u, name):
    t = dy.shape[0]
    tm = _pick(t, (512, 256, 128))
    nj = D_FF // FF_TILE

    def body(dy_ref, w_ref, u_ref, du_ref):
        da = _dot_nt(dy_ref[...], w_ref[...])
        du_ref[0] = (da * u_ref[0].astype(F32)).astype(BF16)
        du_ref[1] = (da * u_ref[1].astype(F32)).astype(BF16)

    blk3 = pl.BlockSpec((2, tm, FF_TILE), lambda j, i: (0, i, j))
    return pl.pallas_call(
        body, name=name, grid=(nj, t // tm),
        in_specs=[pl.BlockSpec((tm, D), lambda j, i: (i, 0)),
                  pl.BlockSpec((FF_TILE, D), lambda j, i: (j, 0)), blk3],
        out_specs=blk3,
        out_shape=jax.ShapeDtypeStruct((2, t, D_FF), BF16),
        compiler_params=_cp("parallel", "parallel"),
    )(dy, w_out, u)


def _res_ln(xin, y, mod3, lg, lb, factor, name, next_mod3=None):
    t = xin.shape[0]

    def body(x_ref, y_ref, m_ref, g_ref, b_ref, *rest):
        r = ALPHA * x_ref[...] + (factor * m_ref[2:3, :]) * y_ref[...]
        xhat, _ = _ln_stats(r)
        out = xhat * g_ref[...] + b_ref[...]
        if next_mod3 is None:
            rest[0][...] = out
        else:
            rest[1][...] = out
            rest[2][...] = _modulated(out, rest[0])

    if next_mod3 is None:
        return _rows(body, name, t, 256, [xin, y], [mod3, lg, lb], [(D, F32)], [])[0], None
    return _rows(body, name, t, 256, [xin, y], [mod3, lg, lb, next_mod3], [(D, F32), (D, BF16)], [])


def _mod_res_bwd(dxres, dh, mod3, xin_p, y_p, mod3_p, lg_p, lb_p, factor_p, name):
    t = dxres.shape[0]

    def body(r_ref, dh_ref, xp_ref, yp_ref, m_ref, mp_ref, g_ref, b_ref, dres_ref, dy_ref, acc_ref):
        _first(pl.program_id(0), [acc_ref])
        gate = factor_p * mp_ref[2:3, :]
        yv = yp_ref[...]
        xhat, rstd = _ln_stats(ALPHA * xp_ref[...] + gate * yv)
        xin = xhat * g_ref[...] + b_ref[...]
        dh_v = dh_ref[...]
        d = r_ref[...] + dh_v * (1.0 + m_ref[1:2, :])
        dr = _ln_bwd(d, xhat, rstd, g_ref[...])
        dres_ref[...] = ALPHA * dr
        dy_ref[...] = (gate * dr).astype(BF16)
        acc_ref[0:1, :] += jnp.sum(dh_v, axis=0, keepdims=True)
        acc_ref[1:2, :] += jnp.sum(dh_v * xin, axis=0, keepdims=True)
        acc_ref[2:3, :] += jnp.sum(d * xhat, axis=0, keepdims=True)
        acc_ref[3:4, :] += jnp.sum(d, axis=0, keepdims=True)
        acc_ref[4:5, :] += jnp.sum(factor_p * yv * dr, axis=0, keepdims=True)

    return _rows(body, name, t, 256, [dxres, dh, xin_p, y_p], [mod3, mod3_p, lg_p, lb_p],
                 [(D, F32), (D, BF16)], [(5, D)])


def _res_ln_bwd(dout, xin, y, mod3, lg, factor, name):
    t = xin.shape[0]

    def body(d_ref, x_ref, y_ref, m_ref, g_ref, dres_ref, dy_ref, acc_ref):
        _first(pl.program_id(0), [acc_ref])
        gate = factor * m_ref[2:3, :]
        yv = y_ref[...]
        r = ALPHA * x_ref[...] + gate * yv
        xhat, rstd = _ln_stats(r)
        d = d_ref[...]
        dr = _ln_bwd(d, xhat, rstd, g_ref[...])
        dres_ref[...] = ALPHA * dr
        dy_ref[...] = (gate * dr).astype(BF16)
        acc_ref[0:1, :] += jnp.sum(d * xhat, axis=0, keepdims=True)
        acc_ref[1:2, :] += jnp.sum(d, axis=0, keepdims=True)
        acc_ref[2:3, :] += jnp.sum(factor * yv * dr, axis=0, keepdims=True)

    return _rows(body, name, t, 256, [dout, xin, y], [mod3, lg], [(D, F32), (D, BF16)], [(3, D)])


def _loss_head(xf, tgt):
    t = xf.shape[0]

    def body(x_ref, t_ref, d_ref, acc_ref):
        _first(pl.program_id(0), [acc_ref])
        e = x_ref[...] - t_ref[...]
        d_ref[...] = e * (1.0 / D)
        part = 0.5 * jnp.sum(jnp.mean(e * e, axis=-1, keepdims=True), axis=0, keepdims=True)
        acc_ref[...] += jnp.broadcast_to(part, acc_ref.shape)

    return _rows(body, "loss_head", t, 256, [xf, tgt], [], [(D, F32)], [(1, LANES)])


def _silu_bf16(c_all):
    def body(c_ref, o_ref):
        v = c_ref[...]
        o_ref[...] = (v * _sigmoid(v)).astype(BF16)

    return _rows(body, "silu_c", c_all.shape[0], c_all.shape[0], [c_all], [], [(c_all.shape[1], BF16)], [])[0]


def _sum_rows(v, name):
    r, n = v.shape
    tn = _pick(n, (8192, 4096, 2048, 1024, 512, 256, 128))

    def body(v_ref, o_ref):
        acc = v_ref[0:1, :]
        for k in range(1, r):
            acc = acc + v_ref[k:k + 1, :]
        o_ref[...] = acc

    return pl.pallas_call(
        body, name=name, grid=(n // tn,),
        in_specs=[pl.BlockSpec((r, tn), lambda j: (0, j))],
        out_specs=pl.BlockSpec((1, tn), lambda j: (0, j)),
        out_shape=jax.ShapeDtypeStruct((1, n), F32),
        compiler_params=_cp("parallel"),
    )(v)


def _elementwise(fn, name, ins, out_dtypes):
    r, c = ins[0].shape
    tb = _pick(r, (128, 64, 32, 16, 8))
    n_in = len(ins)

    def body(*refs):
        outs = fn(*[x[...] for x in refs[:n_in]])
        for o_ref, o in zip(refs[n_in:], outs):
            o_ref[...] = o.astype(o_ref.dtype)

    spec = pl.BlockSpec((tb, c), lambda i: (i, 0))
    return pl.pallas_call(
        body, name=name, grid=(r // tb,), in_specs=[spec] * n_in, out_specs=[spec] * len(out_dtypes),
        out_shape=[jax.ShapeDtypeStruct((r, c), dt) for dt in out_dtypes],
        compiler_params=_cp("parallel"),
    )(*ins)


def _adamw_math(w, g, m, v):
    m = ADAM_B1 * m + (1.0 - ADAM_B1) * g
    v = ADAM_B2 * v + (1.0 - ADAM_B2) * (g * g)
    m_hat = m / (1.0 - ADAM_B1 ** ADAM_STEP)
    v_hat = v / (1.0 - ADAM_B2 ** ADAM_STEP)
    delta = -ADAM_LR * (m_hat / (jnp.sqrt(v_hat) + ADAM_EPS) + ADAM_WD * w)
    return delta, m, v


def _adamw(w, g, m, v, name):
    shape = w.shape
    c = shape[-1]
    w2, g2, m2, v2 = (a.reshape(-1, c) for a in (w, g, m, v))
    outs = _elementwise(_adamw_math, name, [w2, g2, m2, v2], [F32, F32, F32])
    return tuple(o.reshape(shape) for o in outs)


def _remote_exchange(ins, plan, peers_of, name):
    n_in, n_out = len(ins), len(plan)

    def body(*refs):
        in_refs, out_refs = refs[:n_in], refs[n_in:n_in + n_out]
        send_sems, recv_sems = refs[n_in + n_out], refs[n_in + n_out + 1]
        peers = peers_of(lax.axis_index("x"), lax.axis_index("y"), lax.axis_index("c"))
        copies = [
            pltpu.make_async_remote_copy(
                src_ref=in_refs[src], dst_ref=out_refs[k], send_sem=send_sems.at[k], recv_sem=recv_sems.at[k],
                device_id=peers[peer], device_id_type=MESH)
            for k, (peer, src) in enumerate(plan)
        ]
        for cp in copies:
            cp.start()
        for cp in copies:
            cp.wait()

    any_spec = pl.BlockSpec(memory_space=pl.ANY)
    return list(pl.pallas_call(
        body, name=name,
        in_specs=[any_spec] * n_in, out_specs=[any_spec] * n_out,
        out_shape=[jax.ShapeDtypeStruct(ins[src].shape, ins[src].dtype) for _, src in plan],
        scratch_shapes=[pltpu.SemaphoreType.DMA((n_out,)), pltpu.SemaphoreType.DMA((n_out,))],
    )(*ins))


def _sibling(x, y, c):
    return [(x, y, 1 - c)]


def _other_chips(x, y, c):
    return [(1 - x, y, c), (x, 1 - y, c), (1 - x, 1 - y, c)]


def _swap_sibling(arrs, name):
    return _remote_exchange(arrs, [(0, i) for i in range(len(arrs))], _sibling, name)


def _bcast_chips(arrs, name):
    n = len(arrs)
    out = _remote_exchange(arrs, [(p, i) for p in range(3) for i in range(n)], _other_chips, name)
    return [out[p * n:(p + 1) * n] for p in range(3)]


def _by_chip(me, own, got3):
    by_rel = [own, got3[0], got3[1], got3[2]]
    rel_bits = (0, 2, 1, 3)

    def branch(m):
        def f(ops):
            return [ops[rel_bits.index(i ^ m)] for i in range(4)]
        return f

    return lax.switch(me, [branch(m) for m in range(4)], by_rel)


BIG = ("ffn1_w_in", "ffn1_w_out", "mix_w_in", "mix_w_out", "ffn2_w_in", "ffn2_w_out")
BY_COLUMNS = ("ffn1_w_in", "ffn2_w_in")


def _layer_shape(n, shard_shape):
    r, cs = shard_shape
    return (r, 4 * cs) if n in BY_COLUMNS else (4, r, cs)


def _chip_ids(x, y):
    chips = [(1 - x, y), (x, 1 - y), (1 - x, 1 - y)]
    return chips, [2 * cx + cy for cx, cy in chips]


def _half_slot(ref, n, chip, h):
    if n in BY_COLUMNS:
        hr, w = ref.shape[0] // 2, ref.shape[1] // 4
        return ref.at[pl.ds(pl.multiple_of(h * hr, 16), hr), pl.ds(pl.multiple_of(chip * w, LANES), w)]
    hr = ref.shape[1] // 2
    return ref.at[chip, pl.ds(pl.multiple_of(h * hr, 16), hr)]


def _gather_body(names, layer, in_refs, out_refs, sems, handshake):
    send_chip, recv_chip, send_sib, recv_sib, local, send_fwd, recv_fwd = sems
    n_w = len(names)
    x, y, c = lax.axis_index("x"), lax.axis_index("y"), lax.axis_index("c")
    me = 2 * x + y
    chips, chip_idx = _chip_ids(x, y)
    sibling = (x, y, 1 - c)
    if handshake:
        _shake_hands([sibling] + [(*ch, c) for ch in chips])
    remote = _remote

    sends, own_copies = [], []
    for i, n in enumerate(names):
        shard = in_refs[i].at[layer]
        hr = shard.shape[0] // 2
        src = shard.at[pl.ds(pl.multiple_of(c * hr, 16), hr)]
        mine = _half_slot(out_refs[i], n, me, c)
        own = pltpu.make_async_copy(src, mine, local.at[i])
        own.start()
        own_copies.append(own)
        sends.append(remote(src, mine, send_sib.at[i], recv_sib.at[i], sibling))
        sends[-1].start()
        for p in range(3):
            k = p * n_w + i
            sends.append(remote(src, mine, send_chip.at[k], recv_chip.at[k], (*chips[p], c)))
            sends[-1].start()
    for p in range(3):
        for i, n in enumerate(names):
            k = p * n_w + i
            landed = _half_slot(out_refs[i], n, chip_idx[p], c)
            remote(landed, landed, send_chip.at[k], recv_chip.at[k], sibling).wait_recv()
            sends.append(remote(landed, landed, send_fwd.at[k], recv_fwd.at[k], sibling))
            sends[-1].start()
    for i, n in enumerate(names):
        theirs = _half_slot(out_refs[i], n, me, 1 - c)
        remote(theirs, theirs, send_sib.at[i], recv_sib.at[i], sibling).wait_recv()
        for p in range(3):
            k = p * n_w + i
            theirs = _half_slot(out_refs[i], n, chip_idx[p], 1 - c)
            remote(theirs, theirs, send_fwd.at[k], recv_fwd.at[k], sibling).wait_recv()
    for own in own_copies:
        own.wait()
    for cp in sends:
        cp.wait_send()


def _gather_weights(names, shards, layer, after, name, sequencer_id):
    n_w = len(names)

    def body(in_refs, out_refs, sems, handshake):
        _gather_body(names, layer, in_refs, out_refs, sems, handshake)

    dma = pltpu.SemaphoreType.DMA
    sems = [dma((3 * n_w,)), dma((3 * n_w,)), dma((n_w,)), dma((n_w,)), dma((n_w,)), dma((3 * n_w,)), dma((3 * n_w,))]
    shapes = [jax.ShapeDtypeStruct(_layer_shape(n, s.shape[1:]), BF16) for n, s in zip(names, shards)]
    return _comm_call(body, list(shards) + list(after), shapes, sems, name, sequencer_id)


def _comm_call(body_fn, ins, out_shapes, sem_types, name, sequencer_id):
    n_in, n_out = len(ins), len(out_shapes)
    sequencer = sequencer_id is not None

    def body(*refs):
        body_fn(refs[:n_in], refs[n_in:n_in + n_out], refs[n_in + n_out:], sequencer)

    if sequencer:
        return list(pl.kernel(
            body, name=name, out_type=out_shapes,
            mesh=plsc.ScalarSubcoreMesh(axis_name="sequencer", num_cores=1), scratch_types=sem_types,
            compiler_params=pltpu.CompilerParams(collective_id=sequencer_id),
        )(*ins))
    any_spec = pl.BlockSpec(memory_space=pl.ANY)
    return list(pl.pallas_call(
        body, name=name, in_specs=[any_spec] * n_in, out_specs=[any_spec] * n_out, out_shape=out_shapes,
        scratch_shapes=sem_types,
    )(*ins))


def _shake_hands(peers):
    barrier = pltpu.get_barrier_semaphore()
    for peer in peers:
        pl.semaphore_signal(barrier, inc=1, device_id=peer, device_id_type=MESH)
    pl.semaphore_wait(barrier, len(peers))


def _remote(src, dst, s_sem, r_sem, to):
    return pltpu.make_async_remote_copy(src_ref=src, dst_ref=dst, send_sem=s_sem, recv_sem=r_sem,
                                        device_id=to, device_id_type=MESH)


def _rows_half(ref, n, h):
    if n in BY_COLUMNS:
        hr = ref.shape[0] // 2
        return ref.at[pl.ds(pl.multiple_of(h * hr, 16), hr)]
    hr = ref.shape[1] // 2
    return ref.at[:, pl.ds(pl.multiple_of(h * hr, 16), hr)]


def _half_form_shape(n, wire_shape):
    if n in BY_COLUMNS:
        return (wire_shape[0] // 2, wire_shape[1])
    return (wire_shape[0], wire_shape[1] // 2, wire_shape[2])


def _grad_swap_halves(names, wire, name, sequencer_id):
    n_w = len(names)

    def body(in_refs, out_refs, sems, handshake):
        send, recv = sems
        x, y, c = lax.axis_index("x"), lax.axis_index("y"), lax.axis_index("c")
        sibling = (x, y, 1 - c)
        if handshake:
            _shake_hands([sibling])
        cps = [_remote(_rows_half(in_refs[i], n, 1 - c), out_refs[i], send.at[i], recv.at[i], sibling)
               for i, n in enumerate(names)]
        for cp in cps:
            cp.start()
        for cp in cps:
            cp.wait()

    shapes = [jax.ShapeDtypeStruct(_half_form_shape(n, w.shape), w.dtype) for n, w in zip(names, wire)]
    dma = pltpu.SemaphoreType.DMA
    return _comm_call(body, list(wire), shapes, [dma((n_w,)), dma((n_w,))], name, sequencer_id)


def _grad_add_halves(g, got, ci_arr, after, name):
    g3 = g if g.ndim == 3 else g[None]
    r3 = got if got.ndim == 3 else got[None]
    s, hr, cc = r3.shape
    tb = _pick(hr, (256, 176, 128))
    nb = hr // tb

    def body(s_ref, g_ref, r_ref, after_ref, o_ref):
        o_ref[...] = (g_ref[...].astype(F32) + r_ref[...].astype(F32)).astype(BF16)

    out = pl.pallas_call(
        body, name=name,
        grid_spec=pltpu.PrefetchScalarGridSpec(
            num_scalar_prefetch=1, grid=(nb,),
            in_specs=[pl.BlockSpec((s, tb, cc), lambda i, sc: (0, sc[0] * nb + i, 0)),
                      pl.BlockSpec((s, tb, cc), lambda i, sc: (0, i, 0)),
                      pl.BlockSpec(memory_space=pl.ANY)],
            out_specs=pl.BlockSpec((s, tb, cc), lambda i, sc: (0, i, 0))),
        out_shape=jax.ShapeDtypeStruct(r3.shape, BF16),
        compiler_params=_cp("arbitrary"),
    )(ci_arr, g3, r3, after)
    return out.reshape(got.shape)


def _grad_scatter_halves(names, parts, name, sequencer_id):
    n_w = len(names)

    def slot(ref, n, chip):
        if n in BY_COLUMNS:
            w = ref.shape[1] // 4
            return ref.at[:, pl.ds(pl.multiple_of(chip * w, LANES), w)]
        return ref.at[chip]

    def body(in_refs, out_refs, sems, handshake):
        send, recv = sems
        x, y, c = lax.axis_index("x"), lax.axis_index("y"), lax.axis_index("c")
        chips, chip_idx = _chip_ids(x, y)
        if handshake:
            _shake_hands([(*ch, c) for ch in chips])
        cps = []
        for p in range(3):
            for i, n in enumerate(names):
                k = p * n_w + i
                cps.append(_remote(slot(in_refs[i], n, chip_idx[p]), out_refs[k], send.at[k], recv.at[k],
                                   (*chips[p], c)))
        for cp in cps:
            cp.start()
        for cp in cps:
            cp.wait()

    def slot_shape(n, a):
        return (a.shape[0], a.shape[1] // 4) if n in BY_COLUMNS else a.shape[1:]

    shapes = [jax.ShapeDtypeStruct(slot_shape(n, a), BF16) for _ in range(3) for n, a in zip(names, parts)]
    dma = pltpu.SemaphoreType.DMA
    out = _comm_call(body, list(parts), shapes, [dma((3 * n_w,)), dma((3 * n_w,))], name, sequencer_id)
    return [out[p * n_w:(p + 1) * n_w] for p in range(3)]


def _grad_add_slots(part, got3, n, me_arr, after, name):
    hr, cs = got3[0].shape
    tb = _pick(hr, (256, 176, 128))

    def body(s_ref, own_ref, a_ref, b_ref, c_ref, after_ref, out_ref):
        acc = own_ref[...].astype(F32) + a_ref[...].astype(F32)
        out_ref[...] = (acc + b_ref[...].astype(F32)) + c_ref[...].astype(F32)

    if n in BY_COLUMNS:
        own_spec = pl.BlockSpec((tb, cs), lambda i, s: (i, s[0]))
    else:
        own_spec = pl.BlockSpec((None, tb, cs), lambda i, s: (s[0], i, 0))
    plain = pl.BlockSpec((tb, cs), lambda i, s: (i, 0))
    return pl.pallas_call(
        body, name=name,
        grid_spec=pltpu.PrefetchScalarGridSpec(
            num_scalar_prefetch=1, grid=(hr // tb,),
            in_specs=[own_spec, plain, plain, plain, pl.BlockSpec(memory_space=pl.ANY)],
            out_specs=plain),
        out_shape=jax.ShapeDtypeStruct((hr, cs), F32),
        compiler_params=_cp("arbitrary"),
    )(me_arr, part, *got3, after)


def _grad_swap_reduced(halves, name, sequencer_id):
    n_w = len(halves)

    def body(in_refs, out_refs, sems, handshake):
        send, recv = sems
        x, y, c = lax.axis_index("x"), lax.axis_index("y"), lax.axis_index("c")
        sibling = (x, y, 1 - c)
        if handshake:
            _shake_hands([sibling])
        cps = [_remote(in_refs[i], out_refs[i], send.at[i], recv.at[i], sibling) for i in range(n_w)]
        for cp in cps:
            cp.start()
        for cp in cps:
            cp.wait()

    shapes = [jax.ShapeDtypeStruct(h.shape, F32) for h in halves]
    dma = pltpu.SemaphoreType.DMA
    return _comm_call(body, list(halves), shapes, [dma((n_w,)), dma((n_w,))], name, sequencer_id)


def _adamw_layers(w, own, other, m, v, ci_arr, name):
    _, rs, cs = w.shape
    hr = rs // 2
    tb = _pick(hr, (256, 176, 128))
    nbh = hr // tb

    def body(s_ref, w_ref, o0, t0, o1, t1, m_ref, v_ref, g_out, d_out, m_out, v_out):
        mine = (pl.program_id(1) // nbh) == s_ref[0]
        g = jnp.where(pl.program_id(0) == 0, jnp.where(mine, o0[...], t0[...]), jnp.where(mine, o1[...], t1[...]))
        g_out[...] = g
        d_out[...], m_out[...], v_out[...] = _adamw_math(w_ref[...], g, m_ref[...], v_ref[...])

    both = pl.BlockSpec((None, tb, cs), lambda l, i, s: (l, i, 0))

    def half(layer, is_own):
        def index(l, i, s):
            own_block = (i // nbh) == s[0]
            use = (l == layer) & (own_block if is_own else jnp.logical_not(own_block))
            return (jnp.where(use, i % nbh, 0), 0)
        return pl.BlockSpec((tb, cs), index)

    return pl.pallas_call(
        body, name=name,
        grid_spec=pltpu.PrefetchScalarGridSpec(
            num_scalar_prefetch=1, grid=(DEPTH, rs // tb),
            in_specs=[both, half(0, True), half(0, False), half(1, True), half(1, False), both, both],
            out_specs=[both] * 4),
        out_shape=[jax.ShapeDtypeStruct(w.shape, F32)] * 4,
        compiler_params=_cp("arbitrary", "arbitrary"),
    )(ci_arr, w, own[0], other[0], own[1], other[1], m, v)


def _shift_down(v, s, t_iota):
    return jnp.where(t_iota >= s, pltpu.roll(v, s, 0), 0.0)


def _shift_up(v, s, t_iota, t):
    return jnp.where(t_iota < t - s, pltpu.roll(v, t - s, 0), 0.0)


def _ssd_conv_fwd(proj, w, b):
    t = proj.shape[0]
    k_w = SSD_CONV_K

    def body(x_ref, w_ref, b_ref, o_ref):
        x = x_ref[...]
        ti = lax.broadcasted_iota(jnp.int32, x.shape, 0)
        pre = x * w_ref[k_w - 1:k_w, :] + b_ref[...]
        for s in range(1, k_w):
            pre = pre + _shift_down(x, s, ti) * w_ref[k_w - 1 - s:k_w - s, :]
        o_ref[...] = pre * _sigmoid(pre)

    off = P_XBC // LANES
    return pl.pallas_call(
        body, name="ssd_conv_fwd", grid=(SSD_CONV_DIM // LANES,),
        in_specs=[pl.BlockSpec((t, LANES), lambda j: (0, off + j)),
                  pl.BlockSpec((k_w, LANES), lambda j: (0, j)),
                  pl.BlockSpec((1, LANES), lambda j: (0, j))],
        out_specs=pl.BlockSpec((t, LANES), lambda j: (0, j)),
        out_shape=jax.ShapeDtypeStruct((t, SSD_CONV_DIM), F32),
        compiler_params=_cp("parallel"),
    )(proj, w, b)


def _ssd_conv_bwd(dxbc, proj, w, b):
    t = proj.shape[0]
    k_w = SSD_CONV_K

    def body(d_ref, x_ref, w_ref, b_ref, dx_ref, dw_ref, db_ref):
        x = x_ref[...]
        ti = lax.broadcasted_iota(jnp.int32, x.shape, 0)
        shifted = [x] + [_shift_down(x, s, ti) for s in range(1, k_w)]
        pre = b_ref[...] + shifted[0] * w_ref[k_w - 1:k_w, :]
        for s in range(1, k_w):
            pre = pre + shifted[s] * w_ref[k_w - 1 - s:k_w - s, :]
        sg = _sigmoid(pre)
        dpre = d_ref[...] * (sg * (1.0 + pre * (1.0 - sg)))
        db_ref[...] = jnp.sum(dpre, axis=0, keepdims=True)
        dx = dpre * w_ref[k_w - 1:k_w, :]
        for s in range(k_w):
            dw_ref[k_w - 1 - s:k_w - s, :] = jnp.sum(dpre * shifted[s], axis=0, keepdims=True)
            if s:
                dx = dx + _shift_up(dpre, s, ti, t) * w_ref[k_w - 1 - s:k_w - s, :]
        dx_ref[...] = dx

    off = P_XBC // LANES
    return pl.pallas_call(
        body, name="ssd_conv_bwd", grid=(SSD_CONV_DIM // LANES,),
        in_specs=[pl.BlockSpec((t, LANES), lambda j: (0, j)),
                  pl.BlockSpec((t, LANES), lambda j: (0, off + j)),
                  pl.BlockSpec((k_w, LANES), lambda j: (0, j)),
                  pl.BlockSpec((1, LANES), lambda j: (0, j))],
        out_specs=[pl.BlockSpec((t, LANES), lambda j: (0, j)),
                   pl.BlockSpec((k_w, LANES), lambda j: (0, j)),
                   pl.BlockSpec((1, LANES), lambda j: (0, j))],
        out_shape=[jax.ShapeDtypeStruct((t, SSD_CONV_DIM), F32),
                   jax.ShapeDtypeStruct((k_w, SSD_CONV_DIM), F32),
                   jax.ShapeDtypeStruct((1, SSD_CONV_DIM), F32)],
        compiler_params=_cp("parallel"),
    )(dxbc, proj, w, b)


def _shortconv_fwd(proj, w):
    t = proj.shape[0]
    nb = SC_W // LANES
    off = P_SC // LANES

    def body(b_ref, c_ref, x_ref, w_ref, o_ref):
        u = c_ref[...] * x_ref[...]
        ti = lax.broadcasted_iota(jnp.int32, u.shape, 0)
        cv = u * w_ref[SC_K - 1:SC_K, :]
        for s in range(1, SC_K):
            cv = cv + _shift_down(u, s, ti) * w_ref[SC_K - 1 - s:SC_K - s, :]
        o_ref[...] = (b_ref[...] * cv).astype(BF16)

    return pl.pallas_call(
        body, name="shortconv_fwd", grid=(nb,),
        in_specs=[pl.BlockSpec((t, LANES), lambda j: (0, off + j)),
                  pl.BlockSpec((t, LANES), lambda j: (0, off + nb + j)),
                  pl.BlockSpec((t, LANES), lambda j: (0, off + 2 * nb + j)),
                  pl.BlockSpec((SC_K, LANES), lambda j: (0, j))],
        out_specs=pl.BlockSpec((t, LANES), lambda j: (0, j)),
        out_shape=jax.ShapeDtypeStruct((t, SC_W), BF16),
        compiler_params=_cp("parallel"),
    )(proj, proj, proj, w)


def _shortconv_bwd(dy, dy_off, proj, w):
    t = proj.shape[0]
    nb = SC_W // LANES
    off = P_SC // LANES
    doff = dy_off // LANES

    def body(d_ref, b_ref, c_ref, x_ref, w_ref, db_ref, dc_ref, dx_ref, dw_ref):
        cg, xin = c_ref[...], x_ref[...]
        u = cg * xin
        ti = lax.broadcasted_iota(jnp.int32, u.shape, 0)
        shifted = [u] + [_shift_down(u, s, ti) for s in range(1, SC_K)]
        cv = shifted[0] * w_ref[SC_K - 1:SC_K, :]
        for s in range(1, SC_K):
            cv = cv + shifted[s] * w_ref[SC_K - 1 - s:SC_K - s, :]
        d = d_ref[...]
        db_ref[...] = (d * cv).astype(BF16)
        dcv = d * b_ref[...]
        du = dcv * w_ref[SC_K - 1:SC_K, :]
        for s in range(SC_K):
            dw_ref[SC_K - 1 - s:SC_K - s, :] = jnp.sum(dcv * shifted[s], axis=0, keepdims=True)
            if s:
                du = du + _shift_up(dcv, s, ti, t) * w_ref[SC_K - 1 - s:SC_K - s, :]
        dc_ref[...] = (du * xin).astype(BF16)
        dx_ref[...] = (du * cg).astype(BF16)

    tile = pl.BlockSpec((t, LANES), lambda j: (0, j))
    outs = pl.pallas_call(
        body, name="shortconv_bwd", grid=(nb,),
        in_specs=[pl.BlockSpec((t, LANES), lambda j: (0, doff + j)),
                  pl.BlockSpec((t, LANES), lambda j: (0, off + j)),
                  pl.BlockSpec((t, LANES), lambda j: (0, off + nb + j)),
                  pl.BlockSpec((t, LANES), lambda j: (0, off + 2 * nb + j)),
                  pl.BlockSpec((SC_K, LANES), lambda j: (0, j))],
        out_specs=[tile, tile, tile, pl.BlockSpec((SC_K, LANES), lambda j: (0, j))],
        out_shape=[jax.ShapeDtypeStruct((t, SC_W), BF16)] * 3 + [jax.ShapeDtypeStruct((SC_K, SC_W), F32)],
        compiler_params=_cp("parallel"),
    )(dy, proj, proj, proj, w)
    return outs


def _cum_logf(proj, fbias_row):
    t = proj.shape[0]
    blk = CUM_BLOCK

    def body(p_ref, b_ref, o_ref, carry_ref):
        i = pl.program_id(0)

        @pl.when(i == 0)
        def _():
            carry_ref[...] = jnp.zeros_like(carry_ref)

        lf = -_softplus(-(p_ref[...] + b_ref[...]))
        r = lax.broadcasted_iota(jnp.int32, (blk, blk), 0)
        c = lax.broadcasted_iota(jnp.int32, (blk, blk), 1)
        tri = (r >= c).astype(BF16)
        o_ref[...] = _tri_left(tri, lf) + carry_ref[...]
        carry_ref[...] = o_ref[blk - 1:blk, :]

    return pl.pallas_call(
        body, name="cum_logf", grid=(t // blk,),
        in_specs=[pl.BlockSpec((blk, LANES), lambda i: (i, P_PAD // LANES)),
                  pl.BlockSpec((1, LANES), lambda i: (0, 0))],
        out_specs=pl.BlockSpec((blk, LANES), lambda i: (i, 0)),
        out_shape=jax.ShapeDtypeStruct((t, LANES), F32),
        scratch_shapes=[pltpu.VMEM((1, LANES), F32)],
        compiler_params=_cp("arbitrary"),
    )(proj, fbias_row)


def _pad_block_bwd(dcf, ddt, proj, fbias_row):
    t = proj.shape[0]
    blk = CUM_BLOCK
    nb = t // blk

    def body(dcf_ref, ddt_ref, p_ref, b_ref, o_ref, db_ref, carry_ref):
        i = pl.program_id(0)

        @pl.when(i == 0)
        def _():
            carry_ref[...] = jnp.zeros_like(carry_ref)
            db_ref[...] = jnp.zeros_like(db_ref)

        r = lax.broadcasted_iota(jnp.int32, (blk, blk), 0)
        c = lax.broadcasted_iota(jnp.int32, (blk, blk), 1)
        tri = (r <= c).astype(BF16)
        rev = _tri_left(tri, dcf_ref[...]) + carry_ref[...]
        carry_ref[...] = jnp.sum(dcf_ref[...], axis=0, keepdims=True) + carry_ref[...]
        lane = lax.broadcasted_iota(jnp.int32, (blk, LANES), 1)
        is_f = (lane >= PAD_F0) & (lane < PAD_F0 + FOX_HEADS)
        df = jnp.where(is_f, rev * _sigmoid(-(p_ref[...] + b_ref[...])), 0.0)
        db_ref[...] += jnp.sum(df, axis=0, keepdims=True)
        o_ref[...] = jnp.where(lane < PAD_DT0 + SSD_HEADS, ddt_ref[...], df).astype(BF16)

    return pl.pallas_call(
        body, name="pad_block_bwd", grid=(nb,),
        in_specs=[pl.BlockSpec((blk, LANES), lambda i: (nb - 1 - i, 0)),
                  pl.BlockSpec((blk, LANES), lambda i: (nb - 1 - i, 0)),
                  pl.BlockSpec((blk, LANES), lambda i: (nb - 1 - i, P_PAD // LANES)),
                  pl.BlockSpec((1, LANES), lambda i: (0, 0))],
        out_specs=[pl.BlockSpec((blk, LANES), lambda i: (nb - 1 - i, 0)),
                   pl.BlockSpec((1, LANES), lambda i: (0, 0))],
        out_shape=[jax.ShapeDtypeStruct((t, LANES), BF16), jax.ShapeDtypeStruct((1, LANES), F32)],
        scratch_shapes=[pltpu.VMEM((1, LANES), F32)],
        compiler_params=_cp("arbitrary"),
    )(dcf, ddt, proj, fbias_row)


def _att_scores(q, k, ck, diagonal, blk):
    s = _dot_nt(q, k) - ck
    if diagonal:
        r = lax.broadcasted_iota(jnp.int32, (blk, blk), 0)
        c = lax.broadcasted_iota(jnp.int32, (blk, blk), 1)
        s = jnp.where(r >= c, s, NEG)
    return s


def _fox_fwd(q, k, v, cf_row):
    h, t, hd = q.shape
    blk = min(ATT_BLOCK, t)
    nb = t // blk
    pair = 2

    def body(q_ref, k_ref, v_ref, ck_ref, o_ref, lse_ref):
        qi = pl.program_id(1)
        qv = [q_ref[hh] for hh in range(pair)]

        def step(j, carry, diagonal):
            off = pl.multiple_of(j * blk, blk)
            out = []
            for hh in range(pair):
                m, l, acc = carry[hh]
                s = _att_scores(qv[hh], k_ref[hh, pl.ds(off, blk), :], ck_ref[hh, :, pl.ds(off, blk)], diagonal, blk)
                m_new = jnp.maximum(m, jnp.max(s, axis=1, keepdims=True))
                alpha = jnp.exp(m - m_new)
                p = jnp.exp(s - m_new)
                l = alpha * l + jnp.sum(p, axis=1, keepdims=True)
                acc = alpha * acc + _dot(p.astype(BF16), v_ref[hh, pl.ds(off, blk), :])
                out.append((m_new, l, acc))
            return tuple(out)

        one = (jnp.full((blk, 1), NEG, F32), jnp.zeros((blk, 1), F32), jnp.zeros((blk, hd), F32))
        carry = lax.fori_loop(0, qi, lambda j, cr: step(j, cr, False), (one,) * pair)
        for hh, (m, l, acc) in enumerate(step(qi, carry, True)):
            o_ref[hh] = acc / l
            lse_ref[hh] = m + jnp.log(l)

    qmap = lambda hp, i: (hp, i, 0)
    whole = lambda hp, i: (hp, 0, 0)
    return pl.pallas_call(
        body, name="fox_fwd", grid=(h // pair, nb),
        in_specs=[pl.BlockSpec((pair, blk, hd), qmap), pl.BlockSpec((pair, t, hd), whole),
                  pl.BlockSpec((pair, t, hd), whole), pl.BlockSpec((pair, 1, t), whole)],
        out_specs=[pl.BlockSpec((pair, blk, hd), qmap), pl.BlockSpec((pair, blk, 1), qmap)],
        out_shape=[jax.ShapeDtypeStruct((h, t, hd), F32), jax.ShapeDtypeStruct((h, t, 1), F32)],
        compiler_params=_cp("parallel", "arbitrary"),
    )(q, k, v, cf_row)


def _fox_bwd(q, k, v, cf_row, o, lse, do):
    h, t, hd = q.shape
    blk = min(ATT_BLOCK, t)
    nb = t // blk

    def body(q_ref, k_ref, v_ref, ck_ref, o_ref, lse_ref, do_ref,
             dq_ref, dk_ref, dv_ref, dcq_ref, dck_ref, delta_s):
        kj = pl.program_id(1)

        @pl.when(kj == 0)
        def _():
            dq_ref[...] = jnp.zeros_like(dq_ref)
            dcq_ref[...] = jnp.zeros_like(dcq_ref)

            def fill(i, _):
                rows = pl.ds(pl.multiple_of(i * blk, blk), blk)
                delta_s[rows, :] = jnp.sum(do_ref[rows, :] * o_ref[rows, :], axis=1, keepdims=True)
                return 0

            lax.fori_loop(0, nb, fill, 0)

        kb, vb, ck = k_ref[...], v_ref[...], ck_ref[...]

        def step(i, carry, diagonal):
            dk, dv, dck = carry
            rows = pl.ds(pl.multiple_of(i * blk, blk), blk)
            qb = q_ref[rows, :]
            do_b = do_ref[rows, :].astype(BF16)
            s = _att_scores(qb, kb, ck, diagonal, blk)
            p = jnp.exp(s - lse_ref[rows, :])
            dv = dv + _dot_tn(p.astype(BF16), do_b)
            ds = p * (_dot_nt(do_b, vb) - delta_s[rows, :])
            ds_b = ds.astype(BF16)
            dk = dk + _dot_tn(ds_b, qb)
            dq_ref[rows, :] += _dot(ds_b, kb)
            dcq_ref[rows, :] += jnp.sum(ds, axis=1, keepdims=True)
            dck = dck - jnp.sum(ds, axis=0, keepdims=True)
            return dk, dv, dck

        init = (jnp.zeros((blk, hd), F32), jnp.zeros((blk, hd), F32), jnp.zeros((1, blk), F32))
        carry = step(kj, init, True)
        dk, dv, dck = lax.fori_loop(kj + 1, nb, lambda i, cr: step(i, cr, False), carry)
        dk_ref[...] = dk
        dv_ref[...] = dv
        dck_ref[...] = dck

    kmap = lambda hh, j: (hh, j, 0)
    whole = lambda hh, j: (hh, 0, 0)
    return pl.pallas_call(
        body, name="fox_bwd", grid=(h, nb),
        in_specs=[pl.BlockSpec((None, t, hd), whole), pl.BlockSpec((None, blk, hd), kmap),
                  pl.BlockSpec((None, blk, hd), kmap), pl.BlockSpec((None, 1, blk), lambda hh, j: (hh, 0, j)),
                  pl.BlockSpec((None, t, hd), whole), pl.BlockSpec((None, t, 1), whole),
                  pl.BlockSpec((None, t, hd), whole)],
        out_specs=[pl.BlockSpec((None, t, hd), whole), pl.BlockSpec((None, blk, hd), kmap),
                   pl.BlockSpec((None, blk, hd), kmap), pl.BlockSpec((None, t, 1), whole),
                   pl.BlockSpec((None, 1, blk), lambda hh, j: (hh, 0, j))],
        out_shape=[jax.ShapeDtypeStruct((h, t, hd), F32), jax.ShapeDtypeStruct((h, t, hd), F32),
                   jax.ShapeDtypeStruct((h, t, hd), F32), jax.ShapeDtypeStruct((h, t, 1), F32),
                   jax.ShapeDtypeStruct((h, 1, t), F32)],
        scratch_shapes=[pltpu.VMEM((t, 1), F32)],
        compiler_params=_cp("parallel", "arbitrary"),
    )(q, k, v, cf_row, o, lse, do)


def _lane_col(v, h):
    lane = lax.broadcasted_iota(jnp.int32, v.shape, 1)
    return jnp.sum(jnp.where(lane == h, v, 0.0), axis=1, keepdims=True)


def _sub_row(v, h):
    sub = lax.broadcasted_iota(jnp.int32, v.shape, 0)
    return jnp.sum(jnp.where(sub == h, v, 0.0), axis=0, keepdims=True)


def _ssd_decays(pad, pad_t, dtb_row, alog_row, dtb_col, alog_col, blk):
    r = lax.broadcasted_iota(jnp.int32, (blk, blk), 0)
    c = lax.broadcasted_iota(jnp.int32, (blk, blk), 1)
    tril = r >= c
    dt_c = _softplus(pad + dtb_row)
    acs_c = _tri_left(tril.astype(BF16), dt_c * (-jnp.exp(alog_row)))
    dt_r = _softplus(pad_t + dtb_col)
    acs_r = _tri_right(dt_r * (-jnp.exp(alog_col)), (r <= c).astype(BF16))
    rows = lax.broadcasted_iota(jnp.int32, acs_c.shape, 0)
    acs_last = jnp.sum(jnp.where(rows == blk - 1, acs_c, 0.0), axis=0, keepdims=True)
    return dt_c, acs_c, acs_r, acs_last, tril


def _pair_terms(pair, dt_c, acs_c, acs_r, acs_last, d_row, blk):
    lane = lax.broadcasted_iota(jnp.int32, (blk, LANES), 1)
    lo = lane < 64
    lo_row = lax.broadcasted_iota(jnp.int32, (1, LANES), 1) < 64
    h0, h1 = 2 * pair, 2 * pair + 1
    col = [_lane_col(acs_c, h0), _lane_col(acs_c, h1)]
    row = [_sub_row(acs_r, h0), _sub_row(acs_r, h1)]
    last = [_lane_col(acs_last, h0), _lane_col(acs_last, h1)]
    dt_p = jnp.where(lo, _lane_col(dt_c, h0), _lane_col(dt_c, h1))
    e_p = jnp.where(lo, jnp.exp(col[0]), jnp.exp(col[1]))
    w_p = jnp.where(lo, jnp.exp(last[0] - col[0]), jnp.exp(last[1] - col[1]))
    decay_p = jnp.where(lo_row, jnp.exp(last[0]), jnp.exp(last[1]))
    d_p = jnp.where(lo_row, _lane_col(d_row, h0), _lane_col(d_row, h1))
    return lo, lo_row, col, row, last, dt_p, e_p, w_p, decay_p, d_p


def _ssd_specs(t, blk, rev):
    nc = t // blk
    ix = (lambda i: nc - 1 - i) if rev else (lambda i: i)
    xbc = pl.BlockSpec((blk, SSD_CONV_DIM), lambda i: (ix(i), 0))
    pad = pl.BlockSpec((blk, LANES), lambda i: (ix(i), P_PAD // LANES))
    pad_t = pl.BlockSpec((LANES, blk), lambda i: (0, ix(i)))
    z = pl.BlockSpec((blk, SSD_W), lambda i: (ix(i), 0))
    row = pl.BlockSpec((1, LANES), lambda i: (0, 0))
    colv = pl.BlockSpec((LANES, 1), lambda i: (0, 0))
    ng = pl.BlockSpec((1, SSD_W), lambda i: (0, 0))
    y = pl.BlockSpec((blk, SSD_W), lambda i: (ix(i), 0))
    st = pl.BlockSpec((None, 4, LANES, LANES), lambda i: (ix(i), 0, 0, 0))
    return nc, xbc, pad, pad_t, z, row, colv, ng, y, st


def _ssd_fwd(xbc, proj, pad_t, dtb_row, alog_row, d_row, dtb_col, alog_col, ng):
    t = xbc.shape[0]
    blk = min(SSD_CHUNK, t)
    nc, s_xbc, s_pad, s_padt, s_z, s_row, s_col, s_ng, s_y, s_st = _ssd_specs(t, blk, False)

    def body(xbc_ref, pad_ref, padt_ref, z_ref, dtb_ref, alog_ref, d_ref, dtbc_ref, alogc_ref, ng_ref,
             out_ref, ypre_ref, st_ref, state):
        @pl.when(pl.program_id(0) == 0)
        def _():
            state[...] = jnp.zeros_like(state)

        dt_c, acs_c, acs_r, acs_last, tril = _ssd_decays(
            pad_ref[...], padt_ref[...], dtb_ref[...], alog_ref[...], dtbc_ref[...], alogc_ref[...], blk)
        ys = []
        g_mat = {}
        for pair in range(4):
            g = pair // 2
            bg = xbc_ref[:, SSD_W + LANES * g:SSD_W + LANES * (g + 1)].astype(BF16)
            cg = xbc_ref[:, SSD_W + 2 * LANES + LANES * g:SSD_W + 2 * LANES + LANES * (g + 1)].astype(BF16)
            if g not in g_mat:
                g_mat[g] = _dot_nt(cg, bg)
            xs_p = xbc_ref[:, LANES * pair:LANES * (pair + 1)]
            lo, _, col, row, _, dt_p, e_p, w_p, decay_p, d_p = _pair_terms(
                pair, dt_c, acs_c, acs_r, acs_last, d_ref[...], blk)
            x_p = xs_p * dt_p
            y = None
            for hh in range(2):
                lm = jnp.exp(jnp.where(tril, col[hh] - row[hh], NEG))
                m_h = (g_mat[g] * lm).astype(BF16)
                x_h = jnp.where(lo if hh == 0 else ~lo, x_p, 0.0).astype(BF16)
                y_h = _dot(m_h, x_h)
                y = y_h if y is None else y + y_h
            st_in = state[pair]
            st_ref[pair] = st_in
            y = y + e_p * _dot(cg, st_in.astype(BF16))
            state[pair] = decay_p * st_in + _dot_tn(bg, (x_p * w_p).astype(BF16))
            ys.append(y + d_p * xs_p)
        y_all = jnp.concatenate(ys, axis=1)
        ypre_ref[...] = y_all
        z = z_ref[...]
        y2 = y_all * (z * _sigmoid(z))
        outs = []
        for g in range(2):
            seg = y2[:, 256 * g:256 * (g + 1)]
            rr = lax.rsqrt(jnp.mean(seg * seg, axis=-1, keepdims=True) + RMS_EPS)
            outs.append(seg * rr * ng_ref[:, 256 * g:256 * (g + 1)])
        out_ref[...] = jnp.concatenate(outs, axis=1).astype(BF16)

    return pl.pallas_call(
        body, name="ssd_scan_fwd", grid=(nc,),
        in_specs=[s_xbc, s_pad, s_padt, s_z, s_row, s_row, s_row, s_col, s_col, s_ng],
        out_specs=[s_y, s_y, s_st],
        out_shape=[jax.ShapeDtypeStruct((t, SSD_W), BF16), jax.ShapeDtypeStruct((t, SSD_W), F32),
                   jax.ShapeDtypeStruct((nc, 4, LANES, LANES), F32)],
        scratch_shapes=[pltpu.VMEM((4, LANES, LANES), F32)],
        compiler_params=_cp("arbitrary"),
    )(xbc, proj, pad_t, proj, dtb_row, alog_row, d_row, dtb_col, alog_col, ng)


def _ssd_bwd(dout, dout_off, xbc, proj, pad_t, ypre, states, dtb_row, alog_row, d_row, dtb_col, alog_col, ng):
    t = xbc.shape[0]
    blk = min(SSD_CHUNK, t)
    nc, s_xbc, s_pad, s_padt, s_z, s_row, s_col, s_ng, s_y, s_st = _ssd_specs(t, blk, True)
    s_dout = pl.BlockSpec((blk, SSD_W), lambda i: (nc - 1 - i, dout_off // SSD_W))

    def body(dout_ref, xbc_ref, pad_ref, padt_ref, z_ref, ypre_ref, st_ref, dtb_ref, alog_ref, d_ref,
             dtbc_ref, alogc_ref, ng_ref, dxbc_ref, ddt_ref, dz_ref, acc_ref, dng_ref, dstate):
        @pl.when(pl.program_id(0) == 0)
        def _():
            dstate[...] = jnp.zeros_like(dstate)
            acc_ref[...] = jnp.zeros_like(acc_ref)
            dng_ref[...] = jnp.zeros_like(dng_ref)

        pad = pad_ref[...]
        dt_c, acs_c, acs_r, acs_last, tril = _ssd_decays(
            pad, padt_ref[...], dtb_ref[...], alog_ref[...], dtbc_ref[...], alogc_ref[...], blk)
        a_row = -jnp.exp(alog_ref[...])

        z = z_ref[...]
        sz = _sigmoid(z)
        silu_z = z * sz
        y_pre = ypre_ref[...]
        y2 = y_pre * silu_z
        dy2 = []
        for g in range(2):
            sl = slice(256 * g, 256 * (g + 1))
            seg = y2[:, sl]
            rr = lax.rsqrt(jnp.mean(seg * seg, axis=-1, keepdims=True) + RMS_EPS)
            nrm = seg * rr
            d_seg = dout_ref[:, sl]
            dng_ref[:, sl] += jnp.sum(d_seg * nrm, axis=0, keepdims=True)
            dn = d_seg * ng_ref[:, sl]
            dy2.append(rr * (dn - nrm * jnp.mean(dn * nrm, axis=-1, keepdims=True)))
        dy2 = jnp.concatenate(dy2, axis=1)
        dz_ref[...] = (dy2 * y_pre * (sz * (1.0 + z * (1.0 - sz)))).astype(BF16)
        dy_all = dy2 * silu_z

        lane_row = lax.broadcasted_iota(jnp.int32, (1, LANES), 1)
        lane_blk = lax.broadcasted_iota(jnp.int32, (blk, LANES), 1)
        row_col = lax.broadcasted_iota(jnp.int32, (blk, 1), 0)
        ddt = jnp.zeros((blk, LANES), F32)
        dacs = jnp.zeros((blk, LANES), F32)
        dd_row = jnp.zeros((1, LANES), F32)
        ones_b = jnp.ones((blk, LANES), BF16)
        dxs = []
        d_b = [None, None]
        d_c = [None, None]
        d_g = [None, None]
        bgs, cgs = {}, {}
        g_mat = {}
        for pair in range(4):
            g = pair // 2
            if g not in g_mat:
                bgs[g] = xbc_ref[:, SSD_W + LANES * g:SSD_W + LANES * (g + 1)].astype(BF16)
                cgs[g] = xbc_ref[:, SSD_W + 2 * LANES + LANES * g:SSD_W + 2 * LANES + LANES * (g + 1)].astype(BF16)
                g_mat[g] = _dot_nt(cgs[g], bgs[g])
            bg, cg = bgs[g], cgs[g]
            xs_p = xbc_ref[:, LANES * pair:LANES * (pair + 1)]
            lo, lo_row, col, row, last, dt_p, e_p, w_p, decay_p, d_p = _pair_terms(
                pair, dt_c, acs_c, acs_r, acs_last, d_ref[...], blk)
            x_p = xs_p * dt_p
            dy_p = dy_all[:, LANES * pair:LANES * (pair + 1)]
            st_in = st_ref[pair]
            dst = dstate[pair]
            dx_diag = None
            for hh in range(2):
                sel = lo if hh == 0 else ~lo
                lm = jnp.exp(jnp.where(tril, col[hh] - row[hh], NEG))
                m_f = g_mat[g] * lm
                m_h = m_f.astype(BF16)
                x_h = jnp.where(sel, x_p, 0.0).astype(BF16)
                dy_h = jnp.where(sel, dy_p, 0.0).astype(BF16)
                dxd = _dot_tn(m_h, dy_h)
                dm = _dot_nt(dy_h, x_h)
                dg_h = dm * lm
                p_b = (dm * m_f).astype(BF16)
                dacs = dacs + jnp.where(lane_blk == 2 * pair + hh, _dot(p_b, ones_b) - _dot_tn(p_b, ones_b), 0.0)
                dx_diag = dxd if dx_diag is None else dx_diag + dxd
                d_g[g] = dg_h if d_g[g] is None else d_g[g] + dg_h
            st_b = st_in.astype(BF16)
            dst_b = dst.astype(BF16)
            y_off = e_p * _dot(cg, st_b)
            edy = (e_p * dy_p).astype(BF16)
            dc_off = _dot_nt(edy, st_b)
            d_c[g] = dc_off if d_c[g] is None else d_c[g] + dc_off
            dstate[pair] = decay_p * dst + _dot_tn(cg, edy)
            dx_state = _dot(bg, dst_b) * w_p
            db_st = _dot_nt((x_p * w_p).astype(BF16), dst_b)
            d_b[g] = db_st if d_b[g] is None else d_b[g] + db_st
            dx = dx_diag + dx_state
            dxs.append(dx * dt_p + d_p * dy_p)
            prod_dt = dx * xs_p
            prod_acs = dy_p * y_off - x_p * dx_state
            prod_st = x_p * dx_state
            prod_d = dy_p * xs_p
            st_prod = jnp.sum(dst * st_in, axis=0, keepdims=True)
            for hh in range(2):
                h = 2 * pair + hh
                sel = lo if hh == 0 else ~lo
                sel_row = lo_row if hh == 0 else ~lo_row
                ddt_h = jnp.sum(jnp.where(sel, prod_dt, 0.0), axis=1, keepdims=True)
                dacs_h = jnp.sum(jnp.where(sel, prod_acs, 0.0), axis=1, keepdims=True)
                tail = jnp.sum(jnp.sum(jnp.where(sel, prod_st, 0.0), axis=1, keepdims=True), axis=0, keepdims=True)
                tail = tail + jnp.exp(last[hh]) * jnp.sum(jnp.where(sel_row, st_prod, 0.0), axis=1, keepdims=True)
                dacs_h = dacs_h + jnp.where(row_col == blk - 1, tail, 0.0)
                dd_h = jnp.sum(jnp.sum(jnp.where(sel, prod_d, 0.0), axis=1, keepdims=True), axis=0, keepdims=True)
                ddt = ddt + jnp.where(lane_blk == h, ddt_h, 0.0)
                dacs = dacs + jnp.where(lane_blk == h, dacs_h, 0.0)
                dd_row = dd_row + jnp.where(lane_row == h, dd_h, 0.0)
        for g in range(2):
            dg_b = d_g[g].astype(BF16)
            d_c[g] = d_c[g] + _dot(dg_b, bgs[g])
            d_b[g] = d_b[g] + _dot_tn(dg_b, cgs[g])
        r = lax.broadcasted_iota(jnp.int32, (blk, blk), 0)
        c = lax.broadcasted_iota(jnp.int32, (blk, blk), 1)
        da = _tri_left((r <= c).astype(BF16), dacs)
        ddt = ddt + da * a_row
        d_raw = ddt * _sigmoid(pad + dtb_ref[...])
        ddt_ref[...] = d_raw
        acc_ref[0:1, :] += jnp.sum(da * dt_c, axis=0, keepdims=True) * a_row
        acc_ref[1:2, :] += dd_row
        acc_ref[2:3, :] += jnp.sum(d_raw, axis=0, keepdims=True)
        dxbc_ref[...] = jnp.concatenate(dxs + d_b + d_c, axis=1)

    return pl.pallas_call(
        body, name="ssd_scan_bwd", grid=(nc,),
        in_specs=[s_dout, s_xbc, s_pad, s_padt, s_z, s_y, s_st, s_row, s_row, s_row, s_col, s_col, s_ng],
        out_specs=[s_xbc, pl.BlockSpec((blk, LANES), lambda i: (nc - 1 - i, 0)), s_y,
                   pl.BlockSpec((8, LANES), lambda i: (0, 0)), s_ng],
        out_shape=[jax.ShapeDtypeStruct((t, SSD_CONV_DIM), F32), jax.ShapeDtypeStruct((t, LANES), F32),
                   jax.ShapeDtypeStruct((t, SSD_W), BF16), jax.ShapeDtypeStruct((8, LANES), F32),
                   jax.ShapeDtypeStruct((1, SSD_W), F32)],
        scratch_shapes=[pltpu.VMEM((4, LANES, LANES), F32)],
        compiler_params=_cp("arbitrary"),
    )(dout, xbc, proj, pad_t, proj, ypre, states, dtb_row, alog_row, d_row, dtb_col, alog_col, ng)


def _pad_lanes(v, off):
    return jnp.zeros((1, LANES), F32).at[0, off:off + v.shape[0]].set(v)


def _perm_mix_w_in(w):
    z, xbc, dt = w[:, 0:512], w[:, 512:1536], w[:, 1536:1544]
    qkv, f, sc = w[:, 1544:2312], w[:, 2312:2316], w[:, 2316:3084]
    padblk = jnp.zeros((w.shape[0], LANES), w.dtype).at[:, PAD_DT0:PAD_DT0 + 8].set(dt).at[:, PAD_F0:PAD_F0 + 4].set(f)
    return jnp.concatenate([z, xbc, qkv, sc, padblk], axis=1)


def _unperm_mix_w_in(wp):
    z, xbc, qkv, sc = wp[:, 0:512], wp[:, 512:1536], wp[:, 1536:2304], wp[:, 2304:3072]
    dt, f = wp[:, P_PAD + PAD_DT0:P_PAD + PAD_DT0 + 8], wp[:, P_PAD + PAD_F0:P_PAD + PAD_F0 + 4]
    return jnp.concatenate([z, xbc, dt, qkv, f, sc], axis=1)


def _heads(m):
    return jnp.transpose(m.reshape(m.shape[0], FOX_HEADS, FOX_HD), (1, 0, 2))


def _unheads(m):
    return jnp.transpose(m, (1, 0, 2)).reshape(m.shape[1], FOX_W)


def _ffn_fwd(h, w_in, w_out, tag):
    u, a = _ffn_in_swiglu(h, w_in, f"ffn_in_{tag}")
    y = _matmul(a, w_out, "nn", F32, f"ffn_out_{tag}")
    return y, (h, u, a)


def _ffn_bwd(dy, saved, w_in, w_out, tag):
    h, u, a = saved
    du = _ffn_out_dx_swiglu(dy, w_out, u, f"ffn_out_dx_{tag}")
    dw_out = _matmul(a, dy, "tn", BF16, f"ffn_out_dw_{tag}")
    dh = _matmul(du, w_in, "nt", F32, f"ffn_in_dx_{tag}", halves="a")
    dw_in = _matmul(h, du, "tn", BF16, f"ffn_in_dw_{tag}", halves="b")
    return dh, dw_in, dw_out


def _mix_fwd(h, wp, w_out, sp, tag):
    proj = _matmul(h, wp, "nn", F32, f"mix_in_{tag}")
    pad_t = jnp.transpose(proj[:, P_PAD:P_PAD + LANES])
    xbc = _ssd_conv_fwd(proj, sp["conv_w"], sp["conv_b"])
    y_ssd, ypre, states = _ssd_fwd(xbc, proj, pad_t, sp["dtb_row"], sp["alog_row"], sp["d_row"],
                                   sp["dtb_col"], sp["alog_col"], sp["ng"])
    cf = _cum_logf(proj, sp["fbias_row"])
    cf_row = jnp.transpose(cf[:, PAD_F0:PAD_F0 + FOX_HEADS])[:, None, :]
    qkv = lax.optimization_barrier(proj[:, P_QKV:P_QKV + 768])
    q = _heads((qkv[:, 0:256] * FOX_SCALE).astype(BF16))
    k = _heads(qkv[:, 256:512].astype(BF16))
    v = _heads(qkv[:, 512:768].astype(BF16))
    o, lse = _fox_fwd(q, k, v, cf_row)
    y_sc = _shortconv_fwd(proj, sp["sconv_w"])
    ymix = jnp.concatenate([y_ssd, _unheads(o).astype(BF16), y_sc], axis=1)
    y = _matmul(ymix, w_out, "nn", F32, f"mix_out_{tag}")
    return y, (h, proj, pad_t, xbc, ypre, states, q, k, v, cf_row, o, lse, ymix)


def _mix_bwd(dy, saved, wp, w_out, sp, tag):
    h, proj, pad_t, xbc, ypre, states, q, k, v, cf_row, o, lse, ymix = saved
    dymix = _matmul(dy, w_out, "nt", F32, f"mix_out_dx_{tag}")
    dw_out = _matmul(ymix, dy, "tn", BF16, f"mix_out_dw_{tag}")
    dxbc, ddt, dz, ssd_acc, dng = _ssd_bwd(dymix, 0, xbc, proj, pad_t, ypre, states, sp["dtb_row"],
                                           sp["alog_row"], sp["d_row"], sp["dtb_col"], sp["alog_col"], sp["ng"])
    dxbc_raw, dconv_w, dconv_b = _ssd_conv_bwd(dxbc, proj, sp["conv_w"], sp["conv_b"])
    do = _heads(dymix[:, SSD_W:SSD_W + FOX_W])
    dq, dk, dv, dcq, dck = _fox_bwd(q, k, v, cf_row, o, lse, do)
    dq = dq * FOX_SCALE
    dcf4 = dcq[:, :, 0] + dck[:, 0, :]
    dcf = jnp.zeros((h.shape[0], LANES), F32).at[:, PAD_F0:PAD_F0 + FOX_HEADS].set(jnp.transpose(dcf4))
    dpad, dfb = _pad_block_bwd(dcf, ddt, proj, sp["fbias_row"])
    dscb, dscc, dscx, dsconv_w = _shortconv_bwd(dymix, SSD_W + FOX_W, proj, sp["sconv_w"])
    dproj = jnp.concatenate([dz, dxbc_raw.astype(BF16), _unheads(dq).astype(BF16), _unheads(dk).astype(BF16),
                             _unheads(dv).astype(BF16), dscb, dscc, dscx, dpad], axis=1)
    dh = _matmul(dproj, wp, "nt", F32, f"mix_in_dx_{tag}")
    dwp = _matmul(h, dproj, "tn", BF16, f"mix_in_dw_{tag}")
    small = dict(conv_w=dconv_w, conv_b=dconv_b[0], dt_bias=ssd_acc[2, 0:8], a_log=ssd_acc[0, 0:8],
                 d=ssd_acc[1, 0:8], norm_g=dng[0], f_bias=dfb[0, PAD_F0:PAD_F0 + FOX_HEADS], sconv_w=dsconv_w)
    return dh, _unperm_mix_w_in(dwp), dw_out, small


def _local_step(x, tgt, mod, weights_of, small_p, before_sub_backward=None, after_sub_backward=None):
    row = lambda v: v.reshape(1, -1)
    subs = [(l, j) for l in range(DEPTH) for j in range(N_SUB)]
    factor = (0.5, 1.0, 0.5)
    w_names = (("ffn1_w_in", "ffn1_w_out"), ("mix_w_in", "mix_w_out"), ("ffn2_w_in", "ffn2_w_out"))
    lg = [[row(small_p["ln_g"][l, j]) for j in range(N_SUB)] for l in range(DEPTH)]
    lb = [[row(small_p["ln_b"][l, j]) for j in range(N_SUB)] for l in range(DEPTH)]
    sps = []
    for l in range(DEPTH):
        sp = dict(
            conv_w=small_p["ssd_conv_w"][l], conv_b=row(small_p["ssd_conv_b"][l]),
            dtb_row=_pad_lanes(small_p["ssd_dt_bias"][l], PAD_DT0), alog_row=_pad_lanes(small_p["ssd_a_log"][l], 0),
            d_row=_pad_lanes(small_p["ssd_d"][l], 0), ng=row(small_p["ssd_norm_g"][l]),
            fbias_row=_pad_lanes(small_p["fox_f_bias"][l], PAD_F0), sconv_w=small_p["sconv_w"][l])
        sp["dtb_col"] = jnp.transpose(sp["dtb_row"])
        sp["alog_col"] = jnp.transpose(sp["alog_row"])
        sps.append(sp)
    tags = [f"l{l}{('f1', 'mx', 'f2')[j]}" for l, j in subs]

    x0, h = _ln_in_fwd(x, row(small_p["ln_in_g"]), row(small_p["ln_in_b"]), mod[0, 0])
    cur = x0
    xins, ys, inner, weights = [], [], [], []
    for idx, (l, j) in enumerate(subs):
        w_in, w_out = weights_of(l, j, h)
        if idx > 0:
            (w_in, w_out), (cur, h) = lax.optimization_barrier(((w_in, w_out), (cur, h)))
        w_out = w_out.reshape(-1, w_out.shape[-1])
        if j == 1:
            w_in = _perm_mix_w_in(jnp.concatenate([w_in[s] for s in range(4)], axis=1))
        weights.append((w_in, w_out))
        if j == 1:
            y, sv = _mix_fwd(h, w_in, w_out, sps[l], tags[idx])
        else:
            y, sv = _ffn_fwd(h, w_in, w_out, tags[idx])
        nxt = mod[subs[idx + 1]] if idx + 1 < len(subs) else None
        xins.append(cur)
        ys.append(y)
        inner.append(sv)
        cur, h = _res_ln(cur, y, mod[l, j], lg[l][j], lb[l][j], factor[j], f"res_ln_{tags[idx]}", nxt)

    dcur, loss_acc = _loss_head(cur, tgt)
    last = len(subs) - 1
    l, j = subs[last]
    dres, dy, acc = _res_ln_bwd(dcur, xins[last], ys[last], mod[l, j], lg[l][j], factor[j], f"res_ln_bwd_{tags[last]}")
    ln_acc = {last: acc}
    shift_scale = {}
    big_grads = [dict() for _ in range(DEPTH)]
    small_g = [None] * DEPTH
    for idx in reversed(range(len(subs))):
        l, j = subs[idx]
        w_in, w_out = weights[idx]
        if before_sub_backward is not None:
            before_sub_backward(l, j, dy)
        if j == 1:
            dh, g_in, g_out, small_g[l] = _mix_bwd(dy, inner[idx], w_in, w_out, sps[l], tags[idx])
        else:
            dh, g_in, g_out = _ffn_bwd(dy, inner[idx], w_in, w_out, tags[idx])
        big_grads[l][w_names[j][0]], big_grads[l][w_names[j][1]] = g_in, g_out
        if after_sub_backward is not None:
            after_sub_backward(l, j, big_grads[l], dh)
        if idx > 0:
            pl_, pj = subs[idx - 1]
            dres, dy, acc5 = _mod_res_bwd(dres, dh, mod[l, j], xins[idx - 1], ys[idx - 1], mod[pl_, pj], lg[pl_][pj],
                                          lb[pl_][pj], factor[pj], f"mod_res_bwd_{tags[idx]}")
            shift_scale[idx], ln_acc[idx - 1] = acc5[0:2], acc5[2:5]
        else:
            dx0, shift_scale[0] = _modulate_bwd(dres, dh, x0, mod[0, 0], "mod_bwd_first")
    dx, acc_in = _ln_in_bwd(dx0, x, row(small_p["ln_in_g"]))
    dmod = []
    for l in range(DEPTH):
        ids = [N_SUB * l + j for j in range(N_SUB)]
        small_g[l]["ln_g"] = jnp.stack([ln_acc[i][0] for i in ids])
        small_g[l]["ln_b"] = jnp.stack([ln_acc[i][1] for i in ids])
        dmod.append(jnp.stack([jnp.concatenate([shift_scale[i], ln_acc[i][2:3]], axis=0) for i in ids]))
    return loss_acc[0, 0], dx, big_grads, small_g, jnp.stack(dmod), acc_in


SMALL_ORDER = ("ssd_conv_w", "ssd_conv_b", "ssd_dt_bias", "ssd_a_log", "ssd_d", "ssd_norm_g", "fox_f_bias",
               "sconv_w", "ln_g", "ln_b")
SMALL_KEY = dict(ssd_conv_w="conv_w", ssd_conv_b="conv_b", ssd_dt_bias="dt_bias", ssd_a_log="a_log", ssd_d="d",
                 ssd_norm_g="norm_g", fox_f_bias="f_bias", sconv_w="sconv_w", ln_g="ln_g", ln_b="ln_b")
COL_SHARDED_SMALL = ("ssd_conv_w", "sconv_w", "ln_g", "ln_b")


def _pad_to(v, n):
    return jnp.concatenate([v, jnp.zeros((n - v.shape[0],), v.dtype)])


def kernel(x, c, ln_in_g, ln_in_b, ada_w, ada_b, ffn1_w_in, ffn1_w_out, mix_w_in, mix_w_out, ssd_conv_w, ssd_conv_b, ssd_dt_bias, ssd_a_log, ssd_d, ssd_norm_g, fox_f_bias, sconv_w, ffn2_w_in, ffn2_w_out, ln_g, ln_b, loss_target, m_ln_in_g, m_ln_in_b, m_ada_w, m_ada_b, m_ffn1_w_in, m_ffn1_w_out, m_mix_w_in, m_mix_w_out, m_ssd_conv_w, m_ssd_conv_b, m_ssd_dt_bias, m_ssd_a_log, m_ssd_d, m_ssd_norm_g, m_fox_f_bias, m_sconv_w, m_ffn2_w_in, m_ffn2_w_out, m_ln_g, m_ln_b, v_ln_in_g, v_ln_in_b, v_ada_w, v_ada_b, v_ffn1_w_in, v_ffn1_w_out, v_mix_w_in, v_mix_w_out, v_ssd_conv_w, v_ssd_conv_b, v_ssd_dt_bias, v_ssd_a_log, v_ssd_d, v_ssd_norm_g, v_fox_f_bias, v_sconv_w, v_ffn2_w_in, v_ffn2_w_out, v_ln_g, v_ln_b):
    names = ("ln_in_g", "ln_in_b", "ada_w", "ada_b", "ffn1_w_in", "ffn1_w_out", "mix_w_in", "mix_w_out",
             "ssd_conv_w", "ssd_conv_b", "ssd_dt_bias", "ssd_a_log", "ssd_d", "ssd_norm_g", "fox_f_bias", "sconv_w",
             "ffn2_w_in", "ffn2_w_out", "ln_g", "ln_b")
    w_loc = dict(zip(names, (ln_in_g, ln_in_b, ada_w, ada_b, ffn1_w_in, ffn1_w_out, mix_w_in, mix_w_out, ssd_conv_w,
                             ssd_conv_b, ssd_dt_bias, ssd_a_log, ssd_d, ssd_norm_g, fox_f_bias, sconv_w, ffn2_w_in,
                             ffn2_w_out, ln_g, ln_b)))
    m_loc = dict(zip(names, (m_ln_in_g, m_ln_in_b, m_ada_w, m_ada_b, m_ffn1_w_in, m_ffn1_w_out, m_mix_w_in,
                             m_mix_w_out, m_ssd_conv_w, m_ssd_conv_b, m_ssd_dt_bias, m_ssd_a_log, m_ssd_d,
                             m_ssd_norm_g, m_fox_f_bias, m_sconv_w, m_ffn2_w_in, m_ffn2_w_out, m_ln_g, m_ln_b)))
    v_loc = dict(zip(names, (v_ln_in_g, v_ln_in_b, v_ada_w, v_ada_b, v_ffn1_w_in, v_ffn1_w_out, v_mix_w_in,
                             v_mix_w_out, v_ssd_conv_w, v_ssd_conv_b, v_ssd_dt_bias, v_ssd_a_log, v_ssd_d,
                             v_ssd_norm_g, v_fox_f_bias, v_sconv_w, v_ffn2_w_in, v_ffn2_w_out, v_ln_g, v_ln_b)))

    xi, yi, ci = lax.axis_index("x"), lax.axis_index("y"), lax.axis_index("c")
    me = 2 * xi + yi
    dev = 2 * me + ci

    def gather8(v, tag):
        v2 = v.reshape(1, -1)
        got = _bcast_chips([v2], f"gather_chips_{tag}")
        same_c = jnp.concatenate(_by_chip(me, v2, [g[0] for g in got]), axis=0)
        other_c = _swap_sibling([same_c], f"gather_sibling_{tag}")[0]
        pair = lax.switch(ci, [lambda a, b: jnp.stack([a, b], axis=1), lambda a, b: jnp.stack([b, a], axis=1)],
                          same_c, other_c)
        return pair.reshape(8, -1)

    def chip_concat(own, got3, axis):
        return jnp.concatenate(_by_chip(me, own, got3), axis=axis)

    small_cols = [w_loc[n].reshape(-1, w_loc[n].shape[-1]) for n in COL_SHARDED_SMALL]
    got = _bcast_chips(small_cols, "gather_small_params")
    small_p = {n: w_loc[n] for n in ("ln_in_g", "ln_in_b", "ssd_conv_b", "ssd_dt_bias", "ssd_a_log", "ssd_d",
                                     "ssd_norm_g", "fox_f_bias")}
    for i, n in enumerate(COL_SHARDED_SMALL):
        full = chip_concat(small_cols[i], [g[i] for g in got], 1)
        small_p[n] = full.reshape(w_loc[n].shape[:-1] + (full.shape[-1],))

    assert DEPTH == 2
    first, rest = BIG[:2], BIG[2:]
    shard_of = {n: w_loc[n].astype(BF16) for n in BIG}

    def gather(names, layer, after, tag, sequencer_id):
        out = _gather_weights(names, [shard_of[n] for n in names], layer, after, f"gather_weights_{tag}", sequencer_id)
        return dict(zip(names, out))

    gathered = {0: gather(first, 0, [], "l0a", 12)}

    def weights_of(l, j, marker):
        if (l, j) == (0, 0):
            gathered[0].update(gather(rest, 0, [marker], "l0b", 1))
        elif (l, j) == (0, 1):
            gathered[1] = gather(BIG, 1, [marker], "l1", 5)
        pair = BIG[2 * j:2 * j + 2]
        return gathered[l][pair[0]], gathered[l][pair[1]]

    c_all = gather8(c[0], "c")
    c_act = _silu_bf16(c_all)
    ada_w_b = ada_w.astype(BF16)
    mod_loc = [_matmul(c_act, ada_w_b[l], "nn", F32, f"ada_fwd_l{l}") for l in range(DEPTH)]
    mod_loc = jnp.stack(mod_loc)
    got = _bcast_chips([mod_loc], "gather_mod")
    mod_all = chip_concat(mod_loc, [g[0] for g in got], 2)
    mod = lax.dynamic_index_in_dim(mod_all, dev, 1, keepdims=False) + ada_b
    mod = mod.reshape(DEPTH, N_SUB, 3, D)

    ci_arr = jnp.reshape(ci, (1,)).astype(jnp.int32)
    me_arr = jnp.reshape(me, (1,)).astype(jnp.int32)

    def wire_form(n, g):
        if n in BY_COLUMNS:
            return g
        if n == "mix_w_in":
            return jnp.transpose(g.reshape(g.shape[0], 4, g.shape[1] // 4), (1, 0, 2))
        return g.reshape(4, g.shape[0] // 4, g.shape[1])

    units, reduced = {}, {}

    def start_unit(tag, l, names, g, ids):
        wire = [wire_form(n, g[n]) for n in names]
        units[tag] = dict(l=l, names=names, wire=wire, ids=ids,
                          got=_grad_swap_halves(names, wire, f"grad_swap_halves_{tag}", ids[0]))

    def add_and_scatter(tag, after):
        st = units[tag]
        st["part"] = [_grad_add_halves(st["wire"][i], st["got"][i], ci_arr, after, f"grad_add_halves_{tag}_{n}")
                      for i, n in enumerate(st["names"])]
        st["from_chips"] = _grad_scatter_halves(st["names"], st["part"], f"grad_scatter_halves_{tag}", st["ids"][1])

    def finish_unit(tag, after):
        st = units[tag]
        halves = [_grad_add_slots(st["part"][i], [g[i] for g in st["from_chips"]], n, me_arr, after,
                                  f"grad_add_slots_{tag}_{n}") for i, n in enumerate(st["names"])]
        other = _grad_swap_reduced(halves, f"grad_swap_reduced_{tag}", st["ids"][2])
        for i, n in enumerate(st["names"]):
            reduced[(st["l"], n)] = (halves[i], other[i])

    second, third = BIG[2:4], BIG[4:]

    def after_sub_backward(l, j, g, marker):
        if (l, j) == (1, 0):
            start_unit("l1", 1, BIG, g, (2, 3, 4))
        elif (l, j) == (0, 2):
            start_unit("l0f2", 0, third, g, (6, 7, 8))
        elif (l, j) == (0, 1):
            start_unit("l0mx", 0, second, g, (9, 10, 11))

    def before_sub_backward(l, j, marker):
        if (l, j) == (0, 2):
            add_and_scatter("l1", marker)
        elif (l, j) == (0, 1):
            add_and_scatter("l0f2", marker)
        elif (l, j) == (0, 0):
            add_and_scatter("l0mx", marker)

    loss_part, dx, big_g, small_g, dmod, acc_in = _local_step(
        x[0], loss_target[0], mod, weights_of, small_p, before_sub_backward, after_sub_backward)
    loss = lax.psum(loss_part, ("x", "y", "c"))

    pieces = [dmod.reshape(-1), acc_in[0], acc_in[1]]
    for n in SMALL_ORDER:
        pieces.append(jnp.stack([small_g[l][SMALL_KEY[n]] for l in range(DEPTH)]).reshape(-1))
    sizes = [p.shape[0] for p in pieces]
    total = sum(sizes)
    padded = -(-total // 1024) * 1024
    vec = _pad_to(jnp.concatenate(pieces), padded)
    all_rows = gather8(vec, "small_grads")
    summed = _sum_rows(all_rows, "sum_small_grads")[0]
    offs = [0]
    for s in sizes:
        offs.append(offs[-1] + s)
    n_mod = sizes[0]
    grads = {"ada_b": summed[0:n_mod].reshape(DEPTH, 3 * N_SUB * D),
             "ln_in_g": summed[offs[1]:offs[2]], "ln_in_b": summed[offs[2]:offs[3]]}
    for i, n in enumerate(SMALL_ORDER):
        full = summed[offs[3 + i]:offs[4 + i]].reshape(small_p[n].shape)
        if n in COL_SHARDED_SMALL:
            wcols = w_loc[n].shape[-1]
            full = lax.dynamic_slice_in_dim(full, me * wcols, wcols, axis=full.ndim - 1)
        grads[n] = full

    dmod_all = all_rows[:, 0:n_mod].reshape(8, DEPTH, 3 * N_SUB * D)
    ncol = ada_w.shape[-1]
    dmod_cols = lax.dynamic_slice_in_dim(dmod_all, me * ncol, ncol, axis=2).astype(BF16)
    grads["ada_w"] = jnp.stack([_matmul(c_act, dmod_cols[:, l], "tn", F32, f"ada_bwd_l{l}") for l in range(DEPTH)])

    finish_unit("l1", dx)
    finish_unit("l0f2", dx)
    finish_unit("l0mx", dx)
    start_unit("l0f1", 0, first, big_g[0], (None, None, None))
    add_and_scatter("l0f1", dx)
    finish_unit("l0f1", dx)

    delta, new_m, new_v = {}, {}, {}
    delta["ada_w"], new_m["ada_w"], new_v["ada_w"] = _adamw(w_loc["ada_w"], grads["ada_w"], m_loc["ada_w"],
                                                           v_loc["ada_w"], "adamw_ada_w")
    for i, n in enumerate(BIG):
        grads[n], delta[n], new_m[n], new_v[n] = _adamw_layers(
            w_loc[n], [reduced[(l, n)][0] for l in range(DEPTH)], [reduced[(l, n)][1] for l in range(DEPTH)],
            m_loc[n], v_loc[n], ci_arr, f"adamw_{n}")
    small_names = [n for n in names if n not in ("ada_w",) + BIG]
    flat = lambda d: jnp.concatenate([d[n].reshape(-1) for n in small_names])
    n_small = sum(w_loc[n].size for n in small_names)
    n_pad = -(-n_small // 1024) * 1024
    packed = [_pad_to(flat(d), n_pad).reshape(-1, LANES) for d in (w_loc, grads, m_loc, v_loc)]
    d_s, m_s, v_s = _adamw(*packed, "adamw_small")
    off = 0
    for n in small_names:
        sz = w_loc[n].size
        delta[n] = d_s.reshape(-1)[off:off + sz].reshape(w_loc[n].shape)
        new_m[n] = m_s.reshape(-1)[off:off + sz].reshape(w_loc[n].shape)
        new_v[n] = v_s.reshape(-1)[off:off + sz].reshape(w_loc[n].shape)
        off += sz

    return (loss, dx[None], *[grads[n] for n in names], *[delta[n] for n in names],
            *[new_m[n] for n in names], *[new_v[n] for n in names])
```

```python
import functools

import jax
import jax.numpy as jnp
from jax import lax
from jax.experimental import pallas as pl
from jax.experimental.pallas import tpu as pltpu
from jax.experimental.pallas import tpu_sc as plsc

F32 = jnp.float32
BF16 = jnp.bfloat16
MESH = pl.DeviceIdType.MESH

D = 1024
DEPTH = 2
N_SUB = 3
D_FF = 2816
FF_TILE = D_FF // 2
ALPHA = (2 * DEPTH) ** 0.25
LN_EPS = 1e-5
RMS_EPS = 1e-5
SSD_W = 512
SSD_HEADS = 8
SSD_CONV_K = 4
SSD_CONV_DIM = 1024
FOX_W = 256
FOX_HEADS = 4
FOX_HD = 64
FOX_SCALE = FOX_HD ** -0.5
SC_W = 256
SC_K = 3
D_IN_PROJ = 3084
P_Z, P_XBC, P_QKV, P_SC, P_PAD = 0, 512, 1536, 2304, 3072
D_PROJ_PAD = 3200
PAD_DT0, PAD_F0 = 0, 8
SSD_CHUNK = 256
ATT_BLOCK = 512
KEY_SPLIT = 2
CUM_BLOCK = 256
LANES = 128
VMEM_LIMIT = 56 * 1024 * 1024

ADAM_LR, ADAM_B1, ADAM_B2, ADAM_EPS, ADAM_WD, ADAM_STEP = 0.001, 0.9, 0.999, 1e-08, 0.01, 10
NEG = -1e30


def _cp(*sem):
    return pltpu.CompilerParams(dimension_semantics=sem, vmem_limit_bytes=VMEM_LIMIT)


def _pick(n, cands):
    for c in cands:
        if n % c == 0:
            return c
    return n


def _dot(a, b):
    return lax.dot_general(a, b, (((1,), (0,)), ((), ())), preferred_element_type=F32)


def _dot_nt(a, b):
    return lax.dot_general(a, b, (((1,), (1,)), ((), ())), preferred_element_type=F32)


def _dot_tn(a, b):
    return lax.dot_general(a, b, (((0,), (0,)), ((), ())), preferred_element_type=F32)


def _sigmoid(x):
    return 0.5 * jnp.tanh(0.5 * x) + 0.5


def _softplus(x):
    return jnp.maximum(x, 0.0) + jnp.log(1.0 + jnp.exp(-jnp.abs(x)))


def _split3(v):
    h = v.astype(BF16)
    r = v - h.astype(F32)
    m = r.astype(BF16)
    l = (r - m.astype(F32)).astype(BF16)
    return h, m, l


def _tri_left(tri, v):
    h, m, l = _split3(v)
    return _dot(tri, h) + _dot(tri, m) + _dot(tri, l)


def _tri_right(v, tri):
    h, m, l = _split3(v)
    return _dot(h, tri) + _dot(m, tri) + _dot(l, tri)


def _matmul(a, b, mode, out_dtype, name, halves=None):
    assert a.dtype == BF16 and b.dtype == BF16, (name, a.dtype, b.dtype)
    if halves == "a":
        assert mode == "nt" and a.shape[0] == 2 and a.shape[2] == D_FF
        m, k, n = a.shape[1], 2 * D_FF, b.shape[0]
    elif halves == "b":
        assert mode == "tn" and b.shape[0] == 2 and b.shape[2] == D_FF
        (k, m), n = a.shape, 2 * D_FF
    elif mode == "nn":
        (m, k), n = a.shape, b.shape[1]
    elif mode == "nt":
        (m, k), n = a.shape, b.shape[0]
    else:
        (k, m), n = a.shape, b.shape[1]
    tm = m if m <= 1024 else _pick(m, (1024, 1408, 512, 256, 128))
    tn = n if n <= 1024 else _pick(n, (1408, 640, 512, 256, 128))
    tk = k if k <= 1024 else _pick(k, (1408, 1024, 640, 512, 256, 128))
    nk = k // tk
    if mode == "nn":
        dn = (((1,), (0,)), ((), ()))
        a_spec = pl.BlockSpec((tm, tk), lambda i, j, kk: (i, kk))
        b_spec = pl.BlockSpec((tk, tn), lambda i, j, kk: (kk, j))
    elif mode == "nt":
        dn = (((1,), (1,)), ((), ()))
        a_spec = pl.BlockSpec((tm, tk), lambda i, j, kk: (i, kk))
        b_spec = pl.BlockSpec((tn, tk), lambda i, j, kk: (j, kk))
    else:
        dn = (((0,), (0,)), ((), ()))
        a_spec = pl.BlockSpec((tk, tm), lambda i, j, kk: (kk, i))
        b_spec = pl.BlockSpec((tk, tn), lambda i, j, kk: (kk, j))
    per_half = D_FF // FF_TILE
    if halves == "a":
        assert tk == FF_TILE
        a_spec = pl.BlockSpec((None, tm, tk), lambda i, j, kk: (kk // per_half, i, kk % per_half))
    elif halves == "b":
        assert tn == FF_TILE
        b_spec = pl.BlockSpec((None, tk, tn), lambda i, j, kk: (j // per_half, kk, j % per_half))

    def body(a_ref, b_ref, o_ref, *acc):
        d = lax.dot_general(a_ref[...], b_ref[...], dn, preferred_element_type=F32)
        if nk == 1:
            o_ref[...] = d.astype(o_ref.dtype)
            return
        acc_ref, = acc
        kk = pl.program_id(2)

        @pl.when(kk == 0)
        def _():
            acc_ref[...] = d

        @pl.when((kk > 0) & (kk < nk - 1))
        def _():
            acc_ref[...] += d

        @pl.when(kk == nk - 1)
        def _():
            o_ref[...] = (acc_ref[...] + d).astype(o_ref.dtype)

    return pl.pallas_call(
        body, name=name, grid=(m // tm, n // tn, nk),
        in_specs=[a_spec, b_spec],
        out_specs=pl.BlockSpec((tm, tn), lambda i, j, kk: (i, j)),
        out_shape=jax.ShapeDtypeStruct((m, n), out_dtype),
        scratch_shapes=[pltpu.VMEM((tm, tn), F32)] if nk > 1 else [],
        compiler_params=_cp("parallel", "parallel", "arbitrary"),
    )(a, b)


def _rows(body, name, t, tb, row_in, full_in, row_out, acc_out):
    in_specs, args = [], []
    for r in row_in:
        if isinstance(r, tuple):
            arr, w, j = r
            in_specs.append(pl.BlockSpec((tb, w), functools.partial(lambda i, jj: (i, jj), jj=j)))
            args.append(arr)
        else:
            in_specs.append(pl.BlockSpec((tb, r.shape[1]), lambda i: (i, 0)))
            args.append(r)
    for f in full_in:
        in_specs.append(pl.BlockSpec(f.shape, functools.partial(lambda i, nd: (0,) * nd, nd=f.ndim)))
        args.append(f)
    out_specs = [pl.BlockSpec((tb, c), lambda i: (i, 0)) for c, _ in row_out]
    out_specs += [pl.BlockSpec(s, functools.partial(lambda i, nd: (0,) * nd, nd=len(s))) for s in acc_out]
    out_shape = [jax.ShapeDtypeStruct((t, c), dt) for c, dt in row_out]
    out_shape += [jax.ShapeDtypeStruct(s, F32) for s in acc_out]
    return pl.pallas_call(
        body, name=name, grid=(t // tb,), in_specs=in_specs, out_specs=out_specs, out_shape=out_shape,
        compiler_params=_cp("arbitrary"),
    )(*args)


def _ln_stats(r):
    mu = jnp.mean(r, axis=-1, keepdims=True)
    xc = r - mu
    var = jnp.mean(xc * xc, axis=-1, keepdims=True)
    rstd = lax.rsqrt(var + LN_EPS)
    return xc * rstd, rstd


def _ln_bwd(dout, xhat, rstd, g):
    dxh = dout * g
    m1 = jnp.mean(dxh, axis=-1, keepdims=True)
    m2 = jnp.mean(dxh * xhat, axis=-1, keepdims=True)
    return rstd * (dxh - m1 - xhat * m2)


def _first(i, acc_refs):
    @pl.when(i == 0)
    def _():
        for a in acc_refs:
            a[...] = jnp.zeros_like(a)


def _modulated(xv, m_ref):
    return (xv * (1.0 + m_ref[1:2, :]) + m_ref[0:1, :]).astype(BF16)


def _ln_in_fwd(x, g, b, next_mod3):
    t = x.shape[0]

    def body(x_ref, g_ref, b_ref, m_ref, o_ref, h_ref):
        xhat, _ = _ln_stats(x_ref[...])
        out = xhat * g_ref[...] + b_ref[...]
        o_ref[...] = out
        h_ref[...] = _modulated(out, m_ref)

    return _rows(body, "ln_in_fwd", t, 256, [x], [g, b, next_mod3], [(D, F32), (D, BF16)], [])


def _ln_in_bwd(dx0, x, g):
    t = x.shape[0]

    def body(d_ref, x_ref, g_ref, o_ref, acc_ref):
        _first(pl.program_id(0), [acc_ref])
        xhat, rstd = _ln_stats(x_ref[...])
        d = d_ref[...]
        o_ref[...] = _ln_bwd(d, xhat, rstd, g_ref[...])
        acc_ref[0:1, :] += jnp.sum(d * xhat, axis=0, keepdims=True)
        acc_ref[1:2, :] += jnp.sum(d, axis=0, keepdims=True)

    return _rows(body, "ln_in_bwd", t, 256, [dx0, x], [g], [(D, F32)], [(2, D)])


def _modulate_bwd(dxres, dh, xin, mod3, name):
    t = xin.shape[0]

    def body(r_ref, dh_ref, x_ref, m_ref, o_ref, acc_ref):
        _first(pl.program_id(0), [acc_ref])
        dh_v = dh_ref[...]
        o_ref[...] = r_ref[...] + dh_v * (1.0 + m_ref[1:2, :])
        acc_ref[0:1, :] += jnp.sum(dh_v, axis=0, keepdims=True)
        acc_ref[1:2, :] += jnp.sum(dh_v * x_ref[...], axis=0, keepdims=True)

    return _rows(body, name, t, 256, [dxres, dh, xin], [mod3], [(D, F32)], [(2, D)])


def _ffn_in_swiglu(h, w_in, name):
    t = h.shape[0]
    tm = _pick(t, (512, 256, 128))
    nj = D_FF // FF_TILE

    def body(h_ref, wg_ref, wu_ref, u_ref, a_ref):
        hv = h_ref[...]
        gate = _dot(hv, wg_ref[...])
        up = _dot(hv, wu_ref[...])
        sg = _sigmoid(gate)
        silu = gate * sg
        u_ref[0] = (up * (sg + silu * (1.0 - sg))).astype(BF16)
        u_ref[1] = silu.astype(BF16)
        a_ref[...] = (silu * up).astype(BF16)

    return pl.pallas_call(
        body, name=name, grid=(nj, t // tm),
        in_specs=[pl.BlockSpec((tm, D), lambda j, i: (i, 0)),
                  pl.BlockSpec((D, FF_TILE), lambda j, i: (0, j)),
                  pl.BlockSpec((D, FF_TILE), lambda j, i: (0, nj + j))],
        out_specs=[pl.BlockSpec((2, tm, FF_TILE), lambda j, i: (0, i, j)),
                   pl.BlockSpec((tm, FF_TILE), lambda j, i: (i, j))],
        out_shape=[jax.ShapeDtypeStruct((2, t, D_FF), BF16), jax.ShapeDtypeStruct((t, D_FF), BF16)],
        compiler_params=_cp("parallel", "parallel"),
    )(h, w_in, w_in)


def _ffn_out_dx_swiglu(dy, w_out, u, name):
    t = dy.shape[0]
    tm = _pick(t, (512, 256, 128))
    nj = D_FF // FF_TILE

    def body(dy_ref, w_ref, u_ref, du_ref):
        da = _dot_nt(dy_ref[...], w_ref[...])
        du_ref[0] = (da * u_ref[0].astype(F32)).astype(BF16)
        du_ref[1] = (da * u_ref[1].astype(F32)).astype(BF16)

    blk3 = pl.BlockSpec((2, tm, FF_TILE), lambda j, i: (0, i, j))
    return pl.pallas_call(
        body, name=name, grid=(nj, t // tm),
        in_specs=[pl.BlockSpec((tm, D), lambda j, i: (i, 0)),
                  pl.BlockSpec((FF_TILE, D), lambda j, i: (j, 0)), blk3],
        out_specs=blk3,
        out_shape=jax.ShapeDtypeStruct((2, t, D_FF), BF16),
        compiler_params=_cp("parallel", "parallel"),
    )(dy, w_out, u)


def _res_ln(xin, y, mod3, lg, lb, factor, name, next_mod3=None):
    t = xin.shape[0]

    def body(x_ref, y_ref, m_ref, g_ref, b_ref, *rest):
        r = ALPHA * x_ref[...] + (factor * m_ref[2:3, :]) * y_ref[...]
        xhat, _ = _ln_stats(r)
        out = xhat * g_ref[...] + b_ref[...]
        if next_mod3 is None:
            rest[0][...] = out
        else:
            rest[1][...] = out
            rest[2][...] = _modulated(out, rest[0])

    if next_mod3 is None:
        return _rows(body, name, t, 256, [xin, y], [mod3, lg, lb], [(D, F32)], [])[0], None
    return _rows(body, name, t, 256, [xin, y], [mod3, lg, lb, next_mod3], [(D, F32), (D, BF16)], [])


def _mod_res_bwd(dxres, dh, mod3, xin_p, y_p, mod3_p, lg_p, lb_p, factor_p, name):
    t = dxres.shape[0]

    def body(r_ref, dh_ref, xp_ref, yp_ref, m_ref, mp_ref, g_ref, b_ref, dres_ref, dy_ref, acc_ref):
        _first(pl.program_id(0), [acc_ref])
        gate = factor_p * mp_ref[2:3, :]
        yv = yp_ref[...]
        xhat, rstd = _ln_stats(ALPHA * xp_ref[...] + gate * yv)
        xin = xhat * g_ref[...] + b_ref[...]
        dh_v = dh_ref[...]
        d = r_ref[...] + dh_v * (1.0 + m_ref[1:2, :])
        dr = _ln_bwd(d, xhat, rstd, g_ref[...])
        dres_ref[...] = ALPHA * dr
        dy_ref[...] = (gate * dr).astype(BF16)
        acc_ref[0:1, :] += jnp.sum(dh_v, axis=0, keepdims=True)
        acc_ref[1:2, :] += jnp.sum(dh_v * xin, axis=0, keepdims=True)
        acc_ref[2:3, :] += jnp.sum(d * xhat, axis=0, keepdims=True)
        acc_ref[3:4, :] += jnp.sum(d, axis=0, keepdims=True)
        acc_ref[4:5, :] += jnp.sum(factor_p * yv * dr, axis=0, keepdims=True)

    return _rows(body, name, t, 256, [dxres, dh, xin_p, y_p], [mod3, mod3_p, lg_p, lb_p],
                 [(D, F32), (D, BF16)], [(5, D)])


def _res_ln_bwd(dout, xin, y, mod3, lg, factor, name):
    t = xin.shape[0]

    def body(d_ref, x_ref, y_ref, m_ref, g_ref, dres_ref, dy_ref, acc_ref):
        _first(pl.program_id(0), [acc_ref])
        gate = factor * m_ref[2:3, :]
        yv = y_ref[...]
        r = ALPHA * x_ref[...] + gate * yv
        xhat, rstd = _ln_stats(r)
        d = d_ref[...]
        dr = _ln_bwd(d, xhat, rstd, g_ref[...])
        dres_ref[...] = ALPHA * dr
        dy_ref[...] = (gate * dr).astype(BF16)
        acc_ref[0:1, :] += jnp.sum(d * xhat, axis=0, keepdims=True)
        acc_ref[1:2, :] += jnp.sum(d, axis=0, keepdims=True)
        acc_ref[2:3, :] += jnp.sum(factor * yv * dr, axis=0, keepdims=True)

    return _rows(body, name, t, 256, [dout, xin, y], [mod3, lg], [(D, F32), (D, BF16)], [(3, D)])


def _loss_head(xf, tgt):
    t = xf.shape[0]

    def body(x_ref, t_ref, d_ref, acc_ref):
        _first(pl.program_id(0), [acc_ref])
        e = x_ref[...] - t_ref[...]
        d_ref[...] = e * (1.0 / D)
        part = 0.5 * jnp.sum(jnp.mean(e * e, axis=-1, keepdims=True), axis=0, keepdims=True)
        acc_ref[...] += jnp.broadcast_to(part, acc_ref.shape)

    return _rows(body, "loss_head", t, 256, [xf, tgt], [], [(D, F32)], [(1, LANES)])


def _silu_bf16(c_all):
    def body(c_ref, o_ref):
        v = c_ref[...]
        o_ref[...] = (v * _sigmoid(v)).astype(BF16)

    return _rows(body, "silu_c", c_all.shape[0], c_all.shape[0], [c_all], [], [(c_all.shape[1], BF16)], [])[0]


def _sum_rows(v, name):
    r, n = v.shape
    tn = _pick(n, (8192, 4096, 2048, 1024, 512, 256, 128))

    def body(v_ref, o_ref):
        acc = v_ref[0:1, :]
        for k in range(1, r):
            acc = acc + v_ref[k:k + 1, :]
        o_ref[...] = acc

    return pl.pallas_call(
        body, name=name, grid=(n // tn,),
        in_specs=[pl.BlockSpec((r, tn), lambda j: (0, j))],
        out_specs=pl.BlockSpec((1, tn), lambda j: (0, j)),
        out_shape=jax.ShapeDtypeStruct((1, n), F32),
        compiler_params=_cp("parallel"),
    )(v)


def _elementwise(fn, name, ins, out_dtypes):
    r, c = ins[0].shape
    tb = _pick(r, (128, 64, 32, 16, 8))
    n_in = len(ins)

    def body(*refs):
        outs = fn(*[x[...] for x in refs[:n_in]])
        for o_ref, o in zip(refs[n_in:], outs):
            o_ref[...] = o.astype(o_ref.dtype)

    spec = pl.BlockSpec((tb, c), lambda i: (i, 0))
    return pl.pallas_call(
        body, name=name, grid=(r // tb,), in_specs=[spec] * n_in, out_specs=[spec] * len(out_dtypes),
        out_shape=[jax.ShapeDtypeStruct((r, c), dt) for dt in out_dtypes],
        compiler_params=_cp("parallel"),
    )(*ins)


def _adamw_math(w, g, m, v):
    m = ADAM_B1 * m + (1.0 - ADAM_B1) * g
    v = ADAM_B2 * v + (1.0 - ADAM_B2) * (g * g)
    m_hat = m / (1.0 - ADAM_B1 ** ADAM_STEP)
    v_hat = v / (1.0 - ADAM_B2 ** ADAM_STEP)
    delta = -ADAM_LR * (m_hat / (jnp.sqrt(v_hat) + ADAM_EPS) + ADAM_WD * w)
    return delta, m, v


def _adamw(w, g, m, v, name):
    shape = w.shape
    c = shape[-1]
    w2, g2, m2, v2 = (a.reshape(-1, c) for a in (w, g, m, v))
    outs = _elementwise(_adamw_math, name, [w2, g2, m2, v2], [F32, F32, F32])
    return tuple(o.reshape(shape) for o in outs)


def _remote_exchange(ins, plan, peers_of, name):
    n_in, n_out = len(ins), len(plan)

    def body(*refs):
        in_refs, out_refs = refs[:n_in], refs[n_in:n_in + n_out]
        send_sems, recv_sems = refs[n_in + n_out], refs[n_in + n_out + 1]
        peers = peers_of(lax.axis_index("x"), lax.axis_index("y"), lax.axis_index("c"))
        copies = [
            pltpu.make_async_remote_copy(
                src_ref=in_refs[src], dst_ref=out_refs[k], send_sem=send_sems.at[k], recv_sem=recv_sems.at[k],
                device_id=peers[peer], device_id_type=MESH)
            for k, (peer, src) in enumerate(plan)
        ]
        for cp in copies:
            cp.start()
        for cp in copies:
            cp.wait()

    any_spec = pl.BlockSpec(memory_space=pl.ANY)
    return list(pl.pallas_call(
        body, name=name,
        in_specs=[any_spec] * n_in, out_specs=[any_spec] * n_out,
        out_shape=[jax.ShapeDtypeStruct(ins[src].shape, ins[src].dtype) for _, src in plan],
        scratch_shapes=[pltpu.SemaphoreType.DMA((n_out,)), pltpu.SemaphoreType.DMA((n_out,))],
    )(*ins))


def _sibling(x, y, c):
    return [(x, y, 1 - c)]


def _other_chips(x, y, c):
    return [(1 - x, y, c), (x, 1 - y, c), (1 - x, 1 - y, c)]


def _swap_sibling(arrs, name):
    return _remote_exchange(arrs, [(0, i) for i in range(len(arrs))], _sibling, name)


def _bcast_chips(arrs, name):
    n = len(arrs)
    out = _remote_exchange(arrs, [(p, i) for p in range(3) for i in range(n)], _other_chips, name)
    return [out[p * n:(p + 1) * n] for p in range(3)]


def _by_chip(me, own, got3):
    by_rel = [own, got3[0], got3[1], got3[2]]
    rel_bits = (0, 2, 1, 3)

    def branch(m):
        def f(ops):
            return [ops[rel_bits.index(i ^ m)] for i in range(4)]
        return f

    return lax.switch(me, [branch(m) for m in range(4)], by_rel)


BIG = ("ffn1_w_in", "ffn1_w_out", "mix_w_in", "mix_w_out", "ffn2_w_in", "ffn2_w_out")
BY_COLUMNS = ("ffn1_w_in", "ffn2_w_in")


def _layer_shape(n, shard_shape):
    r, cs = shard_shape
    return (r, 4 * cs) if n in BY_COLUMNS else (4, r, cs)


def _chip_ids(x, y):
    chips = [(1 - x, y), (x, 1 - y), (1 - x, 1 - y)]
    return chips, [2 * cx + cy for cx, cy in chips]


def _half_slot(ref, n, chip, h):
    if n in BY_COLUMNS:
        hr, w = ref.shape[0] // 2, ref.shape[1] // 4
        return ref.at[pl.ds(pl.multiple_of(h * hr, 16), hr), pl.ds(pl.multiple_of(chip * w, LANES), w)]
    hr = ref.shape[1] // 2
    return ref.at[chip, pl.ds(pl.multiple_of(h * hr, 16), hr)]


def _gather_body(names, layer, in_refs, out_refs, sems, handshake):
    send_chip, recv_chip, send_sib, recv_sib, local, send_fwd, recv_fwd = sems
    n_w = len(names)
    x, y, c = lax.axis_index("x"), lax.axis_index("y"), lax.axis_index("c")
    me = 2 * x + y
    chips, chip_idx = _chip_ids(x, y)
    sibling = (x, y, 1 - c)
    if handshake:
        _shake_hands([sibling] + [(*ch, c) for ch in chips])
    remote = _remote

    sends, own_copies = [], []
    for i, n in enumerate(names):
        shard = in_refs[i].at[layer]
        hr = shard.shape[0] // 2
        src = shard.at[pl.ds(pl.multiple_of(c * hr, 16), hr)]
        mine = _half_slot(out_refs[i], n, me, c)
        own = pltpu.make_async_copy(src, mine, local.at[i])
        own.start()
        own_copies.append(own)
        sends.append(remote(src, mine, send_sib.at[i], recv_sib.at[i], sibling))
        sends[-1].start()
        for p in range(3):
            k = p * n_w + i
            sends.append(remote(src, mine, send_chip.at[k], recv_chip.at[k], (*chips[p], c)))
            sends[-1].start()
    for p in range(3):
        for i, n in enumerate(names):
            k = p * n_w + i
            landed = _half_slot(out_refs[i], n, chip_idx[p], c)
            remote(landed, landed, send_chip.at[k], recv_chip.at[k], sibling).wait_recv()
            sends.append(remote(landed, landed, send_fwd.at[k], recv_fwd.at[k], sibling))
            sends[-1].start()
    for i, n in enumerate(names):
        theirs = _half_slot(out_refs[i], n, me, 1 - c)
        remote(theirs, theirs, send_sib.at[i], recv_sib.at[i], sibling).wait_recv()
        for p in range(3):
            k = p * n_w + i
            theirs = _half_slot(out_refs[i], n, chip_idx[p], 1 - c)
            remote(theirs, theirs, send_fwd.at[k], recv_fwd.at[k], sibling).wait_recv()
    for own in own_copies:
        own.wait()
    for cp in sends:
        cp.wait_send()


def _gather_weights(names, shards, layer, after, name, sequencer_id):
    n_w = len(names)

    def body(in_refs, out_refs, sems, handshake):
        _gather_body(names, layer, in_refs, out_refs, sems, handshake)

    dma = pltpu.SemaphoreType.DMA
    sems = [dma((3 * n_w,)), dma((3 * n_w,)), dma((n_w,)), dma((n_w,)), dma((n_w,)), dma((3 * n_w,)), dma((3 * n_w,))]
    shapes = [jax.ShapeDtypeStruct(_layer_shape(n, s.shape[1:]), BF16) for n, s in zip(names, shards)]
    return _comm_call(body, list(shards) + list(after), shapes, sems, name, sequencer_id)


def _comm_call(body_fn, ins, out_shapes, sem_types, name, sequencer_id):
    n_in, n_out = len(ins), len(out_shapes)
    sequencer = sequencer_id is not None

    def body(*refs):
        body_fn(refs[:n_in], refs[n_in:n_in + n_out], refs[n_in + n_out:], sequencer)

    if sequencer:
        return list(pl.kernel(
            body, name=name, out_type=out_shapes,
            mesh=plsc.ScalarSubcoreMesh(axis_name="sequencer", num_cores=1), scratch_types=sem_types,
            compiler_params=pltpu.CompilerParams(collective_id=sequencer_id),
        )(*ins))
    any_spec = pl.BlockSpec(memory_space=pl.ANY)
    return list(pl.pallas_call(
        body, name=name, in_specs=[any_spec] * n_in, out_specs=[any_spec] * n_out, out_shape=out_shapes,
        scratch_shapes=sem_types,
    )(*ins))


def _shake_hands(peers):
    barrier = pltpu.get_barrier_semaphore()
    for peer in peers:
        pl.semaphore_signal(barrier, inc=1, device_id=peer, device_id_type=MESH)
    pl.semaphore_wait(barrier, len(peers))


def _remote(src, dst, s_sem, r_sem, to):
    return pltpu.make_async_remote_copy(src_ref=src, dst_ref=dst, send_sem=s_sem, recv_sem=r_sem,
                                        device_id=to, device_id_type=MESH)


def _rows_half(ref, n, h):
    if n in BY_COLUMNS:
        hr = ref.shape[0] // 2
        return ref.at[pl.ds(pl.multiple_of(h * hr, 16), hr)]
    hr = ref.shape[1] // 2
    return ref.at[:, pl.ds(pl.multiple_of(h * hr, 16), hr)]


def _half_form_shape(n, wire_shape):
    if n in BY_COLUMNS:
        return (wire_shape[0] // 2, wire_shape[1])
    return (wire_shape[0], wire_shape[1] // 2, wire_shape[2])


def _grad_swap_halves(names, wire, name, sequencer_id):
    n_w = len(names)

    def body(in_refs, out_refs, sems, handshake):
        send, recv = sems
        x, y, c = lax.axis_index("x"), lax.axis_index("y"), lax.axis_index("c")
        sibling = (x, y, 1 - c)
        if handshake:
            _shake_hands([sibling])
        cps = [_remote(_rows_half(in_refs[i], n, 1 - c), out_refs[i], send.at[i], recv.at[i], sibling)
               for i, n in enumerate(names)]
        for cp in cps:
            cp.start()
        for cp in cps:
            cp.wait()

    shapes = [jax.ShapeDtypeStruct(_half_form_shape(n, w.shape), w.dtype) for n, w in zip(names, wire)]
    dma = pltpu.SemaphoreType.DMA
    return _comm_call(body, list(wire), shapes, [dma((n_w,)), dma((n_w,))], name, sequencer_id)


def _grad_add_halves(g, got, ci_arr, after, name):
    g3 = g if g.ndim == 3 else g[None]
    r3 = got if got.ndim == 3 else got[None]
    s, hr, cc = r3.shape
    tb = _pick(hr, (256, 176, 128))
    nb = hr // tb

    def body(s_ref, g_ref, r_ref, after_ref, o_ref):
        o_ref[...] = (g_ref[...].astype(F32) + r_ref[...].astype(F32)).astype(BF16)

    out = pl.pallas_call(
        body, name=name,
        grid_spec=pltpu.PrefetchScalarGridSpec(
            num_scalar_prefetch=1, grid=(nb,),
            in_specs=[pl.BlockSpec((s, tb, cc), lambda i, sc: (0, sc[0] * nb + i, 0)),
                      pl.BlockSpec((s, tb, cc), lambda i, sc: (0, i, 0)),
                      pl.BlockSpec(memory_space=pl.ANY)],
            out_specs=pl.BlockSpec((s, tb, cc), lambda i, sc: (0, i, 0))),
        out_shape=jax.ShapeDtypeStruct(r3.shape, BF16),
        compiler_params=_cp("arbitrary"),
    )(ci_arr, g3, r3, after)
    return out.reshape(got.shape)


def _grad_scatter_halves(names, parts, name, sequencer_id):
    n_w = len(names)

    def slot(ref, n, chip):
        if n in BY_COLUMNS:
            w = ref.shape[1] // 4
            return ref.at[:, pl.ds(pl.multiple_of(chip * w, LANES), w)]
        return ref.at[chip]

    def body(in_refs, out_refs, sems, handshake):
        send, recv = sems
        x, y, c = lax.axis_index("x"), lax.axis_index("y"), lax.axis_index("c")
        chips, chip_idx = _chip_ids(x, y)
        if handshake:
            _shake_hands([(*ch, c) for ch in chips])
        cps = []
        for p in range(3):
            for i, n in enumerate(names):
                k = p * n_w + i
                cps.append(_remote(slot(in_refs[i], n, chip_idx[p]), out_refs[k], send.at[k], recv.at[k],
                                   (*chips[p], c)))
        for cp in cps:
            cp.start()
        for cp in cps:
            cp.wait()

    def slot_shape(n, a):
        return (a.shape[0], a.shape[1] // 4) if n in BY_COLUMNS else a.shape[1:]

    shapes = [jax.ShapeDtypeStruct(slot_shape(n, a), BF16) for _ in range(3) for n, a in zip(names, parts)]
    dma = pltpu.SemaphoreType.DMA
    out = _comm_call(body, list(parts), shapes, [dma((3 * n_w,)), dma((3 * n_w,))], name, sequencer_id)
    return [out[p * n_w:(p + 1) * n_w] for p in range(3)]


def _grad_add_slots(part, got3, n, me_arr, after, name):
    hr, cs = got3[0].shape
    tb = _pick(hr, (256, 176, 128))

    def body(s_ref, own_ref, a_ref, b_ref, c_ref, after_ref, out_ref):
        acc = own_ref[...].astype(F32) + a_ref[...].astype(F32)
        out_ref[...] = (acc + b_ref[...].astype(F32)) + c_ref[...].astype(F32)

    if n in BY_COLUMNS:
        own_spec = pl.BlockSpec((tb, cs), lambda i, s: (i, s[0]))
    else:
        own_spec = pl.BlockSpec((None, tb, cs), lambda i, s: (s[0], i, 0))
    plain = pl.BlockSpec((tb, cs), lambda i, s: (i, 0))
    return pl.pallas_call(
        body, name=name,
        grid_spec=pltpu.PrefetchScalarGridSpec(
            num_scalar_prefetch=1, grid=(hr // tb,),
            in_specs=[own_spec, plain, plain, plain, pl.BlockSpec(memory_space=pl.ANY)],
            out_specs=plain),
        out_shape=jax.ShapeDtypeStruct((hr, cs), F32),
        compiler_params=_cp("arbitrary"),
    )(me_arr, part, *got3, after)


def _grad_swap_reduced(halves, name, sequencer_id):
    n_w = len(halves)

    def body(in_refs, out_refs, sems, handshake):
        send, recv = sems
        x, y, c = lax.axis_index("x"), lax.axis_index("y"), lax.axis_index("c")
        sibling = (x, y, 1 - c)
        if handshake:
            _shake_hands([sibling])
        cps = [_remote(in_refs[i], out_refs[i], send.at[i], recv.at[i], sibling) for i in range(n_w)]
        for cp in cps:
            cp.start()
        for cp in cps:
            cp.wait()

    shapes = [jax.ShapeDtypeStruct(h.shape, F32) for h in halves]
    dma = pltpu.SemaphoreType.DMA
    return _comm_call(body, list(halves), shapes, [dma((n_w,)), dma((n_w,))], name, sequencer_id)


def _adamw_layers(w, own, other, m, v, ci_arr, name):
    _, rs, cs = w.shape
    hr = rs // 2
    tb = _pick(hr, (256, 176, 128))
    nbh = hr // tb

    def body(s_ref, w_ref, o0, t0, o1, t1, m_ref, v_ref, g_out, d_out, m_out, v_out):
        mine = (pl.program_id(1) // nbh) == s_ref[0]
        g = jnp.where(pl.program_id(0) == 0, jnp.where(mine, o0[...], t0[...]), jnp.where(mine, o1[...], t1[...]))
        g_out[...] = g
        d_out[...], m_out[...], v_out[...] = _adamw_math(w_ref[...], g, m_ref[...], v_ref[...])

    both = pl.BlockSpec((None, tb, cs), lambda l, i, s: (l, i, 0))

    def half(layer, is_own):
        def index(l, i, s):
            own_block = (i // nbh) == s[0]
            use = (l == layer) & (own_block if is_own else jnp.logical_not(own_block))
            return (jnp.where(use, i % nbh, 0), 0)
        return pl.BlockSpec((tb, cs), index)

    return pl.pallas_call(
        body, name=name,
        grid_spec=pltpu.PrefetchScalarGridSpec(
            num_scalar_prefetch=1, grid=(DEPTH, rs // tb),
            in_specs=[both, half(0, True), half(0, False), half(1, True), half(1, False), both, both],
            out_specs=[both] * 4),
        out_shape=[jax.ShapeDtypeStruct(w.shape, F32)] * 4,
        compiler_params=_cp("arbitrary", "arbitrary"),
    )(ci_arr, w, own[0], other[0], own[1], other[1], m, v)


def _shift_down(v, s, t_iota):
    return jnp.where(t_iota >= s, pltpu.roll(v, s, 0), 0.0)


def _shift_up(v, s, t_iota, t):
    return jnp.where(t_iota < t - s, pltpu.roll(v, t - s, 0), 0.0)


def _ssd_conv_fwd(proj, w, b):
    t = proj.shape[0]
    k_w = SSD_CONV_K

    def body(x_ref, w_ref, b_ref, o_ref):
        x = x_ref[...]
        ti = lax.broadcasted_iota(jnp.int32, x.shape, 0)
        pre = x * w_ref[k_w - 1:k_w, :] + b_ref[...]
        for s in range(1, k_w):
            pre = pre + _shift_down(x, s, ti) * w_ref[k_w - 1 - s:k_w - s, :]
        o_ref[...] = pre * _sigmoid(pre)

    off = P_XBC // LANES
    return pl.pallas_call(
        body, name="ssd_conv_fwd", grid=(SSD_CONV_DIM // LANES,),
        in_specs=[pl.BlockSpec((t, LANES), lambda j: (0, off + j)),
                  pl.BlockSpec((k_w, LANES), lambda j: (0, j)),
                  pl.BlockSpec((1, LANES), lambda j: (0, j))],
        out_specs=pl.BlockSpec((t, LANES), lambda j: (0, j)),
        out_shape=jax.ShapeDtypeStruct((t, SSD_CONV_DIM), F32),
        compiler_params=_cp("parallel"),
    )(proj, w, b)


def _ssd_conv_bwd(dxbc, proj, w, b):
    t = proj.shape[0]
    k_w = SSD_CONV_K

    def body(d_ref, x_ref, w_ref, b_ref, dx_ref, dw_ref, db_ref):
        x = x_ref[...]
        ti = lax.broadcasted_iota(jnp.int32, x.shape, 0)
        shifted = [x] + [_shift_down(x, s, ti) for s in range(1, k_w)]
        pre = b_ref[...] + shifted[0] * w_ref[k_w - 1:k_w, :]
        for s in range(1, k_w):
            pre = pre + shifted[s] * w_ref[k_w - 1 - s:k_w - s, :]
        sg = _sigmoid(pre)
        dpre = d_ref[...] * (sg * (1.0 + pre * (1.0 - sg)))
        db_ref[...] = jnp.sum(dpre, axis=0, keepdims=True)
        dx = dpre * w_ref[k_w - 1:k_w, :]
        for s in range(k_w):
            dw_ref[k_w - 1 - s:k_w - s, :] = jnp.sum(dpre * shifted[s], axis=0, keepdims=True)
            if s:
                dx = dx + _shift_up(dpre, s, ti, t) * w_ref[k_w - 1 - s:k_w - s, :]
        dx_ref[...] = dx

    off = P_XBC // LANES
    return pl.pallas_call(
        body, name="ssd_conv_bwd", grid=(SSD_CONV_DIM // LANES,),
        in_specs=[pl.BlockSpec((t, LANES), lambda j: (0, j)),
                  pl.BlockSpec((t, LANES), lambda j: (0, off + j)),
                  pl.BlockSpec((k_w, LANES), lambda j: (0, j)),
                  pl.BlockSpec((1, LANES), lambda j: (0, j))],
        out_specs=[pl.BlockSpec((t, LANES), lambda j: (0, j)),
                   pl.BlockSpec((k_w, LANES), lambda j: (0, j)),
                   pl.BlockSpec((1, LANES), lambda j: (0, j))],
        out_shape=[jax.ShapeDtypeStruct((t, SSD_CONV_DIM), F32),
                   jax.ShapeDtypeStruct((k_w, SSD_CONV_DIM), F32),
                   jax.ShapeDtypeStruct((1, SSD_CONV_DIM), F32)],
        compiler_params=_cp("parallel"),
    )(dxbc, proj, w, b)


def _shortconv_fwd(proj, w):
    t = proj.shape[0]
    nb = SC_W // LANES
    off = P_SC // LANES

    def body(b_ref, c_ref, x_ref, w_ref, o_ref):
        u = c_ref[...] * x_ref[...]
        ti = lax.broadcasted_iota(jnp.int32, u.shape, 0)
        cv = u * w_ref[SC_K - 1:SC_K, :]
        for s in range(1, SC_K):
            cv = cv + _shift_down(u, s, ti) * w_ref[SC_K - 1 - s:SC_K - s, :]
        o_ref[...] = (b_ref[...] * cv).astype(BF16)

    return pl.pallas_call(
        body, name="shortconv_fwd", grid=(nb,),
        in_specs=[pl.BlockSpec((t, LANES), lambda j: (0, off + j)),
                  pl.BlockSpec((t, LANES), lambda j: (0, off + nb + j)),
                  pl.BlockSpec((t, LANES), lambda j: (0, off + 2 * nb + j)),
                  pl.BlockSpec((SC_K, LANES), lambda j: (0, j))],
        out_specs=pl.BlockSpec((t, LANES), lambda j: (0, j)),
        out_shape=jax.ShapeDtypeStruct((t, SC_W), BF16),
        compiler_params=_cp("parallel"),
    )(proj, proj, proj, w)


def _shortconv_bwd(dy, dy_off, proj, w):
    t = proj.shape[0]
    nb = SC_W // LANES
    off = P_SC // LANES
    doff = dy_off // LANES

    def body(d_ref, b_ref, c_ref, x_ref, w_ref, db_ref, dc_ref, dx_ref, dw_ref):
        cg, xin = c_ref[...], x_ref[...]
        u = cg * xin
        ti = lax.broadcasted_iota(jnp.int32, u.shape, 0)
        shifted = [u] + [_shift_down(u, s, ti) for s in range(1, SC_K)]
        cv = shifted[0] * w_ref[SC_K - 1:SC_K, :]
        for s in range(1, SC_K):
            cv = cv + shifted[s] * w_ref[SC_K - 1 - s:SC_K - s, :]
        d = d_ref[...]
        db_ref[...] = (d * cv).astype(BF16)
        dcv = d * b_ref[...]
        du = dcv * w_ref[SC_K - 1:SC_K, :]
        for s in range(SC_K):
            dw_ref[SC_K - 1 - s:SC_K - s, :] = jnp.sum(dcv * shifted[s], axis=0, keepdims=True)
            if s:
                du = du + _shift_up(dcv, s, ti, t) * w_ref[SC_K - 1 - s:SC_K - s, :]
        dc_ref[...] = (du * xin).astype(BF16)
        dx_ref[...] = (du * cg).astype(BF16)

    tile = pl.BlockSpec((t, LANES), lambda j: (0, j))
    outs = pl.pallas_call(
        body, name="shortconv_bwd", grid=(nb,),
        in_specs=[pl.BlockSpec((t, LANES), lambda j: (0, doff + j)),
                  pl.BlockSpec((t, LANES), lambda j: (0, off + j)),
                  pl.BlockSpec((t, LANES), lambda j: (0, off + nb + j)),
                  pl.BlockSpec((t, LANES), lambda j: (0, off + 2 * nb + j)),
                  pl.BlockSpec((SC_K, LANES), lambda j: (0, j))],
        out_specs=[tile, tile, tile, pl.BlockSpec((SC_K, LANES), lambda j: (0, j))],
        out_shape=[jax.ShapeDtypeStruct((t, SC_W), BF16)] * 3 + [jax.ShapeDtypeStruct((SC_K, SC_W), F32)],
        compiler_params=_cp("parallel"),
    )(dy, proj, proj, proj, w)
    return outs


def _cum_logf(proj, fbias_row):
    t = proj.shape[0]
    blk = CUM_BLOCK

    def body(p_ref, b_ref, o_ref, carry_ref):
        i = pl.program_id(0)

        @pl.when(i == 0)
        def _():
            carry_ref[...] = jnp.zeros_like(carry_ref)

        lf = -_softplus(-(p_ref[...] + b_ref[...]))
        r = lax.broadcasted_iota(jnp.int32, (blk, blk), 0)
        c = lax.broadcasted_iota(jnp.int32, (blk, blk), 1)
        tri = (r >= c).astype(BF16)
        o_ref[...] = _tri_left(tri, lf) + carry_ref[...]
        carry_ref[...] = o_ref[blk - 1:blk, :]

    return pl.pallas_call(
        body, name="cum_logf", grid=(t // blk,),
        in_specs=[pl.BlockSpec((blk, LANES), lambda i: (i, P_PAD // LANES)),
                  pl.BlockSpec((1, LANES), lambda i: (0, 0))],
        out_specs=pl.BlockSpec((blk, LANES), lambda i: (i, 0)),
        out_shape=jax.ShapeDtypeStruct((t, LANES), F32),
        scratch_shapes=[pltpu.VMEM((1, LANES), F32)],
        compiler_params=_cp("arbitrary"),
    )(proj, fbias_row)


def _pad_block_bwd(dcf, ddt, proj, fbias_row):
    t = proj.shape[0]
    blk = CUM_BLOCK
    nb = t // blk

    def body(dcf_ref, ddt_ref, p_ref, b_ref, o_ref, db_ref, carry_ref):
        i = pl.program_id(0)

        @pl.when(i == 0)
        def _():
            carry_ref[...] = jnp.zeros_like(carry_ref)
            db_ref[...] = jnp.zeros_like(db_ref)

        r = lax.broadcasted_iota(jnp.int32, (blk, blk), 0)
        c = lax.broadcasted_iota(jnp.int32, (blk, blk), 1)
        tri = (r <= c).astype(BF16)
        rev = _tri_left(tri, dcf_ref[...]) + carry_ref[...]
        carry_ref[...] = jnp.sum(dcf_ref[...], axis=0, keepdims=True) + carry_ref[...]
        lane = lax.broadcasted_iota(jnp.int32, (blk, LANES), 1)
        is_f = (lane >= PAD_F0) & (lane < PAD_F0 + FOX_HEADS)
        df = jnp.where(is_f, rev * _sigmoid(-(p_ref[...] + b_ref[...])), 0.0)
        db_ref[...] += jnp.sum(df, axis=0, keepdims=True)
        o_ref[...] = jnp.where(lane < PAD_DT0 + SSD_HEADS, ddt_ref[...], df).astype(BF16)

    return pl.pallas_call(
        body, name="pad_block_bwd", grid=(nb,),
        in_specs=[pl.BlockSpec((blk, LANES), lambda i: (nb - 1 - i, 0)),
                  pl.BlockSpec((blk, LANES), lambda i: (nb - 1 - i, 0)),
                  pl.BlockSpec((blk, LANES), lambda i: (nb - 1 - i, P_PAD // LANES)),
                  pl.BlockSpec((1, LANES), lambda i: (0, 0))],
        out_specs=[pl.BlockSpec((blk, LANES), lambda i: (nb - 1 - i, 0)),
                   pl.BlockSpec((1, LANES), lambda i: (0, 0))],
        out_shape=[jax.ShapeDtypeStruct((t, LANES), BF16), jax.ShapeDtypeStruct((1, LANES), F32)],
        scratch_shapes=[pltpu.VMEM((1, LANES), F32)],
        compiler_params=_cp("arbitrary"),
    )(dcf, ddt, proj, fbias_row)


def _att_scores(q, k, ck, diagonal, col0=0):
    s = _dot_nt(q, k) - ck
    if diagonal:
        r = lax.broadcasted_iota(jnp.int32, s.shape, 0)
        c = lax.broadcasted_iota(jnp.int32, s.shape, 1) + col0
        s = jnp.where(r >= c, s, NEG)
    return s


def _fox_fwd(q, k, v, cf_row):
    h, t, hd = q.shape
    blk = min(ATT_BLOCK, t)
    nb = t // blk
    pair = 2
    hw = blk // KEY_SPLIT

    def body(q_ref, k_ref, v_ref, ck_ref, o_ref, lse_ref):
        qi = pl.program_id(1)
        qv = [q_ref[hh] for hh in range(pair)]

        def step(j, carry, diagonal):
            out = []
            for hh in range(pair):
                m, l, acc = carry[hh]
                for c in range(KEY_SPLIT):
                    cols = pl.ds(pl.multiple_of(j * blk + c * hw, hw), hw)
                    s = _att_scores(qv[hh], k_ref[hh, cols, :], ck_ref[hh, :, cols], diagonal, c * hw)
                    m_new = jnp.maximum(m, jnp.max(s, axis=1, keepdims=True))
                    alpha = jnp.exp(m - m_new)
                    p = jnp.exp(s - m_new)
                    l = alpha * l + jnp.sum(p, axis=1, keepdims=True)
                    acc = alpha * acc + _dot(p.astype(BF16), v_ref[hh, cols, :])
                    m = m_new
                out.append((m, l, acc))
            return tuple(out)

        one = (jnp.full((blk, 1), NEG, F32), jnp.zeros((blk, 1), F32), jnp.zeros((blk, hd), F32))
        carry = lax.fori_loop(0, qi, lambda j, cr: step(j, cr, False), (one,) * pair)
        for hh, (m, l, acc) in enumerate(step(qi, carry, True)):
            o_ref[hh] = acc / l
            lse_ref[hh] = m + jnp.log(l)

    qmap = lambda hp, i: (hp, i, 0)
    whole = lambda hp, i: (hp, 0, 0)
    return pl.pallas_call(
        body, name="fox_fwd", grid=(h // pair, nb),
        in_specs=[pl.BlockSpec((pair, blk, hd), qmap), pl.BlockSpec((pair, t, hd), whole),
                  pl.BlockSpec((pair, t, hd), whole), pl.BlockSpec((pair, 1, t), whole)],
        out_specs=[pl.BlockSpec((pair, blk, hd), qmap), pl.BlockSpec((pair, blk, 1), qmap)],
        out_shape=[jax.ShapeDtypeStruct((h, t, hd), F32), jax.ShapeDtypeStruct((h, t, 1), F32)],
        compiler_params=_cp("parallel", "arbitrary"),
    )(q, k, v, cf_row)


def _fox_bwd(q, k, v, cf_row, o, lse, do):
    h, t, hd = q.shape
    blk = min(ATT_BLOCK, t)
    nb = t // blk

    def body(q_ref, k_ref, v_ref, ck_ref, o_ref, lse_ref, do_ref,
             dq_ref, dk_ref, dv_ref, dcq_ref, dck_ref, delta_s):
        kj = pl.program_id(1)

        @pl.when(kj == 0)
        def _():
            dq_ref[...] = jnp.zeros_like(dq_ref)
            dcq_ref[...] = jnp.zeros_like(dcq_ref)

            def fill(i, _):
                rows = pl.ds(pl.multiple_of(i * blk, blk), blk)
                delta_s[rows, :] = jnp.sum(do_ref[rows, :] * o_ref[rows, :], axis=1, keepdims=True)
                return 0

            lax.fori_loop(0, nb, fill, 0)

        kb, vb, ck = k_ref[...], v_ref[...], ck_ref[...]

        hw = blk // KEY_SPLIT
        halves = [(kb[c * hw:(c + 1) * hw], vb[c * hw:(c + 1) * hw], ck[:, c * hw:(c + 1) * hw])
                  for c in range(KEY_SPLIT)]

        def step(i, carry, diagonal):
            rows = pl.ds(pl.multiple_of(i * blk, blk), blk)
            qb = q_ref[rows, :]
            do_b = do_ref[rows, :].astype(BF16)
            lse, delta = lse_ref[rows, :], delta_s[rows, :]
            out, dq, dcq = [], None, None
            for c, (k_h, v_h, ck_h) in enumerate(halves):
                dk, dv, dck = carry[c]
                s = _att_scores(qb, k_h, ck_h, diagonal, c * hw)
                p = jnp.exp(s - lse)
                dv = dv + _dot_tn(p.astype(BF16), do_b)
                ds = p * (_dot_nt(do_b, v_h) - delta)
                ds_b = ds.astype(BF16)
                dk = dk + _dot_tn(ds_b, qb)
                dq_h = _dot(ds_b, k_h)
                dcq_h = jnp.sum(ds, axis=1, keepdims=True)
                dq = dq_h if dq is None else dq + dq_h
                dcq = dcq_h if dcq is None else dcq + dcq_h
                out.append((dk, dv, dck - jnp.sum(ds, axis=0, keepdims=True)))
            dq_ref[rows, :] += dq
            dcq_ref[rows, :] += dcq
            return tuple(out)

        one = (jnp.zeros((hw, hd), F32), jnp.zeros((hw, hd), F32), jnp.zeros((1, hw), F32))
        carry = step(kj, (one,) * KEY_SPLIT, True)
        carry = lax.fori_loop(kj + 1, nb, lambda i, cr: step(i, cr, False), carry)
        for c, (dk, dv, dck) in enumerate(carry):
            dk_ref[c * hw:(c + 1) * hw, :] = dk
            dv_ref[c * hw:(c + 1) * hw, :] = dv
            dck_ref[:, c * hw:(c + 1) * hw] = dck

    kmap = lambda hh, j: (hh, j, 0)
    whole = lambda hh, j: (hh, 0, 0)
    return pl.pallas_call(
        body, name="fox_bwd", grid=(h, nb),
        in_specs=[pl.BlockSpec((None, t, hd), whole), pl.BlockSpec((None, blk, hd), kmap),
                  pl.BlockSpec((None, blk, hd), kmap), pl.BlockSpec((None, 1, blk), lambda hh, j: (hh, 0, j)),
                  pl.BlockSpec((None, t, hd), whole), pl.BlockSpec((None, t, 1), whole),
                  pl.BlockSpec((None, t, hd), whole)],
        out_specs=[pl.BlockSpec((None, t, hd), whole), pl.BlockSpec((None, blk, hd), kmap),
                   pl.BlockSpec((None, blk, hd), kmap), pl.BlockSpec((None, t, 1), whole),
                   pl.BlockSpec((None, 1, blk), lambda hh, j: (hh, 0, j))],
        out_shape=[jax.ShapeDtypeStruct((h, t, hd), F32), jax.ShapeDtypeStruct((h, t, hd), F32),
                   jax.ShapeDtypeStruct((h, t, hd), F32), jax.ShapeDtypeStruct((h, t, 1), F32),
                   jax.ShapeDtypeStruct((h, 1, t), F32)],
        scratch_shapes=[pltpu.VMEM((t, 1), F32)],
        compiler_params=_cp("parallel", "arbitrary"),
    )(q, k, v, cf_row, o, lse, do)


def _lane_col(v, h):
    lane = lax.broadcasted_iota(jnp.int32, v.shape, 1)
    return jnp.sum(jnp.where(lane == h, v, 0.0), axis=1, keepdims=True)


def _sub_row(v, h):
    sub = lax.broadcasted_iota(jnp.int32, v.shape, 0)
    return jnp.sum(jnp.where(sub == h, v, 0.0), axis=0, keepdims=True)


def _ssd_decays(pad, pad_t, dtb_row, alog_row, dtb_col, alog_col, blk):
    r = lax.broadcasted_iota(jnp.int32, (blk, blk), 0)
    c = lax.broadcasted_iota(jnp.int32, (blk, blk), 1)
    tril = r >= c
    dt_c = _softplus(pad + dtb_row)
    acs_c = _tri_left(tril.astype(BF16), dt_c * (-jnp.exp(alog_row)))
    dt_r = _softplus(pad_t + dtb_col)
    acs_r = _tri_right(dt_r * (-jnp.exp(alog_col)), (r <= c).astype(BF16))
    rows = lax.broadcasted_iota(jnp.int32, acs_c.shape, 0)
    acs_last = jnp.sum(jnp.where(rows == blk - 1, acs_c, 0.0), axis=0, keepdims=True)
    return dt_c, acs_c, acs_r, acs_last, tril


def _pair_terms(pair, dt_c, acs_c, acs_r, acs_last, d_row, blk):
    lane = lax.broadcasted_iota(jnp.int32, (blk, LANES), 1)
    lo = lane < 64
    lo_row = lax.broadcasted_iota(jnp.int32, (1, LANES), 1) < 64
    h0, h1 = 2 * pair, 2 * pair + 1
    col = [_lane_col(acs_c, h0), _lane_col(acs_c, h1)]
    row = [_sub_row(acs_r, h0), _sub_row(acs_r, h1)]
    last = [_lane_col(acs_last, h0), _lane_col(acs_last, h1)]
    dt_p = jnp.where(lo, _lane_col(dt_c, h0), _lane_col(dt_c, h1))
    e_p = jnp.where(lo, jnp.exp(col[0]), jnp.exp(col[1]))
    w_p = jnp.where(lo, jnp.exp(last[0] - col[0]), jnp.exp(last[1] - col[1]))
    decay_p = jnp.where(lo_row, jnp.exp(last[0]), jnp.exp(last[1]))
    d_p = jnp.where(lo_row, _lane_col(d_row, h0), _lane_col(d_row, h1))
    return lo, lo_row, col, row, last, dt_p, e_p, w_p, decay_p, d_p


def _ssd_specs(t, blk, rev):
    nc = t // blk
    ix = (lambda i: nc - 1 - i) if rev else (lambda i: i)
    xbc = pl.BlockSpec((blk, SSD_CONV_DIM), lambda i: (ix(i), 0))
    pad = pl.BlockSpec((blk, LANES), lambda i: (ix(i), P_PAD // LANES))
    pad_t = pl.BlockSpec((LANES, blk), lambda i: (0, ix(i)))
    z = pl.BlockSpec((blk, SSD_W), lambda i: (ix(i), 0))
    row = pl.BlockSpec((1, LANES), lambda i: (0, 0))
    colv = pl.BlockSpec((LANES, 1), lambda i: (0, 0))
    ng = pl.BlockSpec((1, SSD_W), lambda i: (0, 0))
    y = pl.BlockSpec((blk, SSD_W), lambda i: (ix(i), 0))
    st = pl.BlockSpec((None, 4, LANES, LANES), lambda i: (ix(i), 0, 0, 0))
    return nc, xbc, pad, pad_t, z, row, colv, ng, y, st


def _ssd_fwd(xbc, proj, pad_t, dtb_row, alog_row, d_row, dtb_col, alog_col, ng):
    t = xbc.shape[0]
    blk = min(SSD_CHUNK, t)
    nc, s_xbc, s_pad, s_padt, s_z, s_row, s_col, s_ng, s_y, s_st = _ssd_specs(t, blk, False)

    def body(xbc_ref, pad_ref, padt_ref, z_ref, dtb_ref, alog_ref, d_ref, dtbc_ref, alogc_ref, ng_ref,
             out_ref, ypre_ref, st_ref, state):
        @pl.when(pl.program_id(0) == 0)
        def _():
            state[...] = jnp.zeros_like(state)

        dt_c, acs_c, acs_r, acs_last, tril = _ssd_decays(
            pad_ref[...], padt_ref[...], dtb_ref[...], alog_ref[...], dtbc_ref[...], alogc_ref[...], blk)
        ys = []
        g_mat = {}
        for pair in range(4):
            g = pair // 2
            bg = xbc_ref[:, SSD_W + LANES * g:SSD_W + LANES * (g + 1)].astype(BF16)
            cg = xbc_ref[:, SSD_W + 2 * LANES + LANES * g:SSD_W + 2 * LANES + LANES * (g + 1)].astype(BF16)
            if g not in g_mat:
                g_mat[g] = _dot_nt(cg, bg)
            xs_p = xbc_ref[:, LANES * pair:LANES * (pair + 1)]
            lo, _, col, row, _, dt_p, e_p, w_p, decay_p, d_p = _pair_terms(
                pair, dt_c, acs_c, acs_r, acs_last, d_ref[...], blk)
            x_p = xs_p * dt_p
            y = None
            for hh in range(2):
                lm = jnp.exp(jnp.where(tril, col[hh] - row[hh], NEG))
                m_h = (g_mat[g] * lm).astype(BF16)
                x_h = jnp.where(lo if hh == 0 else ~lo, x_p, 0.0).astype(BF16)
                y_h = _dot(m_h, x_h)
                y = y_h if y is None else y + y_h
            st_in = state[pair]
            st_ref[pair] = st_in
            y = y + e_p * _dot(cg, st_in.astype(BF16))
            state[pair] = decay_p * st_in + _dot_tn(bg, (x_p * w_p).astype(BF16))
            ys.append(y + d_p * xs_p)
        y_all = jnp.concatenate(ys, axis=1)
        ypre_ref[...] = y_all
        z = z_ref[...]
        y2 = y_all * (z * _sigmoid(z))
        outs = []
        for g in range(2):
            seg = y2[:, 256 * g:256 * (g + 1)]
            rr = lax.rsqrt(jnp.mean(seg * seg, axis=-1, keepdims=True) + RMS_EPS)
            outs.append(seg * rr * ng_ref[:, 256 * g:256 * (g + 1)])
        out_ref[...] = jnp.concatenate(outs, axis=1).astype(BF16)

    return pl.pallas_call(
        body, name="ssd_scan_fwd", grid=(nc,),
        in_specs=[s_xbc, s_pad, s_padt, s_z, s_row, s_row, s_row, s_col, s_col, s_ng],
        out_specs=[s_y, s_y, s_st],
        out_shape=[jax.ShapeDtypeStruct((t, SSD_W), BF16), jax.ShapeDtypeStruct((t, SSD_W), F32),
                   jax.ShapeDtypeStruct((nc, 4, LANES, LANES), F32)],
        scratch_shapes=[pltpu.VMEM((4, LANES, LANES), F32)],
        compiler_params=_cp("arbitrary"),
    )(xbc, proj, pad_t, proj, dtb_row, alog_row, d_row, dtb_col, alog_col, ng)


def _ssd_bwd(dout, dout_off, xbc, proj, pad_t, ypre, states, dtb_row, alog_row, d_row, dtb_col, alog_col, ng):
    t = xbc.shape[0]
    blk = min(SSD_CHUNK, t)
    nc, s_xbc, s_pad, s_padt, s_z, s_row, s_col, s_ng, s_y, s_st = _ssd_specs(t, blk, True)
    s_dout = pl.BlockSpec((blk, SSD_W), lambda i: (nc - 1 - i, dout_off // SSD_W))

    def body(dout_ref, xbc_ref, pad_ref, padt_ref, z_ref, ypre_ref, st_ref, dtb_ref, alog_ref, d_ref,
             dtbc_ref, alogc_ref, ng_ref, dxbc_ref, ddt_ref, dz_ref, acc_ref, dng_ref, dstate):
        @pl.when(pl.program_id(0) == 0)
        def _():
            dstate[...] = jnp.zeros_like(dstate)
            acc_ref[...] = jnp.zeros_like(acc_ref)
            dng_ref[...] = jnp.zeros_like(dng_ref)

        pad = pad_ref[...]
        dt_c, acs_c, acs_r, acs_last, tril = _ssd_decays(
            pad, padt_ref[...], dtb_ref[...], alog_ref[...], dtbc_ref[...], alogc_ref[...], blk)
        a_row = -jnp.exp(alog_ref[...])

        z = z_ref[...]
        sz = _sigmoid(z)
        silu_z = z * sz
        y_pre = ypre_ref[...]
        y2 = y_pre * silu_z
        dy2 = []
        for g in range(2):
            sl = slice(256 * g, 256 * (g + 1))
            seg = y2[:, sl]
            rr = lax.rsqrt(jnp.mean(seg * seg, axis=-1, keepdims=True) + RMS_EPS)
            nrm = seg * rr
            d_seg = dout_ref[:, sl]
            dng_ref[:, sl] += jnp.sum(d_seg * nrm, axis=0, keepdims=True)
            dn = d_seg * ng_ref[:, sl]
            dy2.append(rr * (dn - nrm * jnp.mean(dn * nrm, axis=-1, keepdims=True)))
        dy2 = jnp.concatenate(dy2, axis=1)
        dz_ref[...] = (dy2 * y_pre * (sz * (1.0 + z * (1.0 - sz)))).astype(BF16)
        dy_all = dy2 * silu_z

        lane_row = lax.broadcasted_iota(jnp.int32, (1, LANES), 1)
        lane_blk = lax.broadcasted_iota(jnp.int32, (blk, LANES), 1)
        row_col = lax.broadcasted_iota(jnp.int32, (blk, 1), 0)
        ddt = jnp.zeros((blk, LANES), F32)
        dacs = jnp.zeros((blk, LANES), F32)
        dd_row = jnp.zeros((1, LANES), F32)
        ones_b = jnp.ones((blk, LANES), BF16)
        dxs = []
        d_b = [None, None]
        d_c = [None, None]
        d_g = [None, None]
        bgs, cgs = {}, {}
        g_mat = {}
        for pair in range(4):
            g = pair // 2
            if g not in g_mat:
                bgs[g] = xbc_ref[:, SSD_W + LANES * g:SSD_W + LANES * (g + 1)].astype(BF16)
                cgs[g] = xbc_ref[:, SSD_W + 2 * LANES + LANES * g:SSD_W + 2 * LANES + LANES * (g + 1)].astype(BF16)
                g_mat[g] = _dot_nt(cgs[g], bgs[g])
            bg, cg = bgs[g], cgs[g]
            xs_p = xbc_ref[:, LANES * pair:LANES * (pair + 1)]
            lo, lo_row, col, row, last, dt_p, e_p, w_p, decay_p, d_p = _pair_terms(
                pair, dt_c, acs_c, acs_r, acs_last, d_ref[...], blk)
            x_p = xs_p * dt_p
            dy_p = dy_all[:, LANES * pair:LANES * (pair + 1)]
            st_in = st_ref[pair]
            dst = dstate[pair]
            dx_diag = None
            for hh in range(2):
                sel = lo if hh == 0 else ~lo
                lm = jnp.exp(jnp.where(tril, col[hh] - row[hh], NEG))
                m_f = g_mat[g] * lm
                m_h = m_f.astype(BF16)
                x_h = jnp.where(sel, x_p, 0.0).astype(BF16)
                dy_h = jnp.where(sel, dy_p, 0.0).astype(BF16)
                dxd = _dot_tn(m_h, dy_h)
                dm = _dot_nt(dy_h, x_h)
                dg_h = dm * lm
                p_b = (dm * m_f).astype(BF16)
                dacs = dacs + jnp.where(lane_blk == 2 * pair + hh, _dot(p_b, ones_b) - _dot_tn(p_b, ones_b), 0.0)
                dx_diag = dxd if dx_diag is None else dx_diag + dxd
                d_g[g] = dg_h if d_g[g] is None else d_g[g] + dg_h
            st_b = st_in.astype(BF16)
            dst_b = dst.astype(BF16)
            y_off = e_p * _dot(cg, st_b)
            edy = (e_p * dy_p).astype(BF16)
            dc_off = _dot_nt(edy, st_b)
            d_c[g] = dc_off if d_c[g] is None else d_c[g] + dc_off
            dstate[pair] = decay_p * dst + _dot_tn(cg, edy)
            dx_state = _dot(bg, dst_b) * w_p
            db_st = _dot_nt((x_p * w_p).astype(BF16), dst_b)
            d_b[g] = db_st if d_b[g] is None else d_b[g] + db_st
            dx = dx_diag + dx_state
            dxs.append(dx * dt_p + d_p * dy_p)
            prod_dt = dx * xs_p
            prod_acs = dy_p * y_off - x_p * dx_state
            prod_st = x_p * dx_state
            prod_d = dy_p * xs_p
            st_prod = jnp.sum(dst * st_in, axis=0, keepdims=True)
            for hh in range(2):
                h = 2 * pair + hh
                sel = lo if hh == 0 else ~lo
                sel_row = lo_row if hh == 0 else ~lo_row
                ddt_h = jnp.sum(jnp.where(sel, prod_dt, 0.0), axis=1, keepdims=True)
                dacs_h = jnp.sum(jnp.where(sel, prod_acs, 0.0), axis=1, keepdims=True)
                tail = jnp.sum(jnp.sum(jnp.where(sel, prod_st, 0.0), axis=1, keepdims=True), axis=0, keepdims=True)
                tail = tail + jnp.exp(last[hh]) * jnp.sum(jnp.where(sel_row, st_prod, 0.0), axis=1, keepdims=True)
                dacs_h = dacs_h + jnp.where(row_col == blk - 1, tail, 0.0)
                dd_h = jnp.sum(jnp.sum(jnp.where(sel, prod_d, 0.0), axis=1, keepdims=True), axis=0, keepdims=True)
                ddt = ddt + jnp.where(lane_blk == h, ddt_h, 0.0)
                dacs = dacs + jnp.where(lane_blk == h, dacs_h, 0.0)
                dd_row = dd_row + jnp.where(lane_row == h, dd_h, 0.0)
        for g in range(2):
            dg_b = d_g[g].astype(BF16)
            d_c[g] = d_c[g] + _dot(dg_b, bgs[g])
            d_b[g] = d_b[g] + _dot_tn(dg_b, cgs[g])
        r = lax.broadcasted_iota(jnp.int32, (blk, blk), 0)
        c = lax.broadcasted_iota(jnp.int32, (blk, blk), 1)
        da = _tri_left((r <= c).astype(BF16), dacs)
        ddt = ddt + da * a_row
        d_raw = ddt * _sigmoid(pad + dtb_ref[...])
        ddt_ref[...] = d_raw
        acc_ref[0:1, :] += jnp.sum(da * dt_c, axis=0, keepdims=True) * a_row
        acc_ref[1:2, :] += dd_row
        acc_ref[2:3, :] += jnp.sum(d_raw, axis=0, keepdims=True)
        dxbc_ref[...] = jnp.concatenate(dxs + d_b + d_c, axis=1)

    return pl.pallas_call(
        body, name="ssd_scan_bwd", grid=(nc,),
        in_specs=[s_dout, s_xbc, s_pad, s_padt, s_z, s_y, s_st, s_row, s_row, s_row, s_col, s_col, s_ng],
        out_specs=[s_xbc, pl.BlockSpec((blk, LANES), lambda i: (nc - 1 - i, 0)), s_y,
                   pl.BlockSpec((8, LANES), lambda i: (0, 0)), s_ng],
        out_shape=[jax.ShapeDtypeStruct((t, SSD_CONV_DIM), F32), jax.ShapeDtypeStruct((t, LANES), F32),
                   jax.ShapeDtypeStruct((t, SSD_W), BF16), jax.ShapeDtypeStruct((8, LANES), F32),
                   jax.ShapeDtypeStruct((1, SSD_W), F32)],
        scratch_shapes=[pltpu.VMEM((4, LANES, LANES), F32)],
        compiler_params=_cp("arbitrary"),
    )(dout, xbc, proj, pad_t, proj, ypre, states, dtb_row, alog_row, d_row, dtb_col, alog_col, ng)


def _pad_lanes(v, off):
    return jnp.zeros((1, LANES), F32).at[0, off:off + v.shape[0]].set(v)


def _perm_mix_w_in(w):
    z, xbc, dt = w[:, 0:512], w[:, 512:1536], w[:, 1536:1544]
    qkv, f, sc = w[:, 1544:2312], w[:, 2312:2316], w[:, 2316:3084]
    padblk = jnp.zeros((w.shape[0], LANES), w.dtype).at[:, PAD_DT0:PAD_DT0 + 8].set(dt).at[:, PAD_F0:PAD_F0 + 4].set(f)
    return jnp.concatenate([z, xbc, qkv, sc, padblk], axis=1)


def _unperm_mix_w_in(wp):
    z, xbc, qkv, sc = wp[:, 0:512], wp[:, 512:1536], wp[:, 1536:2304], wp[:, 2304:3072]
    dt, f = wp[:, P_PAD + PAD_DT0:P_PAD + PAD_DT0 + 8], wp[:, P_PAD + PAD_F0:P_PAD + PAD_F0 + 4]
    return jnp.concatenate([z, xbc, dt, qkv, f, sc], axis=1)


def _heads(m):
    return jnp.transpose(m.reshape(m.shape[0], FOX_HEADS, FOX_HD), (1, 0, 2))


def _unheads(m):
    return jnp.transpose(m, (1, 0, 2)).reshape(m.shape[1], FOX_W)


def _ffn_fwd(h, w_in, w_out, tag):
    u, a = _ffn_in_swiglu(h, w_in, f"ffn_in_{tag}")
    y = _matmul(a, w_out, "nn", F32, f"ffn_out_{tag}")
    return y, (h, u, a)


def _ffn_bwd(dy, saved, w_in, w_out, tag):
    h, u, a = saved
    du = _ffn_out_dx_swiglu(dy, w_out, u, f"ffn_out_dx_{tag}")
    dw_out = _matmul(a, dy, "tn", BF16, f"ffn_out_dw_{tag}")
    dh = _matmul(du, w_in, "nt", F32, f"ffn_in_dx_{tag}", halves="a")
    dw_in = _matmul(h, du, "tn", BF16, f"ffn_in_dw_{tag}", halves="b")
    return dh, dw_in, dw_out


def _mix_fwd(h, wp, w_out, sp, tag):
    proj = _matmul(h, wp, "nn", F32, f"mix_in_{tag}")
    pad_t = jnp.transpose(proj[:, P_PAD:P_PAD + LANES])
    xbc = _ssd_conv_fwd(proj, sp["conv_w"], sp["conv_b"])
    y_ssd, ypre, states = _ssd_fwd(xbc, proj, pad_t, sp["dtb_row"], sp["alog_row"], sp["d_row"],
                                   sp["dtb_col"], sp["alog_col"], sp["ng"])
    cf = _cum_logf(proj, sp["fbias_row"])
    cf_row = jnp.transpose(cf[:, PAD_F0:PAD_F0 + FOX_HEADS])[:, None, :]
    qkv = lax.optimization_barrier(proj[:, P_QKV:P_QKV + 768])
    q = _heads((qkv[:, 0:256] * FOX_SCALE).astype(BF16))
    k = _heads(qkv[:, 256:512].astype(BF16))
    v = _heads(qkv[:, 512:768].astype(BF16))
    o, lse = _fox_fwd(q, k, v, cf_row)
    y_sc = _shortconv_fwd(proj, sp["sconv_w"])
    ymix = jnp.concatenate([y_ssd, _unheads(o).astype(BF16), y_sc], axis=1)
    y = _matmul(ymix, w_out, "nn", F32, f"mix_out_{tag}")
    return y, (h, proj, pad_t, xbc, ypre, states, q, k, v, cf_row, o, lse, ymix)


def _mix_bwd(dy, saved, wp, w_out, sp, tag):
    h, proj, pad_t, xbc, ypre, states, q, k, v, cf_row, o, lse, ymix = saved
    dymix = _matmul(dy, w_out, "nt", F32, f"mix_out_dx_{tag}")
    dw_out = _matmul(ymix, dy, "tn", BF16, f"mix_out_dw_{tag}")
    dxbc, ddt, dz, ssd_acc, dng = _ssd_bwd(dymix, 0, xbc, proj, pad_t, ypre, states, sp["dtb_row"],
                                           sp["alog_row"], sp["d_row"], sp["dtb_col"], sp["alog_col"], sp["ng"])
    dxbc_raw, dconv_w, dconv_b = _ssd_conv_bwd(dxbc, proj, sp["conv_w"], sp["conv_b"])
    do = _heads(dymix[:, SSD_W:SSD_W + FOX_W])
    dq, dk, dv, dcq, dck = _fox_bwd(q, k, v, cf_row, o, lse, do)
    dq = dq * FOX_SCALE
    dcf4 = dcq[:, :, 0] + dck[:, 0, :]
    dcf = jnp.zeros((h.shape[0], LANES), F32).at[:, PAD_F0:PAD_F0 + FOX_HEADS].set(jnp.transpose(dcf4))
    dpad, dfb = _pad_block_bwd(dcf, ddt, proj, sp["fbias_row"])
    dscb, dscc, dscx, dsconv_w = _shortconv_bwd(dymix, SSD_W + FOX_W, proj, sp["sconv_w"])
    dproj = jnp.concatenate([dz, dxbc_raw.astype(BF16), _unheads(dq).astype(BF16), _unheads(dk).astype(BF16),
                             _unheads(dv).astype(BF16), dscb, dscc, dscx, dpad], axis=1)
    dh = _matmul(dproj, wp, "nt", F32, f"mix_in_dx_{tag}")
    dwp = _matmul(h, dproj, "tn", BF16, f"mix_in_dw_{tag}")
    small = dict(conv_w=dconv_w, conv_b=dconv_b[0], dt_bias=ssd_acc[2, 0:8], a_log=ssd_acc[0, 0:8],
                 d=ssd_acc[1, 0:8], norm_g=dng[0], f_bias=dfb[0, PAD_F0:PAD_F0 + FOX_HEADS], sconv_w=dsconv_w)
    return dh, _unperm_mix_w_in(dwp), dw_out, small


def _local_step(x, tgt, mod, weights_of, small_p, before_sub_backward=None, after_sub_backward=None):
    row = lambda v: v.reshape(1, -1)
    subs = [(l, j) for l in range(DEPTH) for j in range(N_SUB)]
    factor = (0.5, 1.0, 0.5)
    w_names = (("ffn1_w_in", "ffn1_w_out"), ("mix_w_in", "mix_w_out"), ("ffn2_w_in", "ffn2_w_out"))
    lg = [[row(small_p["ln_g"][l, j]) for j in range(N_SUB)] for l in range(DEPTH)]
    lb = [[row(small_p["ln_b"][l, j]) for j in range(N_SUB)] for l in range(DEPTH)]
    sps = []
    for l in range(DEPTH):
        sp = dict(
            conv_w=small_p["ssd_conv_w"][l], conv_b=row(small_p["ssd_conv_b"][l]),
            dtb_row=_pad_lanes(small_p["ssd_dt_bias"][l], PAD_DT0), alog_row=_pad_lanes(small_p["ssd_a_log"][l], 0),
            d_row=_pad_lanes(small_p["ssd_d"][l], 0), ng=row(small_p["ssd_norm_g"][l]),
            fbias_row=_pad_lanes(small_p["fox_f_bias"][l], PAD_F0), sconv_w=small_p["sconv_w"][l])
        sp["dtb_col"] = jnp.transpose(sp["dtb_row"])
        sp["alog_col"] = jnp.transpose(sp["alog_row"])
        sps.append(sp)
    tags = [f"l{l}{('f1', 'mx', 'f2')[j]}" for l, j in subs]

    x0, h = _ln_in_fwd(x, row(small_p["ln_in_g"]), row(small_p["ln_in_b"]), mod[0, 0])
    cur = x0
    xins, ys, inner, weights = [], [], [], []
    for idx, (l, j) in enumerate(subs):
        w_in, w_out = weights_of(l, j, h)
        if idx > 0:
            (w_in, w_out), (cur, h) = lax.optimization_barrier(((w_in, w_out), (cur, h)))
        w_out = w_out.reshape(-1, w_out.shape[-1])
        if j == 1:
            w_in = _perm_mix_w_in(jnp.concatenate([w_in[s] for s in range(4)], axis=1))
        weights.append((w_in, w_out))
        if j == 1:
            y, sv = _mix_fwd(h, w_in, w_out, sps[l], tags[idx])
        else:
            y, sv = _ffn_fwd(h, w_in, w_out, tags[idx])
        nxt = mod[subs[idx + 1]] if idx + 1 < len(subs) else None
        xins.append(cur)
        ys.append(y)
        inner.append(sv)
        cur, h = _res_ln(cur, y, mod[l, j], lg[l][j], lb[l][j], factor[j], f"res_ln_{tags[idx]}", nxt)

    dcur, loss_acc = _loss_head(cur, tgt)
    last = len(subs) - 1
    l, j = subs[last]
    dres, dy, acc = _res_ln_bwd(dcur, xins[last], ys[last], mod[l, j], lg[l][j], factor[j], f"res_ln_bwd_{tags[last]}")
    ln_acc = {last: acc}
    shift_scale = {}
    big_grads = [dict() for _ in range(DEPTH)]
    small_g = [None] * DEPTH
    for idx in reversed(range(len(subs))):
        l, j = subs[idx]
        w_in, w_out = weights[idx]
        if before_sub_backward is not None:
            before_sub_backward(l, j, dy)
        if j == 1:
            dh, g_in, g_out, small_g[l] = _mix_bwd(dy, inner[idx], w_in, w_out, sps[l], tags[idx])
        else:
            dh, g_in, g_out = _ffn_bwd(dy, inner[idx], w_in, w_out, tags[idx])
        big_grads[l][w_names[j][0]], big_grads[l][w_names[j][1]] = g_in, g_out
        if after_sub_backward is not None:
            after_sub_backward(l, j, big_grads[l], dh)
        if idx > 0:
            pl_, pj = subs[idx - 1]
            dres, dy, acc5 = _mod_res_bwd(dres, dh, mod[l, j], xins[idx - 1], ys[idx - 1], mod[pl_, pj], lg[pl_][pj],
                                          lb[pl_][pj], factor[pj], f"mod_res_bwd_{tags[idx]}")
            shift_scale[idx], ln_acc[idx - 1] = acc5[0:2], acc5[2:5]
        else:
            dx0, shift_scale[0] = _modulate_bwd(dres, dh, x0, mod[0, 0], "mod_bwd_first")
    dx, acc_in = _ln_in_bwd(dx0, x, row(small_p["ln_in_g"]))
    dmod = []
    for l in range(DEPTH):
        ids = [N_SUB * l + j for j in range(N_SUB)]
        small_g[l]["ln_g"] = jnp.stack([ln_acc[i][0] for i in ids])
        small_g[l]["ln_b"] = jnp.stack([ln_acc[i][1] for i in ids])
        dmod.append(jnp.stack([jnp.concatenate([shift_scale[i], ln_acc[i][2:3]], axis=0) for i in ids]))
    return loss_acc[0, 0], dx, big_grads, small_g, jnp.stack(dmod), acc_in


SMALL_ORDER = ("ssd_conv_w", "ssd_conv_b", "ssd_dt_bias", "ssd_a_log", "ssd_d", "ssd_norm_g", "fox_f_bias",
               "sconv_w", "ln_g", "ln_b")
SMALL_KEY = dict(ssd_conv_w="conv_w", ssd_conv_b="conv_b", ssd_dt_bias="dt_bias", ssd_a_log="a_log", ssd_d="d",
                 ssd_norm_g="norm_g", fox_f_bias="f_bias", sconv_w="sconv_w", ln_g="ln_g", ln_b="ln_b")
COL_SHARDED_SMALL = ("ssd_conv_w", "sconv_w", "ln_g", "ln_b")


def _pad_to(v, n):
    return jnp.concatenate([v, jnp.zeros((n - v.shape[0],), v.dtype)])


def kernel(x, c, ln_in_g, ln_in_b, ada_w, ada_b, ffn1_w_in, ffn1_w_out, mix_w_in, mix_w_out, ssd_conv_w, ssd_conv_b, ssd_dt_bias, ssd_a_log, ssd_d, ssd_norm_g, fox_f_bias, sconv_w, ffn2_w_in, ffn2_w_out, ln_g, ln_b, loss_target, m_ln_in_g, m_ln_in_b, m_ada_w, m_ada_b, m_ffn1_w_in, m_ffn1_w_out, m_mix_w_in, m_mix_w_out, m_ssd_conv_w, m_ssd_conv_b, m_ssd_dt_bias, m_ssd_a_log, m_ssd_d, m_ssd_norm_g, m_fox_f_bias, m_sconv_w, m_ffn2_w_in, m_ffn2_w_out, m_ln_g, m_ln_b, v_ln_in_g, v_ln_in_b, v_ada_w, v_ada_b, v_ffn1_w_in, v_ffn1_w_out, v_mix_w_in, v_mix_w_out, v_ssd_conv_w, v_ssd_conv_b, v_ssd_dt_bias, v_ssd_a_log, v_ssd_d, v_ssd_norm_g, v_fox_f_bias, v_sconv_w, v_ffn2_w_in, v_ffn2_w_out, v_ln_g, v_ln_b):
    names = ("ln_in_g", "ln_in_b", "ada_w", "ada_b", "ffn1_w_in", "ffn1_w_out", "mix_w_in", "mix_w_out",
             "ssd_conv_w", "ssd_conv_b", "ssd_dt_bias", "ssd_a_log", "ssd_d", "ssd_norm_g", "fox_f_bias", "sconv_w",
             "ffn2_w_in", "ffn2_w_out", "ln_g", "ln_b")
    w_loc = dict(zip(names, (ln_in_g, ln_in_b, ada_w, ada_b, ffn1_w_in, ffn1_w_out, mix_w_in, mix_w_out, ssd_conv_w,
                             ssd_conv_b, ssd_dt_bias, ssd_a_log, ssd_d, ssd_norm_g, fox_f_bias, sconv_w, ffn2_w_in,
                             ffn2_w_out, ln_g, ln_b)))
    m_loc = dict(zip(names, (m_ln_in_g, m_ln_in_b, m_ada_w, m_ada_b, m_ffn1_w_in, m_ffn1_w_out, m_mix_w_in,
                             m_mix_w_out, m_ssd_conv_w, m_ssd_conv_b, m_ssd_dt_bias, m_ssd_a_log, m_ssd_d,
                             m_ssd_norm_g, m_fox_f_bias, m_sconv_w, m_ffn2_w_in, m_ffn2_w_out, m_ln_g, m_ln_b)))
    v_loc = dict(zip(names, (v_ln_in_g, v_ln_in_b, v_ada_w, v_ada_b, v_ffn1_w_in, v_ffn1_w_out, v_mix_w_in,
                             v_mix_w_out, v_ssd_conv_w, v_ssd_conv_b, v_ssd_dt_bias, v_ssd_a_log, v_ssd_d,
                             v_ssd_norm_g, v_fox_f_bias, v_sconv_w, v_ffn2_w_in, v_ffn2_w_out, v_ln_g, v_ln_b)))

    xi, yi, ci = lax.axis_index("x"), lax.axis_index("y"), lax.axis_index("c")
    me = 2 * xi + yi
    dev = 2 * me + ci

    def gather8(v, tag):
        v2 = v.reshape(1, -1)
        got = _bcast_chips([v2], f"gather_chips_{tag}")
        same_c = jnp.concatenate(_by_chip(me, v2, [g[0] for g in got]), axis=0)
        other_c = _swap_sibling([same_c], f"gather_sibling_{tag}")[0]
        pair = lax.switch(ci, [lambda a, b: jnp.stack([a, b], axis=1), lambda a, b: jnp.stack([b, a], axis=1)],
                          same_c, other_c)
        return pair.reshape(8, -1)

    def chip_concat(own, got3, axis):
        return jnp.concatenate(_by_chip(me, own, got3), axis=axis)

    small_cols = [w_loc[n].reshape(-1, w_loc[n].shape[-1]) for n in COL_SHARDED_SMALL]
    got = _bcast_chips(small_cols, "gather_small_params")
    small_p = {n: w_loc[n] for n in ("ln_in_g", "ln_in_b", "ssd_conv_b", "ssd_dt_bias", "ssd_a_log", "ssd_d",
                                     "ssd_norm_g", "fox_f_bias")}
    for i, n in enumerate(COL_SHARDED_SMALL):
        full = chip_concat(small_cols[i], [g[i] for g in got], 1)
        small_p[n] = full.reshape(w_loc[n].shape[:-1] + (full.shape[-1],))

    assert DEPTH == 2
    first, rest = BIG[:2], BIG[2:]
    shard_of = {n: w_loc[n].astype(BF16) for n in BIG}

    def gather(names, layer, after, tag, sequencer_id):
        out = _gather_weights(names, [shard_of[n] for n in names], layer, after, f"gather_weights_{tag}", sequencer_id)
        return dict(zip(names, out))

    gathered = {0: gather(first, 0, [], "l0a", 12)}

    def weights_of(l, j, marker):
        if (l, j) == (0, 0):
            gathered[0].update(gather(rest, 0, [marker], "l0b", 1))
        elif (l, j) == (0, 1):
            gathered[1] = gather(BIG, 1, [marker], "l1", 5)
        pair = BIG[2 * j:2 * j + 2]
        return gathered[l][pair[0]], gathered[l][pair[1]]

    c_all = gather8(c[0], "c")
    c_act = _silu_bf16(c_all)
    ada_w_b = ada_w.astype(BF16)
    mod_loc = [_matmul(c_act, ada_w_b[l], "nn", F32, f"ada_fwd_l{l}") for l in range(DEPTH)]
    mod_loc = jnp.stack(mod_loc)
    got = _bcast_chips([mod_loc], "gather_mod")
    mod_all = chip_concat(mod_loc, [g[0] for g in got], 2)
    mod = lax.dynamic_index_in_dim(mod_all, dev, 1, keepdims=False) + ada_b
    mod = mod.reshape(DEPTH, N_SUB, 3, D)

    ci_arr = jnp.reshape(ci, (1,)).astype(jnp.int32)
    me_arr = jnp.reshape(me, (1,)).astype(jnp.int32)

    def wire_form(n, g):
        if n in BY_COLUMNS:
            return g
        if n == "mix_w_in":
            return jnp.transpose(g.reshape(g.shape[0], 4, g.shape[1] // 4), (1, 0, 2))
        return g.reshape(4, g.shape[0] // 4, g.shape[1])

    units, reduced = {}, {}

    def start_unit(tag, l, names, g, ids):
        wire = [wire_form(n, g[n]) for n in names]
        units[tag] = dict(l=l, names=names, wire=wire, ids=ids,
                          got=_grad_swap_halves(names, wire, f"grad_swap_halves_{tag}", ids[0]))

    def add_and_scatter(tag, after):
        st = units[tag]
        st["part"] = [_grad_add_halves(st["wire"][i], st["got"][i], ci_arr, after, f"grad_add_halves_{tag}_{n}")
                      for i, n in enumerate(st["names"])]
        st["from_chips"] = _grad_scatter_halves(st["names"], st["part"], f"grad_scatter_halves_{tag}", st["ids"][1])

    def finish_unit(tag, after):
        st = units[tag]
        halves = [_grad_add_slots(st["part"][i], [g[i] for g in st["from_chips"]], n, me_arr, after,
                                  f"grad_add_slots_{tag}_{n}") for i, n in enumerate(st["names"])]
        other = _grad_swap_reduced(halves, f"grad_swap_reduced_{tag}", st["ids"][2])
        for i, n in enumerate(st["names"]):
            reduced[(st["l"], n)] = (halves[i], other[i])

    second, third = BIG[2:4], BIG[4:]

    def after_sub_backward(l, j, g, marker):
        if (l, j) == (1, 0):
            start_unit("l1", 1, BIG, g, (2, 3, 4))
        elif (l, j) == (0, 2):
            start_unit("l0f2", 0, third, g, (6, 7, 8))
        elif (l, j) == (0, 1):
            start_unit("l0mx", 0, second, g, (9, 10, 11))

    def before_sub_backward(l, j, marker):
        if (l, j) == (0, 2):
            add_and_scatter("l1", marker)
        elif (l, j) == (0, 1):
            add_and_scatter("l0f2", marker)
        elif (l, j) == (0, 0):
            add_and_scatter("l0mx", marker)

    loss_part, dx, big_g, small_g, dmod, acc_in = _local_step(
        x[0], loss_target[0], mod, weights_of, small_p, before_sub_backward, after_sub_backward)
    loss = lax.psum(loss_part, ("x", "y", "c"))

    pieces = [dmod.reshape(-1), acc_in[0], acc_in[1]]
    for n in SMALL_ORDER:
        pieces.append(jnp.stack([small_g[l][SMALL_KEY[n]] for l in range(DEPTH)]).reshape(-1))
    sizes = [p.shape[0] for p in pieces]
    total = sum(sizes)
    padded = -(-total // 1024) * 1024
    vec = _pad_to(jnp.concatenate(pieces), padded)
    all_rows = gather8(vec, "small_grads")
    summed = _sum_rows(all_rows, "sum_small_grads")[0]
    offs = [0]
    for s in sizes:
        offs.append(offs[-1] + s)
    n_mod = sizes[0]
    grads = {"ada_b": summed[0:n_mod].reshape(DEPTH, 3 * N_SUB * D),
             "ln_in_g": summed[offs[1]:offs[2]], "ln_in_b": summed[offs[2]:offs[3]]}
    for i, n in enumerate(SMALL_ORDER):
        full = summed[offs[3 + i]:offs[4 + i]].reshape(small_p[n].shape)
        if n in COL_SHARDED_SMALL:
            wcols = w_loc[n].shape[-1]
            full = lax.dynamic_slice_in_dim(full, me * wcols, wcols, axis=full.ndim - 1)
        grads[n] = full

    dmod_all = all_rows[:, 0:n_mod].reshape(8, DEPTH, 3 * N_SUB * D)
    ncol = ada_w.shape[-1]
    dmod_cols = lax.dynamic_slice_in_dim(dmod_all, me * ncol, ncol, axis=2).astype(BF16)
    grads["ada_w"] = jnp.stack([_matmul(c_act, dmod_cols[:, l], "tn", F32, f"ada_bwd_l{l}") for l in range(DEPTH)])

    finish_unit("l1", dx)
    finish_unit("l0f2", dx)
    finish_unit("l0mx", dx)
    start_unit("l0f1", 0, first, big_g[0], (None, None, None))
    add_and_scatter("l0f1", dx)
    finish_unit("l0f1", dx)

    delta, new_m, new_v = {}, {}, {}
    delta["ada_w"], new_m["ada_w"], new_v["ada_w"] = _adamw(w_loc["ada_w"], grads["ada_w"], m_loc["ada_w"],
                                                           v_loc["ada_w"], "adamw_ada_w")
    for i, n in enumerate(BIG):
        grads[n], delta[n], new_m[n], new_v[n] = _adamw_layers(
            w_loc[n], [reduced[(l, n)][0] for l in range(DEPTH)], [reduced[(l, n)][1] for l in range(DEPTH)],
            m_loc[n], v_loc[n], ci_arr, f"adamw_{n}")
    small_names = [n for n in names if n not in ("ada_w",) + BIG]
    flat = lambda d: jnp.concatenate([d[n].reshape(-1) for n in small_names])
    n_small = sum(w_loc[n].size for n in small_names)
    n_pad = -(-n_small // 1024) * 1024
    packed = [_pad_to(flat(d), n_pad).reshape(-1, LANES) for d in (w_loc, grads, m_loc, v_loc)]
    d_s, m_s, v_s = _adamw(*packed, "adamw_small")
    off = 0
    for n in small_names:
        sz = w_loc[n].size
        delta[n] = d_s.reshape(-1)[off:off + sz].reshape(w_loc[n].shape)
        new_m[n] = m_s.reshape(-1)[off:off + sz].reshape(w_loc[n].shape)
        new_v[n] = v_s.reshape(-1)[off:off + sz].reshape(w_loc[n].shape)
        off += sz

    return (loss, dx[None], *[grads[n] for n in names], *[delta[n] for n in names],
            *[new_m[n] for n in names], *[new_v[n] for n in names])
```

```python
import functools

import jax
import jax.numpy as jnp
from jax import lax
from jax.experimental import pallas as pl
from jax.experimental.pallas import tpu as pltpu
from jax.experimental.pallas import tpu_sc as plsc

F32 = jnp.float32
BF16 = jnp.bfloat16
MESH = pl.DeviceIdType.MESH

D = 1024
DEPTH = 2
N_SUB = 3
D_FF = 2816
FF_TILE = D_FF // 2
ALPHA = (2 * DEPTH) ** 0.25
LN_EPS = 1e-5
RMS_EPS = 1e-5
SSD_W = 512
SSD_HEADS = 8
SSD_CONV_K = 4
SSD_CONV_DIM = 1024
FOX_W = 256
FOX_HEADS = 4
FOX_HD = 64
FOX_SCALE = FOX_HD ** -0.5
SC_W = 256
SC_K = 3
D_IN_PROJ = 3084
P_Z, P_XBC, P_QKV, P_SC, P_PAD = 0, 512, 1536, 2304, 3072
D_PROJ_PAD = 3200
PAD_DT0, PAD_F0 = 0, 8
SSD_CHUNK = 256
ATT_BLOCK = 512
CUM_BLOCK = 256
LANES = 128
VMEM_LIMIT = 56 * 1024 * 1024

ADAM_LR, ADAM_B1, ADAM_B2, ADAM_EPS, ADAM_WD, ADAM_STEP = 0.001, 0.9, 0.999, 1e-08, 0.01, 10
NEG = -1e30


def _cp(*sem):
    return pltpu.CompilerParams(dimension_semantics=sem, vmem_limit_bytes=VMEM_LIMIT)


def _pick(n, cands):
    for c in cands:
        if n % c == 0:
            return c
    return n


def _dot(a, b):
    return lax.dot_general(a, b, (((1,), (0,)), ((), ())), preferred_element_type=F32)


def _dot_nt(a, b):
    return lax.dot_general(a, b, (((1,), (1,)), ((), ())), preferred_element_type=F32)


def _dot_tn(a, b):
    return lax.dot_general(a, b, (((0,), (0,)), ((), ())), preferred_element_type=F32)


def _sigmoid(x):
    return 0.5 * jnp.tanh(0.5 * x) + 0.5


def _softplus(x):
    return jnp.maximum(x, 0.0) + jnp.log(1.0 + jnp.exp(-jnp.abs(x)))


def _split3(v):
    h = v.astype(BF16)
    r = v - h.astype(F32)
    m = r.astype(BF16)
    l = (r - m.astype(F32)).astype(BF16)
    return h, m, l


def _tri_left(tri, v):
    h, m, l = _split3(v)
    return _dot(tri, h) + _dot(tri, m) + _dot(tri, l)


def _tri_right(v, tri):
    h, m, l = _split3(v)
    return _dot(h, tri) + _dot(m, tri) + _dot(l, tri)


def _matmul(a, b, mode, out_dtype, name, halves=None):
    assert a.dtype == BF16 and b.dtype == BF16, (name, a.dtype, b.dtype)
    if halves == "a":
        assert mode == "nt" and a.shape[0] == 2 and a.shape[2] == D_FF
        m, k, n = a.shape[1], 2 * D_FF, b.shape[0]
    elif halves == "b":
        assert mode == "tn" and b.shape[0] == 2 and b.shape[2] == D_FF
        (k, m), n = a.shape, 2 * D_FF
    elif mode == "nn":
        (m, k), n = a.shape, b.shape[1]
    elif mode == "nt":
        (m, k), n = a.shape, b.shape[0]
    else:
        (k, m), n = a.shape, b.shape[1]
    tm = m if m <= 1024 else _pick(m, (1024, 1408, 512, 256, 128))
    tn = n if n <= 1024 else _pick(n, (1408, 640, 512, 256, 128))
    tk = k if k <= 1024 else _pick(k, (1408, 1024, 640, 512, 256, 128))
    nk = k // tk
    if mode == "nn":
        dn = (((1,), (0,)), ((), ()))
        a_spec = pl.BlockSpec((tm, tk), lambda i, j, kk: (i, kk))
        b_spec = pl.BlockSpec((tk, tn), lambda i, j, kk: (kk, j))
    elif mode == "nt":
        dn = (((1,), (1,)), ((), ()))
        a_spec = pl.BlockSpec((tm, tk), lambda i, j, kk: (i, kk))
        b_spec = pl.BlockSpec((tn, tk), lambda i, j, kk: (j, kk))
    else:
        dn = (((0,), (0,)), ((), ()))
        a_spec = pl.BlockSpec((tk, tm), lambda i, j, kk: (kk, i))
        b_spec = pl.BlockSpec((tk, tn), lambda i, j, kk: (kk, j))
    per_half = D_FF // FF_TILE
    if halves == "a":
        assert tk == FF_TILE
        a_spec = pl.BlockSpec((None, tm, tk), lambda i, j, kk: (kk // per_half, i, kk % per_half))
    elif halves == "b":
        assert tn == FF_TILE
        b_spec = pl.BlockSpec((None, tk, tn), lambda i, j, kk: (j // per_half, kk, j % per_half))

    def body(a_ref, b_ref, o_ref, *acc):
        d = lax.dot_general(a_ref[...], b_ref[...], dn, preferred_element_type=F32)
        if nk == 1:
            o_ref[...] = d.astype(o_ref.dtype)
            return
        acc_ref, = acc
        kk = pl.program_id(2)

        @pl.when(kk == 0)
        def _():
            acc_ref[...] = d

        @pl.when((kk > 0) & (kk < nk - 1))
        def _():
            acc_ref[...] += d

        @pl.when(kk == nk - 1)
        def _():
            o_ref[...] = (acc_ref[...] + d).astype(o_ref.dtype)

    return pl.pallas_call(
        body, name=name, grid=(m // tm, n // tn, nk),
        in_specs=[a_spec, b_spec],
        out_specs=pl.BlockSpec((tm, tn), lambda i, j, kk: (i, j)),
        out_shape=jax.ShapeDtypeStruct((m, n), out_dtype),
        scratch_shapes=[pltpu.VMEM((tm, tn), F32)] if nk > 1 else [],
        compiler_params=_cp("parallel", "parallel", "arbitrary"),
    )(a, b)


def _rows(body, name, t, tb, row_in, full_in, row_out, acc_out):
    in_specs, args = [], []
    for r in row_in:
        if isinstance(r, tuple):
            arr, w, j = r
            in_specs.append(pl.BlockSpec((tb, w), functools.partial(lambda i, jj: (i, jj), jj=j)))
            args.append(arr)
        else:
            in_specs.append(pl.BlockSpec((tb, r.shape[1]), lambda i: (i, 0)))
            args.append(r)
    for f in full_in:
        in_specs.append(pl.BlockSpec(f.shape, functools.partial(lambda i, nd: (0,) * nd, nd=f.ndim)))
        args.append(f)
    out_specs = [pl.BlockSpec((tb, c), lambda i: (i, 0)) for c, _ in row_out]
    out_specs += [pl.BlockSpec(s, functools.partial(lambda i, nd: (0,) * nd, nd=len(s))) for s in acc_out]
    out_shape = [jax.ShapeDtypeStruct((t, c), dt) for c, dt in row_out]
    out_shape += [jax.ShapeDtypeStruct(s, F32) for s in acc_out]
    return pl.pallas_call(
        body, name=name, grid=(t // tb,), in_specs=in_specs, out_specs=out_specs, out_shape=out_shape,
        compiler_params=_cp("arbitrary"),
    )(*args)


def _ln_stats(r):
    mu = jnp.mean(r, axis=-1, keepdims=True)
    xc = r - mu
    var = jnp.mean(xc * xc, axis=-1, keepdims=True)
    rstd = lax.rsqrt(var + LN_EPS)
    return xc * rstd, rstd


def _ln_bwd(dout, xhat, rstd, g):
    dxh = dout * g
    m1 = jnp.mean(dxh, axis=-1, keepdims=True)
    m2 = jnp.mean(dxh * xhat, axis=-1, keepdims=True)
    return rstd * (dxh - m1 - xhat * m2)


def _first(i, acc_refs):
    @pl.when(i == 0)
    def _():
        for a in acc_refs:
            a[...] = jnp.zeros_like(a)


def _modulated(xv, m_ref):
    return (xv * (1.0 + m_ref[1:2, :]) + m_ref[0:1, :]).astype(BF16)


def _ln_in_fwd(x, g, b, next_mod3):
    t = x.shape[0]

    def body(x_ref, g_ref, b_ref, m_ref, o_ref, h_ref):
        xhat, _ = _ln_stats(x_ref[...])
        out = xhat * g_ref[...] + b_ref[...]
        o_ref[...] = out
        h_ref[...] = _modulated(out, m_ref)

    return _rows(body, "ln_in_fwd", t, 256, [x], [g, b, next_mod3], [(D, F32), (D, BF16)], [])


def _ln_in_bwd(dx0, x, g):
    t = x.shape[0]

    def body(d_ref, x_ref, g_ref, o_ref, acc_ref):
        _first(pl.program_id(0), [acc_ref])
        xhat, rstd = _ln_stats(x_ref[...])
        d = d_ref[...]
        o_ref[...] = _ln_bwd(d, xhat, rstd, g_ref[...])
        acc_ref[0:1, :] += jnp.sum(d * xhat, axis=0, keepdims=True)
        acc_ref[1:2, :] += jnp.sum(d, axis=0, keepdims=True)

    return _rows(body, "ln_in_bwd", t, 256, [dx0, x], [g], [(D, F32)], [(2, D)])


def _modulate_bwd(dxres, dh, xin, mod3, name):
    t = xin.shape[0]

    def body(r_ref, dh_ref, x_ref, m_ref, o_ref, acc_ref):
        _first(pl.program_id(0), [acc_ref])
        dh_v = dh_ref[...]
        o_ref[...] = r_ref[...] + dh_v * (1.0 + m_ref[1:2, :])
        acc_ref[0:1, :] += jnp.sum(dh_v, axis=0, keepdims=True)
        acc_ref[1:2, :] += jnp.sum(dh_v * x_ref[...], axis=0, keepdims=True)

    return _rows(body, name, t, 256, [dxres, dh, xin], [mod3], [(D, F32)], [(2, D)])


def _ffn_in_swiglu(h, w_in, name):
    t = h.shape[0]
    tm = _pick(t, (512, 256, 128))
    nj = D_FF // FF_TILE

    def body(h_ref, wg_ref, wu_ref, u_ref, a_ref):
        hv = h_ref[...]
        gate = _dot(hv, wg_ref[...])
        up = _dot(hv, wu_ref[...])
        sg = _sigmoid(gate)
        silu = gate * sg
        u_ref[0] = (up * (sg + silu * (1.0 - sg))).astype(BF16)
        u_ref[1] = silu.astype(BF16)
        a_ref[...] = (silu * up).astype(BF16)

    return pl.pallas_call(
        body, name=name, grid=(nj, t // tm),
        in_specs=[pl.BlockSpec((tm, D), lambda j, i: (i, 0)),
                  pl.BlockSpec((D, FF_TILE), lambda j, i: (0, j)),
                  pl.BlockSpec((D, FF_TILE), lambda j, i: (0, nj + j))],
        out_specs=[pl.BlockSpec((2, tm, FF_TILE), lambda j, i: (0, i, j)),
                   pl.BlockSpec((tm, FF_TILE), lambda j, i: (i, j))],
        out_shape=[jax.ShapeDtypeStruct((2, t, D_FF), BF16), jax.ShapeDtypeStruct((t, D_FF), BF16)],
        compiler_params=_cp("parallel", "parallel"),
    )(h, w_in, w_in)


def _ffn_out_dx_swiglu(dy, w_out, u, name):
    t = dy.shape[0]
    tm = _pick(t, (512, 256, 128))
    nj = D_FF // FF_TILE

    def body(dy_ref, w_ref, u_ref, du_ref):
        da = _dot_nt(dy_ref[...], w_ref[...])
        du_ref[0] = (da * u_ref[0].astype(F32)).astype(BF16)
        du_ref[1] = (da * u_ref[1].astype(F32)).astype(BF16)

    blk3 = pl.BlockSpec((2, tm, FF_TILE), lambda j, i: (0, i, j))
    return pl.pallas_call(
        body, name=name, grid=(nj, t // tm),
        in_specs=[pl.BlockSpec((tm, D), lambda j, i: (i, 0)),
                  pl.BlockSpec((FF_TILE, D), lambda j, i: (j, 0)), blk3],
        out_specs=blk3,
        out_shape=jax.ShapeDtypeStruct((2, t, D_FF), BF16),
        compiler_params=_cp("parallel", "parallel"),
    )(dy, w_out, u)


def _res_ln(xin, y, mod3, lg, lb, factor, name, next_mod3=None):
    t = xin.shape[0]

    def body(x_ref, y_ref, m_ref, g_ref, b_ref, *rest):
        r = ALPHA * x_ref[...] + (factor * m_ref[2:3, :]) * y_ref[...]
        xhat, _ = _ln_stats(r)
        out = xhat * g_ref[...] + b_ref[...]
        if next_mod3 is None:
            rest[0][...] = out
        else:
            rest[1][...] = out
            rest[2][...] = _modulated(out, rest[0])

    if next_mod3 is None:
        return _rows(body, name, t, 256, [xin, y], [mod3, lg, lb], [(D, F32)], [])[0], None
    return _rows(body, name, t, 256, [xin, y], [mod3, lg, lb, next_mod3], [(D, F32), (D, BF16)], [])


def _mod_res_bwd(dxres, dh, mod3, xin_p, y_p, mod3_p, lg_p, lb_p, factor_p, name):
    t = dxres.shape[0]

    def body(r_ref, dh_ref, xp_ref, yp_ref, m_ref, mp_ref, g_ref, b_ref, dres_ref, dy_ref, acc_ref):
        _first(pl.program_id(0), [acc_ref])
        gate = factor_p * mp_ref[2:3, :]
        yv = yp_ref[...]
        xhat, rstd = _ln_stats(ALPHA * xp_ref[...] + gate * yv)
        xin = xhat * g_ref[...] + b_ref[...]
        dh_v = dh_ref[...]
        d = r_ref[...] + dh_v * (1.0 + m_ref[1:2, :])
        dr = _ln_bwd(d, xhat, rstd, g_ref[...])
        dres_ref[...] = ALPHA * dr
        dy_ref[...] = (gate * dr).astype(BF16)
        acc_ref[0:1, :] += jnp.sum(dh_v, axis=0, keepdims=True)
        acc_ref[1:2, :] += jnp.sum(dh_v * xin, axis=0, keepdims=True)
        acc_ref[2:3, :] += jnp.sum(d * xhat, axis=0, keepdims=True)
        acc_ref[3:4, :] += jnp.sum(d, axis=0, keepdims=True)
        acc_ref[4:5, :] += jnp.sum(factor_p * yv * dr, axis=0, keepdims=True)

    return _rows(body, name, t, 256, [dxres, dh, xin_p, y_p], [mod3, mod3_p, lg_p, lb_p],
                 [(D, F32), (D, BF16)], [(5, D)])


def _res_ln_bwd(dout, xin, y, mod3, lg, factor, name):
    t = xin.shape[0]

    def body(d_ref, x_ref, y_ref, m_ref, g_ref, dres_ref, dy_ref, acc_ref):
        _first(pl.program_id(0), [acc_ref])
        gate = factor * m_ref[2:3, :]
        yv = y_ref[...]
        r = ALPHA * x_ref[...] + gate * yv
        xhat, rstd = _ln_stats(r)
        d = d_ref[...]
        dr = _ln_bwd(d, xhat, rstd, g_ref[...])
        dres_ref[...] = ALPHA * dr
        dy_ref[...] = (gate * dr).astype(BF16)
        acc_ref[0:1, :] += jnp.sum(d * xhat, axis=0, keepdims=True)
        acc_ref[1:2, :] += jnp.sum(d, axis=0, keepdims=True)
        acc_ref[2:3, :] += jnp.sum(factor * yv * dr, axis=0, keepdims=True)

    return _rows(body, name, t, 256, [dout, xin, y], [mod3, lg], [(D, F32), (D, BF16)], [(3, D)])


def _res_ln_loss(xin, y, mod3, lg, lb, factor, tgt):
    t = xin.shape[0]

    def body(x_ref, y_ref, t_ref, m_ref, g_ref, b_ref, d_ref, acc_ref):
        _first(pl.program_id(0), [acc_ref])
        r = ALPHA * x_ref[...] + (factor * m_ref[2:3, :]) * y_ref[...]
        xhat, _ = _ln_stats(r)
        e = xhat * g_ref[...] + b_ref[...] - t_ref[...]
        d_ref[...] = e * (1.0 / D)
        part = 0.5 * jnp.sum(jnp.mean(e * e, axis=-1, keepdims=True), axis=0, keepdims=True)
        acc_ref[...] += jnp.broadcast_to(part, acc_ref.shape)

    return _rows(body, "res_ln_loss", t, 256, [xin, y, tgt], [mod3, lg, lb], [(D, F32)], [(1, LANES)])


def _silu_bf16(c_all):
    def body(c_ref, o_ref):
        v = c_ref[...]
        o_ref[...] = (v * _sigmoid(v)).astype(BF16)

    return _rows(body, "silu_c", c_all.shape[0], c_all.shape[0], [c_all], [], [(c_all.shape[1], BF16)], [])[0]


def _sum_rows(v, name):
    r, n = v.shape
    tn = _pick(n, (8192, 4096, 2048, 1024, 512, 256, 128))

    def body(v_ref, o_ref):
        acc = v_ref[0:1, :]
        for k in range(1, r):
            acc = acc + v_ref[k:k + 1, :]
        o_ref[...] = acc

    return pl.pallas_call(
        body, name=name, grid=(n // tn,),
        in_specs=[pl.BlockSpec((r, tn), lambda j: (0, j))],
        out_specs=pl.BlockSpec((1, tn), lambda j: (0, j)),
        out_shape=jax.ShapeDtypeStruct((1, n), F32),
        compiler_params=_cp("parallel"),
    )(v)


def _elementwise(fn, name, ins, out_dtypes):
    r, c = ins[0].shape
    tb = _pick(r, (128, 64, 32, 16, 8))
    n_in = len(ins)

    def body(*refs):
        outs = fn(*[x[...] for x in refs[:n_in]])
        for o_ref, o in zip(refs[n_in:], outs):
            o_ref[...] = o.astype(o_ref.dtype)

    spec = pl.BlockSpec((tb, c), lambda i: (i, 0))
    return pl.pallas_call(
        body, name=name, grid=(r // tb,), in_specs=[spec] * n_in, out_specs=[spec] * len(out_dtypes),
        out_shape=[jax.ShapeDtypeStruct((r, c), dt) for dt in out_dtypes],
        compiler_params=_cp("parallel"),
    )(*ins)


def _adamw_math(w, g, m, v):
    m = ADAM_B1 * m + (1.0 - ADAM_B1) * g
    v = ADAM_B2 * v + (1.0 - ADAM_B2) * (g * g)
    m_hat = m / (1.0 - ADAM_B1 ** ADAM_STEP)
    v_hat = v / (1.0 - ADAM_B2 ** ADAM_STEP)
    delta = -ADAM_LR * (m_hat / (jnp.sqrt(v_hat) + ADAM_EPS) + ADAM_WD * w)
    return delta, m, v


def _adamw(w, g, m, v, name):
    shape = w.shape
    c = shape[-1]
    w2, g2, m2, v2 = (a.reshape(-1, c) for a in (w, g, m, v))
    outs = _elementwise(_adamw_math, name, [w2, g2, m2, v2], [F32, F32, F32])
    return tuple(o.reshape(shape) for o in outs)


def _remote_exchange(ins, plan, peers_of, name):
    n_in, n_out = len(ins), len(plan)

    def body(*refs):
        in_refs, out_refs = refs[:n_in], refs[n_in:n_in + n_out]
        send_sems, recv_sems = refs[n_in + n_out], refs[n_in + n_out + 1]
        peers = peers_of(lax.axis_index("x"), lax.axis_index("y"), lax.axis_index("c"))
        copies = [
            pltpu.make_async_remote_copy(
                src_ref=in_refs[src], dst_ref=out_refs[k], send_sem=send_sems.at[k], recv_sem=recv_sems.at[k],
                device_id=peers[peer], device_id_type=MESH)
            for k, (peer, src) in enumerate(plan)
        ]
        for cp in copies:
            cp.start()
        for cp in copies:
            cp.wait()

    any_spec = pl.BlockSpec(memory_space=pl.ANY)
    return list(pl.pallas_call(
        body, name=name,
        in_specs=[any_spec] * n_in, out_specs=[any_spec] * n_out,
        out_shape=[jax.ShapeDtypeStruct(ins[src].shape, ins[src].dtype) for _, src in plan],
        scratch_shapes=[pltpu.SemaphoreType.DMA((n_out,)), pltpu.SemaphoreType.DMA((n_out,))],
    )(*ins))


def _sibling(x, y, c):
    return [(x, y, 1 - c)]


def _other_chips(x, y, c):
    return [(1 - x, y, c), (x, 1 - y, c), (1 - x, 1 - y, c)]


def _swap_sibling(arrs, name):
    return _remote_exchange(arrs, [(0, i) for i in range(len(arrs))], _sibling, name)


def _bcast_chips(arrs, name):
    n = len(arrs)
    out = _remote_exchange(arrs, [(p, i) for p in range(3) for i in range(n)], _other_chips, name)
    return [out[p * n:(p + 1) * n] for p in range(3)]


def _by_chip(me, own, got3):
    by_rel = [own, got3[0], got3[1], got3[2]]
    rel_bits = (0, 2, 1, 3)

    def branch(m):
        def f(ops):
            return [ops[rel_bits.index(i ^ m)] for i in range(4)]
        return f

    return lax.switch(me, [branch(m) for m in range(4)], by_rel)


BIG = ("ffn1_w_in", "ffn1_w_out", "mix_w_in", "mix_w_out", "ffn2_w_in", "ffn2_w_out")
BY_COLUMNS = ("ffn1_w_in", "ffn2_w_in")


def _layer_shape(n, shard_shape):
    r, cs = shard_shape
    return (r, 4 * cs) if n in BY_COLUMNS else (4, r, cs)


def _chip_ids(x, y):
    chips = [(1 - x, y), (x, 1 - y), (1 - x, 1 - y)]
    return chips, [2 * cx + cy for cx, cy in chips]


def _half_slot(ref, n, chip, h):
    if n in BY_COLUMNS:
        hr, w = ref.shape[0] // 2, ref.shape[1] // 4
        return ref.at[pl.ds(pl.multiple_of(h * hr, 16), hr), pl.ds(pl.multiple_of(chip * w, LANES), w)]
    hr = ref.shape[1] // 2
    return ref.at[chip, pl.ds(pl.multiple_of(h * hr, 16), hr)]


def _gather_body(names, layer, in_refs, out_refs, sems, handshake):
    send_chip, recv_chip, send_sib, recv_sib, local, send_fwd, recv_fwd = sems
    n_w = len(names)
    x, y, c = lax.axis_index("x"), lax.axis_index("y"), lax.axis_index("c")
    me = 2 * x + y
    chips, chip_idx = _chip_ids(x, y)
    sibling = (x, y, 1 - c)
    if handshake:
        _shake_hands([sibling] + [(*ch, c) for ch in chips])
    remote = _remote

    sends, own_copies = [], []
    for i, n in enumerate(names):
        shard = in_refs[i].at[layer]
        hr = shard.shape[0] // 2
        src = shard.at[pl.ds(pl.multiple_of(c * hr, 16), hr)]
        mine = _half_slot(out_refs[i], n, me, c)
        own = pltpu.make_async_copy(src, mine, local.at[i])
        own.start()
        own_copies.append(own)
        sends.append(remote(src, mine, send_sib.at[i], recv_sib.at[i], sibling))
        sends[-1].start()
        for p in range(3):
            k = p * n_w + i
            sends.append(remote(src, mine, send_chip.at[k], recv_chip.at[k], (*chips[p], c)))
            sends[-1].start()
    for p in range(3):
        for i, n in enumerate(names):
            k = p * n_w + i
            landed = _half_slot(out_refs[i], n, chip_idx[p], c)
            remote(landed, landed, send_chip.at[k], recv_chip.at[k], sibling).wait_recv()
            sends.append(remote(landed, landed, send_fwd.at[k], recv_fwd.at[k], sibling))
            sends[-1].start()
    for i, n in enumerate(names):
        theirs = _half_slot(out_refs[i], n, me, 1 - c)
        remote(theirs, theirs, send_sib.at[i], recv_sib.at[i], sibling).wait_recv()
        for p in range(3):
            k = p * n_w + i
            theirs = _half_slot(out_refs[i], n, chip_idx[p], 1 - c)
            remote(theirs, theirs, send_fwd.at[k], recv_fwd.at[k], sibling).wait_recv()
    for own in own_copies:
        own.wait()
    for cp in sends:
        cp.wait_send()


def _gather_weights(names, shards, layer, after, name, sequencer_id):
    n_w = len(names)

    def body(in_refs, out_refs, sems, handshake):
        _gather_body(names, layer, in_refs, out_refs, sems, handshake)

    dma = pltpu.SemaphoreType.DMA
    sems = [dma((3 * n_w,)), dma((3 * n_w,)), dma((n_w,)), dma((n_w,)), dma((n_w,)), dma((3 * n_w,)), dma((3 * n_w,))]
    shapes = [jax.ShapeDtypeStruct(_layer_shape(n, s.shape[1:]), BF16) for n, s in zip(names, shards)]
    return _comm_call(body, list(shards) + list(after), shapes, sems, name, sequencer_id)


def _comm_call(body_fn, ins, out_shapes, sem_types, name, sequencer_id):
    n_in, n_out = len(ins), len(out_shapes)
    sequencer = sequencer_id is not None

    def body(*refs):
        body_fn(refs[:n_in], refs[n_in:n_in + n_out], refs[n_in + n_out:], sequencer)

    if sequencer:
        return list(pl.kernel(
            body, name=name, out_type=out_shapes,
            mesh=plsc.ScalarSubcoreMesh(axis_name="sequencer", num_cores=1), scratch_types=sem_types,
            compiler_params=pltpu.CompilerParams(collective_id=sequencer_id),
        )(*ins))
    any_spec = pl.BlockSpec(memory_space=pl.ANY)
    return list(pl.pallas_call(
        body, name=name, in_specs=[any_spec] * n_in, out_specs=[any_spec] * n_out, out_shape=out_shapes,
        scratch_shapes=sem_types,
    )(*ins))


def _shake_hands(peers):
    barrier = pltpu.get_barrier_semaphore()
    for peer in peers:
        pl.semaphore_signal(barrier, inc=1, device_id=peer, device_id_type=MESH)
    pl.semaphore_wait(barrier, len(peers))


def _remote(src, dst, s_sem, r_sem, to):
    return pltpu.make_async_remote_copy(src_ref=src, dst_ref=dst, send_sem=s_sem, recv_sem=r_sem,
                                        device_id=to, device_id_type=MESH)


def _rows_half(ref, n, h):
    if n in BY_COLUMNS:
        hr = ref.shape[0] // 2
        return ref.at[pl.ds(pl.multiple_of(h * hr, 16), hr)]
    hr = ref.shape[1] // 2
    return ref.at[:, pl.ds(pl.multiple_of(h * hr, 16), hr)]


def _half_form_shape(n, wire_shape):
    if n in BY_COLUMNS:
        return (wire_shape[0] // 2, wire_shape[1])
    return (wire_shape[0], wire_shape[1] // 2, wire_shape[2])


def _grad_swap_halves(names, wire, name, sequencer_id):
    n_w = len(names)

    def body(in_refs, out_refs, sems, handshake):
        send, recv = sems
        x, y, c = lax.axis_index("x"), lax.axis_index("y"), lax.axis_index("c")
        sibling = (x, y, 1 - c)
        if handshake:
            _shake_hands([sibling])
        cps = [_remote(_rows_half(in_refs[i], n, 1 - c), out_refs[i], send.at[i], recv.at[i], sibling)
               for i, n in enumerate(names)]
        for cp in cps:
            cp.start()
        for cp in cps:
            cp.wait()

    shapes = [jax.ShapeDtypeStruct(_half_form_shape(n, w.shape), w.dtype) for n, w in zip(names, wire)]
    dma = pltpu.SemaphoreType.DMA
    return _comm_call(body, list(wire), shapes, [dma((n_w,)), dma((n_w,))], name, sequencer_id)


def _grad_add_halves(g, got, ci_arr, after, name):
    g3 = g if g.ndim == 3 else g[None]
    r3 = got if got.ndim == 3 else got[None]
    s, hr, cc = r3.shape
    tb = _pick(hr, (256, 176, 128))
    nb = hr // tb

    def body(s_ref, g_ref, r_ref, after_ref, o_ref):
        o_ref[...] = (g_ref[...].astype(F32) + r_ref[...].astype(F32)).astype(BF16)

    out = pl.pallas_call(
        body, name=name,
        grid_spec=pltpu.PrefetchScalarGridSpec(
            num_scalar_prefetch=1, grid=(nb,),
            in_specs=[pl.BlockSpec((s, tb, cc), lambda i, sc: (0, sc[0] * nb + i, 0)),
                      pl.BlockSpec((s, tb, cc), lambda i, sc: (0, i, 0)),
                      pl.BlockSpec(memory_space=pl.ANY)],
            out_specs=pl.BlockSpec((s, tb, cc), lambda i, sc: (0, i, 0))),
        out_shape=jax.ShapeDtypeStruct(r3.shape, BF16),
        compiler_params=_cp("arbitrary"),
    )(ci_arr, g3, r3, after)
    return out.reshape(got.shape)


def _grad_scatter_halves(names, parts, name, sequencer_id):
    n_w = len(names)

    def slot(ref, n, chip):
        if n in BY_COLUMNS:
            w = ref.shape[1] // 4
            return ref.at[:, pl.ds(pl.multiple_of(chip * w, LANES), w)]
        return ref.at[chip]

    def body(in_refs, out_refs, sems, handshake):
        send, recv = sems
        x, y, c = lax.axis_index("x"), lax.axis_index("y"), lax.axis_index("c")
        chips, chip_idx = _chip_ids(x, y)
        if handshake:
            _shake_hands([(*ch, c) for ch in chips])
        cps = []
        for p in range(3):
            for i, n in enumerate(names):
                k = p * n_w + i
                cps.append(_remote(slot(in_refs[i], n, chip_idx[p]), out_refs[k], send.at[k], recv.at[k],
                                   (*chips[p], c)))
        for cp in cps:
            cp.start()
        for cp in cps:
            cp.wait()

    def slot_shape(n, a):
        return (a.shape[0], a.shape[1] // 4) if n in BY_COLUMNS else a.shape[1:]

    shapes = [jax.ShapeDtypeStruct(slot_shape(n, a), BF16) for _ in range(3) for n, a in zip(names, parts)]
    dma = pltpu.SemaphoreType.DMA
    out = _comm_call(body, list(parts), shapes, [dma((3 * n_w,)), dma((3 * n_w,))], name, sequencer_id)
    return [out[p * n_w:(p + 1) * n_w] for p in range(3)]


def _grad_add_slots(part, got3, n, me_arr, after, name):
    hr, cs = got3[0].shape
    tb = _pick(hr, (256, 176, 128))

    def body(s_ref, own_ref, a_ref, b_ref, c_ref, after_ref, out_ref):
        acc = own_ref[...].astype(F32) + a_ref[...].astype(F32)
        out_ref[...] = (acc + b_ref[...].astype(F32)) + c_ref[...].astype(F32)

    if n in BY_COLUMNS:
        own_spec = pl.BlockSpec((tb, cs), lambda i, s: (i, s[0]))
    else:
        own_spec = pl.BlockSpec((None, tb, cs), lambda i, s: (s[0], i, 0))
    plain = pl.BlockSpec((tb, cs), lambda i, s: (i, 0))
    return pl.pallas_call(
        body, name=name,
        grid_spec=pltpu.PrefetchScalarGridSpec(
            num_scalar_prefetch=1, grid=(hr // tb,),
            in_specs=[own_spec, plain, plain, plain, pl.BlockSpec(memory_space=pl.ANY)],
            out_specs=plain),
        out_shape=jax.ShapeDtypeStruct((hr, cs), F32),
        compiler_params=_cp("arbitrary"),
    )(me_arr, part, *got3, after)


def _grad_swap_reduced(halves, name, sequencer_id):
    n_w = len(halves)

    def body(in_refs, out_refs, sems, handshake):
        send, recv = sems
        x, y, c = lax.axis_index("x"), lax.axis_index("y"), lax.axis_index("c")
        sibling = (x, y, 1 - c)
        if handshake:
            _shake_hands([sibling])
        cps = [_remote(in_refs[i], out_refs[i], send.at[i], recv.at[i], sibling) for i in range(n_w)]
        for cp in cps:
            cp.start()
        for cp in cps:
            cp.wait()

    shapes = [jax.ShapeDtypeStruct(h.shape, F32) for h in halves]
    dma = pltpu.SemaphoreType.DMA
    return _comm_call(body, list(halves), shapes, [dma((n_w,)), dma((n_w,))], name, sequencer_id)


def _adamw_layers(w, own, other, m, v, ci_arr, name):
    _, rs, cs = w.shape
    hr = rs // 2
    tb = _pick(hr, (256, 176, 128))
    nbh = hr // tb

    def body(s_ref, w_ref, o0, t0, o1, t1, m_ref, v_ref, g_out, d_out, m_out, v_out):
        mine = (pl.program_id(1) // nbh) == s_ref[0]
        g = jnp.where(pl.program_id(0) == 0, jnp.where(mine, o0[...], t0[...]), jnp.where(mine, o1[...], t1[...]))
        g_out[...] = g
        d_out[...], m_out[...], v_out[...] = _adamw_math(w_ref[...], g, m_ref[...], v_ref[...])

    both = pl.BlockSpec((None, tb, cs), lambda l, i, s: (l, i, 0))

    def half(layer, is_own):
        def index(l, i, s):
            own_block = (i // nbh) == s[0]
            use = (l == layer) & (own_block if is_own else jnp.logical_not(own_block))
            return (jnp.where(use, i % nbh, 0), 0)
        return pl.BlockSpec((tb, cs), index)

    return pl.pallas_call(
        body, name=name,
        grid_spec=pltpu.PrefetchScalarGridSpec(
            num_scalar_prefetch=1, grid=(DEPTH, rs // tb),
            in_specs=[both, half(0, True), half(0, False), half(1, True), half(1, False), both, both],
            out_specs=[both] * 4),
        out_shape=[jax.ShapeDtypeStruct(w.shape, F32)] * 4,
        compiler_params=_cp("arbitrary", "arbitrary"),
    )(ci_arr, w, own[0], other[0], own[1], other[1], m, v)


def _shift_down(v, s, t_iota):
    return jnp.where(t_iota >= s, pltpu.roll(v, s, 0), 0.0)


def _shift_up(v, s, t_iota, t):
    return jnp.where(t_iota < t - s, pltpu.roll(v, t - s, 0), 0.0)


def _ssd_conv_fwd(proj, w, b):
    t = proj.shape[0]
    k_w = SSD_CONV_K

    def body(x_ref, w_ref, b_ref, o_ref):
        x = x_ref[...]
        ti = lax.broadcasted_iota(jnp.int32, x.shape, 0)
        pre = x * w_ref[k_w - 1:k_w, :] + b_ref[...]
        for s in range(1, k_w):
            pre = pre + _shift_down(x, s, ti) * w_ref[k_w - 1 - s:k_w - s, :]
        o_ref[...] = pre * _sigmoid(pre)

    off = P_XBC // LANES
    return pl.pallas_call(
        body, name="ssd_conv_fwd", grid=(SSD_CONV_DIM // LANES,),
        in_specs=[pl.BlockSpec((t, LANES), lambda j: (0, off + j)),
                  pl.BlockSpec((k_w, LANES), lambda j: (0, j)),
                  pl.BlockSpec((1, LANES), lambda j: (0, j))],
        out_specs=pl.BlockSpec((t, LANES), lambda j: (0, j)),
        out_shape=jax.ShapeDtypeStruct((t, SSD_CONV_DIM), F32),
        compiler_params=_cp("parallel"),
    )(proj, w, b)


def _ssd_conv_bwd(dxbc, proj, w, b):
    t = proj.shape[0]
    k_w = SSD_CONV_K

    def body(d_ref, x_ref, w_ref, b_ref, dx_ref, dw_ref, db_ref):
        x = x_ref[...]
        ti = lax.broadcasted_iota(jnp.int32, x.shape, 0)
        shifted = [x] + [_shift_down(x, s, ti) for s in range(1, k_w)]
        pre = b_ref[...] + shifted[0] * w_ref[k_w - 1:k_w, :]
        for s in range(1, k_w):
            pre = pre + shifted[s] * w_ref[k_w - 1 - s:k_w - s, :]
        sg = _sigmoid(pre)
        dpre = d_ref[...] * (sg * (1.0 + pre * (1.0 - sg)))
        db_ref[...] = jnp.sum(dpre, axis=0, keepdims=True)
        dx = dpre * w_ref[k_w - 1:k_w, :]
        for s in range(k_w):
            dw_ref[k_w - 1 - s:k_w - s, :] = jnp.sum(dpre * shifted[s], axis=0, keepdims=True)
            if s:
                dx = dx + _shift_up(dpre, s, ti, t) * w_ref[k_w - 1 - s:k_w - s, :]
        dx_ref[...] = dx

    off = P_XBC // LANES
    return pl.pallas_call(
        body, name="ssd_conv_bwd", grid=(SSD_CONV_DIM // LANES,),
        in_specs=[pl.BlockSpec((t, LANES), lambda j: (0, j)),
                  pl.BlockSpec((t, LANES), lambda j: (0, off + j)),
                  pl.BlockSpec((k_w, LANES), lambda j: (0, j)),
                  pl.BlockSpec((1, LANES), lambda j: (0, j))],
        out_specs=[pl.BlockSpec((t, LANES), lambda j: (0, j)),
                   pl.BlockSpec((k_w, LANES), lambda j: (0, j)),
                   pl.BlockSpec((1, LANES), lambda j: (0, j))],
        out_shape=[jax.ShapeDtypeStruct((t, SSD_CONV_DIM), F32),
                   jax.ShapeDtypeStruct((k_w, SSD_CONV_DIM), F32),
                   jax.ShapeDtypeStruct((1, SSD_CONV_DIM), F32)],
        compiler_params=_cp("parallel"),
    )(dxbc, proj, w, b)


def _shortconv_fwd(proj, w):
    t = proj.shape[0]
    nb = SC_W // LANES
    off = P_SC // LANES

    def body(b_ref, c_ref, x_ref, w_ref, o_ref):
        u = c_ref[...] * x_ref[...]
        ti = lax.broadcasted_iota(jnp.int32, u.shape, 0)
        cv = u * w_ref[SC_K - 1:SC_K, :]
        for s in range(1, SC_K):
            cv = cv + _shift_down(u, s, ti) * w_ref[SC_K - 1 - s:SC_K - s, :]
        o_ref[...] = (b_ref[...] * cv).astype(BF16)

    return pl.pallas_call(
        body, name="shortconv_fwd", grid=(nb,),
        in_specs=[pl.BlockSpec((t, LANES), lambda j: (0, off + j)),
                  pl.BlockSpec((t, LANES), lambda j: (0, off + nb + j)),
                  pl.BlockSpec((t, LANES), lambda j: (0, off + 2 * nb + j)),
                  pl.BlockSpec((SC_K, LANES), lambda j: (0, j))],
        out_specs=pl.BlockSpec((t, LANES), lambda j: (0, j)),
        out_shape=jax.ShapeDtypeStruct((t, SC_W), BF16),
        compiler_params=_cp("parallel"),
    )(proj, proj, proj, w)


def _shortconv_bwd(dy, dy_off, proj, w):
    t = proj.shape[0]
    nb = SC_W // LANES
    off = P_SC // LANES
    doff = dy_off // LANES

    def body(d_ref, b_ref, c_ref, x_ref, w_ref, db_ref, dc_ref, dx_ref, dw_ref):
        cg, xin = c_ref[...], x_ref[...]
        u = cg * xin
        ti = lax.broadcasted_iota(jnp.int32, u.shape, 0)
        shifted = [u] + [_shift_down(u, s, ti) for s in range(1, SC_K)]
        cv = shifted[0] * w_ref[SC_K - 1:SC_K, :]
        for s in range(1, SC_K):
            cv = cv + shifted[s] * w_ref[SC_K - 1 - s:SC_K - s, :]
        d = d_ref[...]
        db_ref[...] = (d * cv).astype(BF16)
        dcv = d * b_ref[...]
        du = dcv * w_ref[SC_K - 1:SC_K, :]
        for s in range(SC_K):
            dw_ref[SC_K - 1 - s:SC_K - s, :] = jnp.sum(dcv * shifted[s], axis=0, keepdims=True)
            if s:
                du = du + _shift_up(dcv, s, ti, t) * w_ref[SC_K - 1 - s:SC_K - s, :]
        dc_ref[...] = (du * xin).astype(BF16)
        dx_ref[...] = (du * cg).astype(BF16)

    tile = pl.BlockSpec((t, LANES), lambda j: (0, j))
    outs = pl.pallas_call(
        body, name="shortconv_bwd", grid=(nb,),
        in_specs=[pl.BlockSpec((t, LANES), lambda j: (0, doff + j)),
                  pl.BlockSpec((t, LANES), lambda j: (0, off + j)),
                  pl.BlockSpec((t, LANES), lambda j: (0, off + nb + j)),
                  pl.BlockSpec((t, LANES), lambda j: (0, off + 2 * nb + j)),
                  pl.BlockSpec((SC_K, LANES), lambda j: (0, j))],
        out_specs=[tile, tile, tile, pl.BlockSpec((SC_K, LANES), lambda j: (0, j))],
        out_shape=[jax.ShapeDtypeStruct((t, SC_W), BF16)] * 3 + [jax.ShapeDtypeStruct((SC_K, SC_W), F32)],
        compiler_params=_cp("parallel"),
    )(dy, proj, proj, proj, w)
    return outs


def _cum_logf(proj, fbias_row):
    t = proj.shape[0]
    blk = CUM_BLOCK

    def body(p_ref, b_ref, o_ref, carry_ref):
        i = pl.program_id(0)

        @pl.when(i == 0)
        def _():
            carry_ref[...] = jnp.zeros_like(carry_ref)

        lf = -_softplus(-(p_ref[...] + b_ref[...]))
        r = lax.broadcasted_iota(jnp.int32, (blk, blk), 0)
        c = lax.broadcasted_iota(jnp.int32, (blk, blk), 1)
        tri = (r >= c).astype(BF16)
        o_ref[...] = _tri_left(tri, lf) + carry_ref[...]
        carry_ref[...] = o_ref[blk - 1:blk, :]

    return pl.pallas_call(
        body, name="cum_logf", grid=(t // blk,),
        in_specs=[pl.BlockSpec((blk, LANES), lambda i: (i, P_PAD // LANES)),
                  pl.BlockSpec((1, LANES), lambda i: (0, 0))],
        out_specs=pl.BlockSpec((blk, LANES), lambda i: (i, 0)),
        out_shape=jax.ShapeDtypeStruct((t, LANES), F32),
        scratch_shapes=[pltpu.VMEM((1, LANES), F32)],
        compiler_params=_cp("arbitrary"),
    )(proj, fbias_row)


def _pad_block_bwd(dcf, ddt, proj, fbias_row):
    t = proj.shape[0]
    blk = CUM_BLOCK
    nb = t // blk

    def body(dcf_ref, ddt_ref, p_ref, b_ref, o_ref, db_ref, carry_ref):
        i = pl.program_id(0)

        @pl.when(i == 0)
        def _():
            carry_ref[...] = jnp.zeros_like(carry_ref)
            db_ref[...] = jnp.zeros_like(db_ref)

        r = lax.broadcasted_iota(jnp.int32, (blk, blk), 0)
        c = lax.broadcasted_iota(jnp.int32, (blk, blk), 1)
        tri = (r <= c).astype(BF16)
        rev = _tri_left(tri, dcf_ref[...]) + carry_ref[...]
        carry_ref[...] = jnp.sum(dcf_ref[...], axis=0, keepdims=True) + carry_ref[...]
        lane = lax.broadcasted_iota(jnp.int32, (blk, LANES), 1)
        is_f = (lane >= PAD_F0) & (lane < PAD_F0 + FOX_HEADS)
        df = jnp.where(is_f, rev * _sigmoid(-(p_ref[...] + b_ref[...])), 0.0)
        db_ref[...] += jnp.sum(df, axis=0, keepdims=True)
        o_ref[...] = jnp.where(lane < PAD_DT0 + SSD_HEADS, ddt_ref[...], df).astype(BF16)

    return pl.pallas_call(
        body, name="pad_block_bwd", grid=(nb,),
        in_specs=[pl.BlockSpec((blk, LANES), lambda i: (nb - 1 - i, 0)),
                  pl.BlockSpec((blk, LANES), lambda i: (nb - 1 - i, 0)),
                  pl.BlockSpec((blk, LANES), lambda i: (nb - 1 - i, P_PAD // LANES)),
                  pl.BlockSpec((1, LANES), lambda i: (0, 0))],
        out_specs=[pl.BlockSpec((blk, LANES), lambda i: (nb - 1 - i, 0)),
                   pl.BlockSpec((1, LANES), lambda i: (0, 0))],
        out_shape=[jax.ShapeDtypeStruct((t, LANES), BF16), jax.ShapeDtypeStruct((1, LANES), F32)],
        scratch_shapes=[pltpu.VMEM((1, LANES), F32)],
        compiler_params=_cp("arbitrary"),
    )(dcf, ddt, proj, fbias_row)


def _att_scores(q, k, ck, diagonal, blk):
    s = _dot_nt(q, k) - ck
    if diagonal:
        r = lax.broadcasted_iota(jnp.int32, (blk, blk), 0)
        c = lax.broadcasted_iota(jnp.int32, (blk, blk), 1)
        s = jnp.where(r >= c, s, NEG)
    return s


def _fox_fwd(q, k, v, cf_row):
    h, t, hd = q.shape
    blk = min(ATT_BLOCK, t)
    nb = t // blk
    pair = 2

    def body(q_ref, k_ref, v_ref, ck_ref, o_ref, lse_ref):
        qi = pl.program_id(1)
        qv = [q_ref[hh] for hh in range(pair)]

        def step(j, carry, diagonal):
            off = pl.multiple_of(j * blk, blk)
            out = []
            for hh in range(pair):
                m, l, acc = carry[hh]
                s = _att_scores(qv[hh], k_ref[hh, pl.ds(off, blk), :], ck_ref[hh, :, pl.ds(off, blk)], diagonal, blk)
                m_new = jnp.maximum(m, jnp.max(s, axis=1, keepdims=True))
                alpha = jnp.exp(m - m_new)
                p = jnp.exp(s - m_new)
                l = alpha * l + jnp.sum(p, axis=1, keepdims=True)
                acc = alpha * acc + _dot(p.astype(BF16), v_ref[hh, pl.ds(off, blk), :])
                out.append((m_new, l, acc))
            return tuple(out)

        one = (jnp.full((blk, 1), NEG, F32), jnp.zeros((blk, 1), F32), jnp.zeros((blk, hd), F32))
        carry = lax.fori_loop(0, qi, lambda j, cr: step(j, cr, False), (one,) * pair)
        for hh, (m, l, acc) in enumerate(step(qi, carry, True)):
            o_ref[hh] = acc / l
            lse_ref[hh] = m + jnp.log(l)

    qmap = lambda hp, i: (hp, i, 0)
    whole = lambda hp, i: (hp, 0, 0)
    return pl.pallas_call(
        body, name="fox_fwd", grid=(h // pair, nb),
        in_specs=[pl.BlockSpec((pair, blk, hd), qmap), pl.BlockSpec((pair, t, hd), whole),
                  pl.BlockSpec((pair, t, hd), whole), pl.BlockSpec((pair, 1, t), whole)],
        out_specs=[pl.BlockSpec((pair, blk, hd), qmap), pl.BlockSpec((pair, blk, 1), qmap)],
        out_shape=[jax.ShapeDtypeStruct((h, t, hd), F32), jax.ShapeDtypeStruct((h, t, 1), F32)],
        compiler_params=_cp("parallel", "arbitrary"),
    )(q, k, v, cf_row)


def _fox_bwd(q, k, v, cf_row, o, lse, do):
    h, t, hd = q.shape
    blk = min(ATT_BLOCK, t)
    nb = t // blk

    def body(q_ref, k_ref, v_ref, ck_ref, o_ref, lse_ref, do_ref,
             dq_ref, dk_ref, dv_ref, dcq_ref, dck_ref, delta_s):
        kj = pl.program_id(1)

        @pl.when(kj == 0)
        def _():
            dq_ref[...] = jnp.zeros_like(dq_ref)
            dcq_ref[...] = jnp.zeros_like(dcq_ref)

            def fill(i, _):
                rows = pl.ds(pl.multiple_of(i * blk, blk), blk)
                delta_s[rows, :] = jnp.sum(do_ref[rows, :] * o_ref[rows, :], axis=1, keepdims=True)
                return 0

            lax.fori_loop(0, nb, fill, 0)

        kb, vb, ck = k_ref[...], v_ref[...], ck_ref[...]

        def step(i, carry, diagonal):
            dk, dv, dck = carry
            rows = pl.ds(pl.multiple_of(i * blk, blk), blk)
            qb = q_ref[rows, :]
            do_b = do_ref[rows, :].astype(BF16)
            s = _att_scores(qb, kb, ck, diagonal, blk)
            p = jnp.exp(s - lse_ref[rows, :])
            dv = dv + _dot_tn(p.astype(BF16), do_b)
            ds = p * (_dot_nt(do_b, vb) - delta_s[rows, :])
            ds_b = ds.astype(BF16)
            dk = dk + _dot_tn(ds_b, qb)
            dq_ref[rows, :] += _dot(ds_b, kb)
            dcq_ref[rows, :] += jnp.sum(ds, axis=1, keepdims=True)
            dck = dck - jnp.sum(ds, axis=0, keepdims=True)
            return dk, dv, dck

        init = (jnp.zeros((blk, hd), F32), jnp.zeros((blk, hd), F32), jnp.zeros((1, blk), F32))
        carry = step(kj, init, True)
        dk, dv, dck = lax.fori_loop(kj + 1, nb, lambda i, cr: step(i, cr, False), carry)
        dk_ref[...] = dk
        dv_ref[...] = dv
        dck_ref[...] = dck

    kmap = lambda hh, j: (hh, j, 0)
    whole = lambda hh, j: (hh, 0, 0)
    return pl.pallas_call(
        body, name="fox_bwd", grid=(h, nb),
        in_specs=[pl.BlockSpec((None, t, hd), whole), pl.BlockSpec((None, blk, hd), kmap),
                  pl.BlockSpec((None, blk, hd), kmap), pl.BlockSpec((None, 1, blk), lambda hh, j: (hh, 0, j)),
                  pl.BlockSpec((None, t, hd), whole), pl.BlockSpec((None, t, 1), whole),
                  pl.BlockSpec((None, t, hd), whole)],
        out_specs=[pl.BlockSpec((None, t, hd), whole), pl.BlockSpec((None, blk, hd), kmap),
                   pl.BlockSpec((None, blk, hd), kmap), pl.BlockSpec((None, t, 1), whole),
                   pl.BlockSpec((None, 1, blk), lambda hh, j: (hh, 0, j))],
        out_shape=[jax.ShapeDtypeStruct((h, t, hd), F32), jax.ShapeDtypeStruct((h, t, hd), F32),
                   jax.ShapeDtypeStruct((h, t, hd), F32), jax.ShapeDtypeStruct((h, t, 1), F32),
                   jax.ShapeDtypeStruct((h, 1, t), F32)],
        scratch_shapes=[pltpu.VMEM((t, 1), F32)],
        compiler_params=_cp("parallel", "arbitrary"),
    )(q, k, v, cf_row, o, lse, do)


def _lane_col(v, h):
    lane = lax.broadcasted_iota(jnp.int32, v.shape, 1)
    return jnp.sum(jnp.where(lane == h, v, 0.0), axis=1, keepdims=True)


def _sub_row(v, h):
    sub = lax.broadcasted_iota(jnp.int32, v.shape, 0)
    return jnp.sum(jnp.where(sub == h, v, 0.0), axis=0, keepdims=True)


def _ssd_decays(pad, pad_t, dtb_row, alog_row, dtb_col, alog_col, blk):
    r = lax.broadcasted_iota(jnp.int32, (blk, blk), 0)
    c = lax.broadcasted_iota(jnp.int32, (blk, blk), 1)
    tril = r >= c
    dt_c = _softplus(pad + dtb_row)
    acs_c = _tri_left(tril.astype(BF16), dt_c * (-jnp.exp(alog_row)))
    dt_r = _softplus(pad_t + dtb_col)
    acs_r = _tri_right(dt_r * (-jnp.exp(alog_col)), (r <= c).astype(BF16))
    rows = lax.broadcasted_iota(jnp.int32, acs_c.shape, 0)
    acs_last = jnp.sum(jnp.where(rows == blk - 1, acs_c, 0.0), axis=0, keepdims=True)
    return dt_c, acs_c, acs_r, acs_last, tril


def _pair_terms(pair, dt_c, acs_c, acs_r, acs_last, d_row, blk):
    lane = lax.broadcasted_iota(jnp.int32, (blk, LANES), 1)
    lo = lane < 64
    lo_row = lax.broadcasted_iota(jnp.int32, (1, LANES), 1) < 64
    h0, h1 = 2 * pair, 2 * pair + 1
    col = [_lane_col(acs_c, h0), _lane_col(acs_c, h1)]
    row = [_sub_row(acs_r, h0), _sub_row(acs_r, h1)]
    last = [_lane_col(acs_last, h0), _lane_col(acs_last, h1)]
    dt_p = jnp.where(lo, _lane_col(dt_c, h0), _lane_col(dt_c, h1))
    e_p = jnp.where(lo, jnp.exp(col[0]), jnp.exp(col[1]))
    w_p = jnp.where(lo, jnp.exp(last[0] - col[0]), jnp.exp(last[1] - col[1]))
    decay_p = jnp.where(lo_row, jnp.exp(last[0]), jnp.exp(last[1]))
    d_p = jnp.where(lo_row, _lane_col(d_row, h0), _lane_col(d_row, h1))
    return lo, lo_row, col, row, last, dt_p, e_p, w_p, decay_p, d_p


def _ssd_specs(t, blk, rev):
    nc = t // blk
    ix = (lambda i: nc - 1 - i) if rev else (lambda i: i)
    xbc = pl.BlockSpec((blk, SSD_CONV_DIM), lambda i: (ix(i), 0))
    pad = pl.BlockSpec((blk, LANES), lambda i: (ix(i), P_PAD // LANES))
    pad_t = pl.BlockSpec((LANES, blk), lambda i: (0, ix(i)))
    z = pl.BlockSpec((blk, SSD_W), lambda i: (ix(i), 0))
    row = pl.BlockSpec((1, LANES), lambda i: (0, 0))
    colv = pl.BlockSpec((LANES, 1), lambda i: (0, 0))
    ng = pl.BlockSpec((1, SSD_W), lambda i: (0, 0))
    y = pl.BlockSpec((blk, SSD_W), lambda i: (ix(i), 0))
    st = pl.BlockSpec((None, 4, LANES, LANES), lambda i: (ix(i), 0, 0, 0))
    return nc, xbc, pad, pad_t, z, row, colv, ng, y, st


def _ssd_fwd(xbc, proj, pad_t, dtb_row, alog_row, d_row, dtb_col, alog_col, ng):
    t = xbc.shape[0]
    blk = min(SSD_CHUNK, t)
    nc, s_xbc, s_pad, s_padt, s_z, s_row, s_col, s_ng, s_y, s_st = _ssd_specs(t, blk, False)

    def body(xbc_ref, pad_ref, padt_ref, z_ref, dtb_ref, alog_ref, d_ref, dtbc_ref, alogc_ref, ng_ref,
             out_ref, ypre_ref, st_ref, state):
        @pl.when(pl.program_id(0) == 0)
        def _():
            state[...] = jnp.zeros_like(state)

        dt_c, acs_c, acs_r, acs_last, tril = _ssd_decays(
            pad_ref[...], padt_ref[...], dtb_ref[...], alog_ref[...], dtbc_ref[...], alogc_ref[...], blk)
        ys = []
        g_mat = {}
        for pair in range(4):
            g = pair // 2
            bg = xbc_ref[:, SSD_W + LANES * g:SSD_W + LANES * (g + 1)].astype(BF16)
            cg = xbc_ref[:, SSD_W + 2 * LANES + LANES * g:SSD_W + 2 * LANES + LANES * (g + 1)].astype(BF16)
            if g not in g_mat:
                g_mat[g] = _dot_nt(cg, bg)
            xs_p = xbc_ref[:, LANES * pair:LANES * (pair + 1)]
            lo, _, col, row, _, dt_p, e_p, w_p, decay_p, d_p = _pair_terms(
                pair, dt_c, acs_c, acs_r, acs_last, d_ref[...], blk)
            x_p = xs_p * dt_p
            y = None
            for hh in range(2):
                lm = jnp.exp(jnp.where(tril, col[hh] - row[hh], NEG))
                m_h = (g_mat[g] * lm).astype(BF16)
                x_h = jnp.where(lo if hh == 0 else ~lo, x_p, 0.0).astype(BF16)
                y_h = _dot(m_h, x_h)
                y = y_h if y is None else y + y_h
            st_in = state[pair]
            st_ref[pair] = st_in
            y = y + e_p * _dot(cg, st_in.astype(BF16))
            state[pair] = decay_p * st_in + _dot_tn(bg, (x_p * w_p).astype(BF16))
            ys.append(y + d_p * xs_p)
        y_all = jnp.concatenate(ys, axis=1)
        ypre_ref[...] = y_all
        z = z_ref[...]
        y2 = y_all * (z * _sigmoid(z))
        outs = []
        for g in range(2):
            seg = y2[:, 256 * g:256 * (g + 1)]
            rr = lax.rsqrt(jnp.mean(seg * seg, axis=-1, keepdims=True) + RMS_EPS)
            outs.append(seg * rr * ng_ref[:, 256 * g:256 * (g + 1)])
        out_ref[...] = jnp.concatenate(outs, axis=1).astype(BF16)

    return pl.pallas_call(
        body, name="ssd_scan_fwd", grid=(nc,),
        in_specs=[s_xbc, s_pad, s_padt, s_z, s_row, s_row, s_row, s_col, s_col, s_ng],
        out_specs=[s_y, s_y, s_st],
        out_shape=[jax.ShapeDtypeStruct((t, SSD_W), BF16), jax.ShapeDtypeStruct((t, SSD_W), F32),
                   jax.ShapeDtypeStruct((nc, 4, LANES, LANES), F32)],
        scratch_shapes=[pltpu.VMEM((4, LANES, LANES), F32)],
        compiler_params=_cp("arbitrary"),
    )(xbc, proj, pad_t, proj, dtb_row, alog_row, d_row, dtb_col, alog_col, ng)


def _ssd_bwd(dout, dout_off, xbc, proj, pad_t, ypre, states, dtb_row, alog_row, d_row, dtb_col, alog_col, ng):
    t = xbc.shape[0]
    blk = min(SSD_CHUNK, t)
    nc, s_xbc, s_pad, s_padt, s_z, s_row, s_col, s_ng, s_y, s_st = _ssd_specs(t, blk, True)
    s_dout = pl.BlockSpec((blk, SSD_W), lambda i: (nc - 1 - i, dout_off // SSD_W))

    def body(dout_ref, xbc_ref, pad_ref, padt_ref, z_ref, ypre_ref, st_ref, dtb_ref, alog_ref, d_ref,
             dtbc_ref, alogc_ref, ng_ref, dxbc_ref, ddt_ref, dz_ref, acc_ref, dng_ref, dstate):
        @pl.when(pl.program_id(0) == 0)
        def _():
            dstate[...] = jnp.zeros_like(dstate)
            acc_ref[...] = jnp.zeros_like(acc_ref)
            dng_ref[...] = jnp.zeros_like(dng_ref)

        pad = pad_ref[...]
        dt_c, acs_c, acs_r, acs_last, tril = _ssd_decays(
            pad, padt_ref[...], dtb_ref[...], alog_ref[...], dtbc_ref[...], alogc_ref[...], blk)
        a_row = -jnp.exp(alog_ref[...])

        z = z_ref[...]
        sz = _sigmoid(z)
        silu_z = z * sz
        y_pre = ypre_ref[...]
        y2 = y_pre * silu_z
        dy2 = []
        for g in range(2):
            sl = slice(256 * g, 256 * (g + 1))
            seg = y2[:, sl]
            rr = lax.rsqrt(jnp.mean(seg * seg, axis=-1, keepdims=True) + RMS_EPS)
            nrm = seg * rr
            d_seg = dout_ref[:, sl]
            dng_ref[:, sl] += jnp.sum(d_seg * nrm, axis=0, keepdims=True)
            dn = d_seg * ng_ref[:, sl]
            dy2.append(rr * (dn - nrm * jnp.mean(dn * nrm, axis=-1, keepdims=True)))
        dy2 = jnp.concatenate(dy2, axis=1)
        dz_ref[...] = (dy2 * y_pre * (sz * (1.0 + z * (1.0 - sz)))).astype(BF16)
        dy_all = dy2 * silu_z

        lane_row = lax.broadcasted_iota(jnp.int32, (1, LANES), 1)
        lane_blk = lax.broadcasted_iota(jnp.int32, (blk, LANES), 1)
        row_col = lax.broadcasted_iota(jnp.int32, (blk, 1), 0)
        ddt = jnp.zeros((blk, LANES), F32)
        dacs = jnp.zeros((blk, LANES), F32)
        dd_row = jnp.zeros((1, LANES), F32)
        ones_b = jnp.ones((blk, LANES), BF16)
        dxs = []
        d_b = [None, None]
        d_c = [None, None]
        d_g = [None, None]
        bgs, cgs = {}, {}
        g_mat = {}
        for pair in range(4):
            g = pair // 2
            if g not in g_mat:
                bgs[g] = xbc_ref[:, SSD_W + LANES * g:SSD_W + LANES * (g + 1)].astype(BF16)
                cgs[g] = xbc_ref[:, SSD_W + 2 * LANES + LANES * g:SSD_W + 2 * LANES + LANES * (g + 1)].astype(BF16)
                g_mat[g] = _dot_nt(cgs[g], bgs[g])
            bg, cg = bgs[g], cgs[g]
            xs_p = xbc_ref[:, LANES * pair:LANES * (pair + 1)]
            lo, lo_row, col, row, last, dt_p, e_p, w_p, decay_p, d_p = _pair_terms(
                pair, dt_c, acs_c, acs_r, acs_last, d_ref[...], blk)
            x_p = xs_p * dt_p
            dy_p = dy_all[:, LANES * pair:LANES * (pair + 1)]
            st_in = st_ref[pair]
            dst = dstate[pair]
            dx_diag = None
            for hh in range(2):
                sel = lo if hh == 0 else ~lo
                lm = jnp.exp(jnp.where(tril, col[hh] - row[hh], NEG))
                m_f = g_mat[g] * lm
                m_h = m_f.astype(BF16)
                x_h = jnp.where(sel, x_p, 0.0).astype(BF16)
                dy_h = jnp.where(sel, dy_p, 0.0).astype(BF16)
                dxd = _dot_tn(m_h, dy_h)
                dm = _dot_nt(dy_h, x_h)
                dg_h = dm * lm
                p_b = (dm * m_f).astype(BF16)
                dacs = dacs + jnp.where(lane_blk == 2 * pair + hh, _dot(p_b, ones_b) - _dot_tn(p_b, ones_b), 0.0)
                dx_diag = dxd if dx_diag is None else dx_diag + dxd
                d_g[g] = dg_h if d_g[g] is None else d_g[g] + dg_h
            st_b = st_in.astype(BF16)
            dst_b = dst.astype(BF16)
            y_off = e_p * _dot(cg, st_b)
            edy = (e_p * dy_p).astype(BF16)
            dc_off = _dot_nt(edy, st_b)
            d_c[g] = dc_off if d_c[g] is None else d_c[g] + dc_off
            dstate[pair] = decay_p * dst + _dot_tn(cg, edy)
            dx_state = _dot(bg, dst_b) * w_p
            db_st = _dot_nt((x_p * w_p).astype(BF16), dst_b)
            d_b[g] = db_st if d_b[g] is None else d_b[g] + db_st
            dx = dx_diag + dx_state
            dxs.append(dx * dt_p + d_p * dy_p)
            prod_dt = dx * xs_p
            prod_acs = dy_p * y_off - x_p * dx_state
            prod_st = x_p * dx_state
            prod_d = dy_p * xs_p
            st_prod = jnp.sum(dst * st_in, axis=0, keepdims=True)
            for hh in range(2):
                h = 2 * pair + hh
                sel = lo if hh == 0 else ~lo
                sel_row = lo_row if hh == 0 else ~lo_row
                ddt_h = jnp.sum(jnp.where(sel, prod_dt, 0.0), axis=1, keepdims=True)
                dacs_h = jnp.sum(jnp.where(sel, prod_acs, 0.0), axis=1, keepdims=True)
                tail = jnp.sum(jnp.sum(jnp.where(sel, prod_st, 0.0), axis=1, keepdims=True), axis=0, keepdims=True)
                tail = tail + jnp.exp(last[hh]) * jnp.sum(jnp.where(sel_row, st_prod, 0.0), axis=1, keepdims=True)
                dacs_h = dacs_h + jnp.where(row_col == blk - 1, tail, 0.0)
                dd_h = jnp.sum(jnp.sum(jnp.where(sel, prod_d, 0.0), axis=1, keepdims=True), axis=0, keepdims=True)
                ddt = ddt + jnp.where(lane_blk == h, ddt_h, 0.0)
                dacs = dacs + jnp.where(lane_blk == h, dacs_h, 0.0)
                dd_row = dd_row + jnp.where(lane_row == h, dd_h, 0.0)
        for g in range(2):
            dg_b = d_g[g].astype(BF16)
            d_c[g] = d_c[g] + _dot(dg_b, bgs[g])
            d_b[g] = d_b[g] + _dot_tn(dg_b, cgs[g])
        r = lax.broadcasted_iota(jnp.int32, (blk, blk), 0)
        c = lax.broadcasted_iota(jnp.int32, (blk, blk), 1)
        da = _tri_left((r <= c).astype(BF16), dacs)
        ddt = ddt + da * a_row
        d_raw = ddt * _sigmoid(pad + dtb_ref[...])
        ddt_ref[...] = d_raw
        acc_ref[0:1, :] += jnp.sum(da * dt_c, axis=0, keepdims=True) * a_row
        acc_ref[1:2, :] += dd_row
        acc_ref[2:3, :] += jnp.sum(d_raw, axis=0, keepdims=True)
        dxbc_ref[...] = jnp.concatenate(dxs + d_b + d_c, axis=1)

    return pl.pallas_call(
        body, name="ssd_scan_bwd", grid=(nc,),
        in_specs=[s_dout, s_xbc, s_pad, s_padt, s_z, s_y, s_st, s_row, s_row, s_row, s_col, s_col, s_ng],
        out_specs=[s_xbc, pl.BlockSpec((blk, LANES), lambda i: (nc - 1 - i, 0)), s_y,
                   pl.BlockSpec((8, LANES), lambda i: (0, 0)), s_ng],
        out_shape=[jax.ShapeDtypeStruct((t, SSD_CONV_DIM), F32), jax.ShapeDtypeStruct((t, LANES), F32),
                   jax.ShapeDtypeStruct((t, SSD_W), BF16), jax.ShapeDtypeStruct((8, LANES), F32),
                   jax.ShapeDtypeStruct((1, SSD_W), F32)],
        scratch_shapes=[pltpu.VMEM((4, LANES, LANES), F32)],
        compiler_params=_cp("arbitrary"),
    )(dout, xbc, proj, pad_t, proj, ypre, states, dtb_row, alog_row, d_row, dtb_col, alog_col, ng)


def _pad_lanes(v, off):
    return jnp.zeros((1, LANES), F32).at[0, off:off + v.shape[0]].set(v)


def _perm_mix_w_in(w):
    z, xbc, dt = w[:, 0:512], w[:, 512:1536], w[:, 1536:1544]
    qkv, f, sc = w[:, 1544:2312], w[:, 2312:2316], w[:, 2316:3084]
    padblk = jnp.zeros((w.shape[0], LANES), w.dtype).at[:, PAD_DT0:PAD_DT0 + 8].set(dt).at[:, PAD_F0:PAD_F0 + 4].set(f)
    return jnp.concatenate([z, xbc, qkv, sc, padblk], axis=1)


def _unperm_mix_w_in(wp):
    z, xbc, qkv, sc = wp[:, 0:512], wp[:, 512:1536], wp[:, 1536:2304], wp[:, 2304:3072]
    dt, f = wp[:, P_PAD + PAD_DT0:P_PAD + PAD_DT0 + 8], wp[:, P_PAD + PAD_F0:P_PAD + PAD_F0 + 4]
    return jnp.concatenate([z, xbc, dt, qkv, f, sc], axis=1)


def _heads(m):
    return jnp.transpose(m.reshape(m.shape[0], FOX_HEADS, FOX_HD), (1, 0, 2))


def _unheads(m):
    return jnp.transpose(m, (1, 0, 2)).reshape(m.shape[1], FOX_W)


def _ffn_fwd(h, w_in, w_out, tag):
    u, a = _ffn_in_swiglu(h, w_in, f"ffn_in_{tag}")
    y = _matmul(a, w_out, "nn", F32, f"ffn_out_{tag}")
    return y, (h, u, a)


def _ffn_bwd(dy, saved, w_in, w_out, tag):
    h, u, a = saved
    du = _ffn_out_dx_swiglu(dy, w_out, u, f"ffn_out_dx_{tag}")
    dw_out = _matmul(a, dy, "tn", BF16, f"ffn_out_dw_{tag}")
    dh = _matmul(du, w_in, "nt", F32, f"ffn_in_dx_{tag}", halves="a")
    dw_in = _matmul(h, du, "tn", BF16, f"ffn_in_dw_{tag}", halves="b")
    return dh, dw_in, dw_out


def _mix_fwd(h, wp, w_out, sp, tag):
    proj = _matmul(h, wp, "nn", F32, f"mix_in_{tag}")
    pad_t = jnp.transpose(proj[:, P_PAD:P_PAD + LANES])
    xbc = _ssd_conv_fwd(proj, sp["conv_w"], sp["conv_b"])
    y_ssd, ypre, states = _ssd_fwd(xbc, proj, pad_t, sp["dtb_row"], sp["alog_row"], sp["d_row"],
                                   sp["dtb_col"], sp["alog_col"], sp["ng"])
    cf = _cum_logf(proj, sp["fbias_row"])
    cf_row = jnp.transpose(cf[:, PAD_F0:PAD_F0 + FOX_HEADS])[:, None, :]
    qkv = lax.optimization_barrier(proj[:, P_QKV:P_QKV + 768])
    q = _heads((qkv[:, 0:256] * FOX_SCALE).astype(BF16))
    k = _heads(qkv[:, 256:512].astype(BF16))
    v = _heads(qkv[:, 512:768].astype(BF16))
    o, lse = _fox_fwd(q, k, v, cf_row)
    y_sc = _shortconv_fwd(proj, sp["sconv_w"])
    ymix = jnp.concatenate([y_ssd, _unheads(o).astype(BF16), y_sc], axis=1)
    y = _matmul(ymix, w_out, "nn", F32, f"mix_out_{tag}")
    return y, (h, proj, pad_t, xbc, ypre, states, q, k, v, cf_row, o, lse, ymix)


def _mix_bwd(dy, saved, wp, w_out, sp, tag):
    h, proj, pad_t, xbc, ypre, states, q, k, v, cf_row, o, lse, ymix = saved
    dymix = _matmul(dy, w_out, "nt", F32, f"mix_out_dx_{tag}")
    dw_out = _matmul(ymix, dy, "tn", BF16, f"mix_out_dw_{tag}")
    dxbc, ddt, dz, ssd_acc, dng = _ssd_bwd(dymix, 0, xbc, proj, pad_t, ypre, states, sp["dtb_row"],
                                           sp["alog_row"], sp["d_row"], sp["dtb_col"], sp["alog_col"], sp["ng"])
    dxbc_raw, dconv_w, dconv_b = _ssd_conv_bwd(dxbc, proj, sp["conv_w"], sp["conv_b"])
    do = _heads(dymix[:, SSD_W:SSD_W + FOX_W])
    dq, dk, dv, dcq, dck = _fox_bwd(q, k, v, cf_row, o, lse, do)
    dq = dq * FOX_SCALE
    dcf4 = dcq[:, :, 0] + dck[:, 0, :]
    dcf = jnp.zeros((h.shape[0], LANES), F32).at[:, PAD_F0:PAD_F0 + FOX_HEADS].set(jnp.transpose(dcf4))
    dpad, dfb = _pad_block_bwd(dcf, ddt, proj, sp["fbias_row"])
    dscb, dscc, dscx, dsconv_w = _shortconv_bwd(dymix, SSD_W + FOX_W, proj, sp["sconv_w"])
    dproj = jnp.concatenate([dz, dxbc_raw.astype(BF16), _unheads(dq).astype(BF16), _unheads(dk).astype(BF16),
                             _unheads(dv).astype(BF16), dscb, dscc, dscx, dpad], axis=1)
    dh = _matmul(dproj, wp, "nt", F32, f"mix_in_dx_{tag}")
    dwp = _matmul(h, dproj, "tn", BF16, f"mix_in_dw_{tag}")
    small = dict(conv_w=dconv_w, conv_b=dconv_b[0], dt_bias=ssd_acc[2, 0:8], a_log=ssd_acc[0, 0:8],
                 d=ssd_acc[1, 0:8], norm_g=dng[0], f_bias=dfb[0, PAD_F0:PAD_F0 + FOX_HEADS], sconv_w=dsconv_w)
    return dh, _unperm_mix_w_in(dwp), dw_out, small


def _local_step(x, tgt, mod, weights_of, small_p, before_sub_backward=None, after_sub_backward=None):
    row = lambda v: v.reshape(1, -1)
    subs = [(l, j) for l in range(DEPTH) for j in range(N_SUB)]
    factor = (0.5, 1.0, 0.5)
    w_names = (("ffn1_w_in", "ffn1_w_out"), ("mix_w_in", "mix_w_out"), ("ffn2_w_in", "ffn2_w_out"))
    lg = [[row(small_p["ln_g"][l, j]) for j in range(N_SUB)] for l in range(DEPTH)]
    lb = [[row(small_p["ln_b"][l, j]) for j in range(N_SUB)] for l in range(DEPTH)]
    sps = []
    for l in range(DEPTH):
        sp = dict(
            conv_w=small_p["ssd_conv_w"][l], conv_b=row(small_p["ssd_conv_b"][l]),
            dtb_row=_pad_lanes(small_p["ssd_dt_bias"][l], PAD_DT0), alog_row=_pad_lanes(small_p["ssd_a_log"][l], 0),
            d_row=_pad_lanes(small_p["ssd_d"][l], 0), ng=row(small_p["ssd_norm_g"][l]),
            fbias_row=_pad_lanes(small_p["fox_f_bias"][l], PAD_F0), sconv_w=small_p["sconv_w"][l])
        sp["dtb_col"] = jnp.transpose(sp["dtb_row"])
        sp["alog_col"] = jnp.transpose(sp["alog_row"])
        sps.append(sp)
    tags = [f"l{l}{('f1', 'mx', 'f2')[j]}" for l, j in subs]

    x0, h = _ln_in_fwd(x, row(small_p["ln_in_g"]), row(small_p["ln_in_b"]), mod[0, 0])
    cur = x0
    xins, ys, inner, weights = [], [], [], []
    for idx, (l, j) in enumerate(subs):
        w_in, w_out = weights_of(l, j, h)
        if idx > 0:
            (w_in, w_out), (cur, h) = lax.optimization_barrier(((w_in, w_out), (cur, h)))
        w_out = w_out.reshape(-1, w_out.shape[-1])
        if j == 1:
            w_in = _perm_mix_w_in(jnp.concatenate([w_in[s] for s in range(4)], axis=1))
        weights.append((w_in, w_out))
        if j == 1:
            y, sv = _mix_fwd(h, w_in, w_out, sps[l], tags[idx])
        else:
            y, sv = _ffn_fwd(h, w_in, w_out, tags[idx])
        nxt = mod[subs[idx + 1]] if idx + 1 < len(subs) else None
        xins.append(cur)
        ys.append(y)
        inner.append(sv)
        if nxt is None:
            dcur, loss_acc = _res_ln_loss(cur, y, mod[l, j], lg[l][j], lb[l][j], factor[j], tgt)
        else:
            cur, h = _res_ln(cur, y, mod[l, j], lg[l][j], lb[l][j], factor[j], f"res_ln_{tags[idx]}", nxt)

    last = len(subs) - 1
    l, j = subs[last]
    dres, dy, acc = _res_ln_bwd(dcur, xins[last], ys[last], mod[l, j], lg[l][j], factor[j], f"res_ln_bwd_{tags[last]}")
    ln_acc = {last: acc}
    shift_scale = {}
    big_grads = [dict() for _ in range(DEPTH)]
    small_g = [None] * DEPTH
    for idx in reversed(range(len(subs))):
        l, j = subs[idx]
        w_in, w_out = weights[idx]
        if before_sub_backward is not None:
            before_sub_backward(l, j, dy)
        if j == 1:
            dh, g_in, g_out, small_g[l] = _mix_bwd(dy, inner[idx], w_in, w_out, sps[l], tags[idx])
        else:
            dh, g_in, g_out = _ffn_bwd(dy, inner[idx], w_in, w_out, tags[idx])
        big_grads[l][w_names[j][0]], big_grads[l][w_names[j][1]] = g_in, g_out
        if after_sub_backward is not None:
            after_sub_backward(l, j, big_grads[l], dh)
        if idx > 0:
            pl_, pj = subs[idx - 1]
            dres, dy, acc5 = _mod_res_bwd(dres, dh, mod[l, j], xins[idx - 1], ys[idx - 1], mod[pl_, pj], lg[pl_][pj],
                                          lb[pl_][pj], factor[pj], f"mod_res_bwd_{tags[idx]}")
            shift_scale[idx], ln_acc[idx - 1] = acc5[0:2], acc5[2:5]
        else:
            dx0, shift_scale[0] = _modulate_bwd(dres, dh, x0, mod[0, 0], "mod_bwd_first")
    dx, acc_in = _ln_in_bwd(dx0, x, row(small_p["ln_in_g"]))
    dmod = []
    for l in range(DEPTH):
        ids = [N_SUB * l + j for j in range(N_SUB)]
        small_g[l]["ln_g"] = jnp.stack([ln_acc[i][0] for i in ids])
        small_g[l]["ln_b"] = jnp.stack([ln_acc[i][1] for i in ids])
        dmod.append(jnp.stack([jnp.concatenate([shift_scale[i], ln_acc[i][2:3]], axis=0) for i in ids]))
    return loss_acc[0, 0], dx, big_grads, small_g, jnp.stack(dmod), acc_in


SMALL_ORDER = ("ssd_conv_w", "ssd_conv_b", "ssd_dt_bias", "ssd_a_log", "ssd_d", "ssd_norm_g", "fox_f_bias",
               "sconv_w", "ln_g", "ln_b")
SMALL_KEY = dict(ssd_conv_w="conv_w", ssd_conv_b="conv_b", ssd_dt_bias="dt_bias", ssd_a_log="a_log", ssd_d="d",
                 ssd_norm_g="norm_g", fox_f_bias="f_bias", sconv_w="sconv_w", ln_g="ln_g", ln_b="ln_b")
COL_SHARDED_SMALL = ("ssd_conv_w", "sconv_w", "ln_g", "ln_b")


def _pad_to(v, n):
    return jnp.concatenate([v, jnp.zeros((n - v.shape[0],), v.dtype)])


def kernel(x, c, ln_in_g, ln_in_b, ada_w, ada_b, ffn1_w_in, ffn1_w_out, mix_w_in, mix_w_out, ssd_conv_w, ssd_conv_b, ssd_dt_bias, ssd_a_log, ssd_d, ssd_norm_g, fox_f_bias, sconv_w, ffn2_w_in, ffn2_w_out, ln_g, ln_b, loss_target, m_ln_in_g, m_ln_in_b, m_ada_w, m_ada_b, m_ffn1_w_in, m_ffn1_w_out, m_mix_w_in, m_mix_w_out, m_ssd_conv_w, m_ssd_conv_b, m_ssd_dt_bias, m_ssd_a_log, m_ssd_d, m_ssd_norm_g, m_fox_f_bias, m_sconv_w, m_ffn2_w_in, m_ffn2_w_out, m_ln_g, m_ln_b, v_ln_in_g, v_ln_in_b, v_ada_w, v_ada_b, v_ffn1_w_in, v_ffn1_w_out, v_mix_w_in, v_mix_w_out, v_ssd_conv_w, v_ssd_conv_b, v_ssd_dt_bias, v_ssd_a_log, v_ssd_d, v_ssd_norm_g, v_fox_f_bias, v_sconv_w, v_ffn2_w_in, v_ffn2_w_out, v_ln_g, v_ln_b):
    names = ("ln_in_g", "ln_in_b", "ada_w", "ada_b", "ffn1_w_in", "ffn1_w_out", "mix_w_in", "mix_w_out",
             "ssd_conv_w", "ssd_conv_b", "ssd_dt_bias", "ssd_a_log", "ssd_d", "ssd_norm_g", "fox_f_bias", "sconv_w",
             "ffn2_w_in", "ffn2_w_out", "ln_g", "ln_b")
    w_loc = dict(zip(names, (ln_in_g, ln_in_b, ada_w, ada_b, ffn1_w_in, ffn1_w_out, mix_w_in, mix_w_out, ssd_conv_w,
                             ssd_conv_b, ssd_dt_bias, ssd_a_log, ssd_d, ssd_norm_g, fox_f_bias, sconv_w, ffn2_w_in,
                             ffn2_w_out, ln_g, ln_b)))
    m_loc = dict(zip(names, (m_ln_in_g, m_ln_in_b, m_ada_w, m_ada_b, m_ffn1_w_in, m_ffn1_w_out, m_mix_w_in,
                             m_mix_w_out, m_ssd_conv_w, m_ssd_conv_b, m_ssd_dt_bias, m_ssd_a_log, m_ssd_d,
                             m_ssd_norm_g, m_fox_f_bias, m_sconv_w, m_ffn2_w_in, m_ffn2_w_out, m_ln_g, m_ln_b)))
    v_loc = dict(zip(names, (v_ln_in_g, v_ln_in_b, v_ada_w, v_ada_b, v_ffn1_w_in, v_ffn1_w_out, v_mix_w_in,
                             v_mix_w_out, v_ssd_conv_w, v_ssd_conv_b, v_ssd_dt_bias, v_ssd_a_log, v_ssd_d,
                             v_ssd_norm_g, v_fox_f_bias, v_sconv_w, v_ffn2_w_in, v_ffn2_w_out, v_ln_g, v_ln_b)))

    xi, yi, ci = lax.axis_index("x"), lax.axis_index("y"), lax.axis_index("c")
    me = 2 * xi + yi
    dev = 2 * me + ci

    def gather8(v, tag):
        v2 = v.reshape(1, -1)
        got = _bcast_chips([v2], f"gather_chips_{tag}")
        same_c = jnp.concatenate(_by_chip(me, v2, [g[0] for g in got]), axis=0)
        other_c = _swap_sibling([same_c], f"gather_sibling_{tag}")[0]
        pair = lax.switch(ci, [lambda a, b: jnp.stack([a, b], axis=1), lambda a, b: jnp.stack([b, a], axis=1)],
                          same_c, other_c)
        return pair.reshape(8, -1)

    def chip_concat(own, got3, axis):
        return jnp.concatenate(_by_chip(me, own, got3), axis=axis)

    small_cols = [w_loc[n].reshape(-1, w_loc[n].shape[-1]) for n in COL_SHARDED_SMALL]
    got = _bcast_chips(small_cols, "gather_small_params")
    small_p = {n: w_loc[n] for n in ("ln_in_g", "ln_in_b", "ssd_conv_b", "ssd_dt_bias", "ssd_a_log", "ssd_d",
                                     "ssd_norm_g", "fox_f_bias")}
    for i, n in enumerate(COL_SHARDED_SMALL):
        full = chip_concat(small_cols[i], [g[i] for g in got], 1)
        small_p[n] = full.reshape(w_loc[n].shape[:-1] + (full.shape[-1],))

    assert DEPTH == 2
    first, rest = BIG[:2], BIG[2:]
    shard_of = {n: w_loc[n].astype(BF16) for n in BIG}

    def gather(names, layer, after, tag, sequencer_id):
        out = _gather_weights(names, [shard_of[n] for n in names], layer, after, f"gather_weights_{tag}", sequencer_id)
        return dict(zip(names, out))

    gathered = {0: gather(first, 0, [], "l0a", 12)}

    def weights_of(l, j, marker):
        if (l, j) == (0, 0):
            gathered[0].update(gather(rest, 0, [marker], "l0b", 1))
        elif (l, j) == (0, 1):
            gathered[1] = gather(BIG, 1, [marker], "l1", 5)
        pair = BIG[2 * j:2 * j + 2]
        return gathered[l][pair[0]], gathered[l][pair[1]]

    c_all = gather8(c[0], "c")
    c_act = _silu_bf16(c_all)
    ada_w_b = ada_w.astype(BF16)
    mod_loc = [_matmul(c_act, ada_w_b[l], "nn", F32, f"ada_fwd_l{l}") for l in range(DEPTH)]
    mod_loc = jnp.stack(mod_loc)
    got = _bcast_chips([mod_loc], "gather_mod")
    mod_all = chip_concat(mod_loc, [g[0] for g in got], 2)
    mod = lax.dynamic_index_in_dim(mod_all, dev, 1, keepdims=False) + ada_b
    mod = mod.reshape(DEPTH, N_SUB, 3, D)

    ci_arr = jnp.reshape(ci, (1,)).astype(jnp.int32)
    me_arr = jnp.reshape(me, (1,)).astype(jnp.int32)

    def wire_form(n, g):
        if n in BY_COLUMNS:
            return g
        if n == "mix_w_in":
            return jnp.transpose(g.reshape(g.shape[0], 4, g.shape[1] // 4), (1, 0, 2))
        return g.reshape(4, g.shape[0] // 4, g.shape[1])

    units, reduced = {}, {}

    def start_unit(tag, l, names, g, ids):
        wire = [wire_form(n, g[n]) for n in names]
        units[tag] = dict(l=l, names=names, wire=wire, ids=ids,
                          got=_grad_swap_halves(names, wire, f"grad_swap_halves_{tag}", ids[0]))

    def add_and_scatter(tag, after):
        st = units[tag]
        st["part"] = [_grad_add_halves(st["wire"][i], st["got"][i], ci_arr, after, f"grad_add_halves_{tag}_{n}")
                      for i, n in enumerate(st["names"])]
        st["from_chips"] = _grad_scatter_halves(st["names"], st["part"], f"grad_scatter_halves_{tag}", st["ids"][1])

    def finish_unit(tag, after):
        st = units[tag]
        halves = [_grad_add_slots(st["part"][i], [g[i] for g in st["from_chips"]], n, me_arr, after,
                                  f"grad_add_slots_{tag}_{n}") for i, n in enumerate(st["names"])]
        other = _grad_swap_reduced(halves, f"grad_swap_reduced_{tag}", st["ids"][2])
        for i, n in enumerate(st["names"]):
            reduced[(st["l"], n)] = (halves[i], other[i])

    second, third = BIG[2:4], BIG[4:]

    def after_sub_backward(l, j, g, marker):
        if (l, j) == (1, 0):
            start_unit("l1", 1, BIG, g, (2, 3, 4))
        elif (l, j) == (0, 2):
            start_unit("l0f2", 0, third, g, (6, 7, 8))
        elif (l, j) == (0, 1):
            start_unit("l0mx", 0, second, g, (9, 10, 11))

    def before_sub_backward(l, j, marker):
        if (l, j) == (0, 2):
            add_and_scatter("l1", marker)
        elif (l, j) == (0, 1):
            add_and_scatter("l0f2", marker)
        elif (l, j) == (0, 0):
            add_and_scatter("l0mx", marker)

    loss_part, dx, big_g, small_g, dmod, acc_in = _local_step(
        x[0], loss_target[0], mod, weights_of, small_p, before_sub_backward, after_sub_backward)
    loss = lax.psum(loss_part, ("x", "y", "c"))

    pieces = [dmod.reshape(-1), acc_in[0], acc_in[1]]
    for n in SMALL_ORDER:
        pieces.append(jnp.stack([small_g[l][SMALL_KEY[n]] for l in range(DEPTH)]).reshape(-1))
    sizes = [p.shape[0] for p in pieces]
    total = sum(sizes)
    padded = -(-total // 1024) * 1024
    vec = _pad_to(jnp.concatenate(pieces), padded)
    all_rows = gather8(vec, "small_grads")
    summed = _sum_rows(all_rows, "sum_small_grads")[0]
    offs = [0]
    for s in sizes:
        offs.append(offs[-1] + s)
    n_mod = sizes[0]
    grads = {"ada_b": summed[0:n_mod].reshape(DEPTH, 3 * N_SUB * D),
             "ln_in_g": summed[offs[1]:offs[2]], "ln_in_b": summed[offs[2]:offs[3]]}
    for i, n in enumerate(SMALL_ORDER):
        full = summed[offs[3 + i]:offs[4 + i]].reshape(small_p[n].shape)
        if n in COL_SHARDED_SMALL:
            wcols = w_loc[n].shape[-1]
            full = lax.dynamic_slice_in_dim(full, me * wcols, wcols, axis=full.ndim - 1)
        grads[n] = full

    dmod_all = all_rows[:, 0:n_mod].reshape(8, DEPTH, 3 * N_SUB * D)
    ncol = ada_w.shape[-1]
    dmod_cols = lax.dynamic_slice_in_dim(dmod_all, me * ncol, ncol, axis=2).astype(BF16)
    grads["ada_w"] = jnp.stack([_matmul(c_act, dmod_cols[:, l], "tn", F32, f"ada_bwd_l{l}") for l in range(DEPTH)])

    finish_unit("l1", dx)
    finish_unit("l0f2", dx)
    finish_unit("l0mx", dx)
    start_unit("l0f1", 0, first, big_g[0], (None, None, None))
    add_and_scatter("l0f1", dx)
    finish_unit("l0f1", dx)

    delta, new_m, new_v = {}, {}, {}
    delta["ada_w"], new_m["ada_w"], new_v["ada_w"] = _adamw(w_loc["ada_w"], grads["ada_w"], m_loc["ada_w"],
                                                           v_loc["ada_w"], "adamw_ada_w")
    for i, n in enumerate(BIG):
        grads[n], delta[n], new_m[n], new_v[n] = _adamw_layers(
            w_loc[n], [reduced[(l, n)][0] for l in range(DEPTH)], [reduced[(l, n)][1] for l in range(DEPTH)],
            m_loc[n], v_loc[n], ci_arr, f"adamw_{n}")
    small_names = [n for n in names if n not in ("ada_w",) + BIG]
    flat = lambda d: jnp.concatenate([d[n].reshape(-1) for n in small_names])
    n_small = sum(w_loc[n].size for n in small_names)
    n_pad = -(-n_small // 1024) * 1024
    packed = [_pad_to(flat(d), n_pad).reshape(-1, LANES) for d in (w_loc, grads, m_loc, v_loc)]
    d_s, m_s, v_s = _adamw(*packed, "adamw_small")
    off = 0
    for n in small_names:
        sz = w_loc[n].size
        delta[n] = d_s.reshape(-1)[off:off + sz].reshape(w_loc[n].shape)
        new_m[n] = m_s.reshape(-1)[off:off + sz].reshape(w_loc[n].shape)
        new_v[n] = v_s.reshape(-1)[off:off + sz].reshape(w_loc[n].shape)
        off += sz

    return (loss, dx[None], *[grads[n] for n in names], *[delta[n] for n in names],
            *[new_m[n] for n in names], *[new_v[n] for n in names])
```
